```python
import math
import jax, jax.numpy as jnp
from jax import lax
import numpy as np

D_MODEL = 1024
BATCH = 8
SEQ = 8192
DEPTH = 2

N_MIXERS = 2
N_LRU_LAYERS = (DEPTH + 1) // 2
N_FOX_LAYERS = DEPTH // 2
EPS = 1e-6
LRU_WIDTH = 1536
LRU_BLOCKS = 12
LRU_BLOCK_W = LRU_WIDTH // LRU_BLOCKS
CONV_WIDTH = 4
LRU_C = 8.0
FOX_HEADS = 16
FOX_HEAD_DIM = 64
FOX_WIDTH = FOX_HEADS * FOX_HEAD_DIM
Q_BLOCK = 128
NEG_INF = -1e30

kernel_name = "hybrid_rglru_fox_interleaved"


def rms_norm(x, g):
    xf = x.astype(jnp.float32)
    y = xf * lax.rsqrt(jnp.mean(xf * xf, axis=-1, keepdims=True) + EPS)
    return (y * g.astype(jnp.float32)).astype(x.dtype)


def causal_depthwise_conv(x, w, b):
    c = x.shape[-1]
    y = lax.conv_general_dilated(
        x, w[:, None, :].astype(x.dtype), window_strides=(1,),
        padding=[(CONV_WIDTH - 1, 0)],
        dimension_numbers=("NWC", "WIO", "NWC"),
        feature_group_count=c)
    return y + b.astype(x.dtype)


def block_diag_linear(x, w, b):
    bsz, s, _ = x.shape
    xb = x.reshape(bsz, s, LRU_BLOCKS, LRU_BLOCK_W)
    y = jnp.einsum("bsnc,ncd->bsnd", xb, w.astype(x.dtype))
    return y.reshape(bsz, s, LRU_WIDTH) + b.astype(x.dtype)


def lru_mixer(h, w_in, conv_w, conv_b, wa, ba, wx, bx, a_param, w_out):
    u = h @ w_in.astype(h.dtype)
    xb, gate = u[..., :LRU_WIDTH], u[..., LRU_WIDTH:]
    xc = causal_depthwise_conv(xb, conv_w, conv_b)
    r = jax.nn.sigmoid(block_diag_linear(xc, wa, ba).astype(jnp.float32))
    i = jax.nn.sigmoid(block_diag_linear(xc, wx, bx).astype(jnp.float32))
    log_a = -LRU_C * r * jax.nn.softplus(-a_param.astype(jnp.float32))
    a = jnp.exp(log_a)
    mult = jnp.sqrt(-jnp.expm1(2.0 * log_a))
    bterm = mult * (i * xc.astype(jnp.float32))

    def combine(lhs, rhs):
        a1, b1 = lhs
        a2, b2 = rhs
        return a1 * a2, a2 * b1 + b2

    _, hs = lax.associative_scan(combine, (a, bterm), axis=1)
    y = hs.astype(h.dtype) * jax.nn.silu(gate)
    return y @ w_out.astype(h.dtype)


def fox_mixer(h, w_in, b_f, w_out):
    bsz, s, _ = h.shape
    u = h @ w_in.astype(h.dtype)
    q = u[..., 0 * FOX_WIDTH:1 * FOX_WIDTH].reshape(bsz, s, FOX_HEADS, FOX_HEAD_DIM)
    k = u[..., 1 * FOX_WIDTH:2 * FOX_WIDTH].reshape(bsz, s, FOX_HEADS, FOX_HEAD_DIM)
    v = u[..., 2 * FOX_WIDTH:3 * FOX_WIDTH].reshape(bsz, s, FOX_HEADS, FOX_HEAD_DIM)
    gate = u[..., 3 * FOX_WIDTH:4 * FOX_WIDTH]
    f_logit = u[..., 4 * FOX_WIDTH:].astype(jnp.float32) + b_f.astype(jnp.float32)
    cum = jnp.cumsum(jax.nn.log_sigmoid(f_logit), axis=1)
    ck = jnp.transpose(cum, (0, 2, 1))
    scale = 1.0 / math.sqrt(FOX_HEAD_DIM)
    n_blocks = s // Q_BLOCK
    qb = jnp.transpose(q.reshape(bsz, n_blocks, Q_BLOCK, FOX_HEADS, FOX_HEAD_DIM), (1, 0, 2, 3, 4))
    cqb = jnp.transpose(cum.reshape(bsz, n_blocks, Q_BLOCK, FOX_HEADS), (1, 0, 3, 2))
    starts = jnp.arange(n_blocks, dtype=jnp.int32) * Q_BLOCK
    kpos = jnp.arange(s, dtype=jnp.int32)
    kf = k.astype(jnp.float32)
    vf = v.astype(jnp.float32)

    def one_block(args):
        q_blk, cq_blk, start = args
        qpos = start + jnp.arange(Q_BLOCK, dtype=jnp.int32)
        logits = jnp.einsum("bqhd,bkhd->bhqk", q_blk.astype(jnp.float32), kf) * scale
        logits = logits + (cq_blk[..., :, None] - ck[:, :, None, :])
        mask = kpos[None, :] <= qpos[:, None]
        logits = jnp.where(mask[None, None], logits, NEG_INF)
        p = jax.nn.softmax(logits, axis=-1)
        return jnp.einsum("bhqk,bkhd->bqhd", p, vf)

    o = lax.map(one_block, (qb, cqb, starts))
    o = jnp.transpose(o, (1, 0, 2, 3, 4)).reshape(bsz, s, FOX_WIDTH).astype(h.dtype)
    y = o * jax.nn.silu(gate)
    return y @ w_out.astype(h.dtype)


def _fwd_setup_inputs(seed: int = 0) -> dict:
    key = jax.random.key(seed)
    ks = jax.random.split(key, 16)
    f32 = jnp.float32
    nl, nf = N_LRU_LAYERS, N_FOX_LAYERS
    x = jax.random.normal(ks[0], (BATCH, SEQ, D_MODEL), f32)
    norm_g = 1.0 + 0.05 * jax.random.normal(ks[1], (DEPTH, D_MODEL), f32)
    final_g = 1.0 + 0.05 * jax.random.normal(ks[2], (D_MODEL,), f32)
    lru_w_in = jax.random.normal(ks[3], (nl, D_MODEL, 2 * LRU_WIDTH), f32) * D_MODEL ** -0.5
    lru_conv_w = jax.random.normal(ks[4], (nl, CONV_WIDTH, LRU_WIDTH), f32) * CONV_WIDTH ** -0.5
    lru_conv_b = 0.02 * jax.random.normal(ks[5], (nl, LRU_WIDTH), f32)
    lru_wa = jax.random.normal(ks[6], (nl, LRU_BLOCKS, LRU_BLOCK_W, LRU_BLOCK_W), f32) * LRU_BLOCK_W ** -0.5
    lru_ba = 0.02 * jax.random.normal(ks[7], (nl, LRU_WIDTH), f32)
    lru_wx = jax.random.normal(ks[8], (nl, LRU_BLOCKS, LRU_BLOCK_W, LRU_BLOCK_W), f32) * LRU_BLOCK_W ** -0.5
    lru_bx = 0.02 * jax.random.normal(ks[9], (nl, LRU_WIDTH), f32)
    a_c = jax.random.uniform(ks[10], (nl, LRU_WIDTH), f32, minval=0.9, maxval=0.999)
    a0 = a_c ** (1.0 / LRU_C)
    lru_a_param = jnp.log(a0) - jnp.log1p(-a0)
    lru_w_out = jax.random.normal(ks[11], (nl, LRU_WIDTH, D_MODEL), f32) * LRU_WIDTH ** -0.5
    fox_w_in = jax.random.normal(ks[12], (nf, D_MODEL, 4 * FOX_WIDTH + FOX_HEADS), f32) * D_MODEL ** -0.5
    fox_b_f = 3.0 + 0.5 * jax.random.normal(ks[13], (nf, FOX_HEADS), f32)
    fox_w_out = jax.random.normal(ks[14], (nf, FOX_WIDTH, D_MODEL), f32) * FOX_WIDTH ** -0.5
    return {"x": x, "norm_g": norm_g, "final_g": final_g,
            "lru_w_in": lru_w_in, "lru_conv_w": lru_conv_w, "lru_conv_b": lru_conv_b,
            "lru_wa": lru_wa, "lru_ba": lru_ba, "lru_wx": lru_wx, "lru_bx": lru_bx,
            "lru_a_param": lru_a_param, "lru_w_out": lru_w_out,
            "fox_w_in": fox_w_in, "fox_b_f": fox_b_f, "fox_w_out": fox_w_out}


def _fwd_reference(x, norm_g, final_g, lru_w_in, lru_conv_w, lru_conv_b, lru_wa, lru_ba,
              lru_wx, lru_bx, lru_a_param, lru_w_out, fox_w_in, fox_b_f, fox_w_out):
    for i in range(DEPTH):
        h = rms_norm(x, norm_g[i])
        j = i // N_MIXERS
        if i % N_MIXERS == 0:
            x = x + lru_mixer(h, lru_w_in[j], lru_conv_w[j], lru_conv_b[j], lru_wa[j], lru_ba[j],
                              lru_wx[j], lru_bx[j], lru_a_param[j], lru_w_out[j])
        else:
            x = x + fox_mixer(h, fox_w_in[j], fox_b_f[j], fox_w_out[j])
    return rms_norm(x, final_g)


import jax as _jax
import jax.numpy as _jnp

TWIN_FORMAT = 'train_step'
FWD_PARAMS = ['x', 'norm_g', 'final_g', 'lru_w_in', 'lru_conv_w', 'lru_conv_b', 'lru_wa', 'lru_ba', 'lru_wx', 'lru_bx', 'lru_a_param', 'lru_w_out', 'fox_w_in', 'fox_b_f', 'fox_w_out']
TWIN_WEIGHTS = ['norm_g', 'final_g', 'lru_w_in', 'lru_conv_w', 'lru_conv_b', 'lru_wa', 'lru_ba', 'lru_wx', 'lru_bx', 'lru_a_param', 'lru_w_out', 'fox_w_in', 'fox_b_f', 'fox_w_out']
TWIN_DIFF_INPUT = 'x'
TWIN_INPUTS = ['x', 'norm_g', 'final_g', 'lru_w_in', 'lru_conv_w', 'lru_conv_b', 'lru_wa', 'lru_ba', 'lru_wx', 'lru_bx', 'lru_a_param', 'lru_w_out', 'fox_w_in', 'fox_b_f', 'fox_w_out', 'loss_target', 'm_norm_g', 'm_final_g', 'm_lru_w_in', 'm_lru_conv_w', 'm_lru_conv_b', 'm_lru_wa', 'm_lru_ba', 'm_lru_wx', 'm_lru_bx', 'm_lru_a_param', 'm_lru_w_out', 'm_fox_w_in', 'm_fox_b_f', 'm_fox_w_out', 'v_norm_g', 'v_final_g', 'v_lru_w_in', 'v_lru_conv_w', 'v_lru_conv_b', 'v_lru_wa', 'v_lru_ba', 'v_lru_wx', 'v_lru_bx', 'v_lru_a_param', 'v_lru_w_out', 'v_fox_w_in', 'v_fox_b_f', 'v_fox_w_out']
TWIN_OUTPUTS = ['loss', 'grad_x', 'grad_norm_g', 'grad_final_g', 'grad_lru_w_in', 'grad_lru_conv_w', 'grad_lru_conv_b', 'grad_lru_wa', 'grad_lru_ba', 'grad_lru_wx', 'grad_lru_bx', 'grad_lru_a_param', 'grad_lru_w_out', 'grad_fox_w_in', 'grad_fox_b_f', 'grad_fox_w_out', 'delta_norm_g', 'delta_final_g', 'delta_lru_w_in', 'delta_lru_conv_w', 'delta_lru_conv_b', 'delta_lru_wa', 'delta_lru_ba', 'delta_lru_wx', 'delta_lru_bx', 'delta_lru_a_param', 'delta_lru_w_out', 'delta_fox_w_in', 'delta_fox_b_f', 'delta_fox_w_out', 'new_m_norm_g', 'new_m_final_g', 'new_m_lru_w_in', 'new_m_lru_conv_w', 'new_m_lru_conv_b', 'new_m_lru_wa', 'new_m_lru_ba', 'new_m_lru_wx', 'new_m_lru_bx', 'new_m_lru_a_param', 'new_m_lru_w_out', 'new_m_fox_w_in', 'new_m_fox_b_f', 'new_m_fox_w_out', 'new_v_norm_g', 'new_v_final_g', 'new_v_lru_w_in', 'new_v_lru_conv_w', 'new_v_lru_conv_b', 'new_v_lru_wa', 'new_v_lru_ba', 'new_v_lru_wx', 'new_v_lru_bx', 'new_v_lru_a_param', 'new_v_lru_w_out', 'new_v_fox_w_in', 'new_v_fox_b_f', 'new_v_fox_w_out']
TWIN_LEAF_KINDS = {'loss': 'loss', 'grad_x': 'grad_x', 'grad_norm_g': 'grad_w', 'grad_final_g': 'grad_w', 'grad_lru_w_in': 'grad_w', 'grad_lru_conv_w': 'grad_w', 'grad_lru_conv_b': 'grad_w', 'grad_lru_wa': 'grad_w', 'grad_lru_ba': 'grad_w', 'grad_lru_wx': 'grad_w', 'grad_lru_bx': 'grad_w', 'grad_lru_a_param': 'grad_w', 'grad_lru_w_out': 'grad_w', 'grad_fox_w_in': 'grad_w', 'grad_fox_b_f': 'grad_w', 'grad_fox_w_out': 'grad_w', 'delta_norm_g': 'delta_w', 'delta_final_g': 'delta_w', 'delta_lru_w_in': 'delta_w', 'delta_lru_conv_w': 'delta_w', 'delta_lru_conv_b': 'delta_w', 'delta_lru_wa': 'delta_w', 'delta_lru_ba': 'delta_w', 'delta_lru_wx': 'delta_w', 'delta_lru_bx': 'delta_w', 'delta_lru_a_param': 'delta_w', 'delta_lru_w_out': 'delta_w', 'delta_fox_w_in': 'delta_w', 'delta_fox_b_f': 'delta_w', 'delta_fox_w_out': 'delta_w', 'new_m_norm_g': 'new_m', 'new_m_final_g': 'new_m', 'new_m_lru_w_in': 'new_m', 'new_m_lru_conv_w': 'new_m', 'new_m_lru_conv_b': 'new_m', 'new_m_lru_wa': 'new_m', 'new_m_lru_ba': 'new_m', 'new_m_lru_wx': 'new_m', 'new_m_lru_bx': 'new_m', 'new_m_lru_a_param': 'new_m', 'new_m_lru_w_out': 'new_m', 'new_m_fox_w_in': 'new_m', 'new_m_fox_b_f': 'new_m', 'new_m_fox_w_out': 'new_m', 'new_v_norm_g': 'new_v', 'new_v_final_g': 'new_v', 'new_v_lru_w_in': 'new_v', 'new_v_lru_conv_w': 'new_v', 'new_v_lru_conv_b': 'new_v', 'new_v_lru_wa': 'new_v', 'new_v_lru_ba': 'new_v', 'new_v_lru_wx': 'new_v', 'new_v_lru_bx': 'new_v', 'new_v_lru_a_param': 'new_v', 'new_v_lru_w_out': 'new_v', 'new_v_fox_w_in': 'new_v', 'new_v_fox_b_f': 'new_v', 'new_v_fox_w_out': 'new_v'}


def _forward(args):
    return _fwd_reference(*[args[k] for k in FWD_PARAMS])


def _output_shape():
    def fwd():
        inp = _fwd_setup_inputs(0)
        return _fwd_reference(*[inp[k] for k in FWD_PARAMS])
    out = _jax.eval_shape(fwd)
    return out.shape, out.dtype

N_MICROBATCH = 1
ADAM_LR = 0.001
ADAM_B1 = 0.9
ADAM_B2 = 0.999
ADAM_EPS = 1e-08
ADAM_WD = 0.01
ADAM_STEP = 10
PER_EXAMPLE_BATCH_AXIS = {'x': 0, 'loss_target': 0}
SHARED_INPUTS = []
_WEIGHT_DTYPES = {'norm_g': _jnp.float32, 'final_g': _jnp.float32, 'lru_w_in': _jnp.float32, 'lru_conv_w': _jnp.float32, 'lru_conv_b': _jnp.float32, 'lru_wa': _jnp.float32, 'lru_ba': _jnp.float32, 'lru_wx': _jnp.float32, 'lru_bx': _jnp.float32, 'lru_a_param': _jnp.float32, 'lru_w_out': _jnp.float32, 'fox_w_in': _jnp.float32, 'fox_b_f': _jnp.float32, 'fox_w_out': _jnp.float32}
MOMENT_SCALE = {'norm_g': 1.118936e-01, 'final_g': 6.401414e+01, 'lru_w_in': 8.107638e-02, 'lru_conv_w': 8.360149e-02, 'lru_conv_b': 1.113975e+00, 'lru_wa': 2.692599e-02, 'lru_ba': 2.198583e-02, 'lru_wx': 4.910580e-02, 'lru_bx': 3.165080e-02, 'lru_a_param': 4.304362e-02, 'lru_w_out': 1.076596e-01, 'fox_w_in': 4.047800e-02, 'fox_b_f': 2.320103e-01, 'fox_w_out': 4.444398e-02}


def _to_microbatches(a, axis):
    t = _jnp.moveaxis(a, axis, 0)
    t = t.reshape((N_MICROBATCH, t.shape[0] // N_MICROBATCH) + t.shape[1:])
    return _jnp.moveaxis(t, 1, axis + 1)


def setup_inputs(seed: int = 0) -> dict:
    inp = _fwd_setup_inputs(seed)
    key = _jax.random.fold_in(_jax.random.key(seed), 7919)
    shape, _ = _output_shape()
    out = dict(inp)
    out["loss_target"] = _jax.random.normal(_jax.random.fold_in(key, 0), shape, _jnp.float32)
    for i, name in enumerate(TWIN_WEIGHTS):
        w = inp[name].astype(_jnp.float32)
        if MOMENT_SCALE is None:
            s = _jnp.sqrt(_jnp.mean(_jnp.square(w)) + 1e-30)
        else:
            s = MOMENT_SCALE[name]
        km, kv = _jax.random.split(_jax.random.fold_in(key, i + 1))
        out[name] = w
        out["m_" + name] = s * _jax.random.normal(km, w.shape, _jnp.float32)
        out["v_" + name] = (s * s) * _jax.random.uniform(kv, w.shape, _jnp.float32, 0.5, 1.5)
    if N_MICROBATCH > 1:
        for name, axis in PER_EXAMPLE_BATCH_AXIS.items():
            out[name] = _to_microbatches(out[name], axis)
    return {'x': out['x'], 'norm_g': out['norm_g'], 'final_g': out['final_g'], 'lru_w_in': out['lru_w_in'], 'lru_conv_w': out['lru_conv_w'], 'lru_conv_b': out['lru_conv_b'], 'lru_wa': out['lru_wa'], 'lru_ba': out['lru_ba'], 'lru_wx': out['lru_wx'], 'lru_bx': out['lru_bx'], 'lru_a_param': out['lru_a_param'], 'lru_w_out': out['lru_w_out'], 'fox_w_in': out['fox_w_in'], 'fox_b_f': out['fox_b_f'], 'fox_w_out': out['fox_w_out'], 'loss_target': out['loss_target'], 'm_norm_g': out['m_norm_g'], 'm_final_g': out['m_final_g'], 'm_lru_w_in': out['m_lru_w_in'], 'm_lru_conv_w': out['m_lru_conv_w'], 'm_lru_conv_b': out['m_lru_conv_b'], 'm_lru_wa': out['m_lru_wa'], 'm_lru_ba': out['m_lru_ba'], 'm_lru_wx': out['m_lru_wx'], 'm_lru_bx': out['m_lru_bx'], 'm_lru_a_param': out['m_lru_a_param'], 'm_lru_w_out': out['m_lru_w_out'], 'm_fox_w_in': out['m_fox_w_in'], 'm_fox_b_f': out['m_fox_b_f'], 'm_fox_w_out': out['m_fox_w_out'], 'v_norm_g': out['v_norm_g'], 'v_final_g': out['v_final_g'], 'v_lru_w_in': out['v_lru_w_in'], 'v_lru_conv_w': out['v_lru_conv_w'], 'v_lru_conv_b': out['v_lru_conv_b'], 'v_lru_wa': out['v_lru_wa'], 'v_lru_ba': out['v_lru_ba'], 'v_lru_wx': out['v_lru_wx'], 'v_lru_bx': out['v_lru_bx'], 'v_lru_a_param': out['v_lru_a_param'], 'v_lru_w_out': out['v_lru_w_out'], 'v_fox_w_in': out['v_fox_w_in'], 'v_fox_b_f': out['v_fox_b_f'], 'v_fox_w_out': out['v_fox_w_out']}


def _loss(weights, diff, rest, loss_target):
    with _jax.named_scope("forward"):
        args = {**rest, TWIN_DIFF_INPUT: diff, **{k: w.astype(_WEIGHT_DTYPES[k]) for k, w in weights.items()}}
        y = _forward(args)
    with _jax.named_scope("loss_head"):
        err = _jnp.square(y.astype(_jnp.float32) - loss_target)
        return 0.5 * _jnp.sum(_jnp.mean(err, axis=-1)) if err.ndim else 0.5 * err


def _adamw(w, g, m, v):
    m = ADAM_B1 * m + (1.0 - ADAM_B1) * g
    v = ADAM_B2 * v + (1.0 - ADAM_B2) * _jnp.square(g)
    m_hat = m / (1.0 - ADAM_B1 ** ADAM_STEP)
    v_hat = v / (1.0 - ADAM_B2 ** ADAM_STEP)
    delta = -ADAM_LR * (m_hat / (_jnp.sqrt(v_hat) + ADAM_EPS) + ADAM_WD * w)
    return delta, m, v


def reference(x, norm_g, final_g, lru_w_in, lru_conv_w, lru_conv_b, lru_wa, lru_ba, lru_wx, lru_bx, lru_a_param, lru_w_out, fox_w_in, fox_b_f, fox_w_out, loss_target, m_norm_g, m_final_g, m_lru_w_in, m_lru_conv_w, m_lru_conv_b, m_lru_wa, m_lru_ba, m_lru_wx, m_lru_bx, m_lru_a_param, m_lru_w_out, m_fox_w_in, m_fox_b_f, m_fox_w_out, v_norm_g, v_final_g, v_lru_w_in, v_lru_conv_w, v_lru_conv_b, v_lru_wa, v_lru_ba, v_lru_wx, v_lru_bx, v_lru_a_param, v_lru_w_out, v_fox_w_in, v_fox_b_f, v_fox_w_out):
    given = dict(x=x, norm_g=norm_g, final_g=final_g, lru_w_in=lru_w_in, lru_conv_w=lru_conv_w, lru_conv_b=lru_conv_b, lru_wa=lru_wa, lru_ba=lru_ba, lru_wx=lru_wx, lru_bx=lru_bx, lru_a_param=lru_a_param, lru_w_out=lru_w_out, fox_w_in=fox_w_in, fox_b_f=fox_b_f, fox_w_out=fox_w_out, loss_target=loss_target, m_norm_g=m_norm_g, m_final_g=m_final_g, m_lru_w_in=m_lru_w_in, m_lru_conv_w=m_lru_conv_w, m_lru_conv_b=m_lru_conv_b, m_lru_wa=m_lru_wa, m_lru_ba=m_lru_ba, m_lru_wx=m_lru_wx, m_lru_bx=m_lru_bx, m_lru_a_param=m_lru_a_param, m_lru_w_out=m_lru_w_out, m_fox_w_in=m_fox_w_in, m_fox_b_f=m_fox_b_f, m_fox_w_out=m_fox_w_out, v_norm_g=v_norm_g, v_final_g=v_final_g, v_lru_w_in=v_lru_w_in, v_lru_conv_w=v_lru_conv_w, v_lru_conv_b=v_lru_conv_b, v_lru_wa=v_lru_wa, v_lru_ba=v_lru_ba, v_lru_wx=v_lru_wx, v_lru_bx=v_lru_bx, v_lru_a_param=v_lru_a_param, v_lru_w_out=v_lru_w_out, v_fox_w_in=v_fox_w_in, v_fox_b_f=v_fox_b_f, v_fox_w_out=v_fox_w_out)
    weights = {n: given[n] for n in TWIN_WEIGHTS}
    shared = {n: given[n] for n in SHARED_INPUTS}
    per_example = {n: given[n] for n in ['x']}
    grad_fn = _jax.value_and_grad(_loss, argnums=(0, 1))

    def one_microbatch(ex, loss_target):
        ex = dict(ex)
        diff = ex.pop(TWIN_DIFF_INPUT)
        return grad_fn(weights, diff, {**shared, **ex}, loss_target)

    if N_MICROBATCH == 1:
        loss, (grad_w, grad_x) = one_microbatch(per_example, given["loss_target"])
    else:
        def body(carry, xs):
            loss_sum, grad_sum = carry
            l_k, (gw_k, gx_k) = one_microbatch(xs[0], xs[1])
            with _jax.named_scope("update"):
                return (loss_sum + l_k, _jax.tree.map(_jnp.add, grad_sum, gw_k)), gx_k

        init = (_jnp.zeros((), _jnp.float32), _jax.tree.map(_jnp.zeros_like, weights))
        (loss, grad_w), grad_x = _jax.lax.scan(body, init, (per_example, given["loss_target"]))
    with _jax.named_scope("update"):
        delta_w, new_m, new_v = {}, {}, {}
        for n in TWIN_WEIGHTS:
            delta_w[n], new_m[n], new_v[n] = _adamw(weights[n], grad_w[n], given["m_" + n], given["v_" + n])
    return (loss, grad_x, *[grad_w[n] for n in TWIN_WEIGHTS], *[delta_w[n] for n in TWIN_WEIGHTS],
            *[new_m[n] for n in TWIN_WEIGHTS], *[new_v[n] for n in TWIN_WEIGHTS])
```

```python
import functools

import jax
import jax.numpy as jnp
from jax import lax
from jax.experimental import pallas as pl
from jax.experimental.pallas import tpu as pltpu

F32 = jnp.float32
BF16 = jnp.bfloat16

D_MODEL = 1024
LRU_WIDTH = 1536
LRU_BLOCKS = 12
LRU_BLOCK_W = 128
CONV_WIDTH = 4
LRU_C = 8.0
HEADS = 16
HEAD_DIM = 64
HEAD_PAD = 128
FOX_PAD = HEADS * HEAD_PAD
EPS = 1e-6
NEG_BIG = -1e30
N_DEV = 8

ADAM_LR = 0.001
ADAM_B1 = 0.9
ADAM_B2 = 0.999
ADAM_EPS = 1e-08
ADAM_WD = 0.01
ADAM_STEP = 10

LANE_RB = 64
LANE_CK = 67
LANE_ONE_V = 64

VMEM_LIMIT_BYTES = 56 * 1024 * 1024
LANES = 128
SUBLANES = 8

SHARD_SIZES = (1024 * 384, 4 * 192, 192 * 1024, 1024 * 514, 128 * 1024)
SHARD_TOTAL = sum(SHARD_SIZES)
PACK_ROWS = 10240
PACK_TOTAL = PACK_ROWS * LANES
SMALL_CHUNK = PACK_TOTAL - SHARD_TOTAL
SMALL_SIZES = (2 * 1024, 1024, 1536, 12 * 128 * 128, 1536, 12 * 128 * 128, 1536, 1536, 16)
SMALL_TOTAL = sum(SMALL_SIZES)
assert SMALL_TOTAL <= N_DEV * SMALL_CHUNK


def _params(n_grid_axes=1):
    return pltpu.CompilerParams(
        dimension_semantics=("arbitrary",) * n_grid_axes,
        vmem_limit_bytes=VMEM_LIMIT_BYTES)


def _const_spec(shape):
    nd = len(shape)
    return pl.BlockSpec(shape, lambda *_: (0,) * nd, pipeline_mode=pl.Buffered(1))


def _shift_down(x, k, fill):
    rows = lax.broadcasted_iota(jnp.int32, x.shape, 0)
    return jnp.where(rows >= k, pltpu.roll(x, k, 0), fill)


def _shift_up(x, k, fill):
    n = x.shape[0]
    rows = lax.broadcasted_iota(jnp.int32, x.shape, 0)
    return jnp.where(rows < n - k, pltpu.roll(x, n - k, 0), fill)


def _scan_rows(a, b, reverse=False):
    n = a.shape[0]
    shift = _shift_up if reverse else _shift_down
    k = 1
    while k < n:
        b = a * shift(b, k, 0.0) + b
        a = a * shift(a, k, 1.0)
        k *= 2
    return a, b


def _cumsum_rows(x, reverse=False):
    n = x.shape[0]
    shift = _shift_up if reverse else _shift_down
    k = 1
    while k < n:
        x = x + shift(x, k, 0.0)
        k *= 2
    return x


def _rstd(x):
    return lax.rsqrt(jnp.mean(x * x, axis=-1, keepdims=True) + EPS)


def _norm_bwd(x, g, dh):
    rstd = _rstd(x)
    xhat = x * rstd
    dg = jnp.sum(dh * xhat, axis=0, keepdims=True)
    dxh = dh * g
    dx = rstd * (dxh - xhat * jnp.mean(dxh * xhat, axis=-1, keepdims=True))
    return dx, dg


def _split3(x):
    hi = x.astype(BF16)
    r1 = x - hi.astype(F32)
    mid = r1.astype(BF16)
    lo = (r1 - mid.astype(F32)).astype(BF16)
    return hi, mid, lo


def _sigmoid(x):
    return jax.nn.sigmoid(x)


def _dot(a, b):
    return jnp.dot(a, b, preferred_element_type=F32)


def _dot_nt(a, b):
    return lax.dot_general(a, b, (((1,), (1,)), ((), ())), preferred_element_type=F32)


def _dot_tn(a, b):
    return lax.dot_general(a, b, (((0,), (0,)), ((), ())), preferred_element_type=F32)


def _conv_taps(xb, prev8):
    rows8 = lax.broadcasted_iota(jnp.int32, prev8.shape, 0)
    taps = [xb]
    for j in range(1, CONV_WIDTH):
        r = pltpu.roll(xb, j, 0)
        p = pltpu.roll(prev8, j, 0)
        head = jnp.where(rows8 < j, p, r[0:SUBLANES])
        taps.append(jnp.concatenate([head, r[SUBLANES:]], axis=0))
    return taps


def _lru_pre(taps, cw, cb, wa_ref, ba, wx_ref, bx, a_param):
    xc = cb + cw[3:4] * taps[0] + cw[2:3] * taps[1] + cw[1:2] * taps[2] + cw[0:1] * taps[3]
    xcb = xc.astype(BF16)
    ra, ia = [], []
    for n in range(LRU_BLOCKS):
        blk = xcb[:, n * LRU_BLOCK_W:(n + 1) * LRU_BLOCK_W]
        ra.append(_dot(blk, wa_ref[n]))
        ia.append(_dot(blk, wx_ref[n]))
    r = _sigmoid(jnp.concatenate(ra, axis=1) + ba)
    i = _sigmoid(jnp.concatenate(ia, axis=1) + bx)
    z = -a_param
    sp = jnp.maximum(z, 0.0) + jnp.log1p(jnp.exp(-jnp.abs(z)))
    log_a = (-LRU_C) * r * sp
    a = jnp.exp(log_a)
    one_minus_a2 = -jnp.tanh(log_a) * (a * a + 1.0)
    mult = jnp.sqrt(one_minus_a2)
    return xc, xcb, r, i, sp, a, mult


def _lru_in_fwd(x, g0, w_in, ts):
    s = x.shape[0]

    def body(x_ref, g_ref, w_ref, xb_ref, gate_ref, h_ref):
        xv = x_ref[...]
        h = (xv * _rstd(xv) * g_ref[...]).astype(BF16)
        u = _dot(h, w_ref[...])
        xb_ref[...] = u[:, :LRU_WIDTH]
        gate_ref[...] = u[:, LRU_WIDTH:]
        h_ref[...] = h

    return pl.pallas_call(
        body, name="lru_in_fwd", grid=(s // ts,),
        in_specs=[pl.BlockSpec((ts, D_MODEL), lambda i: (i, 0)),
                  _const_spec((1, D_MODEL)),
                  _const_spec((D_MODEL, 2 * LRU_WIDTH))],
        out_specs=[pl.BlockSpec((ts, LRU_WIDTH), lambda i: (i, 0)),
                   pl.BlockSpec((ts, LRU_WIDTH), lambda i: (i, 0)),
                   pl.BlockSpec((ts, D_MODEL), lambda i: (i, 0))],
        out_shape=[jax.ShapeDtypeStruct((s, LRU_WIDTH), F32),
                   jax.ShapeDtypeStruct((s, LRU_WIDTH), F32),
                   jax.ShapeDtypeStruct((s, D_MODEL), BF16)],
        compiler_params=_params(),
    )(x, g0, w_in)


def _lru_core_fwd(xb, gate, cw, cb, wa, ba, wx, bx, a_param, ts):
    s = xb.shape[0]

    def body(xb_ref, gate_ref, cw_ref, cb_ref, wa_ref, ba_ref, wx_ref, bx_ref, ap_ref,
             y_ref, hs_ref, prev_ref, hcar_ref):
        @pl.when(pl.program_id(0) == 0)
        def _():
            prev_ref[...] = jnp.zeros_like(prev_ref)
            hcar_ref[...] = jnp.zeros_like(hcar_ref)

        xbv = xb_ref[...]
        taps = _conv_taps(xbv, prev_ref[...])
        xc, _, _, i, _, a, mult = _lru_pre(taps, cw_ref[...], cb_ref[...], wa_ref, ba_ref[...],
                                           wx_ref, bx_ref[...], ap_ref[...])
        bterm = mult * (i * xc)
        cum_a, hloc = _scan_rows(a, bterm)
        hs = cum_a * hcar_ref[SUBLANES - 1:SUBLANES, :] + hloc
        gv = gate_ref[...]
        y_ref[...] = (hs * (gv * _sigmoid(gv))).astype(BF16)
        hs_ref[...] = hs
        prev_ref[...] = xbv[ts - SUBLANES:, :]
        hcar_ref[...] = hs[ts - SUBLANES:, :]

    vec = _const_spec((1, LRU_WIDTH))
    blk = _const_spec((LRU_BLOCKS, LRU_BLOCK_W, LRU_BLOCK_W))
    tile = pl.BlockSpec((ts, LRU_WIDTH), lambda i: (i, 0))
    return pl.pallas_call(
        body, name="lru_core_fwd", grid=(s // ts,),
        in_specs=[tile, tile, _const_spec((CONV_WIDTH, LRU_WIDTH)), vec, blk, vec, blk, vec, vec],
        out_specs=[tile, tile],
        out_shape=[jax.ShapeDtypeStruct((s, LRU_WIDTH), BF16),
                   jax.ShapeDtypeStruct((s, LRU_WIDTH), F32)],
        scratch_shapes=[pltpu.VMEM((SUBLANES, LRU_WIDTH), F32),
                        pltpu.VMEM((SUBLANES, LRU_WIDTH), F32)],
        compiler_params=_params(),
    )(xb, gate, cw, cb, wa, ba, wx, bx, a_param)


def _fox_pre_fwd(x, y, w_out, g1, wf, bf, ts):
    s = x.shape[0]

    def body(x_ref, y_ref, w_ref, g_ref, wf_ref, bf_ref, x1_ref, h1_ref, f_ref, cp_ref, ccar_ref):
        @pl.when(pl.program_id(0) == 0)
        def _():
            ccar_ref[...] = jnp.zeros_like(ccar_ref)

        x1 = x_ref[...] + _dot(y_ref[...], w_ref[...])
        h1 = (x1 * _rstd(x1) * g_ref[...]).astype(BF16)
        f = _dot(h1, wf_ref[...]) + bf_ref[...]
        logsig = jnp.minimum(f, 0.0) - jnp.log1p(jnp.exp(-jnp.abs(f)))
        cum = _cumsum_rows(logsig) + ccar_ref[SUBLANES - 1:SUBLANES, :]
        hi, mid, lo = _split3(cum)
        x1_ref[...] = x1
        h1_ref[...] = h1
        f_ref[...] = f
        cp_ref[...] = jnp.concatenate([hi, mid, lo], axis=1)
        ccar_ref[...] = cum[ts - SUBLANES:, :]

    return pl.pallas_call(
        body, name="fox_pre_fwd", grid=(s // ts,),
        in_specs=[pl.BlockSpec((ts, D_MODEL), lambda i: (i, 0)),
                  pl.BlockSpec((ts, LRU_WIDTH), lambda i: (i, 0)),
                  _const_spec((LRU_WIDTH, D_MODEL)),
                  _const_spec((1, D_MODEL)),
                  _const_spec((D_MODEL, LANES)),
                  _const_spec((1, LANES))],
        out_specs=[pl.BlockSpec((ts, D_MODEL), lambda i: (i, 0)),
                   pl.BlockSpec((ts, D_MODEL), lambda i: (i, 0)),
                   pl.BlockSpec((ts, LANES), lambda i: (i, 0)),
                   pl.BlockSpec((ts, 3 * LANES), lambda i: (i, 0))],
        out_shape=[jax.ShapeDtypeStruct((s, D_MODEL), F32),
                   jax.ShapeDtypeStruct((s, D_MODEL), BF16),
                   jax.ShapeDtypeStruct((s, LANES), F32),
                   jax.ShapeDtypeStruct((s, 3 * LANES), BF16)],
        scratch_shapes=[pltpu.VMEM((SUBLANES, LANES), F32)],
        compiler_params=_params(),
    )(x, y, w_out, g1, wf, bf)


def _fox_proj_fwd(h1, cparts, w, sel, bias, out_dtype, ts, name):
    s = h1.shape[0]
    ng = w.shape[0]
    use_sel = sel is not None

    def body(*refs):
        if use_sel:
            h_ref, cp_ref, w_ref, sel_ref, b_ref, o_ref = refs
            acc = _dot(h_ref[...], w_ref[...]) + _dot(cp_ref[...], sel_ref[...]) + b_ref[...]
        else:
            h_ref, w_ref, o_ref = refs
            acc = _dot(h_ref[...], w_ref[...])
        o_ref[...] = acc.astype(out_dtype)

    in_specs = [pl.BlockSpec((ts, D_MODEL), lambda j, i: (i, 0))]
    args = [h1]
    if use_sel:
        in_specs.append(pl.BlockSpec((ts, 3 * LANES), lambda j, i: (i, 0)))
        args.append(cparts)
    in_specs.append(pl.BlockSpec((None, D_MODEL, FOX_PAD), lambda j, i: (j, 0, 0)))
    args.append(w)
    if use_sel:
        in_specs.append(pl.BlockSpec((None, 3 * LANES, FOX_PAD), lambda j, i: (j, 0, 0)))
        in_specs.append(pl.BlockSpec((None, 1, FOX_PAD), lambda j, i: (j, 0, 0)))
        args += [sel, bias]
    return pl.pallas_call(
        body, name=name, grid=(ng, s // ts),
        in_specs=in_specs,
        out_specs=pl.BlockSpec((None, ts, FOX_PAD), lambda j, i: (j, i, 0)),
        out_shape=jax.ShapeDtypeStruct((ng, s, FOX_PAD), out_dtype),
        compiler_params=_params(2),
    )(*args)


def _attn_fwd(qkv, blk):
    s = qkv.shape[1]
    nblk = s // blk

    def body(q_ref, k_ref, v_ref, o_ref, qb_ref):
        row = lax.broadcasted_iota(jnp.int32, (blk, blk), 0)
        col = lax.broadcasted_iota(jnp.int32, (blk, blk), 1)
        lane = lax.broadcasted_iota(jnp.int32, (blk, HEAD_PAD), 1)

        def q_block(qi, _):
            q0 = pl.multiple_of(qi * blk, blk)
            q = q_ref[pl.ds(q0, blk), :]

            def step(k0, carry, masked):
                m, acc = carry
                k = k_ref[pl.ds(k0, blk), :]
                v = v_ref[pl.ds(k0, blk), :]
                sc = _dot_nt(q, k)
                if masked:
                    sc = jnp.where(col <= row, sc, NEG_BIG)
                m_new = jnp.maximum(m, jnp.max(sc, axis=-1, keepdims=True))
                p = jnp.exp(sc - m_new)
                acc = jnp.exp(m - m_new) * acc + _dot(p.astype(BF16), v)
                return m_new, acc

            init = (jnp.full((blk, 1), NEG_BIG, F32), jnp.zeros((blk, HEAD_PAD), F32))
            carry = lax.fori_loop(
                0, qi, lambda kj, c: step(pl.multiple_of(kj * blk, blk), c, False), init)
            m, acc = step(q0, carry, True)
            l = acc[:, LANE_ONE_V:LANE_ONE_V + 1]
            o_ref[pl.ds(q0, blk), :] = acc / l
            qf = q.astype(F32)
            cq = (qf[:, LANE_RB:LANE_RB + 1] + qf[:, LANE_RB + 1:LANE_RB + 2]
                  + qf[:, LANE_RB + 2:LANE_RB + 3])
            hi, mid, lo = _split3(cq - (m + jnp.log(l)))
            qb = jnp.where(lane == LANE_RB, hi, jnp.where(
                lane == LANE_RB + 1, mid, jnp.where(lane == LANE_RB + 2, lo, q)))
            qb_ref[pl.ds(q0, blk), :] = qb
            return 0

        lax.fori_loop(0, nblk, q_block, 0)

    def head_spec(j):
        return pl.BlockSpec((None, s, HEAD_PAD), lambda h: (j, 0, h))

    out_spec = pl.BlockSpec((s, HEAD_PAD), lambda h: (0, h))
    return pl.pallas_call(
        body, name="attn_fwd", grid=(HEADS,),
        in_specs=[head_spec(0), head_spec(1), head_spec(2)],
        out_specs=[out_spec, out_spec],
        out_shape=[jax.ShapeDtypeStruct((s, FOX_PAD), F32),
                   jax.ShapeDtypeStruct((s, FOX_PAD), BF16)],
        compiler_params=_params(),
    )(qkv, qkv, qkv)


def _fox_out_loss(o, gate, w_out, x1, target, gf, ts):
    s = x1.shape[0]

    def body(o_ref, gt_ref, w_ref, x1_ref, t_ref, g_ref, dx2_ref, y2_ref, loss_ref, gfin_ref):
        @pl.when(pl.program_id(0) == 0)
        def _():
            loss_ref[...] = jnp.zeros_like(loss_ref)
            gfin_ref[...] = jnp.zeros_like(gfin_ref)

        gv = gt_ref[...]
        y2 = (o_ref[...] * (gv * _sigmoid(gv))).astype(BF16)
        x2 = x1_ref[...] + _dot(y2, w_ref[...])
        rstd = _rstd(x2)
        xhat = x2 * rstd
        g = g_ref[...]
        diff = xhat * g - t_ref[...]
        loss_ref[...] += 0.5 * jnp.sum(jnp.mean(diff * diff, axis=-1, keepdims=True))
        dy = diff * (1.0 / D_MODEL)
        gfin_ref[...] += jnp.sum(dy * xhat, axis=0, keepdims=True)
        dxh = dy * g
        dx2_ref[...] = rstd * (dxh - xhat * jnp.mean(dxh * xhat, axis=-1, keepdims=True))
        y2_ref[...] = y2

    return pl.pallas_call(
        body, name="fox_out_loss", grid=(s // ts,),
        in_specs=[pl.BlockSpec((ts, FOX_PAD), lambda i: (i, 0)),
                  pl.BlockSpec((ts, FOX_PAD), lambda i: (i, 0)),
                  _const_spec((FOX_PAD, D_MODEL)),
                  pl.BlockSpec((ts, D_MODEL), lambda i: (i, 0)),
                  pl.BlockSpec((ts, D_MODEL), lambda i: (i, 0)),
                  _const_spec((1, D_MODEL))],
        out_specs=[pl.BlockSpec((ts, D_MODEL), lambda i: (i, 0)),
                   pl.BlockSpec((ts, FOX_PAD), lambda i: (i, 0)),
                   pl.BlockSpec((SUBLANES, LANES), lambda i: (0, 0)),
                   pl.BlockSpec((1, D_MODEL), lambda i: (0, 0))],
        out_shape=[jax.ShapeDtypeStruct((s, D_MODEL), F32),
                   jax.ShapeDtypeStruct((s, FOX_PAD), BF16),
                   jax.ShapeDtypeStruct((SUBLANES, LANES), F32),
                   jax.ShapeDtypeStruct((1, D_MODEL), F32)],
        compiler_params=_params(),
    )(o, gate, w_out, x1, target, gf)


def _fox_out_bwd(dx2, w_out_t, o, gate, ts):
    s = dx2.shape[0]

    def body(dx_ref, w_ref, o_ref, gt_ref, do_ref, dg_ref):
        lane = lax.broadcasted_iota(jnp.int32, (ts, HEAD_PAD), 1)
        dy2 = _dot(dx_ref[...].astype(BF16), w_ref[...])
        gv = gt_ref[...]
        sg = _sigmoid(gv)
        ov = o_ref[...]
        dov = dy2 * (gv * sg)
        dg_ref[...] = (dy2 * ov * (sg * (1.0 + gv * (1.0 - sg)))).astype(BF16)
        prod = dov * ov
        for h in range(HEADS):
            sl = slice(h * HEAD_PAD, (h + 1) * HEAD_PAD)
            delta = jnp.sum(prod[:, sl], axis=-1, keepdims=True)
            hi = delta.astype(BF16)
            lo = (delta - hi.astype(F32)).astype(BF16)
            do_h = dov[:, sl].astype(BF16)
            do_ref[:, sl] = jnp.where(lane == LANE_ONE_V, -hi,
                                      jnp.where(lane == LANE_ONE_V + 1, -lo, do_h))

    tile = pl.BlockSpec((ts, FOX_PAD), lambda i: (i, 0))
    return pl.pallas_call(
        body, name="fox_out_bwd", grid=(s // ts,),
        in_specs=[pl.BlockSpec((ts, D_MODEL), lambda i: (i, 0)),
                  _const_spec((D_MODEL, FOX_PAD)), tile, tile],
        out_specs=[tile, tile],
        out_shape=[jax.ShapeDtypeStruct((s, FOX_PAD), BF16),
                   jax.ShapeDtypeStruct((s, FOX_PAD), BF16)],
        compiler_params=_params(),
    )(dx2, w_out_t, o, gate)


def _attn_bwd(qb, qkv, do, blk):
    s = qb.shape[0]
    nblk = s // blk

    def body(q_ref, k_ref, v_ref, do_ref, dq_ref, dk_ref, dv_ref, dcum_ref, dq_acc):
        head = pl.program_id(0)
        row = lax.broadcasted_iota(jnp.int32, (blk, blk), 0)
        col = lax.broadcasted_iota(jnp.int32, (blk, blk), 1)
        mine = lax.broadcasted_iota(jnp.int32, (blk, LANES), 1) == head
        dq_acc[...] = jnp.zeros_like(dq_acc)

        @pl.when(head == 0)
        def _():
            dcum_ref[...] = jnp.zeros_like(dcum_ref)

        def k_block(kj, _):
            k0 = pl.multiple_of(kj * blk, blk)
            k = k_ref[pl.ds(k0, blk), :]
            v = v_ref[pl.ds(k0, blk), :]

            def step(q0, carry, masked):
                dk, dv = carry
                q = q_ref[pl.ds(q0, blk), :]
                dov = do_ref[pl.ds(q0, blk), :]
                p = jnp.exp(_dot_nt(q, k))
                if masked:
                    p = jnp.where(col <= row, p, 0.0)
                ds = (p * _dot_nt(dov, v)).astype(BF16)
                dv = dv + _dot_tn(p.astype(BF16), dov)
                dk = dk + _dot_tn(ds, q)
                dq_acc[pl.ds(q0, blk), :] += _dot(ds, k)
                return dk, dv

            zero = jnp.zeros((blk, HEAD_PAD), F32)
            carry = step(k0, (zero, zero), True)
            dk, dv = lax.fori_loop(
                kj + 1, nblk, lambda qi, c: step(pl.multiple_of(qi * blk, blk), c, False), carry)
            dk_ref[pl.ds(k0, blk), :] = dk.astype(BF16)
            dv_ref[pl.ds(k0, blk), :] = dv.astype(BF16)
            dcum_ref[pl.ds(k0, blk), :] = jnp.where(
                mine, -dk[:, LANE_CK:LANE_CK + 1], dcum_ref[pl.ds(k0, blk), :])
            return 0

        lax.fori_loop(0, nblk, k_block, 0)

        def finish(bi, _):
            r0 = pl.multiple_of(bi * blk, blk)
            dq = dq_acc[pl.ds(r0, blk), :]
            dq_ref[pl.ds(r0, blk), :] = dq.astype(BF16)
            dcum_ref[pl.ds(r0, blk), :] += jnp.where(mine, dq[:, LANE_RB:LANE_RB + 1], 0.0)
            return 0

        lax.fori_loop(0, nblk, finish, 0)

    def head_spec(j):
        return pl.BlockSpec((None, s, HEAD_PAD), lambda h: (j, 0, h))

    flat = pl.BlockSpec((s, HEAD_PAD), lambda h: (0, h))
    out = jax.ShapeDtypeStruct((s, FOX_PAD), BF16)
    return pl.pallas_call(
        body, name="attn_bwd", grid=(HEADS,),
        in_specs=[flat, head_spec(1), head_spec(2), flat],
        out_specs=[flat, flat, flat, pl.BlockSpec((s, LANES), lambda h: (0, 0))],
        out_shape=[out, out, out, jax.ShapeDtypeStruct((s, LANES), F32)],
        scratch_shapes=[pltpu.VMEM((s, HEAD_PAD), F32)],
        compiler_params=_params(),
    )(qb, qkv, qkv, do)


def _fox_in_bwd(dq, dk, dv, dg, wt, wft, dcum, f, x1, dx2, g1, ts):
    s = x1.shape[0]
    nt = s // ts

    def body(dq_ref, dk_ref, dv_ref, dg_ref, wt_ref, wft_ref, dcum_ref, f_ref, x1_ref, dx2_ref,
             g_ref, dx1_ref, dx1b_ref, df_ref, gn_ref, gbf_ref, rcar_ref):
        @pl.when(pl.program_id(0) == 0)
        def _():
            rcar_ref[...] = jnp.zeros_like(rcar_ref)
            gn_ref[...] = jnp.zeros_like(gn_ref)
            gbf_ref[...] = jnp.zeros_like(gbf_ref)

        dkv = dk_ref[...]
        rsum = _cumsum_rows(dcum_ref[...], reverse=True) + rcar_ref[0:1, :]
        df = rsum * _sigmoid(-f_ref[...])
        dfb = df.astype(BF16)
        dh = (_dot(dq_ref[...], wt_ref[0]) + _dot(dkv, wt_ref[1]) + _dot(dv_ref[...], wt_ref[2])
              + _dot(dg_ref[...], wt_ref[3]) + _dot(dfb, wft_ref[...]))
        dxn, dgn = _norm_bwd(x1_ref[...], g_ref[...], dh)
        dx1 = dx2_ref[...] + dxn
        dx1_ref[...] = dx1
        dx1b_ref[...] = dx1.astype(BF16)
        df_ref[...] = dfb
        gn_ref[...] += dgn
        gbf_ref[...] += jnp.sum(df, axis=0, keepdims=True)
        rcar_ref[...] = rsum[0:SUBLANES, :]

    rev = lambda i: (nt - 1 - i, 0)
    wide = pl.BlockSpec((ts, FOX_PAD), rev)
    return pl.pallas_call(
        body, name="fox_in_bwd", grid=(nt,),
        in_specs=[wide, wide, wide, wide,
                  _const_spec((4, FOX_PAD, D_MODEL)),
                  _const_spec((LANES, D_MODEL)),
                  pl.BlockSpec((ts, LANES), rev),
                  pl.BlockSpec((ts, LANES), rev),
                  pl.BlockSpec((ts, D_MODEL), rev),
                  pl.BlockSpec((ts, D_MODEL), rev),
                  _const_spec((1, D_MODEL))],
        out_specs=[pl.BlockSpec((ts, D_MODEL), rev),
                   pl.BlockSpec((ts, D_MODEL), rev),
                   pl.BlockSpec((ts, LANES), rev),
                   pl.BlockSpec((1, D_MODEL), lambda i: (0, 0)),
                   pl.BlockSpec((1, LANES), lambda i: (0, 0))],
        out_shape=[jax.ShapeDtypeStruct((s, D_MODEL), F32),
                   jax.ShapeDtypeStruct((s, D_MODEL), BF16),
                   jax.ShapeDtypeStruct((s, LANES), BF16),
                   jax.ShapeDtypeStruct((1, D_MODEL), F32),
                   jax.ShapeDtypeStruct((1, LANES), F32)],
        scratch_shapes=[pltpu.VMEM((SUBLANES, LANES), F32)],
        compiler_params=_params(),
    )(dq, dk, dv, dg, wt, wft, dcum, f, x1, dx2, g1)


def _lru_core_bwd(dx1b, w_out_t, xb, gate, hs, cw, cb, wa, ba, wx, bx, a_param, wa_t, wx_t, ts):
    s = xb.shape[0]
    nt = s // ts
    tpb = ts // SUBLANES

    def body(dx_ref, wo_ref, xb_ref, xbh_ref, gate_ref, hs_ref, hsh_ref, cw_ref, cb_ref, wa_ref,
             ba_ref, wx_ref, bx_ref, ap_ref, wat_ref, wxt_ref,
             du_ref, gwa_ref, gwx_ref, gvec_ref, acar_ref, dhcar_ref, dxccar_ref):
        step = pl.program_id(0)

        @pl.when(step == 0)
        def _():
            acar_ref[...] = jnp.zeros_like(acar_ref)
            dhcar_ref[...] = jnp.zeros_like(dhcar_ref)
            dxccar_ref[...] = jnp.zeros_like(dxccar_ref)
            gwa_ref[...] = jnp.zeros_like(gwa_ref)
            gwx_ref[...] = jnp.zeros_like(gwx_ref)
            gvec_ref[...] = jnp.zeros_like(gvec_ref)

        first_tile = step == nt - 1
        halo_on = jnp.where(first_tile, 0.0, 1.0)
        prev8 = xbh_ref[...] * halo_on
        hprev_row = hsh_ref[SUBLANES - 1:SUBLANES, :] * halo_on

        xbv = xb_ref[...]
        taps = _conv_taps(xbv, prev8)
        cw_v = cw_ref[...]
        xc, xcb, r, i, sp, a, mult = _lru_pre(taps, cw_v, cb_ref[...], wa_ref, ba_ref[...],
                                              wx_ref, bx_ref[...], ap_ref[...])
        hs = hs_ref[...]
        gv = gate_ref[...]
        sg = _sigmoid(gv)
        dy = _dot(dx_ref[...], wo_ref[...])
        dhs = dy * (gv * sg)
        dgate = dy * hs * (sg * (1.0 + gv * (1.0 - sg)))

        rows = lax.broadcasted_iota(jnp.int32, a.shape, 0)
        a_next = jnp.where(rows < ts - 1, pltpu.roll(a, ts - 1, 0), acar_ref[0:1, :])
        cum_a, dh_loc = _scan_rows(a_next, dhs, reverse=True)
        dh = cum_a * dhcar_ref[0:1, :] + dh_loc
        h_prev = jnp.where(rows >= 1, pltpu.roll(hs, 1, 0), hprev_row)

        da = dh * h_prev
        ixc = i * xc
        dmult = dh * ixc
        di = dh * mult * xc
        dxc = dh * mult * i
        dlog_a = da * a - dmult * (a * a) / mult
        dr = dlog_a * ((-LRU_C) * sp)
        dsp = jnp.sum(dlog_a * ((-LRU_C) * r), axis=0, keepdims=True)
        dra = dr * r * (1.0 - r)
        dia = di * i * (1.0 - i)
        drab = dra.astype(BF16)
        diab = dia.astype(BF16)
        back = []
        for n in range(LRU_BLOCKS):
            sl = slice(n * LRU_BLOCK_W, (n + 1) * LRU_BLOCK_W)
            gwa_ref[n] += _dot_tn(xcb[:, sl], drab[:, sl])
            gwx_ref[n] += _dot_tn(xcb[:, sl], diab[:, sl])
            back.append(_dot(drab[:, sl], wat_ref[n]) + _dot(diab[:, sl], wxt_ref[n]))
        dxc = dxc + jnp.concatenate(back, axis=1)

        nxt8 = dxccar_ref[...]
        rows8 = lax.broadcasted_iota(jnp.int32, nxt8.shape, 0)
        dxb = cw_v[3:4] * dxc
        for j in range(1, CONV_WIDTH):
            rj = pltpu.roll(dxc, ts - j, 0)
            pj = pltpu.roll(nxt8, SUBLANES - j, 0)
            tail = jnp.where(rows8 >= SUBLANES - j, pj, rj[ts - SUBLANES:])
            dxb = dxb + cw_v[3 - j:4 - j] * jnp.concatenate([rj[:ts - SUBLANES], tail], axis=0)

        du_ref[:, :LRU_WIDTH] = dxb.astype(BF16)
        du_ref[:, LRU_WIDTH:] = dgate.astype(BF16)

        z = -ap_ref[...]
        gvec = [jnp.sum(dxc * taps[3 - k], axis=0, keepdims=True) for k in range(CONV_WIDTH)]
        gvec.append(jnp.sum(dxc, axis=0, keepdims=True))
        gvec.append(jnp.sum(dra, axis=0, keepdims=True))
        gvec.append(jnp.sum(dia, axis=0, keepdims=True))
        gvec.append(-dsp * _sigmoid(z))
        gvec_ref[...] += jnp.concatenate(gvec, axis=0)

        acar_ref[...] = a[0:SUBLANES, :]
        dhcar_ref[...] = dh[0:SUBLANES, :]
        dxccar_ref[...] = dxc[0:SUBLANES, :]

    rev = lambda i: (nt - 1 - i, 0)
    halo = lambda i: (jnp.maximum((nt - 1 - i) * tpb - 1, 0), 0)
    tile = pl.BlockSpec((ts, LRU_WIDTH), rev)
    halo_spec = pl.BlockSpec((SUBLANES, LRU_WIDTH), halo)
    vec = _const_spec((1, LRU_WIDTH))
    blk = _const_spec((LRU_BLOCKS, LRU_BLOCK_W, LRU_BLOCK_W))
    acc_blk = pl.BlockSpec((LRU_BLOCKS, LRU_BLOCK_W, LRU_BLOCK_W), lambda i: (0, 0, 0))
    return pl.pallas_call(
        body, name="lru_core_bwd", grid=(nt,),
        in_specs=[pl.BlockSpec((ts, D_MODEL), rev),
                  _const_spec((D_MODEL, LRU_WIDTH)),
                  tile, halo_spec, tile, tile, halo_spec,
                  _const_spec((CONV_WIDTH, LRU_WIDTH)), vec, blk, vec, blk, vec, vec, blk, blk],
        out_specs=[pl.BlockSpec((ts, 2 * LRU_WIDTH), rev), acc_blk, acc_blk,
                   pl.BlockSpec((SUBLANES, LRU_WIDTH), lambda i: (0, 0))],
        out_shape=[jax.ShapeDtypeStruct((s, 2 * LRU_WIDTH), BF16),
                   jax.ShapeDtypeStruct((LRU_BLOCKS, LRU_BLOCK_W, LRU_BLOCK_W), F32),
                   jax.ShapeDtypeStruct((LRU_BLOCKS, LRU_BLOCK_W, LRU_BLOCK_W), F32),
                   jax.ShapeDtypeStruct((SUBLANES, LRU_WIDTH), F32)],
        scratch_shapes=[pltpu.VMEM((SUBLANES, LRU_WIDTH), F32),
                        pltpu.VMEM((SUBLANES, LRU_WIDTH), F32),
                        pltpu.VMEM((SUBLANES, LRU_WIDTH), F32)],
        compiler_params=_params(),
    )(dx1b, w_out_t, xb, xb, gate, hs, hs, cw, cb, wa, ba, wx, bx, a_param, wa_t, wx_t)


def _lru_in_bwd(du, w_in_t, x, dx1, g0, ts):
    s = x.shape[0]

    def body(du_ref, w_ref, x_ref, dx1_ref, g_ref, gx_ref, gn_ref):
        @pl.when(pl.program_id(0) == 0)
        def _():
            gn_ref[...] = jnp.zeros_like(gn_ref)

        dh = _dot(du_ref[...], w_ref[...])
        dxn, dgn = _norm_bwd(x_ref[...], g_ref[...], dh)
        gx_ref[...] = dx1_ref[...] + dxn
        gn_ref[...] += dgn

    tile = pl.BlockSpec((ts, D_MODEL), lambda i: (i, 0))
    return pl.pallas_call(
        body, name="lru_in_bwd", grid=(s // ts,),
        in_specs=[pl.BlockSpec((ts, 2 * LRU_WIDTH), lambda i: (i, 0)),
                  _const_spec((2 * LRU_WIDTH, D_MODEL)), tile, tile, _const_spec((1, D_MODEL))],
        out_specs=[tile, pl.BlockSpec((1, D_MODEL), lambda i: (0, 0))],
        out_shape=[jax.ShapeDtypeStruct((s, D_MODEL), F32),
                   jax.ShapeDtypeStruct((1, D_MODEL), F32)],
        compiler_params=_params(),
    )(du, w_in_t, x, dx1, g0)


def _weight_grad(a, b, ts, name, scale=1.0):
    s, ka = a.shape
    nb = b.shape[1]
    nt = s // ts

    def body(a_ref, b_ref, o_ref):
        @pl.when(pl.program_id(0) == 0)
        def _():
            o_ref[...] = jnp.zeros_like(o_ref)

        o_ref[...] += _dot_tn(a_ref[...], b_ref[...])
        if scale != 1.0:
            @pl.when(pl.program_id(0) == nt - 1)
            def _():
                o_ref[...] = o_ref[...] * scale

    return pl.pallas_call(
        body, name=name, grid=(nt,),
        in_specs=[pl.BlockSpec((ts, ka), lambda i: (i, 0)),
                  pl.BlockSpec((ts, nb), lambda i: (i, 0))],
        out_specs=pl.BlockSpec((ka, nb), lambda i: (0, 0)),
        out_shape=jax.ShapeDtypeStruct((ka, nb), F32),
        compiler_params=_params(),
    )(a, b)


def _adamw(g_parts, w, m, v, tr, name):
    nparts, rows, _ = g_parts.shape

    def body(gp_ref, w_ref, m_ref, v_ref, g_ref, d_ref, mo_ref, vo_ref):
        g = gp_ref[0]
        for k in range(1, nparts):
            g = g + gp_ref[k]
        m2 = ADAM_B1 * m_ref[...] + (1.0 - ADAM_B1) * g
        v2 = ADAM_B2 * v_ref[...] + (1.0 - ADAM_B2) * (g * g)
        m_hat = m2 / (1.0 - ADAM_B1 ** ADAM_STEP)
        v_hat = v2 / (1.0 - ADAM_B2 ** ADAM_STEP)
        g_ref[...] = g
        d_ref[...] = (-ADAM_LR) * (m_hat / (jnp.sqrt(v_hat) + ADAM_EPS) + ADAM_WD * w_ref[...])
        mo_ref[...] = m2
        vo_ref[...] = v2

    tile = pl.BlockSpec((tr, LANES), lambda i: (i, 0))
    out = jax.ShapeDtypeStruct((rows, LANES), F32)
    return pl.pallas_call(
        body, name=name, grid=(rows // tr,),
        in_specs=[pl.BlockSpec((nparts, tr, LANES), lambda i: (0, i, 0)), tile, tile, tile],
        out_specs=[tile, tile, tile, tile],
        out_shape=[out, out, out, out],
        compiler_params=_params(),
    )(g_parts, w, m, v)


def _mesh_pos():
    ix, iy, ic = lax.axis_index("x"), lax.axis_index("y"), lax.axis_index("c")
    return ix, iy, ic


def _peer(ix, iy, ic, mask):
    px = 1 - ix if mask & 4 else ix
    py = 1 - iy if mask & 2 else iy
    pc = 1 - ic if mask & 1 else ic
    return (px, py, pc), 4 * px + 2 * py + pc


def _all_gather(x, name):
    def body(x_ref, o_ref, send_sems, recv_sems, local_sem):
        ix, iy, ic = _mesh_pos()
        me = 4 * ix + 2 * iy + ic
        mine = pltpu.make_async_copy(x_ref, o_ref.at[me], local_sem)
        mine.start()
        sends = []
        for mask in range(1, N_DEV):
            peer, _ = _peer(ix, iy, ic, mask)
            cp = pltpu.make_async_remote_copy(
                src_ref=x_ref, dst_ref=o_ref.at[me],
                send_sem=send_sems.at[mask - 1], recv_sem=recv_sems.at[mask - 1],
                device_id=peer, device_id_type=pl.DeviceIdType.MESH)
            cp.start()
            sends.append(cp)
        for mask in range(1, N_DEV):
            peer, pidx = _peer(ix, iy, ic, mask)
            pltpu.make_async_remote_copy(
                src_ref=x_ref, dst_ref=o_ref.at[pidx],
                send_sem=send_sems.at[mask - 1], recv_sem=recv_sems.at[mask - 1],
                device_id=peer, device_id_type=pl.DeviceIdType.MESH).wait_recv()
        for cp in sends:
            cp.wait_send()
        mine.wait()

    return pl.pallas_call(
        body, name=name,
        in_specs=[pl.BlockSpec(memory_space=pl.ANY)],
        out_specs=pl.BlockSpec(memory_space=pl.ANY),
        out_shape=jax.ShapeDtypeStruct((N_DEV,) + x.shape, x.dtype),
        scratch_shapes=[pltpu.SemaphoreType.DMA((N_DEV - 1,)),
                        pltpu.SemaphoreType.DMA((N_DEV - 1,)),
                        pltpu.SemaphoreType.DMA],
    )(x)


def _all_to_all(x, name):
    def body(x_ref, o_ref, send_sems, recv_sems, local_sem):
        ix, iy, ic = _mesh_pos()
        me = 4 * ix + 2 * iy + ic
        mine = pltpu.make_async_copy(x_ref.at[me], o_ref.at[me], local_sem)
        mine.start()
        sends = []
        for mask in range(1, N_DEV):
            peer, pidx = _peer(ix, iy, ic, mask)
            cp = pltpu.make_async_remote_copy(
                src_ref=x_ref.at[pidx], dst_ref=o_ref.at[me],
                send_sem=send_sems.at[mask - 1], recv_sem=recv_sems.at[mask - 1],
                device_id=peer, device_id_type=pl.DeviceIdType.MESH)
            cp.start()
            sends.append(cp)
        for mask in range(1, N_DEV):
            peer, pidx = _peer(ix, iy, ic, mask)
            pltpu.make_async_remote_copy(
                src_ref=x_ref.at[me], dst_ref=o_ref.at[pidx],
                send_sem=send_sems.at[mask - 1], recv_sem=recv_sems.at[mask - 1],
                device_id=peer, device_id_type=pl.DeviceIdType.MESH).wait_recv()
        for cp in sends:
            cp.wait_send()
        mine.wait()

    return pl.pallas_call(
        body, name=name,
        in_specs=[pl.BlockSpec(memory_space=pl.ANY)],
        out_specs=pl.BlockSpec(memory_space=pl.ANY),
        out_shape=jax.ShapeDtypeStruct(x.shape, x.dtype),
        scratch_shapes=[pltpu.SemaphoreType.DMA((N_DEV - 1,)),
                        pltpu.SemaphoreType.DMA((N_DEV - 1,)),
                        pltpu.SemaphoreType.DMA],
    )(x)


def _pad_heads_cols(w):
    k = w.shape[0]
    w = w.reshape(k, HEADS, HEAD_DIM)
    return jnp.pad(w, ((0, 0), (0, 0), (0, HEAD_PAD - HEAD_DIM))).reshape(k, FOX_PAD)


def _unpad_heads_cols(w):
    k = w.shape[0]
    return w.reshape(k, HEADS, HEAD_PAD)[:, :, :HEAD_DIM].reshape(k, HEADS * HEAD_DIM)


def _selectors():
    r = lax.broadcasted_iota(jnp.int32, (3 * LANES, FOX_PAD), 0)
    c = lax.broadcasted_iota(jnp.int32, (3 * LANES, FOX_PAD), 1)
    part, head_r = r // LANES, r % LANES
    head_c, lane_c = c // HEAD_PAD, c % HEAD_PAD
    same = (head_r == head_c) & (head_r < HEADS)
    sel_q = jnp.where(same & (lane_c == LANE_RB + part), 1.0, 0.0)
    sel_k = jnp.where(same & (lane_c == LANE_CK + part), -1.0, 0.0)
    sel = jnp.stack([sel_q, sel_k, jnp.zeros_like(sel_q)]).astype(BF16)
    lane = lax.broadcasted_iota(jnp.int32, (1, FOX_PAD), 1) % HEAD_PAD
    ones_q = jnp.where((lane >= LANE_CK) & (lane < LANE_CK + 3), 1.0, 0.0)
    ones_k = jnp.where((lane >= LANE_RB) & (lane < LANE_RB + 3), 1.0, 0.0)
    ones_v = jnp.where((lane >= LANE_ONE_V) & (lane < LANE_ONE_V + 2), 1.0, 0.0)
    bias = jnp.stack([ones_q, ones_k, ones_v]).astype(F32)
    return sel, bias


def _local_step(x, target, norm_g, final_g, lru_w_in, conv_w, conv_b, wa, ba, wx, bx, a_param,
                lru_w_out, fox_w_in, b_f, fox_w_out, blk=512, ts=256):
    qk_scale = 1.0 / (HEAD_DIM ** 0.5)
    g0, g1 = norm_g[0:1], norm_g[1:2]
    gf = final_g.reshape(1, D_MODEL)
    w_in_b = lru_w_in.astype(BF16)
    w_out_b = lru_w_out.astype(BF16)
    wa_b, wx_b = wa.astype(BF16), wx.astype(BF16)
    wq = _pad_heads_cols(fox_w_in[:, 0:1024]) * qk_scale
    wk = _pad_heads_cols(fox_w_in[:, 1024:2048])
    wv = _pad_heads_cols(fox_w_in[:, 2048:3072])
    wg = _pad_heads_cols(fox_w_in[:, 3072:4096])
    wf = jnp.pad(fox_w_in[:, 4096:], ((0, 0), (0, LANES - HEADS)))
    w4 = jnp.stack([wq, wk, wv, wg]).astype(BF16)
    w4_t = jnp.transpose(w4, (0, 2, 1))
    wf_b = wf.astype(BF16)
    bf_pad = jnp.pad(b_f, ((0, 0), (0, LANES - HEADS)))
    fo = fox_w_out.reshape(HEADS, HEAD_DIM, D_MODEL)
    fo_pad = jnp.pad(fo, ((0, 0), (0, HEAD_PAD - HEAD_DIM), (0, 0))).reshape(FOX_PAD, D_MODEL)
    fo_b = fo_pad.astype(BF16)
    sel, bias = _selectors()

    xb, gate1, h0 = _lru_in_fwd(x, g0, w_in_b, ts)
    y1, hs = _lru_core_fwd(xb, gate1, conv_w, conv_b, wa_b, ba, wx_b, bx, a_param, ts)
    x1, h1, f, cparts = _fox_pre_fwd(x, y1, w_out_b, g1, wf_b, bf_pad, ts)
    qkv = _fox_proj_fwd(h1, cparts, w4[0:3], sel, bias, BF16, ts, "fox_proj_qkv")
    gate2 = _fox_proj_fwd(h1, None, w4[3:4], None, None, F32, ts, "fox_proj_gate")[0]
    o, qb = _attn_fwd(qkv, blk)
    dx2, y2, loss_acc, g_final = _fox_out_loss(o, gate2, fo_b, x1, target, gf, ts)

    do, dgate2 = _fox_out_bwd(dx2, fo_b.T, o, gate2, ts)
    dq, dk, dv, dcum = _attn_bwd(qb, qkv, do, blk)
    dx1, dx1b, df, g_norm1, g_bf = _fox_in_bwd(dq, dk, dv, dgate2, w4_t, wf_b.T, dcum, f, x1, dx2,
                                               g1, ts)
    du, g_wa, g_wx, g_vec = _lru_core_bwd(dx1b, w_out_b.T, xb, gate1, hs, conv_w, conv_b, wa_b, ba,
                                          wx_b, bx, a_param, jnp.transpose(wa_b, (0, 2, 1)),
                                          jnp.transpose(wx_b, (0, 2, 1)), ts)
    grad_x, g_norm0 = _lru_in_bwd(du, w_in_b.T, x, dx1, g0, ts)

    tw = 512
    g_lru_w_in = _weight_grad(h0, du, tw, "grad_lru_w_in")
    g_lru_w_out = _weight_grad(y1, dx1b, tw, "grad_lru_w_out")
    g_q = _weight_grad(h1, dq, tw, "grad_fox_wq", scale=qk_scale)
    g_k = _weight_grad(h1, dk, tw, "grad_fox_wk")
    g_v = _weight_grad(h1, dv, tw, "grad_fox_wv")
    g_g = _weight_grad(h1, dgate2, tw, "grad_fox_wg")
    g_f = _weight_grad(h1, df, tw, "grad_fox_wf")
    g_fox_w_in = jnp.concatenate(
        [_unpad_heads_cols(g_q), _unpad_heads_cols(g_k), _unpad_heads_cols(g_v),
         _unpad_heads_cols(g_g), g_f[:, :HEADS]], axis=1)
    g_fo = _weight_grad(y2, dx2.astype(BF16), tw, "grad_fox_w_out")
    g_fox_w_out = g_fo.reshape(HEADS, HEAD_PAD, D_MODEL)[:, :HEAD_DIM].reshape(
        HEADS * HEAD_DIM, D_MODEL)

    grads = dict(
        norm_g=jnp.concatenate([g_norm0, g_norm1], axis=0), final_g=g_final[0],
        lru_w_in=g_lru_w_in, lru_conv_w=g_vec[0:4], lru_conv_b=g_vec[4:5], lru_wa=g_wa,
        lru_ba=g_vec[5:6], lru_wx=g_wx, lru_bx=g_vec[6:7], lru_a_param=g_vec[7:8],
        lru_w_out=g_lru_w_out, fox_w_in=g_fox_w_in, fox_b_f=g_bf[:, :HEADS],
        fox_w_out=g_fox_w_out)
    return loss_acc[0, 0], grad_x, grads


SHARDED = ("lru_w_in", "lru_conv_w", "lru_w_out", "fox_w_in", "fox_w_out")
SMALL = ("norm_g", "final_g", "lru_conv_b", "lru_wa", "lru_ba", "lru_wx", "lru_bx", "lru_a_param",
         "fox_b_f")
ALL_WEIGHTS = ("norm_g", "final_g", "lru_w_in", "lru_conv_w", "lru_conv_b", "lru_wa", "lru_ba",
               "lru_wx", "lru_bx", "lru_a_param", "lru_w_out", "fox_w_in", "fox_b_f", "fox_w_out")


def _pack_shards(parts):
    flat = jnp.concatenate([p.reshape(-1) for p in parts])
    return jnp.pad(flat, (0, PACK_TOTAL - SHARD_TOTAL)).reshape(PACK_ROWS, LANES)


def _unpack(flat, shapes):
    out, off = [], 0
    for shp in shapes:
        n = 1
        for d in shp:
            n *= d
        out.append(flat[off:off + n].reshape(shp))
        off += n
    return out


def _cols_to_dest(g, per):
    k = g.shape[0]
    return jnp.transpose(g.reshape(k, N_DEV, per), (1, 0, 2)).reshape(N_DEV, k * per)


def kernel(x, norm_g, final_g, lru_w_in, lru_conv_w, lru_conv_b, lru_wa, lru_ba, lru_wx, lru_bx, lru_a_param, lru_w_out, fox_w_in, fox_b_f, fox_w_out, loss_target, m_norm_g, m_final_g, m_lru_w_in, m_lru_conv_w, m_lru_conv_b, m_lru_wa, m_lru_ba, m_lru_wx, m_lru_bx, m_lru_a_param, m_lru_w_out, m_fox_w_in, m_fox_b_f, m_fox_w_out, v_norm_g, v_final_g, v_lru_w_in, v_lru_conv_w, v_lru_conv_b, v_lru_wa, v_lru_ba, v_lru_wx, v_lru_bx, v_lru_a_param, v_lru_w_out, v_fox_w_in, v_fox_b_f, v_fox_w_out):
    w_loc = dict(norm_g=norm_g, final_g=final_g, lru_w_in=lru_w_in, lru_conv_w=lru_conv_w,
                 lru_conv_b=lru_conv_b, lru_wa=lru_wa, lru_ba=lru_ba, lru_wx=lru_wx, lru_bx=lru_bx,
                 lru_a_param=lru_a_param, lru_w_out=lru_w_out, fox_w_in=fox_w_in, fox_b_f=fox_b_f,
                 fox_w_out=fox_w_out)
    m_loc = dict(norm_g=m_norm_g, final_g=m_final_g, lru_w_in=m_lru_w_in, lru_conv_w=m_lru_conv_w,
                 lru_conv_b=m_lru_conv_b, lru_wa=m_lru_wa, lru_ba=m_lru_ba, lru_wx=m_lru_wx,
                 lru_bx=m_lru_bx, lru_a_param=m_lru_a_param, lru_w_out=m_lru_w_out,
                 fox_w_in=m_fox_w_in, fox_b_f=m_fox_b_f, fox_w_out=m_fox_w_out)
    v_loc = dict(norm_g=v_norm_g, final_g=v_final_g, lru_w_in=v_lru_w_in, lru_conv_w=v_lru_conv_w,
                 lru_conv_b=v_lru_conv_b, lru_wa=v_lru_wa, lru_ba=v_lru_ba, lru_wx=v_lru_wx,
                 lru_bx=v_lru_bx, lru_a_param=v_lru_a_param, lru_w_out=v_lru_w_out,
                 fox_w_in=v_fox_w_in, fox_b_f=v_fox_b_f, fox_w_out=v_fox_w_out)

    w_pack = _pack_shards([w_loc[n] for n in SHARDED])
    gathered = _all_gather(w_pack, "gather_weights").reshape(N_DEV, PACK_TOTAL)
    g_w_in, g_conv, g_w_out, g_fox_in, g_fox_out = _unpack_gathered(gathered)

    loss, grad_x, grads = _local_step(
        x[0], loss_target[0], norm_g, final_g, g_w_in, g_conv, lru_conv_b, lru_wa[0], lru_ba,
        lru_wx[0], lru_bx, lru_a_param, g_w_out, g_fox_in, fox_b_f, g_fox_out)

    small_flat = jnp.concatenate([grads[n].reshape(-1) for n in SMALL])
    small_flat = jnp.pad(small_flat, (0, N_DEV * SMALL_CHUNK - SMALL_TOTAL))
    send = jnp.concatenate([
        _cols_to_dest(grads["lru_w_in"], 384),
        _cols_to_dest(grads["lru_conv_w"], 192),
        grads["lru_w_out"].reshape(N_DEV, 192 * 1024),
        _cols_to_dest(grads["fox_w_in"], 514),
        grads["fox_w_out"].reshape(N_DEV, 128 * 1024),
        small_flat.reshape(N_DEV, SMALL_CHUNK)], axis=1).reshape(N_DEV, PACK_ROWS, LANES)
    recv = _all_to_all(send, "scatter_grads")

    m_pack = _pack_shards([m_loc[n] for n in SHARDED])
    v_pack = _pack_shards([v_loc[n] for n in SHARDED])
    g_sh, d_sh, m_sh, v_sh = _adamw(recv, w_pack, m_pack, v_pack, 2048, "adamw_sharded")

    shard_shapes = [w_loc[n].shape for n in SHARDED]
    out_g, out_d, out_m, out_v = {}, {}, {}, {}
    for dst, src in ((out_g, g_sh), (out_d, d_sh), (out_m, m_sh), (out_v, v_sh)):
        for n, a in zip(SHARDED, _unpack(src.reshape(-1), shard_shapes)):
            dst[n] = a

    small_rows = SMALL_CHUNK // LANES
    g_small = _all_gather(g_sh[PACK_ROWS - small_rows:], "gather_small_grads")
    g_small = g_small.reshape(1, N_DEV * small_rows, LANES)

    def pack_small(d):
        flat = jnp.concatenate([d[n].reshape(-1) for n in SMALL])
        return jnp.pad(flat, (0, N_DEV * SMALL_CHUNK - SMALL_TOTAL)).reshape(
            N_DEV * small_rows, LANES)

    g_sm, d_sm, m_sm, v_sm = _adamw(g_small, pack_small(w_loc), pack_small(m_loc),
                                    pack_small(v_loc), N_DEV * small_rows, "adamw_replicated")
    small_shapes = [w_loc[n].shape for n in SMALL]
    for dst, src in ((out_g, g_sm), (out_d, d_sm), (out_m, m_sm), (out_v, v_sm)):
        for n, a in zip(SMALL, _unpack(src.reshape(-1), small_shapes)):
            dst[n] = a

    loss = lax.psum(loss, ("x", "y", "c"))
    return (loss, grad_x[None], *[out_g[n] for n in ALL_WEIGHTS], *[out_d[n] for n in ALL_WEIGHTS],
            *[out_m[n] for n in ALL_WEIGHTS], *[out_v[n] for n in ALL_WEIGHTS])


def _unpack_gathered(gathered):
    off = 0
    pieces = []
    for n in SHARD_SIZES:
        pieces.append(gathered[:, off:off + n])
        off += n
    w_in = jnp.transpose(pieces[0].reshape(N_DEV, 1024, 384), (1, 0, 2)).reshape(1024, 3072)
    conv = jnp.transpose(pieces[1].reshape(N_DEV, 4, 192), (1, 0, 2)).reshape(4, 1536)
    w_out = pieces[2].reshape(1536, 1024)
    fox_in = jnp.transpose(pieces[3].reshape(N_DEV, 1024, 514), (1, 0, 2)).reshape(1024, 4112)
    fox_out = pieces[4].reshape(1024, 1024)
    return w_in, conv, w_out, fox_in, fox_out
```

```python
import functools

import jax
import jax.numpy as jnp
from jax import lax
from jax.experimental import pallas as pl
from jax.experimental.pallas import tpu as pltpu

F32 = jnp.float32
BF16 = jnp.bfloat16

D_MODEL = 1024
LRU_WIDTH = 1536
LRU_BLOCKS = 12
LRU_BLOCK_W = 128
CONV_WIDTH = 4
LRU_C = 8.0
HEADS = 16
HEAD_DIM = 64
HEAD_PAD = 128
FOX_PAD = HEADS * HEAD_PAD
HEADS_PER_STEP = 2
EPS = 1e-6
NEG_BIG = -1e30
N_DEV = 8

ADAM_LR = 0.001
ADAM_B1 = 0.9
ADAM_B2 = 0.999
ADAM_EPS = 1e-08
ADAM_WD = 0.01
ADAM_STEP = 10

LANE_RB = 64
LANE_CK = 67
LANE_ONE_V = 64

VMEM_LIMIT_BYTES = 56 * 1024 * 1024
LANES = 128
SUBLANES = 8

LRU_IN_SHARD = 2 * LRU_WIDTH // N_DEV
FOX_IN_COLS = 4 * HEADS * HEAD_DIM + HEADS
FOX_IN_SHARD = FOX_IN_COLS // N_DEV

SMALL_ROWS = (16, 8, 12, 1536, 12, 1536, 12, 12, 1)
SMALL_CHUNK_ROWS = 400
assert sum(SMALL_ROWS) <= N_DEV * SMALL_CHUNK_ROWS


def _params(n_grid_axes=1):
    return pltpu.CompilerParams(
        dimension_semantics=("arbitrary",) * n_grid_axes,
        vmem_limit_bytes=VMEM_LIMIT_BYTES)


def _const_spec(shape):
    nd = len(shape)
    return pl.BlockSpec(shape, lambda *_: (0,) * nd, pipeline_mode=pl.Buffered(1))


def _shift_down(x, k, fill):
    rows = lax.broadcasted_iota(jnp.int32, x.shape, 0)
    return jnp.where(rows >= k, pltpu.roll(x, k, 0), fill)


def _shift_up(x, k, fill):
    n = x.shape[0]
    rows = lax.broadcasted_iota(jnp.int32, x.shape, 0)
    return jnp.where(rows < n - k, pltpu.roll(x, n - k, 0), fill)


def _scan_rows(a, b, reverse=False):
    n = a.shape[0]
    shift = _shift_up if reverse else _shift_down
    k = 1
    while k < n:
        b = a * shift(b, k, 0.0) + b
        a = a * shift(a, k, 1.0)
        k *= 2
    return a, b


def _cumsum_rows(x, reverse=False):
    n = x.shape[0]
    shift = _shift_up if reverse else _shift_down
    k = 1
    while k < n:
        x = x + shift(x, k, 0.0)
        k *= 2
    return x


def _rstd(x):
    return lax.rsqrt(jnp.mean(x * x, axis=-1, keepdims=True) + EPS)


def _norm_bwd(x, g, dh):
    rstd = _rstd(x)
    xhat = x * rstd
    dg = jnp.sum(dh * xhat, axis=0, keepdims=True)
    dxh = dh * g
    dx = rstd * (dxh - xhat * jnp.mean(dxh * xhat, axis=-1, keepdims=True))
    return dx, dg


def _split3(x):
    hi = x.astype(BF16)
    r1 = x - hi.astype(F32)
    mid = r1.astype(BF16)
    lo = (r1 - mid.astype(F32)).astype(BF16)
    return hi, mid, lo


def _sigmoid(x):
    return jax.nn.sigmoid(x)


def _dot(a, b):
    return jnp.dot(a, b, preferred_element_type=F32)


def _dot_nt(a, b):
    return lax.dot_general(a, b, (((1,), (1,)), ((), ())), preferred_element_type=F32)


def _dot_tn(a, b):
    return lax.dot_general(a, b, (((0,), (0,)), ((), ())), preferred_element_type=F32)


def _conv_taps(xb, prev8):
    rows8 = lax.broadcasted_iota(jnp.int32, prev8.shape, 0)
    taps = [xb]
    for j in range(1, CONV_WIDTH):
        r = pltpu.roll(xb, j, 0)
        p = pltpu.roll(prev8, j, 0)
        head = jnp.where(rows8 < j, p, r[0:SUBLANES])
        taps.append(jnp.concatenate([head, r[SUBLANES:]], axis=0))
    return taps


def _lru_pre(taps, cw, cb, wa_ref, ba, wx_ref, bx, a_param):
    xc = cb + cw[3:4] * taps[0] + cw[2:3] * taps[1] + cw[1:2] * taps[2] + cw[0:1] * taps[3]
    xcb = xc.astype(BF16)
    ra, ia = [], []
    for n in range(LRU_BLOCKS):
        blk = xcb[:, n * LRU_BLOCK_W:(n + 1) * LRU_BLOCK_W]
        ra.append(_dot(blk, wa_ref[n]))
        ia.append(_dot(blk, wx_ref[n]))
    r = _sigmoid(jnp.concatenate(ra, axis=1) + ba)
    i = _sigmoid(jnp.concatenate(ia, axis=1) + bx)
    z = -a_param
    sp = jnp.maximum(z, 0.0) + jnp.log1p(jnp.exp(-jnp.abs(z)))
    log_a = (-LRU_C) * r * sp
    a = jnp.exp(log_a)
    one_minus_a2 = -jnp.tanh(log_a) * (a * a + 1.0)
    mult = jnp.sqrt(one_minus_a2)
    return xc, xcb, r, i, sp, a, mult


def _lru_in_fwd(x, g0, w_in, ts):
    s = x.shape[0]
    half = N_DEV // 2

    def body(x_ref, g_ref, w_ref, xb_ref, gate_ref, h_ref):
        xv = x_ref[...]
        h = (xv * _rstd(xv) * g_ref[...]).astype(BF16)
        u = [_dot(h, w_ref[j]) for j in range(N_DEV)]
        xb_ref[...] = jnp.concatenate(u[:half], axis=1)
        gate_ref[...] = jnp.concatenate(u[half:], axis=1)
        h_ref[...] = h

    return pl.pallas_call(
        body, name="lru_in_fwd", grid=(s // ts,),
        in_specs=[pl.BlockSpec((ts, D_MODEL), lambda i: (i, 0)),
                  _const_spec((1, D_MODEL)),
                  _const_spec((N_DEV, D_MODEL, LRU_IN_SHARD))],
        out_specs=[pl.BlockSpec((ts, LRU_WIDTH), lambda i: (i, 0)),
                   pl.BlockSpec((ts, LRU_WIDTH), lambda i: (i, 0)),
                   pl.BlockSpec((ts, D_MODEL), lambda i: (i, 0))],
        out_shape=[jax.ShapeDtypeStruct((s, LRU_WIDTH), F32),
                   jax.ShapeDtypeStruct((s, LRU_WIDTH), F32),
                   jax.ShapeDtypeStruct((s, D_MODEL), BF16)],
        compiler_params=_params(),
    )(x, g0, w_in)


def _lru_core_fwd(xb, gate, cw, cb, wa, ba, wx, bx, a_param, ts):
    s = xb.shape[0]

    def body(xb_ref, gate_ref, cw_ref, cb_ref, wa_ref, ba_ref, wx_ref, bx_ref, ap_ref,
             y_ref, hs_ref, prev_ref, hcar_ref):
        @pl.when(pl.program_id(0) == 0)
        def _():
            prev_ref[...] = jnp.zeros_like(prev_ref)
            hcar_ref[...] = jnp.zeros_like(hcar_ref)

        xbv = xb_ref[...]
        taps = _conv_taps(xbv, prev_ref[...])
        xc, _, _, i, _, a, mult = _lru_pre(taps, cw_ref[...], cb_ref[...], wa_ref, ba_ref[...],
                                           wx_ref, bx_ref[...], ap_ref[...])
        bterm = mult * (i * xc)
        cum_a, hloc = _scan_rows(a, bterm)
        hs = cum_a * hcar_ref[SUBLANES - 1:SUBLANES, :] + hloc
        gv = gate_ref[...]
        y_ref[...] = (hs * (gv * _sigmoid(gv))).astype(BF16)
        hs_ref[...] = hs
        prev_ref[...] = xbv[ts - SUBLANES:, :]
        hcar_ref[...] = hs[ts - SUBLANES:, :]

    vec = _const_spec((1, LRU_WIDTH))
    blk = _const_spec((LRU_BLOCKS, LRU_BLOCK_W, LRU_BLOCK_W))
    tile = pl.BlockSpec((ts, LRU_WIDTH), lambda i: (i, 0))
    return pl.pallas_call(
        body, name="lru_core_fwd", grid=(s // ts,),
        in_specs=[tile, tile, _const_spec((CONV_WIDTH, LRU_WIDTH)), vec, blk, vec, blk, vec, vec],
        out_specs=[tile, tile],
        out_shape=[jax.ShapeDtypeStruct((s, LRU_WIDTH), BF16),
                   jax.ShapeDtypeStruct((s, LRU_WIDTH), F32)],
        scratch_shapes=[pltpu.VMEM((SUBLANES, LRU_WIDTH), F32),
                        pltpu.VMEM((SUBLANES, LRU_WIDTH), F32)],
        compiler_params=_params(),
    )(xb, gate, cw, cb, wa, ba, wx, bx, a_param)


def _fox_pre_fwd(x, y, w_out, g1, wf, bf, ts):
    s = x.shape[0]

    def body(x_ref, y_ref, w_ref, g_ref, wf_ref, bf_ref, x1_ref, h1_ref, f_ref, cp_ref, ccar_ref):
        @pl.when(pl.program_id(0) == 0)
        def _():
            ccar_ref[...] = jnp.zeros_like(ccar_ref)

        x1 = x_ref[...] + _dot(y_ref[...], w_ref[...])
        h1 = (x1 * _rstd(x1) * g_ref[...]).astype(BF16)
        f = _dot(h1, wf_ref[...]) + bf_ref[...]
        logsig = jnp.minimum(f, 0.0) - jnp.log1p(jnp.exp(-jnp.abs(f)))
        cum = _cumsum_rows(logsig) + ccar_ref[SUBLANES - 1:SUBLANES, :]
        hi, mid, lo = _split3(cum)
        x1_ref[...] = x1
        h1_ref[...] = h1
        f_ref[...] = f
        cp_ref[...] = jnp.concatenate([hi, mid, lo], axis=1)
        ccar_ref[...] = cum[ts - SUBLANES:, :]

    return pl.pallas_call(
        body, name="fox_pre_fwd", grid=(s // ts,),
        in_specs=[pl.BlockSpec((ts, D_MODEL), lambda i: (i, 0)),
                  pl.BlockSpec((ts, LRU_WIDTH), lambda i: (i, 0)),
                  _const_spec((LRU_WIDTH, D_MODEL)),
                  _const_spec((1, D_MODEL)),
                  _const_spec((D_MODEL, LANES)),
                  _const_spec((1, LANES))],
        out_specs=[pl.BlockSpec((ts, D_MODEL), lambda i: (i, 0)),
                   pl.BlockSpec((ts, D_MODEL), lambda i: (i, 0)),
                   pl.BlockSpec((ts, LANES), lambda i: (i, 0)),
                   pl.BlockSpec((ts, 3 * LANES), lambda i: (i, 0))],
        out_shape=[jax.ShapeDtypeStruct((s, D_MODEL), F32),
                   jax.ShapeDtypeStruct((s, D_MODEL), BF16),
                   jax.ShapeDtypeStruct((s, LANES), F32),
                   jax.ShapeDtypeStruct((s, 3 * LANES), BF16)],
        scratch_shapes=[pltpu.VMEM((SUBLANES, LANES), F32)],
        compiler_params=_params(),
    )(x, y, w_out, g1, wf, bf)


def _fox_proj_fwd(h1, cparts, w, sel, bias, out_dtype, ts, name):
    s = h1.shape[0]
    ng = w.shape[0]
    use_sel = sel is not None

    def body(*refs):
        if use_sel:
            h_ref, cp_ref, w_ref, sel_ref, b_ref, o_ref = refs
            acc = _dot(h_ref[...], w_ref[...]) + _dot(cp_ref[...], sel_ref[...]) + b_ref[...]
        else:
            h_ref, w_ref, o_ref = refs
            acc = _dot(h_ref[...], w_ref[...])
        o_ref[...] = acc.astype(out_dtype)

    in_specs = [pl.BlockSpec((ts, D_MODEL), lambda j, i: (i, 0))]
    args = [h1]
    if use_sel:
        in_specs.append(pl.BlockSpec((ts, 3 * LANES), lambda j, i: (i, 0)))
        args.append(cparts)
    in_specs.append(pl.BlockSpec((None, D_MODEL, FOX_PAD), lambda j, i: (j, 0, 0)))
    args.append(w)
    if use_sel:
        in_specs.append(pl.BlockSpec((None, 3 * LANES, FOX_PAD), lambda j, i: (j, 0, 0)))
        in_specs.append(pl.BlockSpec((None, 1, FOX_PAD), lambda j, i: (j, 0, 0)))
        args += [sel, bias]
    return pl.pallas_call(
        body, name=name, grid=(ng, s // ts),
        in_specs=in_specs,
        out_specs=pl.BlockSpec((None, ts, FOX_PAD), lambda j, i: (j, i, 0)),
        out_shape=jax.ShapeDtypeStruct((ng, s, FOX_PAD), out_dtype),
        compiler_params=_params(2),
    )(*args)


def _attn_fwd(qkv, blk):
    s = qkv.shape[1]
    nblk = s // blk
    heads = [slice(i * HEAD_PAD, (i + 1) * HEAD_PAD) for i in range(HEADS_PER_STEP)]

    def body(q_ref, k_ref, v_ref, o_ref, qb_ref):
        row = lax.broadcasted_iota(jnp.int32, (blk, blk), 0)
        col = lax.broadcasted_iota(jnp.int32, (blk, blk), 1)
        lane = lax.broadcasted_iota(jnp.int32, (blk, HEAD_PAD), 1)

        def q_block(qi, _):
            q0 = pl.multiple_of(qi * blk, blk)
            qs = [q_ref[pl.ds(q0, blk), hd] for hd in heads]

            def step(k0, carry, masked):
                out = []
                scores = [_dot_nt(q, k_ref[pl.ds(k0, blk), hd]) for q, hd in zip(qs, heads)]
                for sc, hd, (m, acc) in zip(scores, heads, carry):
                    v = v_ref[pl.ds(k0, blk), hd]
                    if masked:
                        sc = jnp.where(col <= row, sc, NEG_BIG)
                    m_new = jnp.maximum(m, jnp.max(sc, axis=-1, keepdims=True))
                    p = jnp.exp(sc - m_new)
                    acc = jnp.exp(m - m_new) * acc + _dot(p.astype(BF16), v)
                    out.append((m_new, acc))
                return tuple(out)

            init = tuple((jnp.full((blk, 1), NEG_BIG, F32), jnp.zeros((blk, HEAD_PAD), F32))
                         for _ in heads)
            carry = lax.fori_loop(
                0, qi, lambda kj, c: step(pl.multiple_of(kj * blk, blk), c, False), init)
            carry = step(q0, carry, True)
            for q, hd, (m, acc) in zip(qs, heads, carry):
                l = acc[:, LANE_ONE_V:LANE_ONE_V + 1]
                o_ref[pl.ds(q0, blk), hd] = (acc / l).astype(BF16)
                qf = q.astype(F32)
                cq = (qf[:, LANE_RB:LANE_RB + 1] + qf[:, LANE_RB + 1:LANE_RB + 2]
                      + qf[:, LANE_RB + 2:LANE_RB + 3])
                hi, mid, lo = _split3(cq - (m + jnp.log(l)))
                qb_ref[pl.ds(q0, blk), hd] = jnp.where(lane == LANE_RB, hi, jnp.where(
                    lane == LANE_RB + 1, mid, jnp.where(lane == LANE_RB + 2, lo, q)))
            return 0

        lax.fori_loop(0, nblk, q_block, 0)

    width = HEADS_PER_STEP * HEAD_PAD

    def head_spec(j):
        return pl.BlockSpec((None, s, width), lambda h: (j, 0, h))

    out_spec = pl.BlockSpec((s, width), lambda h: (0, h))
    return pl.pallas_call(
        body, name="attn_fwd", grid=(HEADS // HEADS_PER_STEP,),
        in_specs=[head_spec(0), head_spec(1), head_spec(2)],
        out_specs=[out_spec, out_spec],
        out_shape=[jax.ShapeDtypeStruct((s, FOX_PAD), BF16),
                   jax.ShapeDtypeStruct((s, FOX_PAD), BF16)],
        compiler_params=_params(),
    )(qkv, qkv, qkv)


def _fox_out_loss(o, gate, w_out, x1, target, gf, ts):
    s = x1.shape[0]

    def body(o_ref, gt_ref, w_ref, x1_ref, t_ref, g_ref, dx2_ref, y2_ref, loss_ref, gfin_ref):
        @pl.when(pl.program_id(0) == 0)
        def _():
            loss_ref[...] = jnp.zeros_like(loss_ref)
            gfin_ref[...] = jnp.zeros_like(gfin_ref)

        gv = gt_ref[...]
        y2 = (o_ref[...] * (gv * _sigmoid(gv))).astype(BF16)
        x2 = x1_ref[...] + _dot(y2, w_ref[...])
        rstd = _rstd(x2)
        xhat = x2 * rstd
        g = g_ref[...]
        diff = xhat * g - t_ref[...]
        loss_ref[...] += 0.5 * jnp.sum(jnp.mean(diff * diff, axis=-1, keepdims=True))
        dy = diff * (1.0 / D_MODEL)
        gfin_ref[...] += jnp.sum(dy * xhat, axis=0, keepdims=True)
        dxh = dy * g
        dx2_ref[...] = rstd * (dxh - xhat * jnp.mean(dxh * xhat, axis=-1, keepdims=True))
        y2_ref[...] = y2

    return pl.pallas_call(
        body, name="fox_out_loss", grid=(s // ts,),
        in_specs=[pl.BlockSpec((ts, FOX_PAD), lambda i: (i, 0)),
                  pl.BlockSpec((ts, FOX_PAD), lambda i: (i, 0)),
                  _const_spec((FOX_PAD, D_MODEL)),
                  pl.BlockSpec((ts, D_MODEL), lambda i: (i, 0)),
                  pl.BlockSpec((ts, D_MODEL), lambda i: (i, 0)),
                  _const_spec((1, D_MODEL))],
        out_specs=[pl.BlockSpec((ts, D_MODEL), lambda i: (i, 0)),
                   pl.BlockSpec((ts, FOX_PAD), lambda i: (i, 0)),
                   pl.BlockSpec((SUBLANES, LANES), lambda i: (0, 0)),
                   pl.BlockSpec((1, D_MODEL), lambda i: (0, 0))],
        out_shape=[jax.ShapeDtypeStruct((s, D_MODEL), F32),
                   jax.ShapeDtypeStruct((s, FOX_PAD), BF16),
                   jax.ShapeDtypeStruct((SUBLANES, LANES), F32),
                   jax.ShapeDtypeStruct((1, D_MODEL), F32)],
        compiler_params=_params(),
    )(o, gate, w_out, x1, target, gf)


def _fox_out_bwd(dx2, w_out, o, gate, ts):
    s = dx2.shape[0]

    def body(dx_ref, w_ref, o_ref, gt_ref, do_ref, dg_ref):
        lane = lax.broadcasted_iota(jnp.int32, (ts, HEAD_PAD), 1)
        dy2 = _dot_nt(dx_ref[...].astype(BF16), w_ref[...])
        gv = gt_ref[...]
        sg = _sigmoid(gv)
        ov = o_ref[...]
        dov = dy2 * (gv * sg)
        dg_ref[...] = (dy2 * ov * (sg * (1.0 + gv * (1.0 - sg)))).astype(BF16)
        prod = dov * ov
        for h in range(HEADS):
            sl = slice(h * HEAD_PAD, (h + 1) * HEAD_PAD)
            delta = jnp.sum(prod[:, sl], axis=-1, keepdims=True)
            hi = delta.astype(BF16)
            lo = (delta - hi.astype(F32)).astype(BF16)
            do_h = dov[:, sl].astype(BF16)
            do_ref[:, sl] = jnp.where(lane == LANE_ONE_V, -hi,
                                      jnp.where(lane == LANE_ONE_V + 1, -lo, do_h))

    tile = pl.BlockSpec((ts, FOX_PAD), lambda i: (i, 0))
    return pl.pallas_call(
        body, name="fox_out_bwd", grid=(s // ts,),
        in_specs=[pl.BlockSpec((ts, D_MODEL), lambda i: (i, 0)),
                  _const_spec((FOX_PAD, D_MODEL)), tile, tile],
        out_specs=[tile, tile],
        out_shape=[jax.ShapeDtypeStruct((s, FOX_PAD), BF16),
                   jax.ShapeDtypeStruct((s, FOX_PAD), BF16)],
        compiler_params=_params(),
    )(dx2, w_out, o, gate)


def _attn_bwd(qb, qkv, do, blk):
    s = qb.shape[0]
    nblk = s // blk
    heads = [slice(i * HEAD_PAD, (i + 1) * HEAD_PAD) for i in range(HEADS_PER_STEP)]

    def body(q_ref, k_ref, v_ref, do_ref, dq_ref, dk_ref, dv_ref, dcum_ref, dq_acc):
        group = pl.program_id(0)
        kj = pl.program_id(1)
        row = lax.broadcasted_iota(jnp.int32, (blk, blk), 0)
        col = lax.broadcasted_iota(jnp.int32, (blk, blk), 1)
        lane = lax.broadcasted_iota(jnp.int32, (blk, LANES), 1)
        mine = [lane == group * HEADS_PER_STEP + i for i in range(HEADS_PER_STEP)]

        @pl.when(kj == 0)
        def _():
            dq_acc[...] = jnp.zeros_like(dq_acc)

        @pl.when((group == 0) & (kj == 0))
        def _():
            dcum_ref[...] = jnp.zeros_like(dcum_ref)

        k0 = pl.multiple_of(kj * blk, blk)
        ks = [k_ref[:, hd] for hd in heads]
        vs = [v_ref[:, hd] for hd in heads]

        def step(q0, carry, masked):
            out = []
            qs = [q_ref[pl.ds(q0, blk), hd] for hd in heads]
            dos = [do_ref[pl.ds(q0, blk), hd] for hd in heads]
            scores = [_dot_nt(q, k) for q, k in zip(qs, ks)]
            dps = [_dot_nt(dov, v) for dov, v in zip(dos, vs)]
            for hd, k, q, dov, sc, dp, (dk, dv) in zip(heads, ks, qs, dos, scores, dps, carry):
                p = jnp.exp(sc)
                if masked:
                    p = jnp.where(col <= row, p, 0.0)
                ds = (p * dp).astype(BF16)
                dv = dv + _dot_tn(p.astype(BF16), dov)
                dk = dk + _dot_tn(ds, q)
                dq_acc[pl.ds(q0, blk), hd] += _dot(ds, k)
                out.append((dk, dv))
            return tuple(out)

        zero = jnp.zeros((blk, HEAD_PAD), F32)
        carry = step(k0, tuple((zero, zero) for _ in heads), True)
        carry = lax.fori_loop(
            kj + 1, nblk, lambda qi, c: step(pl.multiple_of(qi * blk, blk), c, False), carry)
        dcum = dcum_ref[pl.ds(k0, blk), :]
        for hd, mask, (dk, dv) in zip(heads, mine, carry):
            dk_ref[:, hd] = dk.astype(BF16)
            dv_ref[:, hd] = dv.astype(BF16)
            dcum = jnp.where(mask, -dk[:, LANE_CK:LANE_CK + 1], dcum)
        dcum_ref[pl.ds(k0, blk), :] = dcum

        @pl.when(kj == nblk - 1)
        def _():
            def finish(bi, _):
                r0 = pl.multiple_of(bi * blk, blk)
                dcum = dcum_ref[pl.ds(r0, blk), :]
                for hd, mask in zip(heads, mine):
                    dq = dq_acc[pl.ds(r0, blk), hd]
                    dq_ref[pl.ds(r0, blk), hd] = dq.astype(BF16)
                    dcum = dcum + jnp.where(mask, dq[:, LANE_RB:LANE_RB + 1], 0.0)
                dcum_ref[pl.ds(r0, blk), :] = dcum
                return 0

            lax.fori_loop(0, nblk, finish, 0)

    width = HEADS_PER_STEP * HEAD_PAD
    whole = pl.BlockSpec((s, width), lambda h, j: (0, h))
    part = pl.BlockSpec((blk, width), lambda h, j: (j, h))
    out = jax.ShapeDtypeStruct((s, FOX_PAD), BF16)
    return pl.pallas_call(
        body, name="attn_bwd", grid=(HEADS // HEADS_PER_STEP, nblk),
        in_specs=[whole,
                  pl.BlockSpec((None, blk, width), lambda h, j: (1, j, h)),
                  pl.BlockSpec((None, blk, width), lambda h, j: (2, j, h)),
                  whole],
        out_specs=[whole, part, part, pl.BlockSpec((s, LANES), lambda h, j: (0, 0))],
        out_shape=[out, out, out, jax.ShapeDtypeStruct((s, LANES), F32)],
        scratch_shapes=[pltpu.VMEM((s, width), F32)],
        compiler_params=_params(2),
    )(qb, qkv, qkv, do)


def _fox_in_bwd(dq, dk, dv, dg, wt, wft, dcum, f, x1, dx2, g1, ts):
    s = x1.shape[0]
    nt = s // ts

    def body(dq_ref, dk_ref, dv_ref, dg_ref, wt_ref, wft_ref, dcum_ref, f_ref, x1_ref, dx2_ref,
             g_ref, dx1_ref, dx1b_ref, df_ref, gn_ref, gbf_ref, rcar_ref):
        @pl.when(pl.program_id(0) == 0)
        def _():
            rcar_ref[...] = jnp.zeros_like(rcar_ref)
            gn_ref[...] = jnp.zeros_like(gn_ref)
            gbf_ref[...] = jnp.zeros_like(gbf_ref)

        dkv = dk_ref[...]
        rsum = _cumsum_rows(dcum_ref[...], reverse=True) + rcar_ref[0:1, :]
        df = rsum * _sigmoid(-f_ref[...])
        dfb = df.astype(BF16)
        dh = (_dot_nt(dq_ref[...], wt_ref[0]) + _dot_nt(dkv, wt_ref[1])
              + _dot_nt(dv_ref[...], wt_ref[2]) + _dot_nt(dg_ref[...], wt_ref[3])
              + _dot_nt(dfb, wft_ref[...]))
        dxn, dgn = _norm_bwd(x1_ref[...], g_ref[...], dh)
        dx1 = dx2_ref[...] + dxn
        dx1_ref[...] = dx1
        dx1b_ref[...] = dx1.astype(BF16)
        df_ref[...] = dfb
        gn_ref[...] += dgn
        gbf_ref[...] += jnp.sum(df, axis=0, keepdims=True)
        rcar_ref[...] = rsum[0:SUBLANES, :]

    rev = lambda i: (nt - 1 - i, 0)
    wide = pl.BlockSpec((ts, FOX_PAD), rev)
    return pl.pallas_call(
        body, name="fox_in_bwd", grid=(nt,),
        in_specs=[wide, wide, wide, wide,
                  _const_spec((4, D_MODEL, FOX_PAD)),
                  _const_spec((D_MODEL, LANES)),
                  pl.BlockSpec((ts, LANES), rev),
                  pl.BlockSpec((ts, LANES), rev),
                  pl.BlockSpec((ts, D_MODEL), rev),
                  pl.BlockSpec((ts, D_MODEL), rev),
                  _const_spec((1, D_MODEL))],
        out_specs=[pl.BlockSpec((ts, D_MODEL), rev),
                   pl.BlockSpec((ts, D_MODEL), rev),
                   pl.BlockSpec((ts, LANES), rev),
                   pl.BlockSpec((1, D_MODEL), lambda i: (0, 0)),
                   pl.BlockSpec((1, LANES), lambda i: (0, 0))],
        out_shape=[jax.ShapeDtypeStruct((s, D_MODEL), F32),
                   jax.ShapeDtypeStruct((s, D_MODEL), BF16),
                   jax.ShapeDtypeStruct((s, LANES), BF16),
                   jax.ShapeDtypeStruct((1, D_MODEL), F32),
                   jax.ShapeDtypeStruct((1, LANES), F32)],
        scratch_shapes=[pltpu.VMEM((SUBLANES, LANES), F32)],
        compiler_params=_params(),
    )(dq, dk, dv, dg, wt, wft, dcum, f, x1, dx2, g1)


def _lru_core_bwd(dx1b, w_out, xb, gate, hs, cw, cb, wa, ba, wx, bx, a_param, wa_t, wx_t, ts):
    s = xb.shape[0]
    nt = s // ts
    tpb = ts // SUBLANES

    def body(dx_ref, wo_ref, xb_ref, xbh_ref, gate_ref, hs_ref, hsh_ref, cw_ref, cb_ref, wa_ref,
             ba_ref, wx_ref, bx_ref, ap_ref, wat_ref, wxt_ref,
             du_ref, gwa_ref, gwx_ref, gvec_ref, acar_ref, dhcar_ref, dxccar_ref):
        step = pl.program_id(0)

        @pl.when(step == 0)
        def _():
            acar_ref[...] = jnp.zeros_like(acar_ref)
            dhcar_ref[...] = jnp.zeros_like(dhcar_ref)
            dxccar_ref[...] = jnp.zeros_like(dxccar_ref)
            gwa_ref[...] = jnp.zeros_like(gwa_ref)
            gwx_ref[...] = jnp.zeros_like(gwx_ref)
            gvec_ref[...] = jnp.zeros_like(gvec_ref)

        first_tile = step == nt - 1
        halo_on = jnp.where(first_tile, 0.0, 1.0)
        prev8 = xbh_ref[...] * halo_on
        hprev_row = hsh_ref[SUBLANES - 1:SUBLANES, :] * halo_on

        xbv = xb_ref[...]
        taps = _conv_taps(xbv, prev8)
        cw_v = cw_ref[...]
        xc, xcb, r, i, sp, a, mult = _lru_pre(taps, cw_v, cb_ref[...], wa_ref, ba_ref[...],
                                              wx_ref, bx_ref[...], ap_ref[...])
        hs = hs_ref[...]
        gv = gate_ref[...]
        sg = _sigmoid(gv)
        dy = _dot_nt(dx_ref[...], wo_ref[...])
        dhs = dy * (gv * sg)
        dgate = dy * hs * (sg * (1.0 + gv * (1.0 - sg)))

        rows = lax.broadcasted_iota(jnp.int32, a.shape, 0)
        a_next = jnp.where(rows < ts - 1, pltpu.roll(a, ts - 1, 0), acar_ref[0:1, :])
        cum_a, dh_loc = _scan_rows(a_next, dhs, reverse=True)
        dh = cum_a * dhcar_ref[0:1, :] + dh_loc
        h_prev = jnp.where(rows >= 1, pltpu.roll(hs, 1, 0), hprev_row)

        da = dh * h_prev
        ixc = i * xc
        dmult = dh * ixc
        di = dh * mult * xc
        dxc = dh * mult * i
        dlog_a = da * a - dmult * (a * a) / mult
        dr = dlog_a * ((-LRU_C) * sp)
        dsp = jnp.sum(dlog_a * ((-LRU_C) * r), axis=0, keepdims=True)
        dra = dr * r * (1.0 - r)
        dia = di * i * (1.0 - i)
        drab = dra.astype(BF16)
        diab = dia.astype(BF16)
        back = []
        for n in range(LRU_BLOCKS):
            sl = slice(n * LRU_BLOCK_W, (n + 1) * LRU_BLOCK_W)
            gwa_ref[n] += _dot_tn(xcb[:, sl], drab[:, sl])
            gwx_ref[n] += _dot_tn(xcb[:, sl], diab[:, sl])
            back.append(_dot(drab[:, sl], wat_ref[n]) + _dot(diab[:, sl], wxt_ref[n]))
        dxc = dxc + jnp.concatenate(back, axis=1)

        nxt8 = dxccar_ref[...]
        rows8 = lax.broadcasted_iota(jnp.int32, nxt8.shape, 0)
        dxb = cw_v[3:4] * dxc
        for j in range(1, CONV_WIDTH):
            rj = pltpu.roll(dxc, ts - j, 0)
            pj = pltpu.roll(nxt8, SUBLANES - j, 0)
            tail = jnp.where(rows8 >= SUBLANES - j, pj, rj[ts - SUBLANES:])
            dxb = dxb + cw_v[3 - j:4 - j] * jnp.concatenate([rj[:ts - SUBLANES], tail], axis=0)

        du_ref[:, :LRU_WIDTH] = dxb.astype(BF16)
        du_ref[:, LRU_WIDTH:] = dgate.astype(BF16)

        z = -ap_ref[...]
        gvec = [jnp.sum(dxc * taps[3 - k], axis=0, keepdims=True) for k in range(CONV_WIDTH)]
        gvec.append(jnp.sum(dxc, axis=0, keepdims=True))
        gvec.append(jnp.sum(dra, axis=0, keepdims=True))
        gvec.append(jnp.sum(dia, axis=0, keepdims=True))
        gvec.append(-dsp * _sigmoid(z))
        gvec_ref[...] += jnp.concatenate(gvec, axis=0)

        acar_ref[...] = a[0:SUBLANES, :]
        dhcar_ref[...] = dh[0:SUBLANES, :]
        dxccar_ref[...] = dxc[0:SUBLANES, :]

    rev = lambda i: (nt - 1 - i, 0)
    halo = lambda i: (jnp.maximum((nt - 1 - i) * tpb - 1, 0), 0)
    tile = pl.BlockSpec((ts, LRU_WIDTH), rev)
    halo_spec = pl.BlockSpec((SUBLANES, LRU_WIDTH), halo)
    vec = _const_spec((1, LRU_WIDTH))
    blk = _const_spec((LRU_BLOCKS, LRU_BLOCK_W, LRU_BLOCK_W))
    acc_blk = pl.BlockSpec((LRU_BLOCKS, LRU_BLOCK_W, LRU_BLOCK_W), lambda i: (0, 0, 0))
    return pl.pallas_call(
        body, name="lru_core_bwd", grid=(nt,),
        in_specs=[pl.BlockSpec((ts, D_MODEL), rev),
                  _const_spec((LRU_WIDTH, D_MODEL)),
                  tile, halo_spec, tile, tile, halo_spec,
                  _const_spec((CONV_WIDTH, LRU_WIDTH)), vec, blk, vec, blk, vec, vec, blk, blk],
        out_specs=[pl.BlockSpec((ts, 2 * LRU_WIDTH), rev), acc_blk, acc_blk,
                   pl.BlockSpec((SUBLANES, LRU_WIDTH), lambda i: (0, 0))],
        out_shape=[jax.ShapeDtypeStruct((s, 2 * LRU_WIDTH), BF16),
                   jax.ShapeDtypeStruct((LRU_BLOCKS, LRU_BLOCK_W, LRU_BLOCK_W), F32),
                   jax.ShapeDtypeStruct((LRU_BLOCKS, LRU_BLOCK_W, LRU_BLOCK_W), F32),
                   jax.ShapeDtypeStruct((SUBLANES, LRU_WIDTH), F32)],
        scratch_shapes=[pltpu.VMEM((SUBLANES, LRU_WIDTH), F32),
                        pltpu.VMEM((SUBLANES, LRU_WIDTH), F32),
                        pltpu.VMEM((SUBLANES, LRU_WIDTH), F32)],
        compiler_params=_params(),
    )(dx1b, w_out, xb, xb, gate, hs, hs, cw, cb, wa, ba, wx, bx, a_param, wa_t, wx_t)


def _lru_in_bwd(du, w_in, x, dx1, g0, ts):
    s = x.shape[0]

    def body(du_ref, w_ref, x_ref, dx1_ref, g_ref, gx_ref, gn_ref):
        @pl.when(pl.program_id(0) == 0)
        def _():
            gn_ref[...] = jnp.zeros_like(gn_ref)

        duv = du_ref[...]
        dh = _dot_nt(duv[:, 0:LRU_IN_SHARD], w_ref[0])
        for j in range(1, N_DEV):
            dh = dh + _dot_nt(duv[:, j * LRU_IN_SHARD:(j + 1) * LRU_IN_SHARD], w_ref[j])
        dxn, dgn = _norm_bwd(x_ref[...], g_ref[...], dh)
        gx_ref[...] = dx1_ref[...] + dxn
        gn_ref[...] += dgn

    tile = pl.BlockSpec((ts, D_MODEL), lambda i: (i, 0))
    return pl.pallas_call(
        body, name="lru_in_bwd", grid=(s // ts,),
        in_specs=[pl.BlockSpec((ts, 2 * LRU_WIDTH), lambda i: (i, 0)),
                  _const_spec((N_DEV, D_MODEL, LRU_IN_SHARD)), tile, tile,
                  _const_spec((1, D_MODEL))],
        out_specs=[tile, pl.BlockSpec((1, D_MODEL), lambda i: (0, 0))],
        out_shape=[jax.ShapeDtypeStruct((s, D_MODEL), F32),
                   jax.ShapeDtypeStruct((1, D_MODEL), F32)],
        compiler_params=_params(),
    )(du, w_in, x, dx1, g0)


def _weight_grad(a, b, ts, name, scale=1.0, col_shards=1):
    s, ka = a.shape
    nb = b.shape[1]
    nt = s // ts
    per = nb // col_shards

    def body(a_ref, b_ref, o_ref):
        @pl.when(pl.program_id(0) == 0)
        def _():
            o_ref[...] = jnp.zeros_like(o_ref)

        if col_shards == 1:
            o_ref[...] += _dot_tn(a_ref[...], b_ref[...])
        else:
            av, bv = a_ref[...], b_ref[...]
            for j in range(col_shards):
                o_ref[j] += _dot_tn(av, bv[:, j * per:(j + 1) * per])
        if scale != 1.0:
            @pl.when(pl.program_id(0) == nt - 1)
            def _():
                o_ref[...] = o_ref[...] * scale

    out_dims = (ka, nb) if col_shards == 1 else (col_shards, ka, per)
    return pl.pallas_call(
        body, name=name, grid=(nt,),
        in_specs=[pl.BlockSpec((ts, ka), lambda i: (i, 0)),
                  pl.BlockSpec((ts, nb), lambda i: (i, 0))],
        out_specs=pl.BlockSpec(out_dims, lambda i: (0,) * len(out_dims)),
        out_shape=jax.ShapeDtypeStruct(out_dims, F32),
        compiler_params=_params(),
    )(a, b)


def _sum_parts(gp_ref):
    g = gp_ref[0]
    for k in range(1, gp_ref.shape[0]):
        g = g + gp_ref[k]
    return g


def _adamw(g_parts, w, m, v, tr, name):
    nparts, rows, cols = g_parts.shape

    def body(gp_ref, w_ref, m_ref, v_ref, g_ref, d_ref, mo_ref, vo_ref):
        g = _sum_parts(gp_ref)
        m2 = ADAM_B1 * m_ref[...] + (1.0 - ADAM_B1) * g
        v2 = ADAM_B2 * v_ref[...] + (1.0 - ADAM_B2) * (g * g)
        m_hat = m2 / (1.0 - ADAM_B1 ** ADAM_STEP)
        v_hat = v2 / (1.0 - ADAM_B2 ** ADAM_STEP)
        g_ref[...] = g
        d_ref[...] = (-ADAM_LR) * (m_hat / (jnp.sqrt(v_hat) + ADAM_EPS) + ADAM_WD * w_ref[...])
        mo_ref[...] = m2
        vo_ref[...] = v2

    tile = pl.BlockSpec((tr, cols), lambda i: (i, 0))
    out = jax.ShapeDtypeStruct((rows, cols), F32)
    return pl.pallas_call(
        body, name=name, grid=(rows // tr,),
        in_specs=[pl.BlockSpec((nparts, tr, cols), lambda i: (0, i, 0)), tile, tile, tile],
        out_specs=[tile, tile, tile, tile],
        out_shape=[out, out, out, out],
        compiler_params=_params(),
    )(g_parts, w, m, v)


def _reduce_parts(g_parts, name):
    _, rows, cols = g_parts.shape

    def body(gp_ref, g_ref):
        g_ref[...] = _sum_parts(gp_ref)

    return pl.pallas_call(
        body, name=name,
        out_shape=jax.ShapeDtypeStruct((rows, cols), F32),
        compiler_params=pltpu.CompilerParams(vmem_limit_bytes=VMEM_LIMIT_BYTES),
    )(g_parts)


def _mesh_pos():
    ix, iy, ic = lax.axis_index("x"), lax.axis_index("y"), lax.axis_index("c")
    return ix, iy, ic


def _peer(ix, iy, ic, mask):
    px = 1 - ix if mask & 4 else ix
    py = 1 - iy if mask & 2 else iy
    pc = 1 - ic if mask & 1 else ic
    return (px, py, pc), 4 * px + 2 * py + pc


def _exchange(arrays, scatter, name):
    n = len(arrays)

    def body(*refs):
        x_refs, o_refs = refs[:n], refs[n:2 * n]
        send_sems, recv_sems, local_sems = refs[2 * n:]
        ix, iy, ic = _mesh_pos()
        me = 4 * ix + 2 * iy + ic

        def src(a, dest):
            return x_refs[a].at[dest] if scatter else x_refs[a]

        local = [pltpu.make_async_copy(src(a, me), o_refs[a].at[me], local_sems.at[a])
                 for a in range(n)]
        for cp in local:
            cp.start()
        sends = []
        for mask in range(1, N_DEV):
            peer, pidx = _peer(ix, iy, ic, mask)
            for a in range(n):
                cp = pltpu.make_async_remote_copy(
                    src_ref=src(a, pidx), dst_ref=o_refs[a].at[me],
                    send_sem=send_sems.at[a, mask - 1], recv_sem=recv_sems.at[a, mask - 1],
                    device_id=peer, device_id_type=pl.DeviceIdType.MESH)
                cp.start()
                sends.append(cp)
        for mask in range(1, N_DEV):
            peer, pidx = _peer(ix, iy, ic, mask)
            for a in range(n):
                pltpu.make_async_remote_copy(
                    src_ref=src(a, me), dst_ref=o_refs[a].at[pidx],
                    send_sem=send_sems.at[a, mask - 1], recv_sem=recv_sems.at[a, mask - 1],
                    device_id=peer, device_id_type=pl.DeviceIdType.MESH).wait_recv()
        for cp in sends:
            cp.wait_send()
        for cp in local:
            cp.wait()

    out_shape = [jax.ShapeDtypeStruct(x.shape if scatter else (N_DEV,) + x.shape, x.dtype)
                 for x in arrays]
    return pl.pallas_call(
        body, name=name,
        in_specs=[pl.BlockSpec(memory_space=pl.ANY)] * n,
        out_specs=[pl.BlockSpec(memory_space=pl.ANY)] * n,
        out_shape=out_shape,
        scratch_shapes=[pltpu.SemaphoreType.DMA((n, N_DEV - 1)),
                        pltpu.SemaphoreType.DMA((n, N_DEV - 1)),
                        pltpu.SemaphoreType.DMA((n,))],
    )(*arrays)


def _pad_heads_cols(w):
    k = w.shape[0]
    w = w.reshape(k, HEADS, HEAD_DIM)
    return jnp.pad(w, ((0, 0), (0, 0), (0, HEAD_PAD - HEAD_DIM))).reshape(k, FOX_PAD)


def _unpad_heads_cols(w):
    k = w.shape[0]
    return w.reshape(k, HEADS, HEAD_PAD)[:, :, :HEAD_DIM].reshape(k, HEADS * HEAD_DIM)


def _selectors():
    r = lax.broadcasted_iota(jnp.int32, (3 * LANES, FOX_PAD), 0)
    c = lax.broadcasted_iota(jnp.int32, (3 * LANES, FOX_PAD), 1)
    part, head_r = r // LANES, r % LANES
    head_c, lane_c = c // HEAD_PAD, c % HEAD_PAD
    same = (head_r == head_c) & (head_r < HEADS)
    sel_q = jnp.where(same & (lane_c == LANE_RB + part), 1.0, 0.0)
    sel_k = jnp.where(same & (lane_c == LANE_CK + part), -1.0, 0.0)
    sel = jnp.stack([sel_q, sel_k, jnp.zeros_like(sel_q)]).astype(BF16)
    lane = lax.broadcasted_iota(jnp.int32, (1, FOX_PAD), 1) % HEAD_PAD
    ones_q = jnp.where((lane >= LANE_CK) & (lane < LANE_CK + 3), 1.0, 0.0)
    ones_k = jnp.where((lane >= LANE_RB) & (lane < LANE_RB + 3), 1.0, 0.0)
    ones_v = jnp.where((lane >= LANE_ONE_V) & (lane < LANE_ONE_V + 2), 1.0, 0.0)
    bias = jnp.stack([ones_q, ones_k, ones_v]).astype(F32)
    return sel, bias


def _local_step(x, target, norm_g, final_g, w_in8, conv_w, conv_b, wa, ba, wx, bx, a_param,
                w_out_b, fox_in8, b_f, fox_out_b, blk=512, ts=256):
    qk_scale = 1.0 / (HEAD_DIM ** 0.5)
    g0, g1 = norm_g[0:1], norm_g[1:2]
    gf = final_g.reshape(1, D_MODEL)
    wa_b, wx_b = wa.astype(BF16), wx.astype(BF16)
    fox_w_in = jnp.transpose(fox_in8, (1, 0, 2)).reshape(D_MODEL, FOX_IN_COLS)
    wq = _pad_heads_cols(fox_w_in[:, 0:1024]) * qk_scale
    wk = _pad_heads_cols(fox_w_in[:, 1024:2048])
    wv = _pad_heads_cols(fox_w_in[:, 2048:3072])
    wg = _pad_heads_cols(fox_w_in[:, 3072:4096])
    wf_b = jnp.pad(fox_w_in[:, 4096:], ((0, 0), (0, LANES - HEADS)))
    w4 = jnp.stack([wq, wk, wv, wg]).astype(BF16)
    bf_pad = jnp.pad(b_f, ((0, 0), (0, LANES - HEADS)))
    fo = fox_out_b.reshape(HEADS, HEAD_DIM, D_MODEL)
    fo_b = jnp.pad(fo, ((0, 0), (0, HEAD_PAD - HEAD_DIM), (0, 0))).reshape(FOX_PAD, D_MODEL)
    sel, bias = _selectors()

    xb, gate1, h0 = _lru_in_fwd(x, g0, w_in8, ts)
    y1, hs = _lru_core_fwd(xb, gate1, conv_w, conv_b, wa_b, ba, wx_b, bx, a_param, ts)
    x1, h1, f, cparts = _fox_pre_fwd(x, y1, w_out_b, g1, wf_b, bf_pad, ts)
    qkv = _fox_proj_fwd(h1, cparts, w4[0:3], sel, bias, BF16, ts, "fox_proj_qkv")
    gate2 = _fox_proj_fwd(h1, None, w4[3:4], None, None, F32, ts, "fox_proj_gate")[0]
    o, qb = _attn_fwd(qkv, blk)
    dx2, y2, loss_acc, g_final = _fox_out_loss(o, gate2, fo_b, x1, target, gf, ts)

    do, dgate2 = _fox_out_bwd(dx2, fo_b, o, gate2, ts)
    dq, dk, dv, dcum = _attn_bwd(qb, qkv, do, blk)
    dx1, dx1b, df, g_norm1, g_bf = _fox_in_bwd(dq, dk, dv, dgate2, w4, wf_b, dcum, f, x1, dx2,
                                               g1, ts)
    du, g_wa, g_wx, g_vec = _lru_core_bwd(dx1b, w_out_b, xb, gate1, hs, conv_w, conv_b, wa_b, ba,
                                          wx_b, bx, a_param, jnp.transpose(wa_b, (0, 2, 1)),
                                          jnp.transpose(wx_b, (0, 2, 1)), ts)
    grad_x, g_norm0 = _lru_in_bwd(du, w_in8, x, dx1, g0, ts)

    tw = 512
    g_lru_w_in = _weight_grad(h0, du, tw, "grad_lru_w_in", col_shards=N_DEV)
    g_lru_w_out = _weight_grad(y1, dx1b, tw, "grad_lru_w_out")
    g_q = _weight_grad(h1, dq, tw, "grad_fox_wq", scale=qk_scale)
    g_k = _weight_grad(h1, dk, tw, "grad_fox_wk")
    g_v = _weight_grad(h1, dv, tw, "grad_fox_wv")
    g_g = _weight_grad(h1, dgate2, tw, "grad_fox_wg")
    g_f = _weight_grad(h1, df, tw, "grad_fox_wf")
    g_fox_w_in = jnp.concatenate(
        [_unpad_heads_cols(g_q), _unpad_heads_cols(g_k), _unpad_heads_cols(g_v),
         _unpad_heads_cols(g_g), g_f[:, :HEADS]], axis=1)
    g_fox_w_in = jnp.transpose(g_fox_w_in.reshape(D_MODEL, N_DEV, FOX_IN_SHARD), (1, 0, 2))
    g_fo = _weight_grad(y2, dx2.astype(BF16), tw, "grad_fox_w_out")
    g_fox_w_out = g_fo.reshape(HEADS, HEAD_PAD, D_MODEL)[:, :HEAD_DIM].reshape(
        HEADS * HEAD_DIM, D_MODEL)

    grads = dict(
        norm_g=jnp.concatenate([g_norm0, g_norm1], axis=0), final_g=g_final[0],
        lru_w_in=g_lru_w_in, lru_conv_w=g_vec[0:4], lru_conv_b=g_vec[4:5], lru_wa=g_wa,
        lru_ba=g_vec[5:6], lru_wx=g_wx, lru_bx=g_vec[6:7], lru_a_param=g_vec[7:8],
        lru_w_out=g_lru_w_out, fox_w_in=g_fox_w_in, fox_b_f=g_bf[:, :HEADS],
        fox_w_out=g_fox_w_out)
    return loss_acc[0, 0], grad_x, grads


SMALL =("norm_g", "final_g", "lru_conv_b", "lru_wa", "lru_ba", "lru_wx", "lru_bx", "lru_a_param",
         "fox_b_f")
ALL_WEIGHTS = ("norm_g", "final_g", "lru_w_in", "lru_conv_w", "lru_conv_b", "lru_wa", "lru_ba",
               "lru_wx", "lru_bx", "lru_a_param", "lru_w_out", "fox_w_in", "fox_b_f", "fox_w_out")


def _pack_small(d):
    rows = []
    for n in SMALL:
        a = d[n].reshape(-1)
        if a.shape[0] % LANES:
            a = jnp.pad(a, (0, LANES - a.shape[0] % LANES))
        rows.append(a.reshape(-1, LANES))
    packed = jnp.concatenate(rows, axis=0)
    return jnp.pad(packed, ((0, N_DEV * SMALL_CHUNK_ROWS - packed.shape[0]), (0, 0)))


def _unpack_small(packed, like):
    out, off = {}, 0
    for n, nrows in zip(SMALL, SMALL_ROWS):
        size = like[n].size
        out[n] = packed[off:off + nrows].reshape(-1)[:size].reshape(like[n].shape)
        off += nrows
    return out


def kernel(x, norm_g, final_g, lru_w_in, lru_conv_w, lru_conv_b, lru_wa, lru_ba, lru_wx, lru_bx, lru_a_param, lru_w_out, fox_w_in, fox_b_f, fox_w_out, loss_target, m_norm_g, m_final_g, m_lru_w_in, m_lru_conv_w, m_lru_conv_b, m_lru_wa, m_lru_ba, m_lru_wx, m_lru_bx, m_lru_a_param, m_lru_w_out, m_fox_w_in, m_fox_b_f, m_fox_w_out, v_norm_g, v_final_g, v_lru_w_in, v_lru_conv_w, v_lru_conv_b, v_lru_wa, v_lru_ba, v_lru_wx, v_lru_bx, v_lru_a_param, v_lru_w_out, v_fox_w_in, v_fox_b_f, v_fox_w_out):
    w_loc = dict(norm_g=norm_g, final_g=final_g, lru_w_in=lru_w_in, lru_conv_w=lru_conv_w,
                 lru_conv_b=lru_conv_b, lru_wa=lru_wa, lru_ba=lru_ba, lru_wx=lru_wx, lru_bx=lru_bx,
                 lru_a_param=lru_a_param, lru_w_out=lru_w_out, fox_w_in=fox_w_in, fox_b_f=fox_b_f,
                 fox_w_out=fox_w_out)
    m_loc = dict(norm_g=m_norm_g, final_g=m_final_g, lru_w_in=m_lru_w_in, lru_conv_w=m_lru_conv_w,
                 lru_conv_b=m_lru_conv_b, lru_wa=m_lru_wa, lru_ba=m_lru_ba, lru_wx=m_lru_wx,
                 lru_bx=m_lru_bx, lru_a_param=m_lru_a_param, lru_w_out=m_lru_w_out,
                 fox_w_in=m_fox_w_in, fox_b_f=m_fox_b_f, fox_w_out=m_fox_w_out)
    v_loc = dict(norm_g=v_norm_g, final_g=v_final_g, lru_w_in=v_lru_w_in, lru_conv_w=v_lru_conv_w,
                 lru_conv_b=v_lru_conv_b, lru_wa=v_lru_wa, lru_ba=v_lru_ba, lru_wx=v_lru_wx,
                 lru_bx=v_lru_bx, lru_a_param=v_lru_a_param, lru_w_out=v_lru_w_out,
                 fox_w_in=v_fox_w_in, fox_b_f=v_fox_b_f, fox_w_out=v_fox_w_out)

    w_in8, conv8, w_out8, fox_in8, fox_out8 = _exchange(
        [lru_w_in[0].astype(BF16), lru_conv_w[0], lru_w_out[0].astype(BF16),
         fox_w_in[0].astype(BF16), fox_w_out[0].astype(BF16)], False, "gather_weights")
    conv_full = jnp.transpose(conv8, (1, 0, 2)).reshape(CONV_WIDTH, LRU_WIDTH)

    loss, grad_x, grads = _local_step(
        x[0], loss_target[0], norm_g, final_g, w_in8, conv_full, lru_conv_b, lru_wa[0], lru_ba,
        lru_wx[0], lru_bx, lru_a_param, w_out8.reshape(LRU_WIDTH, D_MODEL), fox_in8, fox_b_f,
        fox_out8.reshape(HEADS * HEAD_DIM, D_MODEL))

    conv_send = jnp.transpose(grads["lru_conv_w"].reshape(CONV_WIDTH, N_DEV, -1), (1, 0, 2))
    small_send = _pack_small(grads).reshape(N_DEV, SMALL_CHUNK_ROWS, LANES)
    r_w_in, r_conv, r_w_out, r_fox_in, r_fox_out, r_small = _exchange(
        [grads["lru_w_in"], conv_send, grads["lru_w_out"].reshape(N_DEV, -1, D_MODEL),
         grads["fox_w_in"], grads["fox_w_out"].reshape(N_DEV, -1, D_MODEL), small_send],
        True, "scatter_grads")

    out = {}
    for n, recv, tr in (("lru_w_in", r_w_in, 256), ("lru_conv_w", r_conv, CONV_WIDTH),
                        ("lru_w_out", r_w_out, 96), ("fox_w_in", r_fox_in, 128),
                        ("fox_w_out", r_fox_out, 64)):
        res = _adamw(recv, w_loc[n][0], m_loc[n][0], v_loc[n][0], tr, "adamw_" + n)
        out[n] = [a[None] for a in res]

    g_chunk = _reduce_parts(r_small, "reduce_small_grads")
    g_small, = _exchange([g_chunk], False, "gather_small_grads")
    g_small = g_small.reshape(1, N_DEV * SMALL_CHUNK_ROWS, LANES)
    res = _adamw(g_small, _pack_small(w_loc), _pack_small(m_loc), _pack_small(v_loc),
                 N_DEV * SMALL_CHUNK_ROWS, "adamw_replicated")
    small_out = [_unpack_small(a, w_loc) for a in res]
    for n in SMALL:
        out[n] = [d[n] for d in small_out]

    loss = lax.psum(loss, ("x", "y", "c"))
    return (loss, grad_x[None], *[out[n][0] for n in ALL_WEIGHTS], *[out[n][1] for n in ALL_WEIGHTS],
            *[out[n][2] for n in ALL_WEIGHTS], *[out[n][3] for n in ALL_WEIGHTS])
```

```python
import functools

import jax
import jax.numpy as jnp
from jax import lax
from jax.experimental import pallas as pl
from jax.experimental.pallas import tpu as pltpu

F32 = jnp.float32
BF16 = jnp.bfloat16

D_MODEL = 1024
LRU_WIDTH = 1536
LRU_BLOCKS = 12
LRU_BLOCK_W = 128
CONV_WIDTH = 4
LRU_C = 8.0
HEADS = 16
HEAD_DIM = 64
HEAD_PAD = 128
FOX_PAD = HEADS * HEAD_PAD
HEADS_PER_STEP = 2
EPS = 1e-6
NEG_BIG = -1e30
N_DEV = 8

ADAM_LR = 0.001
ADAM_B1 = 0.9
ADAM_B2 = 0.999
ADAM_EPS = 1e-08
ADAM_WD = 0.01
ADAM_STEP = 10

LANE_RB = 64
LANE_CK = 67
LANE_ONE_V = 64

VMEM_LIMIT_BYTES = 56 * 1024 * 1024
LANES = 128
SUBLANES = 8

LRU_IN_SHARD = 2 * LRU_WIDTH // N_DEV
FOX_IN_COLS = 4 * HEADS * HEAD_DIM + HEADS
FOX_IN_SHARD = FOX_IN_COLS // N_DEV

SMALL_ROWS = (16, 8, 12, 1536, 12, 1536, 12, 12, 1)
SMALL_CHUNK_ROWS = 400
assert sum(SMALL_ROWS) <= N_DEV * SMALL_CHUNK_ROWS


def _params(n_grid_axes=1):
    return pltpu.CompilerParams(
        dimension_semantics=("arbitrary",) * n_grid_axes,
        vmem_limit_bytes=VMEM_LIMIT_BYTES)


def _const_spec(shape):
    nd = len(shape)
    return pl.BlockSpec(shape, lambda *_: (0,) * nd, pipeline_mode=pl.Buffered(1))


def _shift_down(x, k, fill):
    rows = lax.broadcasted_iota(jnp.int32, x.shape, 0)
    return jnp.where(rows >= k, pltpu.roll(x, k, 0), fill)


def _shift_up(x, k, fill):
    n = x.shape[0]
    rows = lax.broadcasted_iota(jnp.int32, x.shape, 0)
    return jnp.where(rows < n - k, pltpu.roll(x, n - k, 0), fill)


def _scan_rows(a, b, reverse=False):
    n = a.shape[0]
    shift = _shift_up if reverse else _shift_down
    k = 1
    while k < n:
        b = a * shift(b, k, 0.0) + b
        a = a * shift(a, k, 1.0)
        k *= 2
    return a, b


def _cumsum_rows(x, reverse=False):
    n = x.shape[0]
    shift = _shift_up if reverse else _shift_down
    k = 1
    while k < n:
        x = x + shift(x, k, 0.0)
        k *= 2
    return x


def _rstd(x):
    return lax.rsqrt(jnp.mean(x * x, axis=-1, keepdims=True) + EPS)


def _norm_bwd(x, g, dh):
    rstd = _rstd(x)
    xhat = x * rstd
    dg = jnp.sum(dh * xhat, axis=0, keepdims=True)
    dxh = dh * g
    dx = rstd * (dxh - xhat * jnp.mean(dxh * xhat, axis=-1, keepdims=True))
    return dx, dg


def _split3(x):
    hi = x.astype(BF16)
    r1 = x - hi.astype(F32)
    mid = r1.astype(BF16)
    lo = (r1 - mid.astype(F32)).astype(BF16)
    return hi, mid, lo


def _sigmoid(x):
    return jax.nn.sigmoid(x)


def _dot(a, b):
    return jnp.dot(a, b, preferred_element_type=F32)


def _dot_nt(a, b):
    return lax.dot_general(a, b, (((1,), (1,)), ((), ())), preferred_element_type=F32)


def _dot_tn(a, b):
    return lax.dot_general(a, b, (((0,), (0,)), ((), ())), preferred_element_type=F32)


def _conv_taps(xb, prev8):
    rows8 = lax.broadcasted_iota(jnp.int32, prev8.shape, 0)
    taps = [xb]
    for j in range(1, CONV_WIDTH):
        r = pltpu.roll(xb, j, 0)
        p = pltpu.roll(prev8, j, 0)
        head = jnp.where(rows8 < j, p, r[0:SUBLANES])
        taps.append(jnp.concatenate([head, r[SUBLANES:]], axis=0))
    return taps


def _lru_pre(taps, cw, cb, wa_ref, ba, wx_ref, bx, a_param):
    xc = cb + cw[3:4] * taps[0] + cw[2:3] * taps[1] + cw[1:2] * taps[2] + cw[0:1] * taps[3]
    xcb = xc.astype(BF16)
    ra, ia = [], []
    for n in range(LRU_BLOCKS):
        blk = xcb[:, n * LRU_BLOCK_W:(n + 1) * LRU_BLOCK_W]
        ra.append(_dot(blk, wa_ref[n]))
        ia.append(_dot(blk, wx_ref[n]))
    r = _sigmoid(jnp.concatenate(ra, axis=1) + ba)
    i = _sigmoid(jnp.concatenate(ia, axis=1) + bx)
    z = -a_param
    sp = jnp.maximum(z, 0.0) + jnp.log1p(jnp.exp(-jnp.abs(z)))
    log_a = (-LRU_C) * r * sp
    a = jnp.exp(log_a)
    one_minus_a2 = -jnp.tanh(log_a) * (a * a + 1.0)
    mult = jnp.sqrt(one_minus_a2)
    return xc, xcb, r, i, sp, a, mult


def _lru_in_fwd(x, g0, w_in, ts):
    s = x.shape[0]
    half = N_DEV // 2

    def body(x_ref, g_ref, w_ref, xb_ref, gate_ref, h_ref):
        xv = x_ref[...]
        h = (xv * _rstd(xv) * g_ref[...]).astype(BF16)
        u = [_dot(h, w_ref[j]) for j in range(N_DEV)]
        xb_ref[...] = jnp.concatenate(u[:half], axis=1)
        gate_ref[...] = jnp.concatenate(u[half:], axis=1)
        h_ref[...] = h

    return pl.pallas_call(
        body, name="lru_in_fwd", grid=(s // ts,),
        in_specs=[pl.BlockSpec((ts, D_MODEL), lambda i: (i, 0)),
                  _const_spec((1, D_MODEL)),
                  _const_spec((N_DEV, D_MODEL, LRU_IN_SHARD))],
        out_specs=[pl.BlockSpec((ts, LRU_WIDTH), lambda i: (i, 0)),
                   pl.BlockSpec((ts, LRU_WIDTH), lambda i: (i, 0)),
                   pl.BlockSpec((ts, D_MODEL), lambda i: (i, 0))],
        out_shape=[jax.ShapeDtypeStruct((s, LRU_WIDTH), F32),
                   jax.ShapeDtypeStruct((s, LRU_WIDTH), F32),
                   jax.ShapeDtypeStruct((s, D_MODEL), BF16)],
        compiler_params=_params(),
    )(x, g0, w_in)


def _lru_core_fwd(xb, gate, cw, cb, wa, ba, wx, bx, a_param, ts):
    s = xb.shape[0]

    def body(xb_ref, gate_ref, cw_ref, cb_ref, wa_ref, ba_ref, wx_ref, bx_ref, ap_ref,
             y_ref, hs_ref, prev_ref, hcar_ref):
        @pl.when(pl.program_id(0) == 0)
        def _():
            prev_ref[...] = jnp.zeros_like(prev_ref)
            hcar_ref[...] = jnp.zeros_like(hcar_ref)

        xbv = xb_ref[...]
        taps = _conv_taps(xbv, prev_ref[...])
        xc, _, _, i, _, a, mult = _lru_pre(taps, cw_ref[...], cb_ref[...], wa_ref, ba_ref[...],
                                           wx_ref, bx_ref[...], ap_ref[...])
        bterm = mult * (i * xc)
        cum_a, hloc = _scan_rows(a, bterm)
        hs = cum_a * hcar_ref[SUBLANES - 1:SUBLANES, :] + hloc
        gv = gate_ref[...]
        y_ref[...] = (hs * (gv * _sigmoid(gv))).astype(BF16)
        hs_ref[...] = hs
        prev_ref[...] = xbv[ts - SUBLANES:, :]
        hcar_ref[...] = hs[ts - SUBLANES:, :]

    vec = _const_spec((1, LRU_WIDTH))
    blk = _const_spec((LRU_BLOCKS, LRU_BLOCK_W, LRU_BLOCK_W))
    tile = pl.BlockSpec((ts, LRU_WIDTH), lambda i: (i, 0))
    return pl.pallas_call(
        body, name="lru_core_fwd", grid=(s // ts,),
        in_specs=[tile, tile, _const_spec((CONV_WIDTH, LRU_WIDTH)), vec, blk, vec, blk, vec, vec],
        out_specs=[tile, tile],
        out_shape=[jax.ShapeDtypeStruct((s, LRU_WIDTH), BF16),
                   jax.ShapeDtypeStruct((s, LRU_WIDTH), F32)],
        scratch_shapes=[pltpu.VMEM((SUBLANES, LRU_WIDTH), F32),
                        pltpu.VMEM((SUBLANES, LRU_WIDTH), F32)],
        compiler_params=_params(),
    )(xb, gate, cw, cb, wa, ba, wx, bx, a_param)


def _fox_pre_fwd(x, y, w_out, g1, wf, bf, ts):
    s = x.shape[0]

    def body(x_ref, y_ref, w_ref, g_ref, wf_ref, bf_ref, x1_ref, h1_ref, f_ref, cp_ref, ccar_ref):
        @pl.when(pl.program_id(0) == 0)
        def _():
            ccar_ref[...] = jnp.zeros_like(ccar_ref)

        x1 = x_ref[...] + _dot(y_ref[...], w_ref[...])
        h1 = (x1 * _rstd(x1) * g_ref[...]).astype(BF16)
        f = _dot(h1, wf_ref[...]) + bf_ref[...]
        logsig = jnp.minimum(f, 0.0) - jnp.log1p(jnp.exp(-jnp.abs(f)))
        cum = _cumsum_rows(logsig) + ccar_ref[SUBLANES - 1:SUBLANES, :]
        hi, mid, lo = _split3(cum)
        x1_ref[...] = x1
        h1_ref[...] = h1
        f_ref[...] = f
        cp_ref[...] = jnp.concatenate([hi, mid, lo], axis=1)
        ccar_ref[...] = cum[ts - SUBLANES:, :]

    return pl.pallas_call(
        body, name="fox_pre_fwd", grid=(s // ts,),
        in_specs=[pl.BlockSpec((ts, D_MODEL), lambda i: (i, 0)),
                  pl.BlockSpec((ts, LRU_WIDTH), lambda i: (i, 0)),
                  _const_spec((LRU_WIDTH, D_MODEL)),
                  _const_spec((1, D_MODEL)),
                  _const_spec((D_MODEL, LANES)),
                  _const_spec((1, LANES))],
        out_specs=[pl.BlockSpec((ts, D_MODEL), lambda i: (i, 0)),
                   pl.BlockSpec((ts, D_MODEL), lambda i: (i, 0)),
                   pl.BlockSpec((ts, LANES), lambda i: (i, 0)),
                   pl.BlockSpec((ts, 3 * LANES), lambda i: (i, 0))],
        out_shape=[jax.ShapeDtypeStruct((s, D_MODEL), F32),
                   jax.ShapeDtypeStruct((s, D_MODEL), BF16),
                   jax.ShapeDtypeStruct((s, LANES), F32),
                   jax.ShapeDtypeStruct((s, 3 * LANES), BF16)],
        scratch_shapes=[pltpu.VMEM((SUBLANES, LANES), F32)],
        compiler_params=_params(),
    )(x, y, w_out, g1, wf, bf)


def _fox_proj_fwd(h1, cparts, w, sel, bias, out_dtype, ts, name):
    s = h1.shape[0]
    ng = w.shape[0]
    use_sel = sel is not None

    def body(*refs):
        if use_sel:
            h_ref, cp_ref, w_ref, sel_ref, b_ref, o_ref = refs
            acc = _dot(h_ref[...], w_ref[...]) + _dot(cp_ref[...], sel_ref[...]) + b_ref[...]
        else:
            h_ref, w_ref, o_ref = refs
            acc = _dot(h_ref[...], w_ref[...])
        o_ref[...] = acc.astype(out_dtype)

    in_specs = [pl.BlockSpec((ts, D_MODEL), lambda j, i: (i, 0))]
    args = [h1]
    if use_sel:
        in_specs.append(pl.BlockSpec((ts, 3 * LANES), lambda j, i: (i, 0)))
        args.append(cparts)
    in_specs.append(pl.BlockSpec((None, D_MODEL, FOX_PAD), lambda j, i: (j, 0, 0)))
    args.append(w)
    if use_sel:
        in_specs.append(pl.BlockSpec((None, 3 * LANES, FOX_PAD), lambda j, i: (j, 0, 0)))
        in_specs.append(pl.BlockSpec((None, 1, FOX_PAD), lambda j, i: (j, 0, 0)))
        args += [sel, bias]
    return pl.pallas_call(
        body, name=name, grid=(ng, s // ts),
        in_specs=in_specs,
        out_specs=pl.BlockSpec((None, ts, FOX_PAD), lambda j, i: (j, i, 0)),
        out_shape=jax.ShapeDtypeStruct((ng, s, FOX_PAD), out_dtype),
        compiler_params=_params(2),
    )(*args)


def _attn_fwd(qkv, blk):
    s = qkv.shape[1]
    nblk = s // blk
    heads = [slice(i * HEAD_PAD, (i + 1) * HEAD_PAD) for i in range(HEADS_PER_STEP)]

    def body(q_ref, k_ref, v_ref, o_ref, qb_ref):
        row = lax.broadcasted_iota(jnp.int32, (blk, blk), 0)
        col = lax.broadcasted_iota(jnp.int32, (blk, blk), 1)
        lane = lax.broadcasted_iota(jnp.int32, (blk, HEAD_PAD), 1)

        def q_block(qi, _):
            q0 = pl.multiple_of(qi * blk, blk)
            qs = [q_ref[pl.ds(q0, blk), hd] for hd in heads]

            def step(k0, carry, masked):
                out = []
                scores = [_dot_nt(q, k_ref[pl.ds(k0, blk), hd]) for q, hd in zip(qs, heads)]
                for sc, hd, (m, acc) in zip(scores, heads, carry):
                    v = v_ref[pl.ds(k0, blk), hd]
                    if masked:
                        sc = jnp.where(col <= row, sc, NEG_BIG)
                    m_new = jnp.maximum(m, jnp.max(sc, axis=-1, keepdims=True))
                    p = jnp.exp(sc - m_new)
                    acc = jnp.exp(m - m_new) * acc + _dot(p.astype(BF16), v)
                    out.append((m_new, acc))
                return tuple(out)

            init = tuple((jnp.full((blk, 1), NEG_BIG, F32), jnp.zeros((blk, HEAD_PAD), F32))
                         for _ in heads)
            carry = lax.fori_loop(
                0, qi, lambda kj, c: step(pl.multiple_of(kj * blk, blk), c, False), init)
            carry = step(q0, carry, True)
            for q, hd, (m, acc) in zip(qs, heads, carry):
                l = acc[:, LANE_ONE_V:LANE_ONE_V + 1]
                o_ref[pl.ds(q0, blk), hd] = (acc / l).astype(BF16)
                qf = q.astype(F32)
                cq = (qf[:, LANE_RB:LANE_RB + 1] + qf[:, LANE_RB + 1:LANE_RB + 2]
                      + qf[:, LANE_RB + 2:LANE_RB + 3])
                hi, mid, lo = _split3(cq - (m + jnp.log(l)))
                qb_ref[pl.ds(q0, blk), hd] = jnp.where(lane == LANE_RB, hi, jnp.where(
                    lane == LANE_RB + 1, mid, jnp.where(lane == LANE_RB + 2, lo, q)))
            return 0

        lax.fori_loop(0, nblk, q_block, 0)

    width = HEADS_PER_STEP * HEAD_PAD

    def head_spec(j):
        return pl.BlockSpec((None, s, width), lambda h: (j, 0, h))

    out_spec = pl.BlockSpec((s, width), lambda h: (0, h))
    return pl.pallas_call(
        body, name="attn_fwd", grid=(HEADS // HEADS_PER_STEP,),
        in_specs=[head_spec(0), head_spec(1), head_spec(2)],
        out_specs=[out_spec, out_spec],
        out_shape=[jax.ShapeDtypeStruct((s, FOX_PAD), BF16),
                   jax.ShapeDtypeStruct((s, FOX_PAD), BF16)],
        compiler_params=_params(),
    )(qkv, qkv, qkv)


def _fox_out_loss(o, gate, w_out, x1, target, gf, ts):
    s = x1.shape[0]

    def body(o_ref, gt_ref, w_ref, x1_ref, t_ref, g_ref, dx2_ref, y2_ref, loss_ref, gfin_ref):
        @pl.when(pl.program_id(0) == 0)
        def _():
            loss_ref[...] = jnp.zeros_like(loss_ref)
            gfin_ref[...] = jnp.zeros_like(gfin_ref)

        gv = gt_ref[...]
        y2 = (o_ref[...] * (gv * _sigmoid(gv))).astype(BF16)
        x2 = x1_ref[...] + _dot(y2, w_ref[...])
        rstd = _rstd(x2)
        xhat = x2 * rstd
        g = g_ref[...]
        diff = xhat * g - t_ref[...]
        loss_ref[...] += 0.5 * jnp.sum(jnp.mean(diff * diff, axis=-1, keepdims=True))
        dy = diff * (1.0 / D_MODEL)
        gfin_ref[...] += jnp.sum(dy * xhat, axis=0, keepdims=True)
        dxh = dy * g
        dx2_ref[...] = rstd * (dxh - xhat * jnp.mean(dxh * xhat, axis=-1, keepdims=True))
        y2_ref[...] = y2

    return pl.pallas_call(
        body, name="fox_out_loss", grid=(s // ts,),
        in_specs=[pl.BlockSpec((ts, FOX_PAD), lambda i: (i, 0)),
                  pl.BlockSpec((ts, FOX_PAD), lambda i: (i, 0)),
                  _const_spec((FOX_PAD, D_MODEL)),
                  pl.BlockSpec((ts, D_MODEL), lambda i: (i, 0)),
                  pl.BlockSpec((ts, D_MODEL), lambda i: (i, 0)),
                  _const_spec((1, D_MODEL))],
        out_specs=[pl.BlockSpec((ts, D_MODEL), lambda i: (i, 0)),
                   pl.BlockSpec((ts, FOX_PAD), lambda i: (i, 0)),
                   pl.BlockSpec((SUBLANES, LANES), lambda i: (0, 0)),
                   pl.BlockSpec((1, D_MODEL), lambda i: (0, 0))],
        out_shape=[jax.ShapeDtypeStruct((s, D_MODEL), F32),
                   jax.ShapeDtypeStruct((s, FOX_PAD), BF16),
                   jax.ShapeDtypeStruct((SUBLANES, LANES), F32),
                   jax.ShapeDtypeStruct((1, D_MODEL), F32)],
        compiler_params=_params(),
    )(o, gate, w_out, x1, target, gf)


def _fox_out_bwd(dx2, w_out, o, gate, ts):
    s = dx2.shape[0]

    def body(dx_ref, w_ref, o_ref, gt_ref, do_ref, dg_ref):
        lane = lax.broadcasted_iota(jnp.int32, (ts, HEAD_PAD), 1)
        dy2 = _dot_nt(dx_ref[...].astype(BF16), w_ref[...])
        gv = gt_ref[...]
        sg = _sigmoid(gv)
        ov = o_ref[...]
        dov = dy2 * (gv * sg)
        dg_ref[...] = (dy2 * ov * (sg * (1.0 + gv * (1.0 - sg)))).astype(BF16)
        prod = dov * ov
        for h in range(HEADS):
            sl = slice(h * HEAD_PAD, (h + 1) * HEAD_PAD)
            delta = jnp.sum(prod[:, sl], axis=-1, keepdims=True)
            hi = delta.astype(BF16)
            lo = (delta - hi.astype(F32)).astype(BF16)
            do_h = dov[:, sl].astype(BF16)
            do_ref[:, sl] = jnp.where(lane == LANE_ONE_V, -hi,
                                      jnp.where(lane == LANE_ONE_V + 1, -lo, do_h))

    tile = pl.BlockSpec((ts, FOX_PAD), lambda i: (i, 0))
    return pl.pallas_call(
        body, name="fox_out_bwd", grid=(s // ts,),
        in_specs=[pl.BlockSpec((ts, D_MODEL), lambda i: (i, 0)),
                  _const_spec((FOX_PAD, D_MODEL)), tile, tile],
        out_specs=[tile, tile],
        out_shape=[jax.ShapeDtypeStruct((s, FOX_PAD), BF16),
                   jax.ShapeDtypeStruct((s, FOX_PAD), BF16)],
        compiler_params=_params(),
    )(dx2, w_out, o, gate)


def _attn_bwd(qb, qkv, do, blk):
    s = qb.shape[0]
    nblk = s // blk
    heads = [slice(i * HEAD_PAD, (i + 1) * HEAD_PAD) for i in range(HEADS_PER_STEP)]

    def body(q_ref, k_ref, v_ref, do_ref, dq_ref, dk_ref, dv_ref, dcum_ref, dq_acc):
        group = pl.program_id(0)
        kj = pl.program_id(1)
        row = lax.broadcasted_iota(jnp.int32, (blk, blk), 0)
        col = lax.broadcasted_iota(jnp.int32, (blk, blk), 1)
        lane = lax.broadcasted_iota(jnp.int32, (blk, LANES), 1)
        mine = [lane == group * HEADS_PER_STEP + i for i in range(HEADS_PER_STEP)]

        @pl.when(kj == 0)
        def _():
            dq_acc[...] = jnp.zeros_like(dq_acc)

        @pl.when((group == 0) & (kj == 0))
        def _():
            dcum_ref[...] = jnp.zeros_like(dcum_ref)

        k0 = pl.multiple_of(kj * blk, blk)
        ks = [k_ref[:, hd] for hd in heads]
        vs = [v_ref[:, hd] for hd in heads]

        def step(q0, carry, masked):
            out = []
            qs = [q_ref[pl.ds(q0, blk), hd] for hd in heads]
            dos = [do_ref[pl.ds(q0, blk), hd] for hd in heads]
            scores = [_dot_nt(q, k) for q, k in zip(qs, ks)]
            dps = [_dot_nt(dov, v) for dov, v in zip(dos, vs)]
            for hd, k, q, dov, sc, dp, (dk, dv) in zip(heads, ks, qs, dos, scores, dps, carry):
                p = jnp.exp(sc)
                if masked:
                    p = jnp.where(col <= row, p, 0.0)
                ds = (p * dp).astype(BF16)
                dv = dv + _dot_tn(p.astype(BF16), dov)
                dk = dk + _dot_tn(ds, q)
                dq_acc[pl.ds(q0, blk), hd] += _dot(ds, k)
                out.append((dk, dv))
            return tuple(out)

        zero = jnp.zeros((blk, HEAD_PAD), F32)
        carry = step(k0, tuple((zero, zero) for _ in heads), True)
        carry = lax.fori_loop(
            kj + 1, nblk, lambda qi, c: step(pl.multiple_of(qi * blk, blk), c, False), carry)
        dcum = dcum_ref[pl.ds(k0, blk), :]
        for hd, mask, (dk, dv) in zip(heads, mine, carry):
            dk_ref[:, hd] = dk.astype(BF16)
            dv_ref[:, hd] = dv.astype(BF16)
            dcum = jnp.where(mask, -dk[:, LANE_CK:LANE_CK + 1], dcum)
        dcum_ref[pl.ds(k0, blk), :] = dcum

        @pl.when(kj == nblk - 1)
        def _():
            def finish(bi, _):
                r0 = pl.multiple_of(bi * blk, blk)
                dcum = dcum_ref[pl.ds(r0, blk), :]
                for hd, mask in zip(heads, mine):
                    dq = dq_acc[pl.ds(r0, blk), hd]
                    dq_ref[pl.ds(r0, blk), hd] = dq.astype(BF16)
                    dcum = dcum + jnp.where(mask, dq[:, LANE_RB:LANE_RB + 1], 0.0)
                dcum_ref[pl.ds(r0, blk), :] = dcum
                return 0

            lax.fori_loop(0, nblk, finish, 0)

    width = HEADS_PER_STEP * HEAD_PAD
    whole = pl.BlockSpec((s, width), lambda h, j: (0, h))
    part = pl.BlockSpec((blk, width), lambda h, j: (j, h))
    out = jax.ShapeDtypeStruct((s, FOX_PAD), BF16)
    return pl.pallas_call(
        body, name="attn_bwd", grid=(HEADS // HEADS_PER_STEP, nblk),
        in_specs=[whole,
                  pl.BlockSpec((None, blk, width), lambda h, j: (1, j, h)),
                  pl.BlockSpec((None, blk, width), lambda h, j: (2, j, h)),
                  whole],
        out_specs=[whole, part, part, pl.BlockSpec((s, LANES), lambda h, j: (0, 0))],
        out_shape=[out, out, out, jax.ShapeDtypeStruct((s, LANES), F32)],
        scratch_shapes=[pltpu.VMEM((s, width), F32)],
        compiler_params=_params(2),
    )(qb, qkv, qkv, do)


def _fox_in_bwd(dq, dk, dv, dg, wt, wft, dcum, f, x1, dx2, g1, ts):
    s = x1.shape[0]
    nt = s // ts

    def body(dq_ref, dk_ref, dv_ref, dg_ref, wt_ref, wft_ref, dcum_ref, f_ref, x1_ref, dx2_ref,
             g_ref, dx1_ref, dx1b_ref, df_ref, gn_ref, gbf_ref, rcar_ref):
        @pl.when(pl.program_id(0) == 0)
        def _():
            rcar_ref[...] = jnp.zeros_like(rcar_ref)
            gn_ref[...] = jnp.zeros_like(gn_ref)
            gbf_ref[...] = jnp.zeros_like(gbf_ref)

        dkv = dk_ref[...]
        rsum = _cumsum_rows(dcum_ref[...], reverse=True) + rcar_ref[0:1, :]
        df = rsum * _sigmoid(-f_ref[...])
        dfb = df.astype(BF16)
        dh = (_dot_nt(dq_ref[...], wt_ref[0]) + _dot_nt(dkv, wt_ref[1])
              + _dot_nt(dv_ref[...], wt_ref[2]) + _dot_nt(dg_ref[...], wt_ref[3])
              + _dot_nt(dfb, wft_ref[...]))
        dxn, dgn = _norm_bwd(x1_ref[...], g_ref[...], dh)
        dx1 = dx2_ref[...] + dxn
        dx1_ref[...] = dx1
        dx1b_ref[...] = dx1.astype(BF16)
        df_ref[...] = dfb
        gn_ref[...] += dgn
        gbf_ref[...] += jnp.sum(df, axis=0, keepdims=True)
        rcar_ref[...] = rsum[0:SUBLANES, :]

    rev = lambda i: (nt - 1 - i, 0)
    wide = pl.BlockSpec((ts, FOX_PAD), rev)
    return pl.pallas_call(
        body, name="fox_in_bwd", grid=(nt,),
        in_specs=[wide, wide, wide, wide,
                  _const_spec((4, D_MODEL, FOX_PAD)),
                  _const_spec((D_MODEL, LANES)),
                  pl.BlockSpec((ts, LANES), rev),
                  pl.BlockSpec((ts, LANES), rev),
                  pl.BlockSpec((ts, D_MODEL), rev),
                  pl.BlockSpec((ts, D_MODEL), rev),
                  _const_spec((1, D_MODEL))],
        out_specs=[pl.BlockSpec((ts, D_MODEL), rev),
                   pl.BlockSpec((ts, D_MODEL), rev),
                   pl.BlockSpec((ts, LANES), rev),
                   pl.BlockSpec((1, D_MODEL), lambda i: (0, 0)),
                   pl.BlockSpec((1, LANES), lambda i: (0, 0))],
        out_shape=[jax.ShapeDtypeStruct((s, D_MODEL), F32),
                   jax.ShapeDtypeStruct((s, D_MODEL), BF16),
                   jax.ShapeDtypeStruct((s, LANES), BF16),
                   jax.ShapeDtypeStruct((1, D_MODEL), F32),
                   jax.ShapeDtypeStruct((1, LANES), F32)],
        scratch_shapes=[pltpu.VMEM((SUBLANES, LANES), F32)],
        compiler_params=_params(),
    )(dq, dk, dv, dg, wt, wft, dcum, f, x1, dx2, g1)


def _lru_core_bwd(dx1b, w_out, xb, gate, hs, cw, cb, wa, ba, wx, bx, a_param, wa_t, wx_t, ts):
    s = xb.shape[0]
    nt = s // ts
    tpb = ts // SUBLANES

    def body(dx_ref, wo_ref, xb_ref, xbh_ref, gate_ref, hs_ref, hsh_ref, cw_ref, cb_ref, wa_ref,
             ba_ref, wx_ref, bx_ref, ap_ref, wat_ref, wxt_ref,
             du_ref, gwa_ref, gwx_ref, gvec_ref, acar_ref, dhcar_ref, dxccar_ref):
        step = pl.program_id(0)

        @pl.when(step == 0)
        def _():
            acar_ref[...] = jnp.zeros_like(acar_ref)
            dhcar_ref[...] = jnp.zeros_like(dhcar_ref)
            dxccar_ref[...] = jnp.zeros_like(dxccar_ref)
            gwa_ref[...] = jnp.zeros_like(gwa_ref)
            gwx_ref[...] = jnp.zeros_like(gwx_ref)
            gvec_ref[...] = jnp.zeros_like(gvec_ref)

        first_tile = step == nt - 1
        halo_on = jnp.where(first_tile, 0.0, 1.0)
        prev8 = xbh_ref[...] * halo_on
        hprev_row = hsh_ref[SUBLANES - 1:SUBLANES, :] * halo_on

        xbv = xb_ref[...]
        taps = _conv_taps(xbv, prev8)
        cw_v = cw_ref[...]
        xc, xcb, r, i, sp, a, mult = _lru_pre(taps, cw_v, cb_ref[...], wa_ref, ba_ref[...],
                                              wx_ref, bx_ref[...], ap_ref[...])
        hs = hs_ref[...]
        gv = gate_ref[...]
        sg = _sigmoid(gv)
        dy = _dot_nt(dx_ref[...], wo_ref[...])
        dhs = dy * (gv * sg)
        dgate = dy * hs * (sg * (1.0 + gv * (1.0 - sg)))

        rows = lax.broadcasted_iota(jnp.int32, a.shape, 0)
        a_next = jnp.where(rows < ts - 1, pltpu.roll(a, ts - 1, 0), acar_ref[0:1, :])
        cum_a, dh_loc = _scan_rows(a_next, dhs, reverse=True)
        dh = cum_a * dhcar_ref[0:1, :] + dh_loc
        h_prev = jnp.where(rows >= 1, pltpu.roll(hs, 1, 0), hprev_row)

        da = dh * h_prev
        ixc = i * xc
        dmult = dh * ixc
        di = dh * mult * xc
        dxc = dh * mult * i
        dlog_a = da * a - dmult * (a * a) / mult
        dr = dlog_a * ((-LRU_C) * sp)
        dsp = jnp.sum(dlog_a * ((-LRU_C) * r), axis=0, keepdims=True)
        dra = dr * r * (1.0 - r)
        dia = di * i * (1.0 - i)
        drab = dra.astype(BF16)
        diab = dia.astype(BF16)
        back = []
        for n in range(LRU_BLOCKS):
            sl = slice(n * LRU_BLOCK_W, (n + 1) * LRU_BLOCK_W)
            gwa_ref[n] += _dot_tn(xcb[:, sl], drab[:, sl])
            gwx_ref[n] += _dot_tn(xcb[:, sl], diab[:, sl])
            back.append(_dot(drab[:, sl], wat_ref[n]) + _dot(diab[:, sl], wxt_ref[n]))
        dxc = dxc + jnp.concatenate(back, axis=1)

        nxt8 = dxccar_ref[...]
        rows8 = lax.broadcasted_iota(jnp.int32, nxt8.shape, 0)
        dxb = cw_v[3:4] * dxc
        for j in range(1, CONV_WIDTH):
            rj = pltpu.roll(dxc, ts - j, 0)
            pj = pltpu.roll(nxt8, SUBLANES - j, 0)
            tail = jnp.where(rows8 >= SUBLANES - j, pj, rj[ts - SUBLANES:])
            dxb = dxb + cw_v[3 - j:4 - j] * jnp.concatenate([rj[:ts - SUBLANES], tail], axis=0)

        du_ref[:, :LRU_WIDTH] = dxb.astype(BF16)
        du_ref[:, LRU_WIDTH:] = dgate.astype(BF16)

        z = -ap_ref[...]
        gvec = [jnp.sum(dxc * taps[3 - k], axis=0, keepdims=True) for k in range(CONV_WIDTH)]
        gvec.append(jnp.sum(dxc, axis=0, keepdims=True))
        gvec.append(jnp.sum(dra, axis=0, keepdims=True))
        gvec.append(jnp.sum(dia, axis=0, keepdims=True))
        gvec.append(-dsp * _sigmoid(z))
        gvec_ref[...] += jnp.concatenate(gvec, axis=0)

        acar_ref[...] = a[0:SUBLANES, :]
        dhcar_ref[...] = dh[0:SUBLANES, :]
        dxccar_ref[...] = dxc[0:SUBLANES, :]

    rev = lambda i: (nt - 1 - i, 0)
    halo = lambda i: (jnp.maximum((nt - 1 - i) * tpb - 1, 0), 0)
    tile = pl.BlockSpec((ts, LRU_WIDTH), rev)
    halo_spec = pl.BlockSpec((SUBLANES, LRU_WIDTH), halo)
    vec = _const_spec((1, LRU_WIDTH))
    blk = _const_spec((LRU_BLOCKS, LRU_BLOCK_W, LRU_BLOCK_W))
    acc_blk = pl.BlockSpec((LRU_BLOCKS, LRU_BLOCK_W, LRU_BLOCK_W), lambda i: (0, 0, 0))
    return pl.pallas_call(
        body, name="lru_core_bwd", grid=(nt,),
        in_specs=[pl.BlockSpec((ts, D_MODEL), rev),
                  _const_spec((LRU_WIDTH, D_MODEL)),
                  tile, halo_spec, tile, tile, halo_spec,
                  _const_spec((CONV_WIDTH, LRU_WIDTH)), vec, blk, vec, blk, vec, vec, blk, blk],
        out_specs=[pl.BlockSpec((ts, 2 * LRU_WIDTH), rev), acc_blk, acc_blk,
                   pl.BlockSpec((SUBLANES, LRU_WIDTH), lambda i: (0, 0))],
        out_shape=[jax.ShapeDtypeStruct((s, 2 * LRU_WIDTH), BF16),
                   jax.ShapeDtypeStruct((LRU_BLOCKS, LRU_BLOCK_W, LRU_BLOCK_W), F32),
                   jax.ShapeDtypeStruct((LRU_BLOCKS, LRU_BLOCK_W, LRU_BLOCK_W), F32),
                   jax.ShapeDtypeStruct((SUBLANES, LRU_WIDTH), F32)],
        scratch_shapes=[pltpu.VMEM((SUBLANES, LRU_WIDTH), F32),
                        pltpu.VMEM((SUBLANES, LRU_WIDTH), F32),
                        pltpu.VMEM((SUBLANES, LRU_WIDTH), F32)],
        compiler_params=_params(),
    )(dx1b, w_out, xb, xb, gate, hs, hs, cw, cb, wa, ba, wx, bx, a_param, wa_t, wx_t)


def _lru_in_bwd(du, w_in, x, dx1, g0, ts):
    s = x.shape[0]

    def body(du_ref, w_ref, x_ref, dx1_ref, g_ref, gx_ref, gn_ref):
        @pl.when(pl.program_id(0) == 0)
        def _():
            gn_ref[...] = jnp.zeros_like(gn_ref)

        duv = du_ref[...]
        dh = _dot_nt(duv[:, 0:LRU_IN_SHARD], w_ref[0])
        for j in range(1, N_DEV):
            dh = dh + _dot_nt(duv[:, j * LRU_IN_SHARD:(j + 1) * LRU_IN_SHARD], w_ref[j])
        dxn, dgn = _norm_bwd(x_ref[...], g_ref[...], dh)
        gx_ref[...] = dx1_ref[...] + dxn
        gn_ref[...] += dgn

    tile = pl.BlockSpec((ts, D_MODEL), lambda i: (i, 0))
    return pl.pallas_call(
        body, name="lru_in_bwd", grid=(s // ts,),
        in_specs=[pl.BlockSpec((ts, 2 * LRU_WIDTH), lambda i: (i, 0)),
                  _const_spec((N_DEV, D_MODEL, LRU_IN_SHARD)), tile, tile,
                  _const_spec((1, D_MODEL))],
        out_specs=[tile, pl.BlockSpec((1, D_MODEL), lambda i: (0, 0))],
        out_shape=[jax.ShapeDtypeStruct((s, D_MODEL), F32),
                   jax.ShapeDtypeStruct((1, D_MODEL), F32)],
        compiler_params=_params(),
    )(du, w_in, x, dx1, g0)


def _weight_grad(a, b, ts, name, scale=1.0, col_shards=1):
    s, ka = a.shape
    nb = b.shape[1]
    nt = s // ts
    per = nb // col_shards

    def body(a_ref, b_ref, o_ref):
        @pl.when(pl.program_id(0) == 0)
        def _():
            o_ref[...] = jnp.zeros_like(o_ref)

        if col_shards == 1:
            o_ref[...] += _dot_tn(a_ref[...], b_ref[...])
        else:
            av, bv = a_ref[...], b_ref[...]
            for j in range(col_shards):
                o_ref[j] += _dot_tn(av, bv[:, j * per:(j + 1) * per])
        if scale != 1.0:
            @pl.when(pl.program_id(0) == nt - 1)
            def _():
                o_ref[...] = o_ref[...] * scale

    out_dims = (ka, nb) if col_shards == 1 else (col_shards, ka, per)
    return pl.pallas_call(
        body, name=name, grid=(nt,),
        in_specs=[pl.BlockSpec((ts, ka), lambda i: (i, 0)),
                  pl.BlockSpec((ts, nb), lambda i: (i, 0))],
        out_specs=pl.BlockSpec(out_dims, lambda i: (0,) * len(out_dims)),
        out_shape=jax.ShapeDtypeStruct(out_dims, F32),
        compiler_params=_params(),
    )(a, b)


def _sum_parts(gp_ref):
    g = gp_ref[0].astype(F32)
    for k in range(1, gp_ref.shape[0]):
        g = g + gp_ref[k].astype(F32)
    return g


def _adamw(g_parts, w, m, v, tr, name):
    nparts, rows, cols = g_parts.shape

    def body(gp_ref, w_ref, m_ref, v_ref, g_ref, d_ref, mo_ref, vo_ref):
        g = _sum_parts(gp_ref)
        m2 = ADAM_B1 * m_ref[...] + (1.0 - ADAM_B1) * g
        v2 = ADAM_B2 * v_ref[...] + (1.0 - ADAM_B2) * (g * g)
        m_hat = m2 / (1.0 - ADAM_B1 ** ADAM_STEP)
        v_hat = v2 / (1.0 - ADAM_B2 ** ADAM_STEP)
        g_ref[...] = g
        d_ref[...] = (-ADAM_LR) * (m_hat / (jnp.sqrt(v_hat) + ADAM_EPS) + ADAM_WD * w_ref[...])
        mo_ref[...] = m2
        vo_ref[...] = v2

    tile = pl.BlockSpec((tr, cols), lambda i: (i, 0))
    out = jax.ShapeDtypeStruct((rows, cols), F32)
    return pl.pallas_call(
        body, name=name, grid=(rows // tr,),
        in_specs=[pl.BlockSpec((nparts, tr, cols), lambda i: (0, i, 0)), tile, tile, tile],
        out_specs=[tile, tile, tile, tile],
        out_shape=[out, out, out, out],
        compiler_params=_params(),
    )(g_parts, w, m, v)


def _reduce_parts(g_parts, name):
    _, rows, cols = g_parts.shape

    def body(gp_ref, g_ref):
        g_ref[...] = _sum_parts(gp_ref)

    return pl.pallas_call(
        body, name=name,
        out_shape=jax.ShapeDtypeStruct((rows, cols), F32),
        compiler_params=pltpu.CompilerParams(vmem_limit_bytes=VMEM_LIMIT_BYTES),
    )(g_parts)


def _mesh_pos():
    ix, iy, ic = lax.axis_index("x"), lax.axis_index("y"), lax.axis_index("c")
    return ix, iy, ic


def _peer(ix, iy, ic, mask):
    px = 1 - ix if mask & 4 else ix
    py = 1 - iy if mask & 2 else iy
    pc = 1 - ic if mask & 1 else ic
    return (px, py, pc), 4 * px + 2 * py + pc


def _exchange(arrays, scatter, name):
    n = len(arrays)

    def body(*refs):
        x_refs, o_refs = refs[:n], refs[n:2 * n]
        send_sems, recv_sems, local_sems = refs[2 * n:]
        ix, iy, ic = _mesh_pos()
        me = 4 * ix + 2 * iy + ic

        def src(a, dest):
            return x_refs[a].at[dest] if scatter else x_refs[a]

        local = [pltpu.make_async_copy(src(a, me), o_refs[a].at[me], local_sems.at[a])
                 for a in range(n)]
        for cp in local:
            cp.start()
        sends = []
        for mask in range(1, N_DEV):
            peer, pidx = _peer(ix, iy, ic, mask)
            for a in range(n):
                cp = pltpu.make_async_remote_copy(
                    src_ref=src(a, pidx), dst_ref=o_refs[a].at[me],
                    send_sem=send_sems.at[a, mask - 1], recv_sem=recv_sems.at[a, mask - 1],
                    device_id=peer, device_id_type=pl.DeviceIdType.MESH)
                cp.start()
                sends.append(cp)
        for mask in range(1, N_DEV):
            peer, pidx = _peer(ix, iy, ic, mask)
            for a in range(n):
                pltpu.make_async_remote_copy(
                    src_ref=src(a, me), dst_ref=o_refs[a].at[pidx],
                    send_sem=send_sems.at[a, mask - 1], recv_sem=recv_sems.at[a, mask - 1],
                    device_id=peer, device_id_type=pl.DeviceIdType.MESH).wait_recv()
        for cp in sends:
            cp.wait_send()
        for cp in local:
            cp.wait()

    out_shape = [jax.ShapeDtypeStruct(x.shape if scatter else (N_DEV,) + x.shape, x.dtype)
                 for x in arrays]
    return pl.pallas_call(
        body, name=name,
        in_specs=[pl.BlockSpec(memory_space=pl.ANY)] * n,
        out_specs=[pl.BlockSpec(memory_space=pl.ANY)] * n,
        out_shape=out_shape,
        scratch_shapes=[pltpu.SemaphoreType.DMA((n, N_DEV - 1)),
                        pltpu.SemaphoreType.DMA((n, N_DEV - 1)),
                        pltpu.SemaphoreType.DMA((n,))],
    )(*arrays)


def _gather_two_level(arrays, name):
    n = len(arrays)

    def body(*refs):
        x_refs, o_refs = refs[:n], refs[n:2 * n]
        send_sems, recv_sems, local_sems = refs[2 * n:]
        ix, iy, ic = _mesh_pos()
        me, sibling = (ix, iy, ic), (ix, iy, 1 - ic)
        chips = [(1 - ix, iy), (ix, 1 - iy), (1 - ix, 1 - iy)]

        def idx(px, py, pc):
            return 4 * px + 2 * py + pc

        def copy(a, k, block, to, src=None):
            dst = o_refs[a].at[idx(*block)]
            return pltpu.make_async_remote_copy(
                src_ref=dst if src is None else src, dst_ref=dst,
                send_sem=send_sems.at[a, k], recv_sem=recv_sems.at[a, k],
                device_id=to, device_id_type=pl.DeviceIdType.MESH)

        local = [pltpu.make_async_copy(x_refs[a], o_refs[a].at[idx(*me)], local_sems.at[a])
                 for a in range(n)]
        for cp in local:
            cp.start()
        first = []
        for a in range(n):
            first.append(copy(a, 0, me, sibling, src=x_refs[a]))
            first += [copy(a, 1 + j, me, (*chip, ic), src=x_refs[a])
                      for j, chip in enumerate(chips)]
        for cp in first:
            cp.start()
        passed = []
        for j, chip in enumerate(chips):
            for a in range(n):
                copy(a, 1 + j, (*chip, ic), me).wait_recv()
                cp = copy(a, 4 + j, (*chip, ic), sibling)
                cp.start()
                passed.append(cp)
        for a in range(n):
            copy(a, 0, sibling, me).wait_recv()
            for j, chip in enumerate(chips):
                copy(a, 4 + j, (*chip, 1 - ic), me).wait_recv()
        for cp in first + passed:
            cp.wait_send()
        for cp in local:
            cp.wait()

    return pl.pallas_call(
        body, name=name,
        in_specs=[pl.BlockSpec(memory_space=pl.ANY)] * n,
        out_specs=[pl.BlockSpec(memory_space=pl.ANY)] * n,
        out_shape=[jax.ShapeDtypeStruct((N_DEV,) + x.shape, x.dtype) for x in arrays],
        scratch_shapes=[pltpu.SemaphoreType.DMA((n, N_DEV - 1)),
                        pltpu.SemaphoreType.DMA((n, N_DEV - 1)),
                        pltpu.SemaphoreType.DMA((n,))],
    )(*arrays)


def _swap_sibling(arrays, name):
    n = len(arrays)
    n_chips = N_DEV // 2

    def body(*refs):
        x_refs, own_refs, got_refs = refs[:n], refs[n:2 * n], refs[2 * n:3 * n]
        send_sems, recv_sems, local_sems = refs[3 * n:]
        ix, iy, ic = _mesh_pos()
        sibling = (ix, iy, 1 - ic)
        local, sends = [], []
        for a in range(n):
            for q in range(n_chips):
                cp = pltpu.make_async_copy(x_refs[a].at[q, ic], own_refs[a].at[q],
                                           local_sems.at[a, q])
                cp.start()
                local.append(cp)
                cp = pltpu.make_async_remote_copy(
                    src_ref=x_refs[a].at[q, 1 - ic], dst_ref=got_refs[a].at[q],
                    send_sem=send_sems.at[a, q], recv_sem=recv_sems.at[a, q],
                    device_id=sibling, device_id_type=pl.DeviceIdType.MESH)
                cp.start()
                sends.append(cp)
        for cp in sends:
            cp.wait()
        for cp in local:
            cp.wait()

    half = [jax.ShapeDtypeStruct((n_chips,) + x.shape[2:], x.dtype) for x in arrays]
    res = pl.pallas_call(
        body, name=name,
        in_specs=[pl.BlockSpec(memory_space=pl.ANY)] * n,
        out_specs=[pl.BlockSpec(memory_space=pl.ANY)] * (2 * n),
        out_shape=half + half,
        scratch_shapes=[pltpu.SemaphoreType.DMA((n, n_chips)),
                        pltpu.SemaphoreType.DMA((n, n_chips)),
                        pltpu.SemaphoreType.DMA((n, n_chips))],
    )(*arrays)
    return res[:n], res[n:]


def _exchange_chips(arrays, name):
    n = len(arrays)
    n_chips = N_DEV // 2

    def body(*refs):
        x_refs, o_refs = refs[:n], refs[n:2 * n]
        send_sems, recv_sems, local_sems = refs[2 * n:]
        ix, iy, ic = _mesh_pos()
        my_chip = 2 * ix + iy
        local = [pltpu.make_async_copy(x_refs[a].at[my_chip], o_refs[a].at[my_chip],
                                       local_sems.at[a]) for a in range(n)]
        for cp in local:
            cp.start()
        sends = []
        for mask in range(1, n_chips):
            px = 1 - ix if mask & 2 else ix
            py = 1 - iy if mask & 1 else iy
            for a in range(n):
                cp = pltpu.make_async_remote_copy(
                    src_ref=x_refs[a].at[2 * px + py], dst_ref=o_refs[a].at[my_chip],
                    send_sem=send_sems.at[a, mask - 1], recv_sem=recv_sems.at[a, mask - 1],
                    device_id=(px, py, ic), device_id_type=pl.DeviceIdType.MESH)
                cp.start()
                sends.append(cp)
        for mask in range(1, n_chips):
            px = 1 - ix if mask & 2 else ix
            py = 1 - iy if mask & 1 else iy
            for a in range(n):
                pltpu.make_async_remote_copy(
                    src_ref=x_refs[a].at[my_chip], dst_ref=o_refs[a].at[2 * px + py],
                    send_sem=send_sems.at[a, mask - 1], recv_sem=recv_sems.at[a, mask - 1],
                    device_id=(px, py, ic), device_id_type=pl.DeviceIdType.MESH).wait_recv()
        for cp in sends:
            cp.wait_send()
        for cp in local:
            cp.wait()

    return pl.pallas_call(
        body, name=name,
        in_specs=[pl.BlockSpec(memory_space=pl.ANY)] * n,
        out_specs=[pl.BlockSpec(memory_space=pl.ANY)] * n,
        out_shape=[jax.ShapeDtypeStruct(x.shape, x.dtype) for x in arrays],
        scratch_shapes=[pltpu.SemaphoreType.DMA((n, n_chips - 1)),
                        pltpu.SemaphoreType.DMA((n, n_chips - 1)),
                        pltpu.SemaphoreType.DMA((n,))],
    )(*arrays)


def _pair_sum(a, b, name):
    nq, rows, cols = a.shape

    def body(a_ref, b_ref, o_ref):
        o_ref[...] = (a_ref[...] + b_ref[...]).astype(BF16)

    blk = pl.BlockSpec((None, rows, cols), lambda q: (q, 0, 0))
    return pl.pallas_call(
        body, name=name, grid=(nq,),
        in_specs=[blk, blk], out_specs=blk,
        out_shape=jax.ShapeDtypeStruct(a.shape, BF16),
        compiler_params=_params(),
    )(a, b)


def _pad_heads_cols(w):
    k = w.shape[0]
    w = w.reshape(k, HEADS, HEAD_DIM)
    return jnp.pad(w, ((0, 0), (0, 0), (0, HEAD_PAD - HEAD_DIM))).reshape(k, FOX_PAD)


def _unpad_heads_cols(w):
    k = w.shape[0]
    return w.reshape(k, HEADS, HEAD_PAD)[:, :, :HEAD_DIM].reshape(k, HEADS * HEAD_DIM)


def _selectors():
    r = lax.broadcasted_iota(jnp.int32, (3 * LANES, FOX_PAD), 0)
    c = lax.broadcasted_iota(jnp.int32, (3 * LANES, FOX_PAD), 1)
    part, head_r = r // LANES, r % LANES
    head_c, lane_c = c // HEAD_PAD, c % HEAD_PAD
    same = (head_r == head_c) & (head_r < HEADS)
    sel_q = jnp.where(same & (lane_c == LANE_RB + part), 1.0, 0.0)
    sel_k = jnp.where(same & (lane_c == LANE_CK + part), -1.0, 0.0)
    sel = jnp.stack([sel_q, sel_k, jnp.zeros_like(sel_q)]).astype(BF16)
    lane = lax.broadcasted_iota(jnp.int32, (1, FOX_PAD), 1) % HEAD_PAD
    ones_q = jnp.where((lane >= LANE_CK) & (lane < LANE_CK + 3), 1.0, 0.0)
    ones_k = jnp.where((lane >= LANE_RB) & (lane < LANE_RB + 3), 1.0, 0.0)
    ones_v = jnp.where((lane >= LANE_ONE_V) & (lane < LANE_ONE_V + 2), 1.0, 0.0)
    bias = jnp.stack([ones_q, ones_k, ones_v]).astype(F32)
    return sel, bias


def _local_step(x, target, norm_g, final_g, w_in8, conv_w, conv_b, wa, ba, wx, bx, a_param,
                w_out_b, fox_in8, b_f, fox_out_b, blk=512, ts=256):
    qk_scale = 1.0 / (HEAD_DIM ** 0.5)
    g0, g1 = norm_g[0:1], norm_g[1:2]
    gf = final_g.reshape(1, D_MODEL)
    wa_b, wx_b = wa.astype(BF16), wx.astype(BF16)
    fox_w_in = jnp.transpose(fox_in8, (1, 0, 2)).reshape(D_MODEL, FOX_IN_COLS)
    wq = _pad_heads_cols(fox_w_in[:, 0:1024]) * qk_scale
    wk = _pad_heads_cols(fox_w_in[:, 1024:2048])
    wv = _pad_heads_cols(fox_w_in[:, 2048:3072])
    wg = _pad_heads_cols(fox_w_in[:, 3072:4096])
    wf_b = jnp.pad(fox_w_in[:, 4096:], ((0, 0), (0, LANES - HEADS)))
    w4 = jnp.stack([wq, wk, wv, wg]).astype(BF16)
    bf_pad = jnp.pad(b_f, ((0, 0), (0, LANES - HEADS)))
    fo = fox_out_b.reshape(HEADS, HEAD_DIM, D_MODEL)
    fo_b = jnp.pad(fo, ((0, 0), (0, HEAD_PAD - HEAD_DIM), (0, 0))).reshape(FOX_PAD, D_MODEL)
    sel, bias = _selectors()

    xb, gate1, h0 = _lru_in_fwd(x, g0, w_in8, ts)
    y1, hs = _lru_core_fwd(xb, gate1, conv_w, conv_b, wa_b, ba, wx_b, bx, a_param, ts)
    x1, h1, f, cparts = _fox_pre_fwd(x, y1, w_out_b, g1, wf_b, bf_pad, ts)
    qkv = _fox_proj_fwd(h1, cparts, w4[0:3], sel, bias, BF16, ts, "fox_proj_qkv")
    gate2 = _fox_proj_fwd(h1, None, w4[3:4], None, None, F32, ts, "fox_proj_gate")[0]
    o, qb = _attn_fwd(qkv, blk)
    dx2, y2, loss_acc, g_final = _fox_out_loss(o, gate2, fo_b, x1, target, gf, ts)

    do, dgate2 = _fox_out_bwd(dx2, fo_b, o, gate2, ts)
    dq, dk, dv, dcum = _attn_bwd(qb, qkv, do, blk)
    dx1, dx1b, df, g_norm1, g_bf = _fox_in_bwd(dq, dk, dv, dgate2, w4, wf_b, dcum, f, x1, dx2,
                                               g1, ts)
    du, g_wa, g_wx, g_vec = _lru_core_bwd(dx1b, w_out_b, xb, gate1, hs, conv_w, conv_b, wa_b, ba,
                                          wx_b, bx, a_param, jnp.transpose(wa_b, (0, 2, 1)),
                                          jnp.transpose(wx_b, (0, 2, 1)), ts)
    grad_x, g_norm0 = _lru_in_bwd(du, w_in8, x, dx1, g0, ts)

    tw = 512
    g_lru_w_in = _weight_grad(h0, du, tw, "grad_lru_w_in", col_shards=N_DEV)
    g_lru_w_out = _weight_grad(y1, dx1b, tw, "grad_lru_w_out")
    g_q = _weight_grad(h1, dq, tw, "grad_fox_wq", scale=qk_scale)
    g_k = _weight_grad(h1, dk, tw, "grad_fox_wk")
    g_v = _weight_grad(h1, dv, tw, "grad_fox_wv")
    g_g = _weight_grad(h1, dgate2, tw, "grad_fox_wg")
    g_f = _weight_grad(h1, df, tw, "grad_fox_wf")
    g_fox_w_in = jnp.concatenate(
        [_unpad_heads_cols(g_q), _unpad_heads_cols(g_k), _unpad_heads_cols(g_v),
         _unpad_heads_cols(g_g), g_f[:, :HEADS]], axis=1)
    g_fox_w_in = jnp.transpose(g_fox_w_in.reshape(D_MODEL, N_DEV, FOX_IN_SHARD), (1, 0, 2))
    g_fo = _weight_grad(y2, dx2.astype(BF16), tw, "grad_fox_w_out")
    g_fox_w_out = g_fo.reshape(HEADS, HEAD_PAD, D_MODEL)[:, :HEAD_DIM].reshape(
        HEADS * HEAD_DIM, D_MODEL)

    grads = dict(
        norm_g=jnp.concatenate([g_norm0, g_norm1], axis=0), final_g=g_final[0],
        lru_w_in=g_lru_w_in, lru_conv_w=g_vec[0:4], lru_conv_b=g_vec[4:5], lru_wa=g_wa,
        lru_ba=g_vec[5:6], lru_wx=g_wx, lru_bx=g_vec[6:7], lru_a_param=g_vec[7:8],
        lru_w_out=g_lru_w_out, fox_w_in=g_fox_w_in, fox_b_f=g_bf[:, :HEADS],
        fox_w_out=g_fox_w_out)
    return loss_acc[0, 0], grad_x, grads


SMALL =("norm_g", "final_g", "lru_conv_b", "lru_wa", "lru_ba", "lru_wx", "lru_bx", "lru_a_param",
         "fox_b_f")
ALL_WEIGHTS = ("norm_g", "final_g", "lru_w_in", "lru_conv_w", "lru_conv_b", "lru_wa", "lru_ba",
               "lru_wx", "lru_bx", "lru_a_param", "lru_w_out", "fox_w_in", "fox_b_f", "fox_w_out")


def _pack_small(d):
    rows = []
    for n in SMALL:
        a = d[n].reshape(-1)
        if a.shape[0] % LANES:
            a = jnp.pad(a, (0, LANES - a.shape[0] % LANES))
        rows.append(a.reshape(-1, LANES))
    packed = jnp.concatenate(rows, axis=0)
    return jnp.pad(packed, ((0, N_DEV * SMALL_CHUNK_ROWS - packed.shape[0]), (0, 0)))


def _unpack_small(packed, like):
    out, off = {}, 0
    for n, nrows in zip(SMALL, SMALL_ROWS):
        size = like[n].size
        out[n] = packed[off:off + nrows].reshape(-1)[:size].reshape(like[n].shape)
        off += nrows
    return out


def kernel(x, norm_g, final_g, lru_w_in, lru_conv_w, lru_conv_b, lru_wa, lru_ba, lru_wx, lru_bx, lru_a_param, lru_w_out, fox_w_in, fox_b_f, fox_w_out, loss_target, m_norm_g, m_final_g, m_lru_w_in, m_lru_conv_w, m_lru_conv_b, m_lru_wa, m_lru_ba, m_lru_wx, m_lru_bx, m_lru_a_param, m_lru_w_out, m_fox_w_in, m_fox_b_f, m_fox_w_out, v_norm_g, v_final_g, v_lru_w_in, v_lru_conv_w, v_lru_conv_b, v_lru_wa, v_lru_ba, v_lru_wx, v_lru_bx, v_lru_a_param, v_lru_w_out, v_fox_w_in, v_fox_b_f, v_fox_w_out):
    w_loc = dict(norm_g=norm_g, final_g=final_g, lru_w_in=lru_w_in, lru_conv_w=lru_conv_w,
                 lru_conv_b=lru_conv_b, lru_wa=lru_wa, lru_ba=lru_ba, lru_wx=lru_wx, lru_bx=lru_bx,
                 lru_a_param=lru_a_param, lru_w_out=lru_w_out, fox_w_in=fox_w_in, fox_b_f=fox_b_f,
                 fox_w_out=fox_w_out)
    m_loc = dict(norm_g=m_norm_g, final_g=m_final_g, lru_w_in=m_lru_w_in, lru_conv_w=m_lru_conv_w,
                 lru_conv_b=m_lru_conv_b, lru_wa=m_lru_wa, lru_ba=m_lru_ba, lru_wx=m_lru_wx,
                 lru_bx=m_lru_bx, lru_a_param=m_lru_a_param, lru_w_out=m_lru_w_out,
                 fox_w_in=m_fox_w_in, fox_b_f=m_fox_b_f, fox_w_out=m_fox_w_out)
    v_loc = dict(norm_g=v_norm_g, final_g=v_final_g, lru_w_in=v_lru_w_in, lru_conv_w=v_lru_conv_w,
                 lru_conv_b=v_lru_conv_b, lru_wa=v_lru_wa, lru_ba=v_lru_ba, lru_wx=v_lru_wx,
                 lru_bx=v_lru_bx, lru_a_param=v_lru_a_param, lru_w_out=v_lru_w_out,
                 fox_w_in=v_fox_w_in, fox_b_f=v_fox_b_f, fox_w_out=v_fox_w_out)

    w_in8, conv8, w_out8, fox_in8, fox_out8 = _gather_two_level(
        [lru_w_in[0].astype(BF16), lru_conv_w[0], lru_w_out[0].astype(BF16),
         fox_w_in[0].astype(BF16), fox_w_out[0].astype(BF16)], "gather_weights")
    conv_full = jnp.transpose(conv8, (1, 0, 2)).reshape(CONV_WIDTH, LRU_WIDTH)

    loss, grad_x, grads = _local_step(
        x[0], loss_target[0], norm_g, final_g, w_in8, conv_full, lru_conv_b, lru_wa[0], lru_ba,
        lru_wx[0], lru_bx, lru_a_param, w_out8.reshape(LRU_WIDTH, D_MODEL), fox_in8, fox_b_f,
        fox_out8.reshape(HEADS * HEAD_DIM, D_MODEL))

    n_chips = N_DEV // 2
    conv_send = jnp.transpose(grads["lru_conv_w"].reshape(CONV_WIDTH, N_DEV, -1), (1, 0, 2))
    names = ("lru_w_in", "lru_conv_w", "lru_w_out", "fox_w_in", "fox_w_out", "small")
    send = [grads["lru_w_in"], conv_send, grads["lru_w_out"].reshape(N_DEV, -1, D_MODEL),
            grads["fox_w_in"], grads["fox_w_out"].reshape(N_DEV, -1, D_MODEL),
            _pack_small(grads).reshape(N_DEV, SMALL_CHUNK_ROWS, LANES)]
    own, got = _swap_sibling([a.reshape((n_chips, 2) + a.shape[1:]) for a in send], "swap_grads")
    chip_sums = [_pair_sum(a, b, "pair_sum_" + n) for n, a, b in zip(names, own, got)]
    r_w_in, r_conv, r_w_out, r_fox_in, r_fox_out, r_small = _exchange_chips(
        chip_sums, "scatter_grads")

    out = {}
    for n, recv, tr in (("lru_w_in", r_w_in, 256), ("lru_conv_w", r_conv, CONV_WIDTH),
                        ("lru_w_out", r_w_out, 96), ("fox_w_in", r_fox_in, 128),
                        ("fox_w_out", r_fox_out, 64)):
        res = _adamw(recv, w_loc[n][0], m_loc[n][0], v_loc[n][0], tr, "adamw_" + n)
        out[n] = [a[None] for a in res]

    g_chunk = _reduce_parts(r_small, "reduce_small_grads")
    g_small, = _exchange([g_chunk], False, "gather_small_grads")
    g_small = g_small.reshape(1, N_DEV * SMALL_CHUNK_ROWS, LANES)
    res = _adamw(g_small, _pack_small(w_loc), _pack_small(m_loc), _pack_small(v_loc),
                 N_DEV * SMALL_CHUNK_ROWS, "adamw_replicated")
    small_out = [_unpack_small(a, w_loc) for a in res]
    for n in SMALL:
        out[n] = [d[n] for d in small_out]

    loss = lax.psum(loss, ("x", "y", "c"))
    return (loss, grad_x[None], *[out[n][0] for n in ALL_WEIGHTS], *[out[n][1] for n in ALL_WEIGHTS],
            *[out[n][2] for n in ALL_WEIGHTS], *[out[n][3] for n in ALL_WEIGHTS])
```

```python
import functools

import jax
import jax.numpy as jnp
from jax import lax
from jax.experimental import pallas as pl
from jax.experimental.pallas import tpu as pltpu

F32 = jnp.float32
BF16 = jnp.bfloat16

D_MODEL = 1024
LRU_WIDTH = 1536
LRU_BLOCKS = 12
LRU_BLOCK_W = 128
CONV_WIDTH = 4
LRU_C = 8.0
HEADS = 16
HEAD_DIM = 64
HEAD_PAD = 128
FOX_PAD = HEADS * HEAD_PAD
HEADS_PER_STEP = 2
EPS = 1e-6
NEG_BIG = -1e30
N_DEV = 8

ADAM_LR = 0.001
ADAM_B1 = 0.9
ADAM_B2 = 0.999
ADAM_EPS = 1e-08
ADAM_WD = 0.01
ADAM_STEP = 10

LANE_RB = 64
LANE_CK = 67
LANE_ONE_V = 64

VMEM_LIMIT_BYTES = 56 * 1024 * 1024
LANES = 128
SUBLANES = 8

LRU_IN_SHARD = 2 * LRU_WIDTH // N_DEV
FOX_IN_COLS = 4 * HEADS * HEAD_DIM + HEADS
FOX_IN_SHARD = FOX_IN_COLS // N_DEV

SMALL_ROWS = (16, 8, 12, 1536, 12, 1536, 12, 12, 1)
SMALL_CHUNK_ROWS = 400
assert sum(SMALL_ROWS) <= N_DEV * SMALL_CHUNK_ROWS


def _params(n_grid_axes=1):
    return pltpu.CompilerParams(
        dimension_semantics=("arbitrary",) * n_grid_axes,
        vmem_limit_bytes=VMEM_LIMIT_BYTES)


def _const_spec(shape):
    nd = len(shape)
    return pl.BlockSpec(shape, lambda *_: (0,) * nd, pipeline_mode=pl.Buffered(1))


def _shift_down(x, k, fill):
    rows = lax.broadcasted_iota(jnp.int32, x.shape, 0)
    return jnp.where(rows >= k, pltpu.roll(x, k, 0), fill)


def _shift_up(x, k, fill):
    n = x.shape[0]
    rows = lax.broadcasted_iota(jnp.int32, x.shape, 0)
    return jnp.where(rows < n - k, pltpu.roll(x, n - k, 0), fill)


def _scan_rows(a, b, reverse=False):
    n = a.shape[0]
    shift = _shift_up if reverse else _shift_down
    k = 1
    while k < n:
        b = a * shift(b, k, 0.0) + b
        a = a * shift(a, k, 1.0)
        k *= 2
    return a, b


def _cumsum_rows(x, reverse=False):
    n = x.shape[0]
    shift = _shift_up if reverse else _shift_down
    k = 1
    while k < n:
        x = x + shift(x, k, 0.0)
        k *= 2
    return x


def _rstd(x):
    return lax.rsqrt(jnp.mean(x * x, axis=-1, keepdims=True) + EPS)


def _norm_bwd(x, g, dh):
    rstd = _rstd(x)
    xhat = x * rstd
    dg = jnp.sum(dh * xhat, axis=0, keepdims=True)
    dxh = dh * g
    dx = rstd * (dxh - xhat * jnp.mean(dxh * xhat, axis=-1, keepdims=True))
    return dx, dg


def _split3(x):
    hi = x.astype(BF16)
    r1 = x - hi.astype(F32)
    mid = r1.astype(BF16)
    lo = (r1 - mid.astype(F32)).astype(BF16)
    return hi, mid, lo


def _sigmoid(x):
    return jax.nn.sigmoid(x)


def _dot(a, b):
    return jnp.dot(a, b, preferred_element_type=F32)


def _dot_nt(a, b):
    return lax.dot_general(a, b, (((1,), (1,)), ((), ())), preferred_element_type=F32)


def _dot_tn(a, b):
    return lax.dot_general(a, b, (((0,), (0,)), ((), ())), preferred_element_type=F32)


def _conv_taps(xb, prev8):
    rows8 = lax.broadcasted_iota(jnp.int32, prev8.shape, 0)
    taps = [xb]
    for j in range(1, CONV_WIDTH):
        r = pltpu.roll(xb, j, 0)
        p = pltpu.roll(prev8, j, 0)
        head = jnp.where(rows8 < j, p, r[0:SUBLANES])
        taps.append(jnp.concatenate([head, r[SUBLANES:]], axis=0))
    return taps


def _lru_pre(taps, cw, cb, wa_ref, ba, wx_ref, bx, a_param):
    xc = cb + cw[3:4] * taps[0] + cw[2:3] * taps[1] + cw[1:2] * taps[2] + cw[0:1] * taps[3]
    xcb = xc.astype(BF16)
    ra, ia = [], []
    for n in range(LRU_BLOCKS):
        blk = xcb[:, n * LRU_BLOCK_W:(n + 1) * LRU_BLOCK_W]
        ra.append(_dot(blk, wa_ref[n]))
        ia.append(_dot(blk, wx_ref[n]))
    r = _sigmoid(jnp.concatenate(ra, axis=1) + ba)
    i = _sigmoid(jnp.concatenate(ia, axis=1) + bx)
    z = -a_param
    sp = jnp.maximum(z, 0.0) + jnp.log1p(jnp.exp(-jnp.abs(z)))
    log_a = (-LRU_C) * r * sp
    a = jnp.exp(log_a)
    one_minus_a2 = -jnp.tanh(log_a) * (a * a + 1.0)
    mult = jnp.sqrt(one_minus_a2)
    return xc, xcb, r, i, sp, a, mult


def _lru_in_fwd(x, g0, w_in, ts):
    s = x.shape[0]
    half = N_DEV // 2

    def body(x_ref, g_ref, w_ref, xb_ref, gate_ref, h_ref):
        xv = x_ref[...]
        h = (xv * _rstd(xv) * g_ref[...]).astype(BF16)
        u = [_dot(h, w_ref[j]) for j in range(N_DEV)]
        xb_ref[...] = jnp.concatenate(u[:half], axis=1)
        gate_ref[...] = jnp.concatenate(u[half:], axis=1)
        h_ref[...] = h

    return pl.pallas_call(
        body, name="lru_in_fwd", grid=(s // ts,),
        in_specs=[pl.BlockSpec((ts, D_MODEL), lambda i: (i, 0)),
                  _const_spec((1, D_MODEL)),
                  _const_spec((N_DEV, D_MODEL, LRU_IN_SHARD))],
        out_specs=[pl.BlockSpec((ts, LRU_WIDTH), lambda i: (i, 0)),
                   pl.BlockSpec((ts, LRU_WIDTH), lambda i: (i, 0)),
                   pl.BlockSpec((ts, D_MODEL), lambda i: (i, 0))],
        out_shape=[jax.ShapeDtypeStruct((s, LRU_WIDTH), F32),
                   jax.ShapeDtypeStruct((s, LRU_WIDTH), F32),
                   jax.ShapeDtypeStruct((s, D_MODEL), BF16)],
        compiler_params=_params(),
    )(x, g0, w_in)


def _lru_core_fwd(xb, gate, cw, cb, wa, ba, wx, bx, a_param, ts):
    s = xb.shape[0]

    def body(xb_ref, gate_ref, cw_ref, cb_ref, wa_ref, ba_ref, wx_ref, bx_ref, ap_ref,
             y_ref, hs_ref, prev_ref, hcar_ref):
        @pl.when(pl.program_id(0) == 0)
        def _():
            prev_ref[...] = jnp.zeros_like(prev_ref)
            hcar_ref[...] = jnp.zeros_like(hcar_ref)

        xbv = xb_ref[...]
        taps = _conv_taps(xbv, prev_ref[...])
        xc, _, _, i, _, a, mult = _lru_pre(taps, cw_ref[...], cb_ref[...], wa_ref, ba_ref[...],
                                           wx_ref, bx_ref[...], ap_ref[...])
        bterm = mult * (i * xc)
        cum_a, hloc = _scan_rows(a, bterm)
        hs = cum_a * hcar_ref[SUBLANES - 1:SUBLANES, :] + hloc
        gv = gate_ref[...]
        y_ref[...] = (hs * (gv * _sigmoid(gv))).astype(BF16)
        hs_ref[...] = hs
        prev_ref[...] = xbv[ts - SUBLANES:, :]
        hcar_ref[...] = hs[ts - SUBLANES:, :]

    vec = _const_spec((1, LRU_WIDTH))
    blk = _const_spec((LRU_BLOCKS, LRU_BLOCK_W, LRU_BLOCK_W))
    tile = pl.BlockSpec((ts, LRU_WIDTH), lambda i: (i, 0))
    return pl.pallas_call(
        body, name="lru_core_fwd", grid=(s // ts,),
        in_specs=[tile, tile, _const_spec((CONV_WIDTH, LRU_WIDTH)), vec, blk, vec, blk, vec, vec],
        out_specs=[tile, tile],
        out_shape=[jax.ShapeDtypeStruct((s, LRU_WIDTH), BF16),
                   jax.ShapeDtypeStruct((s, LRU_WIDTH), F32)],
        scratch_shapes=[pltpu.VMEM((SUBLANES, LRU_WIDTH), F32),
                        pltpu.VMEM((SUBLANES, LRU_WIDTH), F32)],
        compiler_params=_params(),
    )(xb, gate, cw, cb, wa, ba, wx, bx, a_param)


def _fox_pre_fwd(x, y, w_out, g1, wf, bf, ts):
    s = x.shape[0]

    def body(x_ref, y_ref, w_ref, g_ref, wf_ref, bf_ref, x1_ref, h1_ref, f_ref, cp_ref, ccar_ref):
        @pl.when(pl.program_id(0) == 0)
        def _():
            ccar_ref[...] = jnp.zeros_like(ccar_ref)

        x1 = x_ref[...] + _dot(y_ref[...], w_ref[...])
        h1 = (x1 * _rstd(x1) * g_ref[...]).astype(BF16)
        f = _dot(h1, wf_ref[...]) + bf_ref[...]
        logsig = jnp.minimum(f, 0.0) - jnp.log1p(jnp.exp(-jnp.abs(f)))
        cum = _cumsum_rows(logsig) + ccar_ref[SUBLANES - 1:SUBLANES, :]
        hi, mid, lo = _split3(cum)
        x1_ref[...] = x1
        h1_ref[...] = h1
        f_ref[...] = f
        cp_ref[...] = jnp.concatenate([hi, mid, lo], axis=1)
        ccar_ref[...] = cum[ts - SUBLANES:, :]

    return pl.pallas_call(
        body, name="fox_pre_fwd", grid=(s // ts,),
        in_specs=[pl.BlockSpec((ts, D_MODEL), lambda i: (i, 0)),
                  pl.BlockSpec((ts, LRU_WIDTH), lambda i: (i, 0)),
                  _const_spec((LRU_WIDTH, D_MODEL)),
                  _const_spec((1, D_MODEL)),
                  _const_spec((D_MODEL, LANES)),
                  _const_spec((1, LANES))],
        out_specs=[pl.BlockSpec((ts, D_MODEL), lambda i: (i, 0)),
                   pl.BlockSpec((ts, D_MODEL), lambda i: (i, 0)),
                   pl.BlockSpec((ts, LANES), lambda i: (i, 0)),
                   pl.BlockSpec((ts, 3 * LANES), lambda i: (i, 0))],
        out_shape=[jax.ShapeDtypeStruct((s, D_MODEL), F32),
                   jax.ShapeDtypeStruct((s, D_MODEL), BF16),
                   jax.ShapeDtypeStruct((s, LANES), F32),
                   jax.ShapeDtypeStruct((s, 3 * LANES), BF16)],
        scratch_shapes=[pltpu.VMEM((SUBLANES, LANES), F32)],
        compiler_params=_params(),
    )(x, y, w_out, g1, wf, bf)


def _fox_proj_fwd(h1, cparts, w, sel, bias, out_dtype, ts, name):
    s = h1.shape[0]
    ng = w.shape[0]
    use_sel = sel is not None

    def body(*refs):
        if use_sel:
            h_ref, cp_ref, w_ref, sel_ref, b_ref, o_ref = refs
            acc = _dot(h_ref[...], w_ref[...]) + _dot(cp_ref[...], sel_ref[...]) + b_ref[...]
        else:
            h_ref, w_ref, o_ref = refs
            acc = _dot(h_ref[...], w_ref[...])
        o_ref[...] = acc.astype(out_dtype)

    in_specs = [pl.BlockSpec((ts, D_MODEL), lambda j, i: (i, 0))]
    args = [h1]
    if use_sel:
        in_specs.append(pl.BlockSpec((ts, 3 * LANES), lambda j, i: (i, 0)))
        args.append(cparts)
    in_specs.append(pl.BlockSpec((None, D_MODEL, FOX_PAD), lambda j, i: (j, 0, 0)))
    args.append(w)
    if use_sel:
        in_specs.append(pl.BlockSpec((None, 3 * LANES, FOX_PAD), lambda j, i: (j, 0, 0)))
        in_specs.append(pl.BlockSpec((None, 1, FOX_PAD), lambda j, i: (j, 0, 0)))
        args += [sel, bias]
    return pl.pallas_call(
        body, name=name, grid=(ng, s // ts),
        in_specs=in_specs,
        out_specs=pl.BlockSpec((None, ts, FOX_PAD), lambda j, i: (j, i, 0)),
        out_shape=jax.ShapeDtypeStruct((ng, s, FOX_PAD), out_dtype),
        compiler_params=_params(2),
    )(*args)


def _attn_fwd(qkv, blk):
    s = qkv.shape[1]
    nblk = s // blk
    heads = [slice(i * HEAD_PAD, (i + 1) * HEAD_PAD) for i in range(HEADS_PER_STEP)]

    def body(q_ref, k_ref, v_ref, o_ref, qb_ref):
        row = lax.broadcasted_iota(jnp.int32, (blk, blk), 0)
        col = lax.broadcasted_iota(jnp.int32, (blk, blk), 1)
        lane = lax.broadcasted_iota(jnp.int32, (blk, HEAD_PAD), 1)

        def q_block(qi, _):
            q0 = pl.multiple_of(qi * blk, blk)
            qs = [q_ref[pl.ds(q0, blk), hd] for hd in heads]

            def step(k0, carry, masked):
                out = []
                scores = [_dot_nt(q, k_ref[pl.ds(k0, blk), hd]) for q, hd in zip(qs, heads)]
                for sc, hd, (m, acc) in zip(scores, heads, carry):
                    v = v_ref[pl.ds(k0, blk), hd]
                    if masked:
                        sc = jnp.where(col <= row, sc, NEG_BIG)
                    m_new = jnp.maximum(m, jnp.max(sc, axis=-1, keepdims=True))
                    p = jnp.exp(sc - m_new)
                    acc = jnp.exp(m - m_new) * acc + _dot(p.astype(BF16), v)
                    out.append((m_new, acc))
                return tuple(out)

            init = tuple((jnp.full((blk, 1), NEG_BIG, F32), jnp.zeros((blk, HEAD_PAD), F32))
                         for _ in heads)
            carry = lax.fori_loop(
                0, qi, lambda kj, c: step(pl.multiple_of(kj * blk, blk), c, False), init)
            carry = step(q0, carry, True)
            for q, hd, (m, acc) in zip(qs, heads, carry):
                l = acc[:, LANE_ONE_V:LANE_ONE_V + 1]
                o_ref[pl.ds(q0, blk), hd] = (acc / l).astype(BF16)
                qf = q.astype(F32)
                cq = (qf[:, LANE_RB:LANE_RB + 1] + qf[:, LANE_RB + 1:LANE_RB + 2]
                      + qf[:, LANE_RB + 2:LANE_RB + 3])
                hi, mid, lo = _split3(cq - (m + jnp.log(l)))
                qb_ref[pl.ds(q0, blk), hd] = jnp.where(lane == LANE_RB, hi, jnp.where(
                    lane == LANE_RB + 1, mid, jnp.where(lane == LANE_RB + 2, lo, q)))
            return 0

        lax.fori_loop(0, nblk, q_block, 0)

    width = HEADS_PER_STEP * HEAD_PAD

    def head_spec(j):
        return pl.BlockSpec((None, s, width), lambda h: (j, 0, h))

    out_spec = pl.BlockSpec((s, width), lambda h: (0, h))
    return pl.pallas_call(
        body, name="attn_fwd", grid=(HEADS // HEADS_PER_STEP,),
        in_specs=[head_spec(0), head_spec(1), head_spec(2)],
        out_specs=[out_spec, out_spec],
        out_shape=[jax.ShapeDtypeStruct((s, FOX_PAD), BF16),
                   jax.ShapeDtypeStruct((s, FOX_PAD), BF16)],
        compiler_params=_params(),
    )(qkv, qkv, qkv)


def _fox_out_loss(o, gate, w_out, x1, target, gf, ts):
    s = x1.shape[0]

    def body(o_ref, gt_ref, w_ref, x1_ref, t_ref, g_ref, dx2_ref, y2_ref, loss_ref, gfin_ref):
        @pl.when(pl.program_id(0) == 0)
        def _():
            loss_ref[...] = jnp.zeros_like(loss_ref)
            gfin_ref[...] = jnp.zeros_like(gfin_ref)

        gv = gt_ref[...]
        y2 = (o_ref[...] * (gv * _sigmoid(gv))).astype(BF16)
        x2 = x1_ref[...] + _dot(y2, w_ref[...])
        rstd = _rstd(x2)
        xhat = x2 * rstd
        g = g_ref[...]
        diff = xhat * g - t_ref[...]
        loss_ref[...] += 0.5 * jnp.sum(jnp.mean(diff * diff, axis=-1, keepdims=True))
        dy = diff * (1.0 / D_MODEL)
        gfin_ref[...] += jnp.sum(dy * xhat, axis=0, keepdims=True)
        dxh = dy * g
        dx2_ref[...] = rstd * (dxh - xhat * jnp.mean(dxh * xhat, axis=-1, keepdims=True))
        y2_ref[...] = y2

    return pl.pallas_call(
        body, name="fox_out_loss", grid=(s // ts,),
        in_specs=[pl.BlockSpec((ts, FOX_PAD), lambda i: (i, 0)),
                  pl.BlockSpec((ts, FOX_PAD), lambda i: (i, 0)),
                  _const_spec((FOX_PAD, D_MODEL)),
                  pl.BlockSpec((ts, D_MODEL), lambda i: (i, 0)),
                  pl.BlockSpec((ts, D_MODEL), lambda i: (i, 0)),
                  _const_spec((1, D_MODEL))],
        out_specs=[pl.BlockSpec((ts, D_MODEL), lambda i: (i, 0)),
                   pl.BlockSpec((ts, FOX_PAD), lambda i: (i, 0)),
                   pl.BlockSpec((SUBLANES, LANES), lambda i: (0, 0)),
                   pl.BlockSpec((1, D_MODEL), lambda i: (0, 0))],
        out_shape=[jax.ShapeDtypeStruct((s, D_MODEL), F32),
                   jax.ShapeDtypeStruct((s, FOX_PAD), BF16),
                   jax.ShapeDtypeStruct((SUBLANES, LANES), F32),
                   jax.ShapeDtypeStruct((1, D_MODEL), F32)],
        compiler_params=_params(),
    )(o, gate, w_out, x1, target, gf)


def _fox_out_bwd(dx2, w_out, o, gate, ts):
    s = dx2.shape[0]

    def body(dx_ref, w_ref, o_ref, gt_ref, do_ref, dg_ref):
        lane = lax.broadcasted_iota(jnp.int32, (ts, HEAD_PAD), 1)
        dy2 = _dot_nt(dx_ref[...].astype(BF16), w_ref[...])
        gv = gt_ref[...]
        sg = _sigmoid(gv)
        ov = o_ref[...]
        dov = dy2 * (gv * sg)
        dg_ref[...] = (dy2 * ov * (sg * (1.0 + gv * (1.0 - sg)))).astype(BF16)
        prod = dov * ov
        for h in range(HEADS):
            sl = slice(h * HEAD_PAD, (h + 1) * HEAD_PAD)
            delta = jnp.sum(prod[:, sl], axis=-1, keepdims=True)
            hi = delta.astype(BF16)
            lo = (delta - hi.astype(F32)).astype(BF16)
            do_h = dov[:, sl].astype(BF16)
            do_ref[:, sl] = jnp.where(lane == LANE_ONE_V, -hi,
                                      jnp.where(lane == LANE_ONE_V + 1, -lo, do_h))

    tile = pl.BlockSpec((ts, FOX_PAD), lambda i: (i, 0))
    return pl.pallas_call(
        body, name="fox_out_bwd", grid=(s // ts,),
        in_specs=[pl.BlockSpec((ts, D_MODEL), lambda i: (i, 0)),
                  _const_spec((FOX_PAD, D_MODEL)), tile, tile],
        out_specs=[tile, tile],
        out_shape=[jax.ShapeDtypeStruct((s, FOX_PAD), BF16),
                   jax.ShapeDtypeStruct((s, FOX_PAD), BF16)],
        compiler_params=_params(),
    )(dx2, w_out, o, gate)


def _attn_bwd(qb, qkv, do, blk):
    s = qb.shape[0]
    nblk = s // blk
    heads = [slice(i * HEAD_PAD, (i + 1) * HEAD_PAD) for i in range(HEADS_PER_STEP)]

    def body(q_ref, k_ref, v_ref, do_ref, dq_ref, dk_ref, dv_ref, dcum_ref, dq_acc):
        group = pl.program_id(0)
        kj = pl.program_id(1)
        row = lax.broadcasted_iota(jnp.int32, (blk, blk), 0)
        col = lax.broadcasted_iota(jnp.int32, (blk, blk), 1)
        lane = lax.broadcasted_iota(jnp.int32, (blk, LANES), 1)
        mine = [lane == group * HEADS_PER_STEP + i for i in range(HEADS_PER_STEP)]

        @pl.when(kj == 0)
        def _():
            dq_acc[...] = jnp.zeros_like(dq_acc)

        @pl.when((group == 0) & (kj == 0))
        def _():
            dcum_ref[...] = jnp.zeros_like(dcum_ref)

        k0 = pl.multiple_of(kj * blk, blk)
        ks = [k_ref[:, hd] for hd in heads]
        vs = [v_ref[:, hd] for hd in heads]

        def step(q0, carry, masked):
            out = []
            qs = [q_ref[pl.ds(q0, blk), hd] for hd in heads]
            dos = [do_ref[pl.ds(q0, blk), hd] for hd in heads]
            scores = [_dot_nt(q, k) for q, k in zip(qs, ks)]
            dps = [_dot_nt(dov, v) for dov, v in zip(dos, vs)]
            for hd, k, q, dov, sc, dp, (dk, dv) in zip(heads, ks, qs, dos, scores, dps, carry):
                p = jnp.exp(sc)
                if masked:
                    p = jnp.where(col <= row, p, 0.0)
                ds = (p * dp).astype(BF16)
                dv = dv + _dot_tn(p.astype(BF16), dov)
                dk = dk + _dot_tn(ds, q)
                dq_acc[pl.ds(q0, blk), hd] += _dot(ds, k)
                out.append((dk, dv))
            return tuple(out)

        zero = jnp.zeros((blk, HEAD_PAD), F32)
        carry = step(k0, tuple((zero, zero) for _ in heads), True)
        carry = lax.fori_loop(
            kj + 1, nblk, lambda qi, c: step(pl.multiple_of(qi * blk, blk), c, False), carry)
        dcum = dcum_ref[pl.ds(k0, blk), :]
        for hd, mask, (dk, dv) in zip(heads, mine, carry):
            dk_ref[:, hd] = dk.astype(BF16)
            dv_ref[:, hd] = dv.astype(BF16)
            dcum = jnp.where(mask, -dk[:, LANE_CK:LANE_CK + 1], dcum)
        dcum_ref[pl.ds(k0, blk), :] = dcum

        @pl.when(kj == nblk - 1)
        def _():
            def finish(bi, _):
                r0 = pl.multiple_of(bi * blk, blk)
                dcum = dcum_ref[pl.ds(r0, blk), :]
                for hd, mask in zip(heads, mine):
                    dq = dq_acc[pl.ds(r0, blk), hd]
                    dq_ref[pl.ds(r0, blk), hd] = dq.astype(BF16)
                    dcum = dcum + jnp.where(mask, dq[:, LANE_RB:LANE_RB + 1], 0.0)
                dcum_ref[pl.ds(r0, blk), :] = dcum
                return 0

            lax.fori_loop(0, nblk, finish, 0)

    width = HEADS_PER_STEP * HEAD_PAD
    whole = pl.BlockSpec((s, width), lambda h, j: (0, h))
    part = pl.BlockSpec((blk, width), lambda h, j: (j, h))
    out = jax.ShapeDtypeStruct((s, FOX_PAD), BF16)
    return pl.pallas_call(
        body, name="attn_bwd", grid=(HEADS // HEADS_PER_STEP, nblk),
        in_specs=[whole,
                  pl.BlockSpec((None, blk, width), lambda h, j: (1, j, h)),
                  pl.BlockSpec((None, blk, width), lambda h, j: (2, j, h)),
                  whole],
        out_specs=[whole, part, part, pl.BlockSpec((s, LANES), lambda h, j: (0, 0))],
        out_shape=[out, out, out, jax.ShapeDtypeStruct((s, LANES), F32)],
        scratch_shapes=[pltpu.VMEM((s, width), F32)],
        compiler_params=_params(2),
    )(qb, qkv, qkv, do)


def _fox_in_bwd(dq, dk, dv, dg, wt, wft, dcum, f, x1, dx2, g1, ts):
    s = x1.shape[0]
    nt = s // ts

    def body(dq_ref, dk_ref, dv_ref, dg_ref, wt_ref, wft_ref, dcum_ref, f_ref, x1_ref, dx2_ref,
             g_ref, dx1_ref, dx1b_ref, df_ref, gn_ref, gbf_ref, rcar_ref):
        @pl.when(pl.program_id(0) == 0)
        def _():
            rcar_ref[...] = jnp.zeros_like(rcar_ref)
            gn_ref[...] = jnp.zeros_like(gn_ref)
            gbf_ref[...] = jnp.zeros_like(gbf_ref)

        dkv = dk_ref[...]
        rsum = _cumsum_rows(dcum_ref[...], reverse=True) + rcar_ref[0:1, :]
        df = rsum * _sigmoid(-f_ref[...])
        dfb = df.astype(BF16)
        dh = (_dot_nt(dq_ref[...], wt_ref[0]) + _dot_nt(dkv, wt_ref[1])
              + _dot_nt(dv_ref[...], wt_ref[2]) + _dot_nt(dg_ref[...], wt_ref[3])
              + _dot_nt(dfb, wft_ref[...]))
        dxn, dgn = _norm_bwd(x1_ref[...], g_ref[...], dh)
        dx1 = dx2_ref[...] + dxn
        dx1_ref[...] = dx1
        dx1b_ref[...] = dx1.astype(BF16)
        df_ref[...] = dfb
        gn_ref[...] += dgn
        gbf_ref[...] += jnp.sum(df, axis=0, keepdims=True)
        rcar_ref[...] = rsum[0:SUBLANES, :]

    rev = lambda i: (nt - 1 - i, 0)
    wide = pl.BlockSpec((ts, FOX_PAD), rev)
    return pl.pallas_call(
        body, name="fox_in_bwd", grid=(nt,),
        in_specs=[wide, wide, wide, wide,
                  _const_spec((4, D_MODEL, FOX_PAD)),
                  _const_spec((D_MODEL, LANES)),
                  pl.BlockSpec((ts, LANES), rev),
                  pl.BlockSpec((ts, LANES), rev),
                  pl.BlockSpec((ts, D_MODEL), rev),
                  pl.BlockSpec((ts, D_MODEL), rev),
                  _const_spec((1, D_MODEL))],
        out_specs=[pl.BlockSpec((ts, D_MODEL), rev),
                   pl.BlockSpec((ts, D_MODEL), rev),
                   pl.BlockSpec((ts, LANES), rev),
                   pl.BlockSpec((1, D_MODEL), lambda i: (0, 0)),
                   pl.BlockSpec((1, LANES), lambda i: (0, 0))],
        out_shape=[jax.ShapeDtypeStruct((s, D_MODEL), F32),
                   jax.ShapeDtypeStruct((s, D_MODEL), BF16),
                   jax.ShapeDtypeStruct((s, LANES), BF16),
                   jax.ShapeDtypeStruct((1, D_MODEL), F32),
                   jax.ShapeDtypeStruct((1, LANES), F32)],
        scratch_shapes=[pltpu.VMEM((SUBLANES, LANES), F32)],
        compiler_params=_params(),
    )(dq, dk, dv, dg, wt, wft, dcum, f, x1, dx2, g1)


def _lru_core_bwd(dx1b, w_out, xb, gate, hs, cw, cb, wa, ba, wx, bx, a_param, wa_t, wx_t, ts):
    s = xb.shape[0]
    nt = s // ts
    tpb = ts // SUBLANES

    def body(dx_ref, wo_ref, xb_ref, xbh_ref, gate_ref, hs_ref, hsh_ref, cw_ref, cb_ref, wa_ref,
             ba_ref, wx_ref, bx_ref, ap_ref, wat_ref, wxt_ref,
             du_ref, gwa_ref, gwx_ref, gvec_ref, acar_ref, dhcar_ref, dxccar_ref):
        step = pl.program_id(0)

        @pl.when(step == 0)
        def _():
            acar_ref[...] = jnp.zeros_like(acar_ref)
            dhcar_ref[...] = jnp.zeros_like(dhcar_ref)
            dxccar_ref[...] = jnp.zeros_like(dxccar_ref)
            gwa_ref[...] = jnp.zeros_like(gwa_ref)
            gwx_ref[...] = jnp.zeros_like(gwx_ref)
            gvec_ref[...] = jnp.zeros_like(gvec_ref)

        first_tile = step == nt - 1
        halo_on = jnp.where(first_tile, 0.0, 1.0)
        prev8 = xbh_ref[...] * halo_on
        hprev_row = hsh_ref[SUBLANES - 1:SUBLANES, :] * halo_on

        xbv = xb_ref[...]
        taps = _conv_taps(xbv, prev8)
        cw_v = cw_ref[...]
        xc, xcb, r, i, sp, a, mult = _lru_pre(taps, cw_v, cb_ref[...], wa_ref, ba_ref[...],
                                              wx_ref, bx_ref[...], ap_ref[...])
        hs = hs_ref[...]
        gv = gate_ref[...]
        sg = _sigmoid(gv)
        dy = _dot_nt(dx_ref[...], wo_ref[...])
        dhs = dy * (gv * sg)
        dgate = dy * hs * (sg * (1.0 + gv * (1.0 - sg)))

        rows = lax.broadcasted_iota(jnp.int32, a.shape, 0)
        a_next = jnp.where(rows < ts - 1, pltpu.roll(a, ts - 1, 0), acar_ref[0:1, :])
        cum_a, dh_loc = _scan_rows(a_next, dhs, reverse=True)
        dh = cum_a * dhcar_ref[0:1, :] + dh_loc
        h_prev = jnp.where(rows >= 1, pltpu.roll(hs, 1, 0), hprev_row)

        da = dh * h_prev
        ixc = i * xc
        dmult = dh * ixc
        di = dh * mult * xc
        dxc = dh * mult * i
        dlog_a = da * a - dmult * (a * a) / mult
        dr = dlog_a * ((-LRU_C) * sp)
        dsp = jnp.sum(dlog_a * ((-LRU_C) * r), axis=0, keepdims=True)
        dra = dr * r * (1.0 - r)
        dia = di * i * (1.0 - i)
        drab = dra.astype(BF16)
        diab = dia.astype(BF16)
        back = []
        for n in range(LRU_BLOCKS):
            sl = slice(n * LRU_BLOCK_W, (n + 1) * LRU_BLOCK_W)
            gwa_ref[n] += _dot_tn(xcb[:, sl], drab[:, sl])
            gwx_ref[n] += _dot_tn(xcb[:, sl], diab[:, sl])
            back.append(_dot(drab[:, sl], wat_ref[n]) + _dot(diab[:, sl], wxt_ref[n]))
        dxc = dxc + jnp.concatenate(back, axis=1)

        nxt8 = dxccar_ref[...]
        rows8 = lax.broadcasted_iota(jnp.int32, nxt8.shape, 0)
        dxb = cw_v[3:4] * dxc
        for j in range(1, CONV_WIDTH):
            rj = pltpu.roll(dxc, ts - j, 0)
            pj = pltpu.roll(nxt8, SUBLANES - j, 0)
            tail = jnp.where(rows8 >= SUBLANES - j, pj, rj[ts - SUBLANES:])
            dxb = dxb + cw_v[3 - j:4 - j] * jnp.concatenate([rj[:ts - SUBLANES], tail], axis=0)

        du_ref[:, :LRU_WIDTH] = dxb.astype(BF16)
        du_ref[:, LRU_WIDTH:] = dgate.astype(BF16)

        z = -ap_ref[...]
        gvec = [jnp.sum(dxc * taps[3 - k], axis=0, keepdims=True) for k in range(CONV_WIDTH)]
        gvec.append(jnp.sum(dxc, axis=0, keepdims=True))
        gvec.append(jnp.sum(dra, axis=0, keepdims=True))
        gvec.append(jnp.sum(dia, axis=0, keepdims=True))
        gvec.append(-dsp * _sigmoid(z))
        gvec_ref[...] += jnp.concatenate(gvec, axis=0)

        acar_ref[...] = a[0:SUBLANES, :]
        dhcar_ref[...] = dh[0:SUBLANES, :]
        dxccar_ref[...] = dxc[0:SUBLANES, :]

    rev = lambda i: (nt - 1 - i, 0)
    halo = lambda i: (jnp.maximum((nt - 1 - i) * tpb - 1, 0), 0)
    tile = pl.BlockSpec((ts, LRU_WIDTH), rev)
    halo_spec = pl.BlockSpec((SUBLANES, LRU_WIDTH), halo)
    vec = _const_spec((1, LRU_WIDTH))
    blk = _const_spec((LRU_BLOCKS, LRU_BLOCK_W, LRU_BLOCK_W))
    acc_blk = pl.BlockSpec((LRU_BLOCKS, LRU_BLOCK_W, LRU_BLOCK_W), lambda i: (0, 0, 0))
    return pl.pallas_call(
        body, name="lru_core_bwd", grid=(nt,),
        in_specs=[pl.BlockSpec((ts, D_MODEL), rev),
                  _const_spec((LRU_WIDTH, D_MODEL)),
                  tile, halo_spec, tile, tile, halo_spec,
                  _const_spec((CONV_WIDTH, LRU_WIDTH)), vec, blk, vec, blk, vec, vec, blk, blk],
        out_specs=[pl.BlockSpec((ts, 2 * LRU_WIDTH), rev), acc_blk, acc_blk,
                   pl.BlockSpec((SUBLANES, LRU_WIDTH), lambda i: (0, 0))],
        out_shape=[jax.ShapeDtypeStruct((s, 2 * LRU_WIDTH), BF16),
                   jax.ShapeDtypeStruct((LRU_BLOCKS, LRU_BLOCK_W, LRU_BLOCK_W), F32),
                   jax.ShapeDtypeStruct((LRU_BLOCKS, LRU_BLOCK_W, LRU_BLOCK_W), F32),
                   jax.ShapeDtypeStruct((SUBLANES, LRU_WIDTH), F32)],
        scratch_shapes=[pltpu.VMEM((SUBLANES, LRU_WIDTH), F32),
                        pltpu.VMEM((SUBLANES, LRU_WIDTH), F32),
                        pltpu.VMEM((SUBLANES, LRU_WIDTH), F32)],
        compiler_params=_params(),
    )(dx1b, w_out, xb, xb, gate, hs, hs, cw, cb, wa, ba, wx, bx, a_param, wa_t, wx_t)


def _lru_in_bwd(du, w_in, x, dx1, g0, ts):
    s = x.shape[0]

    def body(du_ref, w_ref, x_ref, dx1_ref, g_ref, gx_ref, gn_ref):
        @pl.when(pl.program_id(0) == 0)
        def _():
            gn_ref[...] = jnp.zeros_like(gn_ref)

        duv = du_ref[...]
        dh = _dot_nt(duv[:, 0:LRU_IN_SHARD], w_ref[0])
        for j in range(1, N_DEV):
            dh = dh + _dot_nt(duv[:, j * LRU_IN_SHARD:(j + 1) * LRU_IN_SHARD], w_ref[j])
        dxn, dgn = _norm_bwd(x_ref[...], g_ref[...], dh)
        gx_ref[...] = dx1_ref[...] + dxn
        gn_ref[...] += dgn

    tile = pl.BlockSpec((ts, D_MODEL), lambda i: (i, 0))
    return pl.pallas_call(
        body, name="lru_in_bwd", grid=(s // ts,),
        in_specs=[pl.BlockSpec((ts, 2 * LRU_WIDTH), lambda i: (i, 0)),
                  _const_spec((N_DEV, D_MODEL, LRU_IN_SHARD)), tile, tile,
                  _const_spec((1, D_MODEL))],
        out_specs=[tile, pl.BlockSpec((1, D_MODEL), lambda i: (0, 0))],
        out_shape=[jax.ShapeDtypeStruct((s, D_MODEL), F32),
                   jax.ShapeDtypeStruct((1, D_MODEL), F32)],
        compiler_params=_params(),
    )(du, w_in, x, dx1, g0)


def _weight_grad(a, b, ts, name, scale=1.0, col_shards=1):
    s, ka = a.shape
    nb = b.shape[1]
    nt = s // ts
    per = nb // col_shards

    def body(a_ref, b_ref, o_ref):
        @pl.when(pl.program_id(0) == 0)
        def _():
            o_ref[...] = jnp.zeros_like(o_ref)

        if col_shards == 1:
            o_ref[...] += _dot_tn(a_ref[...], b_ref[...])
        else:
            av, bv = a_ref[...], b_ref[...]
            for j in range(col_shards):
                o_ref[j] += _dot_tn(av, bv[:, j * per:(j + 1) * per])
        if scale != 1.0:
            @pl.when(pl.program_id(0) == nt - 1)
            def _():
                o_ref[...] = o_ref[...] * scale

    out_dims = (ka, nb) if col_shards == 1 else (col_shards, ka, per)
    return pl.pallas_call(
        body, name=name, grid=(nt,),
        in_specs=[pl.BlockSpec((ts, ka), lambda i: (i, 0)),
                  pl.BlockSpec((ts, nb), lambda i: (i, 0))],
        out_specs=pl.BlockSpec(out_dims, lambda i: (0,) * len(out_dims)),
        out_shape=jax.ShapeDtypeStruct(out_dims, F32),
        compiler_params=_params(),
    )(a, b)


def _sum_parts(gp_ref):
    g = gp_ref[0].astype(F32)
    for k in range(1, gp_ref.shape[0]):
        g = g + gp_ref[k].astype(F32)
    return g


def _adamw(g_parts, w, m, v, tr, name):
    nparts, rows, cols = g_parts.shape

    def body(gp_ref, w_ref, m_ref, v_ref, g_ref, d_ref, mo_ref, vo_ref):
        g = _sum_parts(gp_ref)
        m2 = ADAM_B1 * m_ref[...] + (1.0 - ADAM_B1) * g
        v2 = ADAM_B2 * v_ref[...] + (1.0 - ADAM_B2) * (g * g)
        m_hat = m2 / (1.0 - ADAM_B1 ** ADAM_STEP)
        v_hat = v2 / (1.0 - ADAM_B2 ** ADAM_STEP)
        g_ref[...] = g
        d_ref[...] = (-ADAM_LR) * (m_hat / (jnp.sqrt(v_hat) + ADAM_EPS) + ADAM_WD * w_ref[...])
        mo_ref[...] = m2
        vo_ref[...] = v2

    tile = pl.BlockSpec((tr, cols), lambda i: (i, 0))
    out = jax.ShapeDtypeStruct((rows, cols), F32)
    return pl.pallas_call(
        body, name=name, grid=(rows // tr,),
        in_specs=[pl.BlockSpec((nparts, tr, cols), lambda i: (0, i, 0)), tile, tile, tile],
        out_specs=[tile, tile, tile, tile],
        out_shape=[out, out, out, out],
        compiler_params=_params(),
    )(g_parts, w, m, v)


def _reduce_parts(g_parts, name):
    _, rows, cols = g_parts.shape

    def body(gp_ref, g_ref):
        g_ref[...] = _sum_parts(gp_ref)

    return pl.pallas_call(
        body, name=name,
        out_shape=jax.ShapeDtypeStruct((rows, cols), F32),
        compiler_params=pltpu.CompilerParams(vmem_limit_bytes=VMEM_LIMIT_BYTES),
    )(g_parts)


def _mesh_pos():
    ix, iy, ic = lax.axis_index("x"), lax.axis_index("y"), lax.axis_index("c")
    return ix, iy, ic


def _peer(ix, iy, ic, mask):
    px = 1 - ix if mask & 4 else ix
    py = 1 - iy if mask & 2 else iy
    pc = 1 - ic if mask & 1 else ic
    return (px, py, pc), 4 * px + 2 * py + pc


def _exchange(arrays, scatter, name):
    n = len(arrays)

    def body(*refs):
        x_refs, o_refs = refs[:n], refs[n:2 * n]
        send_sems, recv_sems, local_sems = refs[2 * n:]
        ix, iy, ic = _mesh_pos()
        me = 4 * ix + 2 * iy + ic

        def src(a, dest):
            return x_refs[a].at[dest] if scatter else x_refs[a]

        local = [pltpu.make_async_copy(src(a, me), o_refs[a].at[me], local_sems.at[a])
                 for a in range(n)]
        for cp in local:
            cp.start()
        sends = []
        for mask in range(1, N_DEV):
            peer, pidx = _peer(ix, iy, ic, mask)
            for a in range(n):
                cp = pltpu.make_async_remote_copy(
                    src_ref=src(a, pidx), dst_ref=o_refs[a].at[me],
                    send_sem=send_sems.at[a, mask - 1], recv_sem=recv_sems.at[a, mask - 1],
                    device_id=peer, device_id_type=pl.DeviceIdType.MESH)
                cp.start()
                sends.append(cp)
        for mask in range(1, N_DEV):
            peer, pidx = _peer(ix, iy, ic, mask)
            for a in range(n):
                pltpu.make_async_remote_copy(
                    src_ref=src(a, me), dst_ref=o_refs[a].at[pidx],
                    send_sem=send_sems.at[a, mask - 1], recv_sem=recv_sems.at[a, mask - 1],
                    device_id=peer, device_id_type=pl.DeviceIdType.MESH).wait_recv()
        for cp in sends:
            cp.wait_send()
        for cp in local:
            cp.wait()

    out_shape = [jax.ShapeDtypeStruct(x.shape if scatter else (N_DEV,) + x.shape, x.dtype)
                 for x in arrays]
    return pl.pallas_call(
        body, name=name,
        in_specs=[pl.BlockSpec(memory_space=pl.ANY)] * n,
        out_specs=[pl.BlockSpec(memory_space=pl.ANY)] * n,
        out_shape=out_shape,
        scratch_shapes=[pltpu.SemaphoreType.DMA((n, N_DEV - 1)),
                        pltpu.SemaphoreType.DMA((n, N_DEV - 1)),
                        pltpu.SemaphoreType.DMA((n,))],
    )(*arrays)


def _gather_two_level(arrays, name):
    n = len(arrays)

    def body(*refs):
        x_refs, o_refs = refs[:n], refs[n:2 * n]
        send_sems, recv_sems, local_sems = refs[2 * n:]
        ix, iy, ic = _mesh_pos()
        me, sibling = (ix, iy, ic), (ix, iy, 1 - ic)
        chips = [(1 - ix, iy), (ix, 1 - iy), (1 - ix, 1 - iy)]

        def idx(px, py, pc):
            return 4 * px + 2 * py + pc

        def copy(a, k, block, to, src=None):
            dst = o_refs[a].at[idx(*block)]
            return pltpu.make_async_remote_copy(
                src_ref=dst if src is None else src, dst_ref=dst,
                send_sem=send_sems.at[a, k], recv_sem=recv_sems.at[a, k],
                device_id=to, device_id_type=pl.DeviceIdType.MESH)

        local = [pltpu.make_async_copy(x_refs[a], o_refs[a].at[idx(*me)], local_sems.at[a])
                 for a in range(n)]
        for cp in local:
            cp.start()
        first = []
        for a in range(n):
            first.append(copy(a, 0, me, sibling, src=x_refs[a]))
            first += [copy(a, 1 + j, me, (*chip, ic), src=x_refs[a])
                      for j, chip in enumerate(chips)]
        for cp in first:
            cp.start()
        passed = []
        for j, chip in enumerate(chips):
            for a in range(n):
                copy(a, 1 + j, (*chip, ic), me).wait_recv()
                cp = copy(a, 4 + j, (*chip, ic), sibling)
                cp.start()
                passed.append(cp)
        for a in range(n):
            copy(a, 0, sibling, me).wait_recv()
            for j, chip in enumerate(chips):
                copy(a, 4 + j, (*chip, 1 - ic), me).wait_recv()
        for cp in first + passed:
            cp.wait_send()
        for cp in local:
            cp.wait()

    return pl.pallas_call(
        body, name=name,
        in_specs=[pl.BlockSpec(memory_space=pl.ANY)] * n,
        out_specs=[pl.BlockSpec(memory_space=pl.ANY)] * n,
        out_shape=[jax.ShapeDtypeStruct((N_DEV,) + x.shape, x.dtype) for x in arrays],
        scratch_shapes=[pltpu.SemaphoreType.DMA((n, N_DEV - 1)),
                        pltpu.SemaphoreType.DMA((n, N_DEV - 1)),
                        pltpu.SemaphoreType.DMA((n,))],
    )(*arrays)


def _swap_sibling(arrays, name):
    n = len(arrays)
    n_chips = N_DEV // 2

    def body(*refs):
        x_refs, got_refs = refs[:n], refs[n:2 * n]
        send_sems, recv_sems = refs[2 * n:]
        ix, iy, ic = _mesh_pos()
        sibling = (ix, iy, 1 - ic)
        sends = []
        for a in range(n):
            for q in range(n_chips):
                cp = pltpu.make_async_remote_copy(
                    src_ref=x_refs[a].at[q, 1 - ic], dst_ref=got_refs[a].at[q],
                    send_sem=send_sems.at[a, q], recv_sem=recv_sems.at[a, q],
                    device_id=sibling, device_id_type=pl.DeviceIdType.MESH)
                cp.start()
                sends.append(cp)
        for cp in sends:
            cp.wait()

    return pl.pallas_call(
        body, name=name,
        in_specs=[pl.BlockSpec(memory_space=pl.ANY)] * n,
        out_specs=[pl.BlockSpec(memory_space=pl.ANY)] * n,
        out_shape=[jax.ShapeDtypeStruct((n_chips,) + x.shape[2:], x.dtype) for x in arrays],
        scratch_shapes=[pltpu.SemaphoreType.DMA((n, n_chips)),
                        pltpu.SemaphoreType.DMA((n, n_chips))],
    )(*arrays)


def _exchange_chips(arrays, name):
    n = len(arrays)
    n_chips = N_DEV // 2

    def body(*refs):
        x_refs, o_refs = refs[:n], refs[n:2 * n]
        send_sems, recv_sems, local_sems = refs[2 * n:]
        ix, iy, ic = _mesh_pos()
        my_chip = 2 * ix + iy
        local = [pltpu.make_async_copy(x_refs[a].at[my_chip], o_refs[a].at[my_chip],
                                       local_sems.at[a]) for a in range(n)]
        for cp in local:
            cp.start()
        sends = []
        for mask in range(1, n_chips):
            px = 1 - ix if mask & 2 else ix
            py = 1 - iy if mask & 1 else iy
            for a in range(n):
                cp = pltpu.make_async_remote_copy(
                    src_ref=x_refs[a].at[2 * px + py], dst_ref=o_refs[a].at[my_chip],
                    send_sem=send_sems.at[a, mask - 1], recv_sem=recv_sems.at[a, mask - 1],
                    device_id=(px, py, ic), device_id_type=pl.DeviceIdType.MESH)
                cp.start()
                sends.append(cp)
        for mask in range(1, n_chips):
            px = 1 - ix if mask & 2 else ix
            py = 1 - iy if mask & 1 else iy
            for a in range(n):
                pltpu.make_async_remote_copy(
                    src_ref=x_refs[a].at[my_chip], dst_ref=o_refs[a].at[2 * px + py],
                    send_sem=send_sems.at[a, mask - 1], recv_sem=recv_sems.at[a, mask - 1],
                    device_id=(px, py, ic), device_id_type=pl.DeviceIdType.MESH).wait_recv()
        for cp in sends:
            cp.wait_send()
        for cp in local:
            cp.wait()

    return pl.pallas_call(
        body, name=name,
        in_specs=[pl.BlockSpec(memory_space=pl.ANY)] * n,
        out_specs=[pl.BlockSpec(memory_space=pl.ANY)] * n,
        out_shape=[jax.ShapeDtypeStruct(x.shape, x.dtype) for x in arrays],
        scratch_shapes=[pltpu.SemaphoreType.DMA((n, n_chips - 1)),
                        pltpu.SemaphoreType.DMA((n, n_chips - 1)),
                        pltpu.SemaphoreType.DMA((n,))],
    )(*arrays)


def _pair_sum(core, x, got, name):
    nq, rows, cols = got.shape

    def body(c_ref, x_ref, g_ref, o_ref):
        o_ref[...] = (x_ref[...] + g_ref[...]).astype(BF16)

    blk = pl.BlockSpec((None, rows, cols), lambda q, c: (q, 0, 0))
    return pl.pallas_call(
        body, name=name,
        grid_spec=pltpu.PrefetchScalarGridSpec(
            num_scalar_prefetch=1, grid=(nq,),
            in_specs=[pl.BlockSpec((None, None, rows, cols), lambda q, c: (q, c[0], 0, 0)), blk],
            out_specs=blk),
        out_shape=jax.ShapeDtypeStruct(got.shape, BF16),
        compiler_params=_params(),
    )(core, x, got)


def _pad_heads_cols(w):
    k = w.shape[0]
    w = w.reshape(k, HEADS, HEAD_DIM)
    return jnp.pad(w, ((0, 0), (0, 0), (0, HEAD_PAD - HEAD_DIM))).reshape(k, FOX_PAD)


def _unpad_heads_cols(w):
    k = w.shape[0]
    return w.reshape(k, HEADS, HEAD_PAD)[:, :, :HEAD_DIM].reshape(k, HEADS * HEAD_DIM)


def _selectors():
    r = lax.broadcasted_iota(jnp.int32, (3 * LANES, FOX_PAD), 0)
    c = lax.broadcasted_iota(jnp.int32, (3 * LANES, FOX_PAD), 1)
    part, head_r = r // LANES, r % LANES
    head_c, lane_c = c // HEAD_PAD, c % HEAD_PAD
    same = (head_r == head_c) & (head_r < HEADS)
    sel_q = jnp.where(same & (lane_c == LANE_RB + part), 1.0, 0.0)
    sel_k = jnp.where(same & (lane_c == LANE_CK + part), -1.0, 0.0)
    sel = jnp.stack([sel_q, sel_k, jnp.zeros_like(sel_q)]).astype(BF16)
    lane = lax.broadcasted_iota(jnp.int32, (1, FOX_PAD), 1) % HEAD_PAD
    ones_q = jnp.where((lane >= LANE_CK) & (lane < LANE_CK + 3), 1.0, 0.0)
    ones_k = jnp.where((lane >= LANE_RB) & (lane < LANE_RB + 3), 1.0, 0.0)
    ones_v = jnp.where((lane >= LANE_ONE_V) & (lane < LANE_ONE_V + 2), 1.0, 0.0)
    bias = jnp.stack([ones_q, ones_k, ones_v]).astype(F32)
    return sel, bias


def _local_step(x, target, norm_g, final_g, w_in8, conv_w, conv_b, wa, ba, wx, bx, a_param,
                w_out_b, fox_in8, b_f, fox_out_b, blk=512, ts=256):
    qk_scale = 1.0 / (HEAD_DIM ** 0.5)
    g0, g1 = norm_g[0:1], norm_g[1:2]
    gf = final_g.reshape(1, D_MODEL)
    wa_b, wx_b = wa.astype(BF16), wx.astype(BF16)
    fox_w_in = jnp.transpose(fox_in8, (1, 0, 2)).reshape(D_MODEL, FOX_IN_COLS)
    wq = _pad_heads_cols(fox_w_in[:, 0:1024]) * qk_scale
    wk = _pad_heads_cols(fox_w_in[:, 1024:2048])
    wv = _pad_heads_cols(fox_w_in[:, 2048:3072])
    wg = _pad_heads_cols(fox_w_in[:, 3072:4096])
    wf_b = jnp.pad(fox_w_in[:, 4096:], ((0, 0), (0, LANES - HEADS)))
    w4 = jnp.stack([wq, wk, wv, wg]).astype(BF16)
    bf_pad = jnp.pad(b_f, ((0, 0), (0, LANES - HEADS)))
    fo = fox_out_b.reshape(HEADS, HEAD_DIM, D_MODEL)
    fo_b = jnp.pad(fo, ((0, 0), (0, HEAD_PAD - HEAD_DIM), (0, 0))).reshape(FOX_PAD, D_MODEL)
    sel, bias = _selectors()

    xb, gate1, h0 = _lru_in_fwd(x, g0, w_in8, ts)
    y1, hs = _lru_core_fwd(xb, gate1, conv_w, conv_b, wa_b, ba, wx_b, bx, a_param, ts)
    x1, h1, f, cparts = _fox_pre_fwd(x, y1, w_out_b, g1, wf_b, bf_pad, ts)
    qkv = _fox_proj_fwd(h1, cparts, w4[0:3], sel, bias, BF16, ts, "fox_proj_qkv")
    gate2 = _fox_proj_fwd(h1, None, w4[3:4], None, None, F32, ts, "fox_proj_gate")[0]
    o, qb = _attn_fwd(qkv, blk)
    dx2, y2, loss_acc, g_final = _fox_out_loss(o, gate2, fo_b, x1, target, gf, ts)

    do, dgate2 = _fox_out_bwd(dx2, fo_b, o, gate2, ts)
    dq, dk, dv, dcum = _attn_bwd(qb, qkv, do, blk)
    dx1, dx1b, df, g_norm1, g_bf = _fox_in_bwd(dq, dk, dv, dgate2, w4, wf_b, dcum, f, x1, dx2,
                                               g1, ts)
    du, g_wa, g_wx, g_vec = _lru_core_bwd(dx1b, w_out_b, xb, gate1, hs, conv_w, conv_b, wa_b, ba,
                                          wx_b, bx, a_param, jnp.transpose(wa_b, (0, 2, 1)),
                                          jnp.transpose(wx_b, (0, 2, 1)), ts)
    grad_x, g_norm0 = _lru_in_bwd(du, w_in8, x, dx1, g0, ts)

    tw = 512
    g_lru_w_in = _weight_grad(h0, du, tw, "grad_lru_w_in", col_shards=N_DEV)
    g_lru_w_out = _weight_grad(y1, dx1b, tw, "grad_lru_w_out")
    g_q = _weight_grad(h1, dq, tw, "grad_fox_wq", scale=qk_scale)
    g_k = _weight_grad(h1, dk, tw, "grad_fox_wk")
    g_v = _weight_grad(h1, dv, tw, "grad_fox_wv")
    g_g = _weight_grad(h1, dgate2, tw, "grad_fox_wg")
    g_f = _weight_grad(h1, df, tw, "grad_fox_wf")
    g_fox_w_in = jnp.concatenate(
        [_unpad_heads_cols(g_q), _unpad_heads_cols(g_k), _unpad_heads_cols(g_v),
         _unpad_heads_cols(g_g), g_f[:, :HEADS]], axis=1)
    g_fox_w_in = jnp.transpose(g_fox_w_in.reshape(D_MODEL, N_DEV, FOX_IN_SHARD), (1, 0, 2))
    g_fo = _weight_grad(y2, dx2.astype(BF16), tw, "grad_fox_w_out")
    g_fox_w_out = g_fo.reshape(HEADS, HEAD_PAD, D_MODEL)[:, :HEAD_DIM].reshape(
        HEADS * HEAD_DIM, D_MODEL)

    grads = dict(
        norm_g=jnp.concatenate([g_norm0, g_norm1], axis=0), final_g=g_final[0],
        lru_w_in=g_lru_w_in, lru_conv_w=g_vec[0:4], lru_conv_b=g_vec[4:5], lru_wa=g_wa,
        lru_ba=g_vec[5:6], lru_wx=g_wx, lru_bx=g_vec[6:7], lru_a_param=g_vec[7:8],
        lru_w_out=g_lru_w_out, fox_w_in=g_fox_w_in, fox_b_f=g_bf[:, :HEADS],
        fox_w_out=g_fox_w_out)
    return loss_acc[0, 0], grad_x, grads


SMALL =("norm_g", "final_g", "lru_conv_b", "lru_wa", "lru_ba", "lru_wx", "lru_bx", "lru_a_param",
         "fox_b_f")
ALL_WEIGHTS = ("norm_g", "final_g", "lru_w_in", "lru_conv_w", "lru_conv_b", "lru_wa", "lru_ba",
               "lru_wx", "lru_bx", "lru_a_param", "lru_w_out", "fox_w_in", "fox_b_f", "fox_w_out")


def _pack_small(d):
    rows = []
    for n in SMALL:
        a = d[n].reshape(-1)
        if a.shape[0] % LANES:
            a = jnp.pad(a, (0, LANES - a.shape[0] % LANES))
        rows.append(a.reshape(-1, LANES))
    packed = jnp.concatenate(rows, axis=0)
    return jnp.pad(packed, ((0, N_DEV * SMALL_CHUNK_ROWS - packed.shape[0]), (0, 0)))


def _unpack_small(packed, like):
    out, off = {}, 0
    for n, nrows in zip(SMALL, SMALL_ROWS):
        size = like[n].size
        out[n] = packed[off:off + nrows].reshape(-1)[:size].reshape(like[n].shape)
        off += nrows
    return out


def kernel(x, norm_g, final_g, lru_w_in, lru_conv_w, lru_conv_b, lru_wa, lru_ba, lru_wx, lru_bx, lru_a_param, lru_w_out, fox_w_in, fox_b_f, fox_w_out, loss_target, m_norm_g, m_final_g, m_lru_w_in, m_lru_conv_w, m_lru_conv_b, m_lru_wa, m_lru_ba, m_lru_wx, m_lru_bx, m_lru_a_param, m_lru_w_out, m_fox_w_in, m_fox_b_f, m_fox_w_out, v_norm_g, v_final_g, v_lru_w_in, v_lru_conv_w, v_lru_conv_b, v_lru_wa, v_lru_ba, v_lru_wx, v_lru_bx, v_lru_a_param, v_lru_w_out, v_fox_w_in, v_fox_b_f, v_fox_w_out):
    w_loc = dict(norm_g=norm_g, final_g=final_g, lru_w_in=lru_w_in, lru_conv_w=lru_conv_w,
                 lru_conv_b=lru_conv_b, lru_wa=lru_wa, lru_ba=lru_ba, lru_wx=lru_wx, lru_bx=lru_bx,
                 lru_a_param=lru_a_param, lru_w_out=lru_w_out, fox_w_in=fox_w_in, fox_b_f=fox_b_f,
                 fox_w_out=fox_w_out)
    m_loc = dict(norm_g=m_norm_g, final_g=m_final_g, lru_w_in=m_lru_w_in, lru_conv_w=m_lru_conv_w,
                 lru_conv_b=m_lru_conv_b, lru_wa=m_lru_wa, lru_ba=m_lru_ba, lru_wx=m_lru_wx,
                 lru_bx=m_lru_bx, lru_a_param=m_lru_a_param, lru_w_out=m_lru_w_out,
                 fox_w_in=m_fox_w_in, fox_b_f=m_fox_b_f, fox_w_out=m_fox_w_out)
    v_loc = dict(norm_g=v_norm_g, final_g=v_final_g, lru_w_in=v_lru_w_in, lru_conv_w=v_lru_conv_w,
                 lru_conv_b=v_lru_conv_b, lru_wa=v_lru_wa, lru_ba=v_lru_ba, lru_wx=v_lru_wx,
                 lru_bx=v_lru_bx, lru_a_param=v_lru_a_param, lru_w_out=v_lru_w_out,
                 fox_w_in=v_fox_w_in, fox_b_f=v_fox_b_f, fox_w_out=v_fox_w_out)

    w_in8, conv8, w_out8, fox_in8, fox_out8 = _gather_two_level(
        [lru_w_in[0].astype(BF16), lru_conv_w[0], lru_w_out[0].astype(BF16),
         fox_w_in[0].astype(BF16), fox_w_out[0].astype(BF16)], "gather_weights")
    conv_full = jnp.transpose(conv8, (1, 0, 2)).reshape(CONV_WIDTH, LRU_WIDTH)

    loss, grad_x, grads = _local_step(
        x[0], loss_target[0], norm_g, final_g, w_in8, conv_full, lru_conv_b, lru_wa[0], lru_ba,
        lru_wx[0], lru_bx, lru_a_param, w_out8.reshape(LRU_WIDTH, D_MODEL), fox_in8, fox_b_f,
        fox_out8.reshape(HEADS * HEAD_DIM, D_MODEL))

    n_chips = N_DEV // 2
    conv_send = jnp.transpose(grads["lru_conv_w"].reshape(CONV_WIDTH, N_DEV, -1), (1, 0, 2))
    names = ("lru_w_in", "lru_conv_w", "lru_w_out", "fox_w_in", "fox_w_out", "small")
    send = [grads["lru_w_in"], conv_send, grads["lru_w_out"].reshape(N_DEV, -1, D_MODEL),
            grads["fox_w_in"], grads["fox_w_out"].reshape(N_DEV, -1, D_MODEL),
            _pack_small(grads).reshape(N_DEV, SMALL_CHUNK_ROWS, LANES)]
    send = [a.reshape((n_chips, 2) + a.shape[1:]) for a in send]
    got = _swap_sibling(send, "swap_grads")
    core = lax.axis_index("c").astype(jnp.int32).reshape(1)
    chip_sums = [_pair_sum(core, a, b, "pair_sum_" + n) for n, a, b in zip(names, send, got)]
    r_w_in, r_conv, r_w_out, r_fox_in, r_fox_out, r_small = _exchange_chips(
        chip_sums, "scatter_grads")

    out = {}
    for n, recv, tr in (("lru_w_in", r_w_in, 256), ("lru_conv_w", r_conv, CONV_WIDTH),
                        ("lru_w_out", r_w_out, 96), ("fox_w_in", r_fox_in, 128),
                        ("fox_w_out", r_fox_out, 64)):
        res = _adamw(recv, w_loc[n][0], m_loc[n][0], v_loc[n][0], tr, "adamw_" + n)
        out[n] = [a[None] for a in res]

    g_chunk = _reduce_parts(r_small, "reduce_small_grads")
    g_small, = _exchange([g_chunk], False, "gather_small_grads")
    g_small = g_small.reshape(1, N_DEV * SMALL_CHUNK_ROWS, LANES)
    res = _adamw(g_small, _pack_small(w_loc), _pack_small(m_loc), _pack_small(v_loc),
                 N_DEV * SMALL_CHUNK_ROWS, "adamw_replicated")
    small_out = [_unpack_small(a, w_loc) for a in res]
    for n in SMALL:
        out[n] = [d[n] for d in small_out]

    loss = lax.psum(loss, ("x", "y", "c"))
    return (loss, grad_x[None], *[out[n][0] for n in ALL_WEIGHTS], *[out[n][1] for n in ALL_WEIGHTS],
            *[out[n][2] for n in ALL_WEIGHTS], *[out[n][3] for n in ALL_WEIGHTS])
```

```python
import functools

import jax
import jax.numpy as jnp
from jax import lax
from jax.experimental import pallas as pl
from jax.experimental.pallas import tpu as pltpu

F32 = jnp.float32
BF16 = jnp.bfloat16

D_MODEL = 1024
LRU_WIDTH = 1536
LRU_BLOCKS = 12
LRU_BLOCK_W = 128
CONV_WIDTH = 4
LRU_C = 8.0
HEADS = 16
HEAD_DIM = 64
HEAD_PAD = 128
FOX_PAD = HEADS * HEAD_PAD
HEADS_PER_STEP = 2
EPS = 1e-6
NEG_BIG = -1e30
N_DEV = 8

ADAM_LR = 0.001
ADAM_B1 = 0.9
ADAM_B2 = 0.999
ADAM_EPS = 1e-08
ADAM_WD = 0.01
ADAM_STEP = 10

LANE_RB = 64
LANE_CK = 67
LANE_ONE_V = 64

VMEM_LIMIT_BYTES = 56 * 1024 * 1024
LANES = 128
SUBLANES = 8

LRU_IN_SHARD = 2 * LRU_WIDTH // N_DEV
FOX_IN_COLS = 4 * HEADS * HEAD_DIM + HEADS
FOX_IN_SHARD = FOX_IN_COLS // N_DEV

SMALL_ROWS = (16, 8, 12, 1536, 12, 1536, 12, 12, 1)
SMALL_CHUNK_ROWS = 400
assert sum(SMALL_ROWS) <= N_DEV * SMALL_CHUNK_ROWS


def _params(n_grid_axes=1):
    return pltpu.CompilerParams(
        dimension_semantics=("arbitrary",) * n_grid_axes,
        vmem_limit_bytes=VMEM_LIMIT_BYTES)


def _const_spec(shape):
    nd = len(shape)
    return pl.BlockSpec(shape, lambda *_: (0,) * nd, pipeline_mode=pl.Buffered(1))


def _shift_down(x, k, fill):
    rows = lax.broadcasted_iota(jnp.int32, x.shape, 0)
    return jnp.where(rows >= k, pltpu.roll(x, k, 0), fill)


def _shift_up(x, k, fill):
    n = x.shape[0]
    rows = lax.broadcasted_iota(jnp.int32, x.shape, 0)
    return jnp.where(rows < n - k, pltpu.roll(x, n - k, 0), fill)


def _scan_rows(a, b, reverse=False):
    n = a.shape[0]
    shift = _shift_up if reverse else _shift_down
    k = 1
    while k < n:
        b = a * shift(b, k, 0.0) + b
        a = a * shift(a, k, 1.0)
        k *= 2
    return a, b


def _cumsum_rows(x, reverse=False):
    n = x.shape[0]
    shift = _shift_up if reverse else _shift_down
    k = 1
    while k < n:
        x = x + shift(x, k, 0.0)
        k *= 2
    return x


def _rstd(x):
    return lax.rsqrt(jnp.mean(x * x, axis=-1, keepdims=True) + EPS)


def _norm_bwd(x, g, dh):
    rstd = _rstd(x)
    xhat = x * rstd
    dg = jnp.sum(dh * xhat, axis=0, keepdims=True)
    dxh = dh * g
    dx = rstd * (dxh - xhat * jnp.mean(dxh * xhat, axis=-1, keepdims=True))
    return dx, dg


def _split3(x):
    hi = x.astype(BF16)
    r1 = x - hi.astype(F32)
    mid = r1.astype(BF16)
    lo = (r1 - mid.astype(F32)).astype(BF16)
    return hi, mid, lo


def _sigmoid(x):
    return jax.nn.sigmoid(x)


def _dot(a, b):
    return jnp.dot(a, b, preferred_element_type=F32)


def _dot_nt(a, b):
    return lax.dot_general(a, b, (((1,), (1,)), ((), ())), preferred_element_type=F32)


def _dot_tn(a, b):
    return lax.dot_general(a, b, (((0,), (0,)), ((), ())), preferred_element_type=F32)


def _conv_taps(xb, prev8):
    rows8 = lax.broadcasted_iota(jnp.int32, prev8.shape, 0)
    taps = [xb]
    for j in range(1, CONV_WIDTH):
        r = pltpu.roll(xb, j, 0)
        p = pltpu.roll(prev8, j, 0)
        head = jnp.where(rows8 < j, p, r[0:SUBLANES])
        taps.append(jnp.concatenate([head, r[SUBLANES:]], axis=0))
    return taps


def _lru_pre(taps, cw, cb, wa_ref, ba, wx_ref, bx, a_param):
    xc = cb + cw[3:4] * taps[0] + cw[2:3] * taps[1] + cw[1:2] * taps[2] + cw[0:1] * taps[3]
    xcb = xc.astype(BF16)
    ra, ia = [], []
    for n in range(LRU_BLOCKS):
        blk = xcb[:, n * LRU_BLOCK_W:(n + 1) * LRU_BLOCK_W]
        ra.append(_dot(blk, wa_ref[n]))
        ia.append(_dot(blk, wx_ref[n]))
    r = _sigmoid(jnp.concatenate(ra, axis=1) + ba)
    i = _sigmoid(jnp.concatenate(ia, axis=1) + bx)
    z = -a_param
    sp = jnp.maximum(z, 0.0) + jnp.log1p(jnp.exp(-jnp.abs(z)))
    log_a = (-LRU_C) * r * sp
    a = jnp.exp(log_a)
    one_minus_a2 = -jnp.tanh(log_a) * (a * a + 1.0)
    mult = jnp.sqrt(one_minus_a2)
    return xc, xcb, r, i, sp, a, mult


def _lru_in_fwd(x, g0, w_in, ts):
    s = x.shape[0]
    half = N_DEV // 2

    def body(x_ref, g_ref, w_ref, xb_ref, gate_ref, h_ref):
        xv = x_ref[...]
        h = (xv * _rstd(xv) * g_ref[...]).astype(BF16)
        u = [_dot(h, w_ref[j]) for j in range(N_DEV)]
        xb_ref[...] = jnp.concatenate(u[:half], axis=1)
        gate_ref[...] = jnp.concatenate(u[half:], axis=1)
        h_ref[...] = h

    return pl.pallas_call(
        body, name="lru_in_fwd", grid=(s // ts,),
        in_specs=[pl.BlockSpec((ts, D_MODEL), lambda i: (i, 0)),
                  _const_spec((1, D_MODEL)),
                  _const_spec((N_DEV, D_MODEL, LRU_IN_SHARD))],
        out_specs=[pl.BlockSpec((ts, LRU_WIDTH), lambda i: (i, 0)),
                   pl.BlockSpec((ts, LRU_WIDTH), lambda i: (i, 0)),
                   pl.BlockSpec((ts, D_MODEL), lambda i: (i, 0))],
        out_shape=[jax.ShapeDtypeStruct((s, LRU_WIDTH), F32),
                   jax.ShapeDtypeStruct((s, LRU_WIDTH), F32),
                   jax.ShapeDtypeStruct((s, D_MODEL), BF16)],
        compiler_params=_params(),
    )(x, g0, w_in)


def _lru_core_fwd(xb, gate, cw, cb, wa, ba, wx, bx, a_param, ts):
    s = xb.shape[0]

    def body(xb_ref, gate_ref, cw_ref, cb_ref, wa_ref, ba_ref, wx_ref, bx_ref, ap_ref,
             y_ref, hs_ref, prev_ref, hcar_ref):
        @pl.when(pl.program_id(0) == 0)
        def _():
            prev_ref[...] = jnp.zeros_like(prev_ref)
            hcar_ref[...] = jnp.zeros_like(hcar_ref)

        xbv = xb_ref[...]
        taps = _conv_taps(xbv, prev_ref[...])
        xc, _, _, i, _, a, mult = _lru_pre(taps, cw_ref[...], cb_ref[...], wa_ref, ba_ref[...],
                                           wx_ref, bx_ref[...], ap_ref[...])
        bterm = mult * (i * xc)
        cum_a, hloc = _scan_rows(a, bterm)
        hs = cum_a * hcar_ref[SUBLANES - 1:SUBLANES, :] + hloc
        gv = gate_ref[...]
        y_ref[...] = (hs * (gv * _sigmoid(gv))).astype(BF16)
        hs_ref[...] = hs
        prev_ref[...] = xbv[ts - SUBLANES:, :]
        hcar_ref[...] = hs[ts - SUBLANES:, :]

    vec = _const_spec((1, LRU_WIDTH))
    blk = _const_spec((LRU_BLOCKS, LRU_BLOCK_W, LRU_BLOCK_W))
    tile = pl.BlockSpec((ts, LRU_WIDTH), lambda i: (i, 0))
    return pl.pallas_call(
        body, name="lru_core_fwd", grid=(s // ts,),
        in_specs=[tile, tile, _const_spec((CONV_WIDTH, LRU_WIDTH)), vec, blk, vec, blk, vec, vec],
        out_specs=[tile, tile],
        out_shape=[jax.ShapeDtypeStruct((s, LRU_WIDTH), BF16),
                   jax.ShapeDtypeStruct((s, LRU_WIDTH), F32)],
        scratch_shapes=[pltpu.VMEM((SUBLANES, LRU_WIDTH), F32),
                        pltpu.VMEM((SUBLANES, LRU_WIDTH), F32)],
        compiler_params=_params(),
    )(xb, gate, cw, cb, wa, ba, wx, bx, a_param)


def _fox_pre_fwd(x, y, w_out, g1, wf, bf, ts):
    s = x.shape[0]

    def body(x_ref, y_ref, w_ref, g_ref, wf_ref, bf_ref, x1_ref, h1_ref, f_ref, cp_ref, ccar_ref):
        @pl.when(pl.program_id(0) == 0)
        def _():
            ccar_ref[...] = jnp.zeros_like(ccar_ref)

        x1 = x_ref[...] + _dot(y_ref[...], w_ref[...])
        h1 = (x1 * _rstd(x1) * g_ref[...]).astype(BF16)
        f = _dot(h1, wf_ref[...]) + bf_ref[...]
        logsig = jnp.minimum(f, 0.0) - jnp.log1p(jnp.exp(-jnp.abs(f)))
        cum = _cumsum_rows(logsig) + ccar_ref[SUBLANES - 1:SUBLANES, :]
        hi, mid, lo = _split3(cum)
        x1_ref[...] = x1
        h1_ref[...] = h1
        f_ref[...] = f
        cp_ref[...] = jnp.concatenate([hi, mid, lo], axis=1)
        ccar_ref[...] = cum[ts - SUBLANES:, :]

    return pl.pallas_call(
        body, name="fox_pre_fwd", grid=(s // ts,),
        in_specs=[pl.BlockSpec((ts, D_MODEL), lambda i: (i, 0)),
                  pl.BlockSpec((ts, LRU_WIDTH), lambda i: (i, 0)),
                  _const_spec((LRU_WIDTH, D_MODEL)),
                  _const_spec((1, D_MODEL)),
                  _const_spec((D_MODEL, LANES)),
                  _const_spec((1, LANES))],
        out_specs=[pl.BlockSpec((ts, D_MODEL), lambda i: (i, 0)),
                   pl.BlockSpec((ts, D_MODEL), lambda i: (i, 0)),
                   pl.BlockSpec((ts, LANES), lambda i: (i, 0)),
                   pl.BlockSpec((ts, 3 * LANES), lambda i: (i, 0))],
        out_shape=[jax.ShapeDtypeStruct((s, D_MODEL), F32),
                   jax.ShapeDtypeStruct((s, D_MODEL), BF16),
                   jax.ShapeDtypeStruct((s, LANES), F32),
                   jax.ShapeDtypeStruct((s, 3 * LANES), BF16)],
        scratch_shapes=[pltpu.VMEM((SUBLANES, LANES), F32)],
        compiler_params=_params(),
    )(x, y, w_out, g1, wf, bf)


def _fox_proj_fwd(h1, cparts, w, sel, bias, out_dtype, ts, name):
    s = h1.shape[0]
    ng = w.shape[0]
    use_sel = sel is not None

    def body(*refs):
        if use_sel:
            h_ref, cp_ref, w_ref, sel_ref, b_ref, o_ref = refs
            acc = _dot(h_ref[...], w_ref[...]) + _dot(cp_ref[...], sel_ref[...]) + b_ref[...]
        else:
            h_ref, w_ref, o_ref = refs
            acc = _dot(h_ref[...], w_ref[...])
        o_ref[...] = acc.astype(out_dtype)

    in_specs = [pl.BlockSpec((ts, D_MODEL), lambda j, i: (i, 0))]
    args = [h1]
    if use_sel:
        in_specs.append(pl.BlockSpec((ts, 3 * LANES), lambda j, i: (i, 0)))
        args.append(cparts)
    in_specs.append(pl.BlockSpec((None, D_MODEL, FOX_PAD), lambda j, i: (j, 0, 0)))
    args.append(w)
    if use_sel:
        in_specs.append(pl.BlockSpec((None, 3 * LANES, FOX_PAD), lambda j, i: (j, 0, 0)))
        in_specs.append(pl.BlockSpec((None, 1, FOX_PAD), lambda j, i: (j, 0, 0)))
        args += [sel, bias]
    return pl.pallas_call(
        body, name=name, grid=(ng, s // ts),
        in_specs=in_specs,
        out_specs=pl.BlockSpec((None, ts, FOX_PAD), lambda j, i: (j, i, 0)),
        out_shape=jax.ShapeDtypeStruct((ng, s, FOX_PAD), out_dtype),
        compiler_params=_params(2),
    )(*args)


def _attn_fwd(qkv, blk):
    s = qkv.shape[1]
    nblk = s // blk
    wide = 2 * blk
    heads = [slice(i * HEAD_PAD, (i + 1) * HEAD_PAD) for i in range(HEADS_PER_STEP)]

    def body(q_ref, k_ref, v_ref, o_ref, qb_ref, acc_ref, m_ref):
        qi = pl.program_id(1)
        row = lax.broadcasted_iota(jnp.int32, (blk, blk), 0)
        col = lax.broadcasted_iota(jnp.int32, (blk, blk), 1)
        lane = lax.broadcasted_iota(jnp.int32, (blk, HEAD_PAD), 1)
        qs = [q_ref[:, hd] for hd in heads]
        for i in range(HEADS_PER_STEP):
            acc_ref[i] = jnp.zeros((blk, HEAD_PAD), F32)
            m_ref[i] = jnp.full((blk, 1), NEG_BIG, F32)

        def step(k0, size, masked):
            scores = [_dot_nt(q, k_ref[pl.ds(k0, size), hd]) for q, hd in zip(qs, heads)]
            for i, (sc, hd) in enumerate(zip(scores, heads)):
                v = v_ref[pl.ds(k0, size), hd]
                if masked:
                    sc = jnp.where(col <= row, sc, NEG_BIG)
                m = m_ref[i]
                m_new = jnp.maximum(m, jnp.max(sc, axis=-1, keepdims=True))
                p = jnp.exp((sc - m_new).astype(BF16))
                acc_ref[i] = jnp.exp(m - m_new) * acc_ref[i] + _dot(p, v)
                m_ref[i] = m_new

        def wide_step(kk, _):
            step(pl.multiple_of(kk * wide, wide), wide, False)
            return 0

        lax.fori_loop(0, qi // 2, wide_step, 0)

        @pl.when(qi % 2 == 1)
        def _():
            step(pl.multiple_of((qi - 1) * blk, blk), blk, False)

        step(pl.multiple_of(qi * blk, blk), blk, True)
        for i, (q, hd) in enumerate(zip(qs, heads)):
            acc = acc_ref[i]
            l = acc[:, LANE_ONE_V:LANE_ONE_V + 1]
            o_ref[:, hd] = (acc / l).astype(BF16)
            qf = q.astype(F32)
            cq = (qf[:, LANE_RB:LANE_RB + 1] + qf[:, LANE_RB + 1:LANE_RB + 2]
                  + qf[:, LANE_RB + 2:LANE_RB + 3])
            hi, mid, lo = _split3(cq - (m_ref[i] + jnp.log(l)))
            qb_ref[:, hd] = jnp.where(lane == LANE_RB, hi, jnp.where(
                lane == LANE_RB + 1, mid, jnp.where(lane == LANE_RB + 2, lo, q)))

    width = HEADS_PER_STEP * HEAD_PAD

    def whole(j):
        return pl.BlockSpec((None, s, width), lambda h, i: (j, 0, h))

    out_spec = pl.BlockSpec((blk, width), lambda h, i: (i, h))
    return pl.pallas_call(
        body, name="attn_fwd", grid=(HEADS // HEADS_PER_STEP, nblk),
        in_specs=[pl.BlockSpec((None, blk, width), lambda h, i: (0, i, h)), whole(1), whole(2)],
        out_specs=[out_spec, out_spec],
        out_shape=[jax.ShapeDtypeStruct((s, FOX_PAD), BF16),
                   jax.ShapeDtypeStruct((s, FOX_PAD), BF16)],
        scratch_shapes=[pltpu.VMEM((HEADS_PER_STEP, blk, HEAD_PAD), F32),
                        pltpu.VMEM((HEADS_PER_STEP, blk, 1), F32)],
        compiler_params=_params(2),
    )(qkv, qkv, qkv)


def _fox_out_loss(o, gate, w_out, x1, target, gf, ts):
    s = x1.shape[0]

    def body(o_ref, gt_ref, w_ref, x1_ref, t_ref, g_ref, dx2_ref, y2_ref, loss_ref, gfin_ref):
        @pl.when(pl.program_id(0) == 0)
        def _():
            loss_ref[...] = jnp.zeros_like(loss_ref)
            gfin_ref[...] = jnp.zeros_like(gfin_ref)

        gv = gt_ref[...]
        y2 = (o_ref[...] * (gv * _sigmoid(gv))).astype(BF16)
        x2 = x1_ref[...] + _dot(y2, w_ref[...])
        rstd = _rstd(x2)
        xhat = x2 * rstd
        g = g_ref[...]
        diff = xhat * g - t_ref[...]
        loss_ref[...] += 0.5 * jnp.sum(jnp.mean(diff * diff, axis=-1, keepdims=True))
        dy = diff * (1.0 / D_MODEL)
        gfin_ref[...] += jnp.sum(dy * xhat, axis=0, keepdims=True)
        dxh = dy * g
        dx2_ref[...] = rstd * (dxh - xhat * jnp.mean(dxh * xhat, axis=-1, keepdims=True))
        y2_ref[...] = y2

    return pl.pallas_call(
        body, name="fox_out_loss", grid=(s // ts,),
        in_specs=[pl.BlockSpec((ts, FOX_PAD), lambda i: (i, 0)),
                  pl.BlockSpec((ts, FOX_PAD), lambda i: (i, 0)),
                  _const_spec((FOX_PAD, D_MODEL)),
                  pl.BlockSpec((ts, D_MODEL), lambda i: (i, 0)),
                  pl.BlockSpec((ts, D_MODEL), lambda i: (i, 0)),
                  _const_spec((1, D_MODEL))],
        out_specs=[pl.BlockSpec((ts, D_MODEL), lambda i: (i, 0)),
                   pl.BlockSpec((ts, FOX_PAD), lambda i: (i, 0)),
                   pl.BlockSpec((SUBLANES, LANES), lambda i: (0, 0)),
                   pl.BlockSpec((1, D_MODEL), lambda i: (0, 0))],
        out_shape=[jax.ShapeDtypeStruct((s, D_MODEL), F32),
                   jax.ShapeDtypeStruct((s, FOX_PAD), BF16),
                   jax.ShapeDtypeStruct((SUBLANES, LANES), F32),
                   jax.ShapeDtypeStruct((1, D_MODEL), F32)],
        compiler_params=_params(),
    )(o, gate, w_out, x1, target, gf)


def _fox_out_bwd(dx2, w_out, o, gate, ts):
    s = dx2.shape[0]

    def body(dx_ref, w_ref, o_ref, gt_ref, do_ref, dg_ref):
        lane = lax.broadcasted_iota(jnp.int32, (ts, HEAD_PAD), 1)
        dy2 = _dot_nt(dx_ref[...].astype(BF16), w_ref[...])
        gv = gt_ref[...]
        sg = _sigmoid(gv)
        ov = o_ref[...]
        dov = dy2 * (gv * sg)
        dg_ref[...] = (dy2 * ov * (sg * (1.0 + gv * (1.0 - sg)))).astype(BF16)
        prod = dov * ov
        for h in range(HEADS):
            sl = slice(h * HEAD_PAD, (h + 1) * HEAD_PAD)
            delta = jnp.sum(prod[:, sl], axis=-1, keepdims=True)
            hi = delta.astype(BF16)
            lo = (delta - hi.astype(F32)).astype(BF16)
            do_h = dov[:, sl].astype(BF16)
            do_ref[:, sl] = jnp.where(lane == LANE_ONE_V, -hi,
                                      jnp.where(lane == LANE_ONE_V + 1, -lo, do_h))

    tile = pl.BlockSpec((ts, FOX_PAD), lambda i: (i, 0))
    return pl.pallas_call(
        body, name="fox_out_bwd", grid=(s // ts,),
        in_specs=[pl.BlockSpec((ts, D_MODEL), lambda i: (i, 0)),
                  _const_spec((FOX_PAD, D_MODEL)), tile, tile],
        out_specs=[tile, tile],
        out_shape=[jax.ShapeDtypeStruct((s, FOX_PAD), BF16),
                   jax.ShapeDtypeStruct((s, FOX_PAD), BF16)],
        compiler_params=_params(),
    )(dx2, w_out, o, gate)


def _attn_bwd(qb, qkv, do, blk):
    s = qb.shape[0]
    nblk = s // blk
    heads = [slice(i * HEAD_PAD, (i + 1) * HEAD_PAD) for i in range(HEADS_PER_STEP)]

    def body(q_ref, k_ref, v_ref, do_ref, dq_ref, dk_ref, dv_ref, dcum_ref, dq_acc, dk_acc,
             dv_acc):
        group = pl.program_id(0)
        kj = pl.program_id(1)
        row = lax.broadcasted_iota(jnp.int32, (blk, blk), 0)
        col = lax.broadcasted_iota(jnp.int32, (blk, blk), 1)
        lane = lax.broadcasted_iota(jnp.int32, (blk, LANES), 1)
        mine = [lane == group * HEADS_PER_STEP + i for i in range(HEADS_PER_STEP)]

        @pl.when(kj == 0)
        def _():
            dq_acc[...] = jnp.zeros_like(dq_acc)

        @pl.when((group == 0) & (kj == 0))
        def _():
            dcum_ref[...] = jnp.zeros_like(dcum_ref)

        k0 = pl.multiple_of(kj * blk, blk)
        ks = [k_ref[:, hd] for hd in heads]
        vs = [v_ref[:, hd] for hd in heads]

        dk_acc[...] = jnp.zeros_like(dk_acc)
        dv_acc[...] = jnp.zeros_like(dv_acc)

        def step(q0, masked):
            qs = [q_ref[pl.ds(q0, blk), hd] for hd in heads]
            dos = [do_ref[pl.ds(q0, blk), hd] for hd in heads]
            scores = [_dot_nt(q, k) for q, k in zip(qs, ks)]
            dps = [_dot_nt(dov, v) for dov, v in zip(dos, vs)]
            for i, (hd, k, q, dov, sc, dp) in enumerate(zip(heads, ks, qs, dos, scores, dps)):
                p = jnp.exp(sc.astype(BF16))
                if masked:
                    p = jnp.where(col <= row, p, jnp.zeros_like(p))
                ds = (p.astype(F32) * dp).astype(BF16)
                dv_acc[i] += _dot_tn(p, dov)
                dk_acc[i] += _dot_tn(ds, q)
                dq_acc[pl.ds(q0, blk), hd] += _dot(ds, k)

        step(k0, True)

        def q_step(qi, _):
            step(pl.multiple_of(qi * blk, blk), False)
            return 0

        lax.fori_loop(kj + 1, nblk, q_step, 0)
        dcum = dcum_ref[pl.ds(k0, blk), :]
        for i, (hd, mask) in enumerate(zip(heads, mine)):
            dk = dk_acc[i]
            dk_ref[:, hd] = dk.astype(BF16)
            dv_ref[:, hd] = dv_acc[i].astype(BF16)
            dcum = jnp.where(mask, -dk[:, LANE_CK:LANE_CK + 1], dcum)
        dcum_ref[pl.ds(k0, blk), :] = dcum

        @pl.when(kj == nblk - 1)
        def _():
            def finish(bi, _):
                r0 = pl.multiple_of(bi * blk, blk)
                dcum = dcum_ref[pl.ds(r0, blk), :]
                for hd, mask in zip(heads, mine):
                    dq = dq_acc[pl.ds(r0, blk), hd]
                    dq_ref[pl.ds(r0, blk), hd] = dq.astype(BF16)
                    dcum = dcum + jnp.where(mask, dq[:, LANE_RB:LANE_RB + 1], 0.0)
                dcum_ref[pl.ds(r0, blk), :] = dcum
                return 0

            lax.fori_loop(0, nblk, finish, 0)

    width = HEADS_PER_STEP * HEAD_PAD
    whole = pl.BlockSpec((s, width), lambda h, j: (0, h))
    part = pl.BlockSpec((blk, width), lambda h, j: (j, h))
    out = jax.ShapeDtypeStruct((s, FOX_PAD), BF16)
    return pl.pallas_call(
        body, name="attn_bwd", grid=(HEADS // HEADS_PER_STEP, nblk),
        in_specs=[whole,
                  pl.BlockSpec((None, blk, width), lambda h, j: (1, j, h)),
                  pl.BlockSpec((None, blk, width), lambda h, j: (2, j, h)),
                  whole],
        out_specs=[whole, part, part, pl.BlockSpec((s, LANES), lambda h, j: (0, 0))],
        out_shape=[out, out, out, jax.ShapeDtypeStruct((s, LANES), F32)],
        scratch_shapes=[pltpu.VMEM((s, width), F32),
                        pltpu.VMEM((HEADS_PER_STEP, blk, HEAD_PAD), F32),
                        pltpu.VMEM((HEADS_PER_STEP, blk, HEAD_PAD), F32)],
        compiler_params=_params(2),
    )(qb, qkv, qkv, do)


def _fox_in_bwd(dq, dk, dv, dg, wt, wft, dcum, f, x1, dx2, g1, ts):
    s = x1.shape[0]
    nt = s // ts

    def body(dq_ref, dk_ref, dv_ref, dg_ref, wt_ref, wft_ref, dcum_ref, f_ref, x1_ref, dx2_ref,
             g_ref, dx1_ref, dx1b_ref, df_ref, gn_ref, gbf_ref, rcar_ref):
        @pl.when(pl.program_id(0) == 0)
        def _():
            rcar_ref[...] = jnp.zeros_like(rcar_ref)
            gn_ref[...] = jnp.zeros_like(gn_ref)
            gbf_ref[...] = jnp.zeros_like(gbf_ref)

        dkv = dk_ref[...]
        rsum = _cumsum_rows(dcum_ref[...], reverse=True) + rcar_ref[0:1, :]
        df = rsum * _sigmoid(-f_ref[...])
        dfb = df.astype(BF16)
        dh = (_dot_nt(dq_ref[...], wt_ref[0]) + _dot_nt(dkv, wt_ref[1])
              + _dot_nt(dv_ref[...], wt_ref[2]) + _dot_nt(dg_ref[...], wt_ref[3])
              + _dot_nt(dfb, wft_ref[...]))
        dxn, dgn = _norm_bwd(x1_ref[...], g_ref[...], dh)
        dx1 = dx2_ref[...] + dxn
        dx1_ref[...] = dx1
        dx1b_ref[...] = dx1.astype(BF16)
        df_ref[...] = dfb
        gn_ref[...] += dgn
        gbf_ref[...] += jnp.sum(df, axis=0, keepdims=True)
        rcar_ref[...] = rsum[0:SUBLANES, :]

    rev = lambda i: (nt - 1 - i, 0)
    wide = pl.BlockSpec((ts, FOX_PAD), rev)
    return pl.pallas_call(
        body, name="fox_in_bwd", grid=(nt,),
        in_specs=[wide, wide, wide, wide,
                  _const_spec((4, D_MODEL, FOX_PAD)),
                  _const_spec((D_MODEL, LANES)),
                  pl.BlockSpec((ts, LANES), rev),
                  pl.BlockSpec((ts, LANES), rev),
                  pl.BlockSpec((ts, D_MODEL), rev),
                  pl.BlockSpec((ts, D_MODEL), rev),
                  _const_spec((1, D_MODEL))],
        out_specs=[pl.BlockSpec((ts, D_MODEL), rev),
                   pl.BlockSpec((ts, D_MODEL), rev),
                   pl.BlockSpec((ts, LANES), rev),
                   pl.BlockSpec((1, D_MODEL), lambda i: (0, 0)),
                   pl.BlockSpec((1, LANES), lambda i: (0, 0))],
        out_shape=[jax.ShapeDtypeStruct((s, D_MODEL), F32),
                   jax.ShapeDtypeStruct((s, D_MODEL), BF16),
                   jax.ShapeDtypeStruct((s, LANES), BF16),
                   jax.ShapeDtypeStruct((1, D_MODEL), F32),
                   jax.ShapeDtypeStruct((1, LANES), F32)],
        scratch_shapes=[pltpu.VMEM((SUBLANES, LANES), F32)],
        compiler_params=_params(),
    )(dq, dk, dv, dg, wt, wft, dcum, f, x1, dx2, g1)


def _lru_core_bwd(dx1b, w_out, xb, gate, hs, cw, cb, wa, ba, wx, bx, a_param, wa_t, wx_t, ts):
    s = xb.shape[0]
    nt = s // ts
    tpb = ts // SUBLANES

    def body(dx_ref, wo_ref, xb_ref, xbh_ref, gate_ref, hs_ref, hsh_ref, cw_ref, cb_ref, wa_ref,
             ba_ref, wx_ref, bx_ref, ap_ref, wat_ref, wxt_ref,
             du_ref, gwa_ref, gwx_ref, gvec_ref, acar_ref, dhcar_ref, dxccar_ref):
        step = pl.program_id(0)

        @pl.when(step == 0)
        def _():
            acar_ref[...] = jnp.zeros_like(acar_ref)
            dhcar_ref[...] = jnp.zeros_like(dhcar_ref)
            dxccar_ref[...] = jnp.zeros_like(dxccar_ref)
            gwa_ref[...] = jnp.zeros_like(gwa_ref)
            gwx_ref[...] = jnp.zeros_like(gwx_ref)
            gvec_ref[...] = jnp.zeros_like(gvec_ref)

        first_tile = step == nt - 1
        halo_on = jnp.where(first_tile, 0.0, 1.0)
        prev8 = xbh_ref[...] * halo_on
        hprev_row = hsh_ref[SUBLANES - 1:SUBLANES, :] * halo_on

        xbv = xb_ref[...]
        taps = _conv_taps(xbv, prev8)
        cw_v = cw_ref[...]
        xc, xcb, r, i, sp, a, mult = _lru_pre(taps, cw_v, cb_ref[...], wa_ref, ba_ref[...],
                                              wx_ref, bx_ref[...], ap_ref[...])
        hs = hs_ref[...]
        gv = gate_ref[...]
        sg = _sigmoid(gv)
        dy = _dot_nt(dx_ref[...], wo_ref[...])
        dhs = dy * (gv * sg)
        dgate = dy * hs * (sg * (1.0 + gv * (1.0 - sg)))

        rows = lax.broadcasted_iota(jnp.int32, a.shape, 0)
        a_next = jnp.where(rows < ts - 1, pltpu.roll(a, ts - 1, 0), acar_ref[0:1, :])
        cum_a, dh_loc = _scan_rows(a_next, dhs, reverse=True)
        dh = cum_a * dhcar_ref[0:1, :] + dh_loc
        h_prev = jnp.where(rows >= 1, pltpu.roll(hs, 1, 0), hprev_row)

        da = dh * h_prev
        ixc = i * xc
        dmult = dh * ixc
        di = dh * mult * xc
        dxc = dh * mult * i
        dlog_a = da * a - dmult * (a * a) / mult
        dr = dlog_a * ((-LRU_C) * sp)
        dsp = jnp.sum(dlog_a * ((-LRU_C) * r), axis=0, keepdims=True)
        dra = dr * r * (1.0 - r)
        dia = di * i * (1.0 - i)
        drab = dra.astype(BF16)
        diab = dia.astype(BF16)
        back = []
        for n in range(LRU_BLOCKS):
            sl = slice(n * LRU_BLOCK_W, (n + 1) * LRU_BLOCK_W)
            gwa_ref[n] += _dot_tn(xcb[:, sl], drab[:, sl])
            gwx_ref[n] += _dot_tn(xcb[:, sl], diab[:, sl])
            back.append(_dot(drab[:, sl], wat_ref[n]) + _dot(diab[:, sl], wxt_ref[n]))
        dxc = dxc + jnp.concatenate(back, axis=1)

        nxt8 = dxccar_ref[...]
        rows8 = lax.broadcasted_iota(jnp.int32, nxt8.shape, 0)
        dxb = cw_v[3:4] * dxc
        for j in range(1, CONV_WIDTH):
            rj = pltpu.roll(dxc, ts - j, 0)
            pj = pltpu.roll(nxt8, SUBLANES - j, 0)
            tail = jnp.where(rows8 >= SUBLANES - j, pj, rj[ts - SUBLANES:])
            dxb = dxb + cw_v[3 - j:4 - j] * jnp.concatenate([rj[:ts - SUBLANES], tail], axis=0)

        du_ref[:, :LRU_WIDTH] = dxb.astype(BF16)
        du_ref[:, LRU_WIDTH:] = dgate.astype(BF16)

        z = -ap_ref[...]
        gvec = [jnp.sum(dxc * taps[3 - k], axis=0, keepdims=True) for k in range(CONV_WIDTH)]
        gvec.append(jnp.sum(dxc, axis=0, keepdims=True))
        gvec.append(jnp.sum(dra, axis=0, keepdims=True))
        gvec.append(jnp.sum(dia, axis=0, keepdims=True))
        gvec.append(-dsp * _sigmoid(z))
        gvec_ref[...] += jnp.concatenate(gvec, axis=0)

        acar_ref[...] = a[0:SUBLANES, :]
        dhcar_ref[...] = dh[0:SUBLANES, :]
        dxccar_ref[...] = dxc[0:SUBLANES, :]

    rev = lambda i: (nt - 1 - i, 0)
    halo = lambda i: (jnp.maximum((nt - 1 - i) * tpb - 1, 0), 0)
    tile = pl.BlockSpec((ts, LRU_WIDTH), rev)
    halo_spec = pl.BlockSpec((SUBLANES, LRU_WIDTH), halo)
    vec = _const_spec((1, LRU_WIDTH))
    blk = _const_spec((LRU_BLOCKS, LRU_BLOCK_W, LRU_BLOCK_W))
    acc_blk = pl.BlockSpec((LRU_BLOCKS, LRU_BLOCK_W, LRU_BLOCK_W), lambda i: (0, 0, 0))
    return pl.pallas_call(
        body, name="lru_core_bwd", grid=(nt,),
        in_specs=[pl.BlockSpec((ts, D_MODEL), rev),
                  _const_spec((LRU_WIDTH, D_MODEL)),
                  tile, halo_spec, tile, tile, halo_spec,
                  _const_spec((CONV_WIDTH, LRU_WIDTH)), vec, blk, vec, blk, vec, vec, blk, blk],
        out_specs=[pl.BlockSpec((ts, 2 * LRU_WIDTH), rev), acc_blk, acc_blk,
                   pl.BlockSpec((SUBLANES, LRU_WIDTH), lambda i: (0, 0))],
        out_shape=[jax.ShapeDtypeStruct((s, 2 * LRU_WIDTH), BF16),
                   jax.ShapeDtypeStruct((LRU_BLOCKS, LRU_BLOCK_W, LRU_BLOCK_W), F32),
                   jax.ShapeDtypeStruct((LRU_BLOCKS, LRU_BLOCK_W, LRU_BLOCK_W), F32),
                   jax.ShapeDtypeStruct((SUBLANES, LRU_WIDTH), F32)],
        scratch_shapes=[pltpu.VMEM((SUBLANES, LRU_WIDTH), F32),
                        pltpu.VMEM((SUBLANES, LRU_WIDTH), F32),
                        pltpu.VMEM((SUBLANES, LRU_WIDTH), F32)],
        compiler_params=_params(),
    )(dx1b, w_out, xb, xb, gate, hs, hs, cw, cb, wa, ba, wx, bx, a_param, wa_t, wx_t)


def _lru_in_bwd(du, w_in, x, dx1, g0, ts):
    s = x.shape[0]

    def body(du_ref, w_ref, x_ref, dx1_ref, g_ref, gx_ref, gn_ref):
        @pl.when(pl.program_id(0) == 0)
        def _():
            gn_ref[...] = jnp.zeros_like(gn_ref)

        duv = du_ref[...]
        dh = _dot_nt(duv[:, 0:LRU_IN_SHARD], w_ref[0])
        for j in range(1, N_DEV):
            dh = dh + _dot_nt(duv[:, j * LRU_IN_SHARD:(j + 1) * LRU_IN_SHARD], w_ref[j])
        dxn, dgn = _norm_bwd(x_ref[...], g_ref[...], dh)
        gx_ref[...] = dx1_ref[...] + dxn
        gn_ref[...] += dgn

    tile = pl.BlockSpec((ts, D_MODEL), lambda i: (i, 0))
    return pl.pallas_call(
        body, name="lru_in_bwd", grid=(s // ts,),
        in_specs=[pl.BlockSpec((ts, 2 * LRU_WIDTH), lambda i: (i, 0)),
                  _const_spec((N_DEV, D_MODEL, LRU_IN_SHARD)), tile, tile,
                  _const_spec((1, D_MODEL))],
        out_specs=[tile, pl.BlockSpec((1, D_MODEL), lambda i: (0, 0))],
        out_shape=[jax.ShapeDtypeStruct((s, D_MODEL), F32),
                   jax.ShapeDtypeStruct((1, D_MODEL), F32)],
        compiler_params=_params(),
    )(du, w_in, x, dx1, g0)


def _weight_grad(a, b, ts, name, scale=1.0, col_shards=1):
    s, ka = a.shape
    nb = b.shape[1]
    nt = s // ts
    per = nb // col_shards

    def body(a_ref, b_ref, o_ref):
        @pl.when(pl.program_id(0) == 0)
        def _():
            o_ref[...] = jnp.zeros_like(o_ref)

        if col_shards == 1:
            o_ref[...] += _dot_tn(a_ref[...], b_ref[...])
        else:
            av, bv = a_ref[...], b_ref[...]
            for j in range(col_shards):
                o_ref[j] += _dot_tn(av, bv[:, j * per:(j + 1) * per])
        if scale != 1.0:
            @pl.when(pl.program_id(0) == nt - 1)
            def _():
                o_ref[...] = o_ref[...] * scale

    out_dims = (ka, nb) if col_shards == 1 else (col_shards, ka, per)
    return pl.pallas_call(
        body, name=name, grid=(nt,),
        in_specs=[pl.BlockSpec((ts, ka), lambda i: (i, 0)),
                  pl.BlockSpec((ts, nb), lambda i: (i, 0))],
        out_specs=pl.BlockSpec(out_dims, lambda i: (0,) * len(out_dims)),
        out_shape=jax.ShapeDtypeStruct(out_dims, F32),
        compiler_params=_params(),
    )(a, b)


def _sum_parts(gp_ref):
    g = gp_ref[0].astype(F32)
    for k in range(1, gp_ref.shape[0]):
        g = g + gp_ref[k].astype(F32)
    return g


def _adamw(g_parts, w, m, v, tr, name):
    nparts, rows, cols = g_parts.shape

    def body(gp_ref, w_ref, m_ref, v_ref, g_ref, d_ref, mo_ref, vo_ref):
        g = _sum_parts(gp_ref)
        m2 = ADAM_B1 * m_ref[...] + (1.0 - ADAM_B1) * g
        v2 = ADAM_B2 * v_ref[...] + (1.0 - ADAM_B2) * (g * g)
        m_hat = m2 / (1.0 - ADAM_B1 ** ADAM_STEP)
        v_hat = v2 / (1.0 - ADAM_B2 ** ADAM_STEP)
        g_ref[...] = g
        d_ref[...] = (-ADAM_LR) * (m_hat / (jnp.sqrt(v_hat) + ADAM_EPS) + ADAM_WD * w_ref[...])
        mo_ref[...] = m2
        vo_ref[...] = v2

    tile = pl.BlockSpec((tr, cols), lambda i: (i, 0))
    out = jax.ShapeDtypeStruct((rows, cols), F32)
    return pl.pallas_call(
        body, name=name, grid=(rows // tr,),
        in_specs=[pl.BlockSpec((nparts, tr, cols), lambda i: (0, i, 0)), tile, tile, tile],
        out_specs=[tile, tile, tile, tile],
        out_shape=[out, out, out, out],
        compiler_params=_params(),
    )(g_parts, w, m, v)


def _reduce_parts(g_parts, name):
    _, rows, cols = g_parts.shape

    def body(gp_ref, g_ref):
        g_ref[...] = _sum_parts(gp_ref)

    return pl.pallas_call(
        body, name=name,
        out_shape=jax.ShapeDtypeStruct((rows, cols), F32),
        compiler_params=pltpu.CompilerParams(vmem_limit_bytes=VMEM_LIMIT_BYTES),
    )(g_parts)


def _mesh_pos():
    ix, iy, ic = lax.axis_index("x"), lax.axis_index("y"), lax.axis_index("c")
    return ix, iy, ic


def _peer(ix, iy, ic, mask):
    px = 1 - ix if mask & 4 else ix
    py = 1 - iy if mask & 2 else iy
    pc = 1 - ic if mask & 1 else ic
    return (px, py, pc), 4 * px + 2 * py + pc


def _exchange(arrays, scatter, name):
    n = len(arrays)

    def body(*refs):
        x_refs, o_refs = refs[:n], refs[n:2 * n]
        send_sems, recv_sems, local_sems = refs[2 * n:]
        ix, iy, ic = _mesh_pos()
        me = 4 * ix + 2 * iy + ic

        def src(a, dest):
            return x_refs[a].at[dest] if scatter else x_refs[a]

        local = [pltpu.make_async_copy(src(a, me), o_refs[a].at[me], local_sems.at[a])
                 for a in range(n)]
        for cp in local:
            cp.start()
        sends = []
        for mask in range(1, N_DEV):
            peer, pidx = _peer(ix, iy, ic, mask)
            for a in range(n):
                cp = pltpu.make_async_remote_copy(
                    src_ref=src(a, pidx), dst_ref=o_refs[a].at[me],
                    send_sem=send_sems.at[a, mask - 1], recv_sem=recv_sems.at[a, mask - 1],
                    device_id=peer, device_id_type=pl.DeviceIdType.MESH)
                cp.start()
                sends.append(cp)
        for mask in range(1, N_DEV):
            peer, pidx = _peer(ix, iy, ic, mask)
            for a in range(n):
                pltpu.make_async_remote_copy(
                    src_ref=src(a, me), dst_ref=o_refs[a].at[pidx],
                    send_sem=send_sems.at[a, mask - 1], recv_sem=recv_sems.at[a, mask - 1],
                    device_id=peer, device_id_type=pl.DeviceIdType.MESH).wait_recv()
        for cp in sends:
            cp.wait_send()
        for cp in local:
            cp.wait()

    out_shape = [jax.ShapeDtypeStruct(x.shape if scatter else (N_DEV,) + x.shape, x.dtype)
                 for x in arrays]
    return pl.pallas_call(
        body, name=name,
        in_specs=[pl.BlockSpec(memory_space=pl.ANY)] * n,
        out_specs=[pl.BlockSpec(memory_space=pl.ANY)] * n,
        out_shape=out_shape,
        scratch_shapes=[pltpu.SemaphoreType.DMA((n, N_DEV - 1)),
                        pltpu.SemaphoreType.DMA((n, N_DEV - 1)),
                        pltpu.SemaphoreType.DMA((n,))],
    )(*arrays)


def _gather_two_level(arrays, name):
    n = len(arrays)

    def body(*refs):
        x_refs, o_refs = refs[:n], refs[n:2 * n]
        send_sems, recv_sems, local_sems = refs[2 * n:]
        ix, iy, ic = _mesh_pos()
        me, sibling = (ix, iy, ic), (ix, iy, 1 - ic)
        chips = [(1 - ix, iy), (ix, 1 - iy), (1 - ix, 1 - iy)]

        def idx(px, py, pc):
            return 4 * px + 2 * py + pc

        def copy(a, k, block, to, src=None):
            dst = o_refs[a].at[idx(*block)]
            return pltpu.make_async_remote_copy(
                src_ref=dst if src is None else src, dst_ref=dst,
                send_sem=send_sems.at[a, k], recv_sem=recv_sems.at[a, k],
                device_id=to, device_id_type=pl.DeviceIdType.MESH)

        local = [pltpu.make_async_copy(x_refs[a], o_refs[a].at[idx(*me)], local_sems.at[a])
                 for a in range(n)]
        for cp in local:
            cp.start()
        first = []
        for a in range(n):
            first.append(copy(a, 0, me, sibling, src=x_refs[a]))
            first += [copy(a, 1 + j, me, (*chip, ic), src=x_refs[a])
                      for j, chip in enumerate(chips)]
        for cp in first:
            cp.start()
        passed = []
        for j, chip in enumerate(chips):
            for a in range(n):
                copy(a, 1 + j, (*chip, ic), me).wait_recv()
                cp = copy(a, 4 + j, (*chip, ic), sibling)
                cp.start()
                passed.append(cp)
        for a in range(n):
            copy(a, 0, sibling, me).wait_recv()
            for j, chip in enumerate(chips):
                copy(a, 4 + j, (*chip, 1 - ic), me).wait_recv()
        for cp in first + passed:
            cp.wait_send()
        for cp in local:
            cp.wait()

    return pl.pallas_call(
        body, name=name,
        in_specs=[pl.BlockSpec(memory_space=pl.ANY)] * n,
        out_specs=[pl.BlockSpec(memory_space=pl.ANY)] * n,
        out_shape=[jax.ShapeDtypeStruct((N_DEV,) + x.shape, x.dtype) for x in arrays],
        scratch_shapes=[pltpu.SemaphoreType.DMA((n, N_DEV - 1)),
                        pltpu.SemaphoreType.DMA((n, N_DEV - 1)),
                        pltpu.SemaphoreType.DMA((n,))],
    )(*arrays)


def _swap_sibling(arrays, name):
    n = len(arrays)
    n_chips = N_DEV // 2

    def body(*refs):
        x_refs, got_refs = refs[:n], refs[n:2 * n]
        send_sems, recv_sems = refs[2 * n:]
        ix, iy, ic = _mesh_pos()
        sibling = (ix, iy, 1 - ic)
        sends = []
        for a in range(n):
            for q in range(n_chips):
                cp = pltpu.make_async_remote_copy(
                    src_ref=x_refs[a].at[q, 1 - ic], dst_ref=got_refs[a].at[q],
                    send_sem=send_sems.at[a, q], recv_sem=recv_sems.at[a, q],
                    device_id=sibling, device_id_type=pl.DeviceIdType.MESH)
                cp.start()
                sends.append(cp)
        for cp in sends:
            cp.wait()

    return pl.pallas_call(
        body, name=name,
        in_specs=[pl.BlockSpec(memory_space=pl.ANY)] * n,
        out_specs=[pl.BlockSpec(memory_space=pl.ANY)] * n,
        out_shape=[jax.ShapeDtypeStruct((n_chips,) + x.shape[2:], x.dtype) for x in arrays],
        scratch_shapes=[pltpu.SemaphoreType.DMA((n, n_chips)),
                        pltpu.SemaphoreType.DMA((n, n_chips))],
    )(*arrays)


def _exchange_chips(arrays, name):
    n = len(arrays)
    n_chips = N_DEV // 2

    def body(*refs):
        x_refs, o_refs = refs[:n], refs[n:2 * n]
        send_sems, recv_sems, local_sems = refs[2 * n:]
        ix, iy, ic = _mesh_pos()
        my_chip = 2 * ix + iy
        local = [pltpu.make_async_copy(x_refs[a].at[my_chip], o_refs[a].at[my_chip],
                                       local_sems.at[a]) for a in range(n)]
        for cp in local:
            cp.start()
        sends = []
        for mask in range(1, n_chips):
            px = 1 - ix if mask & 2 else ix
            py = 1 - iy if mask & 1 else iy
            for a in range(n):
                cp = pltpu.make_async_remote_copy(
                    src_ref=x_refs[a].at[2 * px + py], dst_ref=o_refs[a].at[my_chip],
                    send_sem=send_sems.at[a, mask - 1], recv_sem=recv_sems.at[a, mask - 1],
                    device_id=(px, py, ic), device_id_type=pl.DeviceIdType.MESH)
                cp.start()
                sends.append(cp)
        for mask in range(1, n_chips):
            px = 1 - ix if mask & 2 else ix
            py = 1 - iy if mask & 1 else iy
            for a in range(n):
                pltpu.make_async_remote_copy(
                    src_ref=x_refs[a].at[my_chip], dst_ref=o_refs[a].at[2 * px + py],
                    send_sem=send_sems.at[a, mask - 1], recv_sem=recv_sems.at[a, mask - 1],
                    device_id=(px, py, ic), device_id_type=pl.DeviceIdType.MESH).wait_recv()
        for cp in sends:
            cp.wait_send()
        for cp in local:
            cp.wait()

    return pl.pallas_call(
        body, name=name,
        in_specs=[pl.BlockSpec(memory_space=pl.ANY)] * n,
        out_specs=[pl.BlockSpec(memory_space=pl.ANY)] * n,
        out_shape=[jax.ShapeDtypeStruct(x.shape, x.dtype) for x in arrays],
        scratch_shapes=[pltpu.SemaphoreType.DMA((n, n_chips - 1)),
                        pltpu.SemaphoreType.DMA((n, n_chips - 1)),
                        pltpu.SemaphoreType.DMA((n,))],
    )(*arrays)


def _pair_sum(core, x, got, name):
    nq, rows, cols = got.shape

    def body(c_ref, x_ref, g_ref, o_ref):
        o_ref[...] = (x_ref[...] + g_ref[...]).astype(BF16)

    blk = pl.BlockSpec((None, rows, cols), lambda q, c: (q, 0, 0))
    return pl.pallas_call(
        body, name=name,
        grid_spec=pltpu.PrefetchScalarGridSpec(
            num_scalar_prefetch=1, grid=(nq,),
            in_specs=[pl.BlockSpec((None, None, rows, cols), lambda q, c: (q, c[0], 0, 0)), blk],
            out_specs=blk),
        out_shape=jax.ShapeDtypeStruct(got.shape, BF16),
        compiler_params=_params(),
    )(core, x, got)


def _pad_heads_cols(w):
    k = w.shape[0]
    w = w.reshape(k, HEADS, HEAD_DIM)
    return jnp.pad(w, ((0, 0), (0, 0), (0, HEAD_PAD - HEAD_DIM))).reshape(k, FOX_PAD)


def _unpad_heads_cols(w):
    k = w.shape[0]
    return w.reshape(k, HEADS, HEAD_PAD)[:, :, :HEAD_DIM].reshape(k, HEADS * HEAD_DIM)


def _selectors():
    r = lax.broadcasted_iota(jnp.int32, (3 * LANES, FOX_PAD), 0)
    c = lax.broadcasted_iota(jnp.int32, (3 * LANES, FOX_PAD), 1)
    part, head_r = r // LANES, r % LANES
    head_c, lane_c = c // HEAD_PAD, c % HEAD_PAD
    same = (head_r == head_c) & (head_r < HEADS)
    sel_q = jnp.where(same & (lane_c == LANE_RB + part), 1.0, 0.0)
    sel_k = jnp.where(same & (lane_c == LANE_CK + part), -1.0, 0.0)
    sel = jnp.stack([sel_q, sel_k, jnp.zeros_like(sel_q)]).astype(BF16)
    lane = lax.broadcasted_iota(jnp.int32, (1, FOX_PAD), 1) % HEAD_PAD
    ones_q = jnp.where((lane >= LANE_CK) & (lane < LANE_CK + 3), 1.0, 0.0)
    ones_k = jnp.where((lane >= LANE_RB) & (lane < LANE_RB + 3), 1.0, 0.0)
    ones_v = jnp.where((lane >= LANE_ONE_V) & (lane < LANE_ONE_V + 2), 1.0, 0.0)
    bias = jnp.stack([ones_q, ones_k, ones_v]).astype(F32)
    return sel, bias


def _local_step(x, target, norm_g, final_g, w_in8, conv_w, conv_b, wa, ba, wx, bx, a_param,
                w_out_b, fox_in8, b_f, fox_out_b, blk=512, ts=256):
    qk_scale = 1.0 / (HEAD_DIM ** 0.5)
    g0, g1 = norm_g[0:1], norm_g[1:2]
    gf = final_g.reshape(1, D_MODEL)
    wa_b, wx_b = wa.astype(BF16), wx.astype(BF16)
    fox_w_in = jnp.transpose(fox_in8, (1, 0, 2)).reshape(D_MODEL, FOX_IN_COLS)
    wq = _pad_heads_cols(fox_w_in[:, 0:1024]) * qk_scale
    wk = _pad_heads_cols(fox_w_in[:, 1024:2048])
    wv = _pad_heads_cols(fox_w_in[:, 2048:3072])
    wg = _pad_heads_cols(fox_w_in[:, 3072:4096])
    wf_b = jnp.pad(fox_w_in[:, 4096:], ((0, 0), (0, LANES - HEADS)))
    w4 = jnp.stack([wq, wk, wv, wg]).astype(BF16)
    bf_pad = jnp.pad(b_f, ((0, 0), (0, LANES - HEADS)))
    fo = fox_out_b.reshape(HEADS, HEAD_DIM, D_MODEL)
    fo_b = jnp.pad(fo, ((0, 0), (0, HEAD_PAD - HEAD_DIM), (0, 0))).reshape(FOX_PAD, D_MODEL)
    sel, bias = _selectors()

    xb, gate1, h0 = _lru_in_fwd(x, g0, w_in8, ts)
    y1, hs = _lru_core_fwd(xb, gate1, conv_w, conv_b, wa_b, ba, wx_b, bx, a_param, ts)
    x1, h1, f, cparts = _fox_pre_fwd(x, y1, w_out_b, g1, wf_b, bf_pad, ts)
    qkv = _fox_proj_fwd(h1, cparts, w4[0:3], sel, bias, BF16, ts, "fox_proj_qkv")
    gate2 = _fox_proj_fwd(h1, None, w4[3:4], None, None, F32, ts, "fox_proj_gate")[0]
    o, qb = _attn_fwd(qkv, blk)
    dx2, y2, loss_acc, g_final = _fox_out_loss(o, gate2, fo_b, x1, target, gf, ts)

    do, dgate2 = _fox_out_bwd(dx2, fo_b, o, gate2, ts)
    dq, dk, dv, dcum = _attn_bwd(qb, qkv, do, blk)
    dx1, dx1b, df, g_norm1, g_bf = _fox_in_bwd(dq, dk, dv, dgate2, w4, wf_b, dcum, f, x1, dx2,
                                               g1, ts)
    du, g_wa, g_wx, g_vec = _lru_core_bwd(dx1b, w_out_b, xb, gate1, hs, conv_w, conv_b, wa_b, ba,
                                          wx_b, bx, a_param, jnp.transpose(wa_b, (0, 2, 1)),
                                          jnp.transpose(wx_b, (0, 2, 1)), ts)
    grad_x, g_norm0 = _lru_in_bwd(du, w_in8, x, dx1, g0, ts)

    tw = 512
    g_lru_w_in = _weight_grad(h0, du, tw, "grad_lru_w_in", col_shards=N_DEV)
    g_lru_w_out = _weight_grad(y1, dx1b, tw, "grad_lru_w_out")
    g_q = _weight_grad(h1, dq, tw, "grad_fox_wq", scale=qk_scale)
    g_k = _weight_grad(h1, dk, tw, "grad_fox_wk")
    g_v = _weight_grad(h1, dv, tw, "grad_fox_wv")
    g_g = _weight_grad(h1, dgate2, tw, "grad_fox_wg")
    g_f = _weight_grad(h1, df, tw, "grad_fox_wf")
    g_fox_w_in = jnp.concatenate(
        [_unpad_heads_cols(g_q), _unpad_heads_cols(g_k), _unpad_heads_cols(g_v),
         _unpad_heads_cols(g_g), g_f[:, :HEADS]], axis=1)
    g_fox_w_in = jnp.transpose(g_fox_w_in.reshape(D_MODEL, N_DEV, FOX_IN_SHARD), (1, 0, 2))
    g_fo = _weight_grad(y2, dx2.astype(BF16), tw, "grad_fox_w_out")
    g_fox_w_out = g_fo.reshape(HEADS, HEAD_PAD, D_MODEL)[:, :HEAD_DIM].reshape(
        HEADS * HEAD_DIM, D_MODEL)

    grads = dict(
        norm_g=jnp.concatenate([g_norm0, g_norm1], axis=0), final_g=g_final[0],
        lru_w_in=g_lru_w_in, lru_conv_w=g_vec[0:4], lru_conv_b=g_vec[4:5], lru_wa=g_wa,
        lru_ba=g_vec[5:6], lru_wx=g_wx, lru_bx=g_vec[6:7], lru_a_param=g_vec[7:8],
        lru_w_out=g_lru_w_out, fox_w_in=g_fox_w_in, fox_b_f=g_bf[:, :HEADS],
        fox_w_out=g_fox_w_out)
    return loss_acc[0, 0], grad_x, grads


SMALL =("norm_g", "final_g", "lru_conv_b", "lru_wa", "lru_ba", "lru_wx", "lru_bx", "lru_a_param",
         "fox_b_f")
ALL_WEIGHTS = ("norm_g", "final_g", "lru_w_in", "lru_conv_w", "lru_conv_b", "lru_wa", "lru_ba",
               "lru_wx", "lru_bx", "lru_a_param", "lru_w_out", "fox_w_in", "fox_b_f", "fox_w_out")


def _pack_small(d):
    rows = []
    for n in SMALL:
        a = d[n].reshape(-1)
        if a.shape[0] % LANES:
            a = jnp.pad(a, (0, LANES - a.shape[0] % LANES))
        rows.append(a.reshape(-1, LANES))
    packed = jnp.concatenate(rows, axis=0)
    return jnp.pad(packed, ((0, N_DEV * SMALL_CHUNK_ROWS - packed.shape[0]), (0, 0)))


def _unpack_small(packed, like):
    out, off = {}, 0
    for n, nrows in zip(SMALL, SMALL_ROWS):
        size = like[n].size
        out[n] = packed[off:off + nrows].reshape(-1)[:size].reshape(like[n].shape)
        off += nrows
    return out


def kernel(x, norm_g, final_g, lru_w_in, lru_conv_w, lru_conv_b, lru_wa, lru_ba, lru_wx, lru_bx, lru_a_param, lru_w_out, fox_w_in, fox_b_f, fox_w_out, loss_target, m_norm_g, m_final_g, m_lru_w_in, m_lru_conv_w, m_lru_conv_b, m_lru_wa, m_lru_ba, m_lru_wx, m_lru_bx, m_lru_a_param, m_lru_w_out, m_fox_w_in, m_fox_b_f, m_fox_w_out, v_norm_g, v_final_g, v_lru_w_in, v_lru_conv_w, v_lru_conv_b, v_lru_wa, v_lru_ba, v_lru_wx, v_lru_bx, v_lru_a_param, v_lru_w_out, v_fox_w_in, v_fox_b_f, v_fox_w_out):
    w_loc = dict(norm_g=norm_g, final_g=final_g, lru_w_in=lru_w_in, lru_conv_w=lru_conv_w,
                 lru_conv_b=lru_conv_b, lru_wa=lru_wa, lru_ba=lru_ba, lru_wx=lru_wx, lru_bx=lru_bx,
                 lru_a_param=lru_a_param, lru_w_out=lru_w_out, fox_w_in=fox_w_in, fox_b_f=fox_b_f,
                 fox_w_out=fox_w_out)
    m_loc = dict(norm_g=m_norm_g, final_g=m_final_g, lru_w_in=m_lru_w_in, lru_conv_w=m_lru_conv_w,
                 lru_conv_b=m_lru_conv_b, lru_wa=m_lru_wa, lru_ba=m_lru_ba, lru_wx=m_lru_wx,
                 lru_bx=m_lru_bx, lru_a_param=m_lru_a_param, lru_w_out=m_lru_w_out,
                 fox_w_in=m_fox_w_in, fox_b_f=m_fox_b_f, fox_w_out=m_fox_w_out)
    v_loc = dict(norm_g=v_norm_g, final_g=v_final_g, lru_w_in=v_lru_w_in, lru_conv_w=v_lru_conv_w,
                 lru_conv_b=v_lru_conv_b, lru_wa=v_lru_wa, lru_ba=v_lru_ba, lru_wx=v_lru_wx,
                 lru_bx=v_lru_bx, lru_a_param=v_lru_a_param, lru_w_out=v_lru_w_out,
                 fox_w_in=v_fox_w_in, fox_b_f=v_fox_b_f, fox_w_out=v_fox_w_out)

    w_in8, conv8, w_out8, fox_in8, fox_out8 = _gather_two_level(
        [lru_w_in[0].astype(BF16), lru_conv_w[0], lru_w_out[0].astype(BF16),
         fox_w_in[0].astype(BF16), fox_w_out[0].astype(BF16)], "gather_weights")
    conv_full = jnp.transpose(conv8, (1, 0, 2)).reshape(CONV_WIDTH, LRU_WIDTH)

    loss, grad_x, grads = _local_step(
        x[0], loss_target[0], norm_g, final_g, w_in8, conv_full, lru_conv_b, lru_wa[0], lru_ba,
        lru_wx[0], lru_bx, lru_a_param, w_out8.reshape(LRU_WIDTH, D_MODEL), fox_in8, fox_b_f,
        fox_out8.reshape(HEADS * HEAD_DIM, D_MODEL))

    n_chips = N_DEV // 2
    conv_send = jnp.transpose(grads["lru_conv_w"].reshape(CONV_WIDTH, N_DEV, -1), (1, 0, 2))
    names = ("lru_w_in", "lru_conv_w", "lru_w_out", "fox_w_in", "fox_w_out", "small")
    send = [grads["lru_w_in"], conv_send, grads["lru_w_out"].reshape(N_DEV, -1, D_MODEL),
            grads["fox_w_in"], grads["fox_w_out"].reshape(N_DEV, -1, D_MODEL),
            _pack_small(grads).reshape(N_DEV, SMALL_CHUNK_ROWS, LANES)]
    send = [a.reshape((n_chips, 2) + a.shape[1:]) for a in send]
    got = _swap_sibling(send, "swap_grads")
    core = lax.axis_index("c").astype(jnp.int32).reshape(1)
    chip_sums = [_pair_sum(core, a, b, "pair_sum_" + n) for n, a, b in zip(names, send, got)]
    r_w_in, r_conv, r_w_out, r_fox_in, r_fox_out, r_small = _exchange_chips(
        chip_sums, "scatter_grads")

    out = {}
    for n, recv, tr in (("lru_w_in", r_w_in, 256), ("lru_conv_w", r_conv, CONV_WIDTH),
                        ("lru_w_out", r_w_out, 96), ("fox_w_in", r_fox_in, 128),
                        ("fox_w_out", r_fox_out, 64)):
        res = _adamw(recv, w_loc[n][0], m_loc[n][0], v_loc[n][0], tr, "adamw_" + n)
        out[n] = [a[None] for a in res]

    g_chunk = _reduce_parts(r_small, "reduce_small_grads")
    g_small, = _exchange([g_chunk], False, "gather_small_grads")
    g_small = g_small.reshape(1, N_DEV * SMALL_CHUNK_ROWS, LANES)
    res = _adamw(g_small, _pack_small(w_loc), _pack_small(m_loc), _pack_small(v_loc),
                 N_DEV * SMALL_CHUNK_ROWS, "adamw_replicated")
    small_out = [_unpack_small(a, w_loc) for a in res]
    for n in SMALL:
        out[n] = [d[n] for d in small_out]

    loss = lax.psum(loss, ("x", "y", "c"))
    return (loss, grad_x[None], *[out[n][0] for n in ALL_WEIGHTS], *[out[n][1] for n in ALL_WEIGHTS],
            *[out[n][2] for n in ALL_WEIGHTS], *[out[n][3] for n in ALL_WEIGHTS])
```

```python
import functools

import jax
import jax.numpy as jnp
from jax import lax
from jax.experimental import pallas as pl
from jax.experimental.pallas import tpu as pltpu

F32 = jnp.float32
BF16 = jnp.bfloat16

D_MODEL = 1024
LRU_WIDTH = 1536
LRU_BLOCKS = 12
LRU_BLOCK_W = 128
CONV_WIDTH = 4
LRU_C = 8.0
HEADS = 16
HEAD_DIM = 64
HEAD_PAD = 128
FOX_PAD = HEADS * HEAD_PAD
HEADS_PER_STEP = 2
EPS = 1e-6
NEG_BIG = -1e30
N_DEV = 8

ADAM_LR = 0.001
ADAM_B1 = 0.9
ADAM_B2 = 0.999
ADAM_EPS = 1e-08
ADAM_WD = 0.01
ADAM_STEP = 10

LANE_RB = 64
LANE_CK = 67
LANE_LSE = 70
LANE_ONE_V = 64

VMEM_LIMIT_BYTES = 56 * 1024 * 1024
LANES = 128
SUBLANES = 8

LRU_IN_SHARD = 2 * LRU_WIDTH // N_DEV
FOX_IN_COLS = 4 * HEADS * HEAD_DIM + HEADS
FOX_IN_SHARD = FOX_IN_COLS // N_DEV

SMALL_ROWS = (16, 8, 12, 1536, 12, 1536, 12, 12, 1)
SMALL_CHUNK_ROWS = 400
assert sum(SMALL_ROWS) <= N_DEV * SMALL_CHUNK_ROWS


def _params(n_grid_axes=1):
    return pltpu.CompilerParams(
        dimension_semantics=("arbitrary",) * n_grid_axes,
        vmem_limit_bytes=VMEM_LIMIT_BYTES)


def _const_spec(shape):
    nd = len(shape)
    return pl.BlockSpec(shape, lambda *_: (0,) * nd, pipeline_mode=pl.Buffered(1))


def _shift_down(x, k, fill):
    rows = lax.broadcasted_iota(jnp.int32, x.shape, 0)
    return jnp.where(rows >= k, pltpu.roll(x, k, 0), fill)


def _shift_up(x, k, fill):
    n = x.shape[0]
    rows = lax.broadcasted_iota(jnp.int32, x.shape, 0)
    return jnp.where(rows < n - k, pltpu.roll(x, n - k, 0), fill)


def _scan_rows(a, b, reverse=False):
    n = a.shape[0]
    shift = _shift_up if reverse else _shift_down
    k = 1
    while k < n:
        b = a * shift(b, k, 0.0) + b
        a = a * shift(a, k, 1.0)
        k *= 2
    return a, b


def _cumsum_rows(x, reverse=False):
    n = x.shape[0]
    shift = _shift_up if reverse else _shift_down
    k = 1
    while k < n:
        x = x + shift(x, k, 0.0)
        k *= 2
    return x


def _rstd(x):
    return lax.rsqrt(jnp.mean(x * x, axis=-1, keepdims=True) + EPS)


def _norm_bwd(x, g, dh):
    rstd = _rstd(x)
    xhat = x * rstd
    dg = jnp.sum(dh * xhat, axis=0, keepdims=True)
    dxh = dh * g
    dx = rstd * (dxh - xhat * jnp.mean(dxh * xhat, axis=-1, keepdims=True))
    return dx, dg


def _split3(x):
    hi = x.astype(BF16)
    r1 = x - hi.astype(F32)
    mid = r1.astype(BF16)
    lo = (r1 - mid.astype(F32)).astype(BF16)
    return hi, mid, lo


def _sigmoid(x):
    return jax.nn.sigmoid(x)


def _dot(a, b):
    return jnp.dot(a, b, preferred_element_type=F32)


def _dot_nt(a, b):
    return lax.dot_general(a, b, (((1,), (1,)), ((), ())), preferred_element_type=F32)


def _dot_tn(a, b):
    return lax.dot_general(a, b, (((0,), (0,)), ((), ())), preferred_element_type=F32)


def _conv_taps(xb, prev8):
    rows8 = lax.broadcasted_iota(jnp.int32, prev8.shape, 0)
    taps = [xb]
    for j in range(1, CONV_WIDTH):
        r = pltpu.roll(xb, j, 0)
        p = pltpu.roll(prev8, j, 0)
        head = jnp.where(rows8 < j, p, r[0:SUBLANES])
        taps.append(jnp.concatenate([head, r[SUBLANES:]], axis=0))
    return taps


def _lru_pre(taps, cw, cb, wa_ref, ba, wx_ref, bx, a_param):
    xc = cb + cw[3:4] * taps[0] + cw[2:3] * taps[1] + cw[1:2] * taps[2] + cw[0:1] * taps[3]
    xcb = xc.astype(BF16)
    ra, ia = [], []
    for n in range(LRU_BLOCKS):
        blk = xcb[:, n * LRU_BLOCK_W:(n + 1) * LRU_BLOCK_W]
        ra.append(_dot(blk, wa_ref[n]))
        ia.append(_dot(blk, wx_ref[n]))
    r = _sigmoid(jnp.concatenate(ra, axis=1) + ba)
    i = _sigmoid(jnp.concatenate(ia, axis=1) + bx)
    z = -a_param
    sp = jnp.maximum(z, 0.0) + jnp.log1p(jnp.exp(-jnp.abs(z)))
    log_a = (-LRU_C) * r * sp
    a = jnp.exp(log_a)
    one_minus_a2 = -jnp.tanh(log_a) * (a * a + 1.0)
    mult = jnp.sqrt(one_minus_a2)
    return xc, xcb, r, i, sp, a, mult


def _lru_in_fwd(x, g0, w_in, ts):
    s = x.shape[0]
    half = N_DEV // 2

    def body(x_ref, g_ref, w_ref, xb_ref, gate_ref, h_ref):
        xv = x_ref[...]
        h = (xv * _rstd(xv) * g_ref[...]).astype(BF16)
        u = [_dot(h, w_ref[j]) for j in range(N_DEV)]
        xb_ref[...] = jnp.concatenate(u[:half], axis=1)
        gate_ref[...] = jnp.concatenate(u[half:], axis=1)
        h_ref[...] = h

    return pl.pallas_call(
        body, name="lru_in_fwd", grid=(s // ts,),
        in_specs=[pl.BlockSpec((ts, D_MODEL), lambda i: (i, 0)),
                  _const_spec((1, D_MODEL)),
                  _const_spec((N_DEV, D_MODEL, LRU_IN_SHARD))],
        out_specs=[pl.BlockSpec((ts, LRU_WIDTH), lambda i: (i, 0)),
                   pl.BlockSpec((ts, LRU_WIDTH), lambda i: (i, 0)),
                   pl.BlockSpec((ts, D_MODEL), lambda i: (i, 0))],
        out_shape=[jax.ShapeDtypeStruct((s, LRU_WIDTH), F32),
                   jax.ShapeDtypeStruct((s, LRU_WIDTH), F32),
                   jax.ShapeDtypeStruct((s, D_MODEL), BF16)],
        compiler_params=_params(),
    )(x, g0, w_in)


def _lru_core_fwd(xb, gate, cw, cb, wa, ba, wx, bx, a_param, ts):
    s = xb.shape[0]

    def body(xb_ref, gate_ref, cw_ref, cb_ref, wa_ref, ba_ref, wx_ref, bx_ref, ap_ref,
             y_ref, hs_ref, prev_ref, hcar_ref):
        @pl.when(pl.program_id(0) == 0)
        def _():
            prev_ref[...] = jnp.zeros_like(prev_ref)
            hcar_ref[...] = jnp.zeros_like(hcar_ref)

        xbv = xb_ref[...]
        taps = _conv_taps(xbv, prev_ref[...])
        xc, _, _, i, _, a, mult = _lru_pre(taps, cw_ref[...], cb_ref[...], wa_ref, ba_ref[...],
                                           wx_ref, bx_ref[...], ap_ref[...])
        bterm = mult * (i * xc)
        cum_a, hloc = _scan_rows(a, bterm)
        hs = cum_a * hcar_ref[SUBLANES - 1:SUBLANES, :] + hloc
        gv = gate_ref[...]
        y_ref[...] = (hs * (gv * _sigmoid(gv))).astype(BF16)
        hs_ref[...] = hs
        prev_ref[...] = xbv[ts - SUBLANES:, :]
        hcar_ref[...] = hs[ts - SUBLANES:, :]

    vec = _const_spec((1, LRU_WIDTH))
    blk = _const_spec((LRU_BLOCKS, LRU_BLOCK_W, LRU_BLOCK_W))
    tile = pl.BlockSpec((ts, LRU_WIDTH), lambda i: (i, 0))
    return pl.pallas_call(
        body, name="lru_core_fwd", grid=(s // ts,),
        in_specs=[tile, tile, _const_spec((CONV_WIDTH, LRU_WIDTH)), vec, blk, vec, blk, vec, vec],
        out_specs=[tile, tile],
        out_shape=[jax.ShapeDtypeStruct((s, LRU_WIDTH), BF16),
                   jax.ShapeDtypeStruct((s, LRU_WIDTH), F32)],
        scratch_shapes=[pltpu.VMEM((SUBLANES, LRU_WIDTH), F32),
                        pltpu.VMEM((SUBLANES, LRU_WIDTH), F32)],
        compiler_params=_params(),
    )(xb, gate, cw, cb, wa, ba, wx, bx, a_param)


def _fox_pre_fwd(x, y, w_out, g1, wf, bf, ts):
    s = x.shape[0]

    def body(x_ref, y_ref, w_ref, g_ref, wf_ref, bf_ref, x1_ref, h1_ref, f_ref, cp_ref, ccar_ref):
        @pl.when(pl.program_id(0) == 0)
        def _():
            ccar_ref[...] = jnp.zeros_like(ccar_ref)

        x1 = x_ref[...] + _dot(y_ref[...], w_ref[...])
        h1 = (x1 * _rstd(x1) * g_ref[...]).astype(BF16)
        f = _dot(h1, wf_ref[...]) + bf_ref[...]
        logsig = jnp.minimum(f, 0.0) - jnp.log1p(jnp.exp(-jnp.abs(f)))
        cum = _cumsum_rows(logsig) + ccar_ref[SUBLANES - 1:SUBLANES, :]
        hi, mid, lo = _split3(cum)
        x1_ref[...] = x1
        h1_ref[...] = h1
        f_ref[...] = f
        cp_ref[...] = jnp.concatenate([hi, mid, lo], axis=1)
        ccar_ref[...] = cum[ts - SUBLANES:, :]

    return pl.pallas_call(
        body, name="fox_pre_fwd", grid=(s // ts,),
        in_specs=[pl.BlockSpec((ts, D_MODEL), lambda i: (i, 0)),
                  pl.BlockSpec((ts, LRU_WIDTH), lambda i: (i, 0)),
                  _const_spec((LRU_WIDTH, D_MODEL)),
                  _const_spec((1, D_MODEL)),
                  _const_spec((D_MODEL, LANES)),
                  _const_spec((1, LANES))],
        out_specs=[pl.BlockSpec((ts, D_MODEL), lambda i: (i, 0)),
                   pl.BlockSpec((ts, D_MODEL), lambda i: (i, 0)),
                   pl.BlockSpec((ts, LANES), lambda i: (i, 0)),
                   pl.BlockSpec((ts, 3 * LANES), lambda i: (i, 0))],
        out_shape=[jax.ShapeDtypeStruct((s, D_MODEL), F32),
                   jax.ShapeDtypeStruct((s, D_MODEL), BF16),
                   jax.ShapeDtypeStruct((s, LANES), F32),
                   jax.ShapeDtypeStruct((s, 3 * LANES), BF16)],
        scratch_shapes=[pltpu.VMEM((SUBLANES, LANES), F32)],
        compiler_params=_params(),
    )(x, y, w_out, g1, wf, bf)


def _fox_proj_fwd(h1, cparts, w, sel, bias, out_dtype, ts, name):
    s = h1.shape[0]
    ng = w.shape[0]
    use_sel = sel is not None

    def body(*refs):
        if use_sel:
            h_ref, cp_ref, w_ref, sel_ref, b_ref, o_ref = refs
            acc = _dot(h_ref[...], w_ref[...]) + _dot(cp_ref[...], sel_ref[...]) + b_ref[...]
        else:
            h_ref, w_ref, o_ref = refs
            acc = _dot(h_ref[...], w_ref[...])
        o_ref[...] = acc.astype(out_dtype)

    in_specs = [pl.BlockSpec((ts, D_MODEL), lambda j, i: (i, 0))]
    args = [h1]
    if use_sel:
        in_specs.append(pl.BlockSpec((ts, 3 * LANES), lambda j, i: (i, 0)))
        args.append(cparts)
    in_specs.append(pl.BlockSpec((None, D_MODEL, FOX_PAD), lambda j, i: (j, 0, 0)))
    args.append(w)
    if use_sel:
        in_specs.append(pl.BlockSpec((None, 3 * LANES, FOX_PAD), lambda j, i: (j, 0, 0)))
        in_specs.append(pl.BlockSpec((None, 1, FOX_PAD), lambda j, i: (j, 0, 0)))
        args += [sel, bias]
    return pl.pallas_call(
        body, name=name, grid=(ng, s // ts),
        in_specs=in_specs,
        out_specs=pl.BlockSpec((None, ts, FOX_PAD), lambda j, i: (j, i, 0)),
        out_shape=jax.ShapeDtypeStruct((ng, s, FOX_PAD), out_dtype),
        compiler_params=_params(2),
    )(*args)


def _attn_fwd(qkv, blk):
    s = qkv.shape[1]
    nblk = s // blk
    wide = 2 * blk
    heads = [slice(i * HEAD_PAD, (i + 1) * HEAD_PAD) for i in range(HEADS_PER_STEP)]

    def body(q_ref, k_ref, v_ref, o_ref, qb_ref, acc_ref, m_ref):
        qi = pl.program_id(1)
        row = lax.broadcasted_iota(jnp.int32, (blk, blk), 0)
        col = lax.broadcasted_iota(jnp.int32, (blk, blk), 1)
        lane = lax.broadcasted_iota(jnp.int32, (blk, HEAD_PAD), 1)
        qs = [q_ref[:, hd] for hd in heads]
        for i in range(HEADS_PER_STEP):
            acc_ref[i] = jnp.zeros((blk, HEAD_PAD), F32)
            m_ref[i] = jnp.full((blk, HEAD_PAD), NEG_BIG, F32)

        def step(k0, size, masked):
            scores = [_dot_nt(q, k_ref[pl.ds(k0, size), hd]) for q, hd in zip(qs, heads)]
            for i, (sc, hd) in enumerate(zip(scores, heads)):
                v = v_ref[pl.ds(k0, size), hd]
                if masked:
                    sc = jnp.where(col <= row, sc, NEG_BIG)
                m = m_ref[i]
                m_new = jnp.maximum(m, jnp.max(sc, axis=-1, keepdims=True))
                p = jnp.exp((sc - jnp.tile(m_new, (1, size // HEAD_PAD))).astype(BF16))
                acc_ref[i] = jnp.exp(m - m_new) * acc_ref[i] + _dot(p, v)
                m_ref[i] = m_new

        def wide_step(kk, _):
            step(pl.multiple_of(kk * wide, wide), wide, False)
            return 0

        lax.fori_loop(0, qi // 2, wide_step, 0)

        @pl.when(qi % 2 == 1)
        def _():
            step(pl.multiple_of((qi - 1) * blk, blk), blk, False)

        step(pl.multiple_of(qi * blk, blk), blk, True)
        for i, (q, hd) in enumerate(zip(qs, heads)):
            acc = acc_ref[i]
            l = jnp.broadcast_to(acc[:, LANE_ONE_V:LANE_ONE_V + 1], (blk, HEAD_PAD))
            o_ref[:, hd] = (acc / l).astype(BF16)
            hi, mid, lo = _split3(-(m_ref[i] + jnp.log(l)))
            qb_ref[:, hd] = jnp.where(lane == LANE_LSE, hi, jnp.where(
                lane == LANE_LSE + 1, mid, jnp.where(lane == LANE_LSE + 2, lo, q)))

    width = HEADS_PER_STEP * HEAD_PAD

    def whole(j):
        return pl.BlockSpec((None, s, width), lambda h, i: (j, 0, h))

    out_spec = pl.BlockSpec((blk, width), lambda h, i: (i, h))
    return pl.pallas_call(
        body, name="attn_fwd", grid=(HEADS // HEADS_PER_STEP, nblk),
        in_specs=[pl.BlockSpec((None, blk, width), lambda h, i: (0, i, h)), whole(1), whole(2)],
        out_specs=[out_spec, out_spec],
        out_shape=[jax.ShapeDtypeStruct((s, FOX_PAD), BF16),
                   jax.ShapeDtypeStruct((s, FOX_PAD), BF16)],
        scratch_shapes=[pltpu.VMEM((HEADS_PER_STEP, blk, HEAD_PAD), F32),
                        pltpu.VMEM((HEADS_PER_STEP, blk, HEAD_PAD), F32)],
        compiler_params=_params(2),
    )(qkv, qkv, qkv)


def _fox_out_loss(o, gate, w_out, x1, target, gf, ts):
    s = x1.shape[0]

    def body(o_ref, gt_ref, w_ref, x1_ref, t_ref, g_ref, dx2_ref, y2_ref, loss_ref, gfin_ref):
        @pl.when(pl.program_id(0) == 0)
        def _():
            loss_ref[...] = jnp.zeros_like(loss_ref)
            gfin_ref[...] = jnp.zeros_like(gfin_ref)

        gv = gt_ref[...]
        y2 = (o_ref[...] * (gv * _sigmoid(gv))).astype(BF16)
        x2 = x1_ref[...] + _dot(y2, w_ref[...])
        rstd = _rstd(x2)
        xhat = x2 * rstd
        g = g_ref[...]
        diff = xhat * g - t_ref[...]
        loss_ref[...] += 0.5 * jnp.sum(jnp.mean(diff * diff, axis=-1, keepdims=True))
        dy = diff * (1.0 / D_MODEL)
        gfin_ref[...] += jnp.sum(dy * xhat, axis=0, keepdims=True)
        dxh = dy * g
        dx2_ref[...] = rstd * (dxh - xhat * jnp.mean(dxh * xhat, axis=-1, keepdims=True))
        y2_ref[...] = y2

    return pl.pallas_call(
        body, name="fox_out_loss", grid=(s // ts,),
        in_specs=[pl.BlockSpec((ts, FOX_PAD), lambda i: (i, 0)),
                  pl.BlockSpec((ts, FOX_PAD), lambda i: (i, 0)),
                  _const_spec((FOX_PAD, D_MODEL)),
                  pl.BlockSpec((ts, D_MODEL), lambda i: (i, 0)),
                  pl.BlockSpec((ts, D_MODEL), lambda i: (i, 0)),
                  _const_spec((1, D_MODEL))],
        out_specs=[pl.BlockSpec((ts, D_MODEL), lambda i: (i, 0)),
                   pl.BlockSpec((ts, FOX_PAD), lambda i: (i, 0)),
                   pl.BlockSpec((SUBLANES, LANES), lambda i: (0, 0)),
                   pl.BlockSpec((1, D_MODEL), lambda i: (0, 0))],
        out_shape=[jax.ShapeDtypeStruct((s, D_MODEL), F32),
                   jax.ShapeDtypeStruct((s, FOX_PAD), BF16),
                   jax.ShapeDtypeStruct((SUBLANES, LANES), F32),
                   jax.ShapeDtypeStruct((1, D_MODEL), F32)],
        compiler_params=_params(),
    )(o, gate, w_out, x1, target, gf)


def _fox_out_bwd(dx2, w_out, o, gate, ts):
    s = dx2.shape[0]

    def body(dx_ref, w_ref, o_ref, gt_ref, do_ref, dg_ref):
        lane = lax.broadcasted_iota(jnp.int32, (ts, HEAD_PAD), 1)
        dy2 = _dot_nt(dx_ref[...].astype(BF16), w_ref[...])
        gv = gt_ref[...]
        sg = _sigmoid(gv)
        ov = o_ref[...]
        dov = dy2 * (gv * sg)
        dg_ref[...] = (dy2 * ov * (sg * (1.0 + gv * (1.0 - sg)))).astype(BF16)
        prod = dov * ov
        for h in range(HEADS):
            sl = slice(h * HEAD_PAD, (h + 1) * HEAD_PAD)
            delta = jnp.sum(prod[:, sl], axis=-1, keepdims=True)
            hi = delta.astype(BF16)
            lo = (delta - hi.astype(F32)).astype(BF16)
            do_h = dov[:, sl].astype(BF16)
            do_ref[:, sl] = jnp.where(lane == LANE_ONE_V, -hi,
                                      jnp.where(lane == LANE_ONE_V + 1, -lo, do_h))

    tile = pl.BlockSpec((ts, FOX_PAD), lambda i: (i, 0))
    return pl.pallas_call(
        body, name="fox_out_bwd", grid=(s // ts,),
        in_specs=[pl.BlockSpec((ts, D_MODEL), lambda i: (i, 0)),
                  _const_spec((FOX_PAD, D_MODEL)), tile, tile],
        out_specs=[tile, tile],
        out_shape=[jax.ShapeDtypeStruct((s, FOX_PAD), BF16),
                   jax.ShapeDtypeStruct((s, FOX_PAD), BF16)],
        compiler_params=_params(),
    )(dx2, w_out, o, gate)


def _attn_bwd(qb, qkv, do, blk):
    s = qb.shape[0]
    nblk = s // blk
    heads = [slice(i * HEAD_PAD, (i + 1) * HEAD_PAD) for i in range(HEADS_PER_STEP)]

    def body(q_ref, k_ref, v_ref, do_ref, dq_ref, dk_ref, dv_ref, dcum_ref, dq_acc, dk_acc,
             dv_acc):
        group = pl.program_id(0)
        kj = pl.program_id(1)
        row = lax.broadcasted_iota(jnp.int32, (blk, blk), 0)
        col = lax.broadcasted_iota(jnp.int32, (blk, blk), 1)
        lane = lax.broadcasted_iota(jnp.int32, (blk, LANES), 1)
        mine = [lane == group * HEADS_PER_STEP + i for i in range(HEADS_PER_STEP)]

        @pl.when(kj == 0)
        def _():
            dq_acc[...] = jnp.zeros_like(dq_acc)

        @pl.when((group == 0) & (kj == 0))
        def _():
            dcum_ref[...] = jnp.zeros_like(dcum_ref)

        k0 = pl.multiple_of(kj * blk, blk)
        ks = [k_ref[:, hd] for hd in heads]
        vs = [v_ref[:, hd] for hd in heads]

        dk_acc[...] = jnp.zeros_like(dk_acc)
        dv_acc[...] = jnp.zeros_like(dv_acc)

        def step(q0, masked):
            qs = [q_ref[pl.ds(q0, blk), hd] for hd in heads]
            dos = [do_ref[pl.ds(q0, blk), hd] for hd in heads]
            scores = [_dot_nt(q, k) for q, k in zip(qs, ks)]
            dps = [_dot_nt(dov, v) for dov, v in zip(dos, vs)]
            for i, (hd, k, q, dov, sc, dp) in enumerate(zip(heads, ks, qs, dos, scores, dps)):
                p = jnp.exp(sc.astype(BF16))
                if masked:
                    p = jnp.where(col <= row, p, jnp.zeros_like(p))
                ds = (p.astype(F32) * dp).astype(BF16)
                dv_acc[i] += _dot_tn(p, dov)
                dk_acc[i] += _dot_tn(ds, q)
                dq_acc[pl.ds(q0, blk), hd] += _dot(ds, k)

        step(k0, True)

        def q_step(qi, _):
            step(pl.multiple_of(qi * blk, blk), False)
            return 0

        lax.fori_loop(kj + 1, nblk, q_step, 0)
        dcum = dcum_ref[pl.ds(k0, blk), :]
        for i, (hd, mask) in enumerate(zip(heads, mine)):
            dk = dk_acc[i]
            dk_ref[:, hd] = dk.astype(BF16)
            dv_ref[:, hd] = dv_acc[i].astype(BF16)
            dcum = jnp.where(mask, -dk[:, LANE_CK:LANE_CK + 1], dcum)
        dcum_ref[pl.ds(k0, blk), :] = dcum

        @pl.when(kj == nblk - 1)
        def _():
            def finish(bi, _):
                r0 = pl.multiple_of(bi * blk, blk)
                dcum = dcum_ref[pl.ds(r0, blk), :]
                for hd, mask in zip(heads, mine):
                    dq = dq_acc[pl.ds(r0, blk), hd]
                    dq_ref[pl.ds(r0, blk), hd] = dq.astype(BF16)
                    dcum = dcum + jnp.where(mask, dq[:, LANE_RB:LANE_RB + 1], 0.0)
                dcum_ref[pl.ds(r0, blk), :] = dcum
                return 0

            lax.fori_loop(0, nblk, finish, 0)

    width = HEADS_PER_STEP * HEAD_PAD
    whole = pl.BlockSpec((s, width), lambda h, j: (0, h))
    part = pl.BlockSpec((blk, width), lambda h, j: (j, h))
    out = jax.ShapeDtypeStruct((s, FOX_PAD), BF16)
    return pl.pallas_call(
        body, name="attn_bwd", grid=(HEADS // HEADS_PER_STEP, nblk),
        in_specs=[whole,
                  pl.BlockSpec((None, blk, width), lambda h, j: (1, j, h)),
                  pl.BlockSpec((None, blk, width), lambda h, j: (2, j, h)),
                  whole],
        out_specs=[whole, part, part, pl.BlockSpec((s, LANES), lambda h, j: (0, 0))],
        out_shape=[out, out, out, jax.ShapeDtypeStruct((s, LANES), F32)],
        scratch_shapes=[pltpu.VMEM((s, width), F32),
                        pltpu.VMEM((HEADS_PER_STEP, blk, HEAD_PAD), F32),
                        pltpu.VMEM((HEADS_PER_STEP, blk, HEAD_PAD), F32)],
        compiler_params=_params(2),
    )(qb, qkv, qkv, do)


def _fox_in_bwd(dq, dk, dv, dg, wt, wft, dcum, f, x1, dx2, g1, ts):
    s = x1.shape[0]
    nt = s // ts

    def body(dq_ref, dk_ref, dv_ref, dg_ref, wt_ref, wft_ref, dcum_ref, f_ref, x1_ref, dx2_ref,
             g_ref, dx1_ref, dx1b_ref, df_ref, gn_ref, gbf_ref, rcar_ref):
        @pl.when(pl.program_id(0) == 0)
        def _():
            rcar_ref[...] = jnp.zeros_like(rcar_ref)
            gn_ref[...] = jnp.zeros_like(gn_ref)
            gbf_ref[...] = jnp.zeros_like(gbf_ref)

        dkv = dk_ref[...]
        rsum = _cumsum_rows(dcum_ref[...], reverse=True) + rcar_ref[0:1, :]
        df = rsum * _sigmoid(-f_ref[...])
        dfb = df.astype(BF16)
        dh = (_dot_nt(dq_ref[...], wt_ref[0]) + _dot_nt(dkv, wt_ref[1])
              + _dot_nt(dv_ref[...], wt_ref[2]) + _dot_nt(dg_ref[...], wt_ref[3])
              + _dot_nt(dfb, wft_ref[...]))
        dxn, dgn = _norm_bwd(x1_ref[...], g_ref[...], dh)
        dx1 = dx2_ref[...] + dxn
        dx1_ref[...] = dx1
        dx1b_ref[...] = dx1.astype(BF16)
        df_ref[...] = dfb
        gn_ref[...] += dgn
        gbf_ref[...] += jnp.sum(df, axis=0, keepdims=True)
        rcar_ref[...] = rsum[0:SUBLANES, :]

    rev = lambda i: (nt - 1 - i, 0)
    wide = pl.BlockSpec((ts, FOX_PAD), rev)
    return pl.pallas_call(
        body, name="fox_in_bwd", grid=(nt,),
        in_specs=[wide, wide, wide, wide,
                  _const_spec((4, D_MODEL, FOX_PAD)),
                  _const_spec((D_MODEL, LANES)),
                  pl.BlockSpec((ts, LANES), rev),
                  pl.BlockSpec((ts, LANES), rev),
                  pl.BlockSpec((ts, D_MODEL), rev),
                  pl.BlockSpec((ts, D_MODEL), rev),
                  _const_spec((1, D_MODEL))],
        out_specs=[pl.BlockSpec((ts, D_MODEL), rev),
                   pl.BlockSpec((ts, D_MODEL), rev),
                   pl.BlockSpec((ts, LANES), rev),
                   pl.BlockSpec((1, D_MODEL), lambda i: (0, 0)),
                   pl.BlockSpec((1, LANES), lambda i: (0, 0))],
        out_shape=[jax.ShapeDtypeStruct((s, D_MODEL), F32),
                   jax.ShapeDtypeStruct((s, D_MODEL), BF16),
                   jax.ShapeDtypeStruct((s, LANES), BF16),
                   jax.ShapeDtypeStruct((1, D_MODEL), F32),
                   jax.ShapeDtypeStruct((1, LANES), F32)],
        scratch_shapes=[pltpu.VMEM((SUBLANES, LANES), F32)],
        compiler_params=_params(),
    )(dq, dk, dv, dg, wt, wft, dcum, f, x1, dx2, g1)


def _lru_core_bwd(dx1b, w_out, xb, gate, hs, cw, cb, wa, ba, wx, bx, a_param, wa_t, wx_t, ts):
    s = xb.shape[0]
    nt = s // ts
    tpb = ts // SUBLANES

    def body(dx_ref, wo_ref, xb_ref, xbh_ref, gate_ref, hs_ref, hsh_ref, cw_ref, cb_ref, wa_ref,
             ba_ref, wx_ref, bx_ref, ap_ref, wat_ref, wxt_ref,
             du_ref, gwa_ref, gwx_ref, gvec_ref, acar_ref, dhcar_ref, dxccar_ref):
        step = pl.program_id(0)

        @pl.when(step == 0)
        def _():
            acar_ref[...] = jnp.zeros_like(acar_ref)
            dhcar_ref[...] = jnp.zeros_like(dhcar_ref)
            dxccar_ref[...] = jnp.zeros_like(dxccar_ref)
            gwa_ref[...] = jnp.zeros_like(gwa_ref)
            gwx_ref[...] = jnp.zeros_like(gwx_ref)
            gvec_ref[...] = jnp.zeros_like(gvec_ref)

        first_tile = step == nt - 1
        halo_on = jnp.where(first_tile, 0.0, 1.0)
        prev8 = xbh_ref[...] * halo_on
        hprev_row = hsh_ref[SUBLANES - 1:SUBLANES, :] * halo_on

        xbv = xb_ref[...]
        taps = _conv_taps(xbv, prev8)
        cw_v = cw_ref[...]
        xc, xcb, r, i, sp, a, mult = _lru_pre(taps, cw_v, cb_ref[...], wa_ref, ba_ref[...],
                                              wx_ref, bx_ref[...], ap_ref[...])
        hs = hs_ref[...]
        gv = gate_ref[...]
        sg = _sigmoid(gv)
        dy = _dot_nt(dx_ref[...], wo_ref[...])
        dhs = dy * (gv * sg)
        dgate = dy * hs * (sg * (1.0 + gv * (1.0 - sg)))

        rows = lax.broadcasted_iota(jnp.int32, a.shape, 0)
        a_next = jnp.where(rows < ts - 1, pltpu.roll(a, ts - 1, 0), acar_ref[0:1, :])
        cum_a, dh_loc = _scan_rows(a_next, dhs, reverse=True)
        dh = cum_a * dhcar_ref[0:1, :] + dh_loc
        h_prev = jnp.where(rows >= 1, pltpu.roll(hs, 1, 0), hprev_row)

        da = dh * h_prev
        ixc = i * xc
        dmult = dh * ixc
        di = dh * mult * xc
        dxc = dh * mult * i
        dlog_a = da * a - dmult * (a * a) / mult
        dr = dlog_a * ((-LRU_C) * sp)
        dsp = jnp.sum(dlog_a * ((-LRU_C) * r), axis=0, keepdims=True)
        dra = dr * r * (1.0 - r)
        dia = di * i * (1.0 - i)
        drab = dra.astype(BF16)
        diab = dia.astype(BF16)
        back = []
        for n in range(LRU_BLOCKS):
            sl = slice(n * LRU_BLOCK_W, (n + 1) * LRU_BLOCK_W)
            gwa_ref[n] += _dot_tn(xcb[:, sl], drab[:, sl])
            gwx_ref[n] += _dot_tn(xcb[:, sl], diab[:, sl])
            back.append(_dot(drab[:, sl], wat_ref[n]) + _dot(diab[:, sl], wxt_ref[n]))
        dxc = dxc + jnp.concatenate(back, axis=1)

        nxt8 = dxccar_ref[...]
        rows8 = lax.broadcasted_iota(jnp.int32, nxt8.shape, 0)
        dxb = cw_v[3:4] * dxc
        for j in range(1, CONV_WIDTH):
            rj = pltpu.roll(dxc, ts - j, 0)
            pj = pltpu.roll(nxt8, SUBLANES - j, 0)
            tail = jnp.where(rows8 >= SUBLANES - j, pj, rj[ts - SUBLANES:])
            dxb = dxb + cw_v[3 - j:4 - j] * jnp.concatenate([rj[:ts - SUBLANES], tail], axis=0)

        du_ref[:, :LRU_WIDTH] = dxb.astype(BF16)
        du_ref[:, LRU_WIDTH:] = dgate.astype(BF16)

        z = -ap_ref[...]
        gvec = [jnp.sum(dxc * taps[3 - k], axis=0, keepdims=True) for k in range(CONV_WIDTH)]
        gvec.append(jnp.sum(dxc, axis=0, keepdims=True))
        gvec.append(jnp.sum(dra, axis=0, keepdims=True))
        gvec.append(jnp.sum(dia, axis=0, keepdims=True))
        gvec.append(-dsp * _sigmoid(z))
        gvec_ref[...] += jnp.concatenate(gvec, axis=0)

        acar_ref[...] = a[0:SUBLANES, :]
        dhcar_ref[...] = dh[0:SUBLANES, :]
        dxccar_ref[...] = dxc[0:SUBLANES, :]

    rev = lambda i: (nt - 1 - i, 0)
    halo = lambda i: (jnp.maximum((nt - 1 - i) * tpb - 1, 0), 0)
    tile = pl.BlockSpec((ts, LRU_WIDTH), rev)
    halo_spec = pl.BlockSpec((SUBLANES, LRU_WIDTH), halo)
    vec = _const_spec((1, LRU_WIDTH))
    blk = _const_spec((LRU_BLOCKS, LRU_BLOCK_W, LRU_BLOCK_W))
    acc_blk = pl.BlockSpec((LRU_BLOCKS, LRU_BLOCK_W, LRU_BLOCK_W), lambda i: (0, 0, 0))
    return pl.pallas_call(
        body, name="lru_core_bwd", grid=(nt,),
        in_specs=[pl.BlockSpec((ts, D_MODEL), rev),
                  _const_spec((LRU_WIDTH, D_MODEL)),
                  tile, halo_spec, tile, tile, halo_spec,
                  _const_spec((CONV_WIDTH, LRU_WIDTH)), vec, blk, vec, blk, vec, vec, blk, blk],
        out_specs=[pl.BlockSpec((ts, 2 * LRU_WIDTH), rev), acc_blk, acc_blk,
                   pl.BlockSpec((SUBLANES, LRU_WIDTH), lambda i: (0, 0))],
        out_shape=[jax.ShapeDtypeStruct((s, 2 * LRU_WIDTH), BF16),
                   jax.ShapeDtypeStruct((LRU_BLOCKS, LRU_BLOCK_W, LRU_BLOCK_W), F32),
                   jax.ShapeDtypeStruct((LRU_BLOCKS, LRU_BLOCK_W, LRU_BLOCK_W), F32),
                   jax.ShapeDtypeStruct((SUBLANES, LRU_WIDTH), F32)],
        scratch_shapes=[pltpu.VMEM((SUBLANES, LRU_WIDTH), F32),
                        pltpu.VMEM((SUBLANES, LRU_WIDTH), F32),
                        pltpu.VMEM((SUBLANES, LRU_WIDTH), F32)],
        compiler_params=_params(),
    )(dx1b, w_out, xb, xb, gate, hs, hs, cw, cb, wa, ba, wx, bx, a_param, wa_t, wx_t)


def _lru_in_bwd(du, w_in, x, dx1, g0, ts):
    s = x.shape[0]

    def body(du_ref, w_ref, x_ref, dx1_ref, g_ref, gx_ref, gn_ref):
        @pl.when(pl.program_id(0) == 0)
        def _():
            gn_ref[...] = jnp.zeros_like(gn_ref)

        duv = du_ref[...]
        dh = _dot_nt(duv[:, 0:LRU_IN_SHARD], w_ref[0])
        for j in range(1, N_DEV):
            dh = dh + _dot_nt(duv[:, j * LRU_IN_SHARD:(j + 1) * LRU_IN_SHARD], w_ref[j])
        dxn, dgn = _norm_bwd(x_ref[...], g_ref[...], dh)
        gx_ref[...] = dx1_ref[...] + dxn
        gn_ref[...] += dgn

    tile = pl.BlockSpec((ts, D_MODEL), lambda i: (i, 0))
    return pl.pallas_call(
        body, name="lru_in_bwd", grid=(s // ts,),
        in_specs=[pl.BlockSpec((ts, 2 * LRU_WIDTH), lambda i: (i, 0)),
                  _const_spec((N_DEV, D_MODEL, LRU_IN_SHARD)), tile, tile,
                  _const_spec((1, D_MODEL))],
        out_specs=[tile, pl.BlockSpec((1, D_MODEL), lambda i: (0, 0))],
        out_shape=[jax.ShapeDtypeStruct((s, D_MODEL), F32),
                   jax.ShapeDtypeStruct((1, D_MODEL), F32)],
        compiler_params=_params(),
    )(du, w_in, x, dx1, g0)


def _weight_grad(a, b, ts, name, scale=1.0, col_shards=1):
    s, ka = a.shape
    nb = b.shape[1]
    nt = s // ts
    per = nb // col_shards

    def body(a_ref, b_ref, o_ref):
        @pl.when(pl.program_id(0) == 0)
        def _():
            o_ref[...] = jnp.zeros_like(o_ref)

        if col_shards == 1:
            o_ref[...] += _dot_tn(a_ref[...], b_ref[...])
        else:
            av, bv = a_ref[...], b_ref[...]
            for j in range(col_shards):
                o_ref[j] += _dot_tn(av, bv[:, j * per:(j + 1) * per])
        if scale != 1.0:
            @pl.when(pl.program_id(0) == nt - 1)
            def _():
                o_ref[...] = o_ref[...] * scale

    out_dims = (ka, nb) if col_shards == 1 else (col_shards, ka, per)
    return pl.pallas_call(
        body, name=name, grid=(nt,),
        in_specs=[pl.BlockSpec((ts, ka), lambda i: (i, 0)),
                  pl.BlockSpec((ts, nb), lambda i: (i, 0))],
        out_specs=pl.BlockSpec(out_dims, lambda i: (0,) * len(out_dims)),
        out_shape=jax.ShapeDtypeStruct(out_dims, F32),
        compiler_params=_params(),
    )(a, b)


def _sum_parts(gp_ref):
    g = gp_ref[0].astype(F32)
    for k in range(1, gp_ref.shape[0]):
        g = g + gp_ref[k].astype(F32)
    return g


def _adamw(g_parts, w, m, v, tr, name):
    nparts, rows, cols = g_parts.shape

    def body(gp_ref, w_ref, m_ref, v_ref, g_ref, d_ref, mo_ref, vo_ref):
        g = _sum_parts(gp_ref)
        m2 = ADAM_B1 * m_ref[...] + (1.0 - ADAM_B1) * g
        v2 = ADAM_B2 * v_ref[...] + (1.0 - ADAM_B2) * (g * g)
        m_hat = m2 / (1.0 - ADAM_B1 ** ADAM_STEP)
        v_hat = v2 / (1.0 - ADAM_B2 ** ADAM_STEP)
        g_ref[...] = g
        d_ref[...] = (-ADAM_LR) * (m_hat / (jnp.sqrt(v_hat) + ADAM_EPS) + ADAM_WD * w_ref[...])
        mo_ref[...] = m2
        vo_ref[...] = v2

    tile = pl.BlockSpec((tr, cols), lambda i: (i, 0))
    out = jax.ShapeDtypeStruct((rows, cols), F32)
    return pl.pallas_call(
        body, name=name, grid=(rows // tr,),
        in_specs=[pl.BlockSpec((nparts, tr, cols), lambda i: (0, i, 0)), tile, tile, tile],
        out_specs=[tile, tile, tile, tile],
        out_shape=[out, out, out, out],
        compiler_params=_params(),
    )(g_parts, w, m, v)


def _reduce_parts(g_parts, name):
    _, rows, cols = g_parts.shape

    def body(gp_ref, g_ref):
        g_ref[...] = _sum_parts(gp_ref)

    return pl.pallas_call(
        body, name=name,
        out_shape=jax.ShapeDtypeStruct((rows, cols), F32),
        compiler_params=pltpu.CompilerParams(vmem_limit_bytes=VMEM_LIMIT_BYTES),
    )(g_parts)


def _mesh_pos():
    ix, iy, ic = lax.axis_index("x"), lax.axis_index("y"), lax.axis_index("c")
    return ix, iy, ic


def _peer(ix, iy, ic, mask):
    px = 1 - ix if mask & 4 else ix
    py = 1 - iy if mask & 2 else iy
    pc = 1 - ic if mask & 1 else ic
    return (px, py, pc), 4 * px + 2 * py + pc


def _exchange(arrays, scatter, name):
    n = len(arrays)

    def body(*refs):
        x_refs, o_refs = refs[:n], refs[n:2 * n]
        send_sems, recv_sems, local_sems = refs[2 * n:]
        ix, iy, ic = _mesh_pos()
        me = 4 * ix + 2 * iy + ic

        def src(a, dest):
            return x_refs[a].at[dest] if scatter else x_refs[a]

        local = [pltpu.make_async_copy(src(a, me), o_refs[a].at[me], local_sems.at[a])
                 for a in range(n)]
        for cp in local:
            cp.start()
        sends = []
        for mask in range(1, N_DEV):
            peer, pidx = _peer(ix, iy, ic, mask)
            for a in range(n):
                cp = pltpu.make_async_remote_copy(
                    src_ref=src(a, pidx), dst_ref=o_refs[a].at[me],
                    send_sem=send_sems.at[a, mask - 1], recv_sem=recv_sems.at[a, mask - 1],
                    device_id=peer, device_id_type=pl.DeviceIdType.MESH)
                cp.start()
                sends.append(cp)
        for mask in range(1, N_DEV):
            peer, pidx = _peer(ix, iy, ic, mask)
            for a in range(n):
                pltpu.make_async_remote_copy(
                    src_ref=src(a, me), dst_ref=o_refs[a].at[pidx],
                    send_sem=send_sems.at[a, mask - 1], recv_sem=recv_sems.at[a, mask - 1],
                    device_id=peer, device_id_type=pl.DeviceIdType.MESH).wait_recv()
        for cp in sends:
            cp.wait_send()
        for cp in local:
            cp.wait()

    out_shape = [jax.ShapeDtypeStruct(x.shape if scatter else (N_DEV,) + x.shape, x.dtype)
                 for x in arrays]
    return pl.pallas_call(
        body, name=name,
        in_specs=[pl.BlockSpec(memory_space=pl.ANY)] * n,
        out_specs=[pl.BlockSpec(memory_space=pl.ANY)] * n,
        out_shape=out_shape,
        scratch_shapes=[pltpu.SemaphoreType.DMA((n, N_DEV - 1)),
                        pltpu.SemaphoreType.DMA((n, N_DEV - 1)),
                        pltpu.SemaphoreType.DMA((n,))],
    )(*arrays)


def _gather_two_level(arrays, name):
    n = len(arrays)

    def body(*refs):
        x_refs, o_refs = refs[:n], refs[n:2 * n]
        send_sems, recv_sems, local_sems = refs[2 * n:]
        ix, iy, ic = _mesh_pos()
        me, sibling = (ix, iy, ic), (ix, iy, 1 - ic)
        chips = [(1 - ix, iy), (ix, 1 - iy), (1 - ix, 1 - iy)]

        def idx(px, py, pc):
            return 4 * px + 2 * py + pc

        def copy(a, k, block, to, src=None):
            dst = o_refs[a].at[idx(*block)]
            return pltpu.make_async_remote_copy(
                src_ref=dst if src is None else src, dst_ref=dst,
                send_sem=send_sems.at[a, k], recv_sem=recv_sems.at[a, k],
                device_id=to, device_id_type=pl.DeviceIdType.MESH)

        local = [pltpu.make_async_copy(x_refs[a], o_refs[a].at[idx(*me)], local_sems.at[a])
                 for a in range(n)]
        for cp in local:
            cp.start()
        first = []
        for a in range(n):
            first.append(copy(a, 0, me, sibling, src=x_refs[a]))
            first += [copy(a, 1 + j, me, (*chip, ic), src=x_refs[a])
                      for j, chip in enumerate(chips)]
        for cp in first:
            cp.start()
        passed = []
        for j, chip in enumerate(chips):
            for a in range(n):
                copy(a, 1 + j, (*chip, ic), me).wait_recv()
                cp = copy(a, 4 + j, (*chip, ic), sibling)
                cp.start()
                passed.append(cp)
        for a in range(n):
            copy(a, 0, sibling, me).wait_recv()
            for j, chip in enumerate(chips):
                copy(a, 4 + j, (*chip, 1 - ic), me).wait_recv()
        for cp in first + passed:
            cp.wait_send()
        for cp in local:
            cp.wait()

    return pl.pallas_call(
        body, name=name,
        in_specs=[pl.BlockSpec(memory_space=pl.ANY)] * n,
        out_specs=[pl.BlockSpec(memory_space=pl.ANY)] * n,
        out_shape=[jax.ShapeDtypeStruct((N_DEV,) + x.shape, x.dtype) for x in arrays],
        scratch_shapes=[pltpu.SemaphoreType.DMA((n, N_DEV - 1)),
                        pltpu.SemaphoreType.DMA((n, N_DEV - 1)),
                        pltpu.SemaphoreType.DMA((n,))],
    )(*arrays)


def _swap_sibling(arrays, name):
    n = len(arrays)
    n_chips = N_DEV // 2

    def body(*refs):
        x_refs, got_refs = refs[:n], refs[n:2 * n]
        send_sems, recv_sems = refs[2 * n:]
        ix, iy, ic = _mesh_pos()
        sibling = (ix, iy, 1 - ic)
        sends = []
        for a in range(n):
            for q in range(n_chips):
                cp = pltpu.make_async_remote_copy(
                    src_ref=x_refs[a].at[q, 1 - ic], dst_ref=got_refs[a].at[q],
                    send_sem=send_sems.at[a, q], recv_sem=recv_sems.at[a, q],
                    device_id=sibling, device_id_type=pl.DeviceIdType.MESH)
                cp.start()
                sends.append(cp)
        for cp in sends:
            cp.wait()

    return pl.pallas_call(
        body, name=name,
        in_specs=[pl.BlockSpec(memory_space=pl.ANY)] * n,
        out_specs=[pl.BlockSpec(memory_space=pl.ANY)] * n,
        out_shape=[jax.ShapeDtypeStruct((n_chips,) + x.shape[2:], x.dtype) for x in arrays],
        scratch_shapes=[pltpu.SemaphoreType.DMA((n, n_chips)),
                        pltpu.SemaphoreType.DMA((n, n_chips))],
    )(*arrays)


def _exchange_chips(arrays, name):
    n = len(arrays)
    n_chips = N_DEV // 2

    def body(*refs):
        x_refs, o_refs = refs[:n], refs[n:2 * n]
        send_sems, recv_sems, local_sems = refs[2 * n:]
        ix, iy, ic = _mesh_pos()
        my_chip = 2 * ix + iy
        local = [pltpu.make_async_copy(x_refs[a].at[my_chip], o_refs[a].at[my_chip],
                                       local_sems.at[a]) for a in range(n)]
        for cp in local:
            cp.start()
        sends = []
        for mask in range(1, n_chips):
            px = 1 - ix if mask & 2 else ix
            py = 1 - iy if mask & 1 else iy
            for a in range(n):
                cp = pltpu.make_async_remote_copy(
                    src_ref=x_refs[a].at[2 * px + py], dst_ref=o_refs[a].at[my_chip],
                    send_sem=send_sems.at[a, mask - 1], recv_sem=recv_sems.at[a, mask - 1],
                    device_id=(px, py, ic), device_id_type=pl.DeviceIdType.MESH)
                cp.start()
                sends.append(cp)
        for mask in range(1, n_chips):
            px = 1 - ix if mask & 2 else ix
            py = 1 - iy if mask & 1 else iy
            for a in range(n):
                pltpu.make_async_remote_copy(
                    src_ref=x_refs[a].at[my_chip], dst_ref=o_refs[a].at[2 * px + py],
                    send_sem=send_sems.at[a, mask - 1], recv_sem=recv_sems.at[a, mask - 1],
                    device_id=(px, py, ic), device_id_type=pl.DeviceIdType.MESH).wait_recv()
        for cp in sends:
            cp.wait_send()
        for cp in local:
            cp.wait()

    return pl.pallas_call(
        body, name=name,
        in_specs=[pl.BlockSpec(memory_space=pl.ANY)] * n,
        out_specs=[pl.BlockSpec(memory_space=pl.ANY)] * n,
        out_shape=[jax.ShapeDtypeStruct(x.shape, x.dtype) for x in arrays],
        scratch_shapes=[pltpu.SemaphoreType.DMA((n, n_chips - 1)),
                        pltpu.SemaphoreType.DMA((n, n_chips - 1)),
                        pltpu.SemaphoreType.DMA((n,))],
    )(*arrays)


def _pair_sum(core, x, got, name):
    nq, rows, cols = got.shape

    def body(c_ref, x_ref, g_ref, o_ref):
        o_ref[...] = (x_ref[...] + g_ref[...]).astype(BF16)

    blk = pl.BlockSpec((None, rows, cols), lambda q, c: (q, 0, 0))
    return pl.pallas_call(
        body, name=name,
        grid_spec=pltpu.PrefetchScalarGridSpec(
            num_scalar_prefetch=1, grid=(nq,),
            in_specs=[pl.BlockSpec((None, None, rows, cols), lambda q, c: (q, c[0], 0, 0)), blk],
            out_specs=blk),
        out_shape=jax.ShapeDtypeStruct(got.shape, BF16),
        compiler_params=_params(),
    )(core, x, got)


def _pad_heads_cols(w):
    k = w.shape[0]
    w = w.reshape(k, HEADS, HEAD_DIM)
    return jnp.pad(w, ((0, 0), (0, 0), (0, HEAD_PAD - HEAD_DIM))).reshape(k, FOX_PAD)


def _unpad_heads_cols(w):
    k = w.shape[0]
    return w.reshape(k, HEADS, HEAD_PAD)[:, :, :HEAD_DIM].reshape(k, HEADS * HEAD_DIM)


def _selectors():
    r = lax.broadcasted_iota(jnp.int32, (3 * LANES, FOX_PAD), 0)
    c = lax.broadcasted_iota(jnp.int32, (3 * LANES, FOX_PAD), 1)
    part, head_r = r // LANES, r % LANES
    head_c, lane_c = c // HEAD_PAD, c % HEAD_PAD
    same = (head_r == head_c) & (head_r < HEADS)
    sel_q = jnp.where(same & (lane_c == LANE_RB + part), 1.0, 0.0)
    sel_k = jnp.where(same & (lane_c == LANE_CK + part), -1.0, 0.0)
    sel = jnp.stack([sel_q, sel_k, jnp.zeros_like(sel_q)]).astype(BF16)
    lane = lax.broadcasted_iota(jnp.int32, (1, FOX_PAD), 1) % HEAD_PAD
    ones_q = jnp.where((lane >= LANE_CK) & (lane < LANE_CK + 3), 1.0, 0.0)
    ones_k = jnp.where(((lane >= LANE_RB) & (lane < LANE_RB + 3))
                       | ((lane >= LANE_LSE) & (lane < LANE_LSE + 3)), 1.0, 0.0)
    ones_v = jnp.where((lane >= LANE_ONE_V) & (lane < LANE_ONE_V + 2), 1.0, 0.0)
    bias = jnp.stack([ones_q, ones_k, ones_v]).astype(F32)
    return sel, bias


def _local_step(x, target, norm_g, final_g, w_in8, conv_w, conv_b, wa, ba, wx, bx, a_param,
                w_out_b, fox_in8, b_f, fox_out_b, blk=512, ts=256):
    qk_scale = 1.0 / (HEAD_DIM ** 0.5)
    g0, g1 = norm_g[0:1], norm_g[1:2]
    gf = final_g.reshape(1, D_MODEL)
    wa_b, wx_b = wa.astype(BF16), wx.astype(BF16)
    fox_w_in = jnp.transpose(fox_in8, (1, 0, 2)).reshape(D_MODEL, FOX_IN_COLS)
    wq = _pad_heads_cols(fox_w_in[:, 0:1024]) * qk_scale
    wk = _pad_heads_cols(fox_w_in[:, 1024:2048])
    wv = _pad_heads_cols(fox_w_in[:, 2048:3072])
    wg = _pad_heads_cols(fox_w_in[:, 3072:4096])
    wf_b = jnp.pad(fox_w_in[:, 4096:], ((0, 0), (0, LANES - HEADS)))
    w4 = jnp.stack([wq, wk, wv, wg]).astype(BF16)
    bf_pad = jnp.pad(b_f, ((0, 0), (0, LANES - HEADS)))
    fo = fox_out_b.reshape(HEADS, HEAD_DIM, D_MODEL)
    fo_b = jnp.pad(fo, ((0, 0), (0, HEAD_PAD - HEAD_DIM), (0, 0))).reshape(FOX_PAD, D_MODEL)
    sel, bias = _selectors()

    xb, gate1, h0 = _lru_in_fwd(x, g0, w_in8, ts)
    y1, hs = _lru_core_fwd(xb, gate1, conv_w, conv_b, wa_b, ba, wx_b, bx, a_param, ts)
    x1, h1, f, cparts = _fox_pre_fwd(x, y1, w_out_b, g1, wf_b, bf_pad, ts)
    qkv = _fox_proj_fwd(h1, cparts, w4[0:3], sel, bias, BF16, ts, "fox_proj_qkv")
    gate2 = _fox_proj_fwd(h1, None, w4[3:4], None, None, F32, ts, "fox_proj_gate")[0]
    o, qb = _attn_fwd(qkv, blk)
    dx2, y2, loss_acc, g_final = _fox_out_loss(o, gate2, fo_b, x1, target, gf, ts)

    do, dgate2 = _fox_out_bwd(dx2, fo_b, o, gate2, ts)
    dq, dk, dv, dcum = _attn_bwd(qb, qkv, do, blk)
    dx1, dx1b, df, g_norm1, g_bf = _fox_in_bwd(dq, dk, dv, dgate2, w4, wf_b, dcum, f, x1, dx2,
                                               g1, ts)
    du, g_wa, g_wx, g_vec = _lru_core_bwd(dx1b, w_out_b, xb, gate1, hs, conv_w, conv_b, wa_b, ba,
                                          wx_b, bx, a_param, jnp.transpose(wa_b, (0, 2, 1)),
                                          jnp.transpose(wx_b, (0, 2, 1)), ts)
    grad_x, g_norm0 = _lru_in_bwd(du, w_in8, x, dx1, g0, ts)

    tw = 512
    g_lru_w_in = _weight_grad(h0, du, tw, "grad_lru_w_in", col_shards=N_DEV)
    g_lru_w_out = _weight_grad(y1, dx1b, tw, "grad_lru_w_out")
    g_q = _weight_grad(h1, dq, tw, "grad_fox_wq", scale=qk_scale)
    g_k = _weight_grad(h1, dk, tw, "grad_fox_wk")
    g_v = _weight_grad(h1, dv, tw, "grad_fox_wv")
    g_g = _weight_grad(h1, dgate2, tw, "grad_fox_wg")
    g_f = _weight_grad(h1, df, tw, "grad_fox_wf")
    g_fox_w_in = jnp.concatenate(
        [_unpad_heads_cols(g_q), _unpad_heads_cols(g_k), _unpad_heads_cols(g_v),
         _unpad_heads_cols(g_g), g_f[:, :HEADS]], axis=1)
    g_fox_w_in = jnp.transpose(g_fox_w_in.reshape(D_MODEL, N_DEV, FOX_IN_SHARD), (1, 0, 2))
    g_fo = _weight_grad(y2, dx2.astype(BF16), tw, "grad_fox_w_out")
    g_fox_w_out = g_fo.reshape(HEADS, HEAD_PAD, D_MODEL)[:, :HEAD_DIM].reshape(
        HEADS * HEAD_DIM, D_MODEL)

    grads = dict(
        norm_g=jnp.concatenate([g_norm0, g_norm1], axis=0), final_g=g_final[0],
        lru_w_in=g_lru_w_in, lru_conv_w=g_vec[0:4], lru_conv_b=g_vec[4:5], lru_wa=g_wa,
        lru_ba=g_vec[5:6], lru_wx=g_wx, lru_bx=g_vec[6:7], lru_a_param=g_vec[7:8],
        lru_w_out=g_lru_w_out, fox_w_in=g_fox_w_in, fox_b_f=g_bf[:, :HEADS],
        fox_w_out=g_fox_w_out)
    return loss_acc[0, 0], grad_x, grads


SMALL =("norm_g", "final_g", "lru_conv_b", "lru_wa", "lru_ba", "lru_wx", "lru_bx", "lru_a_param",
         "fox_b_f")
ALL_WEIGHTS = ("norm_g", "final_g", "lru_w_in", "lru_conv_w", "lru_conv_b", "lru_wa", "lru_ba",
               "lru_wx", "lru_bx", "lru_a_param", "lru_w_out", "fox_w_in", "fox_b_f", "fox_w_out")


def _pack_small(d):
    rows = []
    for n in SMALL:
        a = d[n].reshape(-1)
        if a.shape[0] % LANES:
            a = jnp.pad(a, (0, LANES - a.shape[0] % LANES))
        rows.append(a.reshape(-1, LANES))
    packed = jnp.concatenate(rows, axis=0)
    return jnp.pad(packed, ((0, N_DEV * SMALL_CHUNK_ROWS - packed.shape[0]), (0, 0)))


def _unpack_small(packed, like):
    out, off = {}, 0
    for n, nrows in zip(SMALL, SMALL_ROWS):
        size = like[n].size
        out[n] = packed[off:off + nrows].reshape(-1)[:size].reshape(like[n].shape)
        off += nrows
    return out


def kernel(x, norm_g, final_g, lru_w_in, lru_conv_w, lru_conv_b, lru_wa, lru_ba, lru_wx, lru_bx, lru_a_param, lru_w_out, fox_w_in, fox_b_f, fox_w_out, loss_target, m_norm_g, m_final_g, m_lru_w_in, m_lru_conv_w, m_lru_conv_b, m_lru_wa, m_lru_ba, m_lru_wx, m_lru_bx, m_lru_a_param, m_lru_w_out, m_fox_w_in, m_fox_b_f, m_fox_w_out, v_norm_g, v_final_g, v_lru_w_in, v_lru_conv_w, v_lru_conv_b, v_lru_wa, v_lru_ba, v_lru_wx, v_lru_bx, v_lru_a_param, v_lru_w_out, v_fox_w_in, v_fox_b_f, v_fox_w_out):
    w_loc = dict(norm_g=norm_g, final_g=final_g, lru_w_in=lru_w_in, lru_conv_w=lru_conv_w,
                 lru_conv_b=lru_conv_b, lru_wa=lru_wa, lru_ba=lru_ba, lru_wx=lru_wx, lru_bx=lru_bx,
                 lru_a_param=lru_a_param, lru_w_out=lru_w_out, fox_w_in=fox_w_in, fox_b_f=fox_b_f,
                 fox_w_out=fox_w_out)
    m_loc = dict(norm_g=m_norm_g, final_g=m_final_g, lru_w_in=m_lru_w_in, lru_conv_w=m_lru_conv_w,
                 lru_conv_b=m_lru_conv_b, lru_wa=m_lru_wa, lru_ba=m_lru_ba, lru_wx=m_lru_wx,
                 lru_bx=m_lru_bx, lru_a_param=m_lru_a_param, lru_w_out=m_lru_w_out,
                 fox_w_in=m_fox_w_in, fox_b_f=m_fox_b_f, fox_w_out=m_fox_w_out)
    v_loc = dict(norm_g=v_norm_g, final_g=v_final_g, lru_w_in=v_lru_w_in, lru_conv_w=v_lru_conv_w,
                 lru_conv_b=v_lru_conv_b, lru_wa=v_lru_wa, lru_ba=v_lru_ba, lru_wx=v_lru_wx,
                 lru_bx=v_lru_bx, lru_a_param=v_lru_a_param, lru_w_out=v_lru_w_out,
                 fox_w_in=v_fox_w_in, fox_b_f=v_fox_b_f, fox_w_out=v_fox_w_out)

    w_in8, conv8, w_out8, fox_in8, fox_out8 = _gather_two_level(
        [lru_w_in[0].astype(BF16), lru_conv_w[0], lru_w_out[0].astype(BF16),
         fox_w_in[0].astype(BF16), fox_w_out[0].astype(BF16)], "gather_weights")
    conv_full = jnp.transpose(conv8, (1, 0, 2)).reshape(CONV_WIDTH, LRU_WIDTH)

    loss, grad_x, grads = _local_step(
        x[0], loss_target[0], norm_g, final_g, w_in8, conv_full, lru_conv_b, lru_wa[0], lru_ba,
        lru_wx[0], lru_bx, lru_a_param, w_out8.reshape(LRU_WIDTH, D_MODEL), fox_in8, fox_b_f,
        fox_out8.reshape(HEADS * HEAD_DIM, D_MODEL))

    n_chips = N_DEV // 2
    conv_send = jnp.transpose(grads["lru_conv_w"].reshape(CONV_WIDTH, N_DEV, -1), (1, 0, 2))
    names = ("lru_w_in", "lru_conv_w", "lru_w_out", "fox_w_in", "fox_w_out", "small")
    send = [grads["lru_w_in"], conv_send, grads["lru_w_out"].reshape(N_DEV, -1, D_MODEL),
            grads["fox_w_in"], grads["fox_w_out"].reshape(N_DEV, -1, D_MODEL),
            _pack_small(grads).reshape(N_DEV, SMALL_CHUNK_ROWS, LANES)]
    send = [a.reshape((n_chips, 2) + a.shape[1:]) for a in send]
    got = _swap_sibling(send, "swap_grads")
    core = lax.axis_index("c").astype(jnp.int32).reshape(1)
    chip_sums = [_pair_sum(core, a, b, "pair_sum_" + n) for n, a, b in zip(names, send, got)]
    r_w_in, r_conv, r_w_out, r_fox_in, r_fox_out, r_small = _exchange_chips(
        chip_sums, "scatter_grads")

    out = {}
    for n, recv, tr in (("lru_w_in", r_w_in, 256), ("lru_conv_w", r_conv, CONV_WIDTH),
                        ("lru_w_out", r_w_out, 96), ("fox_w_in", r_fox_in, 128),
                        ("fox_w_out", r_fox_out, 64)):
        res = _adamw(recv, w_loc[n][0], m_loc[n][0], v_loc[n][0], tr, "adamw_" + n)
        out[n] = [a[None] for a in res]

    g_chunk = _reduce_parts(r_small, "reduce_small_grads")
    g_small, = _exchange([g_chunk], False, "gather_small_grads")
    g_small = g_small.reshape(1, N_DEV * SMALL_CHUNK_ROWS, LANES)
    res = _adamw(g_small, _pack_small(w_loc), _pack_small(m_loc), _pack_small(v_loc),
                 N_DEV * SMALL_CHUNK_ROWS, "adamw_replicated")
    small_out = [_unpack_small(a, w_loc) for a in res]
    for n in SMALL:
        out[n] = [d[n] for d in small_out]

    loss = lax.psum(loss, ("x", "y", "c"))
    return (loss, grad_x[None], *[out[n][0] for n in ALL_WEIGHTS], *[out[n][1] for n in ALL_WEIGHTS],
            *[out[n][2] for n in ALL_WEIGHTS], *[out[n][3] for n in ALL_WEIGHTS])
```

```python
import functools

import jax
import jax.numpy as jnp
from jax import lax
from jax.experimental import pallas as pl
from jax.experimental.pallas import tpu as pltpu

F32 = jnp.float32
BF16 = jnp.bfloat16

D_MODEL = 1024
LRU_WIDTH = 1536
LRU_BLOCKS = 12
LRU_BLOCK_W = 128
CONV_WIDTH = 4
LRU_C = 8.0
HEADS = 16
HEAD_DIM = 64
HEAD_PAD = 128
FOX_PAD = HEADS * HEAD_PAD
HEADS_PER_STEP = 2
EPS = 1e-6
NEG_BIG = -1e30
N_DEV = 8

ADAM_LR = 0.001
ADAM_B1 = 0.9
ADAM_B2 = 0.999
ADAM_EPS = 1e-08
ADAM_WD = 0.01
ADAM_STEP = 10

LANE_RB = 64
LANE_CK = 67
LANE_LSE = 70
LANE_ONE_V = 64

VMEM_LIMIT_BYTES = 56 * 1024 * 1024
LANES = 128
SUBLANES = 8

LRU_IN_SHARD = 2 * LRU_WIDTH // N_DEV
FOX_IN_COLS = 4 * HEADS * HEAD_DIM + HEADS
FOX_IN_SHARD = FOX_IN_COLS // N_DEV

SMALL_ROWS = (16, 8, 12, 1536, 12, 1536, 12, 12, 1)
SMALL_CHUNK_ROWS = 400
assert sum(SMALL_ROWS) <= N_DEV * SMALL_CHUNK_ROWS


def _params(n_grid_axes=1):
    return pltpu.CompilerParams(
        dimension_semantics=("arbitrary",) * n_grid_axes,
        vmem_limit_bytes=VMEM_LIMIT_BYTES)


def _const_spec(shape):
    nd = len(shape)
    return pl.BlockSpec(shape, lambda *_: (0,) * nd, pipeline_mode=pl.Buffered(1))


def _shift_down(x, k, fill):
    rows = lax.broadcasted_iota(jnp.int32, x.shape, 0)
    return jnp.where(rows >= k, pltpu.roll(x, k, 0), fill)


def _shift_up(x, k, fill):
    n = x.shape[0]
    rows = lax.broadcasted_iota(jnp.int32, x.shape, 0)
    return jnp.where(rows < n - k, pltpu.roll(x, n - k, 0), fill)


def _scan_rows(a, b, reverse=False):
    n = a.shape[0]
    shift = _shift_up if reverse else _shift_down
    k = 1
    while k < n:
        b = a * shift(b, k, 0.0) + b
        a = a * shift(a, k, 1.0)
        k *= 2
    return a, b


def _cumsum_rows(x, reverse=False):
    n = x.shape[0]
    shift = _shift_up if reverse else _shift_down
    k = 1
    while k < n:
        x = x + shift(x, k, 0.0)
        k *= 2
    return x


def _rstd(x):
    return lax.rsqrt(jnp.mean(x * x, axis=-1, keepdims=True) + EPS)


def _norm_bwd(x, g, dh):
    rstd = _rstd(x)
    xhat = x * rstd
    dg = jnp.sum(dh * xhat, axis=0, keepdims=True)
    dxh = dh * g
    dx = rstd * (dxh - xhat * jnp.mean(dxh * xhat, axis=-1, keepdims=True))
    return dx, dg


def _split3(x):
    hi = x.astype(BF16)
    r1 = x - hi.astype(F32)
    mid = r1.astype(BF16)
    lo = (r1 - mid.astype(F32)).astype(BF16)
    return hi, mid, lo


def _sigmoid(x):
    return jax.nn.sigmoid(x)


def _dot(a, b):
    return jnp.dot(a, b, preferred_element_type=F32)


def _dot_nt(a, b):
    return lax.dot_general(a, b, (((1,), (1,)), ((), ())), preferred_element_type=F32)


def _dot_tn(a, b):
    return lax.dot_general(a, b, (((0,), (0,)), ((), ())), preferred_element_type=F32)


def _conv_taps(xb, prev8):
    rows8 = lax.broadcasted_iota(jnp.int32, prev8.shape, 0)
    taps = [xb]
    for j in range(1, CONV_WIDTH):
        r = pltpu.roll(xb, j, 0)
        p = pltpu.roll(prev8, j, 0)
        head = jnp.where(rows8 < j, p, r[0:SUBLANES])
        taps.append(jnp.concatenate([head, r[SUBLANES:]], axis=0))
    return taps


def _lru_pre(taps, cw, cb, wa_ref, ba, wx_ref, bx, a_param):
    xc = cb + cw[3:4] * taps[0] + cw[2:3] * taps[1] + cw[1:2] * taps[2] + cw[0:1] * taps[3]
    xcb = xc.astype(BF16)
    ra, ia = [], []
    for n in range(LRU_BLOCKS):
        blk = xcb[:, n * LRU_BLOCK_W:(n + 1) * LRU_BLOCK_W]
        ra.append(_dot(blk, wa_ref[n]))
        ia.append(_dot(blk, wx_ref[n]))
    r = _sigmoid(jnp.concatenate(ra, axis=1) + ba)
    i = _sigmoid(jnp.concatenate(ia, axis=1) + bx)
    z = -a_param
    sp = jnp.maximum(z, 0.0) + jnp.log1p(jnp.exp(-jnp.abs(z)))
    log_a = (-LRU_C) * r * sp
    a = jnp.exp(log_a)
    one_minus_a2 = -jnp.tanh(log_a) * (a * a + 1.0)
    mult = jnp.sqrt(one_minus_a2)
    return xc, xcb, r, i, sp, a, mult


def _lru_in_fwd(x, g0, w_in, ts):
    s = x.shape[0]
    half = N_DEV // 2

    def body(x_ref, g_ref, w_ref, xb_ref, gate_ref, h_ref):
        xv = x_ref[...]
        h = (xv * _rstd(xv) * g_ref[...]).astype(BF16)
        u = [_dot(h, w_ref[j]) for j in range(N_DEV)]
        xb_ref[...] = jnp.concatenate(u[:half], axis=1)
        gate_ref[...] = jnp.concatenate(u[half:], axis=1)
        h_ref[...] = h

    return pl.pallas_call(
        body, name="lru_in_fwd", grid=(s // ts,),
        in_specs=[pl.BlockSpec((ts, D_MODEL), lambda i: (i, 0)),
                  _const_spec((1, D_MODEL)),
                  _const_spec((N_DEV, D_MODEL, LRU_IN_SHARD))],
        out_specs=[pl.BlockSpec((ts, LRU_WIDTH), lambda i: (i, 0)),
                   pl.BlockSpec((ts, LRU_WIDTH), lambda i: (i, 0)),
                   pl.BlockSpec((ts, D_MODEL), lambda i: (i, 0))],
        out_shape=[jax.ShapeDtypeStruct((s, LRU_WIDTH), F32),
                   jax.ShapeDtypeStruct((s, LRU_WIDTH), F32),
                   jax.ShapeDtypeStruct((s, D_MODEL), BF16)],
        compiler_params=_params(),
    )(x, g0, w_in)


def _lru_core_fwd(xb, gate, cw, cb, wa, ba, wx, bx, a_param, ts):
    s = xb.shape[0]

    def body(xb_ref, gate_ref, cw_ref, cb_ref, wa_ref, ba_ref, wx_ref, bx_ref, ap_ref,
             y_ref, hs_ref, prev_ref, hcar_ref):
        @pl.when(pl.program_id(0) == 0)
        def _():
            prev_ref[...] = jnp.zeros_like(prev_ref)
            hcar_ref[...] = jnp.zeros_like(hcar_ref)

        xbv = xb_ref[...]
        taps = _conv_taps(xbv, prev_ref[...])
        xc, _, _, i, _, a, mult = _lru_pre(taps, cw_ref[...], cb_ref[...], wa_ref, ba_ref[...],
                                           wx_ref, bx_ref[...], ap_ref[...])
        bterm = mult * (i * xc)
        cum_a, hloc = _scan_rows(a, bterm)
        hs = cum_a * hcar_ref[SUBLANES - 1:SUBLANES, :] + hloc
        gv = gate_ref[...]
        y_ref[...] = (hs * (gv * _sigmoid(gv))).astype(BF16)
        hs_ref[...] = hs
        prev_ref[...] = xbv[ts - SUBLANES:, :]
        hcar_ref[...] = hs[ts - SUBLANES:, :]

    vec = _const_spec((1, LRU_WIDTH))
    blk = _const_spec((LRU_BLOCKS, LRU_BLOCK_W, LRU_BLOCK_W))
    tile = pl.BlockSpec((ts, LRU_WIDTH), lambda i: (i, 0))
    return pl.pallas_call(
        body, name="lru_core_fwd", grid=(s // ts,),
        in_specs=[tile, tile, _const_spec((CONV_WIDTH, LRU_WIDTH)), vec, blk, vec, blk, vec, vec],
        out_specs=[tile, tile],
        out_shape=[jax.ShapeDtypeStruct((s, LRU_WIDTH), BF16),
                   jax.ShapeDtypeStruct((s, LRU_WIDTH), F32)],
        scratch_shapes=[pltpu.VMEM((SUBLANES, LRU_WIDTH), F32),
                        pltpu.VMEM((SUBLANES, LRU_WIDTH), F32)],
        compiler_params=_params(),
    )(xb, gate, cw, cb, wa, ba, wx, bx, a_param)


def _fox_pre_fwd(x, y, w_out, g1, wf, bf, ts):
    s = x.shape[0]

    def body(x_ref, y_ref, w_ref, g_ref, wf_ref, bf_ref, x1_ref, h1_ref, f_ref, cp_ref, ccar_ref):
        @pl.when(pl.program_id(0) == 0)
        def _():
            ccar_ref[...] = jnp.zeros_like(ccar_ref)

        x1 = x_ref[...] + _dot(y_ref[...], w_ref[...])
        h1 = (x1 * _rstd(x1) * g_ref[...]).astype(BF16)
        f = _dot(h1, wf_ref[...]) + bf_ref[...]
        logsig = jnp.minimum(f, 0.0) - jnp.log1p(jnp.exp(-jnp.abs(f)))
        cum = _cumsum_rows(logsig) + ccar_ref[SUBLANES - 1:SUBLANES, :]
        hi, mid, lo = _split3(cum)
        x1_ref[...] = x1
        h1_ref[...] = h1
        f_ref[...] = f
        cp_ref[...] = jnp.concatenate([hi, mid, lo], axis=1)
        ccar_ref[...] = cum[ts - SUBLANES:, :]

    return pl.pallas_call(
        body, name="fox_pre_fwd", grid=(s // ts,),
        in_specs=[pl.BlockSpec((ts, D_MODEL), lambda i: (i, 0)),
                  pl.BlockSpec((ts, LRU_WIDTH), lambda i: (i, 0)),
                  _const_spec((LRU_WIDTH, D_MODEL)),
                  _const_spec((1, D_MODEL)),
                  _const_spec((D_MODEL, LANES)),
                  _const_spec((1, LANES))],
        out_specs=[pl.BlockSpec((ts, D_MODEL), lambda i: (i, 0)),
                   pl.BlockSpec((ts, D_MODEL), lambda i: (i, 0)),
                   pl.BlockSpec((ts, LANES), lambda i: (i, 0)),
                   pl.BlockSpec((ts, 3 * LANES), lambda i: (i, 0))],
        out_shape=[jax.ShapeDtypeStruct((s, D_MODEL), F32),
                   jax.ShapeDtypeStruct((s, D_MODEL), BF16),
                   jax.ShapeDtypeStruct((s, LANES), F32),
                   jax.ShapeDtypeStruct((s, 3 * LANES), BF16)],
        scratch_shapes=[pltpu.VMEM((SUBLANES, LANES), F32)],
        compiler_params=_params(),
    )(x, y, w_out, g1, wf, bf)


def _fox_proj_fwd(h1, cparts, w, sel, bias, out_dtype, ts, name):
    s = h1.shape[0]
    ng = w.shape[0]
    use_sel = sel is not None

    def body(*refs):
        if use_sel:
            h_ref, cp_ref, w_ref, sel_ref, b_ref, o_ref = refs
            acc = _dot(h_ref[...], w_ref[...]) + _dot(cp_ref[...], sel_ref[...]) + b_ref[...]
        else:
            h_ref, w_ref, o_ref = refs
            acc = _dot(h_ref[...], w_ref[...])
        o_ref[...] = acc.astype(out_dtype)

    in_specs = [pl.BlockSpec((ts, D_MODEL), lambda j, i: (i, 0))]
    args = [h1]
    if use_sel:
        in_specs.append(pl.BlockSpec((ts, 3 * LANES), lambda j, i: (i, 0)))
        args.append(cparts)
    in_specs.append(pl.BlockSpec((None, D_MODEL, FOX_PAD), lambda j, i: (j, 0, 0)))
    args.append(w)
    if use_sel:
        in_specs.append(pl.BlockSpec((None, 3 * LANES, FOX_PAD), lambda j, i: (j, 0, 0)))
        in_specs.append(pl.BlockSpec((None, 1, FOX_PAD), lambda j, i: (j, 0, 0)))
        args += [sel, bias]
    return pl.pallas_call(
        body, name=name, grid=(ng, s // ts),
        in_specs=in_specs,
        out_specs=pl.BlockSpec((None, ts, FOX_PAD), lambda j, i: (j, i, 0)),
        out_shape=jax.ShapeDtypeStruct((ng, s, FOX_PAD), out_dtype),
        compiler_params=_params(2),
    )(*args)


def _attn_fwd(qkv, blk):
    s = qkv.shape[1]
    nblk = s // blk
    wide = 2 * blk
    heads = [slice(i * HEAD_PAD, (i + 1) * HEAD_PAD) for i in range(HEADS_PER_STEP)]

    def body(q_ref, k_ref, v_ref, o_ref, qb_ref, acc_ref, m_ref):
        qi = pl.program_id(1)
        row = lax.broadcasted_iota(jnp.int32, (blk, blk), 0)
        col = lax.broadcasted_iota(jnp.int32, (blk, blk), 1)
        lane = lax.broadcasted_iota(jnp.int32, (blk, HEAD_PAD), 1)
        qs = [q_ref[:, hd] for hd in heads]
        for i in range(HEADS_PER_STEP):
            acc_ref[i] = jnp.zeros((blk, HEAD_PAD), F32)
            m_ref[i] = jnp.full((blk, HEAD_PAD), NEG_BIG, F32)

        def step(k0, size, masked):
            scores = [_dot_nt(q, k_ref[pl.ds(k0, size), hd]) for q, hd in zip(qs, heads)]
            for i, (sc, hd) in enumerate(zip(scores, heads)):
                v = v_ref[pl.ds(k0, size), hd]
                if masked:
                    sc = jnp.where(col <= row, sc, NEG_BIG)
                m = m_ref[i]
                m_new = jnp.maximum(m, jnp.max(sc, axis=-1, keepdims=True))
                p = jnp.exp((sc - jnp.tile(m_new, (1, size // HEAD_PAD))).astype(BF16))
                acc_ref[i] = jnp.exp(m - m_new) * acc_ref[i] + _dot(p, v)
                m_ref[i] = m_new

        def wide_step(kk, _):
            step(pl.multiple_of(kk * wide, wide), wide, False)
            return 0

        lax.fori_loop(0, qi // 2, wide_step, 0)

        @pl.when(qi % 2 == 1)
        def _():
            step(pl.multiple_of((qi - 1) * blk, blk), blk, False)

        step(pl.multiple_of(qi * blk, blk), blk, True)
        for i, (q, hd) in enumerate(zip(qs, heads)):
            acc = acc_ref[i]
            l = jnp.broadcast_to(acc[:, LANE_ONE_V:LANE_ONE_V + 1], (blk, HEAD_PAD))
            o_ref[:, hd] = (acc / l).astype(BF16)
            hi, mid, lo = _split3(-(m_ref[i] + jnp.log(l)))
            qb_ref[:, hd] = jnp.where(lane == LANE_LSE, hi, jnp.where(
                lane == LANE_LSE + 1, mid, jnp.where(lane == LANE_LSE + 2, lo, q)))

    width = HEADS_PER_STEP * HEAD_PAD

    def whole(j):
        return pl.BlockSpec((None, s, width), lambda h, i: (j, 0, h))

    out_spec = pl.BlockSpec((blk, width), lambda h, i: (i, h))
    return pl.pallas_call(
        body, name="attn_fwd", grid=(HEADS // HEADS_PER_STEP, nblk),
        in_specs=[pl.BlockSpec((None, blk, width), lambda h, i: (0, i, h)), whole(1), whole(2)],
        out_specs=[out_spec, out_spec],
        out_shape=[jax.ShapeDtypeStruct((s, FOX_PAD), BF16),
                   jax.ShapeDtypeStruct((s, FOX_PAD), BF16)],
        scratch_shapes=[pltpu.VMEM((HEADS_PER_STEP, blk, HEAD_PAD), F32),
                        pltpu.VMEM((HEADS_PER_STEP, blk, HEAD_PAD), F32)],
        compiler_params=_params(2),
    )(qkv, qkv, qkv)


def _fox_out_loss(o, gate, w_out, x1, target, gf, ts):
    s = x1.shape[0]

    def body(o_ref, gt_ref, w_ref, x1_ref, t_ref, g_ref, dx2_ref, y2_ref, loss_ref, gfin_ref):
        @pl.when(pl.program_id(0) == 0)
        def _():
            loss_ref[...] = jnp.zeros_like(loss_ref)
            gfin_ref[...] = jnp.zeros_like(gfin_ref)

        gv = gt_ref[...]
        y2 = (o_ref[...] * (gv * _sigmoid(gv))).astype(BF16)
        x2 = x1_ref[...] + _dot(y2, w_ref[...])
        rstd = _rstd(x2)
        xhat = x2 * rstd
        g = g_ref[...]
        diff = xhat * g - t_ref[...]
        loss_ref[...] += 0.5 * jnp.sum(jnp.mean(diff * diff, axis=-1, keepdims=True))
        dy = diff * (1.0 / D_MODEL)
        gfin_ref[...] += jnp.sum(dy * xhat, axis=0, keepdims=True)
        dxh = dy * g
        dx2_ref[...] = rstd * (dxh - xhat * jnp.mean(dxh * xhat, axis=-1, keepdims=True))
        y2_ref[...] = y2

    return pl.pallas_call(
        body, name="fox_out_loss", grid=(s // ts,),
        in_specs=[pl.BlockSpec((ts, FOX_PAD), lambda i: (i, 0)),
                  pl.BlockSpec((ts, FOX_PAD), lambda i: (i, 0)),
                  _const_spec((FOX_PAD, D_MODEL)),
                  pl.BlockSpec((ts, D_MODEL), lambda i: (i, 0)),
                  pl.BlockSpec((ts, D_MODEL), lambda i: (i, 0)),
                  _const_spec((1, D_MODEL))],
        out_specs=[pl.BlockSpec((ts, D_MODEL), lambda i: (i, 0)),
                   pl.BlockSpec((ts, FOX_PAD), lambda i: (i, 0)),
                   pl.BlockSpec((SUBLANES, LANES), lambda i: (0, 0)),
                   pl.BlockSpec((1, D_MODEL), lambda i: (0, 0))],
        out_shape=[jax.ShapeDtypeStruct((s, D_MODEL), F32),
                   jax.ShapeDtypeStruct((s, FOX_PAD), BF16),
                   jax.ShapeDtypeStruct((SUBLANES, LANES), F32),
                   jax.ShapeDtypeStruct((1, D_MODEL), F32)],
        compiler_params=_params(),
    )(o, gate, w_out, x1, target, gf)


def _fox_out_bwd(dx2, w_out, o, gate, ts):
    s = dx2.shape[0]

    def body(dx_ref, w_ref, o_ref, gt_ref, do_ref, dg_ref):
        lane = lax.broadcasted_iota(jnp.int32, (ts, HEAD_PAD), 1)
        dy2 = _dot_nt(dx_ref[...].astype(BF16), w_ref[...])
        gv = gt_ref[...]
        sg = _sigmoid(gv)
        ov = o_ref[...]
        dov = dy2 * (gv * sg)
        dg_ref[...] = (dy2 * ov * (sg * (1.0 + gv * (1.0 - sg)))).astype(BF16)
        prod = dov * ov
        for h in range(HEADS):
            sl = slice(h * HEAD_PAD, (h + 1) * HEAD_PAD)
            delta = jnp.sum(prod[:, sl], axis=-1, keepdims=True)
            hi = delta.astype(BF16)
            lo = (delta - hi.astype(F32)).astype(BF16)
            do_h = dov[:, sl].astype(BF16)
            do_ref[:, sl] = jnp.where(lane == LANE_ONE_V, -hi,
                                      jnp.where(lane == LANE_ONE_V + 1, -lo, do_h))

    tile = pl.BlockSpec((ts, FOX_PAD), lambda i: (i, 0))
    return pl.pallas_call(
        body, name="fox_out_bwd", grid=(s // ts,),
        in_specs=[pl.BlockSpec((ts, D_MODEL), lambda i: (i, 0)),
                  _const_spec((FOX_PAD, D_MODEL)), tile, tile],
        out_specs=[tile, tile],
        out_shape=[jax.ShapeDtypeStruct((s, FOX_PAD), BF16),
                   jax.ShapeDtypeStruct((s, FOX_PAD), BF16)],
        compiler_params=_params(),
    )(dx2, w_out, o, gate)


def _attn_bwd(qb, qkv, do, blk):
    s = qb.shape[0]
    nblk = s // blk
    heads = [slice(i * HEAD_PAD, (i + 1) * HEAD_PAD) for i in range(HEADS_PER_STEP)]

    def body(q_ref, k_ref, v_ref, do_ref, dq_ref, dk_ref, dv_ref, dcum_ref, dq_acc, dkt_acc,
             dvt_acc, qt_ref, dot_ref):
        group = pl.program_id(0)
        kj = pl.program_id(1)
        row = lax.broadcasted_iota(jnp.int32, (blk, blk), 0)
        col = lax.broadcasted_iota(jnp.int32, (blk, blk), 1)
        lane = lax.broadcasted_iota(jnp.int32, (blk, LANES), 1)
        mine = [lane == group * HEADS_PER_STEP + i for i in range(HEADS_PER_STEP)]

        @pl.when(kj == 0)
        def _():
            dq_acc[...] = jnp.zeros_like(dq_acc)

            def transpose_block(bi, _):
                r0 = pl.multiple_of(bi * blk, blk)
                for i, hd in enumerate(heads):
                    qt_ref[i, bi] = q_ref[pl.ds(r0, blk), hd].T
                    dot_ref[i, bi] = do_ref[pl.ds(r0, blk), hd].T
                return 0

            lax.fori_loop(0, nblk, transpose_block, 0)

        @pl.when((group == 0) & (kj == 0))
        def _():
            dcum_ref[...] = jnp.zeros_like(dcum_ref)

        k0 = pl.multiple_of(kj * blk, blk)
        ks = [k_ref[:, hd] for hd in heads]
        vs = [v_ref[:, hd] for hd in heads]

        dkt_acc[...] = jnp.zeros_like(dkt_acc)
        dvt_acc[...] = jnp.zeros_like(dvt_acc)

        def step(qi, masked):
            q0 = pl.multiple_of(qi * blk, blk)
            qs = [q_ref[pl.ds(q0, blk), hd] for hd in heads]
            dos = [do_ref[pl.ds(q0, blk), hd] for hd in heads]
            scores = [_dot_nt(q, k) for q, k in zip(qs, ks)]
            dps = [_dot_nt(dov, v) for dov, v in zip(dos, vs)]
            for i, (hd, k, sc, dp) in enumerate(zip(heads, ks, scores, dps)):
                p = jnp.exp(sc.astype(BF16))
                if masked:
                    p = jnp.where(col <= row, p, jnp.zeros_like(p))
                ds = (p.astype(F32) * dp).astype(BF16)
                dvt_acc[i] += _dot(dot_ref[i, qi], p)
                dkt_acc[i] += _dot(qt_ref[i, qi], ds)
                dq_acc[pl.ds(q0, blk), hd] += _dot(ds, k)

        step(kj, True)

        def q_step(qi, _):
            step(qi, False)
            return 0

        lax.fori_loop(kj + 1, nblk, q_step, 0)
        dcum = dcum_ref[pl.ds(k0, blk), :]
        for i, (hd, mask) in enumerate(zip(heads, mine)):
            dk = dkt_acc[i].T
            dk_ref[:, hd] = dk.astype(BF16)
            dv_ref[:, hd] = dvt_acc[i].T.astype(BF16)
            dcum = jnp.where(mask, -dk[:, LANE_CK:LANE_CK + 1], dcum)
        dcum_ref[pl.ds(k0, blk), :] = dcum

        @pl.when(kj == nblk - 1)
        def _():
            def finish(bi, _):
                r0 = pl.multiple_of(bi * blk, blk)
                dcum = dcum_ref[pl.ds(r0, blk), :]
                for hd, mask in zip(heads, mine):
                    dq = dq_acc[pl.ds(r0, blk), hd]
                    dq_ref[pl.ds(r0, blk), hd] = dq.astype(BF16)
                    dcum = dcum + jnp.where(mask, dq[:, LANE_RB:LANE_RB + 1], 0.0)
                dcum_ref[pl.ds(r0, blk), :] = dcum
                return 0

            lax.fori_loop(0, nblk, finish, 0)

    width = HEADS_PER_STEP * HEAD_PAD
    whole = pl.BlockSpec((s, width), lambda h, j: (0, h))
    whole_in = pl.BlockSpec((s, width), lambda h, j: (0, h), pipeline_mode=pl.Buffered(1))
    part = pl.BlockSpec((blk, width), lambda h, j: (j, h))
    out = jax.ShapeDtypeStruct((s, FOX_PAD), BF16)
    return pl.pallas_call(
        body, name="attn_bwd", grid=(HEADS // HEADS_PER_STEP, nblk),
        in_specs=[whole_in,
                  pl.BlockSpec((None, blk, width), lambda h, j: (1, j, h)),
                  pl.BlockSpec((None, blk, width), lambda h, j: (2, j, h)),
                  whole_in],
        out_specs=[whole, part, part, pl.BlockSpec((s, LANES), lambda h, j: (0, 0))],
        out_shape=[out, out, out, jax.ShapeDtypeStruct((s, LANES), F32)],
        scratch_shapes=[pltpu.VMEM((s, width), F32),
                        pltpu.VMEM((HEADS_PER_STEP, HEAD_PAD, blk), F32),
                        pltpu.VMEM((HEADS_PER_STEP, HEAD_PAD, blk), F32),
                        pltpu.VMEM((HEADS_PER_STEP, nblk, HEAD_PAD, blk), BF16),
                        pltpu.VMEM((HEADS_PER_STEP, nblk, HEAD_PAD, blk), BF16)],
        compiler_params=_params(2),
    )(qb, qkv, qkv, do)


def _fox_in_bwd(dq, dk, dv, dg, wt, wft, dcum, f, x1, dx2, g1, ts):
    s = x1.shape[0]
    nt = s // ts

    def body(dq_ref, dk_ref, dv_ref, dg_ref, wt_ref, wft_ref, dcum_ref, f_ref, x1_ref, dx2_ref,
             g_ref, dx1_ref, dx1b_ref, df_ref, gn_ref, gbf_ref, rcar_ref):
        @pl.when(pl.program_id(0) == 0)
        def _():
            rcar_ref[...] = jnp.zeros_like(rcar_ref)
            gn_ref[...] = jnp.zeros_like(gn_ref)
            gbf_ref[...] = jnp.zeros_like(gbf_ref)

        dkv = dk_ref[...]
        rsum = _cumsum_rows(dcum_ref[...], reverse=True) + rcar_ref[0:1, :]
        df = rsum * _sigmoid(-f_ref[...])
        dfb = df.astype(BF16)
        dh = (_dot_nt(dq_ref[...], wt_ref[0]) + _dot_nt(dkv, wt_ref[1])
              + _dot_nt(dv_ref[...], wt_ref[2]) + _dot_nt(dg_ref[...], wt_ref[3])
              + _dot_nt(dfb, wft_ref[...]))
        dxn, dgn = _norm_bwd(x1_ref[...], g_ref[...], dh)
        dx1 = dx2_ref[...] + dxn
        dx1_ref[...] = dx1
        dx1b_ref[...] = dx1.astype(BF16)
        df_ref[...] = dfb
        gn_ref[...] += dgn
        gbf_ref[...] += jnp.sum(df, axis=0, keepdims=True)
        rcar_ref[...] = rsum[0:SUBLANES, :]

    rev = lambda i: (nt - 1 - i, 0)
    wide = pl.BlockSpec((ts, FOX_PAD), rev)
    return pl.pallas_call(
        body, name="fox_in_bwd", grid=(nt,),
        in_specs=[wide, wide, wide, wide,
                  _const_spec((4, D_MODEL, FOX_PAD)),
                  _const_spec((D_MODEL, LANES)),
                  pl.BlockSpec((ts, LANES), rev),
                  pl.BlockSpec((ts, LANES), rev),
                  pl.BlockSpec((ts, D_MODEL), rev),
                  pl.BlockSpec((ts, D_MODEL), rev),
                  _const_spec((1, D_MODEL))],
        out_specs=[pl.BlockSpec((ts, D_MODEL), rev),
                   pl.BlockSpec((ts, D_MODEL), rev),
                   pl.BlockSpec((ts, LANES), rev),
                   pl.BlockSpec((1, D_MODEL), lambda i: (0, 0)),
                   pl.BlockSpec((1, LANES), lambda i: (0, 0))],
        out_shape=[jax.ShapeDtypeStruct((s, D_MODEL), F32),
                   jax.ShapeDtypeStruct((s, D_MODEL), BF16),
                   jax.ShapeDtypeStruct((s, LANES), BF16),
                   jax.ShapeDtypeStruct((1, D_MODEL), F32),
                   jax.ShapeDtypeStruct((1, LANES), F32)],
        scratch_shapes=[pltpu.VMEM((SUBLANES, LANES), F32)],
        compiler_params=_params(),
    )(dq, dk, dv, dg, wt, wft, dcum, f, x1, dx2, g1)


def _lru_core_bwd(dx1b, w_out, xb, gate, hs, cw, cb, wa, ba, wx, bx, a_param, wa_t, wx_t, ts):
    s = xb.shape[0]
    nt = s // ts
    tpb = ts // SUBLANES

    def body(dx_ref, wo_ref, xb_ref, xbh_ref, gate_ref, hs_ref, hsh_ref, cw_ref, cb_ref, wa_ref,
             ba_ref, wx_ref, bx_ref, ap_ref, wat_ref, wxt_ref,
             du_ref, gwa_ref, gwx_ref, gvec_ref, acar_ref, dhcar_ref, dxccar_ref):
        step = pl.program_id(0)

        @pl.when(step == 0)
        def _():
            acar_ref[...] = jnp.zeros_like(acar_ref)
            dhcar_ref[...] = jnp.zeros_like(dhcar_ref)
            dxccar_ref[...] = jnp.zeros_like(dxccar_ref)
            gwa_ref[...] = jnp.zeros_like(gwa_ref)
            gwx_ref[...] = jnp.zeros_like(gwx_ref)
            gvec_ref[...] = jnp.zeros_like(gvec_ref)

        first_tile = step == nt - 1
        halo_on = jnp.where(first_tile, 0.0, 1.0)
        prev8 = xbh_ref[...] * halo_on
        hprev_row = hsh_ref[SUBLANES - 1:SUBLANES, :] * halo_on

        xbv = xb_ref[...]
        taps = _conv_taps(xbv, prev8)
        cw_v = cw_ref[...]
        xc, xcb, r, i, sp, a, mult = _lru_pre(taps, cw_v, cb_ref[...], wa_ref, ba_ref[...],
                                              wx_ref, bx_ref[...], ap_ref[...])
        hs = hs_ref[...]
        gv = gate_ref[...]
        sg = _sigmoid(gv)
        dy = _dot_nt(dx_ref[...], wo_ref[...])
        dhs = dy * (gv * sg)
        dgate = dy * hs * (sg * (1.0 + gv * (1.0 - sg)))

        rows = lax.broadcasted_iota(jnp.int32, a.shape, 0)
        a_next = jnp.where(rows < ts - 1, pltpu.roll(a, ts - 1, 0), acar_ref[0:1, :])
        cum_a, dh_loc = _scan_rows(a_next, dhs, reverse=True)
        dh = cum_a * dhcar_ref[0:1, :] + dh_loc
        h_prev = jnp.where(rows >= 1, pltpu.roll(hs, 1, 0), hprev_row)

        da = dh * h_prev
        ixc = i * xc
        dmult = dh * ixc
        di = dh * mult * xc
        dxc = dh * mult * i
        dlog_a = da * a - dmult * (a * a) / mult
        dr = dlog_a * ((-LRU_C) * sp)
        dsp = jnp.sum(dlog_a * ((-LRU_C) * r), axis=0, keepdims=True)
        dra = dr * r * (1.0 - r)
        dia = di * i * (1.0 - i)
        drab = dra.astype(BF16)
        diab = dia.astype(BF16)
        back = []
        for n in range(LRU_BLOCKS):
            sl = slice(n * LRU_BLOCK_W, (n + 1) * LRU_BLOCK_W)
            gwa_ref[n] += _dot_tn(xcb[:, sl], drab[:, sl])
            gwx_ref[n] += _dot_tn(xcb[:, sl], diab[:, sl])
            back.append(_dot(drab[:, sl], wat_ref[n]) + _dot(diab[:, sl], wxt_ref[n]))
        dxc = dxc + jnp.concatenate(back, axis=1)

        nxt8 = dxccar_ref[...]
        rows8 = lax.broadcasted_iota(jnp.int32, nxt8.shape, 0)
        dxb = cw_v[3:4] * dxc
        for j in range(1, CONV_WIDTH):
            rj = pltpu.roll(dxc, ts - j, 0)
            pj = pltpu.roll(nxt8, SUBLANES - j, 0)
            tail = jnp.where(rows8 >= SUBLANES - j, pj, rj[ts - SUBLANES:])
            dxb = dxb + cw_v[3 - j:4 - j] * jnp.concatenate([rj[:ts - SUBLANES], tail], axis=0)

        du_ref[:, :LRU_WIDTH] = dxb.astype(BF16)
        du_ref[:, LRU_WIDTH:] = dgate.astype(BF16)

        z = -ap_ref[...]
        gvec = [jnp.sum(dxc * taps[3 - k], axis=0, keepdims=True) for k in range(CONV_WIDTH)]
        gvec.append(jnp.sum(dxc, axis=0, keepdims=True))
        gvec.append(jnp.sum(dra, axis=0, keepdims=True))
        gvec.append(jnp.sum(dia, axis=0, keepdims=True))
        gvec.append(-dsp * _sigmoid(z))
        gvec_ref[...] += jnp.concatenate(gvec, axis=0)

        acar_ref[...] = a[0:SUBLANES, :]
        dhcar_ref[...] = dh[0:SUBLANES, :]
        dxccar_ref[...] = dxc[0:SUBLANES, :]

    rev = lambda i: (nt - 1 - i, 0)
    halo = lambda i: (jnp.maximum((nt - 1 - i) * tpb - 1, 0), 0)
    tile = pl.BlockSpec((ts, LRU_WIDTH), rev)
    halo_spec = pl.BlockSpec((SUBLANES, LRU_WIDTH), halo)
    vec = _const_spec((1, LRU_WIDTH))
    blk = _const_spec((LRU_BLOCKS, LRU_BLOCK_W, LRU_BLOCK_W))
    acc_blk = pl.BlockSpec((LRU_BLOCKS, LRU_BLOCK_W, LRU_BLOCK_W), lambda i: (0, 0, 0))
    return pl.pallas_call(
        body, name="lru_core_bwd", grid=(nt,),
        in_specs=[pl.BlockSpec((ts, D_MODEL), rev),
                  _const_spec((LRU_WIDTH, D_MODEL)),
                  tile, halo_spec, tile, tile, halo_spec,
                  _const_spec((CONV_WIDTH, LRU_WIDTH)), vec, blk, vec, blk, vec, vec, blk, blk],
        out_specs=[pl.BlockSpec((ts, 2 * LRU_WIDTH), rev), acc_blk, acc_blk,
                   pl.BlockSpec((SUBLANES, LRU_WIDTH), lambda i: (0, 0))],
        out_shape=[jax.ShapeDtypeStruct((s, 2 * LRU_WIDTH), BF16),
                   jax.ShapeDtypeStruct((LRU_BLOCKS, LRU_BLOCK_W, LRU_BLOCK_W), F32),
                   jax.ShapeDtypeStruct((LRU_BLOCKS, LRU_BLOCK_W, LRU_BLOCK_W), F32),
                   jax.ShapeDtypeStruct((SUBLANES, LRU_WIDTH), F32)],
        scratch_shapes=[pltpu.VMEM((SUBLANES, LRU_WIDTH), F32),
                        pltpu.VMEM((SUBLANES, LRU_WIDTH), F32),
                        pltpu.VMEM((SUBLANES, LRU_WIDTH), F32)],
        compiler_params=_params(),
    )(dx1b, w_out, xb, xb, gate, hs, hs, cw, cb, wa, ba, wx, bx, a_param, wa_t, wx_t)


def _lru_in_bwd(du, w_in, x, dx1, g0, ts):
    s = x.shape[0]

    def body(du_ref, w_ref, x_ref, dx1_ref, g_ref, gx_ref, gn_ref):
        @pl.when(pl.program_id(0) == 0)
        def _():
            gn_ref[...] = jnp.zeros_like(gn_ref)

        duv = du_ref[...]
        dh = _dot_nt(duv[:, 0:LRU_IN_SHARD], w_ref[0])
        for j in range(1, N_DEV):
            dh = dh + _dot_nt(duv[:, j * LRU_IN_SHARD:(j + 1) * LRU_IN_SHARD], w_ref[j])
        dxn, dgn = _norm_bwd(x_ref[...], g_ref[...], dh)
        gx_ref[...] = dx1_ref[...] + dxn
        gn_ref[...] += dgn

    tile = pl.BlockSpec((ts, D_MODEL), lambda i: (i, 0))
    return pl.pallas_call(
        body, name="lru_in_bwd", grid=(s // ts,),
        in_specs=[pl.BlockSpec((ts, 2 * LRU_WIDTH), lambda i: (i, 0)),
                  _const_spec((N_DEV, D_MODEL, LRU_IN_SHARD)), tile, tile,
                  _const_spec((1, D_MODEL))],
        out_specs=[tile, pl.BlockSpec((1, D_MODEL), lambda i: (0, 0))],
        out_shape=[jax.ShapeDtypeStruct((s, D_MODEL), F32),
                   jax.ShapeDtypeStruct((1, D_MODEL), F32)],
        compiler_params=_params(),
    )(du, w_in, x, dx1, g0)


def _weight_grad(a, b, ts, name, scale=1.0, col_shards=1):
    s, ka = a.shape
    nb = b.shape[1]
    nt = s // ts
    per = nb // col_shards

    def body(a_ref, b_ref, o_ref):
        @pl.when(pl.program_id(0) == 0)
        def _():
            o_ref[...] = jnp.zeros_like(o_ref)

        if col_shards == 1:
            o_ref[...] += _dot_tn(a_ref[...], b_ref[...])
        else:
            av, bv = a_ref[...], b_ref[...]
            for j in range(col_shards):
                o_ref[j] += _dot_tn(av, bv[:, j * per:(j + 1) * per])
        if scale != 1.0:
            @pl.when(pl.program_id(0) == nt - 1)
            def _():
                o_ref[...] = o_ref[...] * scale

    out_dims = (ka, nb) if col_shards == 1 else (col_shards, ka, per)
    return pl.pallas_call(
        body, name=name, grid=(nt,),
        in_specs=[pl.BlockSpec((ts, ka), lambda i: (i, 0)),
                  pl.BlockSpec((ts, nb), lambda i: (i, 0))],
        out_specs=pl.BlockSpec(out_dims, lambda i: (0,) * len(out_dims)),
        out_shape=jax.ShapeDtypeStruct(out_dims, F32),
        compiler_params=_params(),
    )(a, b)


def _sum_parts(gp_ref):
    g = gp_ref[0].astype(F32)
    for k in range(1, gp_ref.shape[0]):
        g = g + gp_ref[k].astype(F32)
    return g


def _adamw(g_parts, w, m, v, tr, name):
    nparts, rows, cols = g_parts.shape

    def body(gp_ref, w_ref, m_ref, v_ref, g_ref, d_ref, mo_ref, vo_ref):
        g = _sum_parts(gp_ref)
        m2 = ADAM_B1 * m_ref[...] + (1.0 - ADAM_B1) * g
        v2 = ADAM_B2 * v_ref[...] + (1.0 - ADAM_B2) * (g * g)
        m_hat = m2 / (1.0 - ADAM_B1 ** ADAM_STEP)
        v_hat = v2 / (1.0 - ADAM_B2 ** ADAM_STEP)
        g_ref[...] = g
        d_ref[...] = (-ADAM_LR) * (m_hat / (jnp.sqrt(v_hat) + ADAM_EPS) + ADAM_WD * w_ref[...])
        mo_ref[...] = m2
        vo_ref[...] = v2

    tile = pl.BlockSpec((tr, cols), lambda i: (i, 0))
    out = jax.ShapeDtypeStruct((rows, cols), F32)
    return pl.pallas_call(
        body, name=name, grid=(rows // tr,),
        in_specs=[pl.BlockSpec((nparts, tr, cols), lambda i: (0, i, 0)), tile, tile, tile],
        out_specs=[tile, tile, tile, tile],
        out_shape=[out, out, out, out],
        compiler_params=_params(),
    )(g_parts, w, m, v)


def _reduce_parts(g_parts, name):
    _, rows, cols = g_parts.shape

    def body(gp_ref, g_ref):
        g_ref[...] = _sum_parts(gp_ref)

    return pl.pallas_call(
        body, name=name,
        out_shape=jax.ShapeDtypeStruct((rows, cols), F32),
        compiler_params=pltpu.CompilerParams(vmem_limit_bytes=VMEM_LIMIT_BYTES),
    )(g_parts)


def _mesh_pos():
    ix, iy, ic = lax.axis_index("x"), lax.axis_index("y"), lax.axis_index("c")
    return ix, iy, ic


def _peer(ix, iy, ic, mask):
    px = 1 - ix if mask & 4 else ix
    py = 1 - iy if mask & 2 else iy
    pc = 1 - ic if mask & 1 else ic
    return (px, py, pc), 4 * px + 2 * py + pc


def _exchange(arrays, scatter, name):
    n = len(arrays)

    def body(*refs):
        x_refs, o_refs = refs[:n], refs[n:2 * n]
        send_sems, recv_sems, local_sems = refs[2 * n:]
        ix, iy, ic = _mesh_pos()
        me = 4 * ix + 2 * iy + ic

        def src(a, dest):
            return x_refs[a].at[dest] if scatter else x_refs[a]

        local = [pltpu.make_async_copy(src(a, me), o_refs[a].at[me], local_sems.at[a])
                 for a in range(n)]
        for cp in local:
            cp.start()
        sends = []
        for mask in range(1, N_DEV):
            peer, pidx = _peer(ix, iy, ic, mask)
            for a in range(n):
                cp = pltpu.make_async_remote_copy(
                    src_ref=src(a, pidx), dst_ref=o_refs[a].at[me],
                    send_sem=send_sems.at[a, mask - 1], recv_sem=recv_sems.at[a, mask - 1],
                    device_id=peer, device_id_type=pl.DeviceIdType.MESH)
                cp.start()
                sends.append(cp)
        for mask in range(1, N_DEV):
            peer, pidx = _peer(ix, iy, ic, mask)
            for a in range(n):
                pltpu.make_async_remote_copy(
                    src_ref=src(a, me), dst_ref=o_refs[a].at[pidx],
                    send_sem=send_sems.at[a, mask - 1], recv_sem=recv_sems.at[a, mask - 1],
                    device_id=peer, device_id_type=pl.DeviceIdType.MESH).wait_recv()
        for cp in sends:
            cp.wait_send()
        for cp in local:
            cp.wait()

    out_shape = [jax.ShapeDtypeStruct(x.shape if scatter else (N_DEV,) + x.shape, x.dtype)
                 for x in arrays]
    return pl.pallas_call(
        body, name=name,
        in_specs=[pl.BlockSpec(memory_space=pl.ANY)] * n,
        out_specs=[pl.BlockSpec(memory_space=pl.ANY)] * n,
        out_shape=out_shape,
        scratch_shapes=[pltpu.SemaphoreType.DMA((n, N_DEV - 1)),
                        pltpu.SemaphoreType.DMA((n, N_DEV - 1)),
                        pltpu.SemaphoreType.DMA((n,))],
    )(*arrays)


def _gather_two_level(arrays, name):
    n = len(arrays)

    def body(*refs):
        x_refs, o_refs = refs[:n], refs[n:2 * n]
        send_sems, recv_sems, local_sems = refs[2 * n:]
        ix, iy, ic = _mesh_pos()
        me, sibling = (ix, iy, ic), (ix, iy, 1 - ic)
        chips = [(1 - ix, iy), (ix, 1 - iy), (1 - ix, 1 - iy)]

        def idx(px, py, pc):
            return 4 * px + 2 * py + pc

        def copy(a, k, block, to, src=None):
            dst = o_refs[a].at[idx(*block)]
            return pltpu.make_async_remote_copy(
                src_ref=dst if src is None else src, dst_ref=dst,
                send_sem=send_sems.at[a, k], recv_sem=recv_sems.at[a, k],
                device_id=to, device_id_type=pl.DeviceIdType.MESH)

        local = [pltpu.make_async_copy(x_refs[a], o_refs[a].at[idx(*me)], local_sems.at[a])
                 for a in range(n)]
        for cp in local:
            cp.start()
        first = []
        for a in range(n):
            first.append(copy(a, 0, me, sibling, src=x_refs[a]))
            first += [copy(a, 1 + j, me, (*chip, ic), src=x_refs[a])
                      for j, chip in enumerate(chips)]
        for cp in first:
            cp.start()
        passed = []
        for j, chip in enumerate(chips):
            for a in range(n):
                copy(a, 1 + j, (*chip, ic), me).wait_recv()
                cp = copy(a, 4 + j, (*chip, ic), sibling)
                cp.start()
                passed.append(cp)
        for a in range(n):
            copy(a, 0, sibling, me).wait_recv()
            for j, chip in enumerate(chips):
                copy(a, 4 + j, (*chip, 1 - ic), me).wait_recv()
        for cp in first + passed:
            cp.wait_send()
        for cp in local:
            cp.wait()

    return pl.pallas_call(
        body, name=name,
        in_specs=[pl.BlockSpec(memory_space=pl.ANY)] * n,
        out_specs=[pl.BlockSpec(memory_space=pl.ANY)] * n,
        out_shape=[jax.ShapeDtypeStruct((N_DEV,) + x.shape, x.dtype) for x in arrays],
        scratch_shapes=[pltpu.SemaphoreType.DMA((n, N_DEV - 1)),
                        pltpu.SemaphoreType.DMA((n, N_DEV - 1)),
                        pltpu.SemaphoreType.DMA((n,))],
    )(*arrays)


def _swap_sibling(arrays, name):
    n = len(arrays)
    n_chips = N_DEV // 2

    def body(*refs):
        x_refs, got_refs = refs[:n], refs[n:2 * n]
        send_sems, recv_sems = refs[2 * n:]
        ix, iy, ic = _mesh_pos()
        sibling = (ix, iy, 1 - ic)
        sends = []
        for a in range(n):
            for q in range(n_chips):
                cp = pltpu.make_async_remote_copy(
                    src_ref=x_refs[a].at[q, 1 - ic], dst_ref=got_refs[a].at[q],
                    send_sem=send_sems.at[a, q], recv_sem=recv_sems.at[a, q],
                    device_id=sibling, device_id_type=pl.DeviceIdType.MESH)
                cp.start()
                sends.append(cp)
        for cp in sends:
            cp.wait()

    return pl.pallas_call(
        body, name=name,
        in_specs=[pl.BlockSpec(memory_space=pl.ANY)] * n,
        out_specs=[pl.BlockSpec(memory_space=pl.ANY)] * n,
        out_shape=[jax.ShapeDtypeStruct((n_chips,) + x.shape[2:], x.dtype) for x in arrays],
        scratch_shapes=[pltpu.SemaphoreType.DMA((n, n_chips)),
                        pltpu.SemaphoreType.DMA((n, n_chips))],
    )(*arrays)


def _exchange_chips(arrays, name):
    n = len(arrays)
    n_chips = N_DEV // 2

    def body(*refs):
        x_refs, o_refs = refs[:n], refs[n:2 * n]
        send_sems, recv_sems, local_sems = refs[2 * n:]
        ix, iy, ic = _mesh_pos()
        my_chip = 2 * ix + iy
        local = [pltpu.make_async_copy(x_refs[a].at[my_chip], o_refs[a].at[my_chip],
                                       local_sems.at[a]) for a in range(n)]
        for cp in local:
            cp.start()
        sends = []
        for mask in range(1, n_chips):
            px = 1 - ix if mask & 2 else ix
            py = 1 - iy if mask & 1 else iy
            for a in range(n):
                cp = pltpu.make_async_remote_copy(
                    src_ref=x_refs[a].at[2 * px + py], dst_ref=o_refs[a].at[my_chip],
                    send_sem=send_sems.at[a, mask - 1], recv_sem=recv_sems.at[a, mask - 1],
                    device_id=(px, py, ic), device_id_type=pl.DeviceIdType.MESH)
                cp.start()
                sends.append(cp)
        for mask in range(1, n_chips):
            px = 1 - ix if mask & 2 else ix
            py = 1 - iy if mask & 1 else iy
            for a in range(n):
                pltpu.make_async_remote_copy(
                    src_ref=x_refs[a].at[my_chip], dst_ref=o_refs[a].at[2 * px + py],
                    send_sem=send_sems.at[a, mask - 1], recv_sem=recv_sems.at[a, mask - 1],
                    device_id=(px, py, ic), device_id_type=pl.DeviceIdType.MESH).wait_recv()
        for cp in sends:
            cp.wait_send()
        for cp in local:
            cp.wait()

    return pl.pallas_call(
        body, name=name,
        in_specs=[pl.BlockSpec(memory_space=pl.ANY)] * n,
        out_specs=[pl.BlockSpec(memory_space=pl.ANY)] * n,
        out_shape=[jax.ShapeDtypeStruct(x.shape, x.dtype) for x in arrays],
        scratch_shapes=[pltpu.SemaphoreType.DMA((n, n_chips - 1)),
                        pltpu.SemaphoreType.DMA((n, n_chips - 1)),
                        pltpu.SemaphoreType.DMA((n,))],
    )(*arrays)


def _pair_sum(core, x, got, name):
    nq, rows, cols = got.shape

    def body(c_ref, x_ref, g_ref, o_ref):
        o_ref[...] = (x_ref[...] + g_ref[...]).astype(BF16)

    blk = pl.BlockSpec((None, rows, cols), lambda q, c: (q, 0, 0))
    return pl.pallas_call(
        body, name=name,
        grid_spec=pltpu.PrefetchScalarGridSpec(
            num_scalar_prefetch=1, grid=(nq,),
            in_specs=[pl.BlockSpec((None, None, rows, cols), lambda q, c: (q, c[0], 0, 0)), blk],
            out_specs=blk),
        out_shape=jax.ShapeDtypeStruct(got.shape, BF16),
        compiler_params=_params(),
    )(core, x, got)


def _pad_heads_cols(w):
    k = w.shape[0]
    w = w.reshape(k, HEADS, HEAD_DIM)
    return jnp.pad(w, ((0, 0), (0, 0), (0, HEAD_PAD - HEAD_DIM))).reshape(k, FOX_PAD)


def _unpad_heads_cols(w):
    k = w.shape[0]
    return w.reshape(k, HEADS, HEAD_PAD)[:, :, :HEAD_DIM].reshape(k, HEADS * HEAD_DIM)


def _selectors():
    r = lax.broadcasted_iota(jnp.int32, (3 * LANES, FOX_PAD), 0)
    c = lax.broadcasted_iota(jnp.int32, (3 * LANES, FOX_PAD), 1)
    part, head_r = r // LANES, r % LANES
    head_c, lane_c = c // HEAD_PAD, c % HEAD_PAD
    same = (head_r == head_c) & (head_r < HEADS)
    sel_q = jnp.where(same & (lane_c == LANE_RB + part), 1.0, 0.0)
    sel_k = jnp.where(same & (lane_c == LANE_CK + part), -1.0, 0.0)
    sel = jnp.stack([sel_q, sel_k, jnp.zeros_like(sel_q)]).astype(BF16)
    lane = lax.broadcasted_iota(jnp.int32, (1, FOX_PAD), 1) % HEAD_PAD
    ones_q = jnp.where((lane >= LANE_CK) & (lane < LANE_CK + 3), 1.0, 0.0)
    ones_k = jnp.where(((lane >= LANE_RB) & (lane < LANE_RB + 3))
                       | ((lane >= LANE_LSE) & (lane < LANE_LSE + 3)), 1.0, 0.0)
    ones_v = jnp.where((lane >= LANE_ONE_V) & (lane < LANE_ONE_V + 2), 1.0, 0.0)
    bias = jnp.stack([ones_q, ones_k, ones_v]).astype(F32)
    return sel, bias


def _local_step(x, target, norm_g, final_g, w_in8, conv_w, conv_b, wa, ba, wx, bx, a_param,
                w_out_b, fox_in8, b_f, fox_out_b, blk=512, ts=256):
    qk_scale = 1.0 / (HEAD_DIM ** 0.5)
    g0, g1 = norm_g[0:1], norm_g[1:2]
    gf = final_g.reshape(1, D_MODEL)
    wa_b, wx_b = wa.astype(BF16), wx.astype(BF16)
    fox_w_in = jnp.transpose(fox_in8, (1, 0, 2)).reshape(D_MODEL, FOX_IN_COLS)
    wq = _pad_heads_cols(fox_w_in[:, 0:1024]) * qk_scale
    wk = _pad_heads_cols(fox_w_in[:, 1024:2048])
    wv = _pad_heads_cols(fox_w_in[:, 2048:3072])
    wg = _pad_heads_cols(fox_w_in[:, 3072:4096])
    wf_b = jnp.pad(fox_w_in[:, 4096:], ((0, 0), (0, LANES - HEADS)))
    w4 = jnp.stack([wq, wk, wv, wg]).astype(BF16)
    bf_pad = jnp.pad(b_f, ((0, 0), (0, LANES - HEADS)))
    fo = fox_out_b.reshape(HEADS, HEAD_DIM, D_MODEL)
    fo_b = jnp.pad(fo, ((0, 0), (0, HEAD_PAD - HEAD_DIM), (0, 0))).reshape(FOX_PAD, D_MODEL)
    sel, bias = _selectors()

    xb, gate1, h0 = _lru_in_fwd(x, g0, w_in8, ts)
    y1, hs = _lru_core_fwd(xb, gate1, conv_w, conv_b, wa_b, ba, wx_b, bx, a_param, ts)
    x1, h1, f, cparts = _fox_pre_fwd(x, y1, w_out_b, g1, wf_b, bf_pad, ts)
    qkv = _fox_proj_fwd(h1, cparts, w4[0:3], sel, bias, BF16, ts, "fox_proj_qkv")
    gate2 = _fox_proj_fwd(h1, None, w4[3:4], None, None, F32, ts, "fox_proj_gate")[0]
    o, qb = _attn_fwd(qkv, blk)
    dx2, y2, loss_acc, g_final = _fox_out_loss(o, gate2, fo_b, x1, target, gf, ts)

    do, dgate2 = _fox_out_bwd(dx2, fo_b, o, gate2, ts)
    dq, dk, dv, dcum = _attn_bwd(qb, qkv, do, blk)
    dx1, dx1b, df, g_norm1, g_bf = _fox_in_bwd(dq, dk, dv, dgate2, w4, wf_b, dcum, f, x1, dx2,
                                               g1, ts)
    du, g_wa, g_wx, g_vec = _lru_core_bwd(dx1b, w_out_b, xb, gate1, hs, conv_w, conv_b, wa_b, ba,
                                          wx_b, bx, a_param, jnp.transpose(wa_b, (0, 2, 1)),
                                          jnp.transpose(wx_b, (0, 2, 1)), ts)
    grad_x, g_norm0 = _lru_in_bwd(du, w_in8, x, dx1, g0, ts)

    tw = 512
    g_lru_w_in = _weight_grad(h0, du, tw, "grad_lru_w_in", col_shards=N_DEV)
    g_lru_w_out = _weight_grad(y1, dx1b, tw, "grad_lru_w_out")
    g_q = _weight_grad(h1, dq, tw, "grad_fox_wq", scale=qk_scale)
    g_k = _weight_grad(h1, dk, tw, "grad_fox_wk")
    g_v = _weight_grad(h1, dv, tw, "grad_fox_wv")
    g_g = _weight_grad(h1, dgate2, tw, "grad_fox_wg")
    g_f = _weight_grad(h1, df, tw, "grad_fox_wf")
    g_fox_w_in = jnp.concatenate(
        [_unpad_heads_cols(g_q), _unpad_heads_cols(g_k), _unpad_heads_cols(g_v),
         _unpad_heads_cols(g_g), g_f[:, :HEADS]], axis=1)
    g_fox_w_in = jnp.transpose(g_fox_w_in.reshape(D_MODEL, N_DEV, FOX_IN_SHARD), (1, 0, 2))
    g_fo = _weight_grad(y2, dx2.astype(BF16), tw, "grad_fox_w_out")
    g_fox_w_out = g_fo.reshape(HEADS, HEAD_PAD, D_MODEL)[:, :HEAD_DIM].reshape(
        HEADS * HEAD_DIM, D_MODEL)

    grads = dict(
        norm_g=jnp.concatenate([g_norm0, g_norm1], axis=0), final_g=g_final[0],
        lru_w_in=g_lru_w_in, lru_conv_w=g_vec[0:4], lru_conv_b=g_vec[4:5], lru_wa=g_wa,
        lru_ba=g_vec[5:6], lru_wx=g_wx, lru_bx=g_vec[6:7], lru_a_param=g_vec[7:8],
        lru_w_out=g_lru_w_out, fox_w_in=g_fox_w_in, fox_b_f=g_bf[:, :HEADS],
        fox_w_out=g_fox_w_out)
    return loss_acc[0, 0], grad_x, grads


SMALL =("norm_g", "final_g", "lru_conv_b", "lru_wa", "lru_ba", "lru_wx", "lru_bx", "lru_a_param",
         "fox_b_f")
ALL_WEIGHTS = ("norm_g", "final_g", "lru_w_in", "lru_conv_w", "lru_conv_b", "lru_wa", "lru_ba",
               "lru_wx", "lru_bx", "lru_a_param", "lru_w_out", "fox_w_in", "fox_b_f", "fox_w_out")


def _pack_small(d):
    rows = []
    for n in SMALL:
        a = d[n].reshape(-1)
        if a.shape[0] % LANES:
            a = jnp.pad(a, (0, LANES - a.shape[0] % LANES))
        rows.append(a.reshape(-1, LANES))
    packed = jnp.concatenate(rows, axis=0)
    return jnp.pad(packed, ((0, N_DEV * SMALL_CHUNK_ROWS - packed.shape[0]), (0, 0)))


def _unpack_small(packed, like):
    out, off = {}, 0
    for n, nrows in zip(SMALL, SMALL_ROWS):
        size = like[n].size
        out[n] = packed[off:off + nrows].reshape(-1)[:size].reshape(like[n].shape)
        off += nrows
    return out


def kernel(x, norm_g, final_g, lru_w_in, lru_conv_w, lru_conv_b, lru_wa, lru_ba, lru_wx, lru_bx, lru_a_param, lru_w_out, fox_w_in, fox_b_f, fox_w_out, loss_target, m_norm_g, m_final_g, m_lru_w_in, m_lru_conv_w, m_lru_conv_b, m_lru_wa, m_lru_ba, m_lru_wx, m_lru_bx, m_lru_a_param, m_lru_w_out, m_fox_w_in, m_fox_b_f, m_fox_w_out, v_norm_g, v_final_g, v_lru_w_in, v_lru_conv_w, v_lru_conv_b, v_lru_wa, v_lru_ba, v_lru_wx, v_lru_bx, v_lru_a_param, v_lru_w_out, v_fox_w_in, v_fox_b_f, v_fox_w_out):
    w_loc = dict(norm_g=norm_g, final_g=final_g, lru_w_in=lru_w_in, lru_conv_w=lru_conv_w,
                 lru_conv_b=lru_conv_b, lru_wa=lru_wa, lru_ba=lru_ba, lru_wx=lru_wx, lru_bx=lru_bx,
                 lru_a_param=lru_a_param, lru_w_out=lru_w_out, fox_w_in=fox_w_in, fox_b_f=fox_b_f,
                 fox_w_out=fox_w_out)
    m_loc = dict(norm_g=m_norm_g, final_g=m_final_g, lru_w_in=m_lru_w_in, lru_conv_w=m_lru_conv_w,
                 lru_conv_b=m_lru_conv_b, lru_wa=m_lru_wa, lru_ba=m_lru_ba, lru_wx=m_lru_wx,
                 lru_bx=m_lru_bx, lru_a_param=m_lru_a_param, lru_w_out=m_lru_w_out,
                 fox_w_in=m_fox_w_in, fox_b_f=m_fox_b_f, fox_w_out=m_fox_w_out)
    v_loc = dict(norm_g=v_norm_g, final_g=v_final_g, lru_w_in=v_lru_w_in, lru_conv_w=v_lru_conv_w,
                 lru_conv_b=v_lru_conv_b, lru_wa=v_lru_wa, lru_ba=v_lru_ba, lru_wx=v_lru_wx,
                 lru_bx=v_lru_bx, lru_a_param=v_lru_a_param, lru_w_out=v_lru_w_out,
                 fox_w_in=v_fox_w_in, fox_b_f=v_fox_b_f, fox_w_out=v_fox_w_out)

    w_in8, conv8, w_out8, fox_in8, fox_out8 = _gather_two_level(
        [lru_w_in[0].astype(BF16), lru_conv_w[0], lru_w_out[0].astype(BF16),
         fox_w_in[0].astype(BF16), fox_w_out[0].astype(BF16)], "gather_weights")
    conv_full = jnp.transpose(conv8, (1, 0, 2)).reshape(CONV_WIDTH, LRU_WIDTH)

    loss, grad_x, grads = _local_step(
        x[0], loss_target[0], norm_g, final_g, w_in8, conv_full, lru_conv_b, lru_wa[0], lru_ba,
        lru_wx[0], lru_bx, lru_a_param, w_out8.reshape(LRU_WIDTH, D_MODEL), fox_in8, fox_b_f,
        fox_out8.reshape(HEADS * HEAD_DIM, D_MODEL))

    n_chips = N_DEV // 2
    conv_send = jnp.transpose(grads["lru_conv_w"].reshape(CONV_WIDTH, N_DEV, -1), (1, 0, 2))
    names = ("lru_w_in", "lru_conv_w", "lru_w_out", "fox_w_in", "fox_w_out", "small")
    send = [grads["lru_w_in"], conv_send, grads["lru_w_out"].reshape(N_DEV, -1, D_MODEL),
            grads["fox_w_in"], grads["fox_w_out"].reshape(N_DEV, -1, D_MODEL),
            _pack_small(grads).reshape(N_DEV, SMALL_CHUNK_ROWS, LANES)]
    send = [a.reshape((n_chips, 2) + a.shape[1:]) for a in send]
    got = _swap_sibling(send, "swap_grads")
    core = lax.axis_index("c").astype(jnp.int32).reshape(1)
    chip_sums = [_pair_sum(core, a, b, "pair_sum_" + n) for n, a, b in zip(names, send, got)]
    r_w_in, r_conv, r_w_out, r_fox_in, r_fox_out, r_small = _exchange_chips(
        chip_sums, "scatter_grads")

    out = {}
    for n, recv, tr in (("lru_w_in", r_w_in, 256), ("lru_conv_w", r_conv, CONV_WIDTH),
                        ("lru_w_out", r_w_out, 96), ("fox_w_in", r_fox_in, 128),
                        ("fox_w_out", r_fox_out, 64)):
        res = _adamw(recv, w_loc[n][0], m_loc[n][0], v_loc[n][0], tr, "adamw_" + n)
        out[n] = [a[None] for a in res]

    g_chunk = _reduce_parts(r_small, "reduce_small_grads")
    g_small, = _exchange([g_chunk], False, "gather_small_grads")
    g_small = g_small.reshape(1, N_DEV * SMALL_CHUNK_ROWS, LANES)
    res = _adamw(g_small, _pack_small(w_loc), _pack_small(m_loc), _pack_small(v_loc),
                 N_DEV * SMALL_CHUNK_ROWS, "adamw_replicated")
    small_out = [_unpack_small(a, w_loc) for a in res]
    for n in SMALL:
        out[n] = [d[n] for d in small_out]

    loss = lax.psum(loss, ("x", "y", "c"))
    return (loss, grad_x[None], *[out[n][0] for n in ALL_WEIGHTS], *[out[n][1] for n in ALL_WEIGHTS],
            *[out[n][2] for n in ALL_WEIGHTS], *[out[n][3] for n in ALL_WEIGHTS])
```

```python
import functools

import jax
import jax.numpy as jnp
from jax import lax
from jax.experimental import pallas as pl
from jax.experimental.pallas import tpu as pltpu

F32 = jnp.float32
BF16 = jnp.bfloat16

D_MODEL = 1024
LRU_WIDTH = 1536
LRU_BLOCKS = 12
LRU_BLOCK_W = 128
CONV_WIDTH = 4
LRU_C = 8.0
HEADS = 16
HEAD_DIM = 64
HEAD_PAD = 128
FOX_PAD = HEADS * HEAD_PAD
HEADS_PER_STEP = 2
EPS = 1e-6
NEG_BIG = -1e30
N_DEV = 8

ADAM_LR = 0.001
ADAM_B1 = 0.9
ADAM_B2 = 0.999
ADAM_EPS = 1e-08
ADAM_WD = 0.01
ADAM_STEP = 10

LANE_RB = 64
LANE_CK = 67
LANE_LSE = 70
LANE_ONE_V = 64

VMEM_LIMIT_BYTES = 56 * 1024 * 1024
LANES = 128
SUBLANES = 8

LRU_IN_SHARD = 2 * LRU_WIDTH // N_DEV
FOX_IN_COLS = 4 * HEADS * HEAD_DIM + HEADS
FOX_IN_SHARD = FOX_IN_COLS // N_DEV

SMALL_ROWS = (16, 8, 12, 1536, 12, 1536, 12, 12, 1)
SMALL_CHUNK_ROWS = 400
assert sum(SMALL_ROWS) <= N_DEV * SMALL_CHUNK_ROWS


def _params(n_grid_axes=1):
    return pltpu.CompilerParams(
        dimension_semantics=("arbitrary",) * n_grid_axes,
        vmem_limit_bytes=VMEM_LIMIT_BYTES)


def _const_spec(shape):
    nd = len(shape)
    return pl.BlockSpec(shape, lambda *_: (0,) * nd, pipeline_mode=pl.Buffered(1))


def _shift_down(x, k, fill):
    rows = lax.broadcasted_iota(jnp.int32, x.shape, 0)
    return jnp.where(rows >= k, pltpu.roll(x, k, 0), fill)


def _shift_up(x, k, fill):
    n = x.shape[0]
    rows = lax.broadcasted_iota(jnp.int32, x.shape, 0)
    return jnp.where(rows < n - k, pltpu.roll(x, n - k, 0), fill)


def _scan_rows(a, b, reverse=False):
    n = a.shape[0]
    shift = _shift_up if reverse else _shift_down
    k = 1
    while k < n:
        b = a * shift(b, k, 0.0) + b
        a = a * shift(a, k, 1.0)
        k *= 2
    return a, b


def _cumsum_rows(x, reverse=False):
    n = x.shape[0]
    shift = _shift_up if reverse else _shift_down
    k = 1
    while k < n:
        x = x + shift(x, k, 0.0)
        k *= 2
    return x


def _rstd(x):
    return lax.rsqrt(jnp.mean(x * x, axis=-1, keepdims=True) + EPS)


def _norm_bwd(x, g, dh):
    rstd = _rstd(x)
    xhat = x * rstd
    dg = jnp.sum(dh * xhat, axis=0, keepdims=True)
    dxh = dh * g
    dx = rstd * (dxh - xhat * jnp.mean(dxh * xhat, axis=-1, keepdims=True))
    return dx, dg


def _split3(x):
    hi = x.astype(BF16)
    r1 = x - hi.astype(F32)
    mid = r1.astype(BF16)
    lo = (r1 - mid.astype(F32)).astype(BF16)
    return hi, mid, lo


def _sigmoid(x):
    return jax.nn.sigmoid(x)


def _dot(a, b):
    return jnp.dot(a, b, preferred_element_type=F32)


def _dot_nt(a, b):
    return lax.dot_general(a, b, (((1,), (1,)), ((), ())), preferred_element_type=F32)


def _dot_tn(a, b):
    return lax.dot_general(a, b, (((0,), (0,)), ((), ())), preferred_element_type=F32)


def _heads_to_padded(u):
    n = u.shape[0]
    low = lax.broadcasted_iota(jnp.int32, (n, LANES), 1) < HEAD_DIM
    zero = jnp.zeros((n, LANES), u.dtype)
    cols = []
    for p in range(HEADS // 2):
        pair = u[:, p * LANES:(p + 1) * LANES]
        cols.append(jnp.where(low, pair, zero))
        cols.append(jnp.where(low, pltpu.roll(pair, HEAD_DIM, 1), zero))
    return jnp.concatenate(cols, axis=1)


def _heads_from_padded(x):
    n = x.shape[0]
    low = lax.broadcasted_iota(jnp.int32, (n, LANES), 1) < HEAD_DIM
    cols = []
    for p in range(HEADS // 2):
        even = x[:, (2 * p) * HEAD_PAD:(2 * p + 1) * HEAD_PAD]
        odd = x[:, (2 * p + 1) * HEAD_PAD:(2 * p + 2) * HEAD_PAD]
        cols.append(jnp.where(low, even, pltpu.roll(odd, HEAD_DIM, 1)))
    return jnp.concatenate(cols, axis=1)


def _conv_taps(xb, prev8):
    rows8 = lax.broadcasted_iota(jnp.int32, prev8.shape, 0)
    taps = [xb]
    for j in range(1, CONV_WIDTH):
        r = pltpu.roll(xb, j, 0)
        p = pltpu.roll(prev8, j, 0)
        head = jnp.where(rows8 < j, p, r[0:SUBLANES])
        taps.append(jnp.concatenate([head, r[SUBLANES:]], axis=0))
    return taps


def _lru_pre(taps, cw, cb, wa_ref, ba, wx_ref, bx, a_param):
    xc = cb + cw[3:4] * taps[0] + cw[2:3] * taps[1] + cw[1:2] * taps[2] + cw[0:1] * taps[3]
    xcb = xc.astype(BF16)
    ra, ia = [], []
    for n in range(LRU_BLOCKS):
        blk = xcb[:, n * LRU_BLOCK_W:(n + 1) * LRU_BLOCK_W]
        ra.append(_dot(blk, wa_ref[n]))
        ia.append(_dot(blk, wx_ref[n]))
    r = _sigmoid(jnp.concatenate(ra, axis=1) + ba)
    i = _sigmoid(jnp.concatenate(ia, axis=1) + bx)
    z = -a_param
    sp = jnp.maximum(z, 0.0) + jnp.log1p(jnp.exp(-jnp.abs(z)))
    log_a = (-LRU_C) * r * sp
    a = jnp.exp(log_a)
    one_minus_a2 = -jnp.tanh(log_a) * (a * a + 1.0)
    mult = jnp.sqrt(one_minus_a2)
    return xc, xcb, r, i, sp, a, mult


def _lru_in_fwd(x, g0, w_in, ts):
    s = x.shape[0]
    half = N_DEV // 2

    def body(x_ref, g_ref, w_ref, xb_ref, gate_ref, h_ref):
        xv = x_ref[...]
        h = (xv * _rstd(xv) * g_ref[...]).astype(BF16)
        u = [_dot(h, w_ref[j]) for j in range(N_DEV)]
        xb_ref[...] = jnp.concatenate(u[:half], axis=1)
        gate_ref[...] = jnp.concatenate(u[half:], axis=1)
        h_ref[...] = h

    return pl.pallas_call(
        body, name="lru_in_fwd", grid=(s // ts,),
        in_specs=[pl.BlockSpec((ts, D_MODEL), lambda i: (i, 0)),
                  _const_spec((1, D_MODEL)),
                  _const_spec((N_DEV, D_MODEL, LRU_IN_SHARD))],
        out_specs=[pl.BlockSpec((ts, LRU_WIDTH), lambda i: (i, 0)),
                   pl.BlockSpec((ts, LRU_WIDTH), lambda i: (i, 0)),
                   pl.BlockSpec((ts, D_MODEL), lambda i: (i, 0))],
        out_shape=[jax.ShapeDtypeStruct((s, LRU_WIDTH), F32),
                   jax.ShapeDtypeStruct((s, LRU_WIDTH), F32),
                   jax.ShapeDtypeStruct((s, D_MODEL), BF16)],
        compiler_params=_params(),
    )(x, g0, w_in)


def _lru_core_fwd(xb, gate, cw, cb, wa, ba, wx, bx, a_param, ts):
    s = xb.shape[0]

    def body(xb_ref, gate_ref, cw_ref, cb_ref, wa_ref, ba_ref, wx_ref, bx_ref, ap_ref,
             y_ref, hs_ref, prev_ref, hcar_ref):
        @pl.when(pl.program_id(0) == 0)
        def _():
            prev_ref[...] = jnp.zeros_like(prev_ref)
            hcar_ref[...] = jnp.zeros_like(hcar_ref)

        xbv = xb_ref[...]
        taps = _conv_taps(xbv, prev_ref[...])
        xc, _, _, i, _, a, mult = _lru_pre(taps, cw_ref[...], cb_ref[...], wa_ref, ba_ref[...],
                                           wx_ref, bx_ref[...], ap_ref[...])
        bterm = mult * (i * xc)
        cum_a, hloc = _scan_rows(a, bterm)
        hs = cum_a * hcar_ref[SUBLANES - 1:SUBLANES, :] + hloc
        gv = gate_ref[...]
        y_ref[...] = (hs * (gv * _sigmoid(gv))).astype(BF16)
        hs_ref[...] = hs
        prev_ref[...] = xbv[ts - SUBLANES:, :]
        hcar_ref[...] = hs[ts - SUBLANES:, :]

    vec = _const_spec((1, LRU_WIDTH))
    blk = _const_spec((LRU_BLOCKS, LRU_BLOCK_W, LRU_BLOCK_W))
    tile = pl.BlockSpec((ts, LRU_WIDTH), lambda i: (i, 0))
    return pl.pallas_call(
        body, name="lru_core_fwd", grid=(s // ts,),
        in_specs=[tile, tile, _const_spec((CONV_WIDTH, LRU_WIDTH)), vec, blk, vec, blk, vec, vec],
        out_specs=[tile, tile],
        out_shape=[jax.ShapeDtypeStruct((s, LRU_WIDTH), BF16),
                   jax.ShapeDtypeStruct((s, LRU_WIDTH), F32)],
        scratch_shapes=[pltpu.VMEM((SUBLANES, LRU_WIDTH), F32),
                        pltpu.VMEM((SUBLANES, LRU_WIDTH), F32)],
        compiler_params=_params(),
    )(xb, gate, cw, cb, wa, ba, wx, bx, a_param)


def _fox_pre_fwd(x, y, w_out, g1, wf, bf, ts):
    s = x.shape[0]

    def body(x_ref, y_ref, w_ref, g_ref, wf_ref, bf_ref, x1_ref, h1_ref, f_ref, cp_ref, ccar_ref):
        @pl.when(pl.program_id(0) == 0)
        def _():
            ccar_ref[...] = jnp.zeros_like(ccar_ref)

        x1 = x_ref[...] + _dot(y_ref[...], w_ref[...])
        h1 = (x1 * _rstd(x1) * g_ref[...]).astype(BF16)
        f = _dot(h1, wf_ref[...]) + bf_ref[...]
        logsig = jnp.minimum(f, 0.0) - jnp.log1p(jnp.exp(-jnp.abs(f)))
        cum = _cumsum_rows(logsig) + ccar_ref[SUBLANES - 1:SUBLANES, :]
        hi, mid, lo = _split3(cum)
        lane = lax.broadcasted_iota(jnp.int32, cum.shape, 1)
        packed = jnp.where(lane < HEADS, hi.astype(F32), jnp.where(
            lane < 2 * HEADS, pltpu.roll(mid.astype(F32), HEADS, 1), jnp.where(
                lane < 3 * HEADS, pltpu.roll(lo.astype(F32), 2 * HEADS, 1), 0.0)))
        x1_ref[...] = x1
        h1_ref[...] = h1
        f_ref[...] = f
        cp_ref[...] = packed.astype(BF16)
        ccar_ref[...] = cum[ts - SUBLANES:, :]

    return pl.pallas_call(
        body, name="fox_pre_fwd", grid=(s // ts,),
        in_specs=[pl.BlockSpec((ts, D_MODEL), lambda i: (i, 0)),
                  pl.BlockSpec((ts, LRU_WIDTH), lambda i: (i, 0)),
                  _const_spec((LRU_WIDTH, D_MODEL)),
                  _const_spec((1, D_MODEL)),
                  _const_spec((D_MODEL, LANES)),
                  _const_spec((1, LANES))],
        out_specs=[pl.BlockSpec((ts, D_MODEL), lambda i: (i, 0)),
                   pl.BlockSpec((ts, D_MODEL), lambda i: (i, 0)),
                   pl.BlockSpec((ts, LANES), lambda i: (i, 0)),
                   pl.BlockSpec((ts, LANES), lambda i: (i, 0))],
        out_shape=[jax.ShapeDtypeStruct((s, D_MODEL), F32),
                   jax.ShapeDtypeStruct((s, D_MODEL), BF16),
                   jax.ShapeDtypeStruct((s, LANES), F32),
                   jax.ShapeDtypeStruct((s, LANES), BF16)],
        scratch_shapes=[pltpu.VMEM((SUBLANES, LANES), F32)],
        compiler_params=_params(),
    )(x, y, w_out, g1, wf, bf)


def _fox_proj_fwd(h1, cparts, w, sel, bias, out_dtype, ts, name):
    s = h1.shape[0]
    ng = w.shape[0]
    width = HEADS * HEAD_DIM
    use_sel = sel is not None

    def body(*refs):
        if use_sel:
            h_ref, cp_ref, w_ref, sel_ref, b_ref, o_ref = refs
            acc = (_heads_to_padded(_dot(h_ref[...], w_ref[...]))
                   + _dot(cp_ref[...], sel_ref[...]) + b_ref[...])
        else:
            h_ref, w_ref, o_ref = refs
            acc = _heads_to_padded(_dot(h_ref[...], w_ref[...]))
        o_ref[...] = acc.astype(out_dtype)

    in_specs = [pl.BlockSpec((ts, D_MODEL), lambda j, i: (i, 0))]
    args = [h1]
    if use_sel:
        in_specs.append(pl.BlockSpec((ts, LANES), lambda j, i: (i, 0)))
        args.append(cparts)
    in_specs.append(pl.BlockSpec((None, D_MODEL, width), lambda j, i: (j, 0, 0)))
    args.append(w)
    if use_sel:
        in_specs.append(pl.BlockSpec((None, LANES, FOX_PAD), lambda j, i: (j, 0, 0)))
        in_specs.append(pl.BlockSpec((None, 1, FOX_PAD), lambda j, i: (j, 0, 0)))
        args += [sel, bias]
    return pl.pallas_call(
        body, name=name, grid=(ng, s // ts),
        in_specs=in_specs,
        out_specs=pl.BlockSpec((None, ts, FOX_PAD), lambda j, i: (j, i, 0)),
        out_shape=jax.ShapeDtypeStruct((ng, s, FOX_PAD), out_dtype),
        compiler_params=_params(2),
    )(*args)


def _attn_fwd(qkv, blk):
    s = qkv.shape[1]
    nblk = s // blk
    wide = 2 * blk
    heads = [slice(i * HEAD_PAD, (i + 1) * HEAD_PAD) for i in range(HEADS_PER_STEP)]

    def body(q_ref, k_ref, v_ref, o_ref, qb_ref, acc_ref, m_ref):
        qi = pl.program_id(1)
        row = lax.broadcasted_iota(jnp.int32, (blk, blk), 0)
        col = lax.broadcasted_iota(jnp.int32, (blk, blk), 1)
        lane = lax.broadcasted_iota(jnp.int32, (blk, HEAD_PAD), 1)
        qs = [q_ref[:, hd] for hd in heads]
        for i in range(HEADS_PER_STEP):
            acc_ref[i] = jnp.zeros((blk, HEAD_PAD), F32)
            m_ref[i] = jnp.full((blk, HEAD_PAD), NEG_BIG, F32)

        def step(k0, size, masked):
            scores = [_dot_nt(q, k_ref[pl.ds(k0, size), hd]) for q, hd in zip(qs, heads)]
            for i, (sc, hd) in enumerate(zip(scores, heads)):
                v = v_ref[pl.ds(k0, size), hd]
                if masked:
                    sc = jnp.where(col <= row, sc, NEG_BIG)
                m = m_ref[i]
                m_new = jnp.maximum(m, jnp.max(sc, axis=-1, keepdims=True))
                p = jnp.exp((sc - jnp.tile(m_new, (1, size // HEAD_PAD))).astype(BF16))
                acc_ref[i] = jnp.exp(m - m_new) * acc_ref[i] + _dot(p, v)
                m_ref[i] = m_new

        def wide_step(kk, _):
            step(pl.multiple_of(kk * wide, wide), wide, False)
            return 0

        lax.fori_loop(0, qi // 2, wide_step, 0)

        @pl.when(qi % 2 == 1)
        def _():
            step(pl.multiple_of((qi - 1) * blk, blk), blk, False)

        step(pl.multiple_of(qi * blk, blk), blk, True)
        for i, (q, hd) in enumerate(zip(qs, heads)):
            acc = acc_ref[i]
            l = jnp.broadcast_to(acc[:, LANE_ONE_V:LANE_ONE_V + 1], (blk, HEAD_PAD))
            o_ref[:, hd] = (acc / l).astype(BF16)
            hi, mid, lo = _split3(-(m_ref[i] + jnp.log(l)))
            qb_ref[:, hd] = jnp.where(lane == LANE_LSE, hi, jnp.where(
                lane == LANE_LSE + 1, mid, jnp.where(lane == LANE_LSE + 2, lo, q)))

    width = HEADS_PER_STEP * HEAD_PAD

    def whole(j):
        return pl.BlockSpec((None, s, width), lambda h, i: (j, 0, h))

    out_spec = pl.BlockSpec((blk, width), lambda h, i: (i, h))
    return pl.pallas_call(
        body, name="attn_fwd", grid=(HEADS // HEADS_PER_STEP, nblk),
        in_specs=[pl.BlockSpec((None, blk, width), lambda h, i: (0, i, h)), whole(1), whole(2)],
        out_specs=[out_spec, out_spec],
        out_shape=[jax.ShapeDtypeStruct((s, FOX_PAD), BF16),
                   jax.ShapeDtypeStruct((s, FOX_PAD), BF16)],
        scratch_shapes=[pltpu.VMEM((HEADS_PER_STEP, blk, HEAD_PAD), F32),
                        pltpu.VMEM((HEADS_PER_STEP, blk, HEAD_PAD), F32)],
        compiler_params=_params(2),
    )(qkv, qkv, qkv)


def _fox_out_loss(o, gate, w_out, x1, target, gf, ts):
    s = x1.shape[0]

    def body(o_ref, gt_ref, w_ref, x1_ref, t_ref, g_ref, dx2_ref, y2_ref, loss_ref, gfin_ref):
        @pl.when(pl.program_id(0) == 0)
        def _():
            loss_ref[...] = jnp.zeros_like(loss_ref)
            gfin_ref[...] = jnp.zeros_like(gfin_ref)

        gv = gt_ref[...]
        y2 = _heads_from_padded(o_ref[...] * (gv * _sigmoid(gv))).astype(BF16)
        x2 = x1_ref[...] + _dot(y2, w_ref[...])
        rstd = _rstd(x2)
        xhat = x2 * rstd
        g = g_ref[...]
        diff = xhat * g - t_ref[...]
        loss_ref[...] += 0.5 * jnp.sum(jnp.mean(diff * diff, axis=-1, keepdims=True))
        dy = diff * (1.0 / D_MODEL)
        gfin_ref[...] += jnp.sum(dy * xhat, axis=0, keepdims=True)
        dxh = dy * g
        dx2_ref[...] = rstd * (dxh - xhat * jnp.mean(dxh * xhat, axis=-1, keepdims=True))
        y2_ref[...] = y2

    return pl.pallas_call(
        body, name="fox_out_loss", grid=(s // ts,),
        in_specs=[pl.BlockSpec((ts, FOX_PAD), lambda i: (i, 0)),
                  pl.BlockSpec((ts, FOX_PAD), lambda i: (i, 0)),
                  _const_spec((HEADS * HEAD_DIM, D_MODEL)),
                  pl.BlockSpec((ts, D_MODEL), lambda i: (i, 0)),
                  pl.BlockSpec((ts, D_MODEL), lambda i: (i, 0)),
                  _const_spec((1, D_MODEL))],
        out_specs=[pl.BlockSpec((ts, D_MODEL), lambda i: (i, 0)),
                   pl.BlockSpec((ts, HEADS * HEAD_DIM), lambda i: (i, 0)),
                   pl.BlockSpec((SUBLANES, LANES), lambda i: (0, 0)),
                   pl.BlockSpec((1, D_MODEL), lambda i: (0, 0))],
        out_shape=[jax.ShapeDtypeStruct((s, D_MODEL), F32),
                   jax.ShapeDtypeStruct((s, HEADS * HEAD_DIM), BF16),
                   jax.ShapeDtypeStruct((SUBLANES, LANES), F32),
                   jax.ShapeDtypeStruct((1, D_MODEL), F32)],
        compiler_params=_params(),
    )(o, gate, w_out, x1, target, gf)


def _fox_out_bwd(dx2, w_out, o, gate, ts):
    s = dx2.shape[0]

    def body(dx_ref, w_ref, o_ref, gt_ref, do_ref, dg_ref):
        lane = lax.broadcasted_iota(jnp.int32, (ts, HEAD_PAD), 1)
        dy2 = _heads_to_padded(_dot_nt(dx_ref[...].astype(BF16), w_ref[...]))
        gv = gt_ref[...]
        sg = _sigmoid(gv)
        ov = o_ref[...]
        dov = dy2 * (gv * sg)
        dg_ref[...] = (dy2 * ov * (sg * (1.0 + gv * (1.0 - sg)))).astype(BF16)
        prod = dov * ov
        for h in range(HEADS):
            sl = slice(h * HEAD_PAD, (h + 1) * HEAD_PAD)
            delta = jnp.sum(prod[:, sl], axis=-1, keepdims=True)
            hi = delta.astype(BF16)
            lo = (delta - hi.astype(F32)).astype(BF16)
            do_h = dov[:, sl].astype(BF16)
            do_ref[:, sl] = jnp.where(lane == LANE_ONE_V, -hi,
                                      jnp.where(lane == LANE_ONE_V + 1, -lo, do_h))

    tile = pl.BlockSpec((ts, FOX_PAD), lambda i: (i, 0))
    return pl.pallas_call(
        body, name="fox_out_bwd", grid=(s // ts,),
        in_specs=[pl.BlockSpec((ts, D_MODEL), lambda i: (i, 0)),
                  _const_spec((HEADS * HEAD_DIM, D_MODEL)), tile, tile],
        out_specs=[tile, tile],
        out_shape=[jax.ShapeDtypeStruct((s, FOX_PAD), BF16),
                   jax.ShapeDtypeStruct((s, FOX_PAD), BF16)],
        compiler_params=_params(),
    )(dx2, w_out, o, gate)


def _attn_bwd(qb, qkv, do, blk):
    s = qb.shape[0]
    nblk = s // blk
    heads = [slice(i * HEAD_PAD, (i + 1) * HEAD_PAD) for i in range(HEADS_PER_STEP)]

    def body(q_ref, k_ref, v_ref, do_ref, dq_ref, dk_ref, dv_ref, dcum_ref, dq_acc, dkt_acc,
             dvt_acc, qt_ref, dot_ref):
        group = pl.program_id(0)
        kj = pl.program_id(1)
        row = lax.broadcasted_iota(jnp.int32, (blk, blk), 0)
        col = lax.broadcasted_iota(jnp.int32, (blk, blk), 1)
        lane = lax.broadcasted_iota(jnp.int32, (blk, LANES), 1)
        mine = [lane == group * HEADS_PER_STEP + i for i in range(HEADS_PER_STEP)]

        @pl.when(kj == 0)
        def _():
            dq_acc[...] = jnp.zeros_like(dq_acc)

            def transpose_block(bi, _):
                r0 = pl.multiple_of(bi * blk, blk)
                for i, hd in enumerate(heads):
                    qt_ref[i, bi] = q_ref[pl.ds(r0, blk), hd].T
                    dot_ref[i, bi] = do_ref[pl.ds(r0, blk), hd].T
                return 0

            lax.fori_loop(0, nblk, transpose_block, 0)

        @pl.when((group == 0) & (kj == 0))
        def _():
            dcum_ref[...] = jnp.zeros_like(dcum_ref)

        k0 = pl.multiple_of(kj * blk, blk)
        ks = [k_ref[:, hd] for hd in heads]
        vs = [v_ref[:, hd] for hd in heads]

        dkt_acc[...] = jnp.zeros_like(dkt_acc)
        dvt_acc[...] = jnp.zeros_like(dvt_acc)

        def step(qi, masked):
            q0 = pl.multiple_of(qi * blk, blk)
            qs = [q_ref[pl.ds(q0, blk), hd] for hd in heads]
            dos = [do_ref[pl.ds(q0, blk), hd] for hd in heads]
            scores = [_dot_nt(q, k) for q, k in zip(qs, ks)]
            dps = [_dot_nt(dov, v) for dov, v in zip(dos, vs)]
            for i, (hd, k, sc, dp) in enumerate(zip(heads, ks, scores, dps)):
                p = jnp.exp(sc.astype(BF16))
                if masked:
                    p = jnp.where(col <= row, p, jnp.zeros_like(p))
                ds = (p.astype(F32) * dp).astype(BF16)
                dvt_acc[i] += _dot(dot_ref[i, qi], p)
                dkt_acc[i] += _dot(qt_ref[i, qi], ds)
                dq_acc[pl.ds(q0, blk), hd] += _dot(ds, k)

        step(kj, True)

        def q_step(qi, _):
            step(qi, False)
            return 0

        lax.fori_loop(kj + 1, nblk, q_step, 0)
        dcum = dcum_ref[pl.ds(k0, blk), :]
        for i, (hd, mask) in enumerate(zip(heads, mine)):
            dk = dkt_acc[i].T
            dk_ref[:, hd] = dk.astype(BF16)
            dv_ref[:, hd] = dvt_acc[i].T.astype(BF16)
            dcum = jnp.where(mask, -dk[:, LANE_CK:LANE_CK + 1], dcum)
        dcum_ref[pl.ds(k0, blk), :] = dcum

        @pl.when(kj == nblk - 1)
        def _():
            def finish(bi, _):
                r0 = pl.multiple_of(bi * blk, blk)
                dcum = dcum_ref[pl.ds(r0, blk), :]
                for hd, mask in zip(heads, mine):
                    dq = dq_acc[pl.ds(r0, blk), hd]
                    dq_ref[pl.ds(r0, blk), hd] = dq.astype(BF16)
                    dcum = dcum + jnp.where(mask, dq[:, LANE_RB:LANE_RB + 1], 0.0)
                dcum_ref[pl.ds(r0, blk), :] = dcum
                return 0

            lax.fori_loop(0, nblk, finish, 0)

    width = HEADS_PER_STEP * HEAD_PAD
    whole = pl.BlockSpec((s, width), lambda h, j: (0, h))
    whole_in = pl.BlockSpec((s, width), lambda h, j: (0, h), pipeline_mode=pl.Buffered(1))
    part = pl.BlockSpec((blk, width), lambda h, j: (j, h))
    out = jax.ShapeDtypeStruct((s, FOX_PAD), BF16)
    return pl.pallas_call(
        body, name="attn_bwd", grid=(HEADS // HEADS_PER_STEP, nblk),
        in_specs=[whole_in,
                  pl.BlockSpec((None, blk, width), lambda h, j: (1, j, h)),
                  pl.BlockSpec((None, blk, width), lambda h, j: (2, j, h)),
                  whole_in],
        out_specs=[whole, part, part, pl.BlockSpec((s, LANES), lambda h, j: (0, 0))],
        out_shape=[out, out, out, jax.ShapeDtypeStruct((s, LANES), F32)],
        scratch_shapes=[pltpu.VMEM((s, width), F32),
                        pltpu.VMEM((HEADS_PER_STEP, HEAD_PAD, blk), F32),
                        pltpu.VMEM((HEADS_PER_STEP, HEAD_PAD, blk), F32),
                        pltpu.VMEM((HEADS_PER_STEP, nblk, HEAD_PAD, blk), BF16),
                        pltpu.VMEM((HEADS_PER_STEP, nblk, HEAD_PAD, blk), BF16)],
        compiler_params=_params(2),
    )(qb, qkv, qkv, do)


def _fox_in_bwd(dq, dk, dv, dg, wt, wft, dcum, f, x1, dx2, g1, ts):
    s = x1.shape[0]
    nt = s // ts
    width = HEADS * HEAD_DIM

    def body(dq_ref, dk_ref, dv_ref, dg_ref, wt_ref, wft_ref, dcum_ref, f_ref, x1_ref, dx2_ref,
             g_ref, dx1_ref, dx1b_ref, df_ref, du_ref, gn_ref, gbf_ref, rcar_ref):
        @pl.when(pl.program_id(0) == 0)
        def _():
            rcar_ref[...] = jnp.zeros_like(rcar_ref)
            gn_ref[...] = jnp.zeros_like(gn_ref)
            gbf_ref[...] = jnp.zeros_like(gbf_ref)

        rsum = _cumsum_rows(dcum_ref[...], reverse=True) + rcar_ref[0:1, :]
        df = rsum * _sigmoid(-f_ref[...])
        dfb = df.astype(BF16)
        dh = _dot_nt(dfb, wft_ref[...])
        for j, ref in enumerate((dq_ref, dk_ref, dv_ref, dg_ref)):
            du = _heads_from_padded(ref[...])
            du_ref[j] = du
            dh = dh + _dot_nt(du, wt_ref[j])
        dxn, dgn = _norm_bwd(x1_ref[...], g_ref[...], dh)
        dx1 = dx2_ref[...] + dxn
        dx1_ref[...] = dx1
        dx1b_ref[...] = dx1.astype(BF16)
        df_ref[...] = dfb
        gn_ref[...] += dgn
        gbf_ref[...] += jnp.sum(df, axis=0, keepdims=True)
        rcar_ref[...] = rsum[0:SUBLANES, :]

    rev = lambda i: (nt - 1 - i, 0)
    wide = pl.BlockSpec((ts, FOX_PAD), rev)
    return pl.pallas_call(
        body, name="fox_in_bwd", grid=(nt,),
        in_specs=[wide, wide, wide, wide,
                  _const_spec((4, D_MODEL, width)),
                  _const_spec((D_MODEL, LANES)),
                  pl.BlockSpec((ts, LANES), rev),
                  pl.BlockSpec((ts, LANES), rev),
                  pl.BlockSpec((ts, D_MODEL), rev),
                  pl.BlockSpec((ts, D_MODEL), rev),
                  _const_spec((1, D_MODEL))],
        out_specs=[pl.BlockSpec((ts, D_MODEL), rev),
                   pl.BlockSpec((ts, D_MODEL), rev),
                   pl.BlockSpec((ts, LANES), rev),
                   pl.BlockSpec((4, ts, width), lambda i: (0, nt - 1 - i, 0)),
                   pl.BlockSpec((1, D_MODEL), lambda i: (0, 0)),
                   pl.BlockSpec((1, LANES), lambda i: (0, 0))],
        out_shape=[jax.ShapeDtypeStruct((s, D_MODEL), F32),
                   jax.ShapeDtypeStruct((s, D_MODEL), BF16),
                   jax.ShapeDtypeStruct((s, LANES), BF16),
                   jax.ShapeDtypeStruct((4, s, width), BF16),
                   jax.ShapeDtypeStruct((1, D_MODEL), F32),
                   jax.ShapeDtypeStruct((1, LANES), F32)],
        scratch_shapes=[pltpu.VMEM((SUBLANES, LANES), F32)],
        compiler_params=_params(),
    )(dq, dk, dv, dg, wt, wft, dcum, f, x1, dx2, g1)


def _lru_core_bwd(dx1b, w_out, xb, gate, hs, cw, cb, wa, ba, wx, bx, a_param, wa_t, wx_t, ts):
    s = xb.shape[0]
    nt = s // ts
    tpb = ts // SUBLANES

    def body(dx_ref, wo_ref, xb_ref, xbh_ref, gate_ref, hs_ref, hsh_ref, cw_ref, cb_ref, wa_ref,
             ba_ref, wx_ref, bx_ref, ap_ref, wat_ref, wxt_ref,
             du_ref, gwa_ref, gwx_ref, gvec_ref, acar_ref, dhcar_ref, dxccar_ref):
        step = pl.program_id(0)

        @pl.when(step == 0)
        def _():
            acar_ref[...] = jnp.zeros_like(acar_ref)
            dhcar_ref[...] = jnp.zeros_like(dhcar_ref)
            dxccar_ref[...] = jnp.zeros_like(dxccar_ref)
            gwa_ref[...] = jnp.zeros_like(gwa_ref)
            gwx_ref[...] = jnp.zeros_like(gwx_ref)
            gvec_ref[...] = jnp.zeros_like(gvec_ref)

        first_tile = step == nt - 1
        halo_on = jnp.where(first_tile, 0.0, 1.0)
        prev8 = xbh_ref[...] * halo_on
        hprev_row = hsh_ref[SUBLANES - 1:SUBLANES, :] * halo_on

        xbv = xb_ref[...]
        taps = _conv_taps(xbv, prev8)
        cw_v = cw_ref[...]
        xc, xcb, r, i, sp, a, mult = _lru_pre(taps, cw_v, cb_ref[...], wa_ref, ba_ref[...],
                                              wx_ref, bx_ref[...], ap_ref[...])
        hs = hs_ref[...]
        gv = gate_ref[...]
        sg = _sigmoid(gv)
        dy = _dot_nt(dx_ref[...], wo_ref[...])
        dhs = dy * (gv * sg)
        dgate = dy * hs * (sg * (1.0 + gv * (1.0 - sg)))

        rows = lax.broadcasted_iota(jnp.int32, a.shape, 0)
        a_next = jnp.where(rows < ts - 1, pltpu.roll(a, ts - 1, 0), acar_ref[0:1, :])
        cum_a, dh_loc = _scan_rows(a_next, dhs, reverse=True)
        dh = cum_a * dhcar_ref[0:1, :] + dh_loc
        h_prev = jnp.where(rows >= 1, pltpu.roll(hs, 1, 0), hprev_row)

        da = dh * h_prev
        ixc = i * xc
        dmult = dh * ixc
        di = dh * mult * xc
        dxc = dh * mult * i
        dlog_a = da * a - dmult * (a * a) / mult
        dr = dlog_a * ((-LRU_C) * sp)
        dsp = jnp.sum(dlog_a * ((-LRU_C) * r), axis=0, keepdims=True)
        dra = dr * r * (1.0 - r)
        dia = di * i * (1.0 - i)
        drab = dra.astype(BF16)
        diab = dia.astype(BF16)
        back = []
        for n in range(LRU_BLOCKS):
            sl = slice(n * LRU_BLOCK_W, (n + 1) * LRU_BLOCK_W)
            gwa_ref[n] += _dot_tn(xcb[:, sl], drab[:, sl])
            gwx_ref[n] += _dot_tn(xcb[:, sl], diab[:, sl])
            back.append(_dot(drab[:, sl], wat_ref[n]) + _dot(diab[:, sl], wxt_ref[n]))
        dxc = dxc + jnp.concatenate(back, axis=1)

        nxt8 = dxccar_ref[...]
        rows8 = lax.broadcasted_iota(jnp.int32, nxt8.shape, 0)
        dxb = cw_v[3:4] * dxc
        for j in range(1, CONV_WIDTH):
            rj = pltpu.roll(dxc, ts - j, 0)
            pj = pltpu.roll(nxt8, SUBLANES - j, 0)
            tail = jnp.where(rows8 >= SUBLANES - j, pj, rj[ts - SUBLANES:])
            dxb = dxb + cw_v[3 - j:4 - j] * jnp.concatenate([rj[:ts - SUBLANES], tail], axis=0)

        du_ref[:, :LRU_WIDTH] = dxb.astype(BF16)
        du_ref[:, LRU_WIDTH:] = dgate.astype(BF16)

        z = -ap_ref[...]
        gvec = [jnp.sum(dxc * taps[3 - k], axis=0, keepdims=True) for k in range(CONV_WIDTH)]
        gvec.append(jnp.sum(dxc, axis=0, keepdims=True))
        gvec.append(jnp.sum(dra, axis=0, keepdims=True))
        gvec.append(jnp.sum(dia, axis=0, keepdims=True))
        gvec.append(-dsp * _sigmoid(z))
        gvec_ref[...] += jnp.concatenate(gvec, axis=0)

        acar_ref[...] = a[0:SUBLANES, :]
        dhcar_ref[...] = dh[0:SUBLANES, :]
        dxccar_ref[...] = dxc[0:SUBLANES, :]

    rev = lambda i: (nt - 1 - i, 0)
    halo = lambda i: (jnp.maximum((nt - 1 - i) * tpb - 1, 0), 0)
    tile = pl.BlockSpec((ts, LRU_WIDTH), rev)
    halo_spec = pl.BlockSpec((SUBLANES, LRU_WIDTH), halo)
    vec = _const_spec((1, LRU_WIDTH))
    blk = _const_spec((LRU_BLOCKS, LRU_BLOCK_W, LRU_BLOCK_W))
    acc_blk = pl.BlockSpec((LRU_BLOCKS, LRU_BLOCK_W, LRU_BLOCK_W), lambda i: (0, 0, 0))
    return pl.pallas_call(
        body, name="lru_core_bwd", grid=(nt,),
        in_specs=[pl.BlockSpec((ts, D_MODEL), rev),
                  _const_spec((LRU_WIDTH, D_MODEL)),
                  tile, halo_spec, tile, tile, halo_spec,
                  _const_spec((CONV_WIDTH, LRU_WIDTH)), vec, blk, vec, blk, vec, vec, blk, blk],
        out_specs=[pl.BlockSpec((ts, 2 * LRU_WIDTH), rev), acc_blk, acc_blk,
                   pl.BlockSpec((SUBLANES, LRU_WIDTH), lambda i: (0, 0))],
        out_shape=[jax.ShapeDtypeStruct((s, 2 * LRU_WIDTH), BF16),
                   jax.ShapeDtypeStruct((LRU_BLOCKS, LRU_BLOCK_W, LRU_BLOCK_W), F32),
                   jax.ShapeDtypeStruct((LRU_BLOCKS, LRU_BLOCK_W, LRU_BLOCK_W), F32),
                   jax.ShapeDtypeStruct((SUBLANES, LRU_WIDTH), F32)],
        scratch_shapes=[pltpu.VMEM((SUBLANES, LRU_WIDTH), F32),
                        pltpu.VMEM((SUBLANES, LRU_WIDTH), F32),
                        pltpu.VMEM((SUBLANES, LRU_WIDTH), F32)],
        compiler_params=_params(),
    )(dx1b, w_out, xb, xb, gate, hs, hs, cw, cb, wa, ba, wx, bx, a_param, wa_t, wx_t)


def _lru_in_bwd(du, w_in, x, dx1, g0, ts):
    s = x.shape[0]

    def body(du_ref, w_ref, x_ref, dx1_ref, g_ref, gx_ref, gn_ref):
        @pl.when(pl.program_id(0) == 0)
        def _():
            gn_ref[...] = jnp.zeros_like(gn_ref)

        duv = du_ref[...]
        dh = _dot_nt(duv[:, 0:LRU_IN_SHARD], w_ref[0])
        for j in range(1, N_DEV):
            dh = dh + _dot_nt(duv[:, j * LRU_IN_SHARD:(j + 1) * LRU_IN_SHARD], w_ref[j])
        dxn, dgn = _norm_bwd(x_ref[...], g_ref[...], dh)
        gx_ref[...] = dx1_ref[...] + dxn
        gn_ref[...] += dgn

    tile = pl.BlockSpec((ts, D_MODEL), lambda i: (i, 0))
    return pl.pallas_call(
        body, name="lru_in_bwd", grid=(s // ts,),
        in_specs=[pl.BlockSpec((ts, 2 * LRU_WIDTH), lambda i: (i, 0)),
                  _const_spec((N_DEV, D_MODEL, LRU_IN_SHARD)), tile, tile,
                  _const_spec((1, D_MODEL))],
        out_specs=[tile, pl.BlockSpec((1, D_MODEL), lambda i: (0, 0))],
        out_shape=[jax.ShapeDtypeStruct((s, D_MODEL), F32),
                   jax.ShapeDtypeStruct((1, D_MODEL), F32)],
        compiler_params=_params(),
    )(du, w_in, x, dx1, g0)


def _weight_grad(a, b, ts, name, scale=1.0, col_shards=1):
    s, ka = a.shape
    nb = b.shape[1]
    nt = s // ts
    per = nb // col_shards

    def body(a_ref, b_ref, o_ref):
        @pl.when(pl.program_id(0) == 0)
        def _():
            o_ref[...] = jnp.zeros_like(o_ref)

        if col_shards == 1:
            o_ref[...] += _dot_tn(a_ref[...], b_ref[...])
        else:
            av, bv = a_ref[...], b_ref[...]
            for j in range(col_shards):
                o_ref[j] += _dot_tn(av, bv[:, j * per:(j + 1) * per])
        if scale != 1.0:
            @pl.when(pl.program_id(0) == nt - 1)
            def _():
                o_ref[...] = o_ref[...] * scale

    out_dims = (ka, nb) if col_shards == 1 else (col_shards, ka, per)
    return pl.pallas_call(
        body, name=name, grid=(nt,),
        in_specs=[pl.BlockSpec((ts, ka), lambda i: (i, 0)),
                  pl.BlockSpec((ts, nb), lambda i: (i, 0))],
        out_specs=pl.BlockSpec(out_dims, lambda i: (0,) * len(out_dims)),
        out_shape=jax.ShapeDtypeStruct(out_dims, F32),
        compiler_params=_params(),
    )(a, b)


def _sum_parts(gp_ref):
    g = gp_ref[0].astype(F32)
    for k in range(1, gp_ref.shape[0]):
        g = g + gp_ref[k].astype(F32)
    return g


def _adamw(g_parts, w, m, v, tr, name):
    nparts, rows, cols = g_parts.shape

    def body(gp_ref, w_ref, m_ref, v_ref, g_ref, d_ref, mo_ref, vo_ref):
        g = _sum_parts(gp_ref)
        m2 = ADAM_B1 * m_ref[...] + (1.0 - ADAM_B1) * g
        v2 = ADAM_B2 * v_ref[...] + (1.0 - ADAM_B2) * (g * g)
        m_hat = m2 / (1.0 - ADAM_B1 ** ADAM_STEP)
        v_hat = v2 / (1.0 - ADAM_B2 ** ADAM_STEP)
        g_ref[...] = g
        d_ref[...] = (-ADAM_LR) * (m_hat / (jnp.sqrt(v_hat) + ADAM_EPS) + ADAM_WD * w_ref[...])
        mo_ref[...] = m2
        vo_ref[...] = v2

    tile = pl.BlockSpec((tr, cols), lambda i: (i, 0))
    out = jax.ShapeDtypeStruct((rows, cols), F32)
    return pl.pallas_call(
        body, name=name, grid=(rows // tr,),
        in_specs=[pl.BlockSpec((nparts, tr, cols), lambda i: (0, i, 0)), tile, tile, tile],
        out_specs=[tile, tile, tile, tile],
        out_shape=[out, out, out, out],
        compiler_params=_params(),
    )(g_parts, w, m, v)


def _reduce_parts(g_parts, name):
    _, rows, cols = g_parts.shape

    def body(gp_ref, g_ref):
        g_ref[...] = _sum_parts(gp_ref)

    return pl.pallas_call(
        body, name=name,
        out_shape=jax.ShapeDtypeStruct((rows, cols), F32),
        compiler_params=pltpu.CompilerParams(vmem_limit_bytes=VMEM_LIMIT_BYTES),
    )(g_parts)


def _mesh_pos():
    ix, iy, ic = lax.axis_index("x"), lax.axis_index("y"), lax.axis_index("c")
    return ix, iy, ic


def _peer(ix, iy, ic, mask):
    px = 1 - ix if mask & 4 else ix
    py = 1 - iy if mask & 2 else iy
    pc = 1 - ic if mask & 1 else ic
    return (px, py, pc), 4 * px + 2 * py + pc


def _exchange(arrays, scatter, name):
    n = len(arrays)

    def body(*refs):
        x_refs, o_refs = refs[:n], refs[n:2 * n]
        send_sems, recv_sems, local_sems = refs[2 * n:]
        ix, iy, ic = _mesh_pos()
        me = 4 * ix + 2 * iy + ic

        def src(a, dest):
            return x_refs[a].at[dest] if scatter else x_refs[a]

        local = [pltpu.make_async_copy(src(a, me), o_refs[a].at[me], local_sems.at[a])
                 for a in range(n)]
        for cp in local:
            cp.start()
        sends = []
        for mask in range(1, N_DEV):
            peer, pidx = _peer(ix, iy, ic, mask)
            for a in range(n):
                cp = pltpu.make_async_remote_copy(
                    src_ref=src(a, pidx), dst_ref=o_refs[a].at[me],
                    send_sem=send_sems.at[a, mask - 1], recv_sem=recv_sems.at[a, mask - 1],
                    device_id=peer, device_id_type=pl.DeviceIdType.MESH)
                cp.start()
                sends.append(cp)
        for mask in range(1, N_DEV):
            peer, pidx = _peer(ix, iy, ic, mask)
            for a in range(n):
                pltpu.make_async_remote_copy(
                    src_ref=src(a, me), dst_ref=o_refs[a].at[pidx],
                    send_sem=send_sems.at[a, mask - 1], recv_sem=recv_sems.at[a, mask - 1],
                    device_id=peer, device_id_type=pl.DeviceIdType.MESH).wait_recv()
        for cp in sends:
            cp.wait_send()
        for cp in local:
            cp.wait()

    out_shape = [jax.ShapeDtypeStruct(x.shape if scatter else (N_DEV,) + x.shape, x.dtype)
                 for x in arrays]
    return pl.pallas_call(
        body, name=name,
        in_specs=[pl.BlockSpec(memory_space=pl.ANY)] * n,
        out_specs=[pl.BlockSpec(memory_space=pl.ANY)] * n,
        out_shape=out_shape,
        scratch_shapes=[pltpu.SemaphoreType.DMA((n, N_DEV - 1)),
                        pltpu.SemaphoreType.DMA((n, N_DEV - 1)),
                        pltpu.SemaphoreType.DMA((n,))],
    )(*arrays)


def _gather_two_level(arrays, name):
    n = len(arrays)

    def body(*refs):
        x_refs, o_refs = refs[:n], refs[n:2 * n]
        send_sems, recv_sems, local_sems = refs[2 * n:]
        ix, iy, ic = _mesh_pos()
        me, sibling = (ix, iy, ic), (ix, iy, 1 - ic)
        chips = [(1 - ix, iy), (ix, 1 - iy), (1 - ix, 1 - iy)]

        def idx(px, py, pc):
            return 4 * px + 2 * py + pc

        def copy(a, k, block, to, src=None):
            dst = o_refs[a].at[idx(*block)]
            return pltpu.make_async_remote_copy(
                src_ref=dst if src is None else src, dst_ref=dst,
                send_sem=send_sems.at[a, k], recv_sem=recv_sems.at[a, k],
                device_id=to, device_id_type=pl.DeviceIdType.MESH)

        local = [pltpu.make_async_copy(x_refs[a], o_refs[a].at[idx(*me)], local_sems.at[a])
                 for a in range(n)]
        for cp in local:
            cp.start()
        first = []
        for a in range(n):
            first.append(copy(a, 0, me, sibling, src=x_refs[a]))
            first += [copy(a, 1 + j, me, (*chip, ic), src=x_refs[a])
                      for j, chip in enumerate(chips)]
        for cp in first:
            cp.start()
        passed = []
        for j, chip in enumerate(chips):
            for a in range(n):
                copy(a, 1 + j, (*chip, ic), me).wait_recv()
                cp = copy(a, 4 + j, (*chip, ic), sibling)
                cp.start()
                passed.append(cp)
        for a in range(n):
            copy(a, 0, sibling, me).wait_recv()
            for j, chip in enumerate(chips):
                copy(a, 4 + j, (*chip, 1 - ic), me).wait_recv()
        for cp in first + passed:
            cp.wait_send()
        for cp in local:
            cp.wait()

    return pl.pallas_call(
        body, name=name,
        in_specs=[pl.BlockSpec(memory_space=pl.ANY)] * n,
        out_specs=[pl.BlockSpec(memory_space=pl.ANY)] * n,
        out_shape=[jax.ShapeDtypeStruct((N_DEV,) + x.shape, x.dtype) for x in arrays],
        scratch_shapes=[pltpu.SemaphoreType.DMA((n, N_DEV - 1)),
                        pltpu.SemaphoreType.DMA((n, N_DEV - 1)),
                        pltpu.SemaphoreType.DMA((n,))],
    )(*arrays)


def _swap_sibling(arrays, name):
    n = len(arrays)
    n_chips = N_DEV // 2

    def body(*refs):
        x_refs, got_refs = refs[:n], refs[n:2 * n]
        send_sems, recv_sems = refs[2 * n:]
        ix, iy, ic = _mesh_pos()
        sibling = (ix, iy, 1 - ic)
        sends = []
        for a in range(n):
            for q in range(n_chips):
                cp = pltpu.make_async_remote_copy(
                    src_ref=x_refs[a].at[q, 1 - ic], dst_ref=got_refs[a].at[q],
                    send_sem=send_sems.at[a, q], recv_sem=recv_sems.at[a, q],
                    device_id=sibling, device_id_type=pl.DeviceIdType.MESH)
                cp.start()
                sends.append(cp)
        for cp in sends:
            cp.wait()

    return pl.pallas_call(
        body, name=name,
        in_specs=[pl.BlockSpec(memory_space=pl.ANY)] * n,
        out_specs=[pl.BlockSpec(memory_space=pl.ANY)] * n,
        out_shape=[jax.ShapeDtypeStruct((n_chips,) + x.shape[2:], x.dtype) for x in arrays],
        scratch_shapes=[pltpu.SemaphoreType.DMA((n, n_chips)),
                        pltpu.SemaphoreType.DMA((n, n_chips))],
    )(*arrays)


def _exchange_chips(arrays, name):
    n = len(arrays)
    n_chips = N_DEV // 2

    def body(*refs):
        x_refs, o_refs = refs[:n], refs[n:2 * n]
        send_sems, recv_sems, local_sems = refs[2 * n:]
        ix, iy, ic = _mesh_pos()
        my_chip = 2 * ix + iy
        local = [pltpu.make_async_copy(x_refs[a].at[my_chip], o_refs[a].at[my_chip],
                                       local_sems.at[a]) for a in range(n)]
        for cp in local:
            cp.start()
        sends = []
        for mask in range(1, n_chips):
            px = 1 - ix if mask & 2 else ix
            py = 1 - iy if mask & 1 else iy
            for a in range(n):
                cp = pltpu.make_async_remote_copy(
                    src_ref=x_refs[a].at[2 * px + py], dst_ref=o_refs[a].at[my_chip],
                    send_sem=send_sems.at[a, mask - 1], recv_sem=recv_sems.at[a, mask - 1],
                    device_id=(px, py, ic), device_id_type=pl.DeviceIdType.MESH)
                cp.start()
                sends.append(cp)
        for mask in range(1, n_chips):
            px = 1 - ix if mask & 2 else ix
            py = 1 - iy if mask & 1 else iy
            for a in range(n):
                pltpu.make_async_remote_copy(
                    src_ref=x_refs[a].at[my_chip], dst_ref=o_refs[a].at[2 * px + py],
                    send_sem=send_sems.at[a, mask - 1], recv_sem=recv_sems.at[a, mask - 1],
                    device_id=(px, py, ic), device_id_type=pl.DeviceIdType.MESH).wait_recv()
        for cp in sends:
            cp.wait_send()
        for cp in local:
            cp.wait()

    return pl.pallas_call(
        body, name=name,
        in_specs=[pl.BlockSpec(memory_space=pl.ANY)] * n,
        out_specs=[pl.BlockSpec(memory_space=pl.ANY)] * n,
        out_shape=[jax.ShapeDtypeStruct(x.shape, x.dtype) for x in arrays],
        scratch_shapes=[pltpu.SemaphoreType.DMA((n, n_chips - 1)),
                        pltpu.SemaphoreType.DMA((n, n_chips - 1)),
                        pltpu.SemaphoreType.DMA((n,))],
    )(*arrays)


def _pair_sum(core, x, got, name):
    nq, rows, cols = got.shape

    def body(c_ref, x_ref, g_ref, o_ref):
        o_ref[...] = (x_ref[...] + g_ref[...]).astype(BF16)

    blk = pl.BlockSpec((None, rows, cols), lambda q, c: (q, 0, 0))
    return pl.pallas_call(
        body, name=name,
        grid_spec=pltpu.PrefetchScalarGridSpec(
            num_scalar_prefetch=1, grid=(nq,),
            in_specs=[pl.BlockSpec((None, None, rows, cols), lambda q, c: (q, c[0], 0, 0)), blk],
            out_specs=blk),
        out_shape=jax.ShapeDtypeStruct(got.shape, BF16),
        compiler_params=_params(),
    )(core, x, got)


def _selectors():
    r = lax.broadcasted_iota(jnp.int32, (LANES, FOX_PAD), 0)
    c = lax.broadcasted_iota(jnp.int32, (LANES, FOX_PAD), 1)
    part, head_r = r // HEADS, r % HEADS
    head_c, lane_c = c // HEAD_PAD, c % HEAD_PAD
    same = (head_r == head_c) & (part < 3)
    sel_q = jnp.where(same & (lane_c == LANE_RB + part), 1.0, 0.0)
    sel_k = jnp.where(same & (lane_c == LANE_CK + part), -1.0, 0.0)
    sel = jnp.stack([sel_q, sel_k, jnp.zeros_like(sel_q)]).astype(BF16)
    lane = lax.broadcasted_iota(jnp.int32, (1, FOX_PAD), 1) % HEAD_PAD
    ones_q = jnp.where((lane >= LANE_CK) & (lane < LANE_CK + 3), 1.0, 0.0)
    ones_k = jnp.where(((lane >= LANE_RB) & (lane < LANE_RB + 3))
                       | ((lane >= LANE_LSE) & (lane < LANE_LSE + 3)), 1.0, 0.0)
    ones_v = jnp.where((lane >= LANE_ONE_V) & (lane < LANE_ONE_V + 2), 1.0, 0.0)
    bias = jnp.stack([ones_q, ones_k, ones_v]).astype(F32)
    return sel, bias


def _local_step(x, target, norm_g, final_g, w_in8, conv_w, conv_b, wa, ba, wx, bx, a_param,
                w_out_b, fox_in8, b_f, fox_out_b, blk=512, ts=256):
    qk_scale = 1.0 / (HEAD_DIM ** 0.5)
    g0, g1 = norm_g[0:1], norm_g[1:2]
    gf = final_g.reshape(1, D_MODEL)
    wa_b, wx_b = wa.astype(BF16), wx.astype(BF16)
    fox_w_in = jnp.transpose(fox_in8, (1, 0, 2)).reshape(D_MODEL, FOX_IN_COLS)
    width = HEADS * HEAD_DIM
    w4 = jnp.stack([fox_w_in[:, 0:width] * qk_scale, fox_w_in[:, width:2 * width],
                    fox_w_in[:, 2 * width:3 * width], fox_w_in[:, 3 * width:4 * width]])
    wf_b = jnp.pad(fox_w_in[:, 4 * width:], ((0, 0), (0, LANES - HEADS)))
    bf_pad = jnp.pad(b_f, ((0, 0), (0, LANES - HEADS)))
    fo_b = fox_out_b
    sel, bias = _selectors()

    xb, gate1, h0 = _lru_in_fwd(x, g0, w_in8, ts)
    y1, hs = _lru_core_fwd(xb, gate1, conv_w, conv_b, wa_b, ba, wx_b, bx, a_param, ts)
    x1, h1, f, cparts = _fox_pre_fwd(x, y1, w_out_b, g1, wf_b, bf_pad, ts)
    qkv = _fox_proj_fwd(h1, cparts, w4[0:3], sel, bias, BF16, ts, "fox_proj_qkv")
    gate2 = _fox_proj_fwd(h1, None, w4[3:4], None, None, F32, ts, "fox_proj_gate")[0]
    o, qb = _attn_fwd(qkv, blk)
    dx2, y2, loss_acc, g_final = _fox_out_loss(o, gate2, fo_b, x1, target, gf, ts)

    do, dgate2 = _fox_out_bwd(dx2, fo_b, o, gate2, ts)
    dq, dk, dv, dcum = _attn_bwd(qb, qkv, do, blk)
    dx1, dx1b, df, du4, g_norm1, g_bf = _fox_in_bwd(dq, dk, dv, dgate2, w4, wf_b, dcum, f, x1,
                                                    dx2, g1, ts)
    du, g_wa, g_wx, g_vec = _lru_core_bwd(dx1b, w_out_b, xb, gate1, hs, conv_w, conv_b, wa_b, ba,
                                          wx_b, bx, a_param, jnp.transpose(wa_b, (0, 2, 1)),
                                          jnp.transpose(wx_b, (0, 2, 1)), ts)
    grad_x, g_norm0 = _lru_in_bwd(du, w_in8, x, dx1, g0, ts)

    tw = 512
    g_lru_w_in = _weight_grad(h0, du, tw, "grad_lru_w_in", col_shards=N_DEV)
    g_lru_w_out = _weight_grad(y1, dx1b, tw, "grad_lru_w_out")
    g_q = _weight_grad(h1, du4[0], tw, "grad_fox_wq", scale=qk_scale)
    g_k = _weight_grad(h1, du4[1], tw, "grad_fox_wk")
    g_v = _weight_grad(h1, du4[2], tw, "grad_fox_wv")
    g_g = _weight_grad(h1, du4[3], tw, "grad_fox_wg")
    g_f = _weight_grad(h1, df, tw, "grad_fox_wf")
    g_fox_w_in = jnp.concatenate([g_q, g_k, g_v, g_g, g_f[:, :HEADS]], axis=1)
    g_fox_w_in = jnp.transpose(g_fox_w_in.reshape(D_MODEL, N_DEV, FOX_IN_SHARD), (1, 0, 2))
    g_fox_w_out = _weight_grad(y2, dx2.astype(BF16), tw, "grad_fox_w_out")

    grads = dict(
        norm_g=jnp.concatenate([g_norm0, g_norm1], axis=0), final_g=g_final[0],
        lru_w_in=g_lru_w_in, lru_conv_w=g_vec[0:4], lru_conv_b=g_vec[4:5], lru_wa=g_wa,
        lru_ba=g_vec[5:6], lru_wx=g_wx, lru_bx=g_vec[6:7], lru_a_param=g_vec[7:8],
        lru_w_out=g_lru_w_out, fox_w_in=g_fox_w_in, fox_b_f=g_bf[:, :HEADS],
        fox_w_out=g_fox_w_out)
    return loss_acc[0, 0], grad_x, grads


SMALL =("norm_g", "final_g", "lru_conv_b", "lru_wa", "lru_ba", "lru_wx", "lru_bx", "lru_a_param",
         "fox_b_f")
ALL_WEIGHTS = ("norm_g", "final_g", "lru_w_in", "lru_conv_w", "lru_conv_b", "lru_wa", "lru_ba",
               "lru_wx", "lru_bx", "lru_a_param", "lru_w_out", "fox_w_in", "fox_b_f", "fox_w_out")


def _pack_small(d):
    rows = []
    for n in SMALL:
        a = d[n].reshape(-1)
        if a.shape[0] % LANES:
            a = jnp.pad(a, (0, LANES - a.shape[0] % LANES))
        rows.append(a.reshape(-1, LANES))
    packed = jnp.concatenate(rows, axis=0)
    return jnp.pad(packed, ((0, N_DEV * SMALL_CHUNK_ROWS - packed.shape[0]), (0, 0)))


def _unpack_small(packed, like):
    out, off = {}, 0
    for n, nrows in zip(SMALL, SMALL_ROWS):
        size = like[n].size
        out[n] = packed[off:off + nrows].reshape(-1)[:size].reshape(like[n].shape)
        off += nrows
    return out


def kernel(x, norm_g, final_g, lru_w_in, lru_conv_w, lru_conv_b, lru_wa, lru_ba, lru_wx, lru_bx, lru_a_param, lru_w_out, fox_w_in, fox_b_f, fox_w_out, loss_target, m_norm_g, m_final_g, m_lru_w_in, m_lru_conv_w, m_lru_conv_b, m_lru_wa, m_lru_ba, m_lru_wx, m_lru_bx, m_lru_a_param, m_lru_w_out, m_fox_w_in, m_fox_b_f, m_fox_w_out, v_norm_g, v_final_g, v_lru_w_in, v_lru_conv_w, v_lru_conv_b, v_lru_wa, v_lru_ba, v_lru_wx, v_lru_bx, v_lru_a_param, v_lru_w_out, v_fox_w_in, v_fox_b_f, v_fox_w_out):
    w_loc = dict(norm_g=norm_g, final_g=final_g, lru_w_in=lru_w_in, lru_conv_w=lru_conv_w,
                 lru_conv_b=lru_conv_b, lru_wa=lru_wa, lru_ba=lru_ba, lru_wx=lru_wx, lru_bx=lru_bx,
                 lru_a_param=lru_a_param, lru_w_out=lru_w_out, fox_w_in=fox_w_in, fox_b_f=fox_b_f,
                 fox_w_out=fox_w_out)
    m_loc = dict(norm_g=m_norm_g, final_g=m_final_g, lru_w_in=m_lru_w_in, lru_conv_w=m_lru_conv_w,
                 lru_conv_b=m_lru_conv_b, lru_wa=m_lru_wa, lru_ba=m_lru_ba, lru_wx=m_lru_wx,
                 lru_bx=m_lru_bx, lru_a_param=m_lru_a_param, lru_w_out=m_lru_w_out,
                 fox_w_in=m_fox_w_in, fox_b_f=m_fox_b_f, fox_w_out=m_fox_w_out)
    v_loc = dict(norm_g=v_norm_g, final_g=v_final_g, lru_w_in=v_lru_w_in, lru_conv_w=v_lru_conv_w,
                 lru_conv_b=v_lru_conv_b, lru_wa=v_lru_wa, lru_ba=v_lru_ba, lru_wx=v_lru_wx,
                 lru_bx=v_lru_bx, lru_a_param=v_lru_a_param, lru_w_out=v_lru_w_out,
                 fox_w_in=v_fox_w_in, fox_b_f=v_fox_b_f, fox_w_out=v_fox_w_out)

    w_in8, conv8, w_out8, fox_in8, fox_out8 = _gather_two_level(
        [lru_w_in[0].astype(BF16), lru_conv_w[0], lru_w_out[0].astype(BF16),
         fox_w_in[0].astype(BF16), fox_w_out[0].astype(BF16)], "gather_weights")
    conv_full = jnp.transpose(conv8, (1, 0, 2)).reshape(CONV_WIDTH, LRU_WIDTH)

    loss, grad_x, grads = _local_step(
        x[0], loss_target[0], norm_g, final_g, w_in8, conv_full, lru_conv_b, lru_wa[0], lru_ba,
        lru_wx[0], lru_bx, lru_a_param, w_out8.reshape(LRU_WIDTH, D_MODEL), fox_in8, fox_b_f,
        fox_out8.reshape(HEADS * HEAD_DIM, D_MODEL))

    n_chips = N_DEV // 2
    conv_send = jnp.transpose(grads["lru_conv_w"].reshape(CONV_WIDTH, N_DEV, -1), (1, 0, 2))
    names = ("lru_w_in", "lru_conv_w", "lru_w_out", "fox_w_in", "fox_w_out", "small")
    send = [grads["lru_w_in"], conv_send, grads["lru_w_out"].reshape(N_DEV, -1, D_MODEL),
            grads["fox_w_in"], grads["fox_w_out"].reshape(N_DEV, -1, D_MODEL),
            _pack_small(grads).reshape(N_DEV, SMALL_CHUNK_ROWS, LANES)]
    send = [a.reshape((n_chips, 2) + a.shape[1:]) for a in send]
    got = _swap_sibling(send, "swap_grads")
    core = lax.axis_index("c").astype(jnp.int32).reshape(1)
    chip_sums = [_pair_sum(core, a, b, "pair_sum_" + n) for n, a, b in zip(names, send, got)]
    r_w_in, r_conv, r_w_out, r_fox_in, r_fox_out, r_small = _exchange_chips(
        chip_sums, "scatter_grads")

    out = {}
    for n, recv, tr in (("lru_w_in", r_w_in, 256), ("lru_conv_w", r_conv, CONV_WIDTH),
                        ("lru_w_out", r_w_out, 96), ("fox_w_in", r_fox_in, 128),
                        ("fox_w_out", r_fox_out, 64)):
        res = _adamw(recv, w_loc[n][0], m_loc[n][0], v_loc[n][0], tr, "adamw_" + n)
        out[n] = [a[None] for a in res]

    g_chunk = _reduce_parts(r_small, "reduce_small_grads")
    g_small, = _exchange([g_chunk], False, "gather_small_grads")
    g_small = g_small.reshape(1, N_DEV * SMALL_CHUNK_ROWS, LANES)
    res = _adamw(g_small, _pack_small(w_loc), _pack_small(m_loc), _pack_small(v_loc),
                 N_DEV * SMALL_CHUNK_ROWS, "adamw_replicated")
    small_out = [_unpack_small(a, w_loc) for a in res]
    for n in SMALL:
        out[n] = [d[n] for d in small_out]

    loss = lax.psum(loss, ("x", "y", "c"))
    return (loss, grad_x[None], *[out[n][0] for n in ALL_WEIGHTS], *[out[n][1] for n in ALL_WEIGHTS],
            *[out[n][2] for n in ALL_WEIGHTS], *[out[n][3] for n in ALL_WEIGHTS])
```

```python
import functools

import jax
import jax.numpy as jnp
from jax import lax
from jax.experimental import pallas as pl
from jax.experimental.pallas import tpu as pltpu

F32 = jnp.float32
BF16 = jnp.bfloat16

D_MODEL = 1024
LRU_WIDTH = 1536
LRU_BLOCKS = 12
LRU_BLOCK_W = 128
CONV_WIDTH = 4
LRU_C = 8.0
HEADS = 16
HEAD_DIM = 64
HEAD_PAD = 128
FOX_PAD = HEADS * HEAD_PAD
HEADS_PER_STEP = 2
EPS = 1e-6
NEG_BIG = -1e30
N_DEV = 8

ADAM_LR = 0.001
ADAM_B1 = 0.9
ADAM_B2 = 0.999
ADAM_EPS = 1e-08
ADAM_WD = 0.01
ADAM_STEP = 10

LANE_RB = 64
LANE_CK = 67
LANE_LSE = 70
LANE_ONE_V = 64

VMEM_LIMIT_BYTES = 56 * 1024 * 1024
LANES = 128
SUBLANES = 8

LRU_IN_SHARD = 2 * LRU_WIDTH // N_DEV
FOX_IN_COLS = 4 * HEADS * HEAD_DIM + HEADS
FOX_IN_SHARD = FOX_IN_COLS // N_DEV

SMALL_ROWS = (16, 8, 12, 1536, 12, 1536, 12, 12, 1)
SMALL_CHUNK_ROWS = 400
assert sum(SMALL_ROWS) <= N_DEV * SMALL_CHUNK_ROWS


def _params(n_grid_axes=1):
    return pltpu.CompilerParams(
        dimension_semantics=("arbitrary",) * n_grid_axes,
        vmem_limit_bytes=VMEM_LIMIT_BYTES)


def _const_spec(shape):
    nd = len(shape)
    return pl.BlockSpec(shape, lambda *_: (0,) * nd, pipeline_mode=pl.Buffered(1))


def _shift_down(x, k, fill):
    rows = lax.broadcasted_iota(jnp.int32, x.shape, 0)
    return jnp.where(rows >= k, pltpu.roll(x, k, 0), fill)


def _shift_up(x, k, fill):
    n = x.shape[0]
    rows = lax.broadcasted_iota(jnp.int32, x.shape, 0)
    return jnp.where(rows < n - k, pltpu.roll(x, n - k, 0), fill)


def _scan_rows(a, b, reverse=False):
    n = a.shape[0]
    shift = _shift_up if reverse else _shift_down
    k = 1
    while k < n:
        b = a * shift(b, k, 0.0) + b
        a = a * shift(a, k, 1.0)
        k *= 2
    return a, b


def _cumsum_rows(x, reverse=False):
    n = x.shape[0]
    shift = _shift_up if reverse else _shift_down
    k = 1
    while k < n:
        x = x + shift(x, k, 0.0)
        k *= 2
    return x


def _rstd(x):
    return lax.rsqrt(jnp.mean(x * x, axis=-1, keepdims=True) + EPS)


def _norm_bwd(x, g, dh):
    rstd = _rstd(x)
    xhat = x * rstd
    dg = jnp.sum(dh * xhat, axis=0, keepdims=True)
    dxh = dh * g
    dx = rstd * (dxh - xhat * jnp.mean(dxh * xhat, axis=-1, keepdims=True))
    return dx, dg


def _split3(x):
    hi = x.astype(BF16)
    r1 = x - hi.astype(F32)
    mid = r1.astype(BF16)
    lo = (r1 - mid.astype(F32)).astype(BF16)
    return hi, mid, lo


def _sigmoid(x):
    return jax.nn.sigmoid(x)


def _dot(a, b):
    return jnp.dot(a, b, preferred_element_type=F32)


def _dot_nt(a, b):
    return lax.dot_general(a, b, (((1,), (1,)), ((), ())), preferred_element_type=F32)


def _dot_tn(a, b):
    return lax.dot_general(a, b, (((0,), (0,)), ((), ())), preferred_element_type=F32)


def _heads_to_padded(u):
    n = u.shape[0]
    low = lax.broadcasted_iota(jnp.int32, (n, LANES), 1) < HEAD_DIM
    zero = jnp.zeros((n, LANES), u.dtype)
    cols = []
    for p in range(HEADS // 2):
        pair = u[:, p * LANES:(p + 1) * LANES]
        cols.append(jnp.where(low, pair, zero))
        cols.append(jnp.where(low, pltpu.roll(pair, HEAD_DIM, 1), zero))
    return jnp.concatenate(cols, axis=1)


def _heads_from_padded(x):
    n = x.shape[0]
    low = lax.broadcasted_iota(jnp.int32, (n, LANES), 1) < HEAD_DIM
    cols = []
    for p in range(HEADS // 2):
        even = x[:, (2 * p) * HEAD_PAD:(2 * p + 1) * HEAD_PAD]
        odd = x[:, (2 * p + 1) * HEAD_PAD:(2 * p + 2) * HEAD_PAD]
        cols.append(jnp.where(low, even, pltpu.roll(odd, HEAD_DIM, 1)))
    return jnp.concatenate(cols, axis=1)


def _conv_taps(xb, prev8):
    rows8 = lax.broadcasted_iota(jnp.int32, prev8.shape, 0)
    taps = [xb]
    for j in range(1, CONV_WIDTH):
        r = pltpu.roll(xb, j, 0)
        p = pltpu.roll(prev8, j, 0)
        head = jnp.where(rows8 < j, p, r[0:SUBLANES])
        taps.append(jnp.concatenate([head, r[SUBLANES:]], axis=0))
    return taps


def _lru_pre(taps, cw, cb, wa_ref, ba, wx_ref, bx, a_param):
    xc = cb + cw[3:4] * taps[0] + cw[2:3] * taps[1] + cw[1:2] * taps[2] + cw[0:1] * taps[3]
    xcb = xc.astype(BF16)
    ra, ia = [], []
    for n in range(LRU_BLOCKS):
        blk = xcb[:, n * LRU_BLOCK_W:(n + 1) * LRU_BLOCK_W]
        ra.append(_dot(blk, wa_ref[n]))
        ia.append(_dot(blk, wx_ref[n]))
    r = _sigmoid(jnp.concatenate(ra, axis=1) + ba)
    i = _sigmoid(jnp.concatenate(ia, axis=1) + bx)
    z = -a_param
    sp = jnp.maximum(z, 0.0) + jnp.log1p(jnp.exp(-jnp.abs(z)))
    log_a = (-LRU_C) * r * sp
    a = jnp.exp(log_a)
    one_minus_a2 = -jnp.tanh(log_a) * (a * a + 1.0)
    mult = jnp.sqrt(one_minus_a2)
    return xc, xcb, r, i, sp, a, mult


def _lru_in_fwd(x, g0, w_in, ts):
    s = x.shape[0]
    half = N_DEV // 2

    def body(x_ref, g_ref, w_ref, xb_ref, gate_ref, h_ref):
        xv = x_ref[...]
        h = (xv * _rstd(xv) * g_ref[...]).astype(BF16)
        u = [_dot(h, w_ref[j]) for j in range(N_DEV)]
        xb_ref[...] = jnp.concatenate(u[:half], axis=1)
        gate_ref[...] = jnp.concatenate(u[half:], axis=1)
        h_ref[...] = h

    return pl.pallas_call(
        body, name="lru_in_fwd", grid=(s // ts,),
        in_specs=[pl.BlockSpec((ts, D_MODEL), lambda i: (i, 0)),
                  _const_spec((1, D_MODEL)),
                  _const_spec((N_DEV, D_MODEL, LRU_IN_SHARD))],
        out_specs=[pl.BlockSpec((ts, LRU_WIDTH), lambda i: (i, 0)),
                   pl.BlockSpec((ts, LRU_WIDTH), lambda i: (i, 0)),
                   pl.BlockSpec((ts, D_MODEL), lambda i: (i, 0))],
        out_shape=[jax.ShapeDtypeStruct((s, LRU_WIDTH), F32),
                   jax.ShapeDtypeStruct((s, LRU_WIDTH), F32),
                   jax.ShapeDtypeStruct((s, D_MODEL), BF16)],
        compiler_params=_params(),
    )(x, g0, w_in)


def _lru_core_fwd(xb, gate, cw, cb, wa, ba, wx, bx, a_param, ts):
    s = xb.shape[0]

    def body(xb_ref, gate_ref, cw_ref, cb_ref, wa_ref, ba_ref, wx_ref, bx_ref, ap_ref,
             y_ref, hs_ref, prev_ref, hcar_ref):
        @pl.when(pl.program_id(0) == 0)
        def _():
            prev_ref[...] = jnp.zeros_like(prev_ref)
            hcar_ref[...] = jnp.zeros_like(hcar_ref)

        xbv = xb_ref[...]
        taps = _conv_taps(xbv, prev_ref[...])
        xc, _, _, i, _, a, mult = _lru_pre(taps, cw_ref[...], cb_ref[...], wa_ref, ba_ref[...],
                                           wx_ref, bx_ref[...], ap_ref[...])
        bterm = mult * (i * xc)
        cum_a, hloc = _scan_rows(a, bterm)
        hs = cum_a * hcar_ref[SUBLANES - 1:SUBLANES, :] + hloc
        gv = gate_ref[...]
        y_ref[...] = (hs * (gv * _sigmoid(gv))).astype(BF16)
        hs_ref[...] = hs
        prev_ref[...] = xbv[ts - SUBLANES:, :]
        hcar_ref[...] = hs[ts - SUBLANES:, :]

    vec = _const_spec((1, LRU_WIDTH))
    blk = _const_spec((LRU_BLOCKS, LRU_BLOCK_W, LRU_BLOCK_W))
    tile = pl.BlockSpec((ts, LRU_WIDTH), lambda i: (i, 0))
    return pl.pallas_call(
        body, name="lru_core_fwd", grid=(s // ts,),
        in_specs=[tile, tile, _const_spec((CONV_WIDTH, LRU_WIDTH)), vec, blk, vec, blk, vec, vec],
        out_specs=[tile, tile],
        out_shape=[jax.ShapeDtypeStruct((s, LRU_WIDTH), BF16),
                   jax.ShapeDtypeStruct((s, LRU_WIDTH), F32)],
        scratch_shapes=[pltpu.VMEM((SUBLANES, LRU_WIDTH), F32),
                        pltpu.VMEM((SUBLANES, LRU_WIDTH), F32)],
        compiler_params=_params(),
    )(xb, gate, cw, cb, wa, ba, wx, bx, a_param)


def _fox_pre_fwd(x, y, w_out, g1, wf, bf, ts):
    s = x.shape[0]

    def body(x_ref, y_ref, w_ref, g_ref, wf_ref, bf_ref, x1_ref, h1_ref, f_ref, cp_ref, ccar_ref):
        @pl.when(pl.program_id(0) == 0)
        def _():
            ccar_ref[...] = jnp.zeros_like(ccar_ref)

        x1 = x_ref[...] + _dot(y_ref[...], w_ref[...])
        h1 = (x1 * _rstd(x1) * g_ref[...]).astype(BF16)
        f = _dot(h1, wf_ref[...]) + bf_ref[...]
        logsig = jnp.minimum(f, 0.0) - jnp.log1p(jnp.exp(-jnp.abs(f)))
        cum = _cumsum_rows(logsig) + ccar_ref[SUBLANES - 1:SUBLANES, :]
        hi, mid, lo = _split3(cum)
        lane = lax.broadcasted_iota(jnp.int32, cum.shape, 1)
        packed = jnp.where(lane < HEADS, hi.astype(F32), jnp.where(
            lane < 2 * HEADS, pltpu.roll(mid.astype(F32), HEADS, 1), jnp.where(
                lane < 3 * HEADS, pltpu.roll(lo.astype(F32), 2 * HEADS, 1), 0.0)))
        x1_ref[...] = x1
        h1_ref[...] = h1
        f_ref[...] = f
        cp_ref[...] = packed.astype(BF16)
        ccar_ref[...] = cum[ts - SUBLANES:, :]

    return pl.pallas_call(
        body, name="fox_pre_fwd", grid=(s // ts,),
        in_specs=[pl.BlockSpec((ts, D_MODEL), lambda i: (i, 0)),
                  pl.BlockSpec((ts, LRU_WIDTH), lambda i: (i, 0)),
                  _const_spec((LRU_WIDTH, D_MODEL)),
                  _const_spec((1, D_MODEL)),
                  _const_spec((D_MODEL, LANES)),
                  _const_spec((1, LANES))],
        out_specs=[pl.BlockSpec((ts, D_MODEL), lambda i: (i, 0)),
                   pl.BlockSpec((ts, D_MODEL), lambda i: (i, 0)),
                   pl.BlockSpec((ts, LANES), lambda i: (i, 0)),
                   pl.BlockSpec((ts, LANES), lambda i: (i, 0))],
        out_shape=[jax.ShapeDtypeStruct((s, D_MODEL), F32),
                   jax.ShapeDtypeStruct((s, D_MODEL), BF16),
                   jax.ShapeDtypeStruct((s, LANES), F32),
                   jax.ShapeDtypeStruct((s, LANES), BF16)],
        scratch_shapes=[pltpu.VMEM((SUBLANES, LANES), F32)],
        compiler_params=_params(),
    )(x, y, w_out, g1, wf, bf)


def _fox_proj_fwd(h1, cparts, w, sel, bias, out_dtype, ts, name):
    s = h1.shape[0]
    ng = w.shape[0]
    width = HEADS * HEAD_DIM
    use_sel = sel is not None

    def body(*refs):
        if use_sel:
            h_ref, cp_ref, w_ref, sel_ref, b_ref, o_ref = refs
            acc = (_heads_to_padded(_dot(h_ref[...], w_ref[...]))
                   + _dot(cp_ref[...], sel_ref[...]) + b_ref[...])
        else:
            h_ref, w_ref, o_ref = refs
            acc = _heads_to_padded(_dot(h_ref[...], w_ref[...]))
        o_ref[...] = acc.astype(out_dtype)

    in_specs = [pl.BlockSpec((ts, D_MODEL), lambda j, i: (i, 0))]
    args = [h1]
    if use_sel:
        in_specs.append(pl.BlockSpec((ts, LANES), lambda j, i: (i, 0)))
        args.append(cparts)
    in_specs.append(pl.BlockSpec((None, D_MODEL, width), lambda j, i: (j, 0, 0)))
    args.append(w)
    if use_sel:
        in_specs.append(pl.BlockSpec((None, LANES, FOX_PAD), lambda j, i: (j, 0, 0)))
        in_specs.append(pl.BlockSpec((None, 1, FOX_PAD), lambda j, i: (j, 0, 0)))
        args += [sel, bias]
    return pl.pallas_call(
        body, name=name, grid=(ng, s // ts),
        in_specs=in_specs,
        out_specs=pl.BlockSpec((None, ts, FOX_PAD), lambda j, i: (j, i, 0)),
        out_shape=jax.ShapeDtypeStruct((ng, s, FOX_PAD), out_dtype),
        compiler_params=_params(2),
    )(*args)


def _attn_fwd(qkv, blk, hps=HEADS_PER_STEP):
    s = qkv.shape[1]
    nblk = s // blk
    wide = 2 * blk
    heads = [slice(i * HEAD_PAD, (i + 1) * HEAD_PAD) for i in range(hps)]

    def body(q_ref, k_ref, v_ref, o_ref, qb_ref, acc_ref, m_ref):
        qi = pl.program_id(1)
        row = lax.broadcasted_iota(jnp.int32, (blk, blk), 0)
        col = lax.broadcasted_iota(jnp.int32, (blk, blk), 1)
        lane = lax.broadcasted_iota(jnp.int32, (blk, HEAD_PAD), 1)
        qs = [q_ref[:, hd] for hd in heads]
        for i in range(hps):
            acc_ref[i] = jnp.zeros((blk, HEAD_PAD), F32)
            m_ref[i] = jnp.full((blk, HEAD_PAD), NEG_BIG, F32)

        def step(k0, size, masked):
            scores = [_dot_nt(q, k_ref[pl.ds(k0, size), hd]) for q, hd in zip(qs, heads)]
            for i, (sc, hd) in enumerate(zip(scores, heads)):
                v = v_ref[pl.ds(k0, size), hd]
                if masked:
                    sc = jnp.where(col <= row, sc, NEG_BIG)
                m = m_ref[i]
                m_new = jnp.maximum(m, jnp.max(sc, axis=-1, keepdims=True))
                p = jnp.exp((sc - jnp.tile(m_new, (1, size // HEAD_PAD))).astype(BF16))
                acc_ref[i] = jnp.exp(m - m_new) * acc_ref[i] + _dot(p, v)
                m_ref[i] = m_new

        def wide_step(kk, _):
            step(pl.multiple_of(kk * wide, wide), wide, False)
            return 0

        lax.fori_loop(0, qi // 2, wide_step, 0)

        @pl.when(qi % 2 == 1)
        def _():
            step(pl.multiple_of((qi - 1) * blk, blk), blk, False)

        step(pl.multiple_of(qi * blk, blk), blk, True)
        for i, (q, hd) in enumerate(zip(qs, heads)):
            acc = acc_ref[i]
            l = jnp.broadcast_to(acc[:, LANE_ONE_V:LANE_ONE_V + 1], (blk, HEAD_PAD))
            o_ref[:, hd] = (acc / l).astype(BF16)
            hi, mid, lo = _split3(-(m_ref[i] + jnp.log(l)))
            qb_ref[:, hd] = jnp.where(lane == LANE_LSE, hi, jnp.where(
                lane == LANE_LSE + 1, mid, jnp.where(lane == LANE_LSE + 2, lo, q)))

    width = hps * HEAD_PAD

    def whole(j):
        return pl.BlockSpec((None, s, width), lambda h, i: (j, 0, h))

    out_spec = pl.BlockSpec((blk, width), lambda h, i: (i, h))
    return pl.pallas_call(
        body, name="attn_fwd", grid=(HEADS // hps, nblk),
        in_specs=[pl.BlockSpec((None, blk, width), lambda h, i: (0, i, h)), whole(1), whole(2)],
        out_specs=[out_spec, out_spec],
        out_shape=[jax.ShapeDtypeStruct((s, FOX_PAD), BF16),
                   jax.ShapeDtypeStruct((s, FOX_PAD), BF16)],
        scratch_shapes=[pltpu.VMEM((hps, blk, HEAD_PAD), F32),
                        pltpu.VMEM((hps, blk, HEAD_PAD), F32)],
        compiler_params=_params(2),
    )(qkv, qkv, qkv)


def _fox_out_loss(o, gate, w_out, x1, target, gf, ts):
    s = x1.shape[0]

    def body(o_ref, gt_ref, w_ref, x1_ref, t_ref, g_ref, dx2_ref, dx2b_ref, y2_ref, loss_ref,
             gfin_ref):
        @pl.when(pl.program_id(0) == 0)
        def _():
            loss_ref[...] = jnp.zeros_like(loss_ref)
            gfin_ref[...] = jnp.zeros_like(gfin_ref)

        gv = gt_ref[...]
        y2 = _heads_from_padded(o_ref[...] * (gv * _sigmoid(gv))).astype(BF16)
        x2 = x1_ref[...] + _dot(y2, w_ref[...])
        rstd = _rstd(x2)
        xhat = x2 * rstd
        g = g_ref[...]
        diff = xhat * g - t_ref[...]
        loss_ref[...] += 0.5 * jnp.sum(jnp.mean(diff * diff, axis=-1, keepdims=True))
        dy = diff * (1.0 / D_MODEL)
        gfin_ref[...] += jnp.sum(dy * xhat, axis=0, keepdims=True)
        dxh = dy * g
        dx2 = rstd * (dxh - xhat * jnp.mean(dxh * xhat, axis=-1, keepdims=True))
        dx2_ref[...] = dx2
        dx2b_ref[...] = dx2.astype(BF16)
        y2_ref[...] = y2

    return pl.pallas_call(
        body, name="fox_out_loss", grid=(s // ts,),
        in_specs=[pl.BlockSpec((ts, FOX_PAD), lambda i: (i, 0)),
                  pl.BlockSpec((ts, FOX_PAD), lambda i: (i, 0)),
                  _const_spec((HEADS * HEAD_DIM, D_MODEL)),
                  pl.BlockSpec((ts, D_MODEL), lambda i: (i, 0)),
                  pl.BlockSpec((ts, D_MODEL), lambda i: (i, 0)),
                  _const_spec((1, D_MODEL))],
        out_specs=[pl.BlockSpec((ts, D_MODEL), lambda i: (i, 0)),
                   pl.BlockSpec((ts, D_MODEL), lambda i: (i, 0)),
                   pl.BlockSpec((ts, HEADS * HEAD_DIM), lambda i: (i, 0)),
                   pl.BlockSpec((SUBLANES, LANES), lambda i: (0, 0)),
                   pl.BlockSpec((1, D_MODEL), lambda i: (0, 0))],
        out_shape=[jax.ShapeDtypeStruct((s, D_MODEL), F32),
                   jax.ShapeDtypeStruct((s, D_MODEL), BF16),
                   jax.ShapeDtypeStruct((s, HEADS * HEAD_DIM), BF16),
                   jax.ShapeDtypeStruct((SUBLANES, LANES), F32),
                   jax.ShapeDtypeStruct((1, D_MODEL), F32)],
        compiler_params=_params(),
    )(o, gate, w_out, x1, target, gf)


def _fox_out_bwd(dx2, w_out, o, gate, ts):
    s = dx2.shape[0]

    def body(dx_ref, w_ref, o_ref, gt_ref, do_ref, dg_ref):
        lane = lax.broadcasted_iota(jnp.int32, (ts, HEAD_PAD), 1)
        dy2 = _heads_to_padded(_dot_nt(dx_ref[...], w_ref[...]))
        gv = gt_ref[...]
        sg = _sigmoid(gv)
        ov = o_ref[...]
        dov = dy2 * (gv * sg)
        dg_ref[...] = (dy2 * ov * (sg * (1.0 + gv * (1.0 - sg)))).astype(BF16)
        prod = dov * ov
        for h in range(HEADS):
            sl = slice(h * HEAD_PAD, (h + 1) * HEAD_PAD)
            delta = jnp.sum(prod[:, sl], axis=-1, keepdims=True)
            hi = delta.astype(BF16)
            lo = (delta - hi.astype(F32)).astype(BF16)
            do_h = dov[:, sl].astype(BF16)
            do_ref[:, sl] = jnp.where(lane == LANE_ONE_V, -hi,
                                      jnp.where(lane == LANE_ONE_V + 1, -lo, do_h))

    tile = pl.BlockSpec((ts, FOX_PAD), lambda i: (i, 0))
    return pl.pallas_call(
        body, name="fox_out_bwd", grid=(s // ts,),
        in_specs=[pl.BlockSpec((ts, D_MODEL), lambda i: (i, 0)),
                  _const_spec((HEADS * HEAD_DIM, D_MODEL)), tile, tile],
        out_specs=[tile, tile],
        out_shape=[jax.ShapeDtypeStruct((s, FOX_PAD), BF16),
                   jax.ShapeDtypeStruct((s, FOX_PAD), BF16)],
        compiler_params=_params(),
    )(dx2, w_out, o, gate)


def _attn_bwd(qb, qkv, do, blk):
    s = qb.shape[0]
    nblk = s // blk
    heads = [slice(i * HEAD_PAD, (i + 1) * HEAD_PAD) for i in range(HEADS_PER_STEP)]

    def body(q_ref, k_ref, v_ref, do_ref, dq_ref, dk_ref, dv_ref, dcum_ref, dq_acc, dkt_acc,
             dvt_acc, qt_ref, dot_ref):
        group = pl.program_id(0)
        kj = pl.program_id(1)
        row = lax.broadcasted_iota(jnp.int32, (blk, blk), 0)
        col = lax.broadcasted_iota(jnp.int32, (blk, blk), 1)
        lane = lax.broadcasted_iota(jnp.int32, (blk, LANES), 1)
        mine = [lane == group * HEADS_PER_STEP + i for i in range(HEADS_PER_STEP)]

        @pl.when(kj == 0)
        def _():
            dq_acc[...] = jnp.zeros_like(dq_acc)

            def transpose_block(bi, _):
                r0 = pl.multiple_of(bi * blk, blk)
                for i, hd in enumerate(heads):
                    qt_ref[i, bi] = q_ref[pl.ds(r0, blk), hd].T
                    dot_ref[i, bi] = do_ref[pl.ds(r0, blk), hd].T
                return 0

            lax.fori_loop(0, nblk, transpose_block, 0)

        @pl.when((group == 0) & (kj == 0))
        def _():
            dcum_ref[...] = jnp.zeros_like(dcum_ref)

        k0 = pl.multiple_of(kj * blk, blk)
        ks = [k_ref[:, hd] for hd in heads]
        vs = [v_ref[:, hd] for hd in heads]

        dkt_acc[...] = jnp.zeros_like(dkt_acc)
        dvt_acc[...] = jnp.zeros_like(dvt_acc)

        def step(qi, masked):
            q0 = pl.multiple_of(qi * blk, blk)
            qs = [q_ref[pl.ds(q0, blk), hd] for hd in heads]
            dos = [do_ref[pl.ds(q0, blk), hd] for hd in heads]
            scores = [_dot_nt(q, k) for q, k in zip(qs, ks)]
            dps = [_dot_nt(dov, v) for dov, v in zip(dos, vs)]
            for i, (hd, k, sc, dp) in enumerate(zip(heads, ks, scores, dps)):
                p = jnp.exp(sc.astype(BF16))
                if masked:
                    p = jnp.where(col <= row, p, jnp.zeros_like(p))
                ds = (p.astype(F32) * dp).astype(BF16)
                dvt_acc[i] += _dot(dot_ref[i, qi], p)
                dkt_acc[i] += _dot(qt_ref[i, qi], ds)
                dq_acc[pl.ds(q0, blk), hd] += _dot(ds, k)

        step(kj, True)

        def q_step(qi, _):
            step(qi, False)
            return 0

        lax.fori_loop(kj + 1, nblk, q_step, 0)
        dcum = dcum_ref[pl.ds(k0, blk), :]
        for i, (hd, mask) in enumerate(zip(heads, mine)):
            dk = dkt_acc[i].T
            dk_ref[:, hd] = dk.astype(BF16)
            dv_ref[:, hd] = dvt_acc[i].T.astype(BF16)
            dcum = jnp.where(mask, -dk[:, LANE_CK:LANE_CK + 1], dcum)
        dcum_ref[pl.ds(k0, blk), :] = dcum

        @pl.when(kj == nblk - 1)
        def _():
            def finish(bi, _):
                r0 = pl.multiple_of(bi * blk, blk)
                dcum = dcum_ref[pl.ds(r0, blk), :]
                for hd, mask in zip(heads, mine):
                    dq = dq_acc[pl.ds(r0, blk), hd]
                    dq_ref[pl.ds(r0, blk), hd] = dq.astype(BF16)
                    dcum = dcum + jnp.where(mask, dq[:, LANE_RB:LANE_RB + 1], 0.0)
                dcum_ref[pl.ds(r0, blk), :] = dcum
                return 0

            lax.fori_loop(0, nblk, finish, 0)

    width = HEADS_PER_STEP * HEAD_PAD
    whole = pl.BlockSpec((s, width), lambda h, j: (0, h))
    whole_in = pl.BlockSpec((s, width), lambda h, j: (0, h), pipeline_mode=pl.Buffered(1))
    part = pl.BlockSpec((blk, width), lambda h, j: (j, h))
    out = jax.ShapeDtypeStruct((s, FOX_PAD), BF16)
    return pl.pallas_call(
        body, name="attn_bwd", grid=(HEADS // HEADS_PER_STEP, nblk),
        in_specs=[whole_in,
                  pl.BlockSpec((None, blk, width), lambda h, j: (1, j, h)),
                  pl.BlockSpec((None, blk, width), lambda h, j: (2, j, h)),
                  whole_in],
        out_specs=[whole, part, part, pl.BlockSpec((s, LANES), lambda h, j: (0, 0))],
        out_shape=[out, out, out, jax.ShapeDtypeStruct((s, LANES), F32)],
        scratch_shapes=[pltpu.VMEM((s, width), F32),
                        pltpu.VMEM((HEADS_PER_STEP, HEAD_PAD, blk), F32),
                        pltpu.VMEM((HEADS_PER_STEP, HEAD_PAD, blk), F32),
                        pltpu.VMEM((HEADS_PER_STEP, nblk, HEAD_PAD, blk), BF16),
                        pltpu.VMEM((HEADS_PER_STEP, nblk, HEAD_PAD, blk), BF16)],
        compiler_params=_params(2),
    )(qb, qkv, qkv, do)


def _fox_in_bwd(dq, dk, dv, dg, wt, wft, dcum, f, x1, dx2, g1, ts):
    s = x1.shape[0]
    nt = s // ts
    width = HEADS * HEAD_DIM

    def body(dq_ref, dk_ref, dv_ref, dg_ref, wt_ref, wft_ref, dcum_ref, f_ref, x1_ref, dx2_ref,
             g_ref, dx1_ref, dx1b_ref, df_ref, du_ref, gn_ref, gbf_ref, rcar_ref):
        @pl.when(pl.program_id(0) == 0)
        def _():
            rcar_ref[...] = jnp.zeros_like(rcar_ref)
            gn_ref[...] = jnp.zeros_like(gn_ref)
            gbf_ref[...] = jnp.zeros_like(gbf_ref)

        rsum = _cumsum_rows(dcum_ref[...], reverse=True) + rcar_ref[0:1, :]
        df = rsum * _sigmoid(-f_ref[...])
        dfb = df.astype(BF16)
        dh = _dot_nt(dfb, wft_ref[...])
        for j, ref in enumerate((dq_ref, dk_ref, dv_ref, dg_ref)):
            du = _heads_from_padded(ref[...])
            du_ref[j] = du
            dh = dh + _dot_nt(du, wt_ref[j])
        dxn, dgn = _norm_bwd(x1_ref[...], g_ref[...], dh)
        dx1 = dx2_ref[...] + dxn
        dx1_ref[...] = dx1
        dx1b_ref[...] = dx1.astype(BF16)
        df_ref[...] = dfb
        gn_ref[...] += dgn
        gbf_ref[...] += jnp.sum(df, axis=0, keepdims=True)
        rcar_ref[...] = rsum[0:SUBLANES, :]

    rev = lambda i: (nt - 1 - i, 0)
    wide = pl.BlockSpec((ts, FOX_PAD), rev)
    return pl.pallas_call(
        body, name="fox_in_bwd", grid=(nt,),
        in_specs=[wide, wide, wide, wide,
                  _const_spec((4, D_MODEL, width)),
                  _const_spec((D_MODEL, LANES)),
                  pl.BlockSpec((ts, LANES), rev),
                  pl.BlockSpec((ts, LANES), rev),
                  pl.BlockSpec((ts, D_MODEL), rev),
                  pl.BlockSpec((ts, D_MODEL), rev),
                  _const_spec((1, D_MODEL))],
        out_specs=[pl.BlockSpec((ts, D_MODEL), rev),
                   pl.BlockSpec((ts, D_MODEL), rev),
                   pl.BlockSpec((ts, LANES), rev),
                   pl.BlockSpec((4, ts, width), lambda i: (0, nt - 1 - i, 0)),
                   pl.BlockSpec((1, D_MODEL), lambda i: (0, 0)),
                   pl.BlockSpec((1, LANES), lambda i: (0, 0))],
        out_shape=[jax.ShapeDtypeStruct((s, D_MODEL), F32),
                   jax.ShapeDtypeStruct((s, D_MODEL), BF16),
                   jax.ShapeDtypeStruct((s, LANES), BF16),
                   jax.ShapeDtypeStruct((4, s, width), BF16),
                   jax.ShapeDtypeStruct((1, D_MODEL), F32),
                   jax.ShapeDtypeStruct((1, LANES), F32)],
        scratch_shapes=[pltpu.VMEM((SUBLANES, LANES), F32)],
        compiler_params=_params(),
    )(dq, dk, dv, dg, wt, wft, dcum, f, x1, dx2, g1)


def _lru_core_bwd(dx1b, w_out, xb, gate, hs, cw, cb, wa, ba, wx, bx, a_param, wa_t, wx_t, ts):
    s = xb.shape[0]
    nt = s // ts
    tpb = ts // SUBLANES

    def body(dx_ref, wo_ref, xb_ref, xbh_ref, gate_ref, hs_ref, hsh_ref, cw_ref, cb_ref, wa_ref,
             ba_ref, wx_ref, bx_ref, ap_ref, wat_ref, wxt_ref,
             du_ref, gwa_ref, gwx_ref, gvec_ref, acar_ref, dhcar_ref, dxccar_ref):
        step = pl.program_id(0)

        @pl.when(step == 0)
        def _():
            acar_ref[...] = jnp.zeros_like(acar_ref)
            dhcar_ref[...] = jnp.zeros_like(dhcar_ref)
            dxccar_ref[...] = jnp.zeros_like(dxccar_ref)
            gwa_ref[...] = jnp.zeros_like(gwa_ref)
            gwx_ref[...] = jnp.zeros_like(gwx_ref)
            gvec_ref[...] = jnp.zeros_like(gvec_ref)

        first_tile = step == nt - 1
        halo_on = jnp.where(first_tile, 0.0, 1.0)
        prev8 = xbh_ref[...] * halo_on
        hprev_row = hsh_ref[SUBLANES - 1:SUBLANES, :] * halo_on

        xbv = xb_ref[...]
        taps = _conv_taps(xbv, prev8)
        cw_v = cw_ref[...]
        xc, xcb, r, i, sp, a, mult = _lru_pre(taps, cw_v, cb_ref[...], wa_ref, ba_ref[...],
                                              wx_ref, bx_ref[...], ap_ref[...])
        hs = hs_ref[...]
        gv = gate_ref[...]
        sg = _sigmoid(gv)
        dy = _dot_nt(dx_ref[...], wo_ref[...])
        dhs = dy * (gv * sg)
        dgate = dy * hs * (sg * (1.0 + gv * (1.0 - sg)))

        rows = lax.broadcasted_iota(jnp.int32, a.shape, 0)
        a_next = jnp.where(rows < ts - 1, pltpu.roll(a, ts - 1, 0), acar_ref[0:1, :])
        cum_a, dh_loc = _scan_rows(a_next, dhs, reverse=True)
        dh = cum_a * dhcar_ref[0:1, :] + dh_loc
        h_prev = jnp.where(rows >= 1, pltpu.roll(hs, 1, 0), hprev_row)

        da = dh * h_prev
        ixc = i * xc
        dmult = dh * ixc
        di = dh * mult * xc
        dxc = dh * mult * i
        dlog_a = da * a - dmult * (a * a) / mult
        dr = dlog_a * ((-LRU_C) * sp)
        dsp = jnp.sum(dlog_a * ((-LRU_C) * r), axis=0, keepdims=True)
        dra = dr * r * (1.0 - r)
        dia = di * i * (1.0 - i)
        drab = dra.astype(BF16)
        diab = dia.astype(BF16)
        back = []
        for n in range(LRU_BLOCKS):
            sl = slice(n * LRU_BLOCK_W, (n + 1) * LRU_BLOCK_W)
            gwa_ref[n] += _dot_tn(xcb[:, sl], drab[:, sl])
            gwx_ref[n] += _dot_tn(xcb[:, sl], diab[:, sl])
            back.append(_dot(drab[:, sl], wat_ref[n]) + _dot(diab[:, sl], wxt_ref[n]))
        dxc = dxc + jnp.concatenate(back, axis=1)

        nxt8 = dxccar_ref[...]
        rows8 = lax.broadcasted_iota(jnp.int32, nxt8.shape, 0)
        dxb = cw_v[3:4] * dxc
        for j in range(1, CONV_WIDTH):
            rj = pltpu.roll(dxc, ts - j, 0)
            pj = pltpu.roll(nxt8, SUBLANES - j, 0)
            tail = jnp.where(rows8 >= SUBLANES - j, pj, rj[ts - SUBLANES:])
            dxb = dxb + cw_v[3 - j:4 - j] * jnp.concatenate([rj[:ts - SUBLANES], tail], axis=0)

        du_ref[:, :LRU_WIDTH] = dxb.astype(BF16)
        du_ref[:, LRU_WIDTH:] = dgate.astype(BF16)

        z = -ap_ref[...]
        gvec = [jnp.sum(dxc * taps[3 - k], axis=0, keepdims=True) for k in range(CONV_WIDTH)]
        gvec.append(jnp.sum(dxc, axis=0, keepdims=True))
        gvec.append(jnp.sum(dra, axis=0, keepdims=True))
        gvec.append(jnp.sum(dia, axis=0, keepdims=True))
        gvec.append(-dsp * _sigmoid(z))
        gvec_ref[...] += jnp.concatenate(gvec, axis=0)

        acar_ref[...] = a[0:SUBLANES, :]
        dhcar_ref[...] = dh[0:SUBLANES, :]
        dxccar_ref[...] = dxc[0:SUBLANES, :]

    rev = lambda i: (nt - 1 - i, 0)
    halo = lambda i: (jnp.maximum((nt - 1 - i) * tpb - 1, 0), 0)
    tile = pl.BlockSpec((ts, LRU_WIDTH), rev)
    halo_spec = pl.BlockSpec((SUBLANES, LRU_WIDTH), halo)
    vec = _const_spec((1, LRU_WIDTH))
    blk = _const_spec((LRU_BLOCKS, LRU_BLOCK_W, LRU_BLOCK_W))
    acc_blk = pl.BlockSpec((LRU_BLOCKS, LRU_BLOCK_W, LRU_BLOCK_W), lambda i: (0, 0, 0))
    return pl.pallas_call(
        body, name="lru_core_bwd", grid=(nt,),
        in_specs=[pl.BlockSpec((ts, D_MODEL), rev),
                  _const_spec((LRU_WIDTH, D_MODEL)),
                  tile, halo_spec, tile, tile, halo_spec,
                  _const_spec((CONV_WIDTH, LRU_WIDTH)), vec, blk, vec, blk, vec, vec, blk, blk],
        out_specs=[pl.BlockSpec((ts, 2 * LRU_WIDTH), rev), acc_blk, acc_blk,
                   pl.BlockSpec((SUBLANES, LRU_WIDTH), lambda i: (0, 0))],
        out_shape=[jax.ShapeDtypeStruct((s, 2 * LRU_WIDTH), BF16),
                   jax.ShapeDtypeStruct((LRU_BLOCKS, LRU_BLOCK_W, LRU_BLOCK_W), F32),
                   jax.ShapeDtypeStruct((LRU_BLOCKS, LRU_BLOCK_W, LRU_BLOCK_W), F32),
                   jax.ShapeDtypeStruct((SUBLANES, LRU_WIDTH), F32)],
        scratch_shapes=[pltpu.VMEM((SUBLANES, LRU_WIDTH), F32),
                        pltpu.VMEM((SUBLANES, LRU_WIDTH), F32),
                        pltpu.VMEM((SUBLANES, LRU_WIDTH), F32)],
        compiler_params=_params(),
    )(dx1b, w_out, xb, xb, gate, hs, hs, cw, cb, wa, ba, wx, bx, a_param, wa_t, wx_t)


def _lru_in_bwd(du, w_in, x, dx1, g0, ts):
    s = x.shape[0]

    def body(du_ref, w_ref, x_ref, dx1_ref, g_ref, gx_ref, gn_ref):
        @pl.when(pl.program_id(0) == 0)
        def _():
            gn_ref[...] = jnp.zeros_like(gn_ref)

        duv = du_ref[...]
        dh = _dot_nt(duv[:, 0:LRU_IN_SHARD], w_ref[0])
        for j in range(1, N_DEV):
            dh = dh + _dot_nt(duv[:, j * LRU_IN_SHARD:(j + 1) * LRU_IN_SHARD], w_ref[j])
        dxn, dgn = _norm_bwd(x_ref[...], g_ref[...], dh)
        gx_ref[...] = dx1_ref[...] + dxn
        gn_ref[...] += dgn

    tile = pl.BlockSpec((ts, D_MODEL), lambda i: (i, 0))
    return pl.pallas_call(
        body, name="lru_in_bwd", grid=(s // ts,),
        in_specs=[pl.BlockSpec((ts, 2 * LRU_WIDTH), lambda i: (i, 0)),
                  _const_spec((N_DEV, D_MODEL, LRU_IN_SHARD)), tile, tile,
                  _const_spec((1, D_MODEL))],
        out_specs=[tile, pl.BlockSpec((1, D_MODEL), lambda i: (0, 0))],
        out_shape=[jax.ShapeDtypeStruct((s, D_MODEL), F32),
                   jax.ShapeDtypeStruct((1, D_MODEL), F32)],
        compiler_params=_params(),
    )(du, w_in, x, dx1, g0)


def _weight_grad(a, b, ts, name, scale=1.0, col_shards=1):
    s, ka = a.shape
    nb = b.shape[1]
    nt = s // ts
    per = nb // col_shards

    def body(a_ref, b_ref, o_ref):
        @pl.when(pl.program_id(0) == 0)
        def _():
            o_ref[...] = jnp.zeros_like(o_ref)

        if col_shards == 1:
            o_ref[...] += _dot_tn(a_ref[...], b_ref[...])
        else:
            av, bv = a_ref[...], b_ref[...]
            for j in range(col_shards):
                o_ref[j] += _dot_tn(av, bv[:, j * per:(j + 1) * per])
        if scale != 1.0:
            @pl.when(pl.program_id(0) == nt - 1)
            def _():
                o_ref[...] = o_ref[...] * scale

    out_dims = (ka, nb) if col_shards == 1 else (col_shards, ka, per)
    return pl.pallas_call(
        body, name=name, grid=(nt,),
        in_specs=[pl.BlockSpec((ts, ka), lambda i: (i, 0)),
                  pl.BlockSpec((ts, nb), lambda i: (i, 0))],
        out_specs=pl.BlockSpec(out_dims, lambda i: (0,) * len(out_dims)),
        out_shape=jax.ShapeDtypeStruct(out_dims, F32),
        compiler_params=_params(),
    )(a, b)


def _sum_parts(gp_ref):
    g = gp_ref[0].astype(F32)
    for k in range(1, gp_ref.shape[0]):
        g = g + gp_ref[k].astype(F32)
    return g


def _adamw(g_parts, w, m, v, tr, name):
    nparts, rows, cols = g_parts.shape

    def body(gp_ref, w_ref, m_ref, v_ref, g_ref, d_ref, mo_ref, vo_ref):
        g = _sum_parts(gp_ref)
        m2 = ADAM_B1 * m_ref[...] + (1.0 - ADAM_B1) * g
        v2 = ADAM_B2 * v_ref[...] + (1.0 - ADAM_B2) * (g * g)
        m_hat = m2 / (1.0 - ADAM_B1 ** ADAM_STEP)
        v_hat = v2 / (1.0 - ADAM_B2 ** ADAM_STEP)
        g_ref[...] = g
        d_ref[...] = (-ADAM_LR) * (m_hat / (jnp.sqrt(v_hat) + ADAM_EPS) + ADAM_WD * w_ref[...])
        mo_ref[...] = m2
        vo_ref[...] = v2

    tile = pl.BlockSpec((tr, cols), lambda i: (i, 0))
    out = jax.ShapeDtypeStruct((rows, cols), F32)
    return pl.pallas_call(
        body, name=name, grid=(rows // tr,),
        in_specs=[pl.BlockSpec((nparts, tr, cols), lambda i: (0, i, 0)), tile, tile, tile],
        out_specs=[tile, tile, tile, tile],
        out_shape=[out, out, out, out],
        compiler_params=_params(),
    )(g_parts, w, m, v)


def _reduce_parts(g_parts, name):
    _, rows, cols = g_parts.shape

    def body(gp_ref, g_ref):
        g_ref[...] = _sum_parts(gp_ref)

    return pl.pallas_call(
        body, name=name,
        out_shape=jax.ShapeDtypeStruct((rows, cols), F32),
        compiler_params=pltpu.CompilerParams(vmem_limit_bytes=VMEM_LIMIT_BYTES),
    )(g_parts)


def _mesh_pos():
    ix, iy, ic = lax.axis_index("x"), lax.axis_index("y"), lax.axis_index("c")
    return ix, iy, ic


def _peer(ix, iy, ic, mask):
    px = 1 - ix if mask & 4 else ix
    py = 1 - iy if mask & 2 else iy
    pc = 1 - ic if mask & 1 else ic
    return (px, py, pc), 4 * px + 2 * py + pc


def _exchange(arrays, scatter, name):
    n = len(arrays)

    def body(*refs):
        x_refs, o_refs = refs[:n], refs[n:2 * n]
        send_sems, recv_sems, local_sems = refs[2 * n:]
        ix, iy, ic = _mesh_pos()
        me = 4 * ix + 2 * iy + ic

        def src(a, dest):
            return x_refs[a].at[dest] if scatter else x_refs[a]

        local = [pltpu.make_async_copy(src(a, me), o_refs[a].at[me], local_sems.at[a])
                 for a in range(n)]
        for cp in local:
            cp.start()
        sends = []
        for mask in range(1, N_DEV):
            peer, pidx = _peer(ix, iy, ic, mask)
            for a in range(n):
                cp = pltpu.make_async_remote_copy(
                    src_ref=src(a, pidx), dst_ref=o_refs[a].at[me],
                    send_sem=send_sems.at[a, mask - 1], recv_sem=recv_sems.at[a, mask - 1],
                    device_id=peer, device_id_type=pl.DeviceIdType.MESH)
                cp.start()
                sends.append(cp)
        for mask in range(1, N_DEV):
            peer, pidx = _peer(ix, iy, ic, mask)
            for a in range(n):
                pltpu.make_async_remote_copy(
                    src_ref=src(a, me), dst_ref=o_refs[a].at[pidx],
                    send_sem=send_sems.at[a, mask - 1], recv_sem=recv_sems.at[a, mask - 1],
                    device_id=peer, device_id_type=pl.DeviceIdType.MESH).wait_recv()
        for cp in sends:
            cp.wait_send()
        for cp in local:
            cp.wait()

    out_shape = [jax.ShapeDtypeStruct(x.shape if scatter else (N_DEV,) + x.shape, x.dtype)
                 for x in arrays]
    return pl.pallas_call(
        body, name=name,
        in_specs=[pl.BlockSpec(memory_space=pl.ANY)] * n,
        out_specs=[pl.BlockSpec(memory_space=pl.ANY)] * n,
        out_shape=out_shape,
        scratch_shapes=[pltpu.SemaphoreType.DMA((n, N_DEV - 1)),
                        pltpu.SemaphoreType.DMA((n, N_DEV - 1)),
                        pltpu.SemaphoreType.DMA((n,))],
    )(*arrays)


def _gather_two_level(arrays, name):
    n = len(arrays)

    def body(*refs):
        x_refs, o_refs = refs[:n], refs[n:2 * n]
        send_sems, recv_sems, local_sems = refs[2 * n:]
        ix, iy, ic = _mesh_pos()
        me, sibling = (ix, iy, ic), (ix, iy, 1 - ic)
        chips = [(1 - ix, iy), (ix, 1 - iy), (1 - ix, 1 - iy)]

        def idx(px, py, pc):
            return 4 * px + 2 * py + pc

        def copy(a, k, block, to, src=None):
            dst = o_refs[a].at[idx(*block)]
            return pltpu.make_async_remote_copy(
                src_ref=dst if src is None else src, dst_ref=dst,
                send_sem=send_sems.at[a, k], recv_sem=recv_sems.at[a, k],
                device_id=to, device_id_type=pl.DeviceIdType.MESH)

        local = [pltpu.make_async_copy(x_refs[a], o_refs[a].at[idx(*me)], local_sems.at[a])
                 for a in range(n)]
        for cp in local:
            cp.start()
        first = []
        for a in range(n):
            first.append(copy(a, 0, me, sibling, src=x_refs[a]))
            first += [copy(a, 1 + j, me, (*chip, ic), src=x_refs[a])
                      for j, chip in enumerate(chips)]
        for cp in first:
            cp.start()
        passed = []
        for j, chip in enumerate(chips):
            for a in range(n):
                copy(a, 1 + j, (*chip, ic), me).wait_recv()
                cp = copy(a, 4 + j, (*chip, ic), sibling)
                cp.start()
                passed.append(cp)
        for a in range(n):
            copy(a, 0, sibling, me).wait_recv()
            for j, chip in enumerate(chips):
                copy(a, 4 + j, (*chip, 1 - ic), me).wait_recv()
        for cp in first + passed:
            cp.wait_send()
        for cp in local:
            cp.wait()

    return pl.pallas_call(
        body, name=name,
        in_specs=[pl.BlockSpec(memory_space=pl.ANY)] * n,
        out_specs=[pl.BlockSpec(memory_space=pl.ANY)] * n,
        out_shape=[jax.ShapeDtypeStruct((N_DEV,) + x.shape, x.dtype) for x in arrays],
        scratch_shapes=[pltpu.SemaphoreType.DMA((n, N_DEV - 1)),
                        pltpu.SemaphoreType.DMA((n, N_DEV - 1)),
                        pltpu.SemaphoreType.DMA((n,))],
    )(*arrays)


def _swap_sibling(arrays, name):
    n = len(arrays)
    n_chips = N_DEV // 2

    def body(*refs):
        x_refs, got_refs = refs[:n], refs[n:2 * n]
        send_sems, recv_sems = refs[2 * n:]
        ix, iy, ic = _mesh_pos()
        sibling = (ix, iy, 1 - ic)
        sends = []
        for a in range(n):
            for q in range(n_chips):
                cp = pltpu.make_async_remote_copy(
                    src_ref=x_refs[a].at[q, 1 - ic], dst_ref=got_refs[a].at[q],
                    send_sem=send_sems.at[a, q], recv_sem=recv_sems.at[a, q],
                    device_id=sibling, device_id_type=pl.DeviceIdType.MESH)
                cp.start()
                sends.append(cp)
        for cp in sends:
            cp.wait()

    return pl.pallas_call(
        body, name=name,
        in_specs=[pl.BlockSpec(memory_space=pl.ANY)] * n,
        out_specs=[pl.BlockSpec(memory_space=pl.ANY)] * n,
        out_shape=[jax.ShapeDtypeStruct((n_chips,) + x.shape[2:], x.dtype) for x in arrays],
        scratch_shapes=[pltpu.SemaphoreType.DMA((n, n_chips)),
                        pltpu.SemaphoreType.DMA((n, n_chips))],
    )(*arrays)


def _exchange_chips(arrays, name):
    n = len(arrays)
    n_chips = N_DEV // 2

    def body(*refs):
        x_refs, o_refs = refs[:n], refs[n:2 * n]
        send_sems, recv_sems, local_sems = refs[2 * n:]
        ix, iy, ic = _mesh_pos()
        my_chip = 2 * ix + iy
        local = [pltpu.make_async_copy(x_refs[a].at[my_chip], o_refs[a].at[my_chip],
                                       local_sems.at[a]) for a in range(n)]
        for cp in local:
            cp.start()
        sends = []
        for mask in range(1, n_chips):
            px = 1 - ix if mask & 2 else ix
            py = 1 - iy if mask & 1 else iy
            for a in range(n):
                cp = pltpu.make_async_remote_copy(
                    src_ref=x_refs[a].at[2 * px + py], dst_ref=o_refs[a].at[my_chip],
                    send_sem=send_sems.at[a, mask - 1], recv_sem=recv_sems.at[a, mask - 1],
                    device_id=(px, py, ic), device_id_type=pl.DeviceIdType.MESH)
                cp.start()
                sends.append(cp)
        for mask in range(1, n_chips):
            px = 1 - ix if mask & 2 else ix
            py = 1 - iy if mask & 1 else iy
            for a in range(n):
                pltpu.make_async_remote_copy(
                    src_ref=x_refs[a].at[my_chip], dst_ref=o_refs[a].at[2 * px + py],
                    send_sem=send_sems.at[a, mask - 1], recv_sem=recv_sems.at[a, mask - 1],
                    device_id=(px, py, ic), device_id_type=pl.DeviceIdType.MESH).wait_recv()
        for cp in sends:
            cp.wait_send()
        for cp in local:
            cp.wait()

    return pl.pallas_call(
        body, name=name,
        in_specs=[pl.BlockSpec(memory_space=pl.ANY)] * n,
        out_specs=[pl.BlockSpec(memory_space=pl.ANY)] * n,
        out_shape=[jax.ShapeDtypeStruct(x.shape, x.dtype) for x in arrays],
        scratch_shapes=[pltpu.SemaphoreType.DMA((n, n_chips - 1)),
                        pltpu.SemaphoreType.DMA((n, n_chips - 1)),
                        pltpu.SemaphoreType.DMA((n,))],
    )(*arrays)


def _pair_sum(core, x, got, name):
    nq, rows, cols = got.shape

    def body(c_ref, x_ref, g_ref, o_ref):
        o_ref[...] = (x_ref[...] + g_ref[...]).astype(BF16)

    blk = pl.BlockSpec((None, rows, cols), lambda q, c: (q, 0, 0))
    return pl.pallas_call(
        body, name=name,
        grid_spec=pltpu.PrefetchScalarGridSpec(
            num_scalar_prefetch=1, grid=(nq,),
            in_specs=[pl.BlockSpec((None, None, rows, cols), lambda q, c: (q, c[0], 0, 0)), blk],
            out_specs=blk),
        out_shape=jax.ShapeDtypeStruct(got.shape, BF16),
        compiler_params=_params(),
    )(core, x, got)


def _selectors():
    r = lax.broadcasted_iota(jnp.int32, (LANES, FOX_PAD), 0)
    c = lax.broadcasted_iota(jnp.int32, (LANES, FOX_PAD), 1)
    part, head_r = r // HEADS, r % HEADS
    head_c, lane_c = c // HEAD_PAD, c % HEAD_PAD
    same = (head_r == head_c) & (part < 3)
    sel_q = jnp.where(same & (lane_c == LANE_RB + part), 1.0, 0.0)
    sel_k = jnp.where(same & (lane_c == LANE_CK + part), -1.0, 0.0)
    sel = jnp.stack([sel_q, sel_k, jnp.zeros_like(sel_q)]).astype(BF16)
    lane = lax.broadcasted_iota(jnp.int32, (1, FOX_PAD), 1) % HEAD_PAD
    ones_q = jnp.where((lane >= LANE_CK) & (lane < LANE_CK + 3), 1.0, 0.0)
    ones_k = jnp.where(((lane >= LANE_RB) & (lane < LANE_RB + 3))
                       | ((lane >= LANE_LSE) & (lane < LANE_LSE + 3)), 1.0, 0.0)
    ones_v = jnp.where((lane >= LANE_ONE_V) & (lane < LANE_ONE_V + 2), 1.0, 0.0)
    bias = jnp.stack([ones_q, ones_k, ones_v]).astype(F32)
    return sel, bias


def _local_step(x, target, norm_g, final_g, w_in8, conv_w, conv_b, wa, ba, wx, bx, a_param,
                w_out_b, fox_in8, b_f, fox_out_b, blk=512, ts=256):
    qk_scale = 1.0 / (HEAD_DIM ** 0.5)
    g0, g1 = norm_g[0:1], norm_g[1:2]
    gf = final_g.reshape(1, D_MODEL)
    wa_b, wx_b = wa.astype(BF16), wx.astype(BF16)
    fox_w_in = jnp.transpose(fox_in8, (1, 0, 2)).reshape(D_MODEL, FOX_IN_COLS)
    width = HEADS * HEAD_DIM
    w4 = jnp.stack([fox_w_in[:, 0:width] * qk_scale, fox_w_in[:, width:2 * width],
                    fox_w_in[:, 2 * width:3 * width], fox_w_in[:, 3 * width:4 * width]])
    wf_b = jnp.pad(fox_w_in[:, 4 * width:], ((0, 0), (0, LANES - HEADS)))
    bf_pad = jnp.pad(b_f, ((0, 0), (0, LANES - HEADS)))
    fo_b = fox_out_b
    sel, bias = _selectors()

    xb, gate1, h0 = _lru_in_fwd(x, g0, w_in8, ts)
    y1, hs = _lru_core_fwd(xb, gate1, conv_w, conv_b, wa_b, ba, wx_b, bx, a_param, ts)
    x1, h1, f, cparts = _fox_pre_fwd(x, y1, w_out_b, g1, wf_b, bf_pad, ts)
    qkv = _fox_proj_fwd(h1, cparts, w4[0:3], sel, bias, BF16, ts, "fox_proj_qkv")
    gate2 = _fox_proj_fwd(h1, None, w4[3:4], None, None, F32, ts, "fox_proj_gate")[0]
    o, qb = _attn_fwd(qkv, blk, hps=4)
    dx2, dx2b, y2, loss_acc, g_final = _fox_out_loss(o, gate2, fo_b, x1, target, gf, ts)

    do, dgate2 = _fox_out_bwd(dx2b, fo_b, o, gate2, ts)
    dq, dk, dv, dcum = _attn_bwd(qb, qkv, do, blk)
    dx1, dx1b, df, du4, g_norm1, g_bf = _fox_in_bwd(dq, dk, dv, dgate2, w4, wf_b, dcum, f, x1,
                                                    dx2, g1, ts)
    du, g_wa, g_wx, g_vec = _lru_core_bwd(dx1b, w_out_b, xb, gate1, hs, conv_w, conv_b, wa_b, ba,
                                          wx_b, bx, a_param, jnp.transpose(wa_b, (0, 2, 1)),
                                          jnp.transpose(wx_b, (0, 2, 1)), ts)
    grad_x, g_norm0 = _lru_in_bwd(du, w_in8, x, dx1, g0, ts)

    tw = 512
    g_lru_w_in = _weight_grad(h0, du, tw, "grad_lru_w_in", col_shards=N_DEV)
    g_lru_w_out = _weight_grad(y1, dx1b, tw, "grad_lru_w_out")
    g_q = _weight_grad(h1, du4[0], tw, "grad_fox_wq", scale=qk_scale)
    g_k = _weight_grad(h1, du4[1], tw, "grad_fox_wk")
    g_v = _weight_grad(h1, du4[2], tw, "grad_fox_wv")
    g_g = _weight_grad(h1, du4[3], tw, "grad_fox_wg")
    g_f = _weight_grad(h1, df, tw, "grad_fox_wf")
    g_fox_w_in = jnp.concatenate([g_q, g_k, g_v, g_g, g_f[:, :HEADS]], axis=1)
    g_fox_w_in = jnp.transpose(g_fox_w_in.reshape(D_MODEL, N_DEV, FOX_IN_SHARD), (1, 0, 2))
    g_fox_w_out = _weight_grad(y2, dx2b, tw, "grad_fox_w_out")

    grads = dict(
        norm_g=jnp.concatenate([g_norm0, g_norm1], axis=0), final_g=g_final[0],
        lru_w_in=g_lru_w_in, lru_conv_w=g_vec[0:4], lru_conv_b=g_vec[4:5], lru_wa=g_wa,
        lru_ba=g_vec[5:6], lru_wx=g_wx, lru_bx=g_vec[6:7], lru_a_param=g_vec[7:8],
        lru_w_out=g_lru_w_out, fox_w_in=g_fox_w_in, fox_b_f=g_bf[:, :HEADS],
        fox_w_out=g_fox_w_out)
    return loss_acc[0, 0], grad_x, grads


SMALL =("norm_g", "final_g", "lru_conv_b", "lru_wa", "lru_ba", "lru_wx", "lru_bx", "lru_a_param",
         "fox_b_f")
ALL_WEIGHTS = ("norm_g", "final_g", "lru_w_in", "lru_conv_w", "lru_conv_b", "lru_wa", "lru_ba",
               "lru_wx", "lru_bx", "lru_a_param", "lru_w_out", "fox_w_in", "fox_b_f", "fox_w_out")


def _pack_small(d):
    rows = []
    for n in SMALL:
        a = d[n].reshape(-1)
        if a.shape[0] % LANES:
            a = jnp.pad(a, (0, LANES - a.shape[0] % LANES))
        rows.append(a.reshape(-1, LANES))
    packed = jnp.concatenate(rows, axis=0)
    return jnp.pad(packed, ((0, N_DEV * SMALL_CHUNK_ROWS - packed.shape[0]), (0, 0)))


def _unpack_small(packed, like):
    out, off = {}, 0
    for n, nrows in zip(SMALL, SMALL_ROWS):
        size = like[n].size
        out[n] = packed[off:off + nrows].reshape(-1)[:size].reshape(like[n].shape)
        off += nrows
    return out


def kernel(x, norm_g, final_g, lru_w_in, lru_conv_w, lru_conv_b, lru_wa, lru_ba, lru_wx, lru_bx, lru_a_param, lru_w_out, fox_w_in, fox_b_f, fox_w_out, loss_target, m_norm_g, m_final_g, m_lru_w_in, m_lru_conv_w, m_lru_conv_b, m_lru_wa, m_lru_ba, m_lru_wx, m_lru_bx, m_lru_a_param, m_lru_w_out, m_fox_w_in, m_fox_b_f, m_fox_w_out, v_norm_g, v_final_g, v_lru_w_in, v_lru_conv_w, v_lru_conv_b, v_lru_wa, v_lru_ba, v_lru_wx, v_lru_bx, v_lru_a_param, v_lru_w_out, v_fox_w_in, v_fox_b_f, v_fox_w_out):
    w_loc = dict(norm_g=norm_g, final_g=final_g, lru_w_in=lru_w_in, lru_conv_w=lru_conv_w,
                 lru_conv_b=lru_conv_b, lru_wa=lru_wa, lru_ba=lru_ba, lru_wx=lru_wx, lru_bx=lru_bx,
                 lru_a_param=lru_a_param, lru_w_out=lru_w_out, fox_w_in=fox_w_in, fox_b_f=fox_b_f,
                 fox_w_out=fox_w_out)
    m_loc = dict(norm_g=m_norm_g, final_g=m_final_g, lru_w_in=m_lru_w_in, lru_conv_w=m_lru_conv_w,
                 lru_conv_b=m_lru_conv_b, lru_wa=m_lru_wa, lru_ba=m_lru_ba, lru_wx=m_lru_wx,
                 lru_bx=m_lru_bx, lru_a_param=m_lru_a_param, lru_w_out=m_lru_w_out,
                 fox_w_in=m_fox_w_in, fox_b_f=m_fox_b_f, fox_w_out=m_fox_w_out)
    v_loc = dict(norm_g=v_norm_g, final_g=v_final_g, lru_w_in=v_lru_w_in, lru_conv_w=v_lru_conv_w,
                 lru_conv_b=v_lru_conv_b, lru_wa=v_lru_wa, lru_ba=v_lru_ba, lru_wx=v_lru_wx,
                 lru_bx=v_lru_bx, lru_a_param=v_lru_a_param, lru_w_out=v_lru_w_out,
                 fox_w_in=v_fox_w_in, fox_b_f=v_fox_b_f, fox_w_out=v_fox_w_out)

    w_in8, conv8, w_out8, fox_in8, fox_out8 = _gather_two_level(
        [lru_w_in[0].astype(BF16), lru_conv_w[0], lru_w_out[0].astype(BF16),
         fox_w_in[0].astype(BF16), fox_w_out[0].astype(BF16)], "gather_weights")
    conv_full = jnp.transpose(conv8, (1, 0, 2)).reshape(CONV_WIDTH, LRU_WIDTH)

    loss, grad_x, grads = _local_step(
        x[0], loss_target[0], norm_g, final_g, w_in8, conv_full, lru_conv_b, lru_wa[0], lru_ba,
        lru_wx[0], lru_bx, lru_a_param, w_out8.reshape(LRU_WIDTH, D_MODEL), fox_in8, fox_b_f,
        fox_out8.reshape(HEADS * HEAD_DIM, D_MODEL))

    n_chips = N_DEV // 2
    conv_send = jnp.transpose(grads["lru_conv_w"].reshape(CONV_WIDTH, N_DEV, -1), (1, 0, 2))
    names = ("lru_w_in", "lru_conv_w", "lru_w_out", "fox_w_in", "fox_w_out", "small")
    send = [grads["lru_w_in"], conv_send, grads["lru_w_out"].reshape(N_DEV, -1, D_MODEL),
            grads["fox_w_in"], grads["fox_w_out"].reshape(N_DEV, -1, D_MODEL),
            _pack_small(grads).reshape(N_DEV, SMALL_CHUNK_ROWS, LANES)]
    send = [a.reshape((n_chips, 2) + a.shape[1:]) for a in send]
    got = _swap_sibling(send, "swap_grads")
    core = lax.axis_index("c").astype(jnp.int32).reshape(1)
    chip_sums = [_pair_sum(core, a, b, "pair_sum_" + n) for n, a, b in zip(names, send, got)]
    r_w_in, r_conv, r_w_out, r_fox_in, r_fox_out, r_small = _exchange_chips(
        chip_sums, "scatter_grads")

    out = {}
    for n, recv, tr in (("lru_w_in", r_w_in, 256), ("lru_conv_w", r_conv, CONV_WIDTH),
                        ("lru_w_out", r_w_out, 96), ("fox_w_in", r_fox_in, 128),
                        ("fox_w_out", r_fox_out, 64)):
        res = _adamw(recv, w_loc[n][0], m_loc[n][0], v_loc[n][0], tr, "adamw_" + n)
        out[n] = [a[None] for a in res]

    g_chunk = _reduce_parts(r_small, "reduce_small_grads")
    g_small, = _exchange([g_chunk], False, "gather_small_grads")
    g_small = g_small.reshape(1, N_DEV * SMALL_CHUNK_ROWS, LANES)
    res = _adamw(g_small, _pack_small(w_loc), _pack_small(m_loc), _pack_small(v_loc),
                 N_DEV * SMALL_CHUNK_ROWS, "adamw_replicated")
    small_out = [_unpack_small(a, w_loc) for a in res]
    for n in SMALL:
        out[n] = [d[n] for d in small_out]

    loss = lax.psum(loss, ("x", "y", "c"))
    return (loss, grad_x[None], *[out[n][0] for n in ALL_WEIGHTS], *[out[n][1] for n in ALL_WEIGHTS],
            *[out[n][2] for n in ALL_WEIGHTS], *[out[n][3] for n in ALL_WEIGHTS])
```

```python
import functools

import jax
import jax.numpy as jnp
from jax import lax
from jax.experimental import pallas as pl
from jax.experimental.pallas import tpu as pltpu

F32 = jnp.float32
BF16 = jnp.bfloat16

D_MODEL = 1024
LRU_WIDTH = 1536
LRU_BLOCKS = 12
LRU_BLOCK_W = 128
CONV_WIDTH = 4
LRU_C = 8.0
HEADS = 16
HEAD_DIM = 64
HEAD_PAD = 128
FOX_PAD = HEADS * HEAD_PAD
HEADS_PER_STEP = 2
EPS = 1e-6
NEG_BIG = -1e30
N_DEV = 8

ADAM_LR = 0.001
ADAM_B1 = 0.9
ADAM_B2 = 0.999
ADAM_EPS = 1e-08
ADAM_WD = 0.01
ADAM_STEP = 10

LANE_RB = 64
LANE_CK = 67
LANE_LSE = 70
LANE_ONE_V = 64

VMEM_LIMIT_BYTES = 56 * 1024 * 1024
LANES = 128
SUBLANES = 8

LRU_IN_SHARD = 2 * LRU_WIDTH // N_DEV
FOX_IN_COLS = 4 * HEADS * HEAD_DIM + HEADS
FOX_IN_SHARD = FOX_IN_COLS // N_DEV

SMALL_ROWS = (16, 8, 12, 1536, 12, 1536, 12, 12, 1)
SMALL_CHUNK_ROWS = 400
assert sum(SMALL_ROWS) <= N_DEV * SMALL_CHUNK_ROWS


def _params(n_grid_axes=1):
    return pltpu.CompilerParams(
        dimension_semantics=("arbitrary",) * n_grid_axes,
        vmem_limit_bytes=VMEM_LIMIT_BYTES)


def _const_spec(shape):
    nd = len(shape)
    return pl.BlockSpec(shape, lambda *_: (0,) * nd, pipeline_mode=pl.Buffered(1))


def _shift_down(x, k, fill):
    rows = lax.broadcasted_iota(jnp.int32, x.shape, 0)
    return jnp.where(rows >= k, pltpu.roll(x, k, 0), fill)


def _shift_up(x, k, fill):
    n = x.shape[0]
    rows = lax.broadcasted_iota(jnp.int32, x.shape, 0)
    return jnp.where(rows < n - k, pltpu.roll(x, n - k, 0), fill)


def _scan_rows(a, b, reverse=False):
    n = a.shape[0]
    shift = _shift_up if reverse else _shift_down
    k = 1
    while k < n:
        b = a * shift(b, k, 0.0) + b
        a = a * shift(a, k, 1.0)
        k *= 2
    return a, b


def _cumsum_rows(x, reverse=False):
    n = x.shape[0]
    shift = _shift_up if reverse else _shift_down
    k = 1
    while k < n:
        x = x + shift(x, k, 0.0)
        k *= 2
    return x


def _rstd(x):
    return lax.rsqrt(jnp.mean(x * x, axis=-1, keepdims=True) + EPS)


def _norm_bwd(x, g, dh):
    rstd = _rstd(x)
    xhat = x * rstd
    dg = jnp.sum(dh * xhat, axis=0, keepdims=True)
    dxh = dh * g
    dx = rstd * (dxh - xhat * jnp.mean(dxh * xhat, axis=-1, keepdims=True))
    return dx, dg


def _split3(x):
    hi = x.astype(BF16)
    r1 = x - hi.astype(F32)
    mid = r1.astype(BF16)
    lo = (r1 - mid.astype(F32)).astype(BF16)
    return hi, mid, lo


def _sigmoid(x):
    return jax.nn.sigmoid(x)


def _dot(a, b):
    return jnp.dot(a, b, preferred_element_type=F32)


def _dot_nt(a, b):
    return lax.dot_general(a, b, (((1,), (1,)), ((), ())), preferred_element_type=F32)


def _dot_tn(a, b):
    return lax.dot_general(a, b, (((0,), (0,)), ((), ())), preferred_element_type=F32)


def _heads_to_padded(u):
    n = u.shape[0]
    low = lax.broadcasted_iota(jnp.int32, (n, LANES), 1) < HEAD_DIM
    zero = jnp.zeros((n, LANES), u.dtype)
    cols = []
    for p in range(HEADS // 2):
        pair = u[:, p * LANES:(p + 1) * LANES]
        cols.append(jnp.where(low, pair, zero))
        cols.append(jnp.where(low, pltpu.roll(pair, HEAD_DIM, 1), zero))
    return jnp.concatenate(cols, axis=1)


def _heads_from_padded(x):
    n = x.shape[0]
    low = lax.broadcasted_iota(jnp.int32, (n, LANES), 1) < HEAD_DIM
    cols = []
    for p in range(HEADS // 2):
        even = x[:, (2 * p) * HEAD_PAD:(2 * p + 1) * HEAD_PAD]
        odd = x[:, (2 * p + 1) * HEAD_PAD:(2 * p + 2) * HEAD_PAD]
        cols.append(jnp.where(low, even, pltpu.roll(odd, HEAD_DIM, 1)))
    return jnp.concatenate(cols, axis=1)


def _conv_taps(xb, prev8):
    rows8 = lax.broadcasted_iota(jnp.int32, prev8.shape, 0)
    taps = [xb]
    for j in range(1, CONV_WIDTH):
        r = pltpu.roll(xb, j, 0)
        p = pltpu.roll(prev8, j, 0)
        head = jnp.where(rows8 < j, p, r[0:SUBLANES])
        taps.append(jnp.concatenate([head, r[SUBLANES:]], axis=0))
    return taps


def _lru_pre(taps, cw, cb, wa_ref, ba, wx_ref, bx, a_param):
    xc = cb + cw[3:4] * taps[0] + cw[2:3] * taps[1] + cw[1:2] * taps[2] + cw[0:1] * taps[3]
    xcb = xc.astype(BF16)
    ra, ia = [], []
    for n in range(LRU_BLOCKS):
        blk = xcb[:, n * LRU_BLOCK_W:(n + 1) * LRU_BLOCK_W]
        ra.append(_dot(blk, wa_ref[n]))
        ia.append(_dot(blk, wx_ref[n]))
    r = _sigmoid(jnp.concatenate(ra, axis=1) + ba)
    i = _sigmoid(jnp.concatenate(ia, axis=1) + bx)
    z = -a_param
    sp = jnp.maximum(z, 0.0) + jnp.log1p(jnp.exp(-jnp.abs(z)))
    log_a = (-LRU_C) * r * sp
    a = jnp.exp(log_a)
    one_minus_a2 = -jnp.tanh(log_a) * (a * a + 1.0)
    mult = jnp.sqrt(one_minus_a2)
    return xc, xcb, r, i, sp, a, mult


def _lru_in_fwd(x, g0, w_in, later_shards, ts):
    s = x.shape[0]
    nt = s // ts
    half = N_DEV // 2
    n = len(later_shards)

    def body(*refs):
        x_ref, g_ref, w_ref = refs[:3]
        shard_refs = refs[3:3 + n]
        xb_ref, gate_ref, h_ref = refs[3 + n:6 + n]
        start, forward, finish = _gather_phases(shard_refs, refs[6 + n:6 + 2 * n],
                                                *refs[6 + 2 * n:])
        step = pl.program_id(0)
        pl.when(step == 0)(start)
        xv = x_ref[...]
        h = (xv * _rstd(xv) * g_ref[...]).astype(BF16)
        u = [_dot(h, w_ref[j]) for j in range(N_DEV)]
        xb_ref[...] = jnp.concatenate(u[:half], axis=1)
        gate_ref[...] = jnp.concatenate(u[half:], axis=1)
        h_ref[...] = h
        pl.when(step == (2 * nt) // 3)(forward)
        pl.when(step == nt - 1)(finish)

    hbm = pl.BlockSpec(memory_space=pl.ANY)
    res = pl.pallas_call(
        body, name="lru_in_fwd", grid=(nt,),
        in_specs=[pl.BlockSpec((ts, D_MODEL), lambda i: (i, 0)),
                  _const_spec((1, D_MODEL)),
                  _const_spec((N_DEV, D_MODEL, LRU_IN_SHARD))] + [hbm] * n,
        out_specs=[pl.BlockSpec((ts, LRU_WIDTH), lambda i: (i, 0)),
                   pl.BlockSpec((ts, LRU_WIDTH), lambda i: (i, 0)),
                   pl.BlockSpec((ts, D_MODEL), lambda i: (i, 0))] + [hbm] * n,
        out_shape=[jax.ShapeDtypeStruct((s, LRU_WIDTH), F32),
                   jax.ShapeDtypeStruct((s, LRU_WIDTH), F32),
                   jax.ShapeDtypeStruct((s, D_MODEL), BF16)]
        + [jax.ShapeDtypeStruct((N_DEV,) + a.shape, a.dtype) for a in later_shards],
        scratch_shapes=_gather_sems(n),
        compiler_params=_params(),
    )(x, g0, w_in, *later_shards)
    return res[0], res[1], res[2], res[3:]


def _lru_core_fwd(xb, gate, cw, cb, wa, ba, wx, bx, a_param, ts):
    s = xb.shape[0]

    def body(xb_ref, gate_ref, cw_ref, cb_ref, wa_ref, ba_ref, wx_ref, bx_ref, ap_ref,
             y_ref, hs_ref, prev_ref, hcar_ref):
        @pl.when(pl.program_id(0) == 0)
        def _():
            prev_ref[...] = jnp.zeros_like(prev_ref)
            hcar_ref[...] = jnp.zeros_like(hcar_ref)

        xbv = xb_ref[...]
        taps = _conv_taps(xbv, prev_ref[...])
        xc, _, _, i, _, a, mult = _lru_pre(taps, cw_ref[...], cb_ref[...], wa_ref, ba_ref[...],
                                           wx_ref, bx_ref[...], ap_ref[...])
        bterm = mult * (i * xc)
        cum_a, hloc = _scan_rows(a, bterm)
        hs = cum_a * hcar_ref[SUBLANES - 1:SUBLANES, :] + hloc
        gv = gate_ref[...]
        y_ref[...] = (hs * (gv * _sigmoid(gv))).astype(BF16)
        hs_ref[...] = hs
        prev_ref[...] = xbv[ts - SUBLANES:, :]
        hcar_ref[...] = hs[ts - SUBLANES:, :]

    vec = _const_spec((1, LRU_WIDTH))
    blk = _const_spec((LRU_BLOCKS, LRU_BLOCK_W, LRU_BLOCK_W))
    tile = pl.BlockSpec((ts, LRU_WIDTH), lambda i: (i, 0))
    return pl.pallas_call(
        body, name="lru_core_fwd", grid=(s // ts,),
        in_specs=[tile, tile, _const_spec((CONV_WIDTH, LRU_WIDTH)), vec, blk, vec, blk, vec, vec],
        out_specs=[tile, tile],
        out_shape=[jax.ShapeDtypeStruct((s, LRU_WIDTH), BF16),
                   jax.ShapeDtypeStruct((s, LRU_WIDTH), F32)],
        scratch_shapes=[pltpu.VMEM((SUBLANES, LRU_WIDTH), F32),
                        pltpu.VMEM((SUBLANES, LRU_WIDTH), F32)],
        compiler_params=_params(),
    )(xb, gate, cw, cb, wa, ba, wx, bx, a_param)


def _fox_pre_fwd(x, y, w_out, g1, wf, bf, ts):
    s = x.shape[0]

    def body(x_ref, y_ref, w_ref, g_ref, wf_ref, bf_ref, x1_ref, h1_ref, f_ref, cp_ref, ccar_ref):
        @pl.when(pl.program_id(0) == 0)
        def _():
            ccar_ref[...] = jnp.zeros_like(ccar_ref)

        x1 = x_ref[...] + _dot(y_ref[...], w_ref[...])
        h1 = (x1 * _rstd(x1) * g_ref[...]).astype(BF16)
        f = _dot(h1, wf_ref[...]) + bf_ref[...]
        logsig = jnp.minimum(f, 0.0) - jnp.log1p(jnp.exp(-jnp.abs(f)))
        cum = _cumsum_rows(logsig) + ccar_ref[SUBLANES - 1:SUBLANES, :]
        hi, mid, lo = _split3(cum)
        lane = lax.broadcasted_iota(jnp.int32, cum.shape, 1)
        packed = jnp.where(lane < HEADS, hi.astype(F32), jnp.where(
            lane < 2 * HEADS, pltpu.roll(mid.astype(F32), HEADS, 1), jnp.where(
                lane < 3 * HEADS, pltpu.roll(lo.astype(F32), 2 * HEADS, 1), 0.0)))
        x1_ref[...] = x1
        h1_ref[...] = h1
        f_ref[...] = f
        cp_ref[...] = packed.astype(BF16)
        ccar_ref[...] = cum[ts - SUBLANES:, :]

    return pl.pallas_call(
        body, name="fox_pre_fwd", grid=(s // ts,),
        in_specs=[pl.BlockSpec((ts, D_MODEL), lambda i: (i, 0)),
                  pl.BlockSpec((ts, LRU_WIDTH), lambda i: (i, 0)),
                  _const_spec((LRU_WIDTH, D_MODEL)),
                  _const_spec((1, D_MODEL)),
                  _const_spec((D_MODEL, LANES)),
                  _const_spec((1, LANES))],
        out_specs=[pl.BlockSpec((ts, D_MODEL), lambda i: (i, 0)),
                   pl.BlockSpec((ts, D_MODEL), lambda i: (i, 0)),
                   pl.BlockSpec((ts, LANES), lambda i: (i, 0)),
                   pl.BlockSpec((ts, LANES), lambda i: (i, 0))],
        out_shape=[jax.ShapeDtypeStruct((s, D_MODEL), F32),
                   jax.ShapeDtypeStruct((s, D_MODEL), BF16),
                   jax.ShapeDtypeStruct((s, LANES), F32),
                   jax.ShapeDtypeStruct((s, LANES), BF16)],
        scratch_shapes=[pltpu.VMEM((SUBLANES, LANES), F32)],
        compiler_params=_params(),
    )(x, y, w_out, g1, wf, bf)


def _fox_proj_fwd(h1, cparts, w, sel, bias, out_dtype, ts, name):
    s = h1.shape[0]
    ng = w.shape[0]
    width = HEADS * HEAD_DIM
    use_sel = sel is not None

    def body(*refs):
        if use_sel:
            h_ref, cp_ref, w_ref, sel_ref, b_ref, o_ref = refs
            acc = (_heads_to_padded(_dot(h_ref[...], w_ref[...]))
                   + _dot(cp_ref[...], sel_ref[...]) + b_ref[...])
        else:
            h_ref, w_ref, o_ref = refs
            acc = _heads_to_padded(_dot(h_ref[...], w_ref[...]))
        o_ref[...] = acc.astype(out_dtype)

    in_specs = [pl.BlockSpec((ts, D_MODEL), lambda j, i: (i, 0))]
    args = [h1]
    if use_sel:
        in_specs.append(pl.BlockSpec((ts, LANES), lambda j, i: (i, 0)))
        args.append(cparts)
    in_specs.append(pl.BlockSpec((None, D_MODEL, width), lambda j, i: (j, 0, 0)))
    args.append(w)
    if use_sel:
        in_specs.append(pl.BlockSpec((None, LANES, FOX_PAD), lambda j, i: (j, 0, 0)))
        in_specs.append(pl.BlockSpec((None, 1, FOX_PAD), lambda j, i: (j, 0, 0)))
        args += [sel, bias]
    return pl.pallas_call(
        body, name=name, grid=(ng, s // ts),
        in_specs=in_specs,
        out_specs=pl.BlockSpec((None, ts, FOX_PAD), lambda j, i: (j, i, 0)),
        out_shape=jax.ShapeDtypeStruct((ng, s, FOX_PAD), out_dtype),
        compiler_params=_params(2),
    )(*args)


def _attn_fwd(qkv, blk, hps=HEADS_PER_STEP):
    s = qkv.shape[1]
    nblk = s // blk
    wide = 2 * blk
    heads = [slice(i * HEAD_PAD, (i + 1) * HEAD_PAD) for i in range(hps)]

    def body(q_ref, k_ref, v_ref, o_ref, qb_ref, acc_ref, m_ref):
        qi = pl.program_id(1)
        row = lax.broadcasted_iota(jnp.int32, (blk, blk), 0)
        col = lax.broadcasted_iota(jnp.int32, (blk, blk), 1)
        lane = lax.broadcasted_iota(jnp.int32, (blk, HEAD_PAD), 1)
        qs = [q_ref[:, hd] for hd in heads]
        for i in range(hps):
            acc_ref[i] = jnp.zeros((blk, HEAD_PAD), F32)
            m_ref[i] = jnp.full((blk, HEAD_PAD), NEG_BIG, F32)

        def step(k0, size, masked):
            scores = [_dot_nt(q, k_ref[pl.ds(k0, size), hd]) for q, hd in zip(qs, heads)]
            for i, (sc, hd) in enumerate(zip(scores, heads)):
                v = v_ref[pl.ds(k0, size), hd]
                if masked:
                    sc = jnp.where(col <= row, sc, NEG_BIG)
                m = m_ref[i]
                m_new = jnp.maximum(m, jnp.max(sc, axis=-1, keepdims=True))
                p = jnp.exp((sc - jnp.tile(m_new, (1, size // HEAD_PAD))).astype(BF16))
                acc_ref[i] = jnp.exp(m - m_new) * acc_ref[i] + _dot(p, v)
                m_ref[i] = m_new

        def wide_step(kk, _):
            step(pl.multiple_of(kk * wide, wide), wide, False)
            return 0

        lax.fori_loop(0, qi // 2, wide_step, 0)

        @pl.when(qi % 2 == 1)
        def _():
            step(pl.multiple_of((qi - 1) * blk, blk), blk, False)

        step(pl.multiple_of(qi * blk, blk), blk, True)
        for i, (q, hd) in enumerate(zip(qs, heads)):
            acc = acc_ref[i]
            l = jnp.broadcast_to(acc[:, LANE_ONE_V:LANE_ONE_V + 1], (blk, HEAD_PAD))
            o_ref[:, hd] = (acc / l).astype(BF16)
            hi, mid, lo = _split3(-(m_ref[i] + jnp.log(l)))
            qb_ref[:, hd] = jnp.where(lane == LANE_LSE, hi, jnp.where(
                lane == LANE_LSE + 1, mid, jnp.where(lane == LANE_LSE + 2, lo, q)))

    width = hps * HEAD_PAD

    def whole(j):
        return pl.BlockSpec((None, s, width), lambda h, i: (j, 0, h))

    out_spec = pl.BlockSpec((blk, width), lambda h, i: (i, h))
    return pl.pallas_call(
        body, name="attn_fwd", grid=(HEADS // hps, nblk),
        in_specs=[pl.BlockSpec((None, blk, width), lambda h, i: (0, i, h)), whole(1), whole(2)],
        out_specs=[out_spec, out_spec],
        out_shape=[jax.ShapeDtypeStruct((s, FOX_PAD), BF16),
                   jax.ShapeDtypeStruct((s, FOX_PAD), BF16)],
        scratch_shapes=[pltpu.VMEM((hps, blk, HEAD_PAD), F32),
                        pltpu.VMEM((hps, blk, HEAD_PAD), F32)],
        compiler_params=_params(2),
    )(qkv, qkv, qkv)


def _fox_out_loss(o, gate, w_out, x1, target, gf, ts):
    s = x1.shape[0]

    def body(o_ref, gt_ref, w_ref, x1_ref, t_ref, g_ref, dx2_ref, dx2b_ref, y2_ref, loss_ref,
             gfin_ref):
        @pl.when(pl.program_id(0) == 0)
        def _():
            loss_ref[...] = jnp.zeros_like(loss_ref)
            gfin_ref[...] = jnp.zeros_like(gfin_ref)

        gv = gt_ref[...]
        y2 = _heads_from_padded(o_ref[...] * (gv * _sigmoid(gv))).astype(BF16)
        x2 = x1_ref[...] + _dot(y2, w_ref[...])
        rstd = _rstd(x2)
        xhat = x2 * rstd
        g = g_ref[...]
        diff = xhat * g - t_ref[...]
        loss_ref[...] += 0.5 * jnp.sum(jnp.mean(diff * diff, axis=-1, keepdims=True))
        dy = diff * (1.0 / D_MODEL)
        gfin_ref[...] += jnp.sum(dy * xhat, axis=0, keepdims=True)
        dxh = dy * g
        dx2 = rstd * (dxh - xhat * jnp.mean(dxh * xhat, axis=-1, keepdims=True))
        dx2_ref[...] = dx2
        dx2b_ref[...] = dx2.astype(BF16)
        y2_ref[...] = y2

    return pl.pallas_call(
        body, name="fox_out_loss", grid=(s // ts,),
        in_specs=[pl.BlockSpec((ts, FOX_PAD), lambda i: (i, 0)),
                  pl.BlockSpec((ts, FOX_PAD), lambda i: (i, 0)),
                  _const_spec((HEADS * HEAD_DIM, D_MODEL)),
                  pl.BlockSpec((ts, D_MODEL), lambda i: (i, 0)),
                  pl.BlockSpec((ts, D_MODEL), lambda i: (i, 0)),
                  _const_spec((1, D_MODEL))],
        out_specs=[pl.BlockSpec((ts, D_MODEL), lambda i: (i, 0)),
                   pl.BlockSpec((ts, D_MODEL), lambda i: (i, 0)),
                   pl.BlockSpec((ts, HEADS * HEAD_DIM), lambda i: (i, 0)),
                   pl.BlockSpec((SUBLANES, LANES), lambda i: (0, 0)),
                   pl.BlockSpec((1, D_MODEL), lambda i: (0, 0))],
        out_shape=[jax.ShapeDtypeStruct((s, D_MODEL), F32),
                   jax.ShapeDtypeStruct((s, D_MODEL), BF16),
                   jax.ShapeDtypeStruct((s, HEADS * HEAD_DIM), BF16),
                   jax.ShapeDtypeStruct((SUBLANES, LANES), F32),
                   jax.ShapeDtypeStruct((1, D_MODEL), F32)],
        compiler_params=_params(),
    )(o, gate, w_out, x1, target, gf)


def _fox_out_bwd(dx2, w_out, o, gate, ts):
    s = dx2.shape[0]

    def body(dx_ref, w_ref, o_ref, gt_ref, do_ref, dg_ref):
        lane = lax.broadcasted_iota(jnp.int32, (ts, HEAD_PAD), 1)
        dy2 = _heads_to_padded(_dot_nt(dx_ref[...], w_ref[...]))
        gv = gt_ref[...]
        sg = _sigmoid(gv)
        ov = o_ref[...]
        dov = dy2 * (gv * sg)
        dg_ref[...] = (dy2 * ov * (sg * (1.0 + gv * (1.0 - sg)))).astype(BF16)
        prod = dov * ov
        for h in range(HEADS):
            sl = slice(h * HEAD_PAD, (h + 1) * HEAD_PAD)
            delta = jnp.sum(prod[:, sl], axis=-1, keepdims=True)
            hi = delta.astype(BF16)
            lo = (delta - hi.astype(F32)).astype(BF16)
            do_h = dov[:, sl].astype(BF16)
            do_ref[:, sl] = jnp.where(lane == LANE_ONE_V, -hi,
                                      jnp.where(lane == LANE_ONE_V + 1, -lo, do_h))

    tile = pl.BlockSpec((ts, FOX_PAD), lambda i: (i, 0))
    return pl.pallas_call(
        body, name="fox_out_bwd", grid=(s // ts,),
        in_specs=[pl.BlockSpec((ts, D_MODEL), lambda i: (i, 0)),
                  _const_spec((HEADS * HEAD_DIM, D_MODEL)), tile, tile],
        out_specs=[tile, tile],
        out_shape=[jax.ShapeDtypeStruct((s, FOX_PAD), BF16),
                   jax.ShapeDtypeStruct((s, FOX_PAD), BF16)],
        compiler_params=_params(),
    )(dx2, w_out, o, gate)


def _attn_bwd(qb, qkv, do, blk):
    s = qb.shape[0]
    nblk = s // blk
    heads = [slice(i * HEAD_PAD, (i + 1) * HEAD_PAD) for i in range(HEADS_PER_STEP)]

    def body(q_ref, k_ref, v_ref, do_ref, dq_ref, dk_ref, dv_ref, dcum_ref, dq_acc, dkt_acc,
             dvt_acc, qt_ref, dot_ref):
        group = pl.program_id(0)
        kj = pl.program_id(1)
        row = lax.broadcasted_iota(jnp.int32, (blk, blk), 0)
        col = lax.broadcasted_iota(jnp.int32, (blk, blk), 1)
        lane = lax.broadcasted_iota(jnp.int32, (blk, LANES), 1)
        mine = [lane == group * HEADS_PER_STEP + i for i in range(HEADS_PER_STEP)]

        @pl.when(kj == 0)
        def _():
            dq_acc[...] = jnp.zeros_like(dq_acc)

            def transpose_block(bi, _):
                r0 = pl.multiple_of(bi * blk, blk)
                for i, hd in enumerate(heads):
                    qt_ref[i, bi] = q_ref[pl.ds(r0, blk), hd].T
                    dot_ref[i, bi] = do_ref[pl.ds(r0, blk), hd].T
                return 0

            lax.fori_loop(0, nblk, transpose_block, 0)

        @pl.when((group == 0) & (kj == 0))
        def _():
            dcum_ref[...] = jnp.zeros_like(dcum_ref)

        k0 = pl.multiple_of(kj * blk, blk)
        ks = [k_ref[:, hd] for hd in heads]
        vs = [v_ref[:, hd] for hd in heads]

        dkt_acc[...] = jnp.zeros_like(dkt_acc)
        dvt_acc[...] = jnp.zeros_like(dvt_acc)

        def step(qi, masked):
            q0 = pl.multiple_of(qi * blk, blk)
            qs = [q_ref[pl.ds(q0, blk), hd] for hd in heads]
            dos = [do_ref[pl.ds(q0, blk), hd] for hd in heads]
            scores = [_dot_nt(q, k) for q, k in zip(qs, ks)]
            dps = [_dot_nt(dov, v) for dov, v in zip(dos, vs)]
            for i, (hd, k, sc, dp) in enumerate(zip(heads, ks, scores, dps)):
                p = jnp.exp(sc.astype(BF16))
                if masked:
                    p = jnp.where(col <= row, p, jnp.zeros_like(p))
                ds = (p.astype(F32) * dp).astype(BF16)
                dvt_acc[i] += _dot(dot_ref[i, qi], p)
                dkt_acc[i] += _dot(qt_ref[i, qi], ds)
                dq_acc[pl.ds(q0, blk), hd] += _dot(ds, k)

        step(kj, True)

        def q_step(qi, _):
            step(qi, False)
            return 0

        lax.fori_loop(kj + 1, nblk, q_step, 0)
        dcum = dcum_ref[pl.ds(k0, blk), :]
        for i, (hd, mask) in enumerate(zip(heads, mine)):
            dk = dkt_acc[i].T
            dk_ref[:, hd] = dk.astype(BF16)
            dv_ref[:, hd] = dvt_acc[i].T.astype(BF16)
            dcum = jnp.where(mask, -dk[:, LANE_CK:LANE_CK + 1], dcum)
        dcum_ref[pl.ds(k0, blk), :] = dcum

        @pl.when(kj == nblk - 1)
        def _():
            def finish(bi, _):
                r0 = pl.multiple_of(bi * blk, blk)
                dcum = dcum_ref[pl.ds(r0, blk), :]
                for hd, mask in zip(heads, mine):
                    dq = dq_acc[pl.ds(r0, blk), hd]
                    dq_ref[pl.ds(r0, blk), hd] = dq.astype(BF16)
                    dcum = dcum + jnp.where(mask, dq[:, LANE_RB:LANE_RB + 1], 0.0)
                dcum_ref[pl.ds(r0, blk), :] = dcum
                return 0

            lax.fori_loop(0, nblk, finish, 0)

    width = HEADS_PER_STEP * HEAD_PAD
    whole = pl.BlockSpec((s, width), lambda h, j: (0, h))
    whole_in = pl.BlockSpec((s, width), lambda h, j: (0, h), pipeline_mode=pl.Buffered(1))
    part = pl.BlockSpec((blk, width), lambda h, j: (j, h))
    out = jax.ShapeDtypeStruct((s, FOX_PAD), BF16)
    return pl.pallas_call(
        body, name="attn_bwd", grid=(HEADS // HEADS_PER_STEP, nblk),
        in_specs=[whole_in,
                  pl.BlockSpec((None, blk, width), lambda h, j: (1, j, h)),
                  pl.BlockSpec((None, blk, width), lambda h, j: (2, j, h)),
                  whole_in],
        out_specs=[whole, part, part, pl.BlockSpec((s, LANES), lambda h, j: (0, 0))],
        out_shape=[out, out, out, jax.ShapeDtypeStruct((s, LANES), F32)],
        scratch_shapes=[pltpu.VMEM((s, width), F32),
                        pltpu.VMEM((HEADS_PER_STEP, HEAD_PAD, blk), F32),
                        pltpu.VMEM((HEADS_PER_STEP, HEAD_PAD, blk), F32),
                        pltpu.VMEM((HEADS_PER_STEP, nblk, HEAD_PAD, blk), BF16),
                        pltpu.VMEM((HEADS_PER_STEP, nblk, HEAD_PAD, blk), BF16)],
        compiler_params=_params(2),
    )(qb, qkv, qkv, do)


def _fox_in_bwd(dq, dk, dv, dg, wt, wft, dcum, f, x1, dx2, g1, ts):
    s = x1.shape[0]
    nt = s // ts
    width = HEADS * HEAD_DIM

    def body(dq_ref, dk_ref, dv_ref, dg_ref, wt_ref, wft_ref, dcum_ref, f_ref, x1_ref, dx2_ref,
             g_ref, dx1_ref, dx1b_ref, df_ref, du_ref, gn_ref, gbf_ref, rcar_ref):
        @pl.when(pl.program_id(0) == 0)
        def _():
            rcar_ref[...] = jnp.zeros_like(rcar_ref)
            gn_ref[...] = jnp.zeros_like(gn_ref)
            gbf_ref[...] = jnp.zeros_like(gbf_ref)

        rsum = _cumsum_rows(dcum_ref[...], reverse=True) + rcar_ref[0:1, :]
        df = rsum * _sigmoid(-f_ref[...])
        dfb = df.astype(BF16)
        dh = _dot_nt(dfb, wft_ref[...])
        for j, ref in enumerate((dq_ref, dk_ref, dv_ref, dg_ref)):
            du = _heads_from_padded(ref[...])
            du_ref[j] = du
            dh = dh + _dot_nt(du, wt_ref[j])
        dxn, dgn = _norm_bwd(x1_ref[...], g_ref[...], dh)
        dx1 = dx2_ref[...] + dxn
        dx1_ref[...] = dx1
        dx1b_ref[...] = dx1.astype(BF16)
        df_ref[...] = dfb
        gn_ref[...] += dgn
        gbf_ref[...] += jnp.sum(df, axis=0, keepdims=True)
        rcar_ref[...] = rsum[0:SUBLANES, :]

    rev = lambda i: (nt - 1 - i, 0)
    wide = pl.BlockSpec((ts, FOX_PAD), rev)
    return pl.pallas_call(
        body, name="fox_in_bwd", grid=(nt,),
        in_specs=[wide, wide, wide, wide,
                  _const_spec((4, D_MODEL, width)),
                  _const_spec((D_MODEL, LANES)),
                  pl.BlockSpec((ts, LANES), rev),
                  pl.BlockSpec((ts, LANES), rev),
                  pl.BlockSpec((ts, D_MODEL), rev),
                  pl.BlockSpec((ts, D_MODEL), rev),
                  _const_spec((1, D_MODEL))],
        out_specs=[pl.BlockSpec((ts, D_MODEL), rev),
                   pl.BlockSpec((ts, D_MODEL), rev),
                   pl.BlockSpec((ts, LANES), rev),
                   pl.BlockSpec((4, ts, width), lambda i: (0, nt - 1 - i, 0)),
                   pl.BlockSpec((1, D_MODEL), lambda i: (0, 0)),
                   pl.BlockSpec((1, LANES), lambda i: (0, 0))],
        out_shape=[jax.ShapeDtypeStruct((s, D_MODEL), F32),
                   jax.ShapeDtypeStruct((s, D_MODEL), BF16),
                   jax.ShapeDtypeStruct((s, LANES), BF16),
                   jax.ShapeDtypeStruct((4, s, width), BF16),
                   jax.ShapeDtypeStruct((1, D_MODEL), F32),
                   jax.ShapeDtypeStruct((1, LANES), F32)],
        scratch_shapes=[pltpu.VMEM((SUBLANES, LANES), F32)],
        compiler_params=_params(),
    )(dq, dk, dv, dg, wt, wft, dcum, f, x1, dx2, g1)


def _lru_core_bwd(dx1b, w_out, xb, gate, hs, cw, cb, wa, ba, wx, bx, a_param, wa_t, wx_t,
                  chip_sums, ts):
    s = xb.shape[0]
    nt = s // ts
    tpb = ts // SUBLANES
    n_ex = len(chip_sums)

    def body(*refs):
        (dx_ref, wo_ref, xb_ref, xbh_ref, gate_ref, hs_ref, hsh_ref, cw_ref, cb_ref, wa_ref,
         ba_ref, wx_ref, bx_ref, ap_ref, wat_ref, wxt_ref) = refs[:16]
        sum_refs = refs[16:16 + n_ex]
        du_ref, gwa_ref, gwx_ref, gvec_ref = refs[16 + n_ex:20 + n_ex]
        got_refs = refs[20 + n_ex:20 + 2 * n_ex]
        acar_ref, dhcar_ref, dxccar_ref = refs[20 + 2 * n_ex:23 + 2 * n_ex]
        start, finish = _chip_exchange_phases(sum_refs, got_refs, *refs[23 + 2 * n_ex:])
        step = pl.program_id(0)
        pl.when(step == 0)(start)

        @pl.when(step == 0)
        def _():
            acar_ref[...] = jnp.zeros_like(acar_ref)
            dhcar_ref[...] = jnp.zeros_like(dhcar_ref)
            dxccar_ref[...] = jnp.zeros_like(dxccar_ref)
            gwa_ref[...] = jnp.zeros_like(gwa_ref)
            gwx_ref[...] = jnp.zeros_like(gwx_ref)
            gvec_ref[...] = jnp.zeros_like(gvec_ref)

        first_tile = step == nt - 1
        halo_on = jnp.where(first_tile, 0.0, 1.0)
        prev8 = xbh_ref[...] * halo_on
        hprev_row = hsh_ref[SUBLANES - 1:SUBLANES, :] * halo_on

        xbv = xb_ref[...]
        taps = _conv_taps(xbv, prev8)
        cw_v = cw_ref[...]
        xc, xcb, r, i, sp, a, mult = _lru_pre(taps, cw_v, cb_ref[...], wa_ref, ba_ref[...],
                                              wx_ref, bx_ref[...], ap_ref[...])
        hs = hs_ref[...]
        gv = gate_ref[...]
        sg = _sigmoid(gv)
        dy = _dot_nt(dx_ref[...], wo_ref[...])
        dhs = dy * (gv * sg)
        dgate = dy * hs * (sg * (1.0 + gv * (1.0 - sg)))

        rows = lax.broadcasted_iota(jnp.int32, a.shape, 0)
        a_next = jnp.where(rows < ts - 1, pltpu.roll(a, ts - 1, 0), acar_ref[0:1, :])
        cum_a, dh_loc = _scan_rows(a_next, dhs, reverse=True)
        dh = cum_a * dhcar_ref[0:1, :] + dh_loc
        h_prev = jnp.where(rows >= 1, pltpu.roll(hs, 1, 0), hprev_row)

        da = dh * h_prev
        ixc = i * xc
        dmult = dh * ixc
        di = dh * mult * xc
        dxc = dh * mult * i
        dlog_a = da * a - dmult * (a * a) / mult
        dr = dlog_a * ((-LRU_C) * sp)
        dsp = jnp.sum(dlog_a * ((-LRU_C) * r), axis=0, keepdims=True)
        dra = dr * r * (1.0 - r)
        dia = di * i * (1.0 - i)
        drab = dra.astype(BF16)
        diab = dia.astype(BF16)
        back = []
        for n in range(LRU_BLOCKS):
            sl = slice(n * LRU_BLOCK_W, (n + 1) * LRU_BLOCK_W)
            gwa_ref[n] += _dot_tn(xcb[:, sl], drab[:, sl])
            gwx_ref[n] += _dot_tn(xcb[:, sl], diab[:, sl])
            back.append(_dot(drab[:, sl], wat_ref[n]) + _dot(diab[:, sl], wxt_ref[n]))
        dxc = dxc + jnp.concatenate(back, axis=1)

        nxt8 = dxccar_ref[...]
        rows8 = lax.broadcasted_iota(jnp.int32, nxt8.shape, 0)
        dxb = cw_v[3:4] * dxc
        for j in range(1, CONV_WIDTH):
            rj = pltpu.roll(dxc, ts - j, 0)
            pj = pltpu.roll(nxt8, SUBLANES - j, 0)
            tail = jnp.where(rows8 >= SUBLANES - j, pj, rj[ts - SUBLANES:])
            dxb = dxb + cw_v[3 - j:4 - j] * jnp.concatenate([rj[:ts - SUBLANES], tail], axis=0)

        du_ref[:, :LRU_WIDTH] = dxb.astype(BF16)
        du_ref[:, LRU_WIDTH:] = dgate.astype(BF16)

        z = -ap_ref[...]
        gvec = [jnp.sum(dxc * taps[3 - k], axis=0, keepdims=True) for k in range(CONV_WIDTH)]
        gvec.append(jnp.sum(dxc, axis=0, keepdims=True))
        gvec.append(jnp.sum(dra, axis=0, keepdims=True))
        gvec.append(jnp.sum(dia, axis=0, keepdims=True))
        gvec.append(-dsp * _sigmoid(z))
        gvec_ref[...] += jnp.concatenate(gvec, axis=0)

        acar_ref[...] = a[0:SUBLANES, :]
        dhcar_ref[...] = dh[0:SUBLANES, :]
        dxccar_ref[...] = dxc[0:SUBLANES, :]
        pl.when(step == nt - 1)(finish)

    rev = lambda i: (nt - 1 - i, 0)
    halo = lambda i: (jnp.maximum((nt - 1 - i) * tpb - 1, 0), 0)
    tile = pl.BlockSpec((ts, LRU_WIDTH), rev)
    halo_spec = pl.BlockSpec((SUBLANES, LRU_WIDTH), halo)
    vec = _const_spec((1, LRU_WIDTH))
    blk = _const_spec((LRU_BLOCKS, LRU_BLOCK_W, LRU_BLOCK_W))
    acc_blk = pl.BlockSpec((LRU_BLOCKS, LRU_BLOCK_W, LRU_BLOCK_W), lambda i: (0, 0, 0))
    hbm = pl.BlockSpec(memory_space=pl.ANY)
    res = pl.pallas_call(
        body, name="lru_core_bwd", grid=(nt,),
        in_specs=[pl.BlockSpec((ts, D_MODEL), rev),
                  _const_spec((LRU_WIDTH, D_MODEL)),
                  tile, halo_spec, tile, tile, halo_spec,
                  _const_spec((CONV_WIDTH, LRU_WIDTH)), vec, blk, vec, blk, vec, vec, blk, blk]
        + [hbm] * n_ex,
        out_specs=[pl.BlockSpec((ts, 2 * LRU_WIDTH), rev), acc_blk, acc_blk,
                   pl.BlockSpec((SUBLANES, LRU_WIDTH), lambda i: (0, 0))] + [hbm] * n_ex,
        out_shape=[jax.ShapeDtypeStruct((s, 2 * LRU_WIDTH), BF16),
                   jax.ShapeDtypeStruct((LRU_BLOCKS, LRU_BLOCK_W, LRU_BLOCK_W), F32),
                   jax.ShapeDtypeStruct((LRU_BLOCKS, LRU_BLOCK_W, LRU_BLOCK_W), F32),
                   jax.ShapeDtypeStruct((SUBLANES, LRU_WIDTH), F32)]
        + [jax.ShapeDtypeStruct(a.shape, a.dtype) for a in chip_sums],
        scratch_shapes=[pltpu.VMEM((SUBLANES, LRU_WIDTH), F32),
                        pltpu.VMEM((SUBLANES, LRU_WIDTH), F32),
                        pltpu.VMEM((SUBLANES, LRU_WIDTH), F32)] + _chip_exchange_sems(n_ex),
        compiler_params=_params(),
    )(dx1b, w_out, xb, xb, gate, hs, hs, cw, cb, wa, ba, wx, bx, a_param, wa_t, wx_t, *chip_sums)
    return res[0], res[1], res[2], res[3], res[4:]


def _lru_in_bwd(du, w_in, x, dx1, g0, ts):
    s = x.shape[0]

    def body(du_ref, w_ref, x_ref, dx1_ref, g_ref, gx_ref, gn_ref):
        @pl.when(pl.program_id(0) == 0)
        def _():
            gn_ref[...] = jnp.zeros_like(gn_ref)

        duv = du_ref[...]
        dh = _dot_nt(duv[:, 0:LRU_IN_SHARD], w_ref[0])
        for j in range(1, N_DEV):
            dh = dh + _dot_nt(duv[:, j * LRU_IN_SHARD:(j + 1) * LRU_IN_SHARD], w_ref[j])
        dxn, dgn = _norm_bwd(x_ref[...], g_ref[...], dh)
        gx_ref[...] = dx1_ref[...] + dxn
        gn_ref[...] += dgn

    tile = pl.BlockSpec((ts, D_MODEL), lambda i: (i, 0))
    return pl.pallas_call(
        body, name="lru_in_bwd", grid=(s // ts,),
        in_specs=[pl.BlockSpec((ts, 2 * LRU_WIDTH), lambda i: (i, 0)),
                  _const_spec((N_DEV, D_MODEL, LRU_IN_SHARD)), tile, tile,
                  _const_spec((1, D_MODEL))],
        out_specs=[tile, pl.BlockSpec((1, D_MODEL), lambda i: (0, 0))],
        out_shape=[jax.ShapeDtypeStruct((s, D_MODEL), F32),
                   jax.ShapeDtypeStruct((1, D_MODEL), F32)],
        compiler_params=_params(),
    )(du, w_in, x, dx1, g0)


def _weight_grad(a, b, ts, name, scale=1.0, col_shards=1):
    s, ka = a.shape
    nb = b.shape[1]
    nt = s // ts
    per = nb // col_shards

    def body(a_ref, b_ref, o_ref):
        @pl.when(pl.program_id(0) == 0)
        def _():
            o_ref[...] = jnp.zeros_like(o_ref)

        if col_shards == 1:
            o_ref[...] += _dot_tn(a_ref[...], b_ref[...])
        else:
            av, bv = a_ref[...], b_ref[...]
            for j in range(col_shards):
                o_ref[j] += _dot_tn(av, bv[:, j * per:(j + 1) * per])
        if scale != 1.0:
            @pl.when(pl.program_id(0) == nt - 1)
            def _():
                o_ref[...] = o_ref[...] * scale

    out_dims = (ka, nb) if col_shards == 1 else (col_shards, ka, per)
    return pl.pallas_call(
        body, name=name, grid=(nt,),
        in_specs=[pl.BlockSpec((ts, ka), lambda i: (i, 0)),
                  pl.BlockSpec((ts, nb), lambda i: (i, 0))],
        out_specs=pl.BlockSpec(out_dims, lambda i: (0,) * len(out_dims)),
        out_shape=jax.ShapeDtypeStruct(out_dims, F32),
        compiler_params=_params(),
    )(a, b)


def _sum_parts(gp_ref):
    g = gp_ref[0].astype(F32)
    for k in range(1, gp_ref.shape[0]):
        g = g + gp_ref[k].astype(F32)
    return g


def _adamw(g_parts, w, m, v, tr, name):
    nparts, rows, cols = g_parts.shape

    def body(gp_ref, w_ref, m_ref, v_ref, g_ref, d_ref, mo_ref, vo_ref):
        g = _sum_parts(gp_ref)
        m2 = ADAM_B1 * m_ref[...] + (1.0 - ADAM_B1) * g
        v2 = ADAM_B2 * v_ref[...] + (1.0 - ADAM_B2) * (g * g)
        m_hat = m2 / (1.0 - ADAM_B1 ** ADAM_STEP)
        v_hat = v2 / (1.0 - ADAM_B2 ** ADAM_STEP)
        g_ref[...] = g
        d_ref[...] = (-ADAM_LR) * (m_hat / (jnp.sqrt(v_hat) + ADAM_EPS) + ADAM_WD * w_ref[...])
        mo_ref[...] = m2
        vo_ref[...] = v2

    tile = pl.BlockSpec((tr, cols), lambda i: (i, 0))
    out = jax.ShapeDtypeStruct((rows, cols), F32)
    return pl.pallas_call(
        body, name=name, grid=(rows // tr,),
        in_specs=[pl.BlockSpec((nparts, tr, cols), lambda i: (0, i, 0)), tile, tile, tile],
        out_specs=[tile, tile, tile, tile],
        out_shape=[out, out, out, out],
        compiler_params=_params(),
    )(g_parts, w, m, v)


def _reduce_parts(g_parts, name):
    _, rows, cols = g_parts.shape

    def body(gp_ref, g_ref):
        g_ref[...] = _sum_parts(gp_ref)

    return pl.pallas_call(
        body, name=name,
        out_shape=jax.ShapeDtypeStruct((rows, cols), F32),
        compiler_params=pltpu.CompilerParams(vmem_limit_bytes=VMEM_LIMIT_BYTES),
    )(g_parts)


def _mesh_pos():
    ix, iy, ic = lax.axis_index("x"), lax.axis_index("y"), lax.axis_index("c")
    return ix, iy, ic


def _peer(ix, iy, ic, mask):
    px = 1 - ix if mask & 4 else ix
    py = 1 - iy if mask & 2 else iy
    pc = 1 - ic if mask & 1 else ic
    return (px, py, pc), 4 * px + 2 * py + pc


def _exchange(arrays, scatter, name):
    n = len(arrays)

    def body(*refs):
        x_refs, o_refs = refs[:n], refs[n:2 * n]
        send_sems, recv_sems, local_sems = refs[2 * n:]
        ix, iy, ic = _mesh_pos()
        me = 4 * ix + 2 * iy + ic

        def src(a, dest):
            return x_refs[a].at[dest] if scatter else x_refs[a]

        local = [pltpu.make_async_copy(src(a, me), o_refs[a].at[me], local_sems.at[a])
                 for a in range(n)]
        for cp in local:
            cp.start()
        sends = []
        for mask in range(1, N_DEV):
            peer, pidx = _peer(ix, iy, ic, mask)
            for a in range(n):
                cp = pltpu.make_async_remote_copy(
                    src_ref=src(a, pidx), dst_ref=o_refs[a].at[me],
                    send_sem=send_sems.at[a, mask - 1], recv_sem=recv_sems.at[a, mask - 1],
                    device_id=peer, device_id_type=pl.DeviceIdType.MESH)
                cp.start()
                sends.append(cp)
        for mask in range(1, N_DEV):
            peer, pidx = _peer(ix, iy, ic, mask)
            for a in range(n):
                pltpu.make_async_remote_copy(
                    src_ref=src(a, me), dst_ref=o_refs[a].at[pidx],
                    send_sem=send_sems.at[a, mask - 1], recv_sem=recv_sems.at[a, mask - 1],
                    device_id=peer, device_id_type=pl.DeviceIdType.MESH).wait_recv()
        for cp in sends:
            cp.wait_send()
        for cp in local:
            cp.wait()

    out_shape = [jax.ShapeDtypeStruct(x.shape if scatter else (N_DEV,) + x.shape, x.dtype)
                 for x in arrays]
    return pl.pallas_call(
        body, name=name,
        in_specs=[pl.BlockSpec(memory_space=pl.ANY)] * n,
        out_specs=[pl.BlockSpec(memory_space=pl.ANY)] * n,
        out_shape=out_shape,
        scratch_shapes=[pltpu.SemaphoreType.DMA((n, N_DEV - 1)),
                        pltpu.SemaphoreType.DMA((n, N_DEV - 1)),
                        pltpu.SemaphoreType.DMA((n,))],
    )(*arrays)


def _gather_two_level(arrays, name):
    n = len(arrays)

    def body(*refs):
        start, forward, finish = _gather_phases(refs[:n], refs[n:2 * n], *refs[2 * n:])
        start()
        forward()
        finish()

    return pl.pallas_call(
        body, name=name,
        in_specs=[pl.BlockSpec(memory_space=pl.ANY)] * n,
        out_specs=[pl.BlockSpec(memory_space=pl.ANY)] * n,
        out_shape=[jax.ShapeDtypeStruct((N_DEV,) + x.shape, x.dtype) for x in arrays],
        scratch_shapes=_gather_sems(n),
    )(*arrays)


def _gather_sems(n):
    return [pltpu.SemaphoreType.DMA((n, N_DEV - 1)), pltpu.SemaphoreType.DMA((n, N_DEV - 1)),
            pltpu.SemaphoreType.DMA((n,))]


def _gather_phases(x_refs, o_refs, send_sems, recv_sems, local_sems):
    n = len(x_refs)
    ix, iy, ic = _mesh_pos()
    me, sibling = (ix, iy, ic), (ix, iy, 1 - ic)
    chips = [(1 - ix, iy), (ix, 1 - iy), (1 - ix, 1 - iy)]

    def idx(px, py, pc):
        return 4 * px + 2 * py + pc

    def copy(a, k, block, to, src=None):
        dst = o_refs[a].at[idx(*block)]
        return pltpu.make_async_remote_copy(
            src_ref=dst if src is None else src, dst_ref=dst,
            send_sem=send_sems.at[a, k], recv_sem=recv_sems.at[a, k],
            device_id=to, device_id_type=pl.DeviceIdType.MESH)

    def local():
        return [pltpu.make_async_copy(x_refs[a], o_refs[a].at[idx(*me)], local_sems.at[a])
                for a in range(n)]

    def first():
        out = []
        for a in range(n):
            out.append(copy(a, 0, me, sibling, src=x_refs[a]))
            out += [copy(a, 1 + j, me, (*chip, ic), src=x_refs[a])
                    for j, chip in enumerate(chips)]
        return out

    def passed():
        return [copy(a, 4 + j, (*chip, ic), sibling)
                for j, chip in enumerate(chips) for a in range(n)]

    def start():
        for cp in local() + first():
            cp.start()

    def forward():
        for j, chip in enumerate(chips):
            for a in range(n):
                copy(a, 1 + j, (*chip, ic), me).wait_recv()
                copy(a, 4 + j, (*chip, ic), sibling).start()

    def finish():
        for a in range(n):
            copy(a, 0, sibling, me).wait_recv()
            for j, chip in enumerate(chips):
                copy(a, 4 + j, (*chip, 1 - ic), me).wait_recv()
        for cp in first() + passed():
            cp.wait_send()
        for cp in local():
            cp.wait()

    return start, forward, finish


def _swap_sibling(arrays, name):
    n = len(arrays)
    n_chips = N_DEV // 2

    def body(*refs):
        x_refs, got_refs = refs[:n], refs[n:2 * n]
        send_sems, recv_sems = refs[2 * n:]
        ix, iy, ic = _mesh_pos()
        sibling = (ix, iy, 1 - ic)
        sends = []
        for a in range(n):
            for q in range(n_chips):
                cp = pltpu.make_async_remote_copy(
                    src_ref=x_refs[a].at[q, 1 - ic], dst_ref=got_refs[a].at[q],
                    send_sem=send_sems.at[a, q], recv_sem=recv_sems.at[a, q],
                    device_id=sibling, device_id_type=pl.DeviceIdType.MESH)
                cp.start()
                sends.append(cp)
        for cp in sends:
            cp.wait()

    return pl.pallas_call(
        body, name=name,
        in_specs=[pl.BlockSpec(memory_space=pl.ANY)] * n,
        out_specs=[pl.BlockSpec(memory_space=pl.ANY)] * n,
        out_shape=[jax.ShapeDtypeStruct((n_chips,) + x.shape[2:], x.dtype) for x in arrays],
        scratch_shapes=[pltpu.SemaphoreType.DMA((n, n_chips)),
                        pltpu.SemaphoreType.DMA((n, n_chips))],
    )(*arrays)


def _exchange_chips(arrays, name):
    n = len(arrays)

    def body(*refs):
        start, finish = _chip_exchange_phases(refs[:n], refs[n:2 * n], *refs[2 * n:])
        start()
        finish()

    return pl.pallas_call(
        body, name=name,
        in_specs=[pl.BlockSpec(memory_space=pl.ANY)] * n,
        out_specs=[pl.BlockSpec(memory_space=pl.ANY)] * n,
        out_shape=[jax.ShapeDtypeStruct(x.shape, x.dtype) for x in arrays],
        scratch_shapes=_chip_exchange_sems(n),
    )(*arrays)


def _chip_exchange_sems(n):
    n_chips = N_DEV // 2
    return [pltpu.SemaphoreType.DMA((n, n_chips - 1)), pltpu.SemaphoreType.DMA((n, n_chips - 1)),
            pltpu.SemaphoreType.DMA((n,))]


def _chip_exchange_phases(x_refs, o_refs, send_sems, recv_sems, local_sems):
    n = len(x_refs)
    n_chips = N_DEV // 2
    ix, iy, ic = _mesh_pos()
    my_chip = 2 * ix + iy

    def peers():
        for mask in range(1, n_chips):
            px = 1 - ix if mask & 2 else ix
            py = 1 - iy if mask & 1 else iy
            yield mask, (px, py, ic), 2 * px + py

    def local():
        return [pltpu.make_async_copy(x_refs[a].at[my_chip], o_refs[a].at[my_chip],
                                      local_sems.at[a]) for a in range(n)]

    def sends():
        return [pltpu.make_async_remote_copy(
            src_ref=x_refs[a].at[chip], dst_ref=o_refs[a].at[my_chip],
            send_sem=send_sems.at[a, mask - 1], recv_sem=recv_sems.at[a, mask - 1],
            device_id=peer, device_id_type=pl.DeviceIdType.MESH)
            for mask, peer, chip in peers() for a in range(n)]

    def start():
        for cp in local() + sends():
            cp.start()

    def finish():
        for mask, peer, chip in peers():
            for a in range(n):
                pltpu.make_async_remote_copy(
                    src_ref=x_refs[a].at[my_chip], dst_ref=o_refs[a].at[chip],
                    send_sem=send_sems.at[a, mask - 1], recv_sem=recv_sems.at[a, mask - 1],
                    device_id=peer, device_id_type=pl.DeviceIdType.MESH).wait_recv()
        for cp in sends():
            cp.wait_send()
        for cp in local():
            cp.wait()

    return start, finish


def _pair_sum(core, x, got, name):
    nq, rows, cols = got.shape

    def body(c_ref, x_ref, g_ref, o_ref):
        o_ref[...] = (x_ref[...] + g_ref[...]).astype(BF16)

    blk = pl.BlockSpec((None, rows, cols), lambda q, c: (q, 0, 0))
    return pl.pallas_call(
        body, name=name,
        grid_spec=pltpu.PrefetchScalarGridSpec(
            num_scalar_prefetch=1, grid=(nq,),
            in_specs=[pl.BlockSpec((None, None, rows, cols), lambda q, c: (q, c[0], 0, 0)), blk],
            out_specs=blk),
        out_shape=jax.ShapeDtypeStruct(got.shape, BF16),
        compiler_params=_params(),
    )(core, x, got)


def _selectors():
    r = lax.broadcasted_iota(jnp.int32, (LANES, FOX_PAD), 0)
    c = lax.broadcasted_iota(jnp.int32, (LANES, FOX_PAD), 1)
    part, head_r = r // HEADS, r % HEADS
    head_c, lane_c = c // HEAD_PAD, c % HEAD_PAD
    same = (head_r == head_c) & (part < 3)
    sel_q = jnp.where(same & (lane_c == LANE_RB + part), 1.0, 0.0)
    sel_k = jnp.where(same & (lane_c == LANE_CK + part), -1.0, 0.0)
    sel = jnp.stack([sel_q, sel_k, jnp.zeros_like(sel_q)]).astype(BF16)
    lane = lax.broadcasted_iota(jnp.int32, (1, FOX_PAD), 1) % HEAD_PAD
    ones_q = jnp.where((lane >= LANE_CK) & (lane < LANE_CK + 3), 1.0, 0.0)
    ones_k = jnp.where(((lane >= LANE_RB) & (lane < LANE_RB + 3))
                       | ((lane >= LANE_LSE) & (lane < LANE_LSE + 3)), 1.0, 0.0)
    ones_v = jnp.where((lane >= LANE_ONE_V) & (lane < LANE_ONE_V + 2), 1.0, 0.0)
    bias = jnp.stack([ones_q, ones_k, ones_v]).astype(F32)
    return sel, bias


def _chip_sums(names, send):
    send = [a.reshape((N_DEV // 2, 2) + a.shape[1:]) for a in send]
    got = _swap_sibling(send, "swap_" + names[0])
    core = lax.axis_index("c").astype(jnp.int32).reshape(1)
    return [_pair_sum(core, a, b, "pair_sum_" + n) for n, a, b in zip(names, send, got)]


def _local_step(x, target, norm_g, final_g, w_in8, conv_w, conv_b, wa, ba, wx, bx, a_param,
                w_out_b, fox_in_shard, b_f, fox_out_shard, blk=512, ts=256):
    qk_scale = 1.0 / (HEAD_DIM ** 0.5)
    g0, g1 = norm_g[0:1], norm_g[1:2]
    gf = final_g.reshape(1, D_MODEL)
    wa_b, wx_b = wa.astype(BF16), wx.astype(BF16)
    sel, bias = _selectors()

    xb, gate1, h0, (fox_in8, fox_out8) = _lru_in_fwd(x, g0, w_in8, [fox_in_shard, fox_out_shard],
                                                     ts)
    fox_w_in = jnp.transpose(fox_in8, (1, 0, 2)).reshape(D_MODEL, FOX_IN_COLS)
    width = HEADS * HEAD_DIM
    w4 = jnp.stack([fox_w_in[:, 0:width] * qk_scale, fox_w_in[:, width:2 * width],
                    fox_w_in[:, 2 * width:3 * width], fox_w_in[:, 3 * width:4 * width]])
    wf_b = jnp.pad(fox_w_in[:, 4 * width:], ((0, 0), (0, LANES - HEADS)))
    bf_pad = jnp.pad(b_f, ((0, 0), (0, LANES - HEADS)))
    fo_b = fox_out8.reshape(width, D_MODEL)
    y1, hs = _lru_core_fwd(xb, gate1, conv_w, conv_b, wa_b, ba, wx_b, bx, a_param, ts)
    x1, h1, f, cparts = _fox_pre_fwd(x, y1, w_out_b, g1, wf_b, bf_pad, ts)
    qkv = _fox_proj_fwd(h1, cparts, w4[0:3], sel, bias, BF16, ts, "fox_proj_qkv")
    gate2 = _fox_proj_fwd(h1, None, w4[3:4], None, None, F32, ts, "fox_proj_gate")[0]
    o, qb = _attn_fwd(qkv, blk, hps=4)
    dx2, dx2b, y2, loss_acc, g_final = _fox_out_loss(o, gate2, fo_b, x1, target, gf, ts)

    do, dgate2 = _fox_out_bwd(dx2b, fo_b, o, gate2, ts)
    dq, dk, dv, dcum = _attn_bwd(qb, qkv, do, blk)
    dx1, dx1b, df, du4, g_norm1, g_bf = _fox_in_bwd(dq, dk, dv, dgate2, w4, wf_b, dcum, f, x1,
                                                    dx2, g1, ts)
    tw = 512
    g_q = _weight_grad(h1, du4[0], tw, "grad_fox_wq", scale=qk_scale)
    g_k = _weight_grad(h1, du4[1], tw, "grad_fox_wk")
    g_v = _weight_grad(h1, du4[2], tw, "grad_fox_wv")
    g_g = _weight_grad(h1, du4[3], tw, "grad_fox_wg")
    g_f = _weight_grad(h1, df, tw, "grad_fox_wf")
    g_fox_w_in = jnp.concatenate([g_q, g_k, g_v, g_g, g_f[:, :HEADS]], axis=1)
    g_fox_w_in = jnp.transpose(g_fox_w_in.reshape(D_MODEL, N_DEV, FOX_IN_SHARD), (1, 0, 2))
    g_fox_w_out = _weight_grad(y2, dx2b, tw, "grad_fox_w_out")
    fox_sums = _chip_sums(("fox_w_in", "fox_w_out"),
                          [g_fox_w_in, g_fox_w_out.reshape(N_DEV, -1, D_MODEL)])

    du, g_wa, g_wx, g_vec, (r_fox_in, r_fox_out) = _lru_core_bwd(
        dx1b, w_out_b, xb, gate1, hs, conv_w, conv_b, wa_b, ba, wx_b, bx, a_param,
        jnp.transpose(wa_b, (0, 2, 1)), jnp.transpose(wx_b, (0, 2, 1)), fox_sums, ts)
    grad_x, g_norm0 = _lru_in_bwd(du, w_in8, x, dx1, g0, ts)
    g_lru_w_in = _weight_grad(h0, du, tw, "grad_lru_w_in", col_shards=N_DEV)
    g_lru_w_out = _weight_grad(y1, dx1b, tw, "grad_lru_w_out")

    grads = dict(
        norm_g=jnp.concatenate([g_norm0, g_norm1], axis=0), final_g=g_final[0],
        lru_w_in=g_lru_w_in, lru_conv_w=g_vec[0:4], lru_conv_b=g_vec[4:5], lru_wa=g_wa,
        lru_ba=g_vec[5:6], lru_wx=g_wx, lru_bx=g_vec[6:7], lru_a_param=g_vec[7:8],
        lru_w_out=g_lru_w_out, fox_b_f=g_bf[:, :HEADS])
    return loss_acc[0, 0], grad_x, grads, r_fox_in, r_fox_out


SMALL =("norm_g", "final_g", "lru_conv_b", "lru_wa", "lru_ba", "lru_wx", "lru_bx", "lru_a_param",
         "fox_b_f")
ALL_WEIGHTS = ("norm_g", "final_g", "lru_w_in", "lru_conv_w", "lru_conv_b", "lru_wa", "lru_ba",
               "lru_wx", "lru_bx", "lru_a_param", "lru_w_out", "fox_w_in", "fox_b_f", "fox_w_out")


def _pack_small(d):
    rows = []
    for n in SMALL:
        a = d[n].reshape(-1)
        if a.shape[0] % LANES:
            a = jnp.pad(a, (0, LANES - a.shape[0] % LANES))
        rows.append(a.reshape(-1, LANES))
    packed = jnp.concatenate(rows, axis=0)
    return jnp.pad(packed, ((0, N_DEV * SMALL_CHUNK_ROWS - packed.shape[0]), (0, 0)))


def _unpack_small(packed, like):
    out, off = {}, 0
    for n, nrows in zip(SMALL, SMALL_ROWS):
        size = like[n].size
        out[n] = packed[off:off + nrows].reshape(-1)[:size].reshape(like[n].shape)
        off += nrows
    return out


def kernel(x, norm_g, final_g, lru_w_in, lru_conv_w, lru_conv_b, lru_wa, lru_ba, lru_wx, lru_bx, lru_a_param, lru_w_out, fox_w_in, fox_b_f, fox_w_out, loss_target, m_norm_g, m_final_g, m_lru_w_in, m_lru_conv_w, m_lru_conv_b, m_lru_wa, m_lru_ba, m_lru_wx, m_lru_bx, m_lru_a_param, m_lru_w_out, m_fox_w_in, m_fox_b_f, m_fox_w_out, v_norm_g, v_final_g, v_lru_w_in, v_lru_conv_w, v_lru_conv_b, v_lru_wa, v_lru_ba, v_lru_wx, v_lru_bx, v_lru_a_param, v_lru_w_out, v_fox_w_in, v_fox_b_f, v_fox_w_out):
    w_loc = dict(norm_g=norm_g, final_g=final_g, lru_w_in=lru_w_in, lru_conv_w=lru_conv_w,
                 lru_conv_b=lru_conv_b, lru_wa=lru_wa, lru_ba=lru_ba, lru_wx=lru_wx, lru_bx=lru_bx,
                 lru_a_param=lru_a_param, lru_w_out=lru_w_out, fox_w_in=fox_w_in, fox_b_f=fox_b_f,
                 fox_w_out=fox_w_out)
    m_loc = dict(norm_g=m_norm_g, final_g=m_final_g, lru_w_in=m_lru_w_in, lru_conv_w=m_lru_conv_w,
                 lru_conv_b=m_lru_conv_b, lru_wa=m_lru_wa, lru_ba=m_lru_ba, lru_wx=m_lru_wx,
                 lru_bx=m_lru_bx, lru_a_param=m_lru_a_param, lru_w_out=m_lru_w_out,
                 fox_w_in=m_fox_w_in, fox_b_f=m_fox_b_f, fox_w_out=m_fox_w_out)
    v_loc = dict(norm_g=v_norm_g, final_g=v_final_g, lru_w_in=v_lru_w_in, lru_conv_w=v_lru_conv_w,
                 lru_conv_b=v_lru_conv_b, lru_wa=v_lru_wa, lru_ba=v_lru_ba, lru_wx=v_lru_wx,
                 lru_bx=v_lru_bx, lru_a_param=v_lru_a_param, lru_w_out=v_lru_w_out,
                 fox_w_in=v_fox_w_in, fox_b_f=v_fox_b_f, fox_w_out=v_fox_w_out)

    w_in8, conv8, w_out8 = _gather_two_level(
        [lru_w_in[0].astype(BF16), lru_conv_w[0], lru_w_out[0].astype(BF16)], "gather_weights")
    conv_full = jnp.transpose(conv8, (1, 0, 2)).reshape(CONV_WIDTH, LRU_WIDTH)

    loss, grad_x, grads, r_fox_in, r_fox_out = _local_step(
        x[0], loss_target[0], norm_g, final_g, w_in8, conv_full, lru_conv_b, lru_wa[0], lru_ba,
        lru_wx[0], lru_bx, lru_a_param, w_out8.reshape(LRU_WIDTH, D_MODEL),
        fox_w_in[0].astype(BF16), fox_b_f, fox_w_out[0].astype(BF16))

    conv_send = jnp.transpose(grads["lru_conv_w"].reshape(CONV_WIDTH, N_DEV, -1), (1, 0, 2))
    chip_sums = _chip_sums(
        ("lru_w_in", "lru_conv_w", "lru_w_out", "small"),
        [grads["lru_w_in"], conv_send, grads["lru_w_out"].reshape(N_DEV, -1, D_MODEL),
         _pack_small(grads).reshape(N_DEV, SMALL_CHUNK_ROWS, LANES)])
    r_w_in, r_conv, r_w_out, r_small = _exchange_chips(chip_sums, "scatter_grads")

    out = {}
    for n, recv, tr in (("lru_w_in", r_w_in, 256), ("lru_conv_w", r_conv, CONV_WIDTH),
                        ("lru_w_out", r_w_out, 96), ("fox_w_in", r_fox_in, 128),
                        ("fox_w_out", r_fox_out, 64)):
        res = _adamw(recv, w_loc[n][0], m_loc[n][0], v_loc[n][0], tr, "adamw_" + n)
        out[n] = [a[None] for a in res]

    g_chunk = _reduce_parts(r_small, "reduce_small_grads")
    g_small, = _exchange([g_chunk], False, "gather_small_grads")
    g_small = g_small.reshape(1, N_DEV * SMALL_CHUNK_ROWS, LANES)
    res = _adamw(g_small, _pack_small(w_loc), _pack_small(m_loc), _pack_small(v_loc),
                 N_DEV * SMALL_CHUNK_ROWS, "adamw_replicated")
    small_out = [_unpack_small(a, w_loc) for a in res]
    for n in SMALL:
        out[n] = [d[n] for d in small_out]

    loss = lax.psum(loss, ("x", "y", "c"))
    return (loss, grad_x[None], *[out[n][0] for n in ALL_WEIGHTS], *[out[n][1] for n in ALL_WEIGHTS],
            *[out[n][2] for n in ALL_WEIGHTS], *[out[n][3] for n in ALL_WEIGHTS])
```

```python
import functools

import jax
import jax.numpy as jnp
from jax import lax
from jax.experimental import pallas as pl
from jax.experimental.pallas import tpu as pltpu

F32 = jnp.float32
BF16 = jnp.bfloat16

D_MODEL = 1024
LRU_WIDTH = 1536
LRU_BLOCKS = 12
LRU_BLOCK_W = 128
CONV_WIDTH = 4
LRU_C = 8.0
HEADS = 16
HEAD_DIM = 64
HEAD_PAD = 128
FOX_PAD = HEADS * HEAD_PAD
HEADS_PER_STEP = 2
EPS = 1e-6
NEG_BIG = -1e30
N_DEV = 8

ADAM_LR = 0.001
ADAM_B1 = 0.9
ADAM_B2 = 0.999
ADAM_EPS = 1e-08
ADAM_WD = 0.01
ADAM_STEP = 10

LANE_RB = 64
LANE_CK = 67
LANE_LSE = 70
LANE_ONE_V = 64

VMEM_LIMIT_BYTES = 56 * 1024 * 1024
LANES = 128
SUBLANES = 8

LRU_IN_SHARD = 2 * LRU_WIDTH // N_DEV
FOX_IN_COLS = 4 * HEADS * HEAD_DIM + HEADS
FOX_IN_SHARD = FOX_IN_COLS // N_DEV

SMALL_ROWS = (16, 8, 12, 1536, 12, 1536, 12, 12, 1)
SMALL_CHUNK_ROWS = 400
assert sum(SMALL_ROWS) <= N_DEV * SMALL_CHUNK_ROWS


def _params(n_grid_axes=1):
    return pltpu.CompilerParams(
        dimension_semantics=("arbitrary",) * n_grid_axes,
        vmem_limit_bytes=VMEM_LIMIT_BYTES)


def _const_spec(shape):
    nd = len(shape)
    return pl.BlockSpec(shape, lambda *_: (0,) * nd, pipeline_mode=pl.Buffered(1))


def _shift_down(x, k, fill):
    rows = lax.broadcasted_iota(jnp.int32, x.shape, 0)
    return jnp.where(rows >= k, pltpu.roll(x, k, 0), fill)


def _shift_up(x, k, fill):
    n = x.shape[0]
    rows = lax.broadcasted_iota(jnp.int32, x.shape, 0)
    return jnp.where(rows < n - k, pltpu.roll(x, n - k, 0), fill)


def _scan_rows(a, b, reverse=False):
    n = a.shape[0]
    shift = _shift_up if reverse else _shift_down
    k = 1
    while k < n:
        b = a * shift(b, k, 0.0) + b
        a = a * shift(a, k, 1.0)
        k *= 2
    return a, b


def _cumsum_rows(x, reverse=False):
    n = x.shape[0]
    shift = _shift_up if reverse else _shift_down
    k = 1
    while k < n:
        x = x + shift(x, k, 0.0)
        k *= 2
    return x


def _rstd(x):
    return lax.rsqrt(jnp.mean(x * x, axis=-1, keepdims=True) + EPS)


def _norm_bwd(x, g, dh):
    rstd = _rstd(x)
    xhat = x * rstd
    dg = jnp.sum(dh * xhat, axis=0, keepdims=True)
    dxh = dh * g
    dx = rstd * (dxh - xhat * jnp.mean(dxh * xhat, axis=-1, keepdims=True))
    return dx, dg


def _split3(x):
    hi = x.astype(BF16)
    r1 = x - hi.astype(F32)
    mid = r1.astype(BF16)
    lo = (r1 - mid.astype(F32)).astype(BF16)
    return hi, mid, lo


def _sigmoid(x):
    return jax.nn.sigmoid(x)


def _dot(a, b):
    return jnp.dot(a, b, preferred_element_type=F32)


def _dot_nt(a, b):
    return lax.dot_general(a, b, (((1,), (1,)), ((), ())), preferred_element_type=F32)


def _dot_tn(a, b):
    return lax.dot_general(a, b, (((0,), (0,)), ((), ())), preferred_element_type=F32)


def _heads_to_padded(u):
    n = u.shape[0]
    low = lax.broadcasted_iota(jnp.int32, (n, LANES), 1) < HEAD_DIM
    zero = jnp.zeros((n, LANES), u.dtype)
    cols = []
    for p in range(HEADS // 2):
        pair = u[:, p * LANES:(p + 1) * LANES]
        cols.append(jnp.where(low, pair, zero))
        cols.append(jnp.where(low, pltpu.roll(pair, HEAD_DIM, 1), zero))
    return jnp.concatenate(cols, axis=1)


def _heads_from_padded(x):
    n = x.shape[0]
    low = lax.broadcasted_iota(jnp.int32, (n, LANES), 1) < HEAD_DIM
    cols = []
    for p in range(HEADS // 2):
        even = x[:, (2 * p) * HEAD_PAD:(2 * p + 1) * HEAD_PAD]
        odd = x[:, (2 * p + 1) * HEAD_PAD:(2 * p + 2) * HEAD_PAD]
        cols.append(jnp.where(low, even, pltpu.roll(odd, HEAD_DIM, 1)))
    return jnp.concatenate(cols, axis=1)


def _conv_taps(xb, prev8):
    rows8 = lax.broadcasted_iota(jnp.int32, prev8.shape, 0)
    taps = [xb]
    for j in range(1, CONV_WIDTH):
        r = pltpu.roll(xb, j, 0)
        p = pltpu.roll(prev8, j, 0)
        head = jnp.where(rows8 < j, p, r[0:SUBLANES])
        taps.append(jnp.concatenate([head, r[SUBLANES:]], axis=0))
    return taps


def _lru_pre(taps, cw, cb, wa_ref, ba, wx_ref, bx, a_param):
    xc = cb + cw[3:4] * taps[0] + cw[2:3] * taps[1] + cw[1:2] * taps[2] + cw[0:1] * taps[3]
    xcb = xc.astype(BF16)
    ra, ia = [], []
    for n in range(LRU_BLOCKS):
        blk = xcb[:, n * LRU_BLOCK_W:(n + 1) * LRU_BLOCK_W]
        ra.append(_dot(blk, wa_ref[n]))
        ia.append(_dot(blk, wx_ref[n]))
    r = _sigmoid(jnp.concatenate(ra, axis=1) + ba)
    i = _sigmoid(jnp.concatenate(ia, axis=1) + bx)
    z = -a_param
    sp = jnp.maximum(z, 0.0) + jnp.log1p(jnp.exp(-jnp.abs(z)))
    log_a = (-LRU_C) * r * sp
    a = jnp.exp(log_a)
    one_minus_a2 = -jnp.tanh(log_a) * (a * a + 1.0)
    mult = jnp.sqrt(one_minus_a2)
    return xc, xcb, r, i, sp, a, mult


def _lru_in_fwd(x, g0, w_in, later_shards, ts):
    s = x.shape[0]
    nt = s // ts
    n = len(later_shards)

    def body(*refs):
        x_ref, g_ref, w_ref = refs[:3]
        shard_refs = refs[3:3 + n]
        xb_ref, gate_ref, h_ref = refs[3 + n:6 + n]
        wfull_ref = refs[6 + 2 * n]
        start, forward, finish = _gather_phases(shard_refs, refs[6 + n:6 + 2 * n],
                                                *refs[7 + 2 * n:])
        step = pl.program_id(0)
        pl.when(step == 0)(start)

        @pl.when(step == 0)
        def _():
            for j in range(N_DEV):
                wfull_ref[:, j * LRU_IN_SHARD:(j + 1) * LRU_IN_SHARD] = w_ref[j]

        xv = x_ref[...]
        h = (xv * _rstd(xv) * g_ref[...]).astype(BF16)
        u = _dot(h, wfull_ref[...])
        xb_ref[...] = u[:, :LRU_WIDTH]
        gate_ref[...] = u[:, LRU_WIDTH:]
        h_ref[...] = h
        pl.when(step == (2 * nt) // 3)(forward)
        pl.when(step == nt - 1)(finish)

    hbm = pl.BlockSpec(memory_space=pl.ANY)
    res = pl.pallas_call(
        body, name="lru_in_fwd", grid=(nt,),
        in_specs=[pl.BlockSpec((ts, D_MODEL), lambda i: (i, 0)),
                  _const_spec((1, D_MODEL)),
                  _const_spec((N_DEV, D_MODEL, LRU_IN_SHARD))] + [hbm] * n,
        out_specs=[pl.BlockSpec((ts, LRU_WIDTH), lambda i: (i, 0)),
                   pl.BlockSpec((ts, LRU_WIDTH), lambda i: (i, 0)),
                   pl.BlockSpec((ts, D_MODEL), lambda i: (i, 0))] + [hbm] * n,
        out_shape=[jax.ShapeDtypeStruct((s, LRU_WIDTH), F32),
                   jax.ShapeDtypeStruct((s, LRU_WIDTH), F32),
                   jax.ShapeDtypeStruct((s, D_MODEL), BF16)]
        + [jax.ShapeDtypeStruct((N_DEV,) + a.shape, a.dtype) for a in later_shards],
        scratch_shapes=[pltpu.VMEM((D_MODEL, 2 * LRU_WIDTH), BF16)] + _gather_sems(n),
        compiler_params=_params(),
    )(x, g0, w_in, *later_shards)
    return res[0], res[1], res[2], res[3:]


def _lru_core_fwd(xb, gate, cw, cb, wa, ba, wx, bx, a_param, ts):
    s = xb.shape[0]

    def body(xb_ref, gate_ref, cw_ref, cb_ref, wa_ref, ba_ref, wx_ref, bx_ref, ap_ref,
             y_ref, hs_ref, prev_ref, hcar_ref):
        @pl.when(pl.program_id(0) == 0)
        def _():
            prev_ref[...] = jnp.zeros_like(prev_ref)
            hcar_ref[...] = jnp.zeros_like(hcar_ref)

        xbv = xb_ref[...]
        taps = _conv_taps(xbv, prev_ref[...])
        xc, _, _, i, _, a, mult = _lru_pre(taps, cw_ref[...], cb_ref[...], wa_ref, ba_ref[...],
                                           wx_ref, bx_ref[...], ap_ref[...])
        bterm = mult * (i * xc)
        cum_a, hloc = _scan_rows(a, bterm)
        hs = cum_a * hcar_ref[SUBLANES - 1:SUBLANES, :] + hloc
        gv = gate_ref[...]
        y_ref[...] = (hs * (gv * _sigmoid(gv))).astype(BF16)
        hs_ref[...] = hs
        prev_ref[...] = xbv[ts - SUBLANES:, :]
        hcar_ref[...] = hs[ts - SUBLANES:, :]

    vec = _const_spec((1, LRU_WIDTH))
    blk = _const_spec((LRU_BLOCKS, LRU_BLOCK_W, LRU_BLOCK_W))
    tile = pl.BlockSpec((ts, LRU_WIDTH), lambda i: (i, 0))
    return pl.pallas_call(
        body, name="lru_core_fwd", grid=(s // ts,),
        in_specs=[tile, tile, _const_spec((CONV_WIDTH, LRU_WIDTH)), vec, blk, vec, blk, vec, vec],
        out_specs=[tile, tile],
        out_shape=[jax.ShapeDtypeStruct((s, LRU_WIDTH), BF16),
                   jax.ShapeDtypeStruct((s, LRU_WIDTH), F32)],
        scratch_shapes=[pltpu.VMEM((SUBLANES, LRU_WIDTH), F32),
                        pltpu.VMEM((SUBLANES, LRU_WIDTH), F32)],
        compiler_params=_params(),
    )(xb, gate, cw, cb, wa, ba, wx, bx, a_param)


def _fox_pre_fwd(x, y, w_out, g1, wf, bf, ts):
    s = x.shape[0]

    def body(x_ref, y_ref, w_ref, g_ref, wf_ref, bf_ref, x1_ref, h1_ref, f_ref, cp_ref, ccar_ref):
        @pl.when(pl.program_id(0) == 0)
        def _():
            ccar_ref[...] = jnp.zeros_like(ccar_ref)

        x1 = x_ref[...] + _dot(y_ref[...], w_ref[...])
        h1 = (x1 * _rstd(x1) * g_ref[...]).astype(BF16)
        f = _dot(h1, wf_ref[...]) + bf_ref[...]
        logsig = jnp.minimum(f, 0.0) - jnp.log1p(jnp.exp(-jnp.abs(f)))
        cum = _cumsum_rows(logsig) + ccar_ref[SUBLANES - 1:SUBLANES, :]
        hi, mid, lo = _split3(cum)
        lane = lax.broadcasted_iota(jnp.int32, cum.shape, 1)
        packed = jnp.where(lane < HEADS, hi.astype(F32), jnp.where(
            lane < 2 * HEADS, pltpu.roll(mid.astype(F32), HEADS, 1), jnp.where(
                lane < 3 * HEADS, pltpu.roll(lo.astype(F32), 2 * HEADS, 1), 0.0)))
        x1_ref[...] = x1
        h1_ref[...] = h1
        f_ref[...] = f
        cp_ref[...] = packed.astype(BF16)
        ccar_ref[...] = cum[ts - SUBLANES:, :]

    return pl.pallas_call(
        body, name="fox_pre_fwd", grid=(s // ts,),
        in_specs=[pl.BlockSpec((ts, D_MODEL), lambda i: (i, 0)),
                  pl.BlockSpec((ts, LRU_WIDTH), lambda i: (i, 0)),
                  _const_spec((LRU_WIDTH, D_MODEL)),
                  _const_spec((1, D_MODEL)),
                  _const_spec((D_MODEL, LANES)),
                  _const_spec((1, LANES))],
        out_specs=[pl.BlockSpec((ts, D_MODEL), lambda i: (i, 0)),
                   pl.BlockSpec((ts, D_MODEL), lambda i: (i, 0)),
                   pl.BlockSpec((ts, LANES), lambda i: (i, 0)),
                   pl.BlockSpec((ts, LANES), lambda i: (i, 0))],
        out_shape=[jax.ShapeDtypeStruct((s, D_MODEL), F32),
                   jax.ShapeDtypeStruct((s, D_MODEL), BF16),
                   jax.ShapeDtypeStruct((s, LANES), F32),
                   jax.ShapeDtypeStruct((s, LANES), BF16)],
        scratch_shapes=[pltpu.VMEM((SUBLANES, LANES), F32)],
        compiler_params=_params(),
    )(x, y, w_out, g1, wf, bf)


def _fox_proj_fwd(h1, cparts, w, sel, bias, out_dtype, ts, name):
    s = h1.shape[0]
    ng = w.shape[0]
    width = HEADS * HEAD_DIM
    use_sel = sel is not None

    def body(*refs):
        if use_sel:
            h_ref, cp_ref, w_ref, sel_ref, b_ref, o_ref = refs
            acc = (_heads_to_padded(_dot(h_ref[...], w_ref[...]))
                   + _dot(cp_ref[...], sel_ref[...]) + b_ref[...])
        else:
            h_ref, w_ref, o_ref = refs
            acc = _heads_to_padded(_dot(h_ref[...], w_ref[...]))
        o_ref[...] = acc.astype(out_dtype)

    in_specs = [pl.BlockSpec((ts, D_MODEL), lambda j, i: (i, 0))]
    args = [h1]
    if use_sel:
        in_specs.append(pl.BlockSpec((ts, LANES), lambda j, i: (i, 0)))
        args.append(cparts)
    in_specs.append(pl.BlockSpec((None, D_MODEL, width), lambda j, i: (j, 0, 0)))
    args.append(w)
    if use_sel:
        in_specs.append(pl.BlockSpec((None, LANES, FOX_PAD), lambda j, i: (j, 0, 0)))
        in_specs.append(pl.BlockSpec((None, 1, FOX_PAD), lambda j, i: (j, 0, 0)))
        args += [sel, bias]
    return pl.pallas_call(
        body, name=name, grid=(ng, s // ts),
        in_specs=in_specs,
        out_specs=pl.BlockSpec((None, ts, FOX_PAD), lambda j, i: (j, i, 0)),
        out_shape=jax.ShapeDtypeStruct((ng, s, FOX_PAD), out_dtype),
        compiler_params=_params(2),
    )(*args)


def _attn_fwd(qkv, blk, hps=HEADS_PER_STEP):
    s = qkv.shape[1]
    nblk = s // blk
    wide = 2 * blk
    heads = [slice(i * HEAD_PAD, (i + 1) * HEAD_PAD) for i in range(hps)]

    def body(q_ref, k_ref, v_ref, o_ref, qb_ref, acc_ref, m_ref):
        qi = pl.program_id(1)
        row = lax.broadcasted_iota(jnp.int32, (blk, blk), 0)
        col = lax.broadcasted_iota(jnp.int32, (blk, blk), 1)
        lane = lax.broadcasted_iota(jnp.int32, (blk, HEAD_PAD), 1)
        qs = [q_ref[:, hd] for hd in heads]
        for i in range(hps):
            acc_ref[i] = jnp.zeros((blk, HEAD_PAD), F32)
            m_ref[i] = jnp.full((blk, HEAD_PAD), NEG_BIG, F32)

        def step(k0, size, masked):
            scores = [_dot_nt(q, k_ref[pl.ds(k0, size), hd]) for q, hd in zip(qs, heads)]
            for i, (sc, hd) in enumerate(zip(scores, heads)):
                v = v_ref[pl.ds(k0, size), hd]
                if masked:
                    sc = jnp.where(col <= row, sc, NEG_BIG)
                m = m_ref[i]
                m_new = jnp.maximum(m, jnp.max(sc, axis=-1, keepdims=True))
                p = jnp.exp((sc - jnp.tile(m_new, (1, size // HEAD_PAD))).astype(BF16))
                acc_ref[i] = jnp.exp(m - m_new) * acc_ref[i] + _dot(p, v)
                m_ref[i] = m_new

        def wide_step(kk, _):
            step(pl.multiple_of(kk * wide, wide), wide, False)
            return 0

        lax.fori_loop(0, qi // 2, wide_step, 0)

        @pl.when(qi % 2 == 1)
        def _():
            step(pl.multiple_of((qi - 1) * blk, blk), blk, False)

        step(pl.multiple_of(qi * blk, blk), blk, True)
        for i, (q, hd) in enumerate(zip(qs, heads)):
            acc = acc_ref[i]
            l = jnp.broadcast_to(acc[:, LANE_ONE_V:LANE_ONE_V + 1], (blk, HEAD_PAD))
            o_ref[:, hd] = (acc / l).astype(BF16)
            hi, mid, lo = _split3(-(m_ref[i] + jnp.log(l)))
            qb_ref[:, hd] = jnp.where(lane == LANE_LSE, hi, jnp.where(
                lane == LANE_LSE + 1, mid, jnp.where(lane == LANE_LSE + 2, lo, q)))

    width = hps * HEAD_PAD

    def whole(j):
        return pl.BlockSpec((None, s, width), lambda h, i: (j, 0, h))

    out_spec = pl.BlockSpec((blk, width), lambda h, i: (i, h))
    return pl.pallas_call(
        body, name="attn_fwd", grid=(HEADS // hps, nblk),
        in_specs=[pl.BlockSpec((None, blk, width), lambda h, i: (0, i, h)), whole(1), whole(2)],
        out_specs=[out_spec, out_spec],
        out_shape=[jax.ShapeDtypeStruct((s, FOX_PAD), BF16),
                   jax.ShapeDtypeStruct((s, FOX_PAD), BF16)],
        scratch_shapes=[pltpu.VMEM((hps, blk, HEAD_PAD), F32),
                        pltpu.VMEM((hps, blk, HEAD_PAD), F32)],
        compiler_params=_params(2),
    )(qkv, qkv, qkv)


def _fox_out_loss(o, gate, w_out, x1, target, gf, ts):
    s = x1.shape[0]

    def body(o_ref, gt_ref, w_ref, x1_ref, t_ref, g_ref, dx2_ref, dx2b_ref, y2_ref, loss_ref,
             gfin_ref):
        @pl.when(pl.program_id(0) == 0)
        def _():
            loss_ref[...] = jnp.zeros_like(loss_ref)
            gfin_ref[...] = jnp.zeros_like(gfin_ref)

        gv = gt_ref[...]
        y2 = _heads_from_padded(o_ref[...] * (gv * _sigmoid(gv))).astype(BF16)
        x2 = x1_ref[...] + _dot(y2, w_ref[...])
        rstd = _rstd(x2)
        xhat = x2 * rstd
        g = g_ref[...]
        diff = xhat * g - t_ref[...]
        loss_ref[...] += 0.5 * jnp.sum(jnp.mean(diff * diff, axis=-1, keepdims=True))
        dy = diff * (1.0 / D_MODEL)
        gfin_ref[...] += jnp.sum(dy * xhat, axis=0, keepdims=True)
        dxh = dy * g
        dx2 = rstd * (dxh - xhat * jnp.mean(dxh * xhat, axis=-1, keepdims=True))
        dx2_ref[...] = dx2
        dx2b_ref[...] = dx2.astype(BF16)
        y2_ref[...] = y2

    return pl.pallas_call(
        body, name="fox_out_loss", grid=(s // ts,),
        in_specs=[pl.BlockSpec((ts, FOX_PAD), lambda i: (i, 0)),
                  pl.BlockSpec((ts, FOX_PAD), lambda i: (i, 0)),
                  _const_spec((HEADS * HEAD_DIM, D_MODEL)),
                  pl.BlockSpec((ts, D_MODEL), lambda i: (i, 0)),
                  pl.BlockSpec((ts, D_MODEL), lambda i: (i, 0)),
                  _const_spec((1, D_MODEL))],
        out_specs=[pl.BlockSpec((ts, D_MODEL), lambda i: (i, 0)),
                   pl.BlockSpec((ts, D_MODEL), lambda i: (i, 0)),
                   pl.BlockSpec((ts, HEADS * HEAD_DIM), lambda i: (i, 0)),
                   pl.BlockSpec((SUBLANES, LANES), lambda i: (0, 0)),
                   pl.BlockSpec((1, D_MODEL), lambda i: (0, 0))],
        out_shape=[jax.ShapeDtypeStruct((s, D_MODEL), F32),
                   jax.ShapeDtypeStruct((s, D_MODEL), BF16),
                   jax.ShapeDtypeStruct((s, HEADS * HEAD_DIM), BF16),
                   jax.ShapeDtypeStruct((SUBLANES, LANES), F32),
                   jax.ShapeDtypeStruct((1, D_MODEL), F32)],
        compiler_params=_params(),
    )(o, gate, w_out, x1, target, gf)


def _fox_out_bwd(dx2, w_out, o, gate, ts):
    s = dx2.shape[0]

    def body(dx_ref, w_ref, o_ref, gt_ref, do_ref, dg_ref):
        lane = lax.broadcasted_iota(jnp.int32, (ts, HEAD_PAD), 1)
        dy2 = _heads_to_padded(_dot_nt(dx_ref[...], w_ref[...]))
        gv = gt_ref[...]
        sg = _sigmoid(gv)
        ov = o_ref[...]
        dov = dy2 * (gv * sg)
        dg_ref[...] = (dy2 * ov * (sg * (1.0 + gv * (1.0 - sg)))).astype(BF16)
        prod = dov * ov
        for h in range(HEADS):
            sl = slice(h * HEAD_PAD, (h + 1) * HEAD_PAD)
            delta = jnp.sum(prod[:, sl], axis=-1, keepdims=True)
            hi = delta.astype(BF16)
            lo = (delta - hi.astype(F32)).astype(BF16)
            do_h = dov[:, sl].astype(BF16)
            do_ref[:, sl] = jnp.where(lane == LANE_ONE_V, -hi,
                                      jnp.where(lane == LANE_ONE_V + 1, -lo, do_h))

    tile = pl.BlockSpec((ts, FOX_PAD), lambda i: (i, 0))
    return pl.pallas_call(
        body, name="fox_out_bwd", grid=(s // ts,),
        in_specs=[pl.BlockSpec((ts, D_MODEL), lambda i: (i, 0)),
                  _const_spec((HEADS * HEAD_DIM, D_MODEL)), tile, tile],
        out_specs=[tile, tile],
        out_shape=[jax.ShapeDtypeStruct((s, FOX_PAD), BF16),
                   jax.ShapeDtypeStruct((s, FOX_PAD), BF16)],
        compiler_params=_params(),
    )(dx2, w_out, o, gate)


def _attn_bwd(qb, qkv, do, blk):
    s = qb.shape[0]
    nblk = s // blk
    heads = [slice(i * HEAD_PAD, (i + 1) * HEAD_PAD) for i in range(HEADS_PER_STEP)]

    def body(q_ref, k_ref, v_ref, do_ref, dq_ref, dk_ref, dv_ref, dcum_ref, dq_acc, dkt_acc,
             dvt_acc, qt_ref, dot_ref):
        group = pl.program_id(0)
        kj = pl.program_id(1)
        row = lax.broadcasted_iota(jnp.int32, (blk, blk), 0)
        col = lax.broadcasted_iota(jnp.int32, (blk, blk), 1)
        lane = lax.broadcasted_iota(jnp.int32, (blk, LANES), 1)
        mine = [lane == group * HEADS_PER_STEP + i for i in range(HEADS_PER_STEP)]

        @pl.when(kj == 0)
        def _():
            dq_acc[...] = jnp.zeros_like(dq_acc)

            def transpose_block(bi, _):
                r0 = pl.multiple_of(bi * blk, blk)
                for i, hd in enumerate(heads):
                    qt_ref[i, bi] = q_ref[pl.ds(r0, blk), hd].T
                    dot_ref[i, bi] = do_ref[pl.ds(r0, blk), hd].T
                return 0

            lax.fori_loop(0, nblk, transpose_block, 0)

        @pl.when((group == 0) & (kj == 0))
        def _():
            dcum_ref[...] = jnp.zeros_like(dcum_ref)

        k0 = pl.multiple_of(kj * blk, blk)
        ks = [k_ref[:, hd] for hd in heads]
        vs = [v_ref[:, hd] for hd in heads]

        dkt_acc[...] = jnp.zeros_like(dkt_acc)
        dvt_acc[...] = jnp.zeros_like(dvt_acc)

        def step(qi, masked):
            q0 = pl.multiple_of(qi * blk, blk)
            qs = [q_ref[pl.ds(q0, blk), hd] for hd in heads]
            dos = [do_ref[pl.ds(q0, blk), hd] for hd in heads]
            scores = [_dot_nt(q, k) for q, k in zip(qs, ks)]
            dps = [_dot_nt(dov, v) for dov, v in zip(dos, vs)]
            for i, (hd, k, sc, dp) in enumerate(zip(heads, ks, scores, dps)):
                p = jnp.exp(sc.astype(BF16))
                if masked:
                    p = jnp.where(col <= row, p, jnp.zeros_like(p))
                ds = (p.astype(F32) * dp).astype(BF16)
                dvt_acc[i] += _dot(dot_ref[i, qi], p)
                dkt_acc[i] += _dot(qt_ref[i, qi], ds)
                dq_acc[pl.ds(q0, blk), hd] += _dot(ds, k)

        step(kj, True)

        def q_step(qi, _):
            step(qi, False)
            return 0

        lax.fori_loop(kj + 1, nblk, q_step, 0)
        dcum = dcum_ref[pl.ds(k0, blk), :]
        for i, (hd, mask) in enumerate(zip(heads, mine)):
            dk = dkt_acc[i].T
            dk_ref[:, hd] = dk.astype(BF16)
            dv_ref[:, hd] = dvt_acc[i].T.astype(BF16)
            dcum = jnp.where(mask, -dk[:, LANE_CK:LANE_CK + 1], dcum)
        dcum_ref[pl.ds(k0, blk), :] = dcum

        @pl.when(kj == nblk - 1)
        def _():
            def finish(bi, _):
                r0 = pl.multiple_of(bi * blk, blk)
                dcum = dcum_ref[pl.ds(r0, blk), :]
                for hd, mask in zip(heads, mine):
                    dq = dq_acc[pl.ds(r0, blk), hd]
                    dq_ref[pl.ds(r0, blk), hd] = dq.astype(BF16)
                    dcum = dcum + jnp.where(mask, dq[:, LANE_RB:LANE_RB + 1], 0.0)
                dcum_ref[pl.ds(r0, blk), :] = dcum
                return 0

            lax.fori_loop(0, nblk, finish, 0)

    width = HEADS_PER_STEP * HEAD_PAD
    whole = pl.BlockSpec((s, width), lambda h, j: (0, h))
    whole_in = pl.BlockSpec((s, width), lambda h, j: (0, h), pipeline_mode=pl.Buffered(1))
    part = pl.BlockSpec((blk, width), lambda h, j: (j, h))
    out = jax.ShapeDtypeStruct((s, FOX_PAD), BF16)
    return pl.pallas_call(
        body, name="attn_bwd", grid=(HEADS // HEADS_PER_STEP, nblk),
        in_specs=[whole_in,
                  pl.BlockSpec((None, blk, width), lambda h, j: (1, j, h)),
                  pl.BlockSpec((None, blk, width), lambda h, j: (2, j, h)),
                  whole_in],
        out_specs=[whole, part, part, pl.BlockSpec((s, LANES), lambda h, j: (0, 0))],
        out_shape=[out, out, out, jax.ShapeDtypeStruct((s, LANES), F32)],
        scratch_shapes=[pltpu.VMEM((s, width), F32),
                        pltpu.VMEM((HEADS_PER_STEP, HEAD_PAD, blk), F32),
                        pltpu.VMEM((HEADS_PER_STEP, HEAD_PAD, blk), F32),
                        pltpu.VMEM((HEADS_PER_STEP, nblk, HEAD_PAD, blk), BF16),
                        pltpu.VMEM((HEADS_PER_STEP, nblk, HEAD_PAD, blk), BF16)],
        compiler_params=_params(2),
    )(qb, qkv, qkv, do)


def _fox_in_bwd(dq, dk, dv, dg, wt, wft, dcum, f, x1, dx2, g1, ts):
    s = x1.shape[0]
    nt = s // ts
    width = HEADS * HEAD_DIM

    def body(dq_ref, dk_ref, dv_ref, dg_ref, wt_ref, wft_ref, dcum_ref, f_ref, x1_ref, dx2_ref,
             g_ref, dx1_ref, dx1b_ref, df_ref, du_ref, gn_ref, gbf_ref, rcar_ref):
        @pl.when(pl.program_id(0) == 0)
        def _():
            rcar_ref[...] = jnp.zeros_like(rcar_ref)
            gn_ref[...] = jnp.zeros_like(gn_ref)
            gbf_ref[...] = jnp.zeros_like(gbf_ref)

        rsum = _cumsum_rows(dcum_ref[...], reverse=True) + rcar_ref[0:1, :]
        df = rsum * _sigmoid(-f_ref[...])
        dfb = df.astype(BF16)
        dh = _dot_nt(dfb, wft_ref[...])
        for j, ref in enumerate((dq_ref, dk_ref, dv_ref, dg_ref)):
            du = _heads_from_padded(ref[...])
            du_ref[j] = du
            dh = dh + _dot_nt(du, wt_ref[j])
        dxn, dgn = _norm_bwd(x1_ref[...], g_ref[...], dh)
        dx1 = dx2_ref[...] + dxn
        dx1_ref[...] = dx1
        dx1b_ref[...] = dx1.astype(BF16)
        df_ref[...] = dfb
        gn_ref[...] += dgn
        gbf_ref[...] += jnp.sum(df, axis=0, keepdims=True)
        rcar_ref[...] = rsum[0:SUBLANES, :]

    rev = lambda i: (nt - 1 - i, 0)
    wide = pl.BlockSpec((ts, FOX_PAD), rev)
    return pl.pallas_call(
        body, name="fox_in_bwd", grid=(nt,),
        in_specs=[wide, wide, wide, wide,
                  _const_spec((4, D_MODEL, width)),
                  _const_spec((D_MODEL, LANES)),
                  pl.BlockSpec((ts, LANES), rev),
                  pl.BlockSpec((ts, LANES), rev),
                  pl.BlockSpec((ts, D_MODEL), rev),
                  pl.BlockSpec((ts, D_MODEL), rev),
                  _const_spec((1, D_MODEL))],
        out_specs=[pl.BlockSpec((ts, D_MODEL), rev),
                   pl.BlockSpec((ts, D_MODEL), rev),
                   pl.BlockSpec((ts, LANES), rev),
                   pl.BlockSpec((4, ts, width), lambda i: (0, nt - 1 - i, 0)),
                   pl.BlockSpec((1, D_MODEL), lambda i: (0, 0)),
                   pl.BlockSpec((1, LANES), lambda i: (0, 0))],
        out_shape=[jax.ShapeDtypeStruct((s, D_MODEL), F32),
                   jax.ShapeDtypeStruct((s, D_MODEL), BF16),
                   jax.ShapeDtypeStruct((s, LANES), BF16),
                   jax.ShapeDtypeStruct((4, s, width), BF16),
                   jax.ShapeDtypeStruct((1, D_MODEL), F32),
                   jax.ShapeDtypeStruct((1, LANES), F32)],
        scratch_shapes=[pltpu.VMEM((SUBLANES, LANES), F32)],
        compiler_params=_params(),
    )(dq, dk, dv, dg, wt, wft, dcum, f, x1, dx2, g1)


def _lru_core_bwd(dx1b, w_out, xb, gate, hs, cw, cb, wa, ba, wx, bx, a_param, wa_t, wx_t,
                  chip_sums, ts):
    s = xb.shape[0]
    nt = s // ts
    tpb = ts // SUBLANES
    n_ex = len(chip_sums)

    def body(*refs):
        (dx_ref, wo_ref, xb_ref, xbh_ref, gate_ref, hs_ref, hsh_ref, cw_ref, cb_ref, wa_ref,
         ba_ref, wx_ref, bx_ref, ap_ref, wat_ref, wxt_ref) = refs[:16]
        sum_refs = refs[16:16 + n_ex]
        du_ref, gwa_ref, gwx_ref, gvec_ref = refs[16 + n_ex:20 + n_ex]
        got_refs = refs[20 + n_ex:20 + 2 * n_ex]
        acar_ref, dhcar_ref, dxccar_ref = refs[20 + 2 * n_ex:23 + 2 * n_ex]
        start, finish = _chip_exchange_phases(sum_refs, got_refs, *refs[23 + 2 * n_ex:])
        step = pl.program_id(0)
        pl.when(step == 0)(start)

        @pl.when(step == 0)
        def _():
            acar_ref[...] = jnp.zeros_like(acar_ref)
            dhcar_ref[...] = jnp.zeros_like(dhcar_ref)
            dxccar_ref[...] = jnp.zeros_like(dxccar_ref)
            gwa_ref[...] = jnp.zeros_like(gwa_ref)
            gwx_ref[...] = jnp.zeros_like(gwx_ref)
            gvec_ref[...] = jnp.zeros_like(gvec_ref)

        first_tile = step == nt - 1
        halo_on = jnp.where(first_tile, 0.0, 1.0)
        prev8 = xbh_ref[...] * halo_on
        hprev_row = hsh_ref[SUBLANES - 1:SUBLANES, :] * halo_on

        xbv = xb_ref[...]
        taps = _conv_taps(xbv, prev8)
        cw_v = cw_ref[...]
        xc, xcb, r, i, sp, a, mult = _lru_pre(taps, cw_v, cb_ref[...], wa_ref, ba_ref[...],
                                              wx_ref, bx_ref[...], ap_ref[...])
        hs = hs_ref[...]
        gv = gate_ref[...]
        sg = _sigmoid(gv)
        dy = _dot_nt(dx_ref[...], wo_ref[...])
        dhs = dy * (gv * sg)
        dgate = dy * hs * (sg * (1.0 + gv * (1.0 - sg)))

        rows = lax.broadcasted_iota(jnp.int32, a.shape, 0)
        a_next = jnp.where(rows < ts - 1, pltpu.roll(a, ts - 1, 0), acar_ref[0:1, :])
        cum_a, dh_loc = _scan_rows(a_next, dhs, reverse=True)
        dh = cum_a * dhcar_ref[0:1, :] + dh_loc
        h_prev = jnp.where(rows >= 1, pltpu.roll(hs, 1, 0), hprev_row)

        da = dh * h_prev
        ixc = i * xc
        dmult = dh * ixc
        di = dh * mult * xc
        dxc = dh * mult * i
        dlog_a = da * a - dmult * (a * a) / mult
        dr = dlog_a * ((-LRU_C) * sp)
        dsp = jnp.sum(dlog_a * ((-LRU_C) * r), axis=0, keepdims=True)
        dra = dr * r * (1.0 - r)
        dia = di * i * (1.0 - i)
        drab = dra.astype(BF16)
        diab = dia.astype(BF16)
        back = []
        for n in range(LRU_BLOCKS):
            sl = slice(n * LRU_BLOCK_W, (n + 1) * LRU_BLOCK_W)
            gwa_ref[n] += _dot_tn(xcb[:, sl], drab[:, sl])
            gwx_ref[n] += _dot_tn(xcb[:, sl], diab[:, sl])
            back.append(_dot(drab[:, sl], wat_ref[n]) + _dot(diab[:, sl], wxt_ref[n]))
        dxc = dxc + jnp.concatenate(back, axis=1)

        nxt8 = dxccar_ref[...]
        rows8 = lax.broadcasted_iota(jnp.int32, nxt8.shape, 0)
        dxb = cw_v[3:4] * dxc
        for j in range(1, CONV_WIDTH):
            rj = pltpu.roll(dxc, ts - j, 0)
            pj = pltpu.roll(nxt8, SUBLANES - j, 0)
            tail = jnp.where(rows8 >= SUBLANES - j, pj, rj[ts - SUBLANES:])
            dxb = dxb + cw_v[3 - j:4 - j] * jnp.concatenate([rj[:ts - SUBLANES], tail], axis=0)

        du_ref[:, :LRU_WIDTH] = dxb.astype(BF16)
        du_ref[:, LRU_WIDTH:] = dgate.astype(BF16)

        z = -ap_ref[...]
        gvec = [jnp.sum(dxc * taps[3 - k], axis=0, keepdims=True) for k in range(CONV_WIDTH)]
        gvec.append(jnp.sum(dxc, axis=0, keepdims=True))
        gvec.append(jnp.sum(dra, axis=0, keepdims=True))
        gvec.append(jnp.sum(dia, axis=0, keepdims=True))
        gvec.append(-dsp * _sigmoid(z))
        gvec_ref[...] += jnp.concatenate(gvec, axis=0)

        acar_ref[...] = a[0:SUBLANES, :]
        dhcar_ref[...] = dh[0:SUBLANES, :]
        dxccar_ref[...] = dxc[0:SUBLANES, :]
        pl.when(step == nt - 1)(finish)

    rev = lambda i: (nt - 1 - i, 0)
    halo = lambda i: (jnp.maximum((nt - 1 - i) * tpb - 1, 0), 0)
    tile = pl.BlockSpec((ts, LRU_WIDTH), rev)
    halo_spec = pl.BlockSpec((SUBLANES, LRU_WIDTH), halo)
    vec = _const_spec((1, LRU_WIDTH))
    blk = _const_spec((LRU_BLOCKS, LRU_BLOCK_W, LRU_BLOCK_W))
    acc_blk = pl.BlockSpec((LRU_BLOCKS, LRU_BLOCK_W, LRU_BLOCK_W), lambda i: (0, 0, 0))
    hbm = pl.BlockSpec(memory_space=pl.ANY)
    res = pl.pallas_call(
        body, name="lru_core_bwd", grid=(nt,),
        in_specs=[pl.BlockSpec((ts, D_MODEL), rev),
                  _const_spec((LRU_WIDTH, D_MODEL)),
                  tile, halo_spec, tile, tile, halo_spec,
                  _const_spec((CONV_WIDTH, LRU_WIDTH)), vec, blk, vec, blk, vec, vec, blk, blk]
        + [hbm] * n_ex,
        out_specs=[pl.BlockSpec((ts, 2 * LRU_WIDTH), rev), acc_blk, acc_blk,
                   pl.BlockSpec((SUBLANES, LRU_WIDTH), lambda i: (0, 0))] + [hbm] * n_ex,
        out_shape=[jax.ShapeDtypeStruct((s, 2 * LRU_WIDTH), BF16),
                   jax.ShapeDtypeStruct((LRU_BLOCKS, LRU_BLOCK_W, LRU_BLOCK_W), F32),
                   jax.ShapeDtypeStruct((LRU_BLOCKS, LRU_BLOCK_W, LRU_BLOCK_W), F32),
                   jax.ShapeDtypeStruct((SUBLANES, LRU_WIDTH), F32)]
        + [jax.ShapeDtypeStruct(a.shape, a.dtype) for a in chip_sums],
        scratch_shapes=[pltpu.VMEM((SUBLANES, LRU_WIDTH), F32),
                        pltpu.VMEM((SUBLANES, LRU_WIDTH), F32),
                        pltpu.VMEM((SUBLANES, LRU_WIDTH), F32)] + _chip_exchange_sems(n_ex),
        compiler_params=_params(),
    )(dx1b, w_out, xb, xb, gate, hs, hs, cw, cb, wa, ba, wx, bx, a_param, wa_t, wx_t, *chip_sums)
    return res[0], res[1], res[2], res[3], res[4:]


def _lru_in_bwd(du, w_in, x, dx1, g0, chip_sums, ts):
    s = x.shape[0]
    nt = s // ts
    n = len(chip_sums)

    def body(*refs):
        du_ref, w_ref, x_ref, dx1_ref, g_ref = refs[:5]
        sum_refs = refs[5:5 + n]
        gx_ref, gn_ref = refs[5 + n:7 + n]
        got_refs = refs[7 + n:7 + 2 * n]
        wfull_ref = refs[7 + 2 * n]
        start, finish = _chip_exchange_phases(sum_refs, got_refs, *refs[8 + 2 * n:])
        step = pl.program_id(0)
        pl.when(step == 0)(start)

        @pl.when(step == 0)
        def _():
            gn_ref[...] = jnp.zeros_like(gn_ref)
            for j in range(N_DEV):
                wfull_ref[:, j * LRU_IN_SHARD:(j + 1) * LRU_IN_SHARD] = w_ref[j]

        dh = _dot_nt(du_ref[...], wfull_ref[...])
        dxn, dgn = _norm_bwd(x_ref[...], g_ref[...], dh)
        gx_ref[...] = dx1_ref[...] + dxn
        gn_ref[...] += dgn
        pl.when(step == nt - 1)(finish)

    tile = pl.BlockSpec((ts, D_MODEL), lambda i: (i, 0))
    hbm = pl.BlockSpec(memory_space=pl.ANY)
    res = pl.pallas_call(
        body, name="lru_in_bwd", grid=(nt,),
        in_specs=[pl.BlockSpec((ts, 2 * LRU_WIDTH), lambda i: (i, 0)),
                  _const_spec((N_DEV, D_MODEL, LRU_IN_SHARD)), tile, tile,
                  _const_spec((1, D_MODEL))] + [hbm] * n,
        out_specs=[tile, pl.BlockSpec((1, D_MODEL), lambda i: (0, 0))] + [hbm] * n,
        out_shape=[jax.ShapeDtypeStruct((s, D_MODEL), F32),
                   jax.ShapeDtypeStruct((1, D_MODEL), F32)]
        + [jax.ShapeDtypeStruct(a.shape, a.dtype) for a in chip_sums],
        scratch_shapes=[pltpu.VMEM((D_MODEL, 2 * LRU_WIDTH), BF16)] + _chip_exchange_sems(n),
        compiler_params=_params(),
    )(du, w_in, x, dx1, g0, *chip_sums)
    return res[0], res[1], res[2:]


def _weight_grad(a, b, ts, name, scale=1.0, col_shards=1):
    s, ka = a.shape
    nb = b.shape[1]
    nt = s // ts
    per = nb // col_shards

    def body(a_ref, b_ref, o_ref):
        @pl.when(pl.program_id(0) == 0)
        def _():
            o_ref[...] = jnp.zeros_like(o_ref)

        if col_shards == 1:
            o_ref[...] += _dot_tn(a_ref[...], b_ref[...])
        else:
            acc = _dot_tn(a_ref[...], b_ref[...])
            for j in range(col_shards):
                o_ref[j] += acc[:, j * per:(j + 1) * per]
        if scale != 1.0:
            @pl.when(pl.program_id(0) == nt - 1)
            def _():
                o_ref[...] = o_ref[...] * scale

    out_dims = (ka, nb) if col_shards == 1 else (col_shards, ka, per)
    return pl.pallas_call(
        body, name=name, grid=(nt,),
        in_specs=[pl.BlockSpec((ts, ka), lambda i: (i, 0)),
                  pl.BlockSpec((ts, nb), lambda i: (i, 0))],
        out_specs=pl.BlockSpec(out_dims, lambda i: (0,) * len(out_dims)),
        out_shape=jax.ShapeDtypeStruct(out_dims, F32),
        compiler_params=_params(),
    )(a, b)


def _sum_parts(gp_ref):
    g = gp_ref[0].astype(F32)
    for k in range(1, gp_ref.shape[0]):
        g = g + gp_ref[k].astype(F32)
    return g


def _adamw(g_parts, w, m, v, tr, name):
    nparts, rows, cols = g_parts.shape

    def body(gp_ref, w_ref, m_ref, v_ref, g_ref, d_ref, mo_ref, vo_ref):
        g = _sum_parts(gp_ref)
        m2 = ADAM_B1 * m_ref[...] + (1.0 - ADAM_B1) * g
        v2 = ADAM_B2 * v_ref[...] + (1.0 - ADAM_B2) * (g * g)
        m_hat = m2 / (1.0 - ADAM_B1 ** ADAM_STEP)
        v_hat = v2 / (1.0 - ADAM_B2 ** ADAM_STEP)
        g_ref[...] = g
        d_ref[...] = (-ADAM_LR) * (m_hat / (jnp.sqrt(v_hat) + ADAM_EPS) + ADAM_WD * w_ref[...])
        mo_ref[...] = m2
        vo_ref[...] = v2

    tile = pl.BlockSpec((tr, cols), lambda i: (i, 0))
    out = jax.ShapeDtypeStruct((rows, cols), F32)
    return pl.pallas_call(
        body, name=name, grid=(rows // tr,),
        in_specs=[pl.BlockSpec((nparts, tr, cols), lambda i: (0, i, 0)), tile, tile, tile],
        out_specs=[tile, tile, tile, tile],
        out_shape=[out, out, out, out],
        compiler_params=_params(),
    )(g_parts, w, m, v)


def _reduce_parts(g_parts, name):
    _, rows, cols = g_parts.shape

    def body(gp_ref, g_ref):
        g_ref[...] = _sum_parts(gp_ref)

    return pl.pallas_call(
        body, name=name,
        out_shape=jax.ShapeDtypeStruct((rows, cols), F32),
        compiler_params=pltpu.CompilerParams(vmem_limit_bytes=VMEM_LIMIT_BYTES),
    )(g_parts)


def _mesh_pos():
    ix, iy, ic = lax.axis_index("x"), lax.axis_index("y"), lax.axis_index("c")
    return ix, iy, ic


def _peer(ix, iy, ic, mask):
    px = 1 - ix if mask & 4 else ix
    py = 1 - iy if mask & 2 else iy
    pc = 1 - ic if mask & 1 else ic
    return (px, py, pc), 4 * px + 2 * py + pc


def _exchange(arrays, scatter, name):
    n = len(arrays)

    def body(*refs):
        x_refs, o_refs = refs[:n], refs[n:2 * n]
        send_sems, recv_sems, local_sems = refs[2 * n:]
        ix, iy, ic = _mesh_pos()
        me = 4 * ix + 2 * iy + ic

        def src(a, dest):
            return x_refs[a].at[dest] if scatter else x_refs[a]

        local = [pltpu.make_async_copy(src(a, me), o_refs[a].at[me], local_sems.at[a])
                 for a in range(n)]
        for cp in local:
            cp.start()
        sends = []
        for mask in range(1, N_DEV):
            peer, pidx = _peer(ix, iy, ic, mask)
            for a in range(n):
                cp = pltpu.make_async_remote_copy(
                    src_ref=src(a, pidx), dst_ref=o_refs[a].at[me],
                    send_sem=send_sems.at[a, mask - 1], recv_sem=recv_sems.at[a, mask - 1],
                    device_id=peer, device_id_type=pl.DeviceIdType.MESH)
                cp.start()
                sends.append(cp)
        for mask in range(1, N_DEV):
            peer, pidx = _peer(ix, iy, ic, mask)
            for a in range(n):
                pltpu.make_async_remote_copy(
                    src_ref=src(a, me), dst_ref=o_refs[a].at[pidx],
                    send_sem=send_sems.at[a, mask - 1], recv_sem=recv_sems.at[a, mask - 1],
                    device_id=peer, device_id_type=pl.DeviceIdType.MESH).wait_recv()
        for cp in sends:
            cp.wait_send()
        for cp in local:
            cp.wait()

    out_shape = [jax.ShapeDtypeStruct(x.shape if scatter else (N_DEV,) + x.shape, x.dtype)
                 for x in arrays]
    return pl.pallas_call(
        body, name=name,
        in_specs=[pl.BlockSpec(memory_space=pl.ANY)] * n,
        out_specs=[pl.BlockSpec(memory_space=pl.ANY)] * n,
        out_shape=out_shape,
        scratch_shapes=[pltpu.SemaphoreType.DMA((n, N_DEV - 1)),
                        pltpu.SemaphoreType.DMA((n, N_DEV - 1)),
                        pltpu.SemaphoreType.DMA((n,))],
    )(*arrays)


def _gather_two_level(arrays, name):
    n = len(arrays)

    def body(*refs):
        start, forward, finish = _gather_phases(refs[:n], refs[n:2 * n], *refs[2 * n:])
        start()
        forward()
        finish()

    return pl.pallas_call(
        body, name=name,
        in_specs=[pl.BlockSpec(memory_space=pl.ANY)] * n,
        out_specs=[pl.BlockSpec(memory_space=pl.ANY)] * n,
        out_shape=[jax.ShapeDtypeStruct((N_DEV,) + x.shape, x.dtype) for x in arrays],
        scratch_shapes=_gather_sems(n),
    )(*arrays)


def _gather_sems(n):
    return [pltpu.SemaphoreType.DMA((n, N_DEV - 1)), pltpu.SemaphoreType.DMA((n, N_DEV - 1)),
            pltpu.SemaphoreType.DMA((n,))]


def _gather_phases(x_refs, o_refs, send_sems, recv_sems, local_sems):
    n = len(x_refs)
    ix, iy, ic = _mesh_pos()
    me, sibling = (ix, iy, ic), (ix, iy, 1 - ic)
    chips = [(1 - ix, iy), (ix, 1 - iy), (1 - ix, 1 - iy)]

    def idx(px, py, pc):
        return 4 * px + 2 * py + pc

    def copy(a, k, block, to, src=None):
        dst = o_refs[a].at[idx(*block)]
        return pltpu.make_async_remote_copy(
            src_ref=dst if src is None else src, dst_ref=dst,
            send_sem=send_sems.at[a, k], recv_sem=recv_sems.at[a, k],
            device_id=to, device_id_type=pl.DeviceIdType.MESH)

    def local():
        return [pltpu.make_async_copy(x_refs[a], o_refs[a].at[idx(*me)], local_sems.at[a])
                for a in range(n)]

    def first():
        out = []
        for a in range(n):
            out.append(copy(a, 0, me, sibling, src=x_refs[a]))
            out += [copy(a, 1 + j, me, (*chip, ic), src=x_refs[a])
                    for j, chip in enumerate(chips)]
        return out

    def passed():
        return [copy(a, 4 + j, (*chip, ic), sibling)
                for j, chip in enumerate(chips) for a in range(n)]

    def start():
        for cp in local() + first():
            cp.start()

    def forward():
        for j, chip in enumerate(chips):
            for a in range(n):
                copy(a, 1 + j, (*chip, ic), me).wait_recv()
                copy(a, 4 + j, (*chip, ic), sibling).start()

    def finish():
        for a in range(n):
            copy(a, 0, sibling, me).wait_recv()
            for j, chip in enumerate(chips):
                copy(a, 4 + j, (*chip, 1 - ic), me).wait_recv()
        for cp in first() + passed():
            cp.wait_send()
        for cp in local():
            cp.wait()

    return start, forward, finish


def _swap_sibling(arrays, name):
    n = len(arrays)
    n_chips = N_DEV // 2

    def body(*refs):
        x_refs, got_refs = refs[:n], refs[n:2 * n]
        send_sems, recv_sems = refs[2 * n:]
        ix, iy, ic = _mesh_pos()
        sibling = (ix, iy, 1 - ic)
        sends = []
        for a in range(n):
            for q in range(n_chips):
                cp = pltpu.make_async_remote_copy(
                    src_ref=x_refs[a].at[q, 1 - ic], dst_ref=got_refs[a].at[q],
                    send_sem=send_sems.at[a, q], recv_sem=recv_sems.at[a, q],
                    device_id=sibling, device_id_type=pl.DeviceIdType.MESH)
                cp.start()
                sends.append(cp)
        for cp in sends:
            cp.wait()

    return pl.pallas_call(
        body, name=name,
        in_specs=[pl.BlockSpec(memory_space=pl.ANY)] * n,
        out_specs=[pl.BlockSpec(memory_space=pl.ANY)] * n,
        out_shape=[jax.ShapeDtypeStruct((n_chips,) + x.shape[2:], x.dtype) for x in arrays],
        scratch_shapes=[pltpu.SemaphoreType.DMA((n, n_chips)),
                        pltpu.SemaphoreType.DMA((n, n_chips))],
    )(*arrays)


def _exchange_chips(arrays, name):
    n = len(arrays)

    def body(*refs):
        start, finish = _chip_exchange_phases(refs[:n], refs[n:2 * n], *refs[2 * n:])
        start()
        finish()

    return pl.pallas_call(
        body, name=name,
        in_specs=[pl.BlockSpec(memory_space=pl.ANY)] * n,
        out_specs=[pl.BlockSpec(memory_space=pl.ANY)] * n,
        out_shape=[jax.ShapeDtypeStruct(x.shape, x.dtype) for x in arrays],
        scratch_shapes=_chip_exchange_sems(n),
    )(*arrays)


def _chip_exchange_sems(n):
    n_chips = N_DEV // 2
    return [pltpu.SemaphoreType.DMA((n, n_chips - 1)), pltpu.SemaphoreType.DMA((n, n_chips - 1)),
            pltpu.SemaphoreType.DMA((n,))]


def _chip_exchange_phases(x_refs, o_refs, send_sems, recv_sems, local_sems):
    n = len(x_refs)
    n_chips = N_DEV // 2
    ix, iy, ic = _mesh_pos()
    my_chip = 2 * ix + iy

    def peers():
        for mask in range(1, n_chips):
            px = 1 - ix if mask & 2 else ix
            py = 1 - iy if mask & 1 else iy
            yield mask, (px, py, ic), 2 * px + py

    def local():
        return [pltpu.make_async_copy(x_refs[a].at[my_chip], o_refs[a].at[my_chip],
                                      local_sems.at[a]) for a in range(n)]

    def sends():
        return [pltpu.make_async_remote_copy(
            src_ref=x_refs[a].at[chip], dst_ref=o_refs[a].at[my_chip],
            send_sem=send_sems.at[a, mask - 1], recv_sem=recv_sems.at[a, mask - 1],
            device_id=peer, device_id_type=pl.DeviceIdType.MESH)
            for mask, peer, chip in peers() for a in range(n)]

    def start():
        for cp in local() + sends():
            cp.start()

    def finish():
        for mask, peer, chip in peers():
            for a in range(n):
                pltpu.make_async_remote_copy(
                    src_ref=x_refs[a].at[my_chip], dst_ref=o_refs[a].at[chip],
                    send_sem=send_sems.at[a, mask - 1], recv_sem=recv_sems.at[a, mask - 1],
                    device_id=peer, device_id_type=pl.DeviceIdType.MESH).wait_recv()
        for cp in sends():
            cp.wait_send()
        for cp in local():
            cp.wait()

    return start, finish


def _pair_sum(core, x, got, name):
    nq, rows, cols = got.shape

    def body(c_ref, x_ref, g_ref, o_ref):
        o_ref[...] = (x_ref[...] + g_ref[...]).astype(BF16)

    blk = pl.BlockSpec((None, rows, cols), lambda q, c: (q, 0, 0))
    return pl.pallas_call(
        body, name=name,
        grid_spec=pltpu.PrefetchScalarGridSpec(
            num_scalar_prefetch=1, grid=(nq,),
            in_specs=[pl.BlockSpec((None, None, rows, cols), lambda q, c: (q, c[0], 0, 0)), blk],
            out_specs=blk),
        out_shape=jax.ShapeDtypeStruct(got.shape, BF16),
        compiler_params=_params(),
    )(core, x, got)


def _selectors():
    r = lax.broadcasted_iota(jnp.int32, (LANES, FOX_PAD), 0)
    c = lax.broadcasted_iota(jnp.int32, (LANES, FOX_PAD), 1)
    part, head_r = r // HEADS, r % HEADS
    head_c, lane_c = c // HEAD_PAD, c % HEAD_PAD
    same = (head_r == head_c) & (part < 3)
    sel_q = jnp.where(same & (lane_c == LANE_RB + part), 1.0, 0.0)
    sel_k = jnp.where(same & (lane_c == LANE_CK + part), -1.0, 0.0)
    sel = jnp.stack([sel_q, sel_k, jnp.zeros_like(sel_q)]).astype(BF16)
    lane = lax.broadcasted_iota(jnp.int32, (1, FOX_PAD), 1) % HEAD_PAD
    ones_q = jnp.where((lane >= LANE_CK) & (lane < LANE_CK + 3), 1.0, 0.0)
    ones_k = jnp.where(((lane >= LANE_RB) & (lane < LANE_RB + 3))
                       | ((lane >= LANE_LSE) & (lane < LANE_LSE + 3)), 1.0, 0.0)
    ones_v = jnp.where((lane >= LANE_ONE_V) & (lane < LANE_ONE_V + 2), 1.0, 0.0)
    bias = jnp.stack([ones_q, ones_k, ones_v]).astype(F32)
    return sel, bias


def _chip_sums(names, send):
    send = [a.reshape((N_DEV // 2, 2) + a.shape[1:]) for a in send]
    got = _swap_sibling(send, "swap_" + names[0])
    core = lax.axis_index("c").astype(jnp.int32).reshape(1)
    return [_pair_sum(core, a, b, "pair_sum_" + n) for n, a, b in zip(names, send, got)]


def _local_step(x, target, norm_g, final_g, w_in8, conv_w, conv_b, wa, ba, wx, bx, a_param,
                w_out_b, fox_in_shard, b_f, fox_out_shard, blk=512, ts=256):
    qk_scale = 1.0 / (HEAD_DIM ** 0.5)
    g0, g1 = norm_g[0:1], norm_g[1:2]
    gf = final_g.reshape(1, D_MODEL)
    wa_b, wx_b = wa.astype(BF16), wx.astype(BF16)
    sel, bias = _selectors()

    xb, gate1, h0, (fox_in8, fox_out8) = _lru_in_fwd(x, g0, w_in8, [fox_in_shard, fox_out_shard],
                                                     ts)
    fox_w_in = jnp.transpose(fox_in8, (1, 0, 2)).reshape(D_MODEL, FOX_IN_COLS)
    width = HEADS * HEAD_DIM
    w4 = jnp.stack([fox_w_in[:, 0:width] * qk_scale, fox_w_in[:, width:2 * width],
                    fox_w_in[:, 2 * width:3 * width], fox_w_in[:, 3 * width:4 * width]])
    wf_b = jnp.pad(fox_w_in[:, 4 * width:], ((0, 0), (0, LANES - HEADS)))
    bf_pad = jnp.pad(b_f, ((0, 0), (0, LANES - HEADS)))
    fo_b = fox_out8.reshape(width, D_MODEL)
    y1, hs = _lru_core_fwd(xb, gate1, conv_w, conv_b, wa_b, ba, wx_b, bx, a_param, ts)
    x1, h1, f, cparts = _fox_pre_fwd(x, y1, w_out_b, g1, wf_b, bf_pad, ts)
    qkv = _fox_proj_fwd(h1, cparts, w4[0:3], sel, bias, BF16, ts, "fox_proj_qkv")
    gate2 = _fox_proj_fwd(h1, None, w4[3:4], None, None, F32, ts, "fox_proj_gate")[0]
    o, qb = _attn_fwd(qkv, blk, hps=4)
    dx2, dx2b, y2, loss_acc, g_final = _fox_out_loss(o, gate2, fo_b, x1, target, gf, ts)

    do, dgate2 = _fox_out_bwd(dx2b, fo_b, o, gate2, ts)
    dq, dk, dv, dcum = _attn_bwd(qb, qkv, do, blk)
    dx1, dx1b, df, du4, g_norm1, g_bf = _fox_in_bwd(dq, dk, dv, dgate2, w4, wf_b, dcum, f, x1,
                                                    dx2, g1, ts)
    tw = 512
    g_q = _weight_grad(h1, du4[0], tw, "grad_fox_wq", scale=qk_scale)
    g_k = _weight_grad(h1, du4[1], tw, "grad_fox_wk")
    g_v = _weight_grad(h1, du4[2], tw, "grad_fox_wv")
    g_g = _weight_grad(h1, du4[3], tw, "grad_fox_wg")
    g_f = _weight_grad(h1, df, tw, "grad_fox_wf")
    g_fox_w_in = jnp.concatenate([g_q, g_k, g_v, g_g, g_f[:, :HEADS]], axis=1)
    g_fox_w_in = jnp.transpose(g_fox_w_in.reshape(D_MODEL, N_DEV, FOX_IN_SHARD), (1, 0, 2))
    g_fox_w_out = _weight_grad(y2, dx2b, tw, "grad_fox_w_out")
    fox_sums = _chip_sums(("fox_w_in", "fox_w_out"),
                          [g_fox_w_in, g_fox_w_out.reshape(N_DEV, -1, D_MODEL)])

    du, g_wa, g_wx, g_vec, (r_fox_in, r_fox_out) = _lru_core_bwd(
        dx1b, w_out_b, xb, gate1, hs, conv_w, conv_b, wa_b, ba, wx_b, bx, a_param,
        jnp.transpose(wa_b, (0, 2, 1)), jnp.transpose(wx_b, (0, 2, 1)), fox_sums, ts)
    g_lru_w_in = _weight_grad(h0, du, tw, "grad_lru_w_in", col_shards=N_DEV)
    g_lru_w_out = _weight_grad(y1, dx1b, tw, "grad_lru_w_out")
    conv_send = jnp.transpose(g_vec[0:CONV_WIDTH].reshape(CONV_WIDTH, N_DEV, -1), (1, 0, 2))
    lru_sums = _chip_sums(("lru_w_in", "lru_conv_w", "lru_w_out"),
                          [g_lru_w_in, conv_send, g_lru_w_out.reshape(N_DEV, -1, D_MODEL)])
    grad_x, g_norm0, (r_w_in, r_conv, r_w_out) = _lru_in_bwd(du, w_in8, x, dx1, g0, lru_sums, ts)

    small = dict(
        norm_g=jnp.concatenate([g_norm0, g_norm1], axis=0), final_g=g_final[0],
        lru_conv_b=g_vec[4:5], lru_wa=g_wa, lru_ba=g_vec[5:6], lru_wx=g_wx, lru_bx=g_vec[6:7],
        lru_a_param=g_vec[7:8], fox_b_f=g_bf[:, :HEADS])
    received = dict(lru_w_in=r_w_in, lru_conv_w=r_conv, lru_w_out=r_w_out, fox_w_in=r_fox_in,
                    fox_w_out=r_fox_out)
    return loss_acc[0, 0], grad_x, small, received


SMALL =("norm_g", "final_g", "lru_conv_b", "lru_wa", "lru_ba", "lru_wx", "lru_bx", "lru_a_param",
         "fox_b_f")
ALL_WEIGHTS = ("norm_g", "final_g", "lru_w_in", "lru_conv_w", "lru_conv_b", "lru_wa", "lru_ba",
               "lru_wx", "lru_bx", "lru_a_param", "lru_w_out", "fox_w_in", "fox_b_f", "fox_w_out")


def _pack_small(d):
    rows = []
    for n in SMALL:
        a = d[n].reshape(-1)
        if a.shape[0] % LANES:
            a = jnp.pad(a, (0, LANES - a.shape[0] % LANES))
        rows.append(a.reshape(-1, LANES))
    packed = jnp.concatenate(rows, axis=0)
    return jnp.pad(packed, ((0, N_DEV * SMALL_CHUNK_ROWS - packed.shape[0]), (0, 0)))


def _unpack_small(packed, like):
    out, off = {}, 0
    for n, nrows in zip(SMALL, SMALL_ROWS):
        size = like[n].size
        out[n] = packed[off:off + nrows].reshape(-1)[:size].reshape(like[n].shape)
        off += nrows
    return out


def kernel(x, norm_g, final_g, lru_w_in, lru_conv_w, lru_conv_b, lru_wa, lru_ba, lru_wx, lru_bx, lru_a_param, lru_w_out, fox_w_in, fox_b_f, fox_w_out, loss_target, m_norm_g, m_final_g, m_lru_w_in, m_lru_conv_w, m_lru_conv_b, m_lru_wa, m_lru_ba, m_lru_wx, m_lru_bx, m_lru_a_param, m_lru_w_out, m_fox_w_in, m_fox_b_f, m_fox_w_out, v_norm_g, v_final_g, v_lru_w_in, v_lru_conv_w, v_lru_conv_b, v_lru_wa, v_lru_ba, v_lru_wx, v_lru_bx, v_lru_a_param, v_lru_w_out, v_fox_w_in, v_fox_b_f, v_fox_w_out):
    w_loc = dict(norm_g=norm_g, final_g=final_g, lru_w_in=lru_w_in, lru_conv_w=lru_conv_w,
                 lru_conv_b=lru_conv_b, lru_wa=lru_wa, lru_ba=lru_ba, lru_wx=lru_wx, lru_bx=lru_bx,
                 lru_a_param=lru_a_param, lru_w_out=lru_w_out, fox_w_in=fox_w_in, fox_b_f=fox_b_f,
                 fox_w_out=fox_w_out)
    m_loc = dict(norm_g=m_norm_g, final_g=m_final_g, lru_w_in=m_lru_w_in, lru_conv_w=m_lru_conv_w,
                 lru_conv_b=m_lru_conv_b, lru_wa=m_lru_wa, lru_ba=m_lru_ba, lru_wx=m_lru_wx,
                 lru_bx=m_lru_bx, lru_a_param=m_lru_a_param, lru_w_out=m_lru_w_out,
                 fox_w_in=m_fox_w_in, fox_b_f=m_fox_b_f, fox_w_out=m_fox_w_out)
    v_loc = dict(norm_g=v_norm_g, final_g=v_final_g, lru_w_in=v_lru_w_in, lru_conv_w=v_lru_conv_w,
                 lru_conv_b=v_lru_conv_b, lru_wa=v_lru_wa, lru_ba=v_lru_ba, lru_wx=v_lru_wx,
                 lru_bx=v_lru_bx, lru_a_param=v_lru_a_param, lru_w_out=v_lru_w_out,
                 fox_w_in=v_fox_w_in, fox_b_f=v_fox_b_f, fox_w_out=v_fox_w_out)

    w_in8, conv8, w_out8 = _gather_two_level(
        [lru_w_in[0].astype(BF16), lru_conv_w[0], lru_w_out[0].astype(BF16)], "gather_weights")
    conv_full = jnp.transpose(conv8, (1, 0, 2)).reshape(CONV_WIDTH, LRU_WIDTH)

    loss, grad_x, small_grads, received = _local_step(
        x[0], loss_target[0], norm_g, final_g, w_in8, conv_full, lru_conv_b, lru_wa[0], lru_ba,
        lru_wx[0], lru_bx, lru_a_param, w_out8.reshape(LRU_WIDTH, D_MODEL),
        fox_w_in[0].astype(BF16), fox_b_f, fox_w_out[0].astype(BF16))

    out = {}
    for n, tr in (("lru_w_in", 256), ("lru_conv_w", CONV_WIDTH), ("lru_w_out", 96),
                  ("fox_w_in", 128), ("fox_w_out", 64)):
        res = _adamw(received[n], w_loc[n][0], m_loc[n][0], v_loc[n][0], tr, "adamw_" + n)
        out[n] = [a[None] for a in res]

    small_sums = _chip_sums(
        ("small",), [_pack_small(small_grads).reshape(N_DEV, SMALL_CHUNK_ROWS, LANES)])
    r_small, = _exchange_chips(small_sums, "scatter_small_grads")

    g_chunk = _reduce_parts(r_small, "reduce_small_grads")
    g_small, = _exchange([g_chunk], False, "gather_small_grads")
    g_small = g_small.reshape(1, N_DEV * SMALL_CHUNK_ROWS, LANES)
    res = _adamw(g_small, _pack_small(w_loc), _pack_small(m_loc), _pack_small(v_loc),
                 N_DEV * SMALL_CHUNK_ROWS, "adamw_replicated")
    small_out = [_unpack_small(a, w_loc) for a in res]
    for n in SMALL:
        out[n] = [d[n] for d in small_out]

    loss = lax.psum(loss, ("x", "y", "c"))
    return (loss, grad_x[None], *[out[n][0] for n in ALL_WEIGHTS], *[out[n][1] for n in ALL_WEIGHTS],
            *[out[n][2] for n in ALL_WEIGHTS], *[out[n][3] for n in ALL_WEIGHTS])
```

```python
import functools

import jax
import jax.numpy as jnp
from jax import lax
from jax.experimental import pallas as pl
from jax.experimental.pallas import tpu as pltpu

F32 = jnp.float32
BF16 = jnp.bfloat16

D_MODEL = 1024
LRU_WIDTH = 1536
LRU_BLOCKS = 12
LRU_BLOCK_W = 128
CONV_WIDTH = 4
LRU_C = 8.0
HEADS = 16
HEAD_DIM = 64
HEAD_PAD = 128
FOX_PAD = HEADS * HEAD_PAD
HEADS_PER_STEP = 2
EPS = 1e-6
NEG_BIG = -1e30
N_DEV = 8

ADAM_LR = 0.001
ADAM_B1 = 0.9
ADAM_B2 = 0.999
ADAM_EPS = 1e-08
ADAM_WD = 0.01
ADAM_STEP = 10

LANE_RB = 64
LANE_CK = 67
LANE_LSE = 70
LANE_ONE_V = 64

VMEM_LIMIT_BYTES = 56 * 1024 * 1024
LANES = 128
SUBLANES = 8

LRU_IN_SHARD = 2 * LRU_WIDTH // N_DEV
FOX_IN_COLS = 4 * HEADS * HEAD_DIM + HEADS
FOX_IN_SHARD = FOX_IN_COLS // N_DEV

SMALL_ROWS = (16, 8, 12, 1536, 12, 1536, 12, 12, 1)
SMALL_CHUNK_ROWS = 400
assert sum(SMALL_ROWS) <= N_DEV * SMALL_CHUNK_ROWS


def _params(n_grid_axes=1):
    return pltpu.CompilerParams(
        dimension_semantics=("arbitrary",) * n_grid_axes,
        vmem_limit_bytes=VMEM_LIMIT_BYTES)


def _const_spec(shape):
    nd = len(shape)
    return pl.BlockSpec(shape, lambda *_: (0,) * nd, pipeline_mode=pl.Buffered(1))


def _shift_down(x, k, fill):
    rows = lax.broadcasted_iota(jnp.int32, x.shape, 0)
    return jnp.where(rows >= k, pltpu.roll(x, k, 0), fill)


def _shift_up(x, k, fill):
    n = x.shape[0]
    rows = lax.broadcasted_iota(jnp.int32, x.shape, 0)
    return jnp.where(rows < n - k, pltpu.roll(x, n - k, 0), fill)


def _scan_rows(a, b, reverse=False):
    n = a.shape[0]
    shift = _shift_up if reverse else _shift_down
    k = 1
    while k < n:
        b = a * shift(b, k, 0.0) + b
        a = a * shift(a, k, 1.0)
        k *= 2
    return a, b


def _cumsum_rows(x, reverse=False):
    n = x.shape[0]
    shift = _shift_up if reverse else _shift_down
    k = 1
    while k < n:
        x = x + shift(x, k, 0.0)
        k *= 2
    return x


def _rstd(x):
    return lax.rsqrt(jnp.mean(x * x, axis=-1, keepdims=True) + EPS)


def _norm_bwd(x, g, dh):
    rstd = _rstd(x)
    xhat = x * rstd
    dg = jnp.sum(dh * xhat, axis=0, keepdims=True)
    dxh = dh * g
    dx = rstd * (dxh - xhat * jnp.mean(dxh * xhat, axis=-1, keepdims=True))
    return dx, dg


def _split3(x):
    hi = x.astype(BF16)
    r1 = x - hi.astype(F32)
    mid = r1.astype(BF16)
    lo = (r1 - mid.astype(F32)).astype(BF16)
    return hi, mid, lo


def _sigmoid(x):
    return jax.nn.sigmoid(x)


def _dot(a, b):
    return jnp.dot(a, b, preferred_element_type=F32)


def _dot_nt(a, b):
    return lax.dot_general(a, b, (((1,), (1,)), ((), ())), preferred_element_type=F32)


def _dot_tn(a, b):
    return lax.dot_general(a, b, (((0,), (0,)), ((), ())), preferred_element_type=F32)


def _heads_to_padded(u):
    n = u.shape[0]
    low = lax.broadcasted_iota(jnp.int32, (n, LANES), 1) < HEAD_DIM
    zero = jnp.zeros((n, LANES), u.dtype)
    cols = []
    for p in range(HEADS // 2):
        pair = u[:, p * LANES:(p + 1) * LANES]
        cols.append(jnp.where(low, pair, zero))
        cols.append(jnp.where(low, pltpu.roll(pair, HEAD_DIM, 1), zero))
    return jnp.concatenate(cols, axis=1)


def _heads_from_padded(x):
    n = x.shape[0]
    low = lax.broadcasted_iota(jnp.int32, (n, LANES), 1) < HEAD_DIM
    cols = []
    for p in range(HEADS // 2):
        even = x[:, (2 * p) * HEAD_PAD:(2 * p + 1) * HEAD_PAD]
        odd = x[:, (2 * p + 1) * HEAD_PAD:(2 * p + 2) * HEAD_PAD]
        cols.append(jnp.where(low, even, pltpu.roll(odd, HEAD_DIM, 1)))
    return jnp.concatenate(cols, axis=1)


def _conv_taps(xb, prev8):
    rows8 = lax.broadcasted_iota(jnp.int32, prev8.shape, 0)
    taps = [xb]
    for j in range(1, CONV_WIDTH):
        r = pltpu.roll(xb, j, 0)
        p = pltpu.roll(prev8, j, 0)
        head = jnp.where(rows8 < j, p, r[0:SUBLANES])
        taps.append(jnp.concatenate([head, r[SUBLANES:]], axis=0))
    return taps


def _lru_pre(taps, cw, cb, wa_ref, ba, wx_ref, bx, a_param):
    xc = cb + cw[3:4] * taps[0] + cw[2:3] * taps[1] + cw[1:2] * taps[2] + cw[0:1] * taps[3]
    xcb = xc.astype(BF16)
    ra, ia = [], []
    for n in range(LRU_BLOCKS):
        blk = xcb[:, n * LRU_BLOCK_W:(n + 1) * LRU_BLOCK_W]
        ra.append(_dot(blk, wa_ref[n]))
        ia.append(_dot(blk, wx_ref[n]))
    r = _sigmoid(jnp.concatenate(ra, axis=1) + ba)
    i = _sigmoid(jnp.concatenate(ia, axis=1) + bx)
    z = -a_param
    sp = jnp.maximum(z, 0.0) + jnp.log1p(jnp.exp(-jnp.abs(z)))
    log_a = (-LRU_C) * r * sp
    a = jnp.exp(log_a)
    one_minus_a2 = -jnp.tanh(log_a) * (a * a + 1.0)
    mult = jnp.sqrt(one_minus_a2)
    return xc, xcb, r, i, sp, a, mult


def _lru_in_fwd(x, g0, w_in, later_shards, ts):
    s = x.shape[0]
    nt = s // ts
    n = len(later_shards)

    def body(*refs):
        x_ref, g_ref, w_ref = refs[:3]
        shard_refs = refs[3:3 + n]
        xb_ref, gate_ref, h_ref = refs[3 + n:6 + n]
        wfull_ref = refs[6 + 2 * n]
        start, forward, finish = _gather_phases(shard_refs, refs[6 + n:6 + 2 * n],
                                                *refs[7 + 2 * n:])
        step = pl.program_id(0)
        pl.when(step == 0)(start)

        @pl.when(step == 0)
        def _():
            for j in range(N_DEV):
                wfull_ref[:, j * LRU_IN_SHARD:(j + 1) * LRU_IN_SHARD] = w_ref[j]

        xv = x_ref[...]
        h = (xv * _rstd(xv) * g_ref[...]).astype(BF16)
        u = _dot(h, wfull_ref[...])
        xb_ref[...] = u[:, :LRU_WIDTH]
        gate_ref[...] = u[:, LRU_WIDTH:]
        h_ref[...] = h
        pl.when(step == (2 * nt) // 3)(forward)
        pl.when(step == nt - 1)(finish)

    hbm = pl.BlockSpec(memory_space=pl.ANY)
    res = pl.pallas_call(
        body, name="lru_in_fwd", grid=(nt,),
        in_specs=[pl.BlockSpec((ts, D_MODEL), lambda i: (i, 0)),
                  _const_spec((1, D_MODEL)),
                  _const_spec((N_DEV, D_MODEL, LRU_IN_SHARD))] + [hbm] * n,
        out_specs=[pl.BlockSpec((ts, LRU_WIDTH), lambda i: (i, 0)),
                   pl.BlockSpec((ts, LRU_WIDTH), lambda i: (i, 0)),
                   pl.BlockSpec((ts, D_MODEL), lambda i: (i, 0))] + [hbm] * n,
        out_shape=[jax.ShapeDtypeStruct((s, LRU_WIDTH), F32),
                   jax.ShapeDtypeStruct((s, LRU_WIDTH), F32),
                   jax.ShapeDtypeStruct((s, D_MODEL), BF16)]
        + [jax.ShapeDtypeStruct((N_DEV,) + a.shape, a.dtype) for a in later_shards],
        scratch_shapes=[pltpu.VMEM((D_MODEL, 2 * LRU_WIDTH), BF16)] + _gather_sems(n),
        compiler_params=_params(),
    )(x, g0, w_in, *later_shards)
    return res[0], res[1], res[2], res[3:]


def _lru_core_fwd(xb, gate, cw, cb, wa, ba, wx, bx, a_param, ts):
    s = xb.shape[0]

    def body(xb_ref, gate_ref, cw_ref, cb_ref, wa_ref, ba_ref, wx_ref, bx_ref, ap_ref,
             y_ref, hs_ref, prev_ref, hcar_ref):
        @pl.when(pl.program_id(0) == 0)
        def _():
            prev_ref[...] = jnp.zeros_like(prev_ref)
            hcar_ref[...] = jnp.zeros_like(hcar_ref)

        xbv = xb_ref[...]
        taps = _conv_taps(xbv, prev_ref[...])
        xc, _, _, i, _, a, mult = _lru_pre(taps, cw_ref[...], cb_ref[...], wa_ref, ba_ref[...],
                                           wx_ref, bx_ref[...], ap_ref[...])
        bterm = mult * (i * xc)
        cum_a, hloc = _scan_rows(a, bterm)
        hs = cum_a * hcar_ref[SUBLANES - 1:SUBLANES, :] + hloc
        gv = gate_ref[...]
        y_ref[...] = (hs * (gv * _sigmoid(gv))).astype(BF16)
        hs_ref[...] = hs
        prev_ref[...] = xbv[ts - SUBLANES:, :]
        hcar_ref[...] = hs[ts - SUBLANES:, :]

    vec = _const_spec((1, LRU_WIDTH))
    blk = _const_spec((LRU_BLOCKS, LRU_BLOCK_W, LRU_BLOCK_W))
    tile = pl.BlockSpec((ts, LRU_WIDTH), lambda i: (i, 0))
    return pl.pallas_call(
        body, name="lru_core_fwd", grid=(s // ts,),
        in_specs=[tile, tile, _const_spec((CONV_WIDTH, LRU_WIDTH)), vec, blk, vec, blk, vec, vec],
        out_specs=[tile, tile],
        out_shape=[jax.ShapeDtypeStruct((s, LRU_WIDTH), BF16),
                   jax.ShapeDtypeStruct((s, LRU_WIDTH), F32)],
        scratch_shapes=[pltpu.VMEM((SUBLANES, LRU_WIDTH), F32),
                        pltpu.VMEM((SUBLANES, LRU_WIDTH), F32)],
        compiler_params=_params(),
    )(xb, gate, cw, cb, wa, ba, wx, bx, a_param)


def _fox_pre_fwd(x, y, w_out, g1, wf, bf, ts):
    s = x.shape[0]

    def body(x_ref, y_ref, w_ref, g_ref, wf_ref, bf_ref, x1_ref, h1_ref, f_ref, cp_ref, ccar_ref):
        @pl.when(pl.program_id(0) == 0)
        def _():
            ccar_ref[...] = jnp.zeros_like(ccar_ref)

        x1 = x_ref[...] + _dot(y_ref[...], w_ref[...])
        h1 = (x1 * _rstd(x1) * g_ref[...]).astype(BF16)
        f = _dot(h1, wf_ref[...]) + bf_ref[...]
        logsig = jnp.minimum(f, 0.0) - jnp.log1p(jnp.exp(-jnp.abs(f)))
        cum = _cumsum_rows(logsig) + ccar_ref[SUBLANES - 1:SUBLANES, :]
        hi, mid, lo = _split3(cum)
        lane = lax.broadcasted_iota(jnp.int32, cum.shape, 1)
        packed = jnp.where(lane < HEADS, hi.astype(F32), jnp.where(
            lane < 2 * HEADS, pltpu.roll(mid.astype(F32), HEADS, 1), jnp.where(
                lane < 3 * HEADS, pltpu.roll(lo.astype(F32), 2 * HEADS, 1), 0.0)))
        x1_ref[...] = x1
        h1_ref[...] = h1
        f_ref[...] = f
        cp_ref[...] = packed.astype(BF16)
        ccar_ref[...] = cum[ts - SUBLANES:, :]

    return pl.pallas_call(
        body, name="fox_pre_fwd", grid=(s // ts,),
        in_specs=[pl.BlockSpec((ts, D_MODEL), lambda i: (i, 0)),
                  pl.BlockSpec((ts, LRU_WIDTH), lambda i: (i, 0)),
                  _const_spec((LRU_WIDTH, D_MODEL)),
                  _const_spec((1, D_MODEL)),
                  _const_spec((D_MODEL, LANES)),
                  _const_spec((1, LANES))],
        out_specs=[pl.BlockSpec((ts, D_MODEL), lambda i: (i, 0)),
                   pl.BlockSpec((ts, D_MODEL), lambda i: (i, 0)),
                   pl.BlockSpec((ts, LANES), lambda i: (i, 0)),
                   pl.BlockSpec((ts, LANES), lambda i: (i, 0))],
        out_shape=[jax.ShapeDtypeStruct((s, D_MODEL), F32),
                   jax.ShapeDtypeStruct((s, D_MODEL), BF16),
                   jax.ShapeDtypeStruct((s, LANES), F32),
                   jax.ShapeDtypeStruct((s, LANES), BF16)],
        scratch_shapes=[pltpu.VMEM((SUBLANES, LANES), F32)],
        compiler_params=_params(),
    )(x, y, w_out, g1, wf, bf)


def _fox_proj_fwd(h1, cparts, w, sel, bias, out_dtype, ts, name):
    s = h1.shape[0]
    ng = w.shape[0]
    width = HEADS * HEAD_DIM
    use_sel = sel is not None

    def body(*refs):
        if use_sel:
            h_ref, cp_ref, w_ref, sel_ref, b_ref, o_ref = refs
            acc = (_heads_to_padded(_dot(h_ref[...], w_ref[...]))
                   + _dot(cp_ref[...], sel_ref[...]) + b_ref[...])
        else:
            h_ref, w_ref, o_ref = refs
            acc = _heads_to_padded(_dot(h_ref[...], w_ref[...]))
        o_ref[...] = acc.astype(out_dtype)

    in_specs = [pl.BlockSpec((ts, D_MODEL), lambda j, i: (i, 0))]
    args = [h1]
    if use_sel:
        in_specs.append(pl.BlockSpec((ts, LANES), lambda j, i: (i, 0)))
        args.append(cparts)
    in_specs.append(pl.BlockSpec((None, D_MODEL, width), lambda j, i: (j, 0, 0)))
    args.append(w)
    if use_sel:
        in_specs.append(pl.BlockSpec((None, LANES, FOX_PAD), lambda j, i: (j, 0, 0)))
        in_specs.append(pl.BlockSpec((None, 1, FOX_PAD), lambda j, i: (j, 0, 0)))
        args += [sel, bias]
    return pl.pallas_call(
        body, name=name, grid=(ng, s // ts),
        in_specs=in_specs,
        out_specs=pl.BlockSpec((None, ts, FOX_PAD), lambda j, i: (j, i, 0)),
        out_shape=jax.ShapeDtypeStruct((ng, s, FOX_PAD), out_dtype),
        compiler_params=_params(2),
    )(*args)


def _attn_fwd(qkv, blk, hps=HEADS_PER_STEP):
    s = qkv.shape[1]
    nblk = s // blk
    wide = 2 * blk
    heads = [slice(i * HEAD_PAD, (i + 1) * HEAD_PAD) for i in range(hps)]

    def body(q_ref, k_ref, v_ref, o_ref, qb_ref, acc_ref, m_ref):
        qi = pl.program_id(1)
        row = lax.broadcasted_iota(jnp.int32, (blk, blk), 0)
        col = lax.broadcasted_iota(jnp.int32, (blk, blk), 1)
        lane = lax.broadcasted_iota(jnp.int32, (blk, HEAD_PAD), 1)
        qs = [q_ref[:, hd] for hd in heads]
        for i in range(hps):
            acc_ref[i] = jnp.zeros((blk, HEAD_PAD), F32)
            m_ref[i] = jnp.full((blk, HEAD_PAD), NEG_BIG, F32)

        def step(k0, size, masked):
            scores = [_dot_nt(q, k_ref[pl.ds(k0, size), hd]) for q, hd in zip(qs, heads)]
            for i, (sc, hd) in enumerate(zip(scores, heads)):
                v = v_ref[pl.ds(k0, size), hd]
                if masked:
                    sc = jnp.where(col <= row, sc, NEG_BIG)
                m = m_ref[i]
                m_new = jnp.maximum(m, jnp.max(sc, axis=-1, keepdims=True))
                p = jnp.exp((sc - jnp.tile(m_new, (1, size // HEAD_PAD))).astype(BF16))
                acc_ref[i] = jnp.exp(m - m_new) * acc_ref[i] + _dot(p, v)
                m_ref[i] = m_new

        def wide_step(kk, _):
            step(pl.multiple_of(kk * wide, wide), wide, False)
            return 0

        lax.fori_loop(0, qi // 2, wide_step, 0)

        @pl.when(qi % 2 == 1)
        def _():
            step(pl.multiple_of((qi - 1) * blk, blk), blk, False)

        step(pl.multiple_of(qi * blk, blk), blk, True)
        for i, (q, hd) in enumerate(zip(qs, heads)):
            acc = acc_ref[i]
            l = jnp.broadcast_to(acc[:, LANE_ONE_V:LANE_ONE_V + 1], (blk, HEAD_PAD))
            o_ref[:, hd] = (acc / l).astype(BF16)
            hi, mid, lo = _split3(-(m_ref[i] + jnp.log(l)))
            qb_ref[:, hd] = jnp.where(lane == LANE_LSE, hi, jnp.where(
                lane == LANE_LSE + 1, mid, jnp.where(lane == LANE_LSE + 2, lo, q)))

    width = hps * HEAD_PAD

    def whole(j):
        return pl.BlockSpec((None, s, width), lambda h, i: (j, 0, h))

    out_spec = pl.BlockSpec((blk, width), lambda h, i: (i, h))
    return pl.pallas_call(
        body, name="attn_fwd", grid=(HEADS // hps, nblk),
        in_specs=[pl.BlockSpec((None, blk, width), lambda h, i: (0, i, h)), whole(1), whole(2)],
        out_specs=[out_spec, out_spec],
        out_shape=[jax.ShapeDtypeStruct((s, FOX_PAD), BF16),
                   jax.ShapeDtypeStruct((s, FOX_PAD), BF16)],
        scratch_shapes=[pltpu.VMEM((hps, blk, HEAD_PAD), F32),
                        pltpu.VMEM((hps, blk, HEAD_PAD), F32)],
        compiler_params=_params(2),
    )(qkv, qkv, qkv)


def _fox_out_loss(o, gate, w_out, x1, target, gf, ts):
    s = x1.shape[0]

    def body(o_ref, gt_ref, w_ref, x1_ref, t_ref, g_ref, dx2_ref, dx2b_ref, y2_ref, loss_ref,
             gfin_ref):
        @pl.when(pl.program_id(0) == 0)
        def _():
            loss_ref[...] = jnp.zeros_like(loss_ref)
            gfin_ref[...] = jnp.zeros_like(gfin_ref)

        gv = gt_ref[...]
        y2 = _heads_from_padded(o_ref[...] * (gv * _sigmoid(gv))).astype(BF16)
        x2 = x1_ref[...] + _dot(y2, w_ref[...])
        rstd = _rstd(x2)
        xhat = x2 * rstd
        g = g_ref[...]
        diff = xhat * g - t_ref[...]
        loss_ref[...] += 0.5 * jnp.sum(jnp.mean(diff * diff, axis=-1, keepdims=True))
        dy = diff * (1.0 / D_MODEL)
        gfin_ref[...] += jnp.sum(dy * xhat, axis=0, keepdims=True)
        dxh = dy * g
        dx2 = rstd * (dxh - xhat * jnp.mean(dxh * xhat, axis=-1, keepdims=True))
        dx2_ref[...] = dx2
        dx2b_ref[...] = dx2.astype(BF16)
        y2_ref[...] = y2

    return pl.pallas_call(
        body, name="fox_out_loss", grid=(s // ts,),
        in_specs=[pl.BlockSpec((ts, FOX_PAD), lambda i: (i, 0)),
                  pl.BlockSpec((ts, FOX_PAD), lambda i: (i, 0)),
                  _const_spec((HEADS * HEAD_DIM, D_MODEL)),
                  pl.BlockSpec((ts, D_MODEL), lambda i: (i, 0)),
                  pl.BlockSpec((ts, D_MODEL), lambda i: (i, 0)),
                  _const_spec((1, D_MODEL))],
        out_specs=[pl.BlockSpec((ts, D_MODEL), lambda i: (i, 0)),
                   pl.BlockSpec((ts, D_MODEL), lambda i: (i, 0)),
                   pl.BlockSpec((ts, HEADS * HEAD_DIM), lambda i: (i, 0)),
                   pl.BlockSpec((SUBLANES, LANES), lambda i: (0, 0)),
                   pl.BlockSpec((1, D_MODEL), lambda i: (0, 0))],
        out_shape=[jax.ShapeDtypeStruct((s, D_MODEL), F32),
                   jax.ShapeDtypeStruct((s, D_MODEL), BF16),
                   jax.ShapeDtypeStruct((s, HEADS * HEAD_DIM), BF16),
                   jax.ShapeDtypeStruct((SUBLANES, LANES), F32),
                   jax.ShapeDtypeStruct((1, D_MODEL), F32)],
        compiler_params=_params(),
    )(o, gate, w_out, x1, target, gf)


def _fox_out_bwd(dx2, w_out, o, gate, ts):
    s = dx2.shape[0]

    def body(dx_ref, w_ref, o_ref, gt_ref, do_ref, dg_ref):
        lane = lax.broadcasted_iota(jnp.int32, (ts, HEAD_PAD), 1)
        dy2 = _heads_to_padded(_dot_nt(dx_ref[...], w_ref[...]))
        gv = gt_ref[...]
        sg = _sigmoid(gv)
        ov = o_ref[...]
        dov = dy2 * (gv * sg)
        dg_ref[...] = (dy2 * ov * (sg * (1.0 + gv * (1.0 - sg)))).astype(BF16)
        prod = dov * ov
        for h in range(HEADS):
            sl = slice(h * HEAD_PAD, (h + 1) * HEAD_PAD)
            delta = jnp.sum(prod[:, sl], axis=-1, keepdims=True)
            hi = delta.astype(BF16)
            lo = (delta - hi.astype(F32)).astype(BF16)
            do_h = dov[:, sl].astype(BF16)
            do_ref[:, sl] = jnp.where(lane == LANE_ONE_V, -hi,
                                      jnp.where(lane == LANE_ONE_V + 1, -lo, do_h))

    tile = pl.BlockSpec((ts, FOX_PAD), lambda i: (i, 0))
    return pl.pallas_call(
        body, name="fox_out_bwd", grid=(s // ts,),
        in_specs=[pl.BlockSpec((ts, D_MODEL), lambda i: (i, 0)),
                  _const_spec((HEADS * HEAD_DIM, D_MODEL)), tile, tile],
        out_specs=[tile, tile],
        out_shape=[jax.ShapeDtypeStruct((s, FOX_PAD), BF16),
                   jax.ShapeDtypeStruct((s, FOX_PAD), BF16)],
        compiler_params=_params(),
    )(dx2, w_out, o, gate)


def _attn_bwd(qb, qkv, do, blk):
    s = qb.shape[0]
    nblk = s // blk
    half = blk // 2
    heads = [slice(i * HEAD_PAD, (i + 1) * HEAD_PAD) for i in range(HEADS_PER_STEP)]

    def body(q_ref, k_ref, v_ref, do_ref, dq_ref, dk_ref, dv_ref, dcum_ref, dq_acc, dkt_acc,
             dvt_acc, qt_ref, dot_ref):
        group = pl.program_id(0)
        kj = pl.program_id(1)
        row = lax.broadcasted_iota(jnp.int32, (blk, blk), 0)
        col = lax.broadcasted_iota(jnp.int32, (blk, blk), 1)
        lane = lax.broadcasted_iota(jnp.int32, (blk, LANES), 1)
        mine = [lane == group * HEADS_PER_STEP + i for i in range(HEADS_PER_STEP)]

        @pl.when(kj == 0)
        def _():
            dq_acc[...] = jnp.zeros_like(dq_acc)

            def transpose_block(bi, _):
                r0 = pl.multiple_of(bi * blk, blk)
                for i, hd in enumerate(heads):
                    qt_ref[i, bi] = q_ref[pl.ds(r0, blk), hd].T
                    dot_ref[i, bi] = do_ref[pl.ds(r0, blk), hd].T
                return 0

            lax.fori_loop(0, nblk, transpose_block, 0)

        @pl.when((group == 0) & (kj == 0))
        def _():
            dcum_ref[...] = jnp.zeros_like(dcum_ref)

        k0 = pl.multiple_of(kj * blk, blk)
        ks = [k_ref[:, hd] for hd in heads]
        vs = [v_ref[:, hd] for hd in heads]

        def step(qi, q_lo, nq, k_lo, nk, masked):
            q0 = pl.multiple_of(qi * blk + q_lo, half)
            qs = [q_ref[pl.ds(q0, nq), hd] for hd in heads]
            dos = [do_ref[pl.ds(q0, nq), hd] for hd in heads]
            kk = [k[k_lo:k_lo + nk] for k in ks]
            vv = [v[k_lo:k_lo + nk] for v in vs]
            scores = [_dot_nt(q, k) for q, k in zip(qs, kk)]
            dps = [_dot_nt(dov, v) for dov, v in zip(dos, vv)]
            for i, (hd, k, sc, dp) in enumerate(zip(heads, kk, scores, dps)):
                p = jnp.exp(sc.astype(BF16))
                if masked:
                    p = jnp.where(col[:nq, :nk] + k_lo <= row[:nq, :nk] + q_lo, p,
                                  jnp.zeros_like(p))
                ds = (p.astype(F32) * dp).astype(BF16)
                dvt = _dot(dot_ref[i, qi, :, q_lo:q_lo + nq], p)
                dkt = _dot(qt_ref[i, qi, :, q_lo:q_lo + nq], ds)
                if masked:
                    dvt_acc[i, :, k_lo:k_lo + nk] = dvt
                    dkt_acc[i, :, k_lo:k_lo + nk] = dkt
                else:
                    dvt_acc[i, :, k_lo:k_lo + nk] += dvt
                    dkt_acc[i, :, k_lo:k_lo + nk] += dkt
                dq_acc[pl.ds(q0, nq), hd] += _dot(ds, k)

        step(kj, 0, blk, 0, half, True)
        step(kj, half, half, half, half, True)

        def q_step(qi, _):
            step(qi, 0, blk, 0, blk, False)
            return 0

        lax.fori_loop(kj + 1, nblk, q_step, 0)
        dcum = dcum_ref[pl.ds(k0, blk), :]
        for i, (hd, mask) in enumerate(zip(heads, mine)):
            dk = dkt_acc[i].T
            dk_ref[:, hd] = dk.astype(BF16)
            dv_ref[:, hd] = dvt_acc[i].astype(BF16).T
            dcum = jnp.where(mask, -dk[:, LANE_CK:LANE_CK + 1], dcum)
        dcum_ref[pl.ds(k0, blk), :] = dcum

        @pl.when(kj == nblk - 1)
        def _():
            def finish(bi, _):
                r0 = pl.multiple_of(bi * blk, blk)
                dcum = dcum_ref[pl.ds(r0, blk), :]
                for hd, mask in zip(heads, mine):
                    dq = dq_acc[pl.ds(r0, blk), hd]
                    dq_ref[pl.ds(r0, blk), hd] = dq.astype(BF16)
                    dcum = dcum + jnp.where(mask, dq[:, LANE_RB:LANE_RB + 1], 0.0)
                dcum_ref[pl.ds(r0, blk), :] = dcum
                return 0

            lax.fori_loop(0, nblk, finish, 0)

    width = HEADS_PER_STEP * HEAD_PAD
    whole = pl.BlockSpec((s, width), lambda h, j: (0, h))
    whole_in = pl.BlockSpec((s, width), lambda h, j: (0, h), pipeline_mode=pl.Buffered(1))
    part = pl.BlockSpec((blk, width), lambda h, j: (j, h))
    out = jax.ShapeDtypeStruct((s, FOX_PAD), BF16)
    return pl.pallas_call(
        body, name="attn_bwd", grid=(HEADS // HEADS_PER_STEP, nblk),
        in_specs=[whole_in,
                  pl.BlockSpec((None, blk, width), lambda h, j: (1, j, h)),
                  pl.BlockSpec((None, blk, width), lambda h, j: (2, j, h)),
                  whole_in],
        out_specs=[whole, part, part, pl.BlockSpec((s, LANES), lambda h, j: (0, 0))],
        out_shape=[out, out, out, jax.ShapeDtypeStruct((s, LANES), F32)],
        scratch_shapes=[pltpu.VMEM((s, width), F32),
                        pltpu.VMEM((HEADS_PER_STEP, HEAD_PAD, blk), F32),
                        pltpu.VMEM((HEADS_PER_STEP, HEAD_PAD, blk), F32),
                        pltpu.VMEM((HEADS_PER_STEP, nblk, HEAD_PAD, blk), BF16),
                        pltpu.VMEM((HEADS_PER_STEP, nblk, HEAD_PAD, blk), BF16)],
        compiler_params=_params(2),
    )(qb, qkv, qkv, do)


def _fox_in_bwd(dq, dk, dv, dg, wt, wft, dcum, f, x1, dx2, g1, ts):
    s = x1.shape[0]
    nt = s // ts
    width = HEADS * HEAD_DIM

    def body(dq_ref, dk_ref, dv_ref, dg_ref, wt_ref, wft_ref, dcum_ref, f_ref, x1_ref, dx2_ref,
             g_ref, dx1_ref, dx1b_ref, df_ref, du_ref, gn_ref, gbf_ref, rcar_ref):
        @pl.when(pl.program_id(0) == 0)
        def _():
            rcar_ref[...] = jnp.zeros_like(rcar_ref)
            gn_ref[...] = jnp.zeros_like(gn_ref)
            gbf_ref[...] = jnp.zeros_like(gbf_ref)

        rsum = _cumsum_rows(dcum_ref[...], reverse=True) + rcar_ref[0:1, :]
        df = rsum * _sigmoid(-f_ref[...])
        dfb = df.astype(BF16)
        dh = _dot_nt(dfb, wft_ref[...])
        for j, ref in enumerate((dq_ref, dk_ref, dv_ref, dg_ref)):
            du = _heads_from_padded(ref[...])
            du_ref[j] = du
            dh = dh + _dot_nt(du, wt_ref[j])
        dxn, dgn = _norm_bwd(x1_ref[...], g_ref[...], dh)
        dx1 = dx2_ref[...] + dxn
        dx1_ref[...] = dx1
        dx1b_ref[...] = dx1.astype(BF16)
        df_ref[...] = dfb
        gn_ref[...] += dgn
        gbf_ref[...] += jnp.sum(df, axis=0, keepdims=True)
        rcar_ref[...] = rsum[0:SUBLANES, :]

    rev = lambda i: (nt - 1 - i, 0)
    wide = pl.BlockSpec((ts, FOX_PAD), rev)
    return pl.pallas_call(
        body, name="fox_in_bwd", grid=(nt,),
        in_specs=[wide, wide, wide, wide,
                  _const_spec((4, D_MODEL, width)),
                  _const_spec((D_MODEL, LANES)),
                  pl.BlockSpec((ts, LANES), rev),
                  pl.BlockSpec((ts, LANES), rev),
                  pl.BlockSpec((ts, D_MODEL), rev),
                  pl.BlockSpec((ts, D_MODEL), rev),
                  _const_spec((1, D_MODEL))],
        out_specs=[pl.BlockSpec((ts, D_MODEL), rev),
                   pl.BlockSpec((ts, D_MODEL), rev),
                   pl.BlockSpec((ts, LANES), rev),
                   pl.BlockSpec((4, ts, width), lambda i: (0, nt - 1 - i, 0)),
                   pl.BlockSpec((1, D_MODEL), lambda i: (0, 0)),
                   pl.BlockSpec((1, LANES), lambda i: (0, 0))],
        out_shape=[jax.ShapeDtypeStruct((s, D_MODEL), F32),
                   jax.ShapeDtypeStruct((s, D_MODEL), BF16),
                   jax.ShapeDtypeStruct((s, LANES), BF16),
                   jax.ShapeDtypeStruct((4, s, width), BF16),
                   jax.ShapeDtypeStruct((1, D_MODEL), F32),
                   jax.ShapeDtypeStruct((1, LANES), F32)],
        scratch_shapes=[pltpu.VMEM((SUBLANES, LANES), F32)],
        compiler_params=_params(),
    )(dq, dk, dv, dg, wt, wft, dcum, f, x1, dx2, g1)


def _lru_core_bwd(dx1b, w_out, xb, gate, hs, cw, cb, wa, ba, wx, bx, a_param, wa_t, wx_t,
                  chip_sums, ts):
    s = xb.shape[0]
    nt = s // ts
    tpb = ts // SUBLANES
    n_ex = len(chip_sums)

    def body(*refs):
        (dx_ref, wo_ref, xb_ref, xbh_ref, gate_ref, hs_ref, hsh_ref, cw_ref, cb_ref, wa_ref,
         ba_ref, wx_ref, bx_ref, ap_ref, wat_ref, wxt_ref) = refs[:16]
        sum_refs = refs[16:16 + n_ex]
        du_ref, gwa_ref, gwx_ref, gvec_ref = refs[16 + n_ex:20 + n_ex]
        got_refs = refs[20 + n_ex:20 + 2 * n_ex]
        acar_ref, dhcar_ref, dxccar_ref = refs[20 + 2 * n_ex:23 + 2 * n_ex]
        start, finish = _chip_exchange_phases(sum_refs, got_refs, *refs[23 + 2 * n_ex:])
        step = pl.program_id(0)
        pl.when(step == 0)(start)

        @pl.when(step == 0)
        def _():
            acar_ref[...] = jnp.zeros_like(acar_ref)
            dhcar_ref[...] = jnp.zeros_like(dhcar_ref)
            dxccar_ref[...] = jnp.zeros_like(dxccar_ref)
            gwa_ref[...] = jnp.zeros_like(gwa_ref)
            gwx_ref[...] = jnp.zeros_like(gwx_ref)
            gvec_ref[...] = jnp.zeros_like(gvec_ref)

        first_tile = step == nt - 1
        halo_on = jnp.where(first_tile, 0.0, 1.0)
        prev8 = xbh_ref[...] * halo_on
        hprev_row = hsh_ref[SUBLANES - 1:SUBLANES, :] * halo_on

        xbv = xb_ref[...]
        taps = _conv_taps(xbv, prev8)
        cw_v = cw_ref[...]
        xc, xcb, r, i, sp, a, mult = _lru_pre(taps, cw_v, cb_ref[...], wa_ref, ba_ref[...],
                                              wx_ref, bx_ref[...], ap_ref[...])
        hs = hs_ref[...]
        gv = gate_ref[...]
        sg = _sigmoid(gv)
        dy = _dot_nt(dx_ref[...], wo_ref[...])
        dhs = dy * (gv * sg)
        dgate = dy * hs * (sg * (1.0 + gv * (1.0 - sg)))

        rows = lax.broadcasted_iota(jnp.int32, a.shape, 0)
        a_next = jnp.where(rows < ts - 1, pltpu.roll(a, ts - 1, 0), acar_ref[0:1, :])
        cum_a, dh_loc = _scan_rows(a_next, dhs, reverse=True)
        dh = cum_a * dhcar_ref[0:1, :] + dh_loc
        h_prev = jnp.where(rows >= 1, pltpu.roll(hs, 1, 0), hprev_row)

        da = dh * h_prev
        ixc = i * xc
        dmult = dh * ixc
        di = dh * mult * xc
        dxc = dh * mult * i
        dlog_a = da * a - dmult * (a * a) / mult
        dr = dlog_a * ((-LRU_C) * sp)
        dsp = jnp.sum(dlog_a * ((-LRU_C) * r), axis=0, keepdims=True)
        dra = dr * r * (1.0 - r)
        dia = di * i * (1.0 - i)
        drab = dra.astype(BF16)
        diab = dia.astype(BF16)
        back = []
        for n in range(LRU_BLOCKS):
            sl = slice(n * LRU_BLOCK_W, (n + 1) * LRU_BLOCK_W)
            gwa_ref[n] += _dot_tn(xcb[:, sl], drab[:, sl])
            gwx_ref[n] += _dot_tn(xcb[:, sl], diab[:, sl])
            back.append(_dot(drab[:, sl], wat_ref[n]) + _dot(diab[:, sl], wxt_ref[n]))
        dxc = dxc + jnp.concatenate(back, axis=1)

        nxt8 = dxccar_ref[...]
        rows8 = lax.broadcasted_iota(jnp.int32, nxt8.shape, 0)
        dxb = cw_v[3:4] * dxc
        for j in range(1, CONV_WIDTH):
            rj = pltpu.roll(dxc, ts - j, 0)
            pj = pltpu.roll(nxt8, SUBLANES - j, 0)
            tail = jnp.where(rows8 >= SUBLANES - j, pj, rj[ts - SUBLANES:])
            dxb = dxb + cw_v[3 - j:4 - j] * jnp.concatenate([rj[:ts - SUBLANES], tail], axis=0)

        du_ref[:, :LRU_WIDTH] = dxb.astype(BF16)
        du_ref[:, LRU_WIDTH:] = dgate.astype(BF16)

        z = -ap_ref[...]
        gvec = [jnp.sum(dxc * taps[3 - k], axis=0, keepdims=True) for k in range(CONV_WIDTH)]
        gvec.append(jnp.sum(dxc, axis=0, keepdims=True))
        gvec.append(jnp.sum(dra, axis=0, keepdims=True))
        gvec.append(jnp.sum(dia, axis=0, keepdims=True))
        gvec.append(-dsp * _sigmoid(z))
        gvec_ref[...] += jnp.concatenate(gvec, axis=0)

        acar_ref[...] = a[0:SUBLANES, :]
        dhcar_ref[...] = dh[0:SUBLANES, :]
        dxccar_ref[...] = dxc[0:SUBLANES, :]
        pl.when(step == nt - 1)(finish)

    rev = lambda i: (nt - 1 - i, 0)
    halo = lambda i: (jnp.maximum((nt - 1 - i) * tpb - 1, 0), 0)
    tile = pl.BlockSpec((ts, LRU_WIDTH), rev)
    halo_spec = pl.BlockSpec((SUBLANES, LRU_WIDTH), halo)
    vec = _const_spec((1, LRU_WIDTH))
    blk = _const_spec((LRU_BLOCKS, LRU_BLOCK_W, LRU_BLOCK_W))
    acc_blk = pl.BlockSpec((LRU_BLOCKS, LRU_BLOCK_W, LRU_BLOCK_W), lambda i: (0, 0, 0))
    hbm = pl.BlockSpec(memory_space=pl.ANY)
    res = pl.pallas_call(
        body, name="lru_core_bwd", grid=(nt,),
        in_specs=[pl.BlockSpec((ts, D_MODEL), rev),
                  _const_spec((LRU_WIDTH, D_MODEL)),
                  tile, halo_spec, tile, tile, halo_spec,
                  _const_spec((CONV_WIDTH, LRU_WIDTH)), vec, blk, vec, blk, vec, vec, blk, blk]
        + [hbm] * n_ex,
        out_specs=[pl.BlockSpec((ts, 2 * LRU_WIDTH), rev), acc_blk, acc_blk,
                   pl.BlockSpec((SUBLANES, LRU_WIDTH), lambda i: (0, 0))] + [hbm] * n_ex,
        out_shape=[jax.ShapeDtypeStruct((s, 2 * LRU_WIDTH), BF16),
                   jax.ShapeDtypeStruct((LRU_BLOCKS, LRU_BLOCK_W, LRU_BLOCK_W), F32),
                   jax.ShapeDtypeStruct((LRU_BLOCKS, LRU_BLOCK_W, LRU_BLOCK_W), F32),
                   jax.ShapeDtypeStruct((SUBLANES, LRU_WIDTH), F32)]
        + [jax.ShapeDtypeStruct(a.shape, a.dtype) for a in chip_sums],
        scratch_shapes=[pltpu.VMEM((SUBLANES, LRU_WIDTH), F32),
                        pltpu.VMEM((SUBLANES, LRU_WIDTH), F32),
                        pltpu.VMEM((SUBLANES, LRU_WIDTH), F32)] + _chip_exchange_sems(n_ex),
        compiler_params=_params(),
    )(dx1b, w_out, xb, xb, gate, hs, hs, cw, cb, wa, ba, wx, bx, a_param, wa_t, wx_t, *chip_sums)
    return res[0], res[1], res[2], res[3], res[4:]


def _lru_in_bwd(du, w_in, x, dx1, g0, chip_sums, ts):
    s = x.shape[0]
    nt = s // ts
    n = len(chip_sums)

    def body(*refs):
        du_ref, w_ref, x_ref, dx1_ref, g_ref = refs[:5]
        sum_refs = refs[5:5 + n]
        gx_ref, gn_ref = refs[5 + n:7 + n]
        got_refs = refs[7 + n:7 + 2 * n]
        wfull_ref = refs[7 + 2 * n]
        start, finish = _chip_exchange_phases(sum_refs, got_refs, *refs[8 + 2 * n:])
        step = pl.program_id(0)
        pl.when(step == 0)(start)

        @pl.when(step == 0)
        def _():
            gn_ref[...] = jnp.zeros_like(gn_ref)
            for j in range(N_DEV):
                wfull_ref[:, j * LRU_IN_SHARD:(j + 1) * LRU_IN_SHARD] = w_ref[j]

        dh = _dot_nt(du_ref[...], wfull_ref[...])
        dxn, dgn = _norm_bwd(x_ref[...], g_ref[...], dh)
        gx_ref[...] = dx1_ref[...] + dxn
        gn_ref[...] += dgn
        pl.when(step == nt - 1)(finish)

    tile = pl.BlockSpec((ts, D_MODEL), lambda i: (i, 0))
    hbm = pl.BlockSpec(memory_space=pl.ANY)
    res = pl.pallas_call(
        body, name="lru_in_bwd", grid=(nt,),
        in_specs=[pl.BlockSpec((ts, 2 * LRU_WIDTH), lambda i: (i, 0)),
                  _const_spec((N_DEV, D_MODEL, LRU_IN_SHARD)), tile, tile,
                  _const_spec((1, D_MODEL))] + [hbm] * n,
        out_specs=[tile, pl.BlockSpec((1, D_MODEL), lambda i: (0, 0))] + [hbm] * n,
        out_shape=[jax.ShapeDtypeStruct((s, D_MODEL), F32),
                   jax.ShapeDtypeStruct((1, D_MODEL), F32)]
        + [jax.ShapeDtypeStruct(a.shape, a.dtype) for a in chip_sums],
        scratch_shapes=[pltpu.VMEM((D_MODEL, 2 * LRU_WIDTH), BF16)] + _chip_exchange_sems(n),
        compiler_params=_params(),
    )(du, w_in, x, dx1, g0, *chip_sums)
    return res[0], res[1], res[2:]


def _weight_grad(a, b, ts, name, scale=1.0, col_shards=1):
    s, ka = a.shape
    nb = b.shape[1]
    nt = s // ts
    per = nb // col_shards

    def body(a_ref, b_ref, o_ref):
        @pl.when(pl.program_id(0) == 0)
        def _():
            o_ref[...] = jnp.zeros_like(o_ref)

        if col_shards == 1:
            o_ref[...] += _dot_tn(a_ref[...], b_ref[...])
        else:
            acc = _dot_tn(a_ref[...], b_ref[...])
            for j in range(col_shards):
                o_ref[j] += acc[:, j * per:(j + 1) * per]
        if scale != 1.0:
            @pl.when(pl.program_id(0) == nt - 1)
            def _():
                o_ref[...] = o_ref[...] * scale

    out_dims = (ka, nb) if col_shards == 1 else (col_shards, ka, per)
    return pl.pallas_call(
        body, name=name, grid=(nt,),
        in_specs=[pl.BlockSpec((ts, ka), lambda i: (i, 0)),
                  pl.BlockSpec((ts, nb), lambda i: (i, 0))],
        out_specs=pl.BlockSpec(out_dims, lambda i: (0,) * len(out_dims)),
        out_shape=jax.ShapeDtypeStruct(out_dims, F32),
        compiler_params=_params(),
    )(a, b)


def _sum_parts(gp_ref):
    g = gp_ref[0].astype(F32)
    for k in range(1, gp_ref.shape[0]):
        g = g + gp_ref[k].astype(F32)
    return g


def _adamw(g_parts, w, m, v, tr, name):
    nparts, rows, cols = g_parts.shape

    def body(gp_ref, w_ref, m_ref, v_ref, g_ref, d_ref, mo_ref, vo_ref):
        g = _sum_parts(gp_ref)
        m2 = ADAM_B1 * m_ref[...] + (1.0 - ADAM_B1) * g
        v2 = ADAM_B2 * v_ref[...] + (1.0 - ADAM_B2) * (g * g)
        m_hat = m2 / (1.0 - ADAM_B1 ** ADAM_STEP)
        v_hat = v2 / (1.0 - ADAM_B2 ** ADAM_STEP)
        g_ref[...] = g
        d_ref[...] = (-ADAM_LR) * (m_hat / (jnp.sqrt(v_hat) + ADAM_EPS) + ADAM_WD * w_ref[...])
        mo_ref[...] = m2
        vo_ref[...] = v2

    tile = pl.BlockSpec((tr, cols), lambda i: (i, 0))
    out = jax.ShapeDtypeStruct((rows, cols), F32)
    return pl.pallas_call(
        body, name=name, grid=(rows // tr,),
        in_specs=[pl.BlockSpec((nparts, tr, cols), lambda i: (0, i, 0)), tile, tile, tile],
        out_specs=[tile, tile, tile, tile],
        out_shape=[out, out, out, out],
        compiler_params=_params(),
    )(g_parts, w, m, v)


def _reduce_parts(g_parts, name):
    _, rows, cols = g_parts.shape

    def body(gp_ref, g_ref):
        g_ref[...] = _sum_parts(gp_ref)

    return pl.pallas_call(
        body, name=name,
        out_shape=jax.ShapeDtypeStruct((rows, cols), F32),
        compiler_params=pltpu.CompilerParams(vmem_limit_bytes=VMEM_LIMIT_BYTES),
    )(g_parts)


def _mesh_pos():
    ix, iy, ic = lax.axis_index("x"), lax.axis_index("y"), lax.axis_index("c")
    return ix, iy, ic


def _peer(ix, iy, ic, mask):
    px = 1 - ix if mask & 4 else ix
    py = 1 - iy if mask & 2 else iy
    pc = 1 - ic if mask & 1 else ic
    return (px, py, pc), 4 * px + 2 * py + pc


def _exchange(arrays, scatter, name):
    n = len(arrays)

    def body(*refs):
        x_refs, o_refs = refs[:n], refs[n:2 * n]
        send_sems, recv_sems, local_sems = refs[2 * n:]
        ix, iy, ic = _mesh_pos()
        me = 4 * ix + 2 * iy + ic

        def src(a, dest):
            return x_refs[a].at[dest] if scatter else x_refs[a]

        local = [pltpu.make_async_copy(src(a, me), o_refs[a].at[me], local_sems.at[a])
                 for a in range(n)]
        for cp in local:
            cp.start()
        sends = []
        for mask in range(1, N_DEV):
            peer, pidx = _peer(ix, iy, ic, mask)
            for a in range(n):
                cp = pltpu.make_async_remote_copy(
                    src_ref=src(a, pidx), dst_ref=o_refs[a].at[me],
                    send_sem=send_sems.at[a, mask - 1], recv_sem=recv_sems.at[a, mask - 1],
                    device_id=peer, device_id_type=pl.DeviceIdType.MESH)
                cp.start()
                sends.append(cp)
        for mask in range(1, N_DEV):
            peer, pidx = _peer(ix, iy, ic, mask)
            for a in range(n):
                pltpu.make_async_remote_copy(
                    src_ref=src(a, me), dst_ref=o_refs[a].at[pidx],
                    send_sem=send_sems.at[a, mask - 1], recv_sem=recv_sems.at[a, mask - 1],
                    device_id=peer, device_id_type=pl.DeviceIdType.MESH).wait_recv()
        for cp in sends:
            cp.wait_send()
        for cp in local:
            cp.wait()

    out_shape = [jax.ShapeDtypeStruct(x.shape if scatter else (N_DEV,) + x.shape, x.dtype)
                 for x in arrays]
    return pl.pallas_call(
        body, name=name,
        in_specs=[pl.BlockSpec(memory_space=pl.ANY)] * n,
        out_specs=[pl.BlockSpec(memory_space=pl.ANY)] * n,
        out_shape=out_shape,
        scratch_shapes=[pltpu.SemaphoreType.DMA((n, N_DEV - 1)),
                        pltpu.SemaphoreType.DMA((n, N_DEV - 1)),
                        pltpu.SemaphoreType.DMA((n,))],
    )(*arrays)


def _gather_two_level(arrays, name):
    n = len(arrays)

    def body(*refs):
        start, forward, finish = _gather_phases(refs[:n], refs[n:2 * n], *refs[2 * n:])
        start()
        forward()
        finish()

    return pl.pallas_call(
        body, name=name,
        in_specs=[pl.BlockSpec(memory_space=pl.ANY)] * n,
        out_specs=[pl.BlockSpec(memory_space=pl.ANY)] * n,
        out_shape=[jax.ShapeDtypeStruct((N_DEV,) + x.shape, x.dtype) for x in arrays],
        scratch_shapes=_gather_sems(n),
    )(*arrays)


def _gather_sems(n):
    return [pltpu.SemaphoreType.DMA((n, N_DEV - 1)), pltpu.SemaphoreType.DMA((n, N_DEV - 1)),
            pltpu.SemaphoreType.DMA((n,))]


def _gather_phases(x_refs, o_refs, send_sems, recv_sems, local_sems):
    n = len(x_refs)
    ix, iy, ic = _mesh_pos()
    me, sibling = (ix, iy, ic), (ix, iy, 1 - ic)
    chips = [(1 - ix, iy), (ix, 1 - iy), (1 - ix, 1 - iy)]

    def idx(px, py, pc):
        return 4 * px + 2 * py + pc

    def copy(a, k, block, to, src=None):
        dst = o_refs[a].at[idx(*block)]
        return pltpu.make_async_remote_copy(
            src_ref=dst if src is None else src, dst_ref=dst,
            send_sem=send_sems.at[a, k], recv_sem=recv_sems.at[a, k],
            device_id=to, device_id_type=pl.DeviceIdType.MESH)

    def local():
        return [pltpu.make_async_copy(x_refs[a], o_refs[a].at[idx(*me)], local_sems.at[a])
                for a in range(n)]

    def first():
        out = []
        for a in range(n):
            out.append(copy(a, 0, me, sibling, src=x_refs[a]))
            out += [copy(a, 1 + j, me, (*chip, ic), src=x_refs[a])
                    for j, chip in enumerate(chips)]
        return out

    def passed():
        return [copy(a, 4 + j, (*chip, ic), sibling)
                for j, chip in enumerate(chips) for a in range(n)]

    def start():
        for cp in local() + first():
            cp.start()

    def forward():
        for j, chip in enumerate(chips):
            for a in range(n):
                copy(a, 1 + j, (*chip, ic), me).wait_recv()
                copy(a, 4 + j, (*chip, ic), sibling).start()

    def finish():
        for a in range(n):
            copy(a, 0, sibling, me).wait_recv()
            for j, chip in enumerate(chips):
                copy(a, 4 + j, (*chip, 1 - ic), me).wait_recv()
        for cp in first() + passed():
            cp.wait_send()
        for cp in local():
            cp.wait()

    return start, forward, finish


def _swap_sibling(arrays, name):
    n = len(arrays)
    n_chips = N_DEV // 2

    def body(*refs):
        x_refs, got_refs = refs[:n], refs[n:2 * n]
        send_sems, recv_sems = refs[2 * n:]
        ix, iy, ic = _mesh_pos()
        sibling = (ix, iy, 1 - ic)
        sends = []
        for a in range(n):
            for q in range(n_chips):
                cp = pltpu.make_async_remote_copy(
                    src_ref=x_refs[a].at[q, 1 - ic], dst_ref=got_refs[a].at[q],
                    send_sem=send_sems.at[a, q], recv_sem=recv_sems.at[a, q],
                    device_id=sibling, device_id_type=pl.DeviceIdType.MESH)
                cp.start()
                sends.append(cp)
        for cp in sends:
            cp.wait()

    return pl.pallas_call(
        body, name=name,
        in_specs=[pl.BlockSpec(memory_space=pl.ANY)] * n,
        out_specs=[pl.BlockSpec(memory_space=pl.ANY)] * n,
        out_shape=[jax.ShapeDtypeStruct((n_chips,) + x.shape[2:], x.dtype) for x in arrays],
        scratch_shapes=[pltpu.SemaphoreType.DMA((n, n_chips)),
                        pltpu.SemaphoreType.DMA((n, n_chips))],
    )(*arrays)


def _exchange_chips(arrays, name):
    n = len(arrays)

    def body(*refs):
        start, finish = _chip_exchange_phases(refs[:n], refs[n:2 * n], *refs[2 * n:])
        start()
        finish()

    return pl.pallas_call(
        body, name=name,
        in_specs=[pl.BlockSpec(memory_space=pl.ANY)] * n,
        out_specs=[pl.BlockSpec(memory_space=pl.ANY)] * n,
        out_shape=[jax.ShapeDtypeStruct(x.shape, x.dtype) for x in arrays],
        scratch_shapes=_chip_exchange_sems(n),
    )(*arrays)


def _chip_exchange_sems(n):
    n_chips = N_DEV // 2
    return [pltpu.SemaphoreType.DMA((n, n_chips - 1)), pltpu.SemaphoreType.DMA((n, n_chips - 1)),
            pltpu.SemaphoreType.DMA((n,))]


def _chip_exchange_phases(x_refs, o_refs, send_sems, recv_sems, local_sems):
    n = len(x_refs)
    n_chips = N_DEV // 2
    ix, iy, ic = _mesh_pos()
    my_chip = 2 * ix + iy

    def peers():
        for mask in range(1, n_chips):
            px = 1 - ix if mask & 2 else ix
            py = 1 - iy if mask & 1 else iy
            yield mask, (px, py, ic), 2 * px + py

    def local():
        return [pltpu.make_async_copy(x_refs[a].at[my_chip], o_refs[a].at[my_chip],
                                      local_sems.at[a]) for a in range(n)]

    def sends():
        return [pltpu.make_async_remote_copy(
            src_ref=x_refs[a].at[chip], dst_ref=o_refs[a].at[my_chip],
            send_sem=send_sems.at[a, mask - 1], recv_sem=recv_sems.at[a, mask - 1],
            device_id=peer, device_id_type=pl.DeviceIdType.MESH)
            for mask, peer, chip in peers() for a in range(n)]

    def start():
        for cp in local() + sends():
            cp.start()

    def finish():
        for mask, peer, chip in peers():
            for a in range(n):
                pltpu.make_async_remote_copy(
                    src_ref=x_refs[a].at[my_chip], dst_ref=o_refs[a].at[chip],
                    send_sem=send_sems.at[a, mask - 1], recv_sem=recv_sems.at[a, mask - 1],
                    device_id=peer, device_id_type=pl.DeviceIdType.MESH).wait_recv()
        for cp in sends():
            cp.wait_send()
        for cp in local():
            cp.wait()

    return start, finish


def _pair_sum(core, x, got, name):
    nq, rows, cols = got.shape

    def body(c_ref, x_ref, g_ref, o_ref):
        o_ref[...] = (x_ref[...] + g_ref[...]).astype(BF16)

    blk = pl.BlockSpec((None, rows, cols), lambda q, c: (q, 0, 0))
    return pl.pallas_call(
        body, name=name,
        grid_spec=pltpu.PrefetchScalarGridSpec(
            num_scalar_prefetch=1, grid=(nq,),
            in_specs=[pl.BlockSpec((None, None, rows, cols), lambda q, c: (q, c[0], 0, 0)), blk],
            out_specs=blk),
        out_shape=jax.ShapeDtypeStruct(got.shape, BF16),
        compiler_params=_params(),
    )(core, x, got)


def _selectors():
    r = lax.broadcasted_iota(jnp.int32, (LANES, FOX_PAD), 0)
    c = lax.broadcasted_iota(jnp.int32, (LANES, FOX_PAD), 1)
    part, head_r = r // HEADS, r % HEADS
    head_c, lane_c = c // HEAD_PAD, c % HEAD_PAD
    same = (head_r == head_c) & (part < 3)
    sel_q = jnp.where(same & (lane_c == LANE_RB + part), 1.0, 0.0)
    sel_k = jnp.where(same & (lane_c == LANE_CK + part), -1.0, 0.0)
    sel = jnp.stack([sel_q, sel_k, jnp.zeros_like(sel_q)]).astype(BF16)
    lane = lax.broadcasted_iota(jnp.int32, (1, FOX_PAD), 1) % HEAD_PAD
    ones_q = jnp.where((lane >= LANE_CK) & (lane < LANE_CK + 3), 1.0, 0.0)
    ones_k = jnp.where(((lane >= LANE_RB) & (lane < LANE_RB + 3))
                       | ((lane >= LANE_LSE) & (lane < LANE_LSE + 3)), 1.0, 0.0)
    ones_v = jnp.where((lane >= LANE_ONE_V) & (lane < LANE_ONE_V + 2), 1.0, 0.0)
    bias = jnp.stack([ones_q, ones_k, ones_v]).astype(F32)
    return sel, bias


def _chip_sums(names, send):
    send = [a.reshape((N_DEV // 2, 2) + a.shape[1:]) for a in send]
    got = _swap_sibling(send, "swap_" + names[0])
    core = lax.axis_index("c").astype(jnp.int32).reshape(1)
    return [_pair_sum(core, a, b, "pair_sum_" + n) for n, a, b in zip(names, send, got)]


def _local_step(x, target, norm_g, final_g, w_in8, conv_w, conv_b, wa, ba, wx, bx, a_param,
                w_out_b, fox_in_shard, b_f, fox_out_shard, blk=512, ts=256):
    qk_scale = 1.0 / (HEAD_DIM ** 0.5)
    g0, g1 = norm_g[0:1], norm_g[1:2]
    gf = final_g.reshape(1, D_MODEL)
    wa_b, wx_b = wa.astype(BF16), wx.astype(BF16)
    sel, bias = _selectors()

    xb, gate1, h0, (fox_in8, fox_out8) = _lru_in_fwd(x, g0, w_in8, [fox_in_shard, fox_out_shard],
                                                     ts)
    fox_w_in = jnp.transpose(fox_in8, (1, 0, 2)).reshape(D_MODEL, FOX_IN_COLS)
    width = HEADS * HEAD_DIM
    w4 = jnp.stack([fox_w_in[:, 0:width] * qk_scale, fox_w_in[:, width:2 * width],
                    fox_w_in[:, 2 * width:3 * width], fox_w_in[:, 3 * width:4 * width]])
    wf_b = jnp.pad(fox_w_in[:, 4 * width:], ((0, 0), (0, LANES - HEADS)))
    bf_pad = jnp.pad(b_f, ((0, 0), (0, LANES - HEADS)))
    fo_b = fox_out8.reshape(width, D_MODEL)
    y1, hs = _lru_core_fwd(xb, gate1, conv_w, conv_b, wa_b, ba, wx_b, bx, a_param, ts)
    x1, h1, f, cparts = _fox_pre_fwd(x, y1, w_out_b, g1, wf_b, bf_pad, ts)
    qkv = _fox_proj_fwd(h1, cparts, w4[0:3], sel, bias, BF16, ts, "fox_proj_qkv")
    gate2 = _fox_proj_fwd(h1, None, w4[3:4], None, None, F32, ts, "fox_proj_gate")[0]
    o, qb = _attn_fwd(qkv, blk, hps=4)
    dx2, dx2b, y2, loss_acc, g_final = _fox_out_loss(o, gate2, fo_b, x1, target, gf, ts)

    do, dgate2 = _fox_out_bwd(dx2b, fo_b, o, gate2, ts)
    dq, dk, dv, dcum = _attn_bwd(qb, qkv, do, blk)
    dx1, dx1b, df, du4, g_norm1, g_bf = _fox_in_bwd(dq, dk, dv, dgate2, w4, wf_b, dcum, f, x1,
                                                    dx2, g1, ts)
    tw = 512
    g_q = _weight_grad(h1, du4[0], tw, "grad_fox_wq", scale=qk_scale)
    g_k = _weight_grad(h1, du4[1], tw, "grad_fox_wk")
    g_v = _weight_grad(h1, du4[2], tw, "grad_fox_wv")
    g_g = _weight_grad(h1, du4[3], tw, "grad_fox_wg")
    g_f = _weight_grad(h1, df, tw, "grad_fox_wf")
    g_fox_w_in = jnp.concatenate([g_q, g_k, g_v, g_g, g_f[:, :HEADS]], axis=1)
    g_fox_w_in = jnp.transpose(g_fox_w_in.reshape(D_MODEL, N_DEV, FOX_IN_SHARD), (1, 0, 2))
    g_fox_w_out = _weight_grad(y2, dx2b, tw, "grad_fox_w_out")
    fox_sums = _chip_sums(("fox_w_in", "fox_w_out"),
                          [g_fox_w_in, g_fox_w_out.reshape(N_DEV, -1, D_MODEL)])

    du, g_wa, g_wx, g_vec, (r_fox_in, r_fox_out) = _lru_core_bwd(
        dx1b, w_out_b, xb, gate1, hs, conv_w, conv_b, wa_b, ba, wx_b, bx, a_param,
        jnp.transpose(wa_b, (0, 2, 1)), jnp.transpose(wx_b, (0, 2, 1)), fox_sums, ts)
    g_lru_w_in = _weight_grad(h0, du, tw, "grad_lru_w_in", col_shards=N_DEV)
    g_lru_w_out = _weight_grad(y1, dx1b, tw, "grad_lru_w_out")
    conv_send = jnp.transpose(g_vec[0:CONV_WIDTH].reshape(CONV_WIDTH, N_DEV, -1), (1, 0, 2))
    lru_sums = _chip_sums(("lru_w_in", "lru_conv_w", "lru_w_out"),
                          [g_lru_w_in, conv_send, g_lru_w_out.reshape(N_DEV, -1, D_MODEL)])
    grad_x, g_norm0, (r_w_in, r_conv, r_w_out) = _lru_in_bwd(du, w_in8, x, dx1, g0, lru_sums, ts)

    small = dict(
        norm_g=jnp.concatenate([g_norm0, g_norm1], axis=0), final_g=g_final[0],
        lru_conv_b=g_vec[4:5], lru_wa=g_wa, lru_ba=g_vec[5:6], lru_wx=g_wx, lru_bx=g_vec[6:7],
        lru_a_param=g_vec[7:8], fox_b_f=g_bf[:, :HEADS])
    received = dict(lru_w_in=r_w_in, lru_conv_w=r_conv, lru_w_out=r_w_out, fox_w_in=r_fox_in,
                    fox_w_out=r_fox_out)
    return loss_acc[0, 0], grad_x, small, received


SMALL =("norm_g", "final_g", "lru_conv_b", "lru_wa", "lru_ba", "lru_wx", "lru_bx", "lru_a_param",
         "fox_b_f")
ALL_WEIGHTS = ("norm_g", "final_g", "lru_w_in", "lru_conv_w", "lru_conv_b", "lru_wa", "lru_ba",
               "lru_wx", "lru_bx", "lru_a_param", "lru_w_out", "fox_w_in", "fox_b_f", "fox_w_out")


def _pack_small(d):
    rows = []
    for n in SMALL:
        a = d[n].reshape(-1)
        if a.shape[0] % LANES:
            a = jnp.pad(a, (0, LANES - a.shape[0] % LANES))
        rows.append(a.reshape(-1, LANES))
    packed = jnp.concatenate(rows, axis=0)
    return jnp.pad(packed, ((0, N_DEV * SMALL_CHUNK_ROWS - packed.shape[0]), (0, 0)))


def _unpack_small(packed, like):
    out, off = {}, 0
    for n, nrows in zip(SMALL, SMALL_ROWS):
        size = like[n].size
        out[n] = packed[off:off + nrows].reshape(-1)[:size].reshape(like[n].shape)
        off += nrows
    return out


def kernel(x, norm_g, final_g, lru_w_in, lru_conv_w, lru_conv_b, lru_wa, lru_ba, lru_wx, lru_bx, lru_a_param, lru_w_out, fox_w_in, fox_b_f, fox_w_out, loss_target, m_norm_g, m_final_g, m_lru_w_in, m_lru_conv_w, m_lru_conv_b, m_lru_wa, m_lru_ba, m_lru_wx, m_lru_bx, m_lru_a_param, m_lru_w_out, m_fox_w_in, m_fox_b_f, m_fox_w_out, v_norm_g, v_final_g, v_lru_w_in, v_lru_conv_w, v_lru_conv_b, v_lru_wa, v_lru_ba, v_lru_wx, v_lru_bx, v_lru_a_param, v_lru_w_out, v_fox_w_in, v_fox_b_f, v_fox_w_out):
    w_loc = dict(norm_g=norm_g, final_g=final_g, lru_w_in=lru_w_in, lru_conv_w=lru_conv_w,
                 lru_conv_b=lru_conv_b, lru_wa=lru_wa, lru_ba=lru_ba, lru_wx=lru_wx, lru_bx=lru_bx,
                 lru_a_param=lru_a_param, lru_w_out=lru_w_out, fox_w_in=fox_w_in, fox_b_f=fox_b_f,
                 fox_w_out=fox_w_out)
    m_loc = dict(norm_g=m_norm_g, final_g=m_final_g, lru_w_in=m_lru_w_in, lru_conv_w=m_lru_conv_w,
                 lru_conv_b=m_lru_conv_b, lru_wa=m_lru_wa, lru_ba=m_lru_ba, lru_wx=m_lru_wx,
                 lru_bx=m_lru_bx, lru_a_param=m_lru_a_param, lru_w_out=m_lru_w_out,
                 fox_w_in=m_fox_w_in, fox_b_f=m_fox_b_f, fox_w_out=m_fox_w_out)
    v_loc = dict(norm_g=v_norm_g, final_g=v_final_g, lru_w_in=v_lru_w_in, lru_conv_w=v_lru_conv_w,
                 lru_conv_b=v_lru_conv_b, lru_wa=v_lru_wa, lru_ba=v_lru_ba, lru_wx=v_lru_wx,
                 lru_bx=v_lru_bx, lru_a_param=v_lru_a_param, lru_w_out=v_lru_w_out,
                 fox_w_in=v_fox_w_in, fox_b_f=v_fox_b_f, fox_w_out=v_fox_w_out)

    w_in8, conv8, w_out8 = _gather_two_level(
        [lru_w_in[0].astype(BF16), lru_conv_w[0], lru_w_out[0].astype(BF16)], "gather_weights")
    conv_full = jnp.transpose(conv8, (1, 0, 2)).reshape(CONV_WIDTH, LRU_WIDTH)

    loss, grad_x, small_grads, received = _local_step(
        x[0], loss_target[0], norm_g, final_g, w_in8, conv_full, lru_conv_b, lru_wa[0], lru_ba,
        lru_wx[0], lru_bx, lru_a_param, w_out8.reshape(LRU_WIDTH, D_MODEL),
        fox_w_in[0].astype(BF16), fox_b_f, fox_w_out[0].astype(BF16))

    out = {}
    for n, tr in (("lru_w_in", 256), ("lru_conv_w", CONV_WIDTH), ("lru_w_out", 96),
                  ("fox_w_in", 128), ("fox_w_out", 64)):
        res = _adamw(received[n], w_loc[n][0], m_loc[n][0], v_loc[n][0], tr, "adamw_" + n)
        out[n] = [a[None] for a in res]

    small_sums = _chip_sums(
        ("small",), [_pack_small(small_grads).reshape(N_DEV, SMALL_CHUNK_ROWS, LANES)])
    r_small, = _exchange_chips(small_sums, "scatter_small_grads")

    g_chunk = _reduce_parts(r_small, "reduce_small_grads")
    g_small, = _exchange([g_chunk], False, "gather_small_grads")
    g_small = g_small.reshape(1, N_DEV * SMALL_CHUNK_ROWS, LANES)
    res = _adamw(g_small, _pack_small(w_loc), _pack_small(m_loc), _pack_small(v_loc),
                 N_DEV * SMALL_CHUNK_ROWS, "adamw_replicated")
    small_out = [_unpack_small(a, w_loc) for a in res]
    for n in SMALL:
        out[n] = [d[n] for d in small_out]

    loss = lax.psum(loss, ("x", "y", "c"))
    return (loss, grad_x[None], *[out[n][0] for n in ALL_WEIGHTS], *[out[n][1] for n in ALL_WEIGHTS],
            *[out[n][2] for n in ALL_WEIGHTS], *[out[n][3] for n in ALL_WEIGHTS])
```

```python
import functools

import jax
import jax.numpy as jnp
from jax import lax
from jax.experimental import pallas as pl
from jax.experimental.pallas import tpu as pltpu

F32 = jnp.float32
BF16 = jnp.bfloat16

D_MODEL = 1024
LRU_WIDTH = 1536
LRU_BLOCKS = 12
LRU_BLOCK_W = 128
CONV_WIDTH = 4
LRU_C = 8.0
HEADS = 16
HEAD_DIM = 64
HEAD_PAD = 128
FOX_PAD = HEADS * HEAD_PAD
HEADS_PER_STEP = 2
QK_SCALE = 1.0 / HEAD_DIM ** 0.5
EPS = 1e-6
NEG_BIG = -1e30
N_DEV = 8

ADAM_LR = 0.001
ADAM_B1 = 0.9
ADAM_B2 = 0.999
ADAM_EPS = 1e-08
ADAM_WD = 0.01
ADAM_STEP = 10

LANE_RB = 64
LANE_CK = 67
LANE_LSE = 70
LANE_ONE_V = 64

VMEM_LIMIT_BYTES = 56 * 1024 * 1024
LANES = 128
SUBLANES = 8

LRU_IN_SHARD = 2 * LRU_WIDTH // N_DEV
FOX_IN_COLS = 4 * HEADS * HEAD_DIM + HEADS
FOX_IN_SHARD = FOX_IN_COLS // N_DEV

SMALL_ROWS = (16, 8, 12, 1536, 12, 1536, 12, 12, 1)
SMALL_CHUNK_ROWS = 400
assert sum(SMALL_ROWS) <= N_DEV * SMALL_CHUNK_ROWS


def _params(n_grid_axes=1):
    return pltpu.CompilerParams(
        dimension_semantics=("arbitrary",) * n_grid_axes,
        vmem_limit_bytes=VMEM_LIMIT_BYTES)


def _const_spec(shape):
    nd = len(shape)
    return pl.BlockSpec(shape, lambda *_: (0,) * nd, pipeline_mode=pl.Buffered(1))


def _shift_down(x, k, fill):
    rows = lax.broadcasted_iota(jnp.int32, x.shape, 0)
    return jnp.where(rows >= k, pltpu.roll(x, k, 0), fill)


def _shift_up(x, k, fill):
    n = x.shape[0]
    rows = lax.broadcasted_iota(jnp.int32, x.shape, 0)
    return jnp.where(rows < n - k, pltpu.roll(x, n - k, 0), fill)


def _scan_rows(a, b, reverse=False):
    n = a.shape[0]
    shift = _shift_up if reverse else _shift_down
    k = 1
    while k < n:
        b = a * shift(b, k, 0.0) + b
        a = a * shift(a, k, 1.0)
        k *= 2
    return a, b


def _cumsum_rows(x, reverse=False):
    n = x.shape[0]
    shift = _shift_up if reverse else _shift_down
    k = 1
    while k < n:
        x = x + shift(x, k, 0.0)
        k *= 2
    return x


def _rstd(x):
    return lax.rsqrt(jnp.mean(x * x, axis=-1, keepdims=True) + EPS)


def _norm_bwd(x, g, dh):
    rstd = _rstd(x)
    xhat = x * rstd
    dg = jnp.sum(dh * xhat, axis=0, keepdims=True)
    dxh = dh * g
    dx = rstd * (dxh - xhat * jnp.mean(dxh * xhat, axis=-1, keepdims=True))
    return dx, dg


def _split3(x):
    hi = x.astype(BF16)
    r1 = x - hi.astype(F32)
    mid = r1.astype(BF16)
    lo = (r1 - mid.astype(F32)).astype(BF16)
    return hi, mid, lo


def _sigmoid(x):
    return jax.nn.sigmoid(x)


def _dot(a, b):
    return jnp.dot(a, b, preferred_element_type=F32)


def _dot_nt(a, b):
    return lax.dot_general(a, b, (((1,), (1,)), ((), ())), preferred_element_type=F32)


def _dot_tn(a, b):
    return lax.dot_general(a, b, (((0,), (0,)), ((), ())), preferred_element_type=F32)


def _heads_to_padded(u):
    n = u.shape[0]
    low = lax.broadcasted_iota(jnp.int32, (n, LANES), 1) < HEAD_DIM
    zero = jnp.zeros((n, LANES), u.dtype)
    cols = []
    for p in range(HEADS // 2):
        pair = u[:, p * LANES:(p + 1) * LANES]
        cols.append(jnp.where(low, pair, zero))
        cols.append(jnp.where(low, pltpu.roll(pair, HEAD_DIM, 1), zero))
    return jnp.concatenate(cols, axis=1)


def _heads_from_padded(x):
    n = x.shape[0]
    low = lax.broadcasted_iota(jnp.int32, (n, LANES), 1) < HEAD_DIM
    cols = []
    for p in range(HEADS // 2):
        even = x[:, (2 * p) * HEAD_PAD:(2 * p + 1) * HEAD_PAD]
        odd = x[:, (2 * p + 1) * HEAD_PAD:(2 * p + 2) * HEAD_PAD]
        cols.append(jnp.where(low, even, pltpu.roll(odd, HEAD_DIM, 1)))
    return jnp.concatenate(cols, axis=1)


def _conv_taps(xb, prev8):
    rows8 = lax.broadcasted_iota(jnp.int32, prev8.shape, 0)
    taps = [xb]
    for j in range(1, CONV_WIDTH):
        r = pltpu.roll(xb, j, 0)
        p = pltpu.roll(prev8, j, 0)
        head = jnp.where(rows8 < j, p, r[0:SUBLANES])
        taps.append(jnp.concatenate([head, r[SUBLANES:]], axis=0))
    return taps


def _lru_pre(taps, cw, cb, wa_ref, ba, wx_ref, bx, a_param):
    xc = cb + cw[3:4] * taps[0] + cw[2:3] * taps[1] + cw[1:2] * taps[2] + cw[0:1] * taps[3]
    xcb = xc.astype(BF16)
    ra, ia = [], []
    for n in range(LRU_BLOCKS):
        blk = xcb[:, n * LRU_BLOCK_W:(n + 1) * LRU_BLOCK_W]
        ra.append(_dot(blk, wa_ref[n]))
        ia.append(_dot(blk, wx_ref[n]))
    r = _sigmoid(jnp.concatenate(ra, axis=1) + ba)
    i = _sigmoid(jnp.concatenate(ia, axis=1) + bx)
    z = -a_param
    sp = jnp.maximum(z, 0.0) + jnp.log1p(jnp.exp(-jnp.abs(z)))
    log_a = (-LRU_C) * r * sp
    a = jnp.exp(log_a)
    one_minus_a2 = -jnp.tanh(log_a) * (a * a + 1.0)
    mult = jnp.sqrt(one_minus_a2)
    return xc, xcb, r, i, sp, a, mult


def _lru_in_fwd(x, g0, w_in, later_shards, ts):
    s = x.shape[0]
    nt = s // ts
    n = len(later_shards)

    def body(*refs):
        x_ref, g_ref, w_ref = refs[:3]
        shard_refs = refs[3:3 + n]
        xb_ref, gate_ref, h_ref = refs[3 + n:6 + n]
        wfull_ref = refs[6 + 2 * n]
        start, forward, finish = _gather_phases(shard_refs, refs[6 + n:6 + 2 * n],
                                                *refs[7 + 2 * n:])
        step = pl.program_id(0)
        pl.when(step == 0)(start)

        @pl.when(step == 0)
        def _():
            for j in range(N_DEV):
                wfull_ref[:, j * LRU_IN_SHARD:(j + 1) * LRU_IN_SHARD] = w_ref[j]

        xv = x_ref[...]
        h = (xv * _rstd(xv) * g_ref[...]).astype(BF16)
        u = _dot(h, wfull_ref[...])
        xb_ref[...] = u[:, :LRU_WIDTH]
        gate_ref[...] = u[:, LRU_WIDTH:]
        h_ref[...] = h
        pl.when(step == (2 * nt) // 3)(forward)
        pl.when(step == nt - 1)(finish)

    hbm = pl.BlockSpec(memory_space=pl.ANY)
    res = pl.pallas_call(
        body, name="lru_in_fwd", grid=(nt,),
        in_specs=[pl.BlockSpec((ts, D_MODEL), lambda i: (i, 0)),
                  _const_spec((1, D_MODEL)),
                  _const_spec((N_DEV, D_MODEL, LRU_IN_SHARD))] + [hbm] * n,
        out_specs=[pl.BlockSpec((ts, LRU_WIDTH), lambda i: (i, 0)),
                   pl.BlockSpec((ts, LRU_WIDTH), lambda i: (i, 0)),
                   pl.BlockSpec((ts, D_MODEL), lambda i: (i, 0))] + [hbm] * n,
        out_shape=[jax.ShapeDtypeStruct((s, LRU_WIDTH), F32),
                   jax.ShapeDtypeStruct((s, LRU_WIDTH), F32),
                   jax.ShapeDtypeStruct((s, D_MODEL), BF16)]
        + [jax.ShapeDtypeStruct((N_DEV,) + a.shape, a.dtype) for a in later_shards],
        scratch_shapes=[pltpu.VMEM((D_MODEL, 2 * LRU_WIDTH), BF16)] + _gather_sems(n),
        compiler_params=_params(),
    )(x, g0, w_in, *later_shards)
    return res[0], res[1], res[2], res[3:]


def _lru_core_fwd(xb, gate, cw, cb, wa, ba, wx, bx, a_param, ts):
    s = xb.shape[0]

    def body(xb_ref, gate_ref, cw_ref, cb_ref, wa_ref, ba_ref, wx_ref, bx_ref, ap_ref,
             y_ref, hs_ref, prev_ref, hcar_ref):
        @pl.when(pl.program_id(0) == 0)
        def _():
            prev_ref[...] = jnp.zeros_like(prev_ref)
            hcar_ref[...] = jnp.zeros_like(hcar_ref)

        xbv = xb_ref[...]
        taps = _conv_taps(xbv, prev_ref[...])
        xc, _, _, i, _, a, mult = _lru_pre(taps, cw_ref[...], cb_ref[...], wa_ref, ba_ref[...],
                                           wx_ref, bx_ref[...], ap_ref[...])
        bterm = mult * (i * xc)
        cum_a, hloc = _scan_rows(a, bterm)
        hs = cum_a * hcar_ref[SUBLANES - 1:SUBLANES, :] + hloc
        gv = gate_ref[...]
        y_ref[...] = (hs * (gv * _sigmoid(gv))).astype(BF16)
        hs_ref[...] = hs
        prev_ref[...] = xbv[ts - SUBLANES:, :]
        hcar_ref[...] = hs[ts - SUBLANES:, :]

    vec = _const_spec((1, LRU_WIDTH))
    blk = _const_spec((LRU_BLOCKS, LRU_BLOCK_W, LRU_BLOCK_W))
    tile = pl.BlockSpec((ts, LRU_WIDTH), lambda i: (i, 0))
    return pl.pallas_call(
        body, name="lru_core_fwd", grid=(s // ts,),
        in_specs=[tile, tile, _const_spec((CONV_WIDTH, LRU_WIDTH)), vec, blk, vec, blk, vec, vec],
        out_specs=[tile, tile],
        out_shape=[jax.ShapeDtypeStruct((s, LRU_WIDTH), BF16),
                   jax.ShapeDtypeStruct((s, LRU_WIDTH), F32)],
        scratch_shapes=[pltpu.VMEM((SUBLANES, LRU_WIDTH), F32),
                        pltpu.VMEM((SUBLANES, LRU_WIDTH), F32)],
        compiler_params=_params(),
    )(xb, gate, cw, cb, wa, ba, wx, bx, a_param)


def _fox_pre_fwd(x, y, w_out, g1, wf, bf, ts):
    s = x.shape[0]

    def body(x_ref, y_ref, w_ref, g_ref, wf_ref, bf_ref, x1_ref, h1_ref, f_ref, cp_ref, ccar_ref):
        @pl.when(pl.program_id(0) == 0)
        def _():
            ccar_ref[...] = jnp.zeros_like(ccar_ref)

        x1 = x_ref[...] + _dot(y_ref[...], w_ref[...])
        h1 = (x1 * _rstd(x1) * g_ref[...]).astype(BF16)
        f = _dot(h1, wf_ref[...]) + bf_ref[...]
        logsig = jnp.minimum(f, 0.0) - jnp.log1p(jnp.exp(-jnp.abs(f)))
        cum = _cumsum_rows(logsig) + ccar_ref[SUBLANES - 1:SUBLANES, :]
        hi, mid, lo = _split3(cum)
        lane = lax.broadcasted_iota(jnp.int32, cum.shape, 1)
        packed = jnp.where(lane < HEADS, hi.astype(F32), jnp.where(
            lane < 2 * HEADS, pltpu.roll(mid.astype(F32), HEADS, 1), jnp.where(
                lane < 3 * HEADS, pltpu.roll(lo.astype(F32), 2 * HEADS, 1), 0.0)))
        x1_ref[...] = x1
        h1_ref[...] = h1
        f_ref[...] = f
        cp_ref[...] = packed.astype(BF16)
        ccar_ref[...] = cum[ts - SUBLANES:, :]

    return pl.pallas_call(
        body, name="fox_pre_fwd", grid=(s // ts,),
        in_specs=[pl.BlockSpec((ts, D_MODEL), lambda i: (i, 0)),
                  pl.BlockSpec((ts, LRU_WIDTH), lambda i: (i, 0)),
                  _const_spec((LRU_WIDTH, D_MODEL)),
                  _const_spec((1, D_MODEL)),
                  _const_spec((D_MODEL, LANES)),
                  _const_spec((1, LANES))],
        out_specs=[pl.BlockSpec((ts, D_MODEL), lambda i: (i, 0)),
                   pl.BlockSpec((ts, D_MODEL), lambda i: (i, 0)),
                   pl.BlockSpec((ts, LANES), lambda i: (i, 0)),
                   pl.BlockSpec((ts, LANES), lambda i: (i, 0))],
        out_shape=[jax.ShapeDtypeStruct((s, D_MODEL), F32),
                   jax.ShapeDtypeStruct((s, D_MODEL), BF16),
                   jax.ShapeDtypeStruct((s, LANES), F32),
                   jax.ShapeDtypeStruct((s, LANES), BF16)],
        scratch_shapes=[pltpu.VMEM((SUBLANES, LANES), F32)],
        compiler_params=_params(),
    )(x, y, w_out, g1, wf, bf)


def _fox_proj_fwd(h1, cparts, w, first, ng, sel, bias, out_dtype, ts, name):
    s = h1.shape[0]
    width = HEADS * HEAD_DIM
    use_sel = sel is not None

    def body(*refs):
        if use_sel:
            h_ref, cp_ref, w_ref, sel_ref, b_ref, o_ref = refs
            proj = _dot(h_ref[...], w_ref[...])
            if first == 0:
                proj = proj * jnp.where(pl.program_id(0) == 0, QK_SCALE, 1.0)
            acc = _heads_to_padded(proj) + _dot(cp_ref[...], sel_ref[...]) + b_ref[...]
        else:
            h_ref, w_ref, o_ref = refs
            acc = _heads_to_padded(_dot(h_ref[...], w_ref[...]))
        o_ref[...] = acc.astype(out_dtype)

    in_specs = [pl.BlockSpec((ts, D_MODEL), lambda j, i: (i, 0))]
    args = [h1]
    if use_sel:
        in_specs.append(pl.BlockSpec((ts, LANES), lambda j, i: (i, 0)))
        args.append(cparts)
    in_specs.append(pl.BlockSpec((D_MODEL, width), lambda j, i: (0, first + j)))
    args.append(w)
    if use_sel:
        in_specs.append(pl.BlockSpec((None, LANES, FOX_PAD), lambda j, i: (j, 0, 0)))
        in_specs.append(pl.BlockSpec((None, 1, FOX_PAD), lambda j, i: (j, 0, 0)))
        args += [sel, bias]
    return pl.pallas_call(
        body, name=name, grid=(ng, s // ts),
        in_specs=in_specs,
        out_specs=pl.BlockSpec((None, ts, FOX_PAD), lambda j, i: (j, i, 0)),
        out_shape=jax.ShapeDtypeStruct((ng, s, FOX_PAD), out_dtype),
        compiler_params=_params(2),
    )(*args)


def _attn_fwd(qkv, blk, hps=HEADS_PER_STEP):
    s = qkv.shape[1]
    nblk = s // blk
    wide = 2 * blk
    heads = [slice(i * HEAD_PAD, (i + 1) * HEAD_PAD) for i in range(hps)]

    def body(q_ref, k_ref, v_ref, o_ref, qb_ref, acc_ref, m_ref):
        qi = pl.program_id(1)
        row = lax.broadcasted_iota(jnp.int32, (blk, blk), 0)
        col = lax.broadcasted_iota(jnp.int32, (blk, blk), 1)
        lane = lax.broadcasted_iota(jnp.int32, (blk, HEAD_PAD), 1)
        qs = [q_ref[:, hd] for hd in heads]
        for i in range(hps):
            acc_ref[i] = jnp.zeros((blk, HEAD_PAD), F32)
            m_ref[i] = jnp.full((blk, HEAD_PAD), NEG_BIG, F32)

        def step(k0, size, masked):
            scores = [_dot_nt(q, k_ref[pl.ds(k0, size), hd]) for q, hd in zip(qs, heads)]
            for i, (sc, hd) in enumerate(zip(scores, heads)):
                v = v_ref[pl.ds(k0, size), hd]
                if masked:
                    sc = jnp.where(col <= row, sc, NEG_BIG)
                m = m_ref[i]
                m_new = jnp.maximum(m, jnp.max(sc, axis=-1, keepdims=True))
                p = jnp.exp((sc - jnp.tile(m_new, (1, size // HEAD_PAD))).astype(BF16))
                acc_ref[i] = jnp.exp(m - m_new) * acc_ref[i] + _dot(p, v)
                m_ref[i] = m_new

        def wide_step(kk, _):
            step(pl.multiple_of(kk * wide, wide), wide, False)
            return 0

        lax.fori_loop(0, qi // 2, wide_step, 0)

        @pl.when(qi % 2 == 1)
        def _():
            step(pl.multiple_of((qi - 1) * blk, blk), blk, False)

        step(pl.multiple_of(qi * blk, blk), blk, True)
        for i, (q, hd) in enumerate(zip(qs, heads)):
            acc = acc_ref[i]
            l = jnp.broadcast_to(acc[:, LANE_ONE_V:LANE_ONE_V + 1], (blk, HEAD_PAD))
            o_ref[:, hd] = (acc / l).astype(BF16)
            hi, mid, lo = _split3(-(m_ref[i] + jnp.log(l)))
            qb_ref[:, hd] = jnp.where(lane == LANE_LSE, hi, jnp.where(
                lane == LANE_LSE + 1, mid, jnp.where(lane == LANE_LSE + 2, lo, q)))

    width = hps * HEAD_PAD

    def whole(j):
        return pl.BlockSpec((None, s, width), lambda h, i: (j, 0, h))

    out_spec = pl.BlockSpec((blk, width), lambda h, i: (i, h))
    return pl.pallas_call(
        body, name="attn_fwd", grid=(HEADS // hps, nblk),
        in_specs=[pl.BlockSpec((None, blk, width), lambda h, i: (0, i, h)), whole(1), whole(2)],
        out_specs=[out_spec, out_spec],
        out_shape=[jax.ShapeDtypeStruct((s, FOX_PAD), BF16),
                   jax.ShapeDtypeStruct((s, FOX_PAD), BF16)],
        scratch_shapes=[pltpu.VMEM((hps, blk, HEAD_PAD), F32),
                        pltpu.VMEM((hps, blk, HEAD_PAD), F32)],
        compiler_params=_params(2),
    )(qkv, qkv, qkv)


def _fox_out_loss(o, gate, w_out, x1, target, gf, ts):
    s = x1.shape[0]

    def body(o_ref, gt_ref, w_ref, x1_ref, t_ref, g_ref, dx2_ref, dx2b_ref, y2_ref, loss_ref,
             gfin_ref):
        @pl.when(pl.program_id(0) == 0)
        def _():
            loss_ref[...] = jnp.zeros_like(loss_ref)
            gfin_ref[...] = jnp.zeros_like(gfin_ref)

        gv = gt_ref[...]
        y2 = _heads_from_padded(o_ref[...] * (gv * _sigmoid(gv))).astype(BF16)
        x2 = x1_ref[...] + _dot(y2, w_ref[...])
        rstd = _rstd(x2)
        xhat = x2 * rstd
        g = g_ref[...]
        diff = xhat * g - t_ref[...]
        loss_ref[...] += 0.5 * jnp.sum(jnp.mean(diff * diff, axis=-1, keepdims=True))
        dy = diff * (1.0 / D_MODEL)
        gfin_ref[...] += jnp.sum(dy * xhat, axis=0, keepdims=True)
        dxh = dy * g
        dx2 = rstd * (dxh - xhat * jnp.mean(dxh * xhat, axis=-1, keepdims=True))
        dx2_ref[...] = dx2
        dx2b_ref[...] = dx2.astype(BF16)
        y2_ref[...] = y2

    return pl.pallas_call(
        body, name="fox_out_loss", grid=(s // ts,),
        in_specs=[pl.BlockSpec((ts, FOX_PAD), lambda i: (i, 0)),
                  pl.BlockSpec((ts, FOX_PAD), lambda i: (i, 0)),
                  _const_spec((HEADS * HEAD_DIM, D_MODEL)),
                  pl.BlockSpec((ts, D_MODEL), lambda i: (i, 0)),
                  pl.BlockSpec((ts, D_MODEL), lambda i: (i, 0)),
                  _const_spec((1, D_MODEL))],
        out_specs=[pl.BlockSpec((ts, D_MODEL), lambda i: (i, 0)),
                   pl.BlockSpec((ts, D_MODEL), lambda i: (i, 0)),
                   pl.BlockSpec((ts, HEADS * HEAD_DIM), lambda i: (i, 0)),
                   pl.BlockSpec((SUBLANES, LANES), lambda i: (0, 0)),
                   pl.BlockSpec((1, D_MODEL), lambda i: (0, 0))],
        out_shape=[jax.ShapeDtypeStruct((s, D_MODEL), F32),
                   jax.ShapeDtypeStruct((s, D_MODEL), BF16),
                   jax.ShapeDtypeStruct((s, HEADS * HEAD_DIM), BF16),
                   jax.ShapeDtypeStruct((SUBLANES, LANES), F32),
                   jax.ShapeDtypeStruct((1, D_MODEL), F32)],
        compiler_params=_params(),
    )(o, gate, w_out, x1, target, gf)


def _fox_out_bwd(dx2, w_out, o, gate, ts):
    s = dx2.shape[0]

    def body(dx_ref, w_ref, o_ref, gt_ref, do_ref, dg_ref):
        lane = lax.broadcasted_iota(jnp.int32, (ts, HEAD_PAD), 1)
        dy2 = _heads_to_padded(_dot_nt(dx_ref[...], w_ref[...]))
        gv = gt_ref[...]
        sg = _sigmoid(gv)
        ov = o_ref[...]
        dov = dy2 * (gv * sg)
        dg_ref[...] = (dy2 * ov * (sg * (1.0 + gv * (1.0 - sg)))).astype(BF16)
        prod = dov * ov
        for h in range(HEADS):
            sl = slice(h * HEAD_PAD, (h + 1) * HEAD_PAD)
            delta = jnp.sum(prod[:, sl], axis=-1, keepdims=True)
            hi = delta.astype(BF16)
            lo = (delta - hi.astype(F32)).astype(BF16)
            do_h = dov[:, sl].astype(BF16)
            do_ref[:, sl] = jnp.where(lane == LANE_ONE_V, -hi,
                                      jnp.where(lane == LANE_ONE_V + 1, -lo, do_h))

    tile = pl.BlockSpec((ts, FOX_PAD), lambda i: (i, 0))
    return pl.pallas_call(
        body, name="fox_out_bwd", grid=(s // ts,),
        in_specs=[pl.BlockSpec((ts, D_MODEL), lambda i: (i, 0)),
                  _const_spec((HEADS * HEAD_DIM, D_MODEL)), tile, tile],
        out_specs=[tile, tile],
        out_shape=[jax.ShapeDtypeStruct((s, FOX_PAD), BF16),
                   jax.ShapeDtypeStruct((s, FOX_PAD), BF16)],
        compiler_params=_params(),
    )(dx2, w_out, o, gate)


def _attn_bwd(qb, qkv, do, blk):
    s = qb.shape[0]
    nblk = s // blk
    half = blk // 2
    heads = [slice(i * HEAD_PAD, (i + 1) * HEAD_PAD) for i in range(HEADS_PER_STEP)]

    def body(q_ref, k_ref, v_ref, do_ref, dq_ref, dk_ref, dv_ref, dcum_ref, dq_acc, dkt_acc,
             dvt_acc, qt_ref, dot_ref):
        group = pl.program_id(0)
        kj = pl.program_id(1)
        row = lax.broadcasted_iota(jnp.int32, (blk, blk), 0)
        col = lax.broadcasted_iota(jnp.int32, (blk, blk), 1)
        lane = lax.broadcasted_iota(jnp.int32, (blk, LANES), 1)
        mine = [lane == group * HEADS_PER_STEP + i for i in range(HEADS_PER_STEP)]

        @pl.when(kj == 0)
        def _():
            dq_acc[...] = jnp.zeros_like(dq_acc)

            def transpose_block(bi, _):
                r0 = pl.multiple_of(bi * blk, blk)
                for i, hd in enumerate(heads):
                    qt_ref[i, bi] = q_ref[pl.ds(r0, blk), hd].T
                    dot_ref[i, bi] = do_ref[pl.ds(r0, blk), hd].T
                return 0

            lax.fori_loop(0, nblk, transpose_block, 0)

        @pl.when((group == 0) & (kj == 0))
        def _():
            dcum_ref[...] = jnp.zeros_like(dcum_ref)

        k0 = pl.multiple_of(kj * blk, blk)
        ks = [k_ref[:, hd] for hd in heads]
        vs = [v_ref[:, hd] for hd in heads]

        def step(qi, q_lo, nq, k_lo, nk, masked):
            q0 = pl.multiple_of(qi * blk + q_lo, half)
            qs = [q_ref[pl.ds(q0, nq), hd] for hd in heads]
            dos = [do_ref[pl.ds(q0, nq), hd] for hd in heads]
            kk = [k[k_lo:k_lo + nk] for k in ks]
            vv = [v[k_lo:k_lo + nk] for v in vs]
            scores = [_dot_nt(q, k) for q, k in zip(qs, kk)]
            dps = [_dot_nt(dov, v) for dov, v in zip(dos, vv)]
            for i, (hd, k, sc, dp) in enumerate(zip(heads, kk, scores, dps)):
                p = jnp.exp(sc.astype(BF16))
                if masked:
                    p = jnp.where(col[:nq, :nk] + k_lo <= row[:nq, :nk] + q_lo, p,
                                  jnp.zeros_like(p))
                ds = (p.astype(F32) * dp).astype(BF16)
                dvt = _dot(dot_ref[i, qi, :, q_lo:q_lo + nq], p)
                dkt = _dot(qt_ref[i, qi, :, q_lo:q_lo + nq], ds)
                if masked:
                    dvt_acc[i, :, k_lo:k_lo + nk] = dvt
                    dkt_acc[i, :, k_lo:k_lo + nk] = dkt
                else:
                    dvt_acc[i, :, k_lo:k_lo + nk] += dvt
                    dkt_acc[i, :, k_lo:k_lo + nk] += dkt
                dq_acc[pl.ds(q0, nq), hd] += _dot(ds, k)

        step(kj, 0, blk, 0, half, True)
        step(kj, half, half, half, half, True)

        def q_step(qi, _):
            step(qi, 0, blk, 0, blk, False)
            return 0

        lax.fori_loop(kj + 1, nblk, q_step, 0)
        dcum = dcum_ref[pl.ds(k0, blk), :]
        for i, (hd, mask) in enumerate(zip(heads, mine)):
            dk = dkt_acc[i].T
            dk_ref[:, hd] = dk.astype(BF16)
            dv_ref[:, hd] = dvt_acc[i].astype(BF16).T
            dcum = jnp.where(mask, -dk[:, LANE_CK:LANE_CK + 1], dcum)
        dcum_ref[pl.ds(k0, blk), :] = dcum

        @pl.when(kj == nblk - 1)
        def _():
            def finish(bi, _):
                r0 = pl.multiple_of(bi * blk, blk)
                dcum = dcum_ref[pl.ds(r0, blk), :]
                for hd, mask in zip(heads, mine):
                    dq = dq_acc[pl.ds(r0, blk), hd]
                    dq_ref[pl.ds(r0, blk), hd] = dq.astype(BF16)
                    dcum = dcum + jnp.where(mask, dq[:, LANE_RB:LANE_RB + 1], 0.0)
                dcum_ref[pl.ds(r0, blk), :] = dcum
                return 0

            lax.fori_loop(0, nblk, finish, 0)

    width = HEADS_PER_STEP * HEAD_PAD
    whole = pl.BlockSpec((s, width), lambda h, j: (0, h))
    whole_in = pl.BlockSpec((s, width), lambda h, j: (0, h), pipeline_mode=pl.Buffered(1))
    part = pl.BlockSpec((blk, width), lambda h, j: (j, h))
    out = jax.ShapeDtypeStruct((s, FOX_PAD), BF16)
    return pl.pallas_call(
        body, name="attn_bwd", grid=(HEADS // HEADS_PER_STEP, nblk),
        in_specs=[whole_in,
                  pl.BlockSpec((None, blk, width), lambda h, j: (1, j, h)),
                  pl.BlockSpec((None, blk, width), lambda h, j: (2, j, h)),
                  whole_in],
        out_specs=[whole, part, part, pl.BlockSpec((s, LANES), lambda h, j: (0, 0))],
        out_shape=[out, out, out, jax.ShapeDtypeStruct((s, LANES), F32)],
        scratch_shapes=[pltpu.VMEM((s, width), F32),
                        pltpu.VMEM((HEADS_PER_STEP, HEAD_PAD, blk), F32),
                        pltpu.VMEM((HEADS_PER_STEP, HEAD_PAD, blk), F32),
                        pltpu.VMEM((HEADS_PER_STEP, nblk, HEAD_PAD, blk), BF16),
                        pltpu.VMEM((HEADS_PER_STEP, nblk, HEAD_PAD, blk), BF16)],
        compiler_params=_params(2),
    )(qb, qkv, qkv, do)


def _fox_in_bwd(dq, dk, dv, dg, wt, wft, dcum, f, x1, dx2, g1, ts):
    s = x1.shape[0]
    nt = s // ts
    width = HEADS * HEAD_DIM

    def body(dq_ref, dk_ref, dv_ref, dg_ref, wt_ref, wft_ref, dcum_ref, f_ref, x1_ref, dx2_ref,
             g_ref, dx1_ref, dx1b_ref, df_ref, du_ref, gn_ref, gbf_ref, rcar_ref):
        @pl.when(pl.program_id(0) == 0)
        def _():
            rcar_ref[...] = jnp.zeros_like(rcar_ref)
            gn_ref[...] = jnp.zeros_like(gn_ref)
            gbf_ref[...] = jnp.zeros_like(gbf_ref)

        rsum = _cumsum_rows(dcum_ref[...], reverse=True) + rcar_ref[0:1, :]
        df = rsum * _sigmoid(-f_ref[...])
        dfb = df.astype(BF16)
        dh = _dot_nt(dfb, wft_ref[...])
        for j, ref in enumerate((dq_ref, dk_ref, dv_ref, dg_ref)):
            du = _heads_from_padded(ref[...])
            du_ref[j] = du
            if j == 0:
                du = du * QK_SCALE
            dh = dh + _dot_nt(du, wt_ref[:, j * width:(j + 1) * width])
        dxn, dgn = _norm_bwd(x1_ref[...], g_ref[...], dh)
        dx1 = dx2_ref[...] + dxn
        dx1_ref[...] = dx1
        dx1b_ref[...] = dx1.astype(BF16)
        df_ref[...] = dfb
        gn_ref[...] += dgn
        gbf_ref[...] += jnp.sum(df, axis=0, keepdims=True)
        rcar_ref[...] = rsum[0:SUBLANES, :]

    rev = lambda i: (nt - 1 - i, 0)
    wide = pl.BlockSpec((ts, FOX_PAD), rev)
    return pl.pallas_call(
        body, name="fox_in_bwd", grid=(nt,),
        in_specs=[wide, wide, wide, wide,
                  _const_spec((D_MODEL, FOX_IN_COLS)),
                  _const_spec((D_MODEL, LANES)),
                  pl.BlockSpec((ts, LANES), rev),
                  pl.BlockSpec((ts, LANES), rev),
                  pl.BlockSpec((ts, D_MODEL), rev),
                  pl.BlockSpec((ts, D_MODEL), rev),
                  _const_spec((1, D_MODEL))],
        out_specs=[pl.BlockSpec((ts, D_MODEL), rev),
                   pl.BlockSpec((ts, D_MODEL), rev),
                   pl.BlockSpec((ts, LANES), rev),
                   pl.BlockSpec((4, ts, width), lambda i: (0, nt - 1 - i, 0)),
                   pl.BlockSpec((1, D_MODEL), lambda i: (0, 0)),
                   pl.BlockSpec((1, LANES), lambda i: (0, 0))],
        out_shape=[jax.ShapeDtypeStruct((s, D_MODEL), F32),
                   jax.ShapeDtypeStruct((s, D_MODEL), BF16),
                   jax.ShapeDtypeStruct((s, LANES), BF16),
                   jax.ShapeDtypeStruct((4, s, width), BF16),
                   jax.ShapeDtypeStruct((1, D_MODEL), F32),
                   jax.ShapeDtypeStruct((1, LANES), F32)],
        scratch_shapes=[pltpu.VMEM((SUBLANES, LANES), F32)],
        compiler_params=_params(),
    )(dq, dk, dv, dg, wt, wft, dcum, f, x1, dx2, g1)


def _lru_core_bwd(dx1b, w_out, xb, gate, hs, cw, cb, wa, ba, wx, bx, a_param, wa_t, wx_t,
                  chip_sums, ts):
    s = xb.shape[0]
    nt = s // ts
    tpb = ts // SUBLANES
    n_ex = len(chip_sums)

    def body(*refs):
        (dx_ref, wo_ref, xb_ref, xbh_ref, gate_ref, hs_ref, hsh_ref, cw_ref, cb_ref, wa_ref,
         ba_ref, wx_ref, bx_ref, ap_ref, wat_ref, wxt_ref) = refs[:16]
        sum_refs = refs[16:16 + n_ex]
        du_ref, gwa_ref, gwx_ref, gvec_ref = refs[16 + n_ex:20 + n_ex]
        got_refs = refs[20 + n_ex:20 + 2 * n_ex]
        acar_ref, dhcar_ref, dxccar_ref = refs[20 + 2 * n_ex:23 + 2 * n_ex]
        start, finish = _chip_exchange_phases(sum_refs, got_refs, *refs[23 + 2 * n_ex:])
        step = pl.program_id(0)
        pl.when(step == 0)(start)

        @pl.when(step == 0)
        def _():
            acar_ref[...] = jnp.zeros_like(acar_ref)
            dhcar_ref[...] = jnp.zeros_like(dhcar_ref)
            dxccar_ref[...] = jnp.zeros_like(dxccar_ref)
            gwa_ref[...] = jnp.zeros_like(gwa_ref)
            gwx_ref[...] = jnp.zeros_like(gwx_ref)
            gvec_ref[...] = jnp.zeros_like(gvec_ref)

        first_tile = step == nt - 1
        halo_on = jnp.where(first_tile, 0.0, 1.0)
        prev8 = xbh_ref[...] * halo_on
        hprev_row = hsh_ref[SUBLANES - 1:SUBLANES, :] * halo_on

        xbv = xb_ref[...]
        taps = _conv_taps(xbv, prev8)
        cw_v = cw_ref[...]
        xc, xcb, r, i, sp, a, mult = _lru_pre(taps, cw_v, cb_ref[...], wa_ref, ba_ref[...],
                                              wx_ref, bx_ref[...], ap_ref[...])
        hs = hs_ref[...]
        gv = gate_ref[...]
        sg = _sigmoid(gv)
        dy = _dot_nt(dx_ref[...], wo_ref[...])
        dhs = dy * (gv * sg)
        dgate = dy * hs * (sg * (1.0 + gv * (1.0 - sg)))

        rows = lax.broadcasted_iota(jnp.int32, a.shape, 0)
        a_next = jnp.where(rows < ts - 1, pltpu.roll(a, ts - 1, 0), acar_ref[0:1, :])
        cum_a, dh_loc = _scan_rows(a_next, dhs, reverse=True)
        dh = cum_a * dhcar_ref[0:1, :] + dh_loc
        h_prev = jnp.where(rows >= 1, pltpu.roll(hs, 1, 0), hprev_row)

        da = dh * h_prev
        ixc = i * xc
        dmult = dh * ixc
        di = dh * mult * xc
        dxc = dh * mult * i
        dlog_a = da * a - dmult * (a * a) / mult
        dr = dlog_a * ((-LRU_C) * sp)
        dsp = jnp.sum(dlog_a * ((-LRU_C) * r), axis=0, keepdims=True)
        dra = dr * r * (1.0 - r)
        dia = di * i * (1.0 - i)
        drab = dra.astype(BF16)
        diab = dia.astype(BF16)
        back = []
        for n in range(LRU_BLOCKS):
            sl = slice(n * LRU_BLOCK_W, (n + 1) * LRU_BLOCK_W)
            gwa_ref[n] += _dot_tn(xcb[:, sl], drab[:, sl])
            gwx_ref[n] += _dot_tn(xcb[:, sl], diab[:, sl])
            back.append(_dot(drab[:, sl], wat_ref[n]) + _dot(diab[:, sl], wxt_ref[n]))
        dxc = dxc + jnp.concatenate(back, axis=1)

        nxt8 = dxccar_ref[...]
        rows8 = lax.broadcasted_iota(jnp.int32, nxt8.shape, 0)
        dxb = cw_v[3:4] * dxc
        for j in range(1, CONV_WIDTH):
            rj = pltpu.roll(dxc, ts - j, 0)
            pj = pltpu.roll(nxt8, SUBLANES - j, 0)
            tail = jnp.where(rows8 >= SUBLANES - j, pj, rj[ts - SUBLANES:])
            dxb = dxb + cw_v[3 - j:4 - j] * jnp.concatenate([rj[:ts - SUBLANES], tail], axis=0)

        du_ref[:, :LRU_WIDTH] = dxb.astype(BF16)
        du_ref[:, LRU_WIDTH:] = dgate.astype(BF16)

        z = -ap_ref[...]
        gvec = [jnp.sum(dxc * taps[3 - k], axis=0, keepdims=True) for k in range(CONV_WIDTH)]
        gvec.append(jnp.sum(dxc, axis=0, keepdims=True))
        gvec.append(jnp.sum(dra, axis=0, keepdims=True))
        gvec.append(jnp.sum(dia, axis=0, keepdims=True))
        gvec.append(-dsp * _sigmoid(z))
        gvec_ref[...] += jnp.concatenate(gvec, axis=0)

        acar_ref[...] = a[0:SUBLANES, :]
        dhcar_ref[...] = dh[0:SUBLANES, :]
        dxccar_ref[...] = dxc[0:SUBLANES, :]
        pl.when(step == nt - 1)(finish)

    rev = lambda i: (nt - 1 - i, 0)
    halo = lambda i: (jnp.maximum((nt - 1 - i) * tpb - 1, 0), 0)
    tile = pl.BlockSpec((ts, LRU_WIDTH), rev)
    halo_spec = pl.BlockSpec((SUBLANES, LRU_WIDTH), halo)
    vec = _const_spec((1, LRU_WIDTH))
    blk = _const_spec((LRU_BLOCKS, LRU_BLOCK_W, LRU_BLOCK_W))
    acc_blk = pl.BlockSpec((LRU_BLOCKS, LRU_BLOCK_W, LRU_BLOCK_W), lambda i: (0, 0, 0))
    hbm = pl.BlockSpec(memory_space=pl.ANY)
    res = pl.pallas_call(
        body, name="lru_core_bwd", grid=(nt,),
        in_specs=[pl.BlockSpec((ts, D_MODEL), rev),
                  _const_spec((LRU_WIDTH, D_MODEL)),
                  tile, halo_spec, tile, tile, halo_spec,
                  _const_spec((CONV_WIDTH, LRU_WIDTH)), vec, blk, vec, blk, vec, vec, blk, blk]
        + [hbm] * n_ex,
        out_specs=[pl.BlockSpec((ts, 2 * LRU_WIDTH), rev), acc_blk, acc_blk,
                   pl.BlockSpec((SUBLANES, LRU_WIDTH), lambda i: (0, 0))] + [hbm] * n_ex,
        out_shape=[jax.ShapeDtypeStruct((s, 2 * LRU_WIDTH), BF16),
                   jax.ShapeDtypeStruct((LRU_BLOCKS, LRU_BLOCK_W, LRU_BLOCK_W), F32),
                   jax.ShapeDtypeStruct((LRU_BLOCKS, LRU_BLOCK_W, LRU_BLOCK_W), F32),
                   jax.ShapeDtypeStruct((SUBLANES, LRU_WIDTH), F32)]
        + [jax.ShapeDtypeStruct(a.shape, a.dtype) for a in chip_sums],
        scratch_shapes=[pltpu.VMEM((SUBLANES, LRU_WIDTH), F32),
                        pltpu.VMEM((SUBLANES, LRU_WIDTH), F32),
                        pltpu.VMEM((SUBLANES, LRU_WIDTH), F32)] + _chip_exchange_sems(n_ex),
        compiler_params=_params(),
    )(dx1b, w_out, xb, xb, gate, hs, hs, cw, cb, wa, ba, wx, bx, a_param, wa_t, wx_t, *chip_sums)
    return res[0], res[1], res[2], res[3], res[4:]


def _lru_in_bwd(du, w_in, x, dx1, g0, chip_sums, ts):
    s = x.shape[0]
    nt = s // ts
    n = len(chip_sums)

    def body(*refs):
        du_ref, w_ref, x_ref, dx1_ref, g_ref = refs[:5]
        sum_refs = refs[5:5 + n]
        gx_ref, gn_ref = refs[5 + n:7 + n]
        got_refs = refs[7 + n:7 + 2 * n]
        wfull_ref = refs[7 + 2 * n]
        start, finish = _chip_exchange_phases(sum_refs, got_refs, *refs[8 + 2 * n:])
        step = pl.program_id(0)
        pl.when(step == 0)(start)

        @pl.when(step == 0)
        def _():
            gn_ref[...] = jnp.zeros_like(gn_ref)
            for j in range(N_DEV):
                wfull_ref[:, j * LRU_IN_SHARD:(j + 1) * LRU_IN_SHARD] = w_ref[j]

        dh = _dot_nt(du_ref[...], wfull_ref[...])
        dxn, dgn = _norm_bwd(x_ref[...], g_ref[...], dh)
        gx_ref[...] = dx1_ref[...] + dxn
        gn_ref[...] += dgn
        pl.when(step == nt - 1)(finish)

    tile = pl.BlockSpec((ts, D_MODEL), lambda i: (i, 0))
    hbm = pl.BlockSpec(memory_space=pl.ANY)
    res = pl.pallas_call(
        body, name="lru_in_bwd", grid=(nt,),
        in_specs=[pl.BlockSpec((ts, 2 * LRU_WIDTH), lambda i: (i, 0)),
                  _const_spec((N_DEV, D_MODEL, LRU_IN_SHARD)), tile, tile,
                  _const_spec((1, D_MODEL))] + [hbm] * n,
        out_specs=[tile, pl.BlockSpec((1, D_MODEL), lambda i: (0, 0))] + [hbm] * n,
        out_shape=[jax.ShapeDtypeStruct((s, D_MODEL), F32),
                   jax.ShapeDtypeStruct((1, D_MODEL), F32)]
        + [jax.ShapeDtypeStruct(a.shape, a.dtype) for a in chip_sums],
        scratch_shapes=[pltpu.VMEM((D_MODEL, 2 * LRU_WIDTH), BF16)] + _chip_exchange_sems(n),
        compiler_params=_params(),
    )(du, w_in, x, dx1, g0, *chip_sums)
    return res[0], res[1], res[2:]


def _weight_grad(a, b, ts, name, scale=1.0, col_shards=1):
    s, ka = a.shape
    nb = b.shape[1]
    nt = s // ts
    per = nb // col_shards

    def body(a_ref, b_ref, o_ref):
        @pl.when(pl.program_id(0) == 0)
        def _():
            o_ref[...] = jnp.zeros_like(o_ref)

        if col_shards == 1:
            o_ref[...] += _dot_tn(a_ref[...], b_ref[...])
        else:
            acc = _dot_tn(a_ref[...], b_ref[...])
            for j in range(col_shards):
                o_ref[j] += acc[:, j * per:(j + 1) * per]
        if scale != 1.0:
            @pl.when(pl.program_id(0) == nt - 1)
            def _():
                o_ref[...] = o_ref[...] * scale

    out_dims = (ka, nb) if col_shards == 1 else (col_shards, ka, per)
    return pl.pallas_call(
        body, name=name, grid=(nt,),
        in_specs=[pl.BlockSpec((ts, ka), lambda i: (i, 0)),
                  pl.BlockSpec((ts, nb), lambda i: (i, 0))],
        out_specs=pl.BlockSpec(out_dims, lambda i: (0,) * len(out_dims)),
        out_shape=jax.ShapeDtypeStruct(out_dims, F32),
        compiler_params=_params(),
    )(a, b)


def _sum_parts(gp_ref):
    g = gp_ref[0].astype(F32)
    for k in range(1, gp_ref.shape[0]):
        g = g + gp_ref[k].astype(F32)
    return g


def _adamw(g_parts, w, m, v, tr, name):
    nparts, rows, cols = g_parts.shape

    def body(gp_ref, w_ref, m_ref, v_ref, g_ref, d_ref, mo_ref, vo_ref):
        g = _sum_parts(gp_ref)
        m2 = ADAM_B1 * m_ref[...] + (1.0 - ADAM_B1) * g
        v2 = ADAM_B2 * v_ref[...] + (1.0 - ADAM_B2) * (g * g)
        m_hat = m2 / (1.0 - ADAM_B1 ** ADAM_STEP)
        v_hat = v2 / (1.0 - ADAM_B2 ** ADAM_STEP)
        g_ref[...] = g
        d_ref[...] = (-ADAM_LR) * (m_hat / (jnp.sqrt(v_hat) + ADAM_EPS) + ADAM_WD * w_ref[...])
        mo_ref[...] = m2
        vo_ref[...] = v2

    tile = pl.BlockSpec((tr, cols), lambda i: (i, 0))
    out = jax.ShapeDtypeStruct((rows, cols), F32)
    return pl.pallas_call(
        body, name=name, grid=(rows // tr,),
        in_specs=[pl.BlockSpec((nparts, tr, cols), lambda i: (0, i, 0)), tile, tile, tile],
        out_specs=[tile, tile, tile, tile],
        out_shape=[out, out, out, out],
        compiler_params=_params(),
    )(g_parts, w, m, v)


def _reduce_parts(g_parts, name):
    _, rows, cols = g_parts.shape

    def body(gp_ref, g_ref):
        g_ref[...] = _sum_parts(gp_ref)

    return pl.pallas_call(
        body, name=name,
        out_shape=jax.ShapeDtypeStruct((rows, cols), F32),
        compiler_params=pltpu.CompilerParams(vmem_limit_bytes=VMEM_LIMIT_BYTES),
    )(g_parts)


def _mesh_pos():
    ix, iy, ic = lax.axis_index("x"), lax.axis_index("y"), lax.axis_index("c")
    return ix, iy, ic


def _peer(ix, iy, ic, mask):
    px = 1 - ix if mask & 4 else ix
    py = 1 - iy if mask & 2 else iy
    pc = 1 - ic if mask & 1 else ic
    return (px, py, pc), 4 * px + 2 * py + pc


def _exchange(arrays, scatter, name):
    n = len(arrays)

    def body(*refs):
        x_refs, o_refs = refs[:n], refs[n:2 * n]
        send_sems, recv_sems, local_sems = refs[2 * n:]
        ix, iy, ic = _mesh_pos()
        me = 4 * ix + 2 * iy + ic

        def src(a, dest):
            return x_refs[a].at[dest] if scatter else x_refs[a]

        local = [pltpu.make_async_copy(src(a, me), o_refs[a].at[me], local_sems.at[a])
                 for a in range(n)]
        for cp in local:
            cp.start()
        sends = []
        for mask in range(1, N_DEV):
            peer, pidx = _peer(ix, iy, ic, mask)
            for a in range(n):
                cp = pltpu.make_async_remote_copy(
                    src_ref=src(a, pidx), dst_ref=o_refs[a].at[me],
                    send_sem=send_sems.at[a, mask - 1], recv_sem=recv_sems.at[a, mask - 1],
                    device_id=peer, device_id_type=pl.DeviceIdType.MESH)
                cp.start()
                sends.append(cp)
        for mask in range(1, N_DEV):
            peer, pidx = _peer(ix, iy, ic, mask)
            for a in range(n):
                pltpu.make_async_remote_copy(
                    src_ref=src(a, me), dst_ref=o_refs[a].at[pidx],
                    send_sem=send_sems.at[a, mask - 1], recv_sem=recv_sems.at[a, mask - 1],
                    device_id=peer, device_id_type=pl.DeviceIdType.MESH).wait_recv()
        for cp in sends:
            cp.wait_send()
        for cp in local:
            cp.wait()

    out_shape = [jax.ShapeDtypeStruct(x.shape if scatter else (N_DEV,) + x.shape, x.dtype)
                 for x in arrays]
    return pl.pallas_call(
        body, name=name,
        in_specs=[pl.BlockSpec(memory_space=pl.ANY)] * n,
        out_specs=[pl.BlockSpec(memory_space=pl.ANY)] * n,
        out_shape=out_shape,
        scratch_shapes=[pltpu.SemaphoreType.DMA((n, N_DEV - 1)),
                        pltpu.SemaphoreType.DMA((n, N_DEV - 1)),
                        pltpu.SemaphoreType.DMA((n,))],
    )(*arrays)


def _gather_two_level(arrays, name):
    n = len(arrays)

    def body(*refs):
        start, forward, finish = _gather_phases(refs[:n], refs[n:2 * n], *refs[2 * n:])
        start()
        forward()
        finish()

    return pl.pallas_call(
        body, name=name,
        in_specs=[pl.BlockSpec(memory_space=pl.ANY)] * n,
        out_specs=[pl.BlockSpec(memory_space=pl.ANY)] * n,
        out_shape=[jax.ShapeDtypeStruct((N_DEV,) + x.shape, x.dtype) for x in arrays],
        scratch_shapes=_gather_sems(n),
    )(*arrays)


def _gather_sems(n):
    return [pltpu.SemaphoreType.DMA((n, N_DEV - 1)), pltpu.SemaphoreType.DMA((n, N_DEV - 1)),
            pltpu.SemaphoreType.DMA((n,))]


def _gather_phases(x_refs, o_refs, send_sems, recv_sems, local_sems):
    n = len(x_refs)
    ix, iy, ic = _mesh_pos()
    me, sibling = (ix, iy, ic), (ix, iy, 1 - ic)
    chips = [(1 - ix, iy), (ix, 1 - iy), (1 - ix, 1 - iy)]

    def idx(px, py, pc):
        return 4 * px + 2 * py + pc

    def copy(a, k, block, to, src=None):
        dst = o_refs[a].at[idx(*block)]
        return pltpu.make_async_remote_copy(
            src_ref=dst if src is None else src, dst_ref=dst,
            send_sem=send_sems.at[a, k], recv_sem=recv_sems.at[a, k],
            device_id=to, device_id_type=pl.DeviceIdType.MESH)

    def local():
        return [pltpu.make_async_copy(x_refs[a], o_refs[a].at[idx(*me)], local_sems.at[a])
                for a in range(n)]

    def first():
        out = []
        for a in range(n):
            out.append(copy(a, 0, me, sibling, src=x_refs[a]))
            out += [copy(a, 1 + j, me, (*chip, ic), src=x_refs[a])
                    for j, chip in enumerate(chips)]
        return out

    def passed():
        return [copy(a, 4 + j, (*chip, ic), sibling)
                for j, chip in enumerate(chips) for a in range(n)]

    def start():
        for cp in local() + first():
            cp.start()

    def forward():
        for j, chip in enumerate(chips):
            for a in range(n):
                copy(a, 1 + j, (*chip, ic), me).wait_recv()
                copy(a, 4 + j, (*chip, ic), sibling).start()

    def finish():
        for a in range(n):
            copy(a, 0, sibling, me).wait_recv()
            for j, chip in enumerate(chips):
                copy(a, 4 + j, (*chip, 1 - ic), me).wait_recv()
        for cp in first() + passed():
            cp.wait_send()
        for cp in local():
            cp.wait()

    return start, forward, finish


def _swap_sibling(arrays, name):
    n = len(arrays)
    n_chips = N_DEV // 2

    def body(*refs):
        x_refs, got_refs = refs[:n], refs[n:2 * n]
        send_sems, recv_sems = refs[2 * n:]
        ix, iy, ic = _mesh_pos()
        sibling = (ix, iy, 1 - ic)
        sends = []
        for a in range(n):
            for q in range(n_chips):
                cp = pltpu.make_async_remote_copy(
                    src_ref=x_refs[a].at[q, 1 - ic], dst_ref=got_refs[a].at[q],
                    send_sem=send_sems.at[a, q], recv_sem=recv_sems.at[a, q],
                    device_id=sibling, device_id_type=pl.DeviceIdType.MESH)
                cp.start()
                sends.append(cp)
        for cp in sends:
            cp.wait()

    return pl.pallas_call(
        body, name=name,
        in_specs=[pl.BlockSpec(memory_space=pl.ANY)] * n,
        out_specs=[pl.BlockSpec(memory_space=pl.ANY)] * n,
        out_shape=[jax.ShapeDtypeStruct((n_chips,) + x.shape[2:], x.dtype) for x in arrays],
        scratch_shapes=[pltpu.SemaphoreType.DMA((n, n_chips)),
                        pltpu.SemaphoreType.DMA((n, n_chips))],
    )(*arrays)


def _exchange_chips(arrays, name):
    n = len(arrays)

    def body(*refs):
        start, finish = _chip_exchange_phases(refs[:n], refs[n:2 * n], *refs[2 * n:])
        start()
        finish()

    return pl.pallas_call(
        body, name=name,
        in_specs=[pl.BlockSpec(memory_space=pl.ANY)] * n,
        out_specs=[pl.BlockSpec(memory_space=pl.ANY)] * n,
        out_shape=[jax.ShapeDtypeStruct(x.shape, x.dtype) for x in arrays],
        scratch_shapes=_chip_exchange_sems(n),
    )(*arrays)


def _chip_exchange_sems(n):
    n_chips = N_DEV // 2
    return [pltpu.SemaphoreType.DMA((n, n_chips - 1)), pltpu.SemaphoreType.DMA((n, n_chips - 1)),
            pltpu.SemaphoreType.DMA((n,))]


def _chip_exchange_phases(x_refs, o_refs, send_sems, recv_sems, local_sems):
    n = len(x_refs)
    n_chips = N_DEV // 2
    ix, iy, ic = _mesh_pos()
    my_chip = 2 * ix + iy

    def peers():
        for mask in range(1, n_chips):
            px = 1 - ix if mask & 2 else ix
            py = 1 - iy if mask & 1 else iy
            yield mask, (px, py, ic), 2 * px + py

    def local():
        return [pltpu.make_async_copy(x_refs[a].at[my_chip], o_refs[a].at[my_chip],
                                      local_sems.at[a]) for a in range(n)]

    def sends():
        return [pltpu.make_async_remote_copy(
            src_ref=x_refs[a].at[chip], dst_ref=o_refs[a].at[my_chip],
            send_sem=send_sems.at[a, mask - 1], recv_sem=recv_sems.at[a, mask - 1],
            device_id=peer, device_id_type=pl.DeviceIdType.MESH)
            for mask, peer, chip in peers() for a in range(n)]

    def start():
        for cp in local() + sends():
            cp.start()

    def finish():
        for mask, peer, chip in peers():
            for a in range(n):
                pltpu.make_async_remote_copy(
                    src_ref=x_refs[a].at[my_chip], dst_ref=o_refs[a].at[chip],
                    send_sem=send_sems.at[a, mask - 1], recv_sem=recv_sems.at[a, mask - 1],
                    device_id=peer, device_id_type=pl.DeviceIdType.MESH).wait_recv()
        for cp in sends():
            cp.wait_send()
        for cp in local():
            cp.wait()

    return start, finish


def _pair_sum(core, x, got, name):
    nq, rows, cols = got.shape

    def body(c_ref, x_ref, g_ref, o_ref):
        o_ref[...] = (x_ref[...] + g_ref[...]).astype(BF16)

    blk = pl.BlockSpec((None, rows, cols), lambda q, c: (q, 0, 0))
    return pl.pallas_call(
        body, name=name,
        grid_spec=pltpu.PrefetchScalarGridSpec(
            num_scalar_prefetch=1, grid=(nq,),
            in_specs=[pl.BlockSpec((None, None, rows, cols), lambda q, c: (q, c[0], 0, 0)), blk],
            out_specs=blk),
        out_shape=jax.ShapeDtypeStruct(got.shape, BF16),
        compiler_params=_params(),
    )(core, x, got)


def _selectors():
    r = lax.broadcasted_iota(jnp.int32, (LANES, FOX_PAD), 0)
    c = lax.broadcasted_iota(jnp.int32, (LANES, FOX_PAD), 1)
    part, head_r = r // HEADS, r % HEADS
    head_c, lane_c = c // HEAD_PAD, c % HEAD_PAD
    same = (head_r == head_c) & (part < 3)
    sel_q = jnp.where(same & (lane_c == LANE_RB + part), 1.0, 0.0)
    sel_k = jnp.where(same & (lane_c == LANE_CK + part), -1.0, 0.0)
    sel = jnp.stack([sel_q, sel_k, jnp.zeros_like(sel_q)]).astype(BF16)
    lane = lax.broadcasted_iota(jnp.int32, (1, FOX_PAD), 1) % HEAD_PAD
    ones_q = jnp.where((lane >= LANE_CK) & (lane < LANE_CK + 3), 1.0, 0.0)
    ones_k = jnp.where(((lane >= LANE_RB) & (lane < LANE_RB + 3))
                       | ((lane >= LANE_LSE) & (lane < LANE_LSE + 3)), 1.0, 0.0)
    ones_v = jnp.where((lane >= LANE_ONE_V) & (lane < LANE_ONE_V + 2), 1.0, 0.0)
    bias = jnp.stack([ones_q, ones_k, ones_v]).astype(F32)
    return sel, bias


def _chip_sums(names, send):
    send = [a.reshape((N_DEV // 2, 2) + a.shape[1:]) for a in send]
    got = _swap_sibling(send, "swap_" + names[0])
    core = lax.axis_index("c").astype(jnp.int32).reshape(1)
    return [_pair_sum(core, a, b, "pair_sum_" + n) for n, a, b in zip(names, send, got)]


def _local_step(x, target, norm_g, final_g, w_in8, conv_w, conv_b, wa, ba, wx, bx, a_param,
                w_out_b, fox_in_shard, b_f, fox_out_shard, blk=512, ts=256):
    g0, g1 = norm_g[0:1], norm_g[1:2]
    gf = final_g.reshape(1, D_MODEL)
    wa_b, wx_b = wa.astype(BF16), wx.astype(BF16)
    sel, bias = _selectors()

    xb, gate1, h0, (fox_in8, fox_out8) = _lru_in_fwd(x, g0, w_in8, [fox_in_shard, fox_out_shard],
                                                     ts)
    fox_w_in = jnp.transpose(fox_in8, (1, 0, 2)).reshape(D_MODEL, FOX_IN_COLS)
    width = HEADS * HEAD_DIM
    wf_b = jnp.pad(fox_w_in[:, 4 * width:], ((0, 0), (0, LANES - HEADS)))
    bf_pad = jnp.pad(b_f, ((0, 0), (0, LANES - HEADS)))
    fo_b = fox_out8.reshape(width, D_MODEL)
    y1, hs = _lru_core_fwd(xb, gate1, conv_w, conv_b, wa_b, ba, wx_b, bx, a_param, ts)
    x1, h1, f, cparts = _fox_pre_fwd(x, y1, w_out_b, g1, wf_b, bf_pad, ts)
    qkv = _fox_proj_fwd(h1, cparts, fox_w_in, 0, 3, sel, bias, BF16, ts, "fox_proj_qkv")
    gate2 = _fox_proj_fwd(h1, None, fox_w_in, 3, 1, None, None, F32, ts, "fox_proj_gate")[0]
    o, qb = _attn_fwd(qkv, blk, hps=4)
    dx2, dx2b, y2, loss_acc, g_final = _fox_out_loss(o, gate2, fo_b, x1, target, gf, ts)

    do, dgate2 = _fox_out_bwd(dx2b, fo_b, o, gate2, ts)
    dq, dk, dv, dcum = _attn_bwd(qb, qkv, do, blk)
    dx1, dx1b, df, du4, g_norm1, g_bf = _fox_in_bwd(dq, dk, dv, dgate2, fox_w_in, wf_b, dcum, f, x1,
                                                    dx2, g1, ts)
    tw = 512
    g_q = _weight_grad(h1, du4[0], tw, "grad_fox_wq", scale=QK_SCALE)
    g_k = _weight_grad(h1, du4[1], tw, "grad_fox_wk")
    g_v = _weight_grad(h1, du4[2], tw, "grad_fox_wv")
    g_g = _weight_grad(h1, du4[3], tw, "grad_fox_wg")
    g_f = _weight_grad(h1, df, tw, "grad_fox_wf")
    g_fox_w_in = jnp.concatenate([g_q, g_k, g_v, g_g, g_f[:, :HEADS]], axis=1)
    g_fox_w_in = jnp.transpose(g_fox_w_in.reshape(D_MODEL, N_DEV, FOX_IN_SHARD), (1, 0, 2))
    g_fox_w_out = _weight_grad(y2, dx2b, tw, "grad_fox_w_out")
    fox_sums = _chip_sums(("fox_w_in", "fox_w_out"),
                          [g_fox_w_in, g_fox_w_out.reshape(N_DEV, -1, D_MODEL)])

    du, g_wa, g_wx, g_vec, (r_fox_in, r_fox_out) = _lru_core_bwd(
        dx1b, w_out_b, xb, gate1, hs, conv_w, conv_b, wa_b, ba, wx_b, bx, a_param,
        jnp.transpose(wa_b, (0, 2, 1)), jnp.transpose(wx_b, (0, 2, 1)), fox_sums, ts)
    g_lru_w_in = _weight_grad(h0, du, tw, "grad_lru_w_in", col_shards=N_DEV)
    g_lru_w_out = _weight_grad(y1, dx1b, tw, "grad_lru_w_out")
    conv_send = jnp.transpose(g_vec[0:CONV_WIDTH].reshape(CONV_WIDTH, N_DEV, -1), (1, 0, 2))
    lru_sums = _chip_sums(("lru_w_in", "lru_conv_w", "lru_w_out"),
                          [g_lru_w_in, conv_send, g_lru_w_out.reshape(N_DEV, -1, D_MODEL)])
    grad_x, g_norm0, (r_w_in, r_conv, r_w_out) = _lru_in_bwd(du, w_in8, x, dx1, g0, lru_sums, ts)

    small = dict(
        norm_g=jnp.concatenate([g_norm0, g_norm1], axis=0), final_g=g_final[0],
        lru_conv_b=g_vec[4:5], lru_wa=g_wa, lru_ba=g_vec[5:6], lru_wx=g_wx, lru_bx=g_vec[6:7],
        lru_a_param=g_vec[7:8], fox_b_f=g_bf[:, :HEADS])
    received = dict(lru_w_in=r_w_in, lru_conv_w=r_conv, lru_w_out=r_w_out, fox_w_in=r_fox_in,
                    fox_w_out=r_fox_out)
    return loss_acc[0, 0], grad_x, small, received


SMALL =("norm_g", "final_g", "lru_conv_b", "lru_wa", "lru_ba", "lru_wx", "lru_bx", "lru_a_param",
         "fox_b_f")
ALL_WEIGHTS = ("norm_g", "final_g", "lru_w_in", "lru_conv_w", "lru_conv_b", "lru_wa", "lru_ba",
               "lru_wx", "lru_bx", "lru_a_param", "lru_w_out", "fox_w_in", "fox_b_f", "fox_w_out")


def _pack_small(d):
    rows = []
    for n in SMALL:
        a = d[n].reshape(-1)
        if a.shape[0] % LANES:
            a = jnp.pad(a, (0, LANES - a.shape[0] % LANES))
        rows.append(a.reshape(-1, LANES))
    packed = jnp.concatenate(rows, axis=0)
    return jnp.pad(packed, ((0, N_DEV * SMALL_CHUNK_ROWS - packed.shape[0]), (0, 0)))


def _unpack_small(packed, like):
    out, off = {}, 0
    for n, nrows in zip(SMALL, SMALL_ROWS):
        size = like[n].size
        out[n] = packed[off:off + nrows].reshape(-1)[:size].reshape(like[n].shape)
        off += nrows
    return out


def kernel(x, norm_g, final_g, lru_w_in, lru_conv_w, lru_conv_b, lru_wa, lru_ba, lru_wx, lru_bx, lru_a_param, lru_w_out, fox_w_in, fox_b_f, fox_w_out, loss_target, m_norm_g, m_final_g, m_lru_w_in, m_lru_conv_w, m_lru_conv_b, m_lru_wa, m_lru_ba, m_lru_wx, m_lru_bx, m_lru_a_param, m_lru_w_out, m_fox_w_in, m_fox_b_f, m_fox_w_out, v_norm_g, v_final_g, v_lru_w_in, v_lru_conv_w, v_lru_conv_b, v_lru_wa, v_lru_ba, v_lru_wx, v_lru_bx, v_lru_a_param, v_lru_w_out, v_fox_w_in, v_fox_b_f, v_fox_w_out):
    w_loc = dict(norm_g=norm_g, final_g=final_g, lru_w_in=lru_w_in, lru_conv_w=lru_conv_w,
                 lru_conv_b=lru_conv_b, lru_wa=lru_wa, lru_ba=lru_ba, lru_wx=lru_wx, lru_bx=lru_bx,
                 lru_a_param=lru_a_param, lru_w_out=lru_w_out, fox_w_in=fox_w_in, fox_b_f=fox_b_f,
                 fox_w_out=fox_w_out)
    m_loc = dict(norm_g=m_norm_g, final_g=m_final_g, lru_w_in=m_lru_w_in, lru_conv_w=m_lru_conv_w,
                 lru_conv_b=m_lru_conv_b, lru_wa=m_lru_wa, lru_ba=m_lru_ba, lru_wx=m_lru_wx,
                 lru_bx=m_lru_bx, lru_a_param=m_lru_a_param, lru_w_out=m_lru_w_out,
                 fox_w_in=m_fox_w_in, fox_b_f=m_fox_b_f, fox_w_out=m_fox_w_out)
    v_loc = dict(norm_g=v_norm_g, final_g=v_final_g, lru_w_in=v_lru_w_in, lru_conv_w=v_lru_conv_w,
                 lru_conv_b=v_lru_conv_b, lru_wa=v_lru_wa, lru_ba=v_lru_ba, lru_wx=v_lru_wx,
                 lru_bx=v_lru_bx, lru_a_param=v_lru_a_param, lru_w_out=v_lru_w_out,
                 fox_w_in=v_fox_w_in, fox_b_f=v_fox_b_f, fox_w_out=v_fox_w_out)

    w_in8, conv8, w_out8 = _gather_two_level(
        [lru_w_in[0].astype(BF16), lru_conv_w[0], lru_w_out[0].astype(BF16)], "gather_weights")
    conv_full = jnp.transpose(conv8, (1, 0, 2)).reshape(CONV_WIDTH, LRU_WIDTH)

    loss, grad_x, small_grads, received = _local_step(
        x[0], loss_target[0], norm_g, final_g, w_in8, conv_full, lru_conv_b, lru_wa[0], lru_ba,
        lru_wx[0], lru_bx, lru_a_param, w_out8.reshape(LRU_WIDTH, D_MODEL),
        fox_w_in[0].astype(BF16), fox_b_f, fox_w_out[0].astype(BF16))

    out = {}
    for n, tr in (("lru_w_in", 256), ("lru_conv_w", CONV_WIDTH), ("lru_w_out", 96),
                  ("fox_w_in", 128), ("fox_w_out", 64)):
        res = _adamw(received[n], w_loc[n][0], m_loc[n][0], v_loc[n][0], tr, "adamw_" + n)
        out[n] = [a[None] for a in res]

    small_sums = _chip_sums(
        ("small",), [_pack_small(small_grads).reshape(N_DEV, SMALL_CHUNK_ROWS, LANES)])
    r_small, = _exchange_chips(small_sums, "scatter_small_grads")

    g_chunk = _reduce_parts(r_small, "reduce_small_grads")
    g_small, = _exchange([g_chunk], False, "gather_small_grads")
    g_small = g_small.reshape(1, N_DEV * SMALL_CHUNK_ROWS, LANES)
    res = _adamw(g_small, _pack_small(w_loc), _pack_small(m_loc), _pack_small(v_loc),
                 N_DEV * SMALL_CHUNK_ROWS, "adamw_replicated")
    small_out = [_unpack_small(a, w_loc) for a in res]
    for n in SMALL:
        out[n] = [d[n] for d in small_out]

    loss = lax.psum(loss, ("x", "y", "c"))
    return (loss, grad_x[None], *[out[n][0] for n in ALL_WEIGHTS], *[out[n][1] for n in ALL_WEIGHTS],
            *[out[n][2] for n in ALL_WEIGHTS], *[out[n][3] for n in ALL_WEIGHTS])
```

```python
import functools

import jax
import jax.numpy as jnp
from jax import lax
from jax.experimental import pallas as pl
from jax.experimental.pallas import tpu as pltpu

F32 = jnp.float32
BF16 = jnp.bfloat16

D_MODEL = 1024
LRU_WIDTH = 1536
LRU_BLOCKS = 12
LRU_BLOCK_W = 128
CONV_WIDTH = 4
LRU_C = 8.0
HEADS = 16
HEAD_DIM = 64
HEAD_PAD = 128
FOX_PAD = HEADS * HEAD_PAD
HEADS_PER_STEP = 2
QK_SCALE = 1.0 / HEAD_DIM ** 0.5
EPS = 1e-6
NEG_BIG = -1e30
N_DEV = 8

ADAM_LR = 0.001
ADAM_B1 = 0.9
ADAM_B2 = 0.999
ADAM_EPS = 1e-08
ADAM_WD = 0.01
ADAM_STEP = 10

LANE_RB = 64
LANE_CK = 67
LANE_LSE = 70
LANE_ONE_V = 64

VMEM_LIMIT_BYTES = 56 * 1024 * 1024
LANES = 128
SUBLANES = 8

LRU_IN_SHARD = 2 * LRU_WIDTH // N_DEV
FOX_IN_COLS = 4 * HEADS * HEAD_DIM + HEADS
FOX_IN_SHARD = FOX_IN_COLS // N_DEV

SMALL_ROWS = (16, 8, 12, 1536, 12, 1536, 12, 12, 1)
SMALL_CHUNK_ROWS = 400
assert sum(SMALL_ROWS) <= N_DEV * SMALL_CHUNK_ROWS


def _params(n_grid_axes=1):
    return pltpu.CompilerParams(
        dimension_semantics=("arbitrary",) * n_grid_axes,
        vmem_limit_bytes=VMEM_LIMIT_BYTES)


def _const_spec(shape):
    nd = len(shape)
    return pl.BlockSpec(shape, lambda *_: (0,) * nd, pipeline_mode=pl.Buffered(1))


def _shift_down(x, k, fill):
    rows = lax.broadcasted_iota(jnp.int32, x.shape, 0)
    return jnp.where(rows >= k, pltpu.roll(x, k, 0), fill)


def _shift_up(x, k, fill):
    n = x.shape[0]
    rows = lax.broadcasted_iota(jnp.int32, x.shape, 0)
    return jnp.where(rows < n - k, pltpu.roll(x, n - k, 0), fill)


def _scan_rows(a, b, reverse=False):
    n = a.shape[0]
    shift = _shift_up if reverse else _shift_down
    k = 1
    while k < n:
        b = a * shift(b, k, 0.0) + b
        a = a * shift(a, k, 1.0)
        k *= 2
    return a, b


def _cumsum_rows(x, reverse=False):
    n = x.shape[0]
    shift = _shift_up if reverse else _shift_down
    k = 1
    while k < n:
        x = x + shift(x, k, 0.0)
        k *= 2
    return x


def _rstd(x):
    return lax.rsqrt(jnp.mean(x * x, axis=-1, keepdims=True) + EPS)


def _norm_bwd(x, g, dh):
    rstd = _rstd(x)
    xhat = x * rstd
    dg = jnp.sum(dh * xhat, axis=0, keepdims=True)
    dxh = dh * g
    dx = rstd * (dxh - xhat * jnp.mean(dxh * xhat, axis=-1, keepdims=True))
    return dx, dg


def _split3(x):
    hi = x.astype(BF16)
    r1 = x - hi.astype(F32)
    mid = r1.astype(BF16)
    lo = (r1 - mid.astype(F32)).astype(BF16)
    return hi, mid, lo


def _sigmoid(x):
    return jax.nn.sigmoid(x)


def _dot(a, b):
    return jnp.dot(a, b, preferred_element_type=F32)


def _dot_nt(a, b):
    return lax.dot_general(a, b, (((1,), (1,)), ((), ())), preferred_element_type=F32)


def _dot_tn(a, b):
    return lax.dot_general(a, b, (((0,), (0,)), ((), ())), preferred_element_type=F32)


def _heads_to_padded(u):
    n = u.shape[0]
    low = lax.broadcasted_iota(jnp.int32, (n, LANES), 1) < HEAD_DIM
    zero = jnp.zeros((n, LANES), u.dtype)
    cols = []
    for p in range(HEADS // 2):
        pair = u[:, p * LANES:(p + 1) * LANES]
        cols.append(jnp.where(low, pair, zero))
        cols.append(jnp.where(low, pltpu.roll(pair, HEAD_DIM, 1), zero))
    return jnp.concatenate(cols, axis=1)


def _heads_from_padded(x):
    n = x.shape[0]
    low = lax.broadcasted_iota(jnp.int32, (n, LANES), 1) < HEAD_DIM
    cols = []
    for p in range(HEADS // 2):
        even = x[:, (2 * p) * HEAD_PAD:(2 * p + 1) * HEAD_PAD]
        odd = x[:, (2 * p + 1) * HEAD_PAD:(2 * p + 2) * HEAD_PAD]
        cols.append(jnp.where(low, even, pltpu.roll(odd, HEAD_DIM, 1)))
    return jnp.concatenate(cols, axis=1)


def _conv_taps(xb, prev8):
    rows8 = lax.broadcasted_iota(jnp.int32, prev8.shape, 0)
    taps = [xb]
    for j in range(1, CONV_WIDTH):
        r = pltpu.roll(xb, j, 0)
        p = pltpu.roll(prev8, j, 0)
        head = jnp.where(rows8 < j, p, r[0:SUBLANES])
        taps.append(jnp.concatenate([head, r[SUBLANES:]], axis=0))
    return taps


def _lru_pre(taps, cw, cb, wa_ref, ba, wx_ref, bx, a_param):
    xc = cb + cw[3:4] * taps[0] + cw[2:3] * taps[1] + cw[1:2] * taps[2] + cw[0:1] * taps[3]
    xcb = xc.astype(BF16)
    ra, ia = [], []
    for n in range(LRU_BLOCKS):
        blk = xcb[:, n * LRU_BLOCK_W:(n + 1) * LRU_BLOCK_W]
        ra.append(_dot(blk, wa_ref[n]))
        ia.append(_dot(blk, wx_ref[n]))
    r = _sigmoid(jnp.concatenate(ra, axis=1) + ba)
    i = _sigmoid(jnp.concatenate(ia, axis=1) + bx)
    z = -a_param
    sp = jnp.maximum(z, 0.0) + jnp.log1p(jnp.exp(-jnp.abs(z)))
    log_a = (-LRU_C) * r * sp
    a = jnp.exp(log_a)
    one_minus_a2 = -jnp.tanh(log_a) * (a * a + 1.0)
    mult = jnp.sqrt(one_minus_a2)
    return xc, xcb, r, i, sp, a, mult


def _lru_in_fwd(x, g0, w_in, later_shards, ts):
    s = x.shape[0]
    nt = s // ts
    n = len(later_shards)

    def body(*refs):
        x_ref, g_ref, w_ref = refs[:3]
        shard_refs = refs[3:3 + n]
        xb_ref, gate_ref, h_ref = refs[3 + n:6 + n]
        wfull_ref = refs[6 + 2 * n]
        start, forward, finish = _gather_phases(shard_refs, refs[6 + n:6 + 2 * n],
                                                *refs[7 + 2 * n:])
        step = pl.program_id(0)
        pl.when(step == 0)(start)

        @pl.when(step == 0)
        def _():
            for j in range(N_DEV):
                wfull_ref[:, j * LRU_IN_SHARD:(j + 1) * LRU_IN_SHARD] = w_ref[j]

        xv = x_ref[...]
        h = (xv * _rstd(xv) * g_ref[...]).astype(BF16)
        u = _dot(h, wfull_ref[...])
        xb_ref[...] = u[:, :LRU_WIDTH]
        gate_ref[...] = u[:, LRU_WIDTH:]
        h_ref[...] = h
        pl.when(step == (2 * nt) // 3)(forward)
        pl.when(step == nt - 1)(finish)

    hbm = pl.BlockSpec(memory_space=pl.ANY)
    res = pl.pallas_call(
        body, name="lru_in_fwd", grid=(nt,),
        in_specs=[pl.BlockSpec((ts, D_MODEL), lambda i: (i, 0)),
                  _const_spec((1, D_MODEL)),
                  _const_spec((N_DEV, D_MODEL, LRU_IN_SHARD))] + [hbm] * n,
        out_specs=[pl.BlockSpec((ts, LRU_WIDTH), lambda i: (i, 0)),
                   pl.BlockSpec((ts, LRU_WIDTH), lambda i: (i, 0)),
                   pl.BlockSpec((ts, D_MODEL), lambda i: (i, 0))] + [hbm] * n,
        out_shape=[jax.ShapeDtypeStruct((s, LRU_WIDTH), F32),
                   jax.ShapeDtypeStruct((s, LRU_WIDTH), F32),
                   jax.ShapeDtypeStruct((s, D_MODEL), BF16)]
        + [jax.ShapeDtypeStruct((N_DEV,) + a.shape, a.dtype) for a in later_shards],
        scratch_shapes=[pltpu.VMEM((D_MODEL, 2 * LRU_WIDTH), BF16)] + _gather_sems(n),
        compiler_params=_params(),
    )(x, g0, w_in, *later_shards)
    return res[0], res[1], res[2], res[3:]


def _lru_core_fwd(xb, gate, cw, cb, wa, ba, wx, bx, a_param, ts):
    s = xb.shape[0]

    def body(xb_ref, gate_ref, cw_ref, cb_ref, wa_ref, ba_ref, wx_ref, bx_ref, ap_ref,
             y_ref, hs_ref, prev_ref, hcar_ref):
        @pl.when(pl.program_id(0) == 0)
        def _():
            prev_ref[...] = jnp.zeros_like(prev_ref)
            hcar_ref[...] = jnp.zeros_like(hcar_ref)

        xbv = xb_ref[...]
        taps = _conv_taps(xbv, prev_ref[...])
        xc, _, _, i, _, a, mult = _lru_pre(taps, cw_ref[...], cb_ref[...], wa_ref, ba_ref[...],
                                           wx_ref, bx_ref[...], ap_ref[...])
        bterm = mult * (i * xc)
        cum_a, hloc = _scan_rows(a, bterm)
        hs = cum_a * hcar_ref[SUBLANES - 1:SUBLANES, :] + hloc
        gv = gate_ref[...]
        y_ref[...] = (hs * (gv * _sigmoid(gv))).astype(BF16)
        hs_ref[...] = hs
        prev_ref[...] = xbv[ts - SUBLANES:, :]
        hcar_ref[...] = hs[ts - SUBLANES:, :]

    vec = _const_spec((1, LRU_WIDTH))
    blk = _const_spec((LRU_BLOCKS, LRU_BLOCK_W, LRU_BLOCK_W))
    tile = pl.BlockSpec((ts, LRU_WIDTH), lambda i: (i, 0))
    return pl.pallas_call(
        body, name="lru_core_fwd", grid=(s // ts,),
        in_specs=[tile, tile, _const_spec((CONV_WIDTH, LRU_WIDTH)), vec, blk, vec, blk, vec, vec],
        out_specs=[tile, tile],
        out_shape=[jax.ShapeDtypeStruct((s, LRU_WIDTH), BF16),
                   jax.ShapeDtypeStruct((s, LRU_WIDTH), F32)],
        scratch_shapes=[pltpu.VMEM((SUBLANES, LRU_WIDTH), F32),
                        pltpu.VMEM((SUBLANES, LRU_WIDTH), F32)],
        compiler_params=_params(),
    )(xb, gate, cw, cb, wa, ba, wx, bx, a_param)


def _fox_pre_fwd(x, y, w_out, g1, wf, bf, ts):
    s = x.shape[0]

    def body(x_ref, y_ref, w_ref, g_ref, wf_ref, bf_ref, x1_ref, h1_ref, f_ref, cp_ref, ccar_ref):
        @pl.when(pl.program_id(0) == 0)
        def _():
            ccar_ref[...] = jnp.zeros_like(ccar_ref)

        x1 = x_ref[...] + _dot(y_ref[...], w_ref[...])
        h1 = (x1 * _rstd(x1) * g_ref[...]).astype(BF16)
        f = _dot(h1, wf_ref[...]) + bf_ref[...]
        logsig = jnp.minimum(f, 0.0) - jnp.log1p(jnp.exp(-jnp.abs(f)))
        cum = _cumsum_rows(logsig) + ccar_ref[SUBLANES - 1:SUBLANES, :]
        hi, mid, lo = _split3(cum)
        lane = lax.broadcasted_iota(jnp.int32, cum.shape, 1)
        packed = jnp.where(lane < HEADS, hi.astype(F32), jnp.where(
            lane < 2 * HEADS, pltpu.roll(mid.astype(F32), HEADS, 1), jnp.where(
                lane < 3 * HEADS, pltpu.roll(lo.astype(F32), 2 * HEADS, 1), 0.0)))
        x1_ref[...] = x1
        h1_ref[...] = h1
        f_ref[...] = f
        cp_ref[...] = packed.astype(BF16)
        ccar_ref[...] = cum[ts - SUBLANES:, :]

    return pl.pallas_call(
        body, name="fox_pre_fwd", grid=(s // ts,),
        in_specs=[pl.BlockSpec((ts, D_MODEL), lambda i: (i, 0)),
                  pl.BlockSpec((ts, LRU_WIDTH), lambda i: (i, 0)),
                  _const_spec((LRU_WIDTH, D_MODEL)),
                  _const_spec((1, D_MODEL)),
                  _const_spec((D_MODEL, LANES)),
                  _const_spec((1, LANES))],
        out_specs=[pl.BlockSpec((ts, D_MODEL), lambda i: (i, 0)),
                   pl.BlockSpec((ts, D_MODEL), lambda i: (i, 0)),
                   pl.BlockSpec((ts, LANES), lambda i: (i, 0)),
                   pl.BlockSpec((ts, LANES), lambda i: (i, 0))],
        out_shape=[jax.ShapeDtypeStruct((s, D_MODEL), F32),
                   jax.ShapeDtypeStruct((s, D_MODEL), BF16),
                   jax.ShapeDtypeStruct((s, LANES), F32),
                   jax.ShapeDtypeStruct((s, LANES), BF16)],
        scratch_shapes=[pltpu.VMEM((SUBLANES, LANES), F32)],
        compiler_params=_params(),
    )(x, y, w_out, g1, wf, bf)


def _fox_proj_fwd(h1, cparts, w, first, ng, sel, bias, out_dtype, ts, name):
    s = h1.shape[0]
    width = HEADS * HEAD_DIM
    use_sel = sel is not None

    def body(*refs):
        if use_sel:
            h_ref, cp_ref, w_ref, sel_ref, b_ref, o_ref = refs
            proj = _dot(h_ref[...], w_ref[...])
            if first == 0:
                proj = proj * jnp.where(pl.program_id(0) == 0, QK_SCALE, 1.0)
            acc = _heads_to_padded(proj) + _dot(cp_ref[...], sel_ref[...]) + b_ref[...]
        else:
            h_ref, w_ref, o_ref = refs
            acc = _heads_to_padded(_dot(h_ref[...], w_ref[...]))
        o_ref[...] = acc.astype(out_dtype)

    in_specs = [pl.BlockSpec((ts, D_MODEL), lambda j, i: (i, 0))]
    args = [h1]
    if use_sel:
        in_specs.append(pl.BlockSpec((ts, LANES), lambda j, i: (i, 0)))
        args.append(cparts)
    in_specs.append(pl.BlockSpec((D_MODEL, width), lambda j, i: (0, first + j)))
    args.append(w)
    if use_sel:
        in_specs.append(pl.BlockSpec((None, LANES, FOX_PAD), lambda j, i: (j, 0, 0)))
        in_specs.append(pl.BlockSpec((None, 1, FOX_PAD), lambda j, i: (j, 0, 0)))
        args += [sel, bias]
    return pl.pallas_call(
        body, name=name, grid=(ng, s // ts),
        in_specs=in_specs,
        out_specs=pl.BlockSpec((None, ts, FOX_PAD), lambda j, i: (j, i, 0)),
        out_shape=jax.ShapeDtypeStruct((ng, s, FOX_PAD), out_dtype),
        compiler_params=_params(2),
    )(*args)


def _attn_fwd(qkv, blk, hps=HEADS_PER_STEP):
    s = qkv.shape[1]
    nblk = s // blk
    wide = 2 * blk
    heads = [slice(i * HEAD_PAD, (i + 1) * HEAD_PAD) for i in range(hps)]

    def body(q_ref, k_ref, v_ref, o_ref, qb_ref, acc_ref, m_ref):
        qi = pl.program_id(1)
        row = lax.broadcasted_iota(jnp.int32, (blk, blk), 0)
        col = lax.broadcasted_iota(jnp.int32, (blk, blk), 1)
        lane = lax.broadcasted_iota(jnp.int32, (blk, HEAD_PAD), 1)
        qs = [q_ref[:, hd] for hd in heads]
        for i in range(hps):
            acc_ref[i] = jnp.zeros((blk, HEAD_PAD), F32)
            m_ref[i] = jnp.full((blk, HEAD_PAD), NEG_BIG, F32)

        def step(k0, size, masked):
            scores = [_dot_nt(q, k_ref[pl.ds(k0, size), hd]) for q, hd in zip(qs, heads)]
            for i, (sc, hd) in enumerate(zip(scores, heads)):
                v = v_ref[pl.ds(k0, size), hd]
                if masked:
                    sc = jnp.where(col <= row, sc, NEG_BIG)
                m = m_ref[i]
                m_new = jnp.maximum(m, jnp.max(sc, axis=-1, keepdims=True))
                p = jnp.exp((sc - jnp.tile(m_new, (1, size // HEAD_PAD))).astype(BF16))
                acc_ref[i] = jnp.exp(m - m_new) * acc_ref[i] + _dot(p, v)
                m_ref[i] = m_new

        def wide_step(kk, _):
            step(pl.multiple_of(kk * wide, wide), wide, False)
            return 0

        lax.fori_loop(0, qi // 2, wide_step, 0)

        @pl.when(qi % 2 == 1)
        def _():
            step(pl.multiple_of((qi - 1) * blk, blk), blk, False)

        step(pl.multiple_of(qi * blk, blk), blk, True)
        for i, (q, hd) in enumerate(zip(qs, heads)):
            acc = acc_ref[i]
            l = jnp.broadcast_to(acc[:, LANE_ONE_V:LANE_ONE_V + 1], (blk, HEAD_PAD))
            o_ref[:, hd] = (acc / l).astype(BF16)
            hi, mid, lo = _split3(-(m_ref[i] + jnp.log(l)))
            qb_ref[:, hd] = jnp.where(lane == LANE_LSE, hi, jnp.where(
                lane == LANE_LSE + 1, mid, jnp.where(lane == LANE_LSE + 2, lo, q)))

    width = hps * HEAD_PAD

    def whole(j):
        return pl.BlockSpec((None, s, width), lambda h, i: (j, 0, h))

    out_spec = pl.BlockSpec((blk, width), lambda h, i: (i, h))
    return pl.pallas_call(
        body, name="attn_fwd", grid=(HEADS // hps, nblk),
        in_specs=[pl.BlockSpec((None, blk, width), lambda h, i: (0, i, h)), whole(1), whole(2)],
        out_specs=[out_spec, out_spec],
        out_shape=[jax.ShapeDtypeStruct((s, FOX_PAD), BF16),
                   jax.ShapeDtypeStruct((s, FOX_PAD), BF16)],
        scratch_shapes=[pltpu.VMEM((hps, blk, HEAD_PAD), F32),
                        pltpu.VMEM((hps, blk, HEAD_PAD), F32)],
        compiler_params=_params(2),
    )(qkv, qkv, qkv)


def _fox_out_loss(o, gate, w_out, x1, target, gf, ts):
    s = x1.shape[0]

    def body(o_ref, gt_ref, w_ref, x1_ref, t_ref, g_ref, dx2_ref, dx2b_ref, y2_ref, loss_ref,
             gfin_ref):
        @pl.when(pl.program_id(0) == 0)
        def _():
            loss_ref[...] = jnp.zeros_like(loss_ref)
            gfin_ref[...] = jnp.zeros_like(gfin_ref)

        gv = gt_ref[...]
        y2 = _heads_from_padded(o_ref[...] * (gv * _sigmoid(gv))).astype(BF16)
        x2 = x1_ref[...] + _dot(y2, w_ref[...])
        rstd = _rstd(x2)
        xhat = x2 * rstd
        g = g_ref[...]
        diff = xhat * g - t_ref[...]
        loss_ref[...] += 0.5 * jnp.sum(jnp.mean(diff * diff, axis=-1, keepdims=True))
        dy = diff * (1.0 / D_MODEL)
        gfin_ref[...] += jnp.sum(dy * xhat, axis=0, keepdims=True)
        dxh = dy * g
        dx2 = rstd * (dxh - xhat * jnp.mean(dxh * xhat, axis=-1, keepdims=True))
        dx2_ref[...] = dx2
        dx2b_ref[...] = dx2.astype(BF16)
        y2_ref[...] = y2

    return pl.pallas_call(
        body, name="fox_out_loss", grid=(s // ts,),
        in_specs=[pl.BlockSpec((ts, FOX_PAD), lambda i: (i, 0)),
                  pl.BlockSpec((ts, FOX_PAD), lambda i: (i, 0)),
                  _const_spec((HEADS * HEAD_DIM, D_MODEL)),
                  pl.BlockSpec((ts, D_MODEL), lambda i: (i, 0)),
                  pl.BlockSpec((ts, D_MODEL), lambda i: (i, 0)),
                  _const_spec((1, D_MODEL))],
        out_specs=[pl.BlockSpec((ts, D_MODEL), lambda i: (i, 0)),
                   pl.BlockSpec((ts, D_MODEL), lambda i: (i, 0)),
                   pl.BlockSpec((ts, HEADS * HEAD_DIM), lambda i: (i, 0)),
                   pl.BlockSpec((SUBLANES, LANES), lambda i: (0, 0)),
                   pl.BlockSpec((1, D_MODEL), lambda i: (0, 0))],
        out_shape=[jax.ShapeDtypeStruct((s, D_MODEL), F32),
                   jax.ShapeDtypeStruct((s, D_MODEL), BF16),
                   jax.ShapeDtypeStruct((s, HEADS * HEAD_DIM), BF16),
                   jax.ShapeDtypeStruct((SUBLANES, LANES), F32),
                   jax.ShapeDtypeStruct((1, D_MODEL), F32)],
        compiler_params=_params(),
    )(o, gate, w_out, x1, target, gf)


def _fox_out_bwd(dx2, w_out, o, gate, ts):
    s = dx2.shape[0]

    def body(dx_ref, w_ref, o_ref, gt_ref, do_ref, dg_ref):
        lane = lax.broadcasted_iota(jnp.int32, (ts, HEAD_PAD), 1)
        dy2 = _heads_to_padded(_dot_nt(dx_ref[...], w_ref[...]))
        gv = gt_ref[...]
        sg = _sigmoid(gv)
        ov = o_ref[...]
        dov = dy2 * (gv * sg)
        dg_ref[...] = (dy2 * ov * (sg * (1.0 + gv * (1.0 - sg)))).astype(BF16)
        prod = dov * ov
        for h in range(HEADS):
            sl = slice(h * HEAD_PAD, (h + 1) * HEAD_PAD)
            delta = jnp.sum(prod[:, sl], axis=-1, keepdims=True)
            hi = delta.astype(BF16)
            lo = (delta - hi.astype(F32)).astype(BF16)
            do_h = dov[:, sl].astype(BF16)
            do_ref[:, sl] = jnp.where(lane == LANE_ONE_V, -hi,
                                      jnp.where(lane == LANE_ONE_V + 1, -lo, do_h))

    tile = pl.BlockSpec((ts, FOX_PAD), lambda i: (i, 0))
    return pl.pallas_call(
        body, name="fox_out_bwd", grid=(s // ts,),
        in_specs=[pl.BlockSpec((ts, D_MODEL), lambda i: (i, 0)),
                  _const_spec((HEADS * HEAD_DIM, D_MODEL)), tile, tile],
        out_specs=[tile, tile],
        out_shape=[jax.ShapeDtypeStruct((s, FOX_PAD), BF16),
                   jax.ShapeDtypeStruct((s, FOX_PAD), BF16)],
        compiler_params=_params(),
    )(dx2, w_out, o, gate)


def _attn_bwd(qb, qkv, do, blk):
    s = qb.shape[0]
    nblk = s // blk
    half = blk // 2
    heads = [slice(i * HEAD_PAD, (i + 1) * HEAD_PAD) for i in range(HEADS_PER_STEP)]

    def body(q_ref, k_ref, v_ref, do_ref, dq_ref, dk_ref, dv_ref, dcum_ref, dq_acc, dkt_acc,
             dvt_acc, qt_ref, dot_ref):
        group = pl.program_id(0)
        kj = pl.program_id(1)
        row = lax.broadcasted_iota(jnp.int32, (blk, blk), 0)
        col = lax.broadcasted_iota(jnp.int32, (blk, blk), 1)
        lane = lax.broadcasted_iota(jnp.int32, (blk, LANES), 1)
        mine = [lane == group * HEADS_PER_STEP + i for i in range(HEADS_PER_STEP)]

        @pl.when(kj == 0)
        def _():
            dq_acc[...] = jnp.zeros_like(dq_acc)

            def transpose_block(bi, _):
                r0 = pl.multiple_of(bi * blk, blk)
                for i, hd in enumerate(heads):
                    qt_ref[i, bi] = q_ref[pl.ds(r0, blk), hd].T
                    dot_ref[i, bi] = do_ref[pl.ds(r0, blk), hd].T
                return 0

            lax.fori_loop(0, nblk, transpose_block, 0)

        @pl.when((group == 0) & (kj == 0))
        def _():
            dcum_ref[...] = jnp.zeros_like(dcum_ref)

        k0 = pl.multiple_of(kj * blk, blk)
        ks = [k_ref[:, hd] for hd in heads]
        vs = [v_ref[:, hd] for hd in heads]

        def step(qi, q_lo, nq, k_lo, nk, masked):
            q0 = pl.multiple_of(qi * blk + q_lo, half)
            qs = [q_ref[pl.ds(q0, nq), hd] for hd in heads]
            dos = [do_ref[pl.ds(q0, nq), hd] for hd in heads]
            kk = [k[k_lo:k_lo + nk] for k in ks]
            vv = [v[k_lo:k_lo + nk] for v in vs]
            scores = [_dot_nt(q, k) for q, k in zip(qs, kk)]
            dps = [_dot_nt(dov, v) for dov, v in zip(dos, vv)]
            for i, (hd, k, sc, dp) in enumerate(zip(heads, kk, scores, dps)):
                p = jnp.exp(sc.astype(BF16))
                if masked:
                    p = jnp.where(col[:nq, :nk] + k_lo <= row[:nq, :nk] + q_lo, p,
                                  jnp.zeros_like(p))
                ds = (p.astype(F32) * dp).astype(BF16)
                dvt = _dot(dot_ref[i, qi, :, q_lo:q_lo + nq], p)
                dkt = _dot(qt_ref[i, qi, :, q_lo:q_lo + nq], ds)
                if masked:
                    dvt_acc[i, :, k_lo:k_lo + nk] = dvt
                    dkt_acc[i, :, k_lo:k_lo + nk] = dkt
                else:
                    dvt_acc[i, :, k_lo:k_lo + nk] += dvt
                    dkt_acc[i, :, k_lo:k_lo + nk] += dkt
                dq_acc[pl.ds(q0, nq), hd] += _dot(ds, k)

        step(kj, 0, blk, 0, half, True)
        step(kj, half, half, half, half, True)

        def q_step(qi, _):
            step(qi, 0, blk, 0, blk, False)
            return 0

        lax.fori_loop(kj + 1, nblk, q_step, 0)
        dcum = dcum_ref[pl.ds(k0, blk), :]
        for i, (hd, mask) in enumerate(zip(heads, mine)):
            dk = dkt_acc[i].T
            dk_ref[:, hd] = dk.astype(BF16)
            dv_ref[:, hd] = dvt_acc[i].astype(BF16).T
            dcum = jnp.where(mask, -dk[:, LANE_CK:LANE_CK + 1], dcum)
        dcum_ref[pl.ds(k0, blk), :] = dcum

        @pl.when(kj == nblk - 1)
        def _():
            def finish(bi, _):
                r0 = pl.multiple_of(bi * blk, blk)
                dcum = dcum_ref[pl.ds(r0, blk), :]
                for hd, mask in zip(heads, mine):
                    dq = dq_acc[pl.ds(r0, blk), hd]
                    dq_ref[pl.ds(r0, blk), hd] = dq.astype(BF16)
                    dcum = dcum + jnp.where(mask, dq[:, LANE_RB:LANE_RB + 1], 0.0)
                dcum_ref[pl.ds(r0, blk), :] = dcum
                return 0

            lax.fori_loop(0, nblk, finish, 0)

    width = HEADS_PER_STEP * HEAD_PAD
    whole = pl.BlockSpec((s, width), lambda h, j: (0, h))
    whole_in = pl.BlockSpec((s, width), lambda h, j: (0, h), pipeline_mode=pl.Buffered(1))
    part = pl.BlockSpec((blk, width), lambda h, j: (j, h))
    out = jax.ShapeDtypeStruct((s, FOX_PAD), BF16)
    return pl.pallas_call(
        body, name="attn_bwd", grid=(HEADS // HEADS_PER_STEP, nblk),
        in_specs=[whole_in,
                  pl.BlockSpec((None, blk, width), lambda h, j: (1, j, h)),
                  pl.BlockSpec((None, blk, width), lambda h, j: (2, j, h)),
                  whole_in],
        out_specs=[whole, part, part, pl.BlockSpec((s, LANES), lambda h, j: (0, 0))],
        out_shape=[out, out, out, jax.ShapeDtypeStruct((s, LANES), F32)],
        scratch_shapes=[pltpu.VMEM((s, width), F32),
                        pltpu.VMEM((HEADS_PER_STEP, HEAD_PAD, blk), F32),
                        pltpu.VMEM((HEADS_PER_STEP, HEAD_PAD, blk), F32),
                        pltpu.VMEM((HEADS_PER_STEP, nblk, HEAD_PAD, blk), BF16),
                        pltpu.VMEM((HEADS_PER_STEP, nblk, HEAD_PAD, blk), BF16)],
        compiler_params=_params(2),
    )(qb, qkv, qkv, do)


def _fox_in_bwd(dq, dk, dv, dg, wt, wft, dcum, f, x1, dx2, g1, ts):
    s = x1.shape[0]
    nt = s // ts
    width = HEADS * HEAD_DIM

    def body(dq_ref, dk_ref, dv_ref, dg_ref, wt_ref, wft_ref, dcum_ref, f_ref, x1_ref, dx2_ref,
             g_ref, dx1_ref, dx1b_ref, df_ref, duq_ref, duk_ref, duv_ref, dug_ref, gn_ref, gbf_ref,
             rcar_ref):
        du_refs = (duq_ref, duk_ref, duv_ref, dug_ref)


        @pl.when(pl.program_id(0) == 0)
        def _():
            rcar_ref[...] = jnp.zeros_like(rcar_ref)
            gn_ref[...] = jnp.zeros_like(gn_ref)
            gbf_ref[...] = jnp.zeros_like(gbf_ref)

        rsum = _cumsum_rows(dcum_ref[...], reverse=True) + rcar_ref[0:1, :]
        df = rsum * _sigmoid(-f_ref[...])
        dfb = df.astype(BF16)
        dh = _dot_nt(dfb, wft_ref[...])
        for j, ref in enumerate((dq_ref, dk_ref, dv_ref, dg_ref)):
            du = _heads_from_padded(ref[...])
            du_refs[j][...] = du
            if j == 0:
                du = du * QK_SCALE
            dh = dh + _dot_nt(du, wt_ref[:, j * width:(j + 1) * width])
        dxn, dgn = _norm_bwd(x1_ref[...], g_ref[...], dh)
        dx1 = dx2_ref[...] + dxn
        dx1_ref[...] = dx1
        dx1b_ref[...] = dx1.astype(BF16)
        df_ref[...] = dfb
        gn_ref[...] += dgn
        gbf_ref[...] += jnp.sum(df, axis=0, keepdims=True)
        rcar_ref[...] = rsum[0:SUBLANES, :]

    rev = lambda i: (nt - 1 - i, 0)
    wide = pl.BlockSpec((ts, FOX_PAD), rev)
    return pl.pallas_call(
        body, name="fox_in_bwd", grid=(nt,),
        in_specs=[wide, wide, wide, wide,
                  _const_spec((D_MODEL, FOX_IN_COLS)),
                  _const_spec((D_MODEL, LANES)),
                  pl.BlockSpec((ts, LANES), rev),
                  pl.BlockSpec((ts, LANES), rev),
                  pl.BlockSpec((ts, D_MODEL), rev),
                  pl.BlockSpec((ts, D_MODEL), rev),
                  _const_spec((1, D_MODEL))],
        out_specs=[pl.BlockSpec((ts, D_MODEL), rev),
                   pl.BlockSpec((ts, D_MODEL), rev),
                   pl.BlockSpec((ts, LANES), rev)]
        + [pl.BlockSpec((ts, width), rev)] * 4
        + [pl.BlockSpec((1, D_MODEL), lambda i: (0, 0)),
           pl.BlockSpec((1, LANES), lambda i: (0, 0))],
        out_shape=[jax.ShapeDtypeStruct((s, D_MODEL), F32),
                   jax.ShapeDtypeStruct((s, D_MODEL), BF16),
                   jax.ShapeDtypeStruct((s, LANES), BF16)]
        + [jax.ShapeDtypeStruct((s, width), BF16)] * 4
        + [jax.ShapeDtypeStruct((1, D_MODEL), F32),
                   jax.ShapeDtypeStruct((1, LANES), F32)],
        scratch_shapes=[pltpu.VMEM((SUBLANES, LANES), F32)],
        compiler_params=_params(),
    )(dq, dk, dv, dg, wt, wft, dcum, f, x1, dx2, g1)


def _lru_core_bwd(dx1b, w_out, xb, gate, hs, cw, cb, wa, ba, wx, bx, a_param, wa_t, wx_t,
                  chip_sums, ts):
    s = xb.shape[0]
    nt = s // ts
    tpb = ts // SUBLANES
    n_ex = len(chip_sums)

    def body(*refs):
        (dx_ref, wo_ref, xb_ref, xbh_ref, gate_ref, hs_ref, hsh_ref, cw_ref, cb_ref, wa_ref,
         ba_ref, wx_ref, bx_ref, ap_ref, wat_ref, wxt_ref) = refs[:16]
        sum_refs = refs[16:16 + n_ex]
        du_ref, gwa_ref, gwx_ref, gvec_ref = refs[16 + n_ex:20 + n_ex]
        got_refs = refs[20 + n_ex:20 + 2 * n_ex]
        acar_ref, dhcar_ref, dxccar_ref = refs[20 + 2 * n_ex:23 + 2 * n_ex]
        start, finish = _chip_exchange_phases(sum_refs, got_refs, *refs[23 + 2 * n_ex:])
        step = pl.program_id(0)
        pl.when(step == 0)(start)

        @pl.when(step == 0)
        def _():
            acar_ref[...] = jnp.zeros_like(acar_ref)
            dhcar_ref[...] = jnp.zeros_like(dhcar_ref)
            dxccar_ref[...] = jnp.zeros_like(dxccar_ref)
            gwa_ref[...] = jnp.zeros_like(gwa_ref)
            gwx_ref[...] = jnp.zeros_like(gwx_ref)
            gvec_ref[...] = jnp.zeros_like(gvec_ref)

        first_tile = step == nt - 1
        halo_on = jnp.where(first_tile, 0.0, 1.0)
        prev8 = xbh_ref[...] * halo_on
        hprev_row = hsh_ref[SUBLANES - 1:SUBLANES, :] * halo_on

        xbv = xb_ref[...]
        taps = _conv_taps(xbv, prev8)
        cw_v = cw_ref[...]
        xc, xcb, r, i, sp, a, mult = _lru_pre(taps, cw_v, cb_ref[...], wa_ref, ba_ref[...],
                                              wx_ref, bx_ref[...], ap_ref[...])
        hs = hs_ref[...]
        gv = gate_ref[...]
        sg = _sigmoid(gv)
        dy = _dot_nt(dx_ref[...], wo_ref[...])
        dhs = dy * (gv * sg)
        dgate = dy * hs * (sg * (1.0 + gv * (1.0 - sg)))

        rows = lax.broadcasted_iota(jnp.int32, a.shape, 0)
        a_next = jnp.where(rows < ts - 1, pltpu.roll(a, ts - 1, 0), acar_ref[0:1, :])
        cum_a, dh_loc = _scan_rows(a_next, dhs, reverse=True)
        dh = cum_a * dhcar_ref[0:1, :] + dh_loc
        h_prev = jnp.where(rows >= 1, pltpu.roll(hs, 1, 0), hprev_row)

        da = dh * h_prev
        ixc = i * xc
        dmult = dh * ixc
        di = dh * mult * xc
        dxc = dh * mult * i
        dlog_a = da * a - dmult * (a * a) / mult
        dr = dlog_a * ((-LRU_C) * sp)
        dsp = jnp.sum(dlog_a * ((-LRU_C) * r), axis=0, keepdims=True)
        dra = dr * r * (1.0 - r)
        dia = di * i * (1.0 - i)
        drab = dra.astype(BF16)
        diab = dia.astype(BF16)
        back = []
        for n in range(LRU_BLOCKS):
            sl = slice(n * LRU_BLOCK_W, (n + 1) * LRU_BLOCK_W)
            gwa_ref[n] += _dot_tn(xcb[:, sl], drab[:, sl])
            gwx_ref[n] += _dot_tn(xcb[:, sl], diab[:, sl])
            back.append(_dot(drab[:, sl], wat_ref[n]) + _dot(diab[:, sl], wxt_ref[n]))
        dxc = dxc + jnp.concatenate(back, axis=1)

        nxt8 = dxccar_ref[...]
        rows8 = lax.broadcasted_iota(jnp.int32, nxt8.shape, 0)
        dxb = cw_v[3:4] * dxc
        for j in range(1, CONV_WIDTH):
            rj = pltpu.roll(dxc, ts - j, 0)
            pj = pltpu.roll(nxt8, SUBLANES - j, 0)
            tail = jnp.where(rows8 >= SUBLANES - j, pj, rj[ts - SUBLANES:])
            dxb = dxb + cw_v[3 - j:4 - j] * jnp.concatenate([rj[:ts - SUBLANES], tail], axis=0)

        du_ref[:, :LRU_WIDTH] = dxb.astype(BF16)
        du_ref[:, LRU_WIDTH:] = dgate.astype(BF16)

        z = -ap_ref[...]
        gvec = [jnp.sum(dxc * taps[3 - k], axis=0, keepdims=True) for k in range(CONV_WIDTH)]
        gvec.append(jnp.sum(dxc, axis=0, keepdims=True))
        gvec.append(jnp.sum(dra, axis=0, keepdims=True))
        gvec.append(jnp.sum(dia, axis=0, keepdims=True))
        gvec.append(-dsp * _sigmoid(z))
        gvec_ref[...] += jnp.concatenate(gvec, axis=0)

        acar_ref[...] = a[0:SUBLANES, :]
        dhcar_ref[...] = dh[0:SUBLANES, :]
        dxccar_ref[...] = dxc[0:SUBLANES, :]
        pl.when(step == nt - 1)(finish)

    rev = lambda i: (nt - 1 - i, 0)
    halo = lambda i: (jnp.maximum((nt - 1 - i) * tpb - 1, 0), 0)
    tile = pl.BlockSpec((ts, LRU_WIDTH), rev)
    halo_spec = pl.BlockSpec((SUBLANES, LRU_WIDTH), halo)
    vec = _const_spec((1, LRU_WIDTH))
    blk = _const_spec((LRU_BLOCKS, LRU_BLOCK_W, LRU_BLOCK_W))
    acc_blk = pl.BlockSpec((LRU_BLOCKS, LRU_BLOCK_W, LRU_BLOCK_W), lambda i: (0, 0, 0))
    hbm = pl.BlockSpec(memory_space=pl.ANY)
    res = pl.pallas_call(
        body, name="lru_core_bwd", grid=(nt,),
        in_specs=[pl.BlockSpec((ts, D_MODEL), rev),
                  _const_spec((LRU_WIDTH, D_MODEL)),
                  tile, halo_spec, tile, tile, halo_spec,
                  _const_spec((CONV_WIDTH, LRU_WIDTH)), vec, blk, vec, blk, vec, vec, blk, blk]
        + [hbm] * n_ex,
        out_specs=[pl.BlockSpec((ts, 2 * LRU_WIDTH), rev), acc_blk, acc_blk,
                   pl.BlockSpec((SUBLANES, LRU_WIDTH), lambda i: (0, 0))] + [hbm] * n_ex,
        out_shape=[jax.ShapeDtypeStruct((s, 2 * LRU_WIDTH), BF16),
                   jax.ShapeDtypeStruct((LRU_BLOCKS, LRU_BLOCK_W, LRU_BLOCK_W), F32),
                   jax.ShapeDtypeStruct((LRU_BLOCKS, LRU_BLOCK_W, LRU_BLOCK_W), F32),
                   jax.ShapeDtypeStruct((SUBLANES, LRU_WIDTH), F32)]
        + [jax.ShapeDtypeStruct(a.shape, a.dtype) for a in chip_sums],
        scratch_shapes=[pltpu.VMEM((SUBLANES, LRU_WIDTH), F32),
                        pltpu.VMEM((SUBLANES, LRU_WIDTH), F32),
                        pltpu.VMEM((SUBLANES, LRU_WIDTH), F32)] + _chip_exchange_sems(n_ex),
        compiler_params=_params(),
    )(dx1b, w_out, xb, xb, gate, hs, hs, cw, cb, wa, ba, wx, bx, a_param, wa_t, wx_t, *chip_sums)
    return res[0], res[1], res[2], res[3], res[4:]


def _lru_in_bwd(du, w_in, x, dx1, g0, chip_sums, ts):
    s = x.shape[0]
    nt = s // ts
    n = len(chip_sums)

    def body(*refs):
        du_ref, w_ref, x_ref, dx1_ref, g_ref = refs[:5]
        sum_refs = refs[5:5 + n]
        gx_ref, gn_ref = refs[5 + n:7 + n]
        got_refs = refs[7 + n:7 + 2 * n]
        wfull_ref = refs[7 + 2 * n]
        start, finish = _chip_exchange_phases(sum_refs, got_refs, *refs[8 + 2 * n:])
        step = pl.program_id(0)
        pl.when(step == 0)(start)

        @pl.when(step == 0)
        def _():
            gn_ref[...] = jnp.zeros_like(gn_ref)
            for j in range(N_DEV):
                wfull_ref[:, j * LRU_IN_SHARD:(j + 1) * LRU_IN_SHARD] = w_ref[j]

        dh = _dot_nt(du_ref[...], wfull_ref[...])
        dxn, dgn = _norm_bwd(x_ref[...], g_ref[...], dh)
        gx_ref[...] = dx1_ref[...] + dxn
        gn_ref[...] += dgn
        pl.when(step == nt - 1)(finish)

    tile = pl.BlockSpec((ts, D_MODEL), lambda i: (i, 0))
    hbm = pl.BlockSpec(memory_space=pl.ANY)
    res = pl.pallas_call(
        body, name="lru_in_bwd", grid=(nt,),
        in_specs=[pl.BlockSpec((ts, 2 * LRU_WIDTH), lambda i: (i, 0)),
                  _const_spec((N_DEV, D_MODEL, LRU_IN_SHARD)), tile, tile,
                  _const_spec((1, D_MODEL))] + [hbm] * n,
        out_specs=[tile, pl.BlockSpec((1, D_MODEL), lambda i: (0, 0))] + [hbm] * n,
        out_shape=[jax.ShapeDtypeStruct((s, D_MODEL), F32),
                   jax.ShapeDtypeStruct((1, D_MODEL), F32)]
        + [jax.ShapeDtypeStruct(a.shape, a.dtype) for a in chip_sums],
        scratch_shapes=[pltpu.VMEM((D_MODEL, 2 * LRU_WIDTH), BF16)] + _chip_exchange_sems(n),
        compiler_params=_params(),
    )(du, w_in, x, dx1, g0, *chip_sums)
    return res[0], res[1], res[2:]


def _weight_grad(a, b, ts, name, scale=1.0, col_shards=1):
    s, ka = a.shape
    nb = b.shape[1]
    nt = s // ts
    per = nb // col_shards

    def body(a_ref, b_ref, o_ref):
        @pl.when(pl.program_id(0) == 0)
        def _():
            o_ref[...] = jnp.zeros_like(o_ref)

        if col_shards == 1:
            o_ref[...] += _dot_tn(a_ref[...], b_ref[...])
        else:
            acc = _dot_tn(a_ref[...], b_ref[...])
            for j in range(col_shards):
                o_ref[j] += acc[:, j * per:(j + 1) * per]
        if scale != 1.0:
            @pl.when(pl.program_id(0) == nt - 1)
            def _():
                o_ref[...] = o_ref[...] * scale

    out_dims = (ka, nb) if col_shards == 1 else (col_shards, ka, per)
    return pl.pallas_call(
        body, name=name, grid=(nt,),
        in_specs=[pl.BlockSpec((ts, ka), lambda i: (i, 0)),
                  pl.BlockSpec((ts, nb), lambda i: (i, 0))],
        out_specs=pl.BlockSpec(out_dims, lambda i: (0,) * len(out_dims)),
        out_shape=jax.ShapeDtypeStruct(out_dims, F32),
        compiler_params=_params(),
    )(a, b)


def _sum_parts(gp_ref):
    g = gp_ref[0].astype(F32)
    for k in range(1, gp_ref.shape[0]):
        g = g + gp_ref[k].astype(F32)
    return g


def _adamw(g_parts, w, m, v, tr, name):
    nparts, rows, cols = g_parts.shape

    def body(gp_ref, w_ref, m_ref, v_ref, g_ref, d_ref, mo_ref, vo_ref):
        g = _sum_parts(gp_ref)
        m2 = ADAM_B1 * m_ref[...] + (1.0 - ADAM_B1) * g
        v2 = ADAM_B2 * v_ref[...] + (1.0 - ADAM_B2) * (g * g)
        m_hat = m2 / (1.0 - ADAM_B1 ** ADAM_STEP)
        v_hat = v2 / (1.0 - ADAM_B2 ** ADAM_STEP)
        g_ref[...] = g
        d_ref[...] = (-ADAM_LR) * (m_hat / (jnp.sqrt(v_hat) + ADAM_EPS) + ADAM_WD * w_ref[...])
        mo_ref[...] = m2
        vo_ref[...] = v2

    tile = pl.BlockSpec((tr, cols), lambda i: (i, 0))
    out = jax.ShapeDtypeStruct((rows, cols), F32)
    return pl.pallas_call(
        body, name=name, grid=(rows // tr,),
        in_specs=[pl.BlockSpec((nparts, tr, cols), lambda i: (0, i, 0)), tile, tile, tile],
        out_specs=[tile, tile, tile, tile],
        out_shape=[out, out, out, out],
        compiler_params=_params(),
    )(g_parts, w, m, v)


def _reduce_parts(g_parts, name):
    _, rows, cols = g_parts.shape

    def body(gp_ref, g_ref):
        g_ref[...] = _sum_parts(gp_ref)

    return pl.pallas_call(
        body, name=name,
        out_shape=jax.ShapeDtypeStruct((rows, cols), F32),
        compiler_params=pltpu.CompilerParams(vmem_limit_bytes=VMEM_LIMIT_BYTES),
    )(g_parts)


def _mesh_pos():
    ix, iy, ic = lax.axis_index("x"), lax.axis_index("y"), lax.axis_index("c")
    return ix, iy, ic


def _peer(ix, iy, ic, mask):
    px = 1 - ix if mask & 4 else ix
    py = 1 - iy if mask & 2 else iy
    pc = 1 - ic if mask & 1 else ic
    return (px, py, pc), 4 * px + 2 * py + pc


def _exchange(arrays, scatter, name):
    n = len(arrays)

    def body(*refs):
        x_refs, o_refs = refs[:n], refs[n:2 * n]
        send_sems, recv_sems, local_sems = refs[2 * n:]
        ix, iy, ic = _mesh_pos()
        me = 4 * ix + 2 * iy + ic

        def src(a, dest):
            return x_refs[a].at[dest] if scatter else x_refs[a]

        local = [pltpu.make_async_copy(src(a, me), o_refs[a].at[me], local_sems.at[a])
                 for a in range(n)]
        for cp in local:
            cp.start()
        sends = []
        for mask in range(1, N_DEV):
            peer, pidx = _peer(ix, iy, ic, mask)
            for a in range(n):
                cp = pltpu.make_async_remote_copy(
                    src_ref=src(a, pidx), dst_ref=o_refs[a].at[me],
                    send_sem=send_sems.at[a, mask - 1], recv_sem=recv_sems.at[a, mask - 1],
                    device_id=peer, device_id_type=pl.DeviceIdType.MESH)
                cp.start()
                sends.append(cp)
        for mask in range(1, N_DEV):
            peer, pidx = _peer(ix, iy, ic, mask)
            for a in range(n):
                pltpu.make_async_remote_copy(
                    src_ref=src(a, me), dst_ref=o_refs[a].at[pidx],
                    send_sem=send_sems.at[a, mask - 1], recv_sem=recv_sems.at[a, mask - 1],
                    device_id=peer, device_id_type=pl.DeviceIdType.MESH).wait_recv()
        for cp in sends:
            cp.wait_send()
        for cp in local:
            cp.wait()

    out_shape = [jax.ShapeDtypeStruct(x.shape if scatter else (N_DEV,) + x.shape, x.dtype)
                 for x in arrays]
    return pl.pallas_call(
        body, name=name,
        in_specs=[pl.BlockSpec(memory_space=pl.ANY)] * n,
        out_specs=[pl.BlockSpec(memory_space=pl.ANY)] * n,
        out_shape=out_shape,
        scratch_shapes=[pltpu.SemaphoreType.DMA((n, N_DEV - 1)),
                        pltpu.SemaphoreType.DMA((n, N_DEV - 1)),
                        pltpu.SemaphoreType.DMA((n,))],
    )(*arrays)


def _gather_two_level(arrays, name):
    n = len(arrays)

    def body(*refs):
        start, forward, finish = _gather_phases(refs[:n], refs[n:2 * n], *refs[2 * n:])
        start()
        forward()
        finish()

    return pl.pallas_call(
        body, name=name,
        in_specs=[pl.BlockSpec(memory_space=pl.ANY)] * n,
        out_specs=[pl.BlockSpec(memory_space=pl.ANY)] * n,
        out_shape=[jax.ShapeDtypeStruct((N_DEV,) + x.shape, x.dtype) for x in arrays],
        scratch_shapes=_gather_sems(n),
    )(*arrays)


def _gather_sems(n):
    return [pltpu.SemaphoreType.DMA((n, N_DEV - 1)), pltpu.SemaphoreType.DMA((n, N_DEV - 1)),
            pltpu.SemaphoreType.DMA((n,))]


def _gather_phases(x_refs, o_refs, send_sems, recv_sems, local_sems):
    n = len(x_refs)
    ix, iy, ic = _mesh_pos()
    me, sibling = (ix, iy, ic), (ix, iy, 1 - ic)
    chips = [(1 - ix, iy), (ix, 1 - iy), (1 - ix, 1 - iy)]

    def idx(px, py, pc):
        return 4 * px + 2 * py + pc

    def copy(a, k, block, to, src=None):
        dst = o_refs[a].at[idx(*block)]
        return pltpu.make_async_remote_copy(
            src_ref=dst if src is None else src, dst_ref=dst,
            send_sem=send_sems.at[a, k], recv_sem=recv_sems.at[a, k],
            device_id=to, device_id_type=pl.DeviceIdType.MESH)

    def local():
        return [pltpu.make_async_copy(x_refs[a], o_refs[a].at[idx(*me)], local_sems.at[a])
                for a in range(n)]

    def first():
        out = []
        for a in range(n):
            out.append(copy(a, 0, me, sibling, src=x_refs[a]))
            out += [copy(a, 1 + j, me, (*chip, ic), src=x_refs[a])
                    for j, chip in enumerate(chips)]
        return out

    def passed():
        return [copy(a, 4 + j, (*chip, ic), sibling)
                for j, chip in enumerate(chips) for a in range(n)]

    def start():
        for cp in local() + first():
            cp.start()

    def forward():
        for j, chip in enumerate(chips):
            for a in range(n):
                copy(a, 1 + j, (*chip, ic), me).wait_recv()
                copy(a, 4 + j, (*chip, ic), sibling).start()

    def finish():
        for a in range(n):
            copy(a, 0, sibling, me).wait_recv()
            for j, chip in enumerate(chips):
                copy(a, 4 + j, (*chip, 1 - ic), me).wait_recv()
        for cp in first() + passed():
            cp.wait_send()
        for cp in local():
            cp.wait()

    return start, forward, finish


def _swap_sibling(arrays, name):
    n = len(arrays)
    n_chips = N_DEV // 2

    def body(*refs):
        x_refs, got_refs = refs[:n], refs[n:2 * n]
        send_sems, recv_sems = refs[2 * n:]
        ix, iy, ic = _mesh_pos()
        sibling = (ix, iy, 1 - ic)
        sends = []
        for a in range(n):
            for q in range(n_chips):
                cp = pltpu.make_async_remote_copy(
                    src_ref=x_refs[a].at[q, 1 - ic], dst_ref=got_refs[a].at[q],
                    send_sem=send_sems.at[a, q], recv_sem=recv_sems.at[a, q],
                    device_id=sibling, device_id_type=pl.DeviceIdType.MESH)
                cp.start()
                sends.append(cp)
        for cp in sends:
            cp.wait()

    return pl.pallas_call(
        body, name=name,
        in_specs=[pl.BlockSpec(memory_space=pl.ANY)] * n,
        out_specs=[pl.BlockSpec(memory_space=pl.ANY)] * n,
        out_shape=[jax.ShapeDtypeStruct((n_chips,) + x.shape[2:], x.dtype) for x in arrays],
        scratch_shapes=[pltpu.SemaphoreType.DMA((n, n_chips)),
                        pltpu.SemaphoreType.DMA((n, n_chips))],
    )(*arrays)


def _exchange_chips(arrays, name):
    n = len(arrays)

    def body(*refs):
        start, finish = _chip_exchange_phases(refs[:n], refs[n:2 * n], *refs[2 * n:])
        start()
        finish()

    return pl.pallas_call(
        body, name=name,
        in_specs=[pl.BlockSpec(memory_space=pl.ANY)] * n,
        out_specs=[pl.BlockSpec(memory_space=pl.ANY)] * n,
        out_shape=[jax.ShapeDtypeStruct(x.shape, x.dtype) for x in arrays],
        scratch_shapes=_chip_exchange_sems(n),
    )(*arrays)


def _chip_exchange_sems(n):
    n_chips = N_DEV // 2
    return [pltpu.SemaphoreType.DMA((n, n_chips - 1)), pltpu.SemaphoreType.DMA((n, n_chips - 1)),
            pltpu.SemaphoreType.DMA((n,))]


def _chip_exchange_phases(x_refs, o_refs, send_sems, recv_sems, local_sems):
    n = len(x_refs)
    n_chips = N_DEV // 2
    ix, iy, ic = _mesh_pos()
    my_chip = 2 * ix + iy

    def peers():
        for mask in range(1, n_chips):
            px = 1 - ix if mask & 2 else ix
            py = 1 - iy if mask & 1 else iy
            yield mask, (px, py, ic), 2 * px + py

    def local():
        return [pltpu.make_async_copy(x_refs[a].at[my_chip], o_refs[a].at[my_chip],
                                      local_sems.at[a]) for a in range(n)]

    def sends():
        return [pltpu.make_async_remote_copy(
            src_ref=x_refs[a].at[chip], dst_ref=o_refs[a].at[my_chip],
            send_sem=send_sems.at[a, mask - 1], recv_sem=recv_sems.at[a, mask - 1],
            device_id=peer, device_id_type=pl.DeviceIdType.MESH)
            for mask, peer, chip in peers() for a in range(n)]

    def start():
        for cp in local() + sends():
            cp.start()

    def finish():
        for mask, peer, chip in peers():
            for a in range(n):
                pltpu.make_async_remote_copy(
                    src_ref=x_refs[a].at[my_chip], dst_ref=o_refs[a].at[chip],
                    send_sem=send_sems.at[a, mask - 1], recv_sem=recv_sems.at[a, mask - 1],
                    device_id=peer, device_id_type=pl.DeviceIdType.MESH).wait_recv()
        for cp in sends():
            cp.wait_send()
        for cp in local():
            cp.wait()

    return start, finish


def _pair_sum(core, x, got, name):
    nq, rows, cols = got.shape

    def body(c_ref, x_ref, g_ref, o_ref):
        o_ref[...] = (x_ref[...] + g_ref[...]).astype(BF16)

    blk = pl.BlockSpec((None, rows, cols), lambda q, c: (q, 0, 0))
    return pl.pallas_call(
        body, name=name,
        grid_spec=pltpu.PrefetchScalarGridSpec(
            num_scalar_prefetch=1, grid=(nq,),
            in_specs=[pl.BlockSpec((None, None, rows, cols), lambda q, c: (q, c[0], 0, 0)), blk],
            out_specs=blk),
        out_shape=jax.ShapeDtypeStruct(got.shape, BF16),
        compiler_params=_params(),
    )(core, x, got)


def _selectors():
    r = lax.broadcasted_iota(jnp.int32, (LANES, FOX_PAD), 0)
    c = lax.broadcasted_iota(jnp.int32, (LANES, FOX_PAD), 1)
    part, head_r = r // HEADS, r % HEADS
    head_c, lane_c = c // HEAD_PAD, c % HEAD_PAD
    same = (head_r == head_c) & (part < 3)
    sel_q = jnp.where(same & (lane_c == LANE_RB + part), 1.0, 0.0)
    sel_k = jnp.where(same & (lane_c == LANE_CK + part), -1.0, 0.0)
    sel = jnp.stack([sel_q, sel_k, jnp.zeros_like(sel_q)]).astype(BF16)
    lane = lax.broadcasted_iota(jnp.int32, (1, FOX_PAD), 1) % HEAD_PAD
    ones_q = jnp.where((lane >= LANE_CK) & (lane < LANE_CK + 3), 1.0, 0.0)
    ones_k = jnp.where(((lane >= LANE_RB) & (lane < LANE_RB + 3))
                       | ((lane >= LANE_LSE) & (lane < LANE_LSE + 3)), 1.0, 0.0)
    ones_v = jnp.where((lane >= LANE_ONE_V) & (lane < LANE_ONE_V + 2), 1.0, 0.0)
    bias = jnp.stack([ones_q, ones_k, ones_v]).astype(F32)
    return sel, bias


def _chip_sums(names, send):
    send = [a.reshape((N_DEV // 2, 2) + a.shape[1:]) for a in send]
    got = _swap_sibling(send, "swap_" + names[0])
    core = lax.axis_index("c").astype(jnp.int32).reshape(1)
    return [_pair_sum(core, a, b, "pair_sum_" + n) for n, a, b in zip(names, send, got)]


def _local_step(x, target, norm_g, final_g, w_in8, conv_w, conv_b, wa, ba, wx, bx, a_param,
                w_out_b, fox_in_shard, b_f, fox_out_shard, blk=512, ts=256):
    g0, g1 = norm_g[0:1], norm_g[1:2]
    gf = final_g.reshape(1, D_MODEL)
    wa_b, wx_b = wa.astype(BF16), wx.astype(BF16)
    sel, bias = _selectors()

    xb, gate1, h0, (fox_in8, fox_out8) = _lru_in_fwd(x, g0, w_in8, [fox_in_shard, fox_out_shard],
                                                     ts)
    fox_w_in = jnp.transpose(fox_in8, (1, 0, 2)).reshape(D_MODEL, FOX_IN_COLS)
    width = HEADS * HEAD_DIM
    wf_b = jnp.pad(fox_w_in[:, 4 * width:], ((0, 0), (0, LANES - HEADS)))
    bf_pad = jnp.pad(b_f, ((0, 0), (0, LANES - HEADS)))
    fo_b = fox_out8.reshape(width, D_MODEL)
    y1, hs = _lru_core_fwd(xb, gate1, conv_w, conv_b, wa_b, ba, wx_b, bx, a_param, ts)
    x1, h1, f, cparts = _fox_pre_fwd(x, y1, w_out_b, g1, wf_b, bf_pad, ts)
    qkv = _fox_proj_fwd(h1, cparts, fox_w_in, 0, 3, sel, bias, BF16, ts, "fox_proj_qkv")
    gate2 = _fox_proj_fwd(h1, None, fox_w_in, 3, 1, None, None, F32, ts, "fox_proj_gate")[0]
    o, qb = _attn_fwd(qkv, blk, hps=4)
    dx2, dx2b, y2, loss_acc, g_final = _fox_out_loss(o, gate2, fo_b, x1, target, gf, ts)

    do, dgate2 = _fox_out_bwd(dx2b, fo_b, o, gate2, ts)
    dq, dk, dv, dcum = _attn_bwd(qb, qkv, do, blk)
    dx1, dx1b, df, du_q, du_k, du_v, du_g, g_norm1, g_bf = _fox_in_bwd(
        dq, dk, dv, dgate2, fox_w_in, wf_b, dcum, f, x1, dx2, g1, ts)
    tw = 512
    g_q = _weight_grad(h1, du_q, tw, "grad_fox_wq", scale=QK_SCALE)
    g_k = _weight_grad(h1, du_k, tw, "grad_fox_wk")
    g_v = _weight_grad(h1, du_v, tw, "grad_fox_wv")
    g_g = _weight_grad(h1, du_g, tw, "grad_fox_wg")
    g_f = _weight_grad(h1, df, tw, "grad_fox_wf")
    g_fox_w_in = jnp.concatenate([g_q, g_k, g_v, g_g, g_f[:, :HEADS]], axis=1)
    g_fox_w_in = jnp.transpose(g_fox_w_in.reshape(D_MODEL, N_DEV, FOX_IN_SHARD), (1, 0, 2))
    g_fox_w_out = _weight_grad(y2, dx2b, tw, "grad_fox_w_out")
    fox_sums = _chip_sums(("fox_w_in", "fox_w_out"),
                          [g_fox_w_in, g_fox_w_out.reshape(N_DEV, -1, D_MODEL)])

    du, g_wa, g_wx, g_vec, (r_fox_in, r_fox_out) = _lru_core_bwd(
        dx1b, w_out_b, xb, gate1, hs, conv_w, conv_b, wa_b, ba, wx_b, bx, a_param,
        jnp.transpose(wa_b, (0, 2, 1)), jnp.transpose(wx_b, (0, 2, 1)), fox_sums, ts)
    g_lru_w_in = _weight_grad(h0, du, tw, "grad_lru_w_in", col_shards=N_DEV)
    g_lru_w_out = _weight_grad(y1, dx1b, tw, "grad_lru_w_out")
    conv_send = jnp.transpose(g_vec[0:CONV_WIDTH].reshape(CONV_WIDTH, N_DEV, -1), (1, 0, 2))
    lru_sums = _chip_sums(("lru_w_in", "lru_conv_w", "lru_w_out"),
                          [g_lru_w_in, conv_send, g_lru_w_out.reshape(N_DEV, -1, D_MODEL)])
    grad_x, g_norm0, (r_w_in, r_conv, r_w_out) = _lru_in_bwd(du, w_in8, x, dx1, g0, lru_sums, ts)

    small = dict(
        norm_g=jnp.concatenate([g_norm0, g_norm1], axis=0), final_g=g_final[0],
        lru_conv_b=g_vec[4:5], lru_wa=g_wa, lru_ba=g_vec[5:6], lru_wx=g_wx, lru_bx=g_vec[6:7],
        lru_a_param=g_vec[7:8], fox_b_f=g_bf[:, :HEADS])
    received = dict(lru_w_in=r_w_in, lru_conv_w=r_conv, lru_w_out=r_w_out, fox_w_in=r_fox_in,
                    fox_w_out=r_fox_out)
    return loss_acc[0, 0], grad_x, small, received


SMALL =("norm_g", "final_g", "lru_conv_b", "lru_wa", "lru_ba", "lru_wx", "lru_bx", "lru_a_param",
         "fox_b_f")
ALL_WEIGHTS = ("norm_g", "final_g", "lru_w_in", "lru_conv_w", "lru_conv_b", "lru_wa", "lru_ba",
               "lru_wx", "lru_bx", "lru_a_param", "lru_w_out", "fox_w_in", "fox_b_f", "fox_w_out")


def _pack_small(d):
    rows = []
    for n in SMALL:
        a = d[n].reshape(-1)
        if a.shape[0] % LANES:
            a = jnp.pad(a, (0, LANES - a.shape[0] % LANES))
        rows.append(a.reshape(-1, LANES))
    packed = jnp.concatenate(rows, axis=0)
    return jnp.pad(packed, ((0, N_DEV * SMALL_CHUNK_ROWS - packed.shape[0]), (0, 0)))


def _unpack_small(packed, like):
    out, off = {}, 0
    for n, nrows in zip(SMALL, SMALL_ROWS):
        size = like[n].size
        out[n] = packed[off:off + nrows].reshape(-1)[:size].reshape(like[n].shape)
        off += nrows
    return out


def kernel(x, norm_g, final_g, lru_w_in, lru_conv_w, lru_conv_b, lru_wa, lru_ba, lru_wx, lru_bx, lru_a_param, lru_w_out, fox_w_in, fox_b_f, fox_w_out, loss_target, m_norm_g, m_final_g, m_lru_w_in, m_lru_conv_w, m_lru_conv_b, m_lru_wa, m_lru_ba, m_lru_wx, m_lru_bx, m_lru_a_param, m_lru_w_out, m_fox_w_in, m_fox_b_f, m_fox_w_out, v_norm_g, v_final_g, v_lru_w_in, v_lru_conv_w, v_lru_conv_b, v_lru_wa, v_lru_ba, v_lru_wx, v_lru_bx, v_lru_a_param, v_lru_w_out, v_fox_w_in, v_fox_b_f, v_fox_w_out):
    w_loc = dict(norm_g=norm_g, final_g=final_g, lru_w_in=lru_w_in, lru_conv_w=lru_conv_w,
                 lru_conv_b=lru_conv_b, lru_wa=lru_wa, lru_ba=lru_ba, lru_wx=lru_wx, lru_bx=lru_bx,
                 lru_a_param=lru_a_param, lru_w_out=lru_w_out, fox_w_in=fox_w_in, fox_b_f=fox_b_f,
                 fox_w_out=fox_w_out)
    m_loc = dict(norm_g=m_norm_g, final_g=m_final_g, lru_w_in=m_lru_w_in, lru_conv_w=m_lru_conv_w,
                 lru_conv_b=m_lru_conv_b, lru_wa=m_lru_wa, lru_ba=m_lru_ba, lru_wx=m_lru_wx,
                 lru_bx=m_lru_bx, lru_a_param=m_lru_a_param, lru_w_out=m_lru_w_out,
                 fox_w_in=m_fox_w_in, fox_b_f=m_fox_b_f, fox_w_out=m_fox_w_out)
    v_loc = dict(norm_g=v_norm_g, final_g=v_final_g, lru_w_in=v_lru_w_in, lru_conv_w=v_lru_conv_w,
                 lru_conv_b=v_lru_conv_b, lru_wa=v_lru_wa, lru_ba=v_lru_ba, lru_wx=v_lru_wx,
                 lru_bx=v_lru_bx, lru_a_param=v_lru_a_param, lru_w_out=v_lru_w_out,
                 fox_w_in=v_fox_w_in, fox_b_f=v_fox_b_f, fox_w_out=v_fox_w_out)

    w_in8, conv8, w_out8 = _gather_two_level(
        [lru_w_in[0].astype(BF16), lru_conv_w[0], lru_w_out[0].astype(BF16)], "gather_weights")
    conv_full = jnp.transpose(conv8, (1, 0, 2)).reshape(CONV_WIDTH, LRU_WIDTH)

    loss, grad_x, small_grads, received = _local_step(
        x[0], loss_target[0], norm_g, final_g, w_in8, conv_full, lru_conv_b, lru_wa[0], lru_ba,
        lru_wx[0], lru_bx, lru_a_param, w_out8.reshape(LRU_WIDTH, D_MODEL),
        fox_w_in[0].astype(BF16), fox_b_f, fox_w_out[0].astype(BF16))

    out = {}
    for n, tr in (("lru_w_in", 256), ("lru_conv_w", CONV_WIDTH), ("lru_w_out", 96),
                  ("fox_w_in", 128), ("fox_w_out", 64)):
        res = _adamw(received[n], w_loc[n][0], m_loc[n][0], v_loc[n][0], tr, "adamw_" + n)
        out[n] = [a[None] for a in res]

    small_sums = _chip_sums(
        ("small",), [_pack_small(small_grads).reshape(N_DEV, SMALL_CHUNK_ROWS, LANES)])
    r_small, = _exchange_chips(small_sums, "scatter_small_grads")

    g_chunk = _reduce_parts(r_small, "reduce_small_grads")
    g_small, = _exchange([g_chunk], False, "gather_small_grads")
    g_small = g_small.reshape(1, N_DEV * SMALL_CHUNK_ROWS, LANES)
    res = _adamw(g_small, _pack_small(w_loc), _pack_small(m_loc), _pack_small(v_loc),
                 N_DEV * SMALL_CHUNK_ROWS, "adamw_replicated")
    small_out = [_unpack_small(a, w_loc) for a in res]
    for n in SMALL:
        out[n] = [d[n] for d in small_out]

    loss = lax.psum(loss, ("x", "y", "c"))
    return (loss, grad_x[None], *[out[n][0] for n in ALL_WEIGHTS], *[out[n][1] for n in ALL_WEIGHTS],
            *[out[n][2] for n in ALL_WEIGHTS], *[out[n][3] for n in ALL_WEIGHTS])
```

```python
import functools

import jax
import jax.numpy as jnp
from jax import lax
from jax.experimental import pallas as pl
from jax.experimental.pallas import tpu as pltpu

F32 = jnp.float32
BF16 = jnp.bfloat16

D_MODEL = 1024
LRU_WIDTH = 1536
LRU_BLOCKS = 12
LRU_BLOCK_W = 128
CONV_WIDTH = 4
LRU_C = 8.0
HEADS = 16
HEAD_DIM = 64
HEAD_PAD = 128
FOX_PAD = HEADS * HEAD_PAD
HEADS_PER_STEP = 2
QK_SCALE = 1.0 / HEAD_DIM ** 0.5
EPS = 1e-6
NEG_BIG = -1e30
N_DEV = 8

ADAM_LR = 0.001
ADAM_B1 = 0.9
ADAM_B2 = 0.999
ADAM_EPS = 1e-08
ADAM_WD = 0.01
ADAM_STEP = 10

LANE_RB = 64
LANE_CK = 67
LANE_LSE = 70
LANE_ONE_V = 64

VMEM_LIMIT_BYTES = 56 * 1024 * 1024
LANES = 128
SUBLANES = 8

LRU_IN_SHARD = 2 * LRU_WIDTH // N_DEV
FOX_IN_COLS = 4 * HEADS * HEAD_DIM + HEADS
FOX_IN_SHARD = FOX_IN_COLS // N_DEV

SMALL_ROWS = (16, 8, 12, 1536, 12, 1536, 12, 12, 1)
SMALL_CHUNK_ROWS = 400
assert sum(SMALL_ROWS) <= N_DEV * SMALL_CHUNK_ROWS


def _params(n_grid_axes=1):
    return pltpu.CompilerParams(
        dimension_semantics=("arbitrary",) * n_grid_axes,
        vmem_limit_bytes=VMEM_LIMIT_BYTES)


def _const_spec(shape):
    nd = len(shape)
    return pl.BlockSpec(shape, lambda *_: (0,) * nd, pipeline_mode=pl.Buffered(1))


def _shift_down(x, k, fill):
    rows = lax.broadcasted_iota(jnp.int32, x.shape, 0)
    return jnp.where(rows >= k, pltpu.roll(x, k, 0), fill)


def _shift_up(x, k, fill):
    n = x.shape[0]
    rows = lax.broadcasted_iota(jnp.int32, x.shape, 0)
    return jnp.where(rows < n - k, pltpu.roll(x, n - k, 0), fill)


def _scan_rows(a, b, reverse=False):
    n = a.shape[0]
    shift = _shift_up if reverse else _shift_down
    k = 1
    while k < n:
        b = a * shift(b, k, 0.0) + b
        a = a * shift(a, k, 1.0)
        k *= 2
    return a, b


def _cumsum_rows(x, reverse=False):
    n = x.shape[0]
    shift = _shift_up if reverse else _shift_down
    k = 1
    while k < n:
        x = x + shift(x, k, 0.0)
        k *= 2
    return x


def _rstd(x):
    return lax.rsqrt(jnp.mean(x * x, axis=-1, keepdims=True) + EPS)


def _norm_bwd(x, g, dh):
    rstd = _rstd(x)
    xhat = x * rstd
    dg = jnp.sum(dh * xhat, axis=0, keepdims=True)
    dxh = dh * g
    dx = rstd * (dxh - xhat * jnp.mean(dxh * xhat, axis=-1, keepdims=True))
    return dx, dg


def _split3(x):
    hi = x.astype(BF16)
    r1 = x - hi.astype(F32)
    mid = r1.astype(BF16)
    lo = (r1 - mid.astype(F32)).astype(BF16)
    return hi, mid, lo


def _sigmoid(x):
    return jax.nn.sigmoid(x)


def _dot(a, b):
    return jnp.dot(a, b, preferred_element_type=F32)


def _dot_nt(a, b):
    return lax.dot_general(a, b, (((1,), (1,)), ((), ())), preferred_element_type=F32)


def _dot_tn(a, b):
    return lax.dot_general(a, b, (((0,), (0,)), ((), ())), preferred_element_type=F32)


def _heads_to_padded(u):
    n = u.shape[0]
    low = lax.broadcasted_iota(jnp.int32, (n, LANES), 1) < HEAD_DIM
    zero = jnp.zeros((n, LANES), u.dtype)
    cols = []
    for p in range(HEADS // 2):
        pair = u[:, p * LANES:(p + 1) * LANES]
        cols.append(jnp.where(low, pair, zero))
        cols.append(jnp.where(low, pltpu.roll(pair, HEAD_DIM, 1), zero))
    return jnp.concatenate(cols, axis=1)


def _heads_from_padded(x):
    n = x.shape[0]
    low = lax.broadcasted_iota(jnp.int32, (n, LANES), 1) < HEAD_DIM
    cols = []
    for p in range(HEADS // 2):
        even = x[:, (2 * p) * HEAD_PAD:(2 * p + 1) * HEAD_PAD]
        odd = x[:, (2 * p + 1) * HEAD_PAD:(2 * p + 2) * HEAD_PAD]
        cols.append(jnp.where(low, even, pltpu.roll(odd, HEAD_DIM, 1)))
    return jnp.concatenate(cols, axis=1)


def _conv_taps(xb, prev8):
    rows8 = lax.broadcasted_iota(jnp.int32, prev8.shape, 0)
    taps = [xb]
    for j in range(1, CONV_WIDTH):
        r = pltpu.roll(xb, j, 0)
        p = pltpu.roll(prev8, j, 0)
        head = jnp.where(rows8 < j, p, r[0:SUBLANES])
        taps.append(jnp.concatenate([head, r[SUBLANES:]], axis=0))
    return taps


def _lru_pre(taps, cw, cb, wa_ref, ba, wx_ref, bx, a_param):
    xc = cb + cw[3:4] * taps[0] + cw[2:3] * taps[1] + cw[1:2] * taps[2] + cw[0:1] * taps[3]
    xcb = xc.astype(BF16)
    ra, ia = [], []
    for n in range(LRU_BLOCKS):
        blk = xcb[:, n * LRU_BLOCK_W:(n + 1) * LRU_BLOCK_W]
        ra.append(_dot(blk, wa_ref[n]))
        ia.append(_dot(blk, wx_ref[n]))
    r = _sigmoid(jnp.concatenate(ra, axis=1) + ba)
    i = _sigmoid(jnp.concatenate(ia, axis=1) + bx)
    z = -a_param
    sp = jnp.maximum(z, 0.0) + jnp.log1p(jnp.exp(-jnp.abs(z)))
    log_a = (-LRU_C) * r * sp
    a = jnp.exp(log_a)
    one_minus_a2 = -jnp.tanh(log_a) * (a * a + 1.0)
    mult = jnp.sqrt(one_minus_a2)
    return xc, xcb, r, i, sp, a, mult


def _lru_in_fwd(x, g0, w_in, later_shards, ts):
    s = x.shape[0]
    nt = s // ts
    n = len(later_shards)

    def body(*refs):
        x_ref, g_ref, w_ref = refs[:3]
        shard_refs = refs[3:3 + n]
        xb_ref, gate_ref, h_ref = refs[3 + n:6 + n]
        wfull_ref = refs[6 + 2 * n]
        start, forward, finish = _gather_phases(shard_refs, refs[6 + n:6 + 2 * n],
                                                *refs[7 + 2 * n:])
        step = pl.program_id(0)
        pl.when(step == 0)(start)

        @pl.when(step == 0)
        def _():
            for j in range(N_DEV):
                wfull_ref[:, j * LRU_IN_SHARD:(j + 1) * LRU_IN_SHARD] = w_ref[j]

        xv = x_ref[...]
        h = (xv * _rstd(xv) * g_ref[...]).astype(BF16)
        u = _dot(h, wfull_ref[...])
        xb_ref[...] = u[:, :LRU_WIDTH]
        gate_ref[...] = u[:, LRU_WIDTH:]
        h_ref[...] = h
        pl.when(step == (2 * nt) // 3)(forward)
        pl.when(step == nt - 1)(finish)

    hbm = pl.BlockSpec(memory_space=pl.ANY)
    res = pl.pallas_call(
        body, name="lru_in_fwd", grid=(nt,),
        in_specs=[pl.BlockSpec((ts, D_MODEL), lambda i: (i, 0)),
                  _const_spec((1, D_MODEL)),
                  _const_spec((N_DEV, D_MODEL, LRU_IN_SHARD))] + [hbm] * n,
        out_specs=[pl.BlockSpec((ts, LRU_WIDTH), lambda i: (i, 0)),
                   pl.BlockSpec((ts, LRU_WIDTH), lambda i: (i, 0)),
                   pl.BlockSpec((ts, D_MODEL), lambda i: (i, 0))] + [hbm] * n,
        out_shape=[jax.ShapeDtypeStruct((s, LRU_WIDTH), F32),
                   jax.ShapeDtypeStruct((s, LRU_WIDTH), F32),
                   jax.ShapeDtypeStruct((s, D_MODEL), BF16)]
        + [jax.ShapeDtypeStruct((N_DEV,) + a.shape, a.dtype) for a in later_shards],
        scratch_shapes=[pltpu.VMEM((D_MODEL, 2 * LRU_WIDTH), BF16)] + _gather_sems(n),
        compiler_params=_params(),
    )(x, g0, w_in, *later_shards)
    return res[0], res[1], res[2], res[3:]


def _lru_core_fwd(xb, gate, cw, cb, wa, ba, wx, bx, a_param, ts):
    s = xb.shape[0]

    def body(xb_ref, gate_ref, cw_ref, cb_ref, wa_ref, ba_ref, wx_ref, bx_ref, ap_ref,
             y_ref, hs_ref, prev_ref, hcar_ref):
        @pl.when(pl.program_id(0) == 0)
        def _():
            prev_ref[...] = jnp.zeros_like(prev_ref)
            hcar_ref[...] = jnp.zeros_like(hcar_ref)

        xbv = xb_ref[...]
        taps = _conv_taps(xbv, prev_ref[...])
        xc, _, _, i, _, a, mult = _lru_pre(taps, cw_ref[...], cb_ref[...], wa_ref, ba_ref[...],
                                           wx_ref, bx_ref[...], ap_ref[...])
        bterm = mult * (i * xc)
        cum_a, hloc = _scan_rows(a, bterm)
        hs = cum_a * hcar_ref[SUBLANES - 1:SUBLANES, :] + hloc
        gv = gate_ref[...]
        y_ref[...] = (hs * (gv * _sigmoid(gv))).astype(BF16)
        hs_ref[...] = hs
        prev_ref[...] = xbv[ts - SUBLANES:, :]
        hcar_ref[...] = hs[ts - SUBLANES:, :]

    vec = _const_spec((1, LRU_WIDTH))
    blk = _const_spec((LRU_BLOCKS, LRU_BLOCK_W, LRU_BLOCK_W))
    tile = pl.BlockSpec((ts, LRU_WIDTH), lambda i: (i, 0))
    return pl.pallas_call(
        body, name="lru_core_fwd", grid=(s // ts,),
        in_specs=[tile, tile, _const_spec((CONV_WIDTH, LRU_WIDTH)), vec, blk, vec, blk, vec, vec],
        out_specs=[tile, tile],
        out_shape=[jax.ShapeDtypeStruct((s, LRU_WIDTH), BF16),
                   jax.ShapeDtypeStruct((s, LRU_WIDTH), F32)],
        scratch_shapes=[pltpu.VMEM((SUBLANES, LRU_WIDTH), F32),
                        pltpu.VMEM((SUBLANES, LRU_WIDTH), F32)],
        compiler_params=_params(),
    )(xb, gate, cw, cb, wa, ba, wx, bx, a_param)


def _fox_pre_fwd(x, y, w_out, g1, wf, bf, ts):
    s = x.shape[0]

    def body(x_ref, y_ref, w_ref, g_ref, wf_ref, bf_ref, x1_ref, h1_ref, f_ref, cp_ref, ccar_ref):
        @pl.when(pl.program_id(0) == 0)
        def _():
            ccar_ref[...] = jnp.zeros_like(ccar_ref)

        x1 = x_ref[...] + _dot(y_ref[...], w_ref[...])
        h1 = (x1 * _rstd(x1) * g_ref[...]).astype(BF16)
        f = _dot(h1, wf_ref[...]) + bf_ref[...]
        logsig = jnp.minimum(f, 0.0) - jnp.log1p(jnp.exp(-jnp.abs(f)))
        cum = _cumsum_rows(logsig) + ccar_ref[SUBLANES - 1:SUBLANES, :]
        hi, mid, lo = _split3(cum)
        lane = lax.broadcasted_iota(jnp.int32, cum.shape, 1)
        packed = jnp.where(lane < HEADS, hi.astype(F32), jnp.where(
            lane < 2 * HEADS, pltpu.roll(mid.astype(F32), HEADS, 1), jnp.where(
                lane < 3 * HEADS, pltpu.roll(lo.astype(F32), 2 * HEADS, 1), 0.0)))
        x1_ref[...] = x1
        h1_ref[...] = h1
        f_ref[...] = f
        cp_ref[...] = packed.astype(BF16)
        ccar_ref[...] = cum[ts - SUBLANES:, :]

    return pl.pallas_call(
        body, name="fox_pre_fwd", grid=(s // ts,),
        in_specs=[pl.BlockSpec((ts, D_MODEL), lambda i: (i, 0)),
                  pl.BlockSpec((ts, LRU_WIDTH), lambda i: (i, 0)),
                  _const_spec((LRU_WIDTH, D_MODEL)),
                  _const_spec((1, D_MODEL)),
                  _const_spec((D_MODEL, LANES)),
                  _const_spec((1, LANES))],
        out_specs=[pl.BlockSpec((ts, D_MODEL), lambda i: (i, 0)),
                   pl.BlockSpec((ts, D_MODEL), lambda i: (i, 0)),
                   pl.BlockSpec((ts, LANES), lambda i: (i, 0)),
                   pl.BlockSpec((ts, LANES), lambda i: (i, 0))],
        out_shape=[jax.ShapeDtypeStruct((s, D_MODEL), F32),
                   jax.ShapeDtypeStruct((s, D_MODEL), BF16),
                   jax.ShapeDtypeStruct((s, LANES), F32),
                   jax.ShapeDtypeStruct((s, LANES), BF16)],
        scratch_shapes=[pltpu.VMEM((SUBLANES, LANES), F32)],
        compiler_params=_params(),
    )(x, y, w_out, g1, wf, bf)


def _fox_proj_fwd(h1, cparts, w, first, ng, sel, bias, out_dtype, ts, name):
    s = h1.shape[0]
    width = HEADS * HEAD_DIM
    use_sel = sel is not None

    def body(*refs):
        if use_sel:
            h_ref, cp_ref, w_ref, sel_ref, b_ref, o_ref = refs
            proj = _dot(h_ref[...], w_ref[...])
            if first == 0:
                proj = proj * jnp.where(pl.program_id(0) == 0, QK_SCALE, 1.0)
            acc = _heads_to_padded(proj) + _dot(cp_ref[...], sel_ref[...]) + b_ref[...]
        else:
            h_ref, w_ref, o_ref = refs
            acc = _heads_to_padded(_dot(h_ref[...], w_ref[...]))
        o_ref[...] = acc.astype(out_dtype)

    in_specs = [pl.BlockSpec((ts, D_MODEL), lambda j, i: (i, 0))]
    args = [h1]
    if use_sel:
        in_specs.append(pl.BlockSpec((ts, LANES), lambda j, i: (i, 0)))
        args.append(cparts)
    in_specs.append(pl.BlockSpec((D_MODEL, width), lambda j, i: (0, first + j)))
    args.append(w)
    if use_sel:
        in_specs.append(pl.BlockSpec((None, LANES, FOX_PAD), lambda j, i: (j, 0, 0)))
        in_specs.append(pl.BlockSpec((None, 1, FOX_PAD), lambda j, i: (j, 0, 0)))
        args += [sel, bias]
    return pl.pallas_call(
        body, name=name, grid=(ng, s // ts),
        in_specs=in_specs,
        out_specs=pl.BlockSpec((None, ts, FOX_PAD), lambda j, i: (j, i, 0)),
        out_shape=jax.ShapeDtypeStruct((ng, s, FOX_PAD), out_dtype),
        compiler_params=_params(2),
    )(*args)


def _attn_fwd(qkv, blk, hps=HEADS_PER_STEP):
    s = qkv.shape[1]
    nblk = s // blk
    wide = 2 * blk
    heads = [slice(i * HEAD_PAD, (i + 1) * HEAD_PAD) for i in range(hps)]

    def body(q_ref, k_ref, v_ref, o_ref, qb_ref, acc_ref, m_ref):
        qi = pl.program_id(1)
        row = lax.broadcasted_iota(jnp.int32, (blk, blk), 0)
        col = lax.broadcasted_iota(jnp.int32, (blk, blk), 1)
        lane = lax.broadcasted_iota(jnp.int32, (blk, HEAD_PAD), 1)
        qs = [q_ref[:, hd] for hd in heads]
        for i in range(hps):
            acc_ref[i] = jnp.zeros((blk, HEAD_PAD), F32)
            m_ref[i] = jnp.full((blk, HEAD_PAD), NEG_BIG, F32)

        def step(k0, size, masked):
            scores = [_dot_nt(q, k_ref[pl.ds(k0, size), hd]) for q, hd in zip(qs, heads)]
            for i, (sc, hd) in enumerate(zip(scores, heads)):
                v = v_ref[pl.ds(k0, size), hd]
                if masked:
                    sc = jnp.where(col <= row, sc, NEG_BIG)
                m = m_ref[i]
                m_new = jnp.maximum(m, jnp.max(sc, axis=-1, keepdims=True))
                p = jnp.exp((sc - jnp.tile(m_new, (1, size // HEAD_PAD))).astype(BF16))
                acc_ref[i] = jnp.exp(m - m_new) * acc_ref[i] + _dot(p, v)
                m_ref[i] = m_new

        def wide_step(kk, _):
            step(pl.multiple_of(kk * wide, wide), wide, False)
            return 0

        lax.fori_loop(0, qi // 2, wide_step, 0)

        @pl.when(qi % 2 == 1)
        def _():
            step(pl.multiple_of((qi - 1) * blk, blk), blk, False)

        step(pl.multiple_of(qi * blk, blk), blk, True)
        for i, (q, hd) in enumerate(zip(qs, heads)):
            acc = acc_ref[i]
            l = jnp.broadcast_to(acc[:, LANE_ONE_V:LANE_ONE_V + 1], (blk, HEAD_PAD))
            o_ref[:, hd] = (acc / l).astype(BF16)
            hi, mid, lo = _split3(-(m_ref[i] + jnp.log(l)))
            qb_ref[:, hd] = jnp.where(lane == LANE_LSE, hi, jnp.where(
                lane == LANE_LSE + 1, mid, jnp.where(lane == LANE_LSE + 2, lo, q)))

    width = hps * HEAD_PAD

    def whole(j):
        return pl.BlockSpec((None, s, width), lambda h, i: (j, 0, h))

    out_spec = pl.BlockSpec((blk, width), lambda h, i: (i, h))
    return pl.pallas_call(
        body, name="attn_fwd", grid=(HEADS // hps, nblk),
        in_specs=[pl.BlockSpec((None, blk, width), lambda h, i: (0, i, h)), whole(1), whole(2)],
        out_specs=[out_spec, out_spec],
        out_shape=[jax.ShapeDtypeStruct((s, FOX_PAD), BF16),
                   jax.ShapeDtypeStruct((s, FOX_PAD), BF16)],
        scratch_shapes=[pltpu.VMEM((hps, blk, HEAD_PAD), F32),
                        pltpu.VMEM((hps, blk, HEAD_PAD), F32)],
        compiler_params=_params(2),
    )(qkv, qkv, qkv)


def _fox_out_loss(o, gate, w_out, x1, target, gf, ts):
    s = x1.shape[0]

    def body(o_ref, gt_ref, w_ref, x1_ref, t_ref, g_ref, dx2_ref, dx2b_ref, y2_ref, loss_ref,
             gfin_ref):
        @pl.when(pl.program_id(0) == 0)
        def _():
            loss_ref[...] = jnp.zeros_like(loss_ref)
            gfin_ref[...] = jnp.zeros_like(gfin_ref)

        gv = gt_ref[...]
        y2 = _heads_from_padded(o_ref[...] * (gv * _sigmoid(gv))).astype(BF16)
        x2 = x1_ref[...] + _dot(y2, w_ref[...])
        rstd = _rstd(x2)
        xhat = x2 * rstd
        g = g_ref[...]
        diff = xhat * g - t_ref[...]
        loss_ref[...] += 0.5 * jnp.sum(jnp.mean(diff * diff, axis=-1, keepdims=True))
        dy = diff * (1.0 / D_MODEL)
        gfin_ref[...] += jnp.sum(dy * xhat, axis=0, keepdims=True)
        dxh = dy * g
        dx2 = rstd * (dxh - xhat * jnp.mean(dxh * xhat, axis=-1, keepdims=True))
        dx2_ref[...] = dx2
        dx2b_ref[...] = dx2.astype(BF16)
        y2_ref[...] = y2

    return pl.pallas_call(
        body, name="fox_out_loss", grid=(s // ts,),
        in_specs=[pl.BlockSpec((ts, FOX_PAD), lambda i: (i, 0)),
                  pl.BlockSpec((ts, FOX_PAD), lambda i: (i, 0)),
                  _const_spec((HEADS * HEAD_DIM, D_MODEL)),
                  pl.BlockSpec((ts, D_MODEL), lambda i: (i, 0)),
                  pl.BlockSpec((ts, D_MODEL), lambda i: (i, 0)),
                  _const_spec((1, D_MODEL))],
        out_specs=[pl.BlockSpec((ts, D_MODEL), lambda i: (i, 0)),
                   pl.BlockSpec((ts, D_MODEL), lambda i: (i, 0)),
                   pl.BlockSpec((ts, HEADS * HEAD_DIM), lambda i: (i, 0)),
                   pl.BlockSpec((SUBLANES, LANES), lambda i: (0, 0)),
                   pl.BlockSpec((1, D_MODEL), lambda i: (0, 0))],
        out_shape=[jax.ShapeDtypeStruct((s, D_MODEL), F32),
                   jax.ShapeDtypeStruct((s, D_MODEL), BF16),
                   jax.ShapeDtypeStruct((s, HEADS * HEAD_DIM), BF16),
                   jax.ShapeDtypeStruct((SUBLANES, LANES), F32),
                   jax.ShapeDtypeStruct((1, D_MODEL), F32)],
        compiler_params=_params(),
    )(o, gate, w_out, x1, target, gf)


def _fox_out_bwd(dx2, w_out, o, gate, ts):
    s = dx2.shape[0]

    def body(dx_ref, w_ref, o_ref, gt_ref, do_ref, dg_ref):
        lane = lax.broadcasted_iota(jnp.int32, (ts, HEAD_PAD), 1)
        dy2 = _heads_to_padded(_dot_nt(dx_ref[...], w_ref[...]))
        gv = gt_ref[...]
        sg = _sigmoid(gv)
        ov = o_ref[...]
        dov = dy2 * (gv * sg)
        dg_ref[...] = (dy2 * ov * (sg * (1.0 + gv * (1.0 - sg)))).astype(BF16)
        prod = dov * ov
        for h in range(HEADS):
            sl = slice(h * HEAD_PAD, (h + 1) * HEAD_PAD)
            delta = jnp.sum(prod[:, sl], axis=-1, keepdims=True)
            hi = delta.astype(BF16)
            lo = (delta - hi.astype(F32)).astype(BF16)
            do_h = dov[:, sl].astype(BF16)
            do_ref[:, sl] = jnp.where(lane == LANE_ONE_V, -hi,
                                      jnp.where(lane == LANE_ONE_V + 1, -lo, do_h))

    tile = pl.BlockSpec((ts, FOX_PAD), lambda i: (i, 0))
    return pl.pallas_call(
        body, name="fox_out_bwd", grid=(s // ts,),
        in_specs=[pl.BlockSpec((ts, D_MODEL), lambda i: (i, 0)),
                  _const_spec((HEADS * HEAD_DIM, D_MODEL)), tile, tile],
        out_specs=[tile, tile],
        out_shape=[jax.ShapeDtypeStruct((s, FOX_PAD), BF16),
                   jax.ShapeDtypeStruct((s, FOX_PAD), BF16)],
        compiler_params=_params(),
    )(dx2, w_out, o, gate)


def _attn_bwd(qb, qkv, do, blk):
    s = qb.shape[0]
    nblk = s // blk
    half = blk // 2
    heads = [slice(i * HEAD_PAD, (i + 1) * HEAD_PAD) for i in range(HEADS_PER_STEP)]

    def body(q_ref, k_ref, v_ref, do_ref, dq_ref, dk_ref, dv_ref, dcum_ref, dq_acc, dkt_acc,
             dvt_acc, qt_ref, dot_ref):
        group = pl.program_id(0)
        kj = pl.program_id(1)
        row = lax.broadcasted_iota(jnp.int32, (blk, blk), 0)
        col = lax.broadcasted_iota(jnp.int32, (blk, blk), 1)
        lane = lax.broadcasted_iota(jnp.int32, (blk, LANES), 1)
        mine = [lane == group * HEADS_PER_STEP + i for i in range(HEADS_PER_STEP)]

        @pl.when(kj == 0)
        def _():
            dq_acc[...] = jnp.zeros_like(dq_acc)

            def transpose_block(bi, _):
                r0 = pl.multiple_of(bi * blk, blk)
                for i, hd in enumerate(heads):
                    qt_ref[i, bi] = q_ref[pl.ds(r0, blk), hd].T
                    dot_ref[i, bi] = do_ref[pl.ds(r0, blk), hd].T
                return 0

            lax.fori_loop(0, nblk, transpose_block, 0)

        @pl.when((group == 0) & (kj == 0))
        def _():
            dcum_ref[...] = jnp.zeros_like(dcum_ref)

        k0 = pl.multiple_of(kj * blk, blk)
        ks = [k_ref[:, hd] for hd in heads]
        vs = [v_ref[:, hd] for hd in heads]

        def step(qi, q_lo, nq, k_lo, nk, masked):
            parts = ([(0, q_lo, 0, nq)] if nq <= blk
                     else [(b, 0, b * blk, blk) for b in range(nq // blk)])
            q0 = pl.multiple_of(qi * blk + q_lo, half)
            qs = [q_ref[pl.ds(q0, nq), hd] for hd in heads]
            dos = [do_ref[pl.ds(q0, nq), hd] for hd in heads]
            kk = [k[k_lo:k_lo + nk] for k in ks]
            vv = [v[k_lo:k_lo + nk] for v in vs]
            scores = [_dot_nt(q, k) for q, k in zip(qs, kk)]
            dps = [_dot_nt(dov, v) for dov, v in zip(dos, vv)]
            for i, (hd, k, sc, dp) in enumerate(zip(heads, kk, scores, dps)):
                p = jnp.exp(sc.astype(BF16))
                if masked:
                    p = jnp.where(col[:nq, :nk] + k_lo <= row[:nq, :nk] + q_lo, p,
                                  jnp.zeros_like(p))
                ds = (p.astype(F32) * dp).astype(BF16)
                dvt = sum(_dot(dot_ref[i, qi + b, :, c:c + n], p[r:r + n]) for b, c, r, n in parts)
                dkt = sum(_dot(qt_ref[i, qi + b, :, c:c + n], ds[r:r + n]) for b, c, r, n in parts)
                if masked:
                    dvt_acc[i, :, k_lo:k_lo + nk] = dvt
                    dkt_acc[i, :, k_lo:k_lo + nk] = dkt
                else:
                    dvt_acc[i, :, k_lo:k_lo + nk] += dvt
                    dkt_acc[i, :, k_lo:k_lo + nk] += dkt
                dq_acc[pl.ds(q0, nq), hd] += _dot(ds, k)

        step(kj, 0, blk, 0, half, True)
        step(kj, half, half, half, half, True)

        n_after = nblk - 1 - kj

        def q_step(t, _):
            step(kj + 1 + 2 * t, 0, 2 * blk, 0, blk, False)
            return 0

        lax.fori_loop(0, n_after // 2, q_step, 0)

        @pl.when(n_after % 2 == 1)
        def _():
            step(nblk - 1, 0, blk, 0, blk, False)
        dcum = dcum_ref[pl.ds(k0, blk), :]
        for i, (hd, mask) in enumerate(zip(heads, mine)):
            dk = dkt_acc[i].T
            dk_ref[:, hd] = dk.astype(BF16)
            dv_ref[:, hd] = dvt_acc[i].astype(BF16).T
            dcum = jnp.where(mask, -dk[:, LANE_CK:LANE_CK + 1], dcum)
        dcum_ref[pl.ds(k0, blk), :] = dcum

        @pl.when(kj == nblk - 1)
        def _():
            def finish(bi, _):
                r0 = pl.multiple_of(bi * blk, blk)
                dcum = dcum_ref[pl.ds(r0, blk), :]
                for hd, mask in zip(heads, mine):
                    dq = dq_acc[pl.ds(r0, blk), hd]
                    dq_ref[pl.ds(r0, blk), hd] = dq.astype(BF16)
                    dcum = dcum + jnp.where(mask, dq[:, LANE_RB:LANE_RB + 1], 0.0)
                dcum_ref[pl.ds(r0, blk), :] = dcum
                return 0

            lax.fori_loop(0, nblk, finish, 0)

    width = HEADS_PER_STEP * HEAD_PAD
    whole = pl.BlockSpec((s, width), lambda h, j: (0, h))
    whole_in = pl.BlockSpec((s, width), lambda h, j: (0, h), pipeline_mode=pl.Buffered(1))
    part = pl.BlockSpec((blk, width), lambda h, j: (j, h))
    out = jax.ShapeDtypeStruct((s, FOX_PAD), BF16)
    return pl.pallas_call(
        body, name="attn_bwd", grid=(HEADS // HEADS_PER_STEP, nblk),
        in_specs=[whole_in,
                  pl.BlockSpec((None, blk, width), lambda h, j: (1, j, h)),
                  pl.BlockSpec((None, blk, width), lambda h, j: (2, j, h)),
                  whole_in],
        out_specs=[whole, part, part, pl.BlockSpec((s, LANES), lambda h, j: (0, 0))],
        out_shape=[out, out, out, jax.ShapeDtypeStruct((s, LANES), F32)],
        scratch_shapes=[pltpu.VMEM((s, width), F32),
                        pltpu.VMEM((HEADS_PER_STEP, HEAD_PAD, blk), F32),
                        pltpu.VMEM((HEADS_PER_STEP, HEAD_PAD, blk), F32),
                        pltpu.VMEM((HEADS_PER_STEP, nblk, HEAD_PAD, blk), BF16),
                        pltpu.VMEM((HEADS_PER_STEP, nblk, HEAD_PAD, blk), BF16)],
        compiler_params=_params(2),
    )(qb, qkv, qkv, do)


def _fox_in_bwd(dq, dk, dv, dg, wt, wft, dcum, f, x1, dx2, g1, ts):
    s = x1.shape[0]
    nt = s // ts
    width = HEADS * HEAD_DIM

    def body(dq_ref, dk_ref, dv_ref, dg_ref, wt_ref, wft_ref, dcum_ref, f_ref, x1_ref, dx2_ref,
             g_ref, dx1_ref, dx1b_ref, df_ref, duq_ref, duk_ref, duv_ref, dug_ref, gn_ref, gbf_ref,
             rcar_ref):
        du_refs = (duq_ref, duk_ref, duv_ref, dug_ref)


        @pl.when(pl.program_id(0) == 0)
        def _():
            rcar_ref[...] = jnp.zeros_like(rcar_ref)
            gn_ref[...] = jnp.zeros_like(gn_ref)
            gbf_ref[...] = jnp.zeros_like(gbf_ref)

        rsum = _cumsum_rows(dcum_ref[...], reverse=True) + rcar_ref[0:1, :]
        df = rsum * _sigmoid(-f_ref[...])
        dfb = df.astype(BF16)
        dh = _dot_nt(dfb, wft_ref[...])
        for j, ref in enumerate((dq_ref, dk_ref, dv_ref, dg_ref)):
            du = _heads_from_padded(ref[...])
            du_refs[j][...] = du
            if j == 0:
                du = du * QK_SCALE
            dh = dh + _dot_nt(du, wt_ref[:, j * width:(j + 1) * width])
        dxn, dgn = _norm_bwd(x1_ref[...], g_ref[...], dh)
        dx1 = dx2_ref[...] + dxn
        dx1_ref[...] = dx1
        dx1b_ref[...] = dx1.astype(BF16)
        df_ref[...] = dfb
        gn_ref[...] += dgn
        gbf_ref[...] += jnp.sum(df, axis=0, keepdims=True)
        rcar_ref[...] = rsum[0:SUBLANES, :]

    rev = lambda i: (nt - 1 - i, 0)
    wide = pl.BlockSpec((ts, FOX_PAD), rev)
    return pl.pallas_call(
        body, name="fox_in_bwd", grid=(nt,),
        in_specs=[wide, wide, wide, wide,
                  _const_spec((D_MODEL, FOX_IN_COLS)),
                  _const_spec((D_MODEL, LANES)),
                  pl.BlockSpec((ts, LANES), rev),
                  pl.BlockSpec((ts, LANES), rev),
                  pl.BlockSpec((ts, D_MODEL), rev),
                  pl.BlockSpec((ts, D_MODEL), rev),
                  _const_spec((1, D_MODEL))],
        out_specs=[pl.BlockSpec((ts, D_MODEL), rev),
                   pl.BlockSpec((ts, D_MODEL), rev),
                   pl.BlockSpec((ts, LANES), rev)]
        + [pl.BlockSpec((ts, width), rev)] * 4
        + [pl.BlockSpec((1, D_MODEL), lambda i: (0, 0)),
           pl.BlockSpec((1, LANES), lambda i: (0, 0))],
        out_shape=[jax.ShapeDtypeStruct((s, D_MODEL), F32),
                   jax.ShapeDtypeStruct((s, D_MODEL), BF16),
                   jax.ShapeDtypeStruct((s, LANES), BF16)]
        + [jax.ShapeDtypeStruct((s, width), BF16)] * 4
        + [jax.ShapeDtypeStruct((1, D_MODEL), F32),
                   jax.ShapeDtypeStruct((1, LANES), F32)],
        scratch_shapes=[pltpu.VMEM((SUBLANES, LANES), F32)],
        compiler_params=_params(),
    )(dq, dk, dv, dg, wt, wft, dcum, f, x1, dx2, g1)


def _lru_core_bwd(dx1b, w_out, xb, gate, hs, cw, cb, wa, ba, wx, bx, a_param, wa_t, wx_t,
                  chip_sums, ts):
    s = xb.shape[0]
    nt = s // ts
    tpb = ts // SUBLANES
    n_ex = len(chip_sums)

    def body(*refs):
        (dx_ref, wo_ref, xb_ref, xbh_ref, gate_ref, hs_ref, hsh_ref, cw_ref, cb_ref, wa_ref,
         ba_ref, wx_ref, bx_ref, ap_ref, wat_ref, wxt_ref) = refs[:16]
        sum_refs = refs[16:16 + n_ex]
        du_ref, gwa_ref, gwx_ref, gvec_ref = refs[16 + n_ex:20 + n_ex]
        got_refs = refs[20 + n_ex:20 + 2 * n_ex]
        acar_ref, dhcar_ref, dxccar_ref = refs[20 + 2 * n_ex:23 + 2 * n_ex]
        start, finish = _chip_exchange_phases(sum_refs, got_refs, *refs[23 + 2 * n_ex:])
        step = pl.program_id(0)
        pl.when(step == 0)(start)

        @pl.when(step == 0)
        def _():
            acar_ref[...] = jnp.zeros_like(acar_ref)
            dhcar_ref[...] = jnp.zeros_like(dhcar_ref)
            dxccar_ref[...] = jnp.zeros_like(dxccar_ref)
            gwa_ref[...] = jnp.zeros_like(gwa_ref)
            gwx_ref[...] = jnp.zeros_like(gwx_ref)
            gvec_ref[...] = jnp.zeros_like(gvec_ref)

        first_tile = step == nt - 1
        halo_on = jnp.where(first_tile, 0.0, 1.0)
        prev8 = xbh_ref[...] * halo_on
        hprev_row = hsh_ref[SUBLANES - 1:SUBLANES, :] * halo_on

        xbv = xb_ref[...]
        taps = _conv_taps(xbv, prev8)
        cw_v = cw_ref[...]
        xc, xcb, r, i, sp, a, mult = _lru_pre(taps, cw_v, cb_ref[...], wa_ref, ba_ref[...],
                                              wx_ref, bx_ref[...], ap_ref[...])
        hs = hs_ref[...]
        gv = gate_ref[...]
        sg = _sigmoid(gv)
        dy = _dot_nt(dx_ref[...], wo_ref[...])
        dhs = dy * (gv * sg)
        dgate = dy * hs * (sg * (1.0 + gv * (1.0 - sg)))

        rows = lax.broadcasted_iota(jnp.int32, a.shape, 0)
        a_next = jnp.where(rows < ts - 1, pltpu.roll(a, ts - 1, 0), acar_ref[0:1, :])
        cum_a, dh_loc = _scan_rows(a_next, dhs, reverse=True)
        dh = cum_a * dhcar_ref[0:1, :] + dh_loc
        h_prev = jnp.where(rows >= 1, pltpu.roll(hs, 1, 0), hprev_row)

        da = dh * h_prev
        ixc = i * xc
        dmult = dh * ixc
        di = dh * mult * xc
        dxc = dh * mult * i
        dlog_a = da * a - dmult * (a * a) / mult
        dr = dlog_a * ((-LRU_C) * sp)
        dsp = jnp.sum(dlog_a * ((-LRU_C) * r), axis=0, keepdims=True)
        dra = dr * r * (1.0 - r)
        dia = di * i * (1.0 - i)
        drab = dra.astype(BF16)
        diab = dia.astype(BF16)
        back = []
        for n in range(LRU_BLOCKS):
            sl = slice(n * LRU_BLOCK_W, (n + 1) * LRU_BLOCK_W)
            gwa_ref[n] += _dot_tn(xcb[:, sl], drab[:, sl])
            gwx_ref[n] += _dot_tn(xcb[:, sl], diab[:, sl])
            back.append(_dot(drab[:, sl], wat_ref[n]) + _dot(diab[:, sl], wxt_ref[n]))
        dxc = dxc + jnp.concatenate(back, axis=1)

        nxt8 = dxccar_ref[...]
        rows8 = lax.broadcasted_iota(jnp.int32, nxt8.shape, 0)
        dxb = cw_v[3:4] * dxc
        for j in range(1, CONV_WIDTH):
            rj = pltpu.roll(dxc, ts - j, 0)
            pj = pltpu.roll(nxt8, SUBLANES - j, 0)
            tail = jnp.where(rows8 >= SUBLANES - j, pj, rj[ts - SUBLANES:])
            dxb = dxb + cw_v[3 - j:4 - j] * jnp.concatenate([rj[:ts - SUBLANES], tail], axis=0)

        du_ref[:, :LRU_WIDTH] = dxb.astype(BF16)
        du_ref[:, LRU_WIDTH:] = dgate.astype(BF16)

        z = -ap_ref[...]
        gvec = [jnp.sum(dxc * taps[3 - k], axis=0, keepdims=True) for k in range(CONV_WIDTH)]
        gvec.append(jnp.sum(dxc, axis=0, keepdims=True))
        gvec.append(jnp.sum(dra, axis=0, keepdims=True))
        gvec.append(jnp.sum(dia, axis=0, keepdims=True))
        gvec.append(-dsp * _sigmoid(z))
        gvec_ref[...] += jnp.concatenate(gvec, axis=0)

        acar_ref[...] = a[0:SUBLANES, :]
        dhcar_ref[...] = dh[0:SUBLANES, :]
        dxccar_ref[...] = dxc[0:SUBLANES, :]
        pl.when(step == nt - 1)(finish)

    rev = lambda i: (nt - 1 - i, 0)
    halo = lambda i: (jnp.maximum((nt - 1 - i) * tpb - 1, 0), 0)
    tile = pl.BlockSpec((ts, LRU_WIDTH), rev)
    halo_spec = pl.BlockSpec((SUBLANES, LRU_WIDTH), halo)
    vec = _const_spec((1, LRU_WIDTH))
    blk = _const_spec((LRU_BLOCKS, LRU_BLOCK_W, LRU_BLOCK_W))
    acc_blk = pl.BlockSpec((LRU_BLOCKS, LRU_BLOCK_W, LRU_BLOCK_W), lambda i: (0, 0, 0))
    hbm = pl.BlockSpec(memory_space=pl.ANY)
    res = pl.pallas_call(
        body, name="lru_core_bwd", grid=(nt,),
        in_specs=[pl.BlockSpec((ts, D_MODEL), rev),
                  _const_spec((LRU_WIDTH, D_MODEL)),
                  tile, halo_spec, tile, tile, halo_spec,
                  _const_spec((CONV_WIDTH, LRU_WIDTH)), vec, blk, vec, blk, vec, vec, blk, blk]
        + [hbm] * n_ex,
        out_specs=[pl.BlockSpec((ts, 2 * LRU_WIDTH), rev), acc_blk, acc_blk,
                   pl.BlockSpec((SUBLANES, LRU_WIDTH), lambda i: (0, 0))] + [hbm] * n_ex,
        out_shape=[jax.ShapeDtypeStruct((s, 2 * LRU_WIDTH), BF16),
                   jax.ShapeDtypeStruct((LRU_BLOCKS, LRU_BLOCK_W, LRU_BLOCK_W), F32),
                   jax.ShapeDtypeStruct((LRU_BLOCKS, LRU_BLOCK_W, LRU_BLOCK_W), F32),
                   jax.ShapeDtypeStruct((SUBLANES, LRU_WIDTH), F32)]
        + [jax.ShapeDtypeStruct(a.shape, a.dtype) for a in chip_sums],
        scratch_shapes=[pltpu.VMEM((SUBLANES, LRU_WIDTH), F32),
                        pltpu.VMEM((SUBLANES, LRU_WIDTH), F32),
                        pltpu.VMEM((SUBLANES, LRU_WIDTH), F32)] + _chip_exchange_sems(n_ex),
        compiler_params=_params(),
    )(dx1b, w_out, xb, xb, gate, hs, hs, cw, cb, wa, ba, wx, bx, a_param, wa_t, wx_t, *chip_sums)
    return res[0], res[1], res[2], res[3], res[4:]


def _lru_in_bwd(du, w_in, x, dx1, g0, chip_sums, ts):
    s = x.shape[0]
    nt = s // ts
    n = len(chip_sums)

    def body(*refs):
        du_ref, w_ref, x_ref, dx1_ref, g_ref = refs[:5]
        sum_refs = refs[5:5 + n]
        gx_ref, gn_ref = refs[5 + n:7 + n]
        got_refs = refs[7 + n:7 + 2 * n]
        wfull_ref = refs[7 + 2 * n]
        start, finish = _chip_exchange_phases(sum_refs, got_refs, *refs[8 + 2 * n:])
        step = pl.program_id(0)
        pl.when(step == 0)(start)

        @pl.when(step == 0)
        def _():
            gn_ref[...] = jnp.zeros_like(gn_ref)
            for j in range(N_DEV):
                wfull_ref[:, j * LRU_IN_SHARD:(j + 1) * LRU_IN_SHARD] = w_ref[j]

        dh = _dot_nt(du_ref[...], wfull_ref[...])
        dxn, dgn = _norm_bwd(x_ref[...], g_ref[...], dh)
        gx_ref[...] = dx1_ref[...] + dxn
        gn_ref[...] += dgn
        pl.when(step == nt - 1)(finish)

    tile = pl.BlockSpec((ts, D_MODEL), lambda i: (i, 0))
    hbm = pl.BlockSpec(memory_space=pl.ANY)
    res = pl.pallas_call(
        body, name="lru_in_bwd", grid=(nt,),
        in_specs=[pl.BlockSpec((ts, 2 * LRU_WIDTH), lambda i: (i, 0)),
                  _const_spec((N_DEV, D_MODEL, LRU_IN_SHARD)), tile, tile,
                  _const_spec((1, D_MODEL))] + [hbm] * n,
        out_specs=[tile, pl.BlockSpec((1, D_MODEL), lambda i: (0, 0))] + [hbm] * n,
        out_shape=[jax.ShapeDtypeStruct((s, D_MODEL), F32),
                   jax.ShapeDtypeStruct((1, D_MODEL), F32)]
        + [jax.ShapeDtypeStruct(a.shape, a.dtype) for a in chip_sums],
        scratch_shapes=[pltpu.VMEM((D_MODEL, 2 * LRU_WIDTH), BF16)] + _chip_exchange_sems(n),
        compiler_params=_params(),
    )(du, w_in, x, dx1, g0, *chip_sums)
    return res[0], res[1], res[2:]


def _weight_grad(a, b, ts, name, scale=1.0, col_shards=1):
    s, ka = a.shape
    nb = b.shape[1]
    nt = s // ts
    per = nb // col_shards

    def body(a_ref, b_ref, o_ref):
        @pl.when(pl.program_id(0) == 0)
        def _():
            o_ref[...] = jnp.zeros_like(o_ref)

        if col_shards == 1:
            o_ref[...] += _dot_tn(a_ref[...], b_ref[...])
        else:
            acc = _dot_tn(a_ref[...], b_ref[...])
            for j in range(col_shards):
                o_ref[j] += acc[:, j * per:(j + 1) * per]
        if scale != 1.0:
            @pl.when(pl.program_id(0) == nt - 1)
            def _():
                o_ref[...] = o_ref[...] * scale

    out_dims = (ka, nb) if col_shards == 1 else (col_shards, ka, per)
    return pl.pallas_call(
        body, name=name, grid=(nt,),
        in_specs=[pl.BlockSpec((ts, ka), lambda i: (i, 0)),
                  pl.BlockSpec((ts, nb), lambda i: (i, 0))],
        out_specs=pl.BlockSpec(out_dims, lambda i: (0,) * len(out_dims)),
        out_shape=jax.ShapeDtypeStruct(out_dims, F32),
        compiler_params=_params(),
    )(a, b)


def _sum_parts(gp_ref):
    g = gp_ref[0].astype(F32)
    for k in range(1, gp_ref.shape[0]):
        g = g + gp_ref[k].astype(F32)
    return g


def _adamw(g_parts, w, m, v, tr, name):
    nparts, rows, cols = g_parts.shape

    def body(gp_ref, w_ref, m_ref, v_ref, g_ref, d_ref, mo_ref, vo_ref):
        g = _sum_parts(gp_ref)
        m2 = ADAM_B1 * m_ref[...] + (1.0 - ADAM_B1) * g
        v2 = ADAM_B2 * v_ref[...] + (1.0 - ADAM_B2) * (g * g)
        m_hat = m2 / (1.0 - ADAM_B1 ** ADAM_STEP)
        v_hat = v2 / (1.0 - ADAM_B2 ** ADAM_STEP)
        g_ref[...] = g
        d_ref[...] = (-ADAM_LR) * (m_hat / (jnp.sqrt(v_hat) + ADAM_EPS) + ADAM_WD * w_ref[...])
        mo_ref[...] = m2
        vo_ref[...] = v2

    tile = pl.BlockSpec((tr, cols), lambda i: (i, 0))
    out = jax.ShapeDtypeStruct((rows, cols), F32)
    return pl.pallas_call(
        body, name=name, grid=(rows // tr,),
        in_specs=[pl.BlockSpec((nparts, tr, cols), lambda i: (0, i, 0)), tile, tile, tile],
        out_specs=[tile, tile, tile, tile],
        out_shape=[out, out, out, out],
        compiler_params=_params(),
    )(g_parts, w, m, v)


def _reduce_parts(g_parts, name):
    _, rows, cols = g_parts.shape

    def body(gp_ref, g_ref):
        g_ref[...] = _sum_parts(gp_ref)

    return pl.pallas_call(
        body, name=name,
        out_shape=jax.ShapeDtypeStruct((rows, cols), F32),
        compiler_params=pltpu.CompilerParams(vmem_limit_bytes=VMEM_LIMIT_BYTES),
    )(g_parts)


def _mesh_pos():
    ix, iy, ic = lax.axis_index("x"), lax.axis_index("y"), lax.axis_index("c")
    return ix, iy, ic


def _peer(ix, iy, ic, mask):
    px = 1 - ix if mask & 4 else ix
    py = 1 - iy if mask & 2 else iy
    pc = 1 - ic if mask & 1 else ic
    return (px, py, pc), 4 * px + 2 * py + pc


def _exchange(arrays, scatter, name):
    n = len(arrays)

    def body(*refs):
        x_refs, o_refs = refs[:n], refs[n:2 * n]
        send_sems, recv_sems, local_sems = refs[2 * n:]
        ix, iy, ic = _mesh_pos()
        me = 4 * ix + 2 * iy + ic

        def src(a, dest):
            return x_refs[a].at[dest] if scatter else x_refs[a]

        local = [pltpu.make_async_copy(src(a, me), o_refs[a].at[me], local_sems.at[a])
                 for a in range(n)]
        for cp in local:
            cp.start()
        sends = []
        for mask in range(1, N_DEV):
            peer, pidx = _peer(ix, iy, ic, mask)
            for a in range(n):
                cp = pltpu.make_async_remote_copy(
                    src_ref=src(a, pidx), dst_ref=o_refs[a].at[me],
                    send_sem=send_sems.at[a, mask - 1], recv_sem=recv_sems.at[a, mask - 1],
                    device_id=peer, device_id_type=pl.DeviceIdType.MESH)
                cp.start()
                sends.append(cp)
        for mask in range(1, N_DEV):
            peer, pidx = _peer(ix, iy, ic, mask)
            for a in range(n):
                pltpu.make_async_remote_copy(
                    src_ref=src(a, me), dst_ref=o_refs[a].at[pidx],
                    send_sem=send_sems.at[a, mask - 1], recv_sem=recv_sems.at[a, mask - 1],
                    device_id=peer, device_id_type=pl.DeviceIdType.MESH).wait_recv()
        for cp in sends:
            cp.wait_send()
        for cp in local:
            cp.wait()

    out_shape = [jax.ShapeDtypeStruct(x.shape if scatter else (N_DEV,) + x.shape, x.dtype)
                 for x in arrays]
    return pl.pallas_call(
        body, name=name,
        in_specs=[pl.BlockSpec(memory_space=pl.ANY)] * n,
        out_specs=[pl.BlockSpec(memory_space=pl.ANY)] * n,
        out_shape=out_shape,
        scratch_shapes=[pltpu.SemaphoreType.DMA((n, N_DEV - 1)),
                        pltpu.SemaphoreType.DMA((n, N_DEV - 1)),
                        pltpu.SemaphoreType.DMA((n,))],
    )(*arrays)


def _gather_two_level(arrays, name):
    n = len(arrays)

    def body(*refs):
        start, forward, finish = _gather_phases(refs[:n], refs[n:2 * n], *refs[2 * n:])
        start()
        forward()
        finish()

    return pl.pallas_call(
        body, name=name,
        in_specs=[pl.BlockSpec(memory_space=pl.ANY)] * n,
        out_specs=[pl.BlockSpec(memory_space=pl.ANY)] * n,
        out_shape=[jax.ShapeDtypeStruct((N_DEV,) + x.shape, x.dtype) for x in arrays],
        scratch_shapes=_gather_sems(n),
    )(*arrays)


def _gather_sems(n):
    return [pltpu.SemaphoreType.DMA((n, N_DEV - 1)), pltpu.SemaphoreType.DMA((n, N_DEV - 1)),
            pltpu.SemaphoreType.DMA((n,))]


def _gather_phases(x_refs, o_refs, send_sems, recv_sems, local_sems):
    n = len(x_refs)
    ix, iy, ic = _mesh_pos()
    me, sibling = (ix, iy, ic), (ix, iy, 1 - ic)
    chips = [(1 - ix, iy), (ix, 1 - iy), (1 - ix, 1 - iy)]

    def idx(px, py, pc):
        return 4 * px + 2 * py + pc

    def copy(a, k, block, to, src=None):
        dst = o_refs[a].at[idx(*block)]
        return pltpu.make_async_remote_copy(
            src_ref=dst if src is None else src, dst_ref=dst,
            send_sem=send_sems.at[a, k], recv_sem=recv_sems.at[a, k],
            device_id=to, device_id_type=pl.DeviceIdType.MESH)

    def local():
        return [pltpu.make_async_copy(x_refs[a], o_refs[a].at[idx(*me)], local_sems.at[a])
                for a in range(n)]

    def first():
        out = []
        for a in range(n):
            out.append(copy(a, 0, me, sibling, src=x_refs[a]))
            out += [copy(a, 1 + j, me, (*chip, ic), src=x_refs[a])
                    for j, chip in enumerate(chips)]
        return out

    def passed():
        return [copy(a, 4 + j, (*chip, ic), sibling)
                for j, chip in enumerate(chips) for a in range(n)]

    def start():
        for cp in local() + first():
            cp.start()

    def forward():
        for j, chip in enumerate(chips):
            for a in range(n):
                copy(a, 1 + j, (*chip, ic), me).wait_recv()
                copy(a, 4 + j, (*chip, ic), sibling).start()

    def finish():
        for a in range(n):
            copy(a, 0, sibling, me).wait_recv()
            for j, chip in enumerate(chips):
                copy(a, 4 + j, (*chip, 1 - ic), me).wait_recv()
        for cp in first() + passed():
            cp.wait_send()
        for cp in local():
            cp.wait()

    return start, forward, finish


def _swap_sibling(arrays, name):
    n = len(arrays)
    n_chips = N_DEV // 2

    def body(*refs):
        x_refs, got_refs = refs[:n], refs[n:2 * n]
        send_sems, recv_sems = refs[2 * n:]
        ix, iy, ic = _mesh_pos()
        sibling = (ix, iy, 1 - ic)
        sends = []
        for a in range(n):
            for q in range(n_chips):
                cp = pltpu.make_async_remote_copy(
                    src_ref=x_refs[a].at[q, 1 - ic], dst_ref=got_refs[a].at[q],
                    send_sem=send_sems.at[a, q], recv_sem=recv_sems.at[a, q],
                    device_id=sibling, device_id_type=pl.DeviceIdType.MESH)
                cp.start()
                sends.append(cp)
        for cp in sends:
            cp.wait()

    return pl.pallas_call(
        body, name=name,
        in_specs=[pl.BlockSpec(memory_space=pl.ANY)] * n,
        out_specs=[pl.BlockSpec(memory_space=pl.ANY)] * n,
        out_shape=[jax.ShapeDtypeStruct((n_chips,) + x.shape[2:], x.dtype) for x in arrays],
        scratch_shapes=[pltpu.SemaphoreType.DMA((n, n_chips)),
                        pltpu.SemaphoreType.DMA((n, n_chips))],
    )(*arrays)


def _exchange_chips(arrays, name):
    n = len(arrays)

    def body(*refs):
        start, finish = _chip_exchange_phases(refs[:n], refs[n:2 * n], *refs[2 * n:])
        start()
        finish()

    return pl.pallas_call(
        body, name=name,
        in_specs=[pl.BlockSpec(memory_space=pl.ANY)] * n,
        out_specs=[pl.BlockSpec(memory_space=pl.ANY)] * n,
        out_shape=[jax.ShapeDtypeStruct(x.shape, x.dtype) for x in arrays],
        scratch_shapes=_chip_exchange_sems(n),
    )(*arrays)


def _chip_exchange_sems(n):
    n_chips = N_DEV // 2
    return [pltpu.SemaphoreType.DMA((n, n_chips - 1)), pltpu.SemaphoreType.DMA((n, n_chips - 1)),
            pltpu.SemaphoreType.DMA((n,))]


def _chip_exchange_phases(x_refs, o_refs, send_sems, recv_sems, local_sems):
    n = len(x_refs)
    n_chips = N_DEV // 2
    ix, iy, ic = _mesh_pos()
    my_chip = 2 * ix + iy

    def peers():
        for mask in range(1, n_chips):
            px = 1 - ix if mask & 2 else ix
            py = 1 - iy if mask & 1 else iy
            yield mask, (px, py, ic), 2 * px + py

    def local():
        return [pltpu.make_async_copy(x_refs[a].at[my_chip], o_refs[a].at[my_chip],
                                      local_sems.at[a]) for a in range(n)]

    def sends():
        return [pltpu.make_async_remote_copy(
            src_ref=x_refs[a].at[chip], dst_ref=o_refs[a].at[my_chip],
            send_sem=send_sems.at[a, mask - 1], recv_sem=recv_sems.at[a, mask - 1],
            device_id=peer, device_id_type=pl.DeviceIdType.MESH)
            for mask, peer, chip in peers() for a in range(n)]

    def start():
        for cp in local() + sends():
            cp.start()

    def finish():
        for mask, peer, chip in peers():
            for a in range(n):
                pltpu.make_async_remote_copy(
                    src_ref=x_refs[a].at[my_chip], dst_ref=o_refs[a].at[chip],
                    send_sem=send_sems.at[a, mask - 1], recv_sem=recv_sems.at[a, mask - 1],
                    device_id=peer, device_id_type=pl.DeviceIdType.MESH).wait_recv()
        for cp in sends():
            cp.wait_send()
        for cp in local():
            cp.wait()

    return start, finish


def _pair_sum(core, x, got, name):
    nq, rows, cols = got.shape

    def body(c_ref, x_ref, g_ref, o_ref):
        o_ref[...] = (x_ref[...] + g_ref[...]).astype(BF16)

    blk = pl.BlockSpec((None, rows, cols), lambda q, c: (q, 0, 0))
    return pl.pallas_call(
        body, name=name,
        grid_spec=pltpu.PrefetchScalarGridSpec(
            num_scalar_prefetch=1, grid=(nq,),
            in_specs=[pl.BlockSpec((None, None, rows, cols), lambda q, c: (q, c[0], 0, 0)), blk],
            out_specs=blk),
        out_shape=jax.ShapeDtypeStruct(got.shape, BF16),
        compiler_params=_params(),
    )(core, x, got)


def _selectors():
    r = lax.broadcasted_iota(jnp.int32, (LANES, FOX_PAD), 0)
    c = lax.broadcasted_iota(jnp.int32, (LANES, FOX_PAD), 1)
    part, head_r = r // HEADS, r % HEADS
    head_c, lane_c = c // HEAD_PAD, c % HEAD_PAD
    same = (head_r == head_c) & (part < 3)
    sel_q = jnp.where(same & (lane_c == LANE_RB + part), 1.0, 0.0)
    sel_k = jnp.where(same & (lane_c == LANE_CK + part), -1.0, 0.0)
    sel = jnp.stack([sel_q, sel_k, jnp.zeros_like(sel_q)]).astype(BF16)
    lane = lax.broadcasted_iota(jnp.int32, (1, FOX_PAD), 1) % HEAD_PAD
    ones_q = jnp.where((lane >= LANE_CK) & (lane < LANE_CK + 3), 1.0, 0.0)
    ones_k = jnp.where(((lane >= LANE_RB) & (lane < LANE_RB + 3))
                       | ((lane >= LANE_LSE) & (lane < LANE_LSE + 3)), 1.0, 0.0)
    ones_v = jnp.where((lane >= LANE_ONE_V) & (lane < LANE_ONE_V + 2), 1.0, 0.0)
    bias = jnp.stack([ones_q, ones_k, ones_v]).astype(F32)
    return sel, bias


def _chip_sums(names, send):
    send = [a.reshape((N_DEV // 2, 2) + a.shape[1:]) for a in send]
    got = _swap_sibling(send, "swap_" + names[0])
    core = lax.axis_index("c").astype(jnp.int32).reshape(1)
    return [_pair_sum(core, a, b, "pair_sum_" + n) for n, a, b in zip(names, send, got)]


def _local_step(x, target, norm_g, final_g, w_in8, conv_w, conv_b, wa, ba, wx, bx, a_param,
                w_out_b, fox_in_shard, b_f, fox_out_shard, blk=512, ts=256):
    g0, g1 = norm_g[0:1], norm_g[1:2]
    gf = final_g.reshape(1, D_MODEL)
    wa_b, wx_b = wa.astype(BF16), wx.astype(BF16)
    sel, bias = _selectors()

    xb, gate1, h0, (fox_in8, fox_out8) = _lru_in_fwd(x, g0, w_in8, [fox_in_shard, fox_out_shard],
                                                     ts)
    fox_w_in = jnp.transpose(fox_in8, (1, 0, 2)).reshape(D_MODEL, FOX_IN_COLS)
    width = HEADS * HEAD_DIM
    wf_b = jnp.pad(fox_w_in[:, 4 * width:], ((0, 0), (0, LANES - HEADS)))
    bf_pad = jnp.pad(b_f, ((0, 0), (0, LANES - HEADS)))
    fo_b = fox_out8.reshape(width, D_MODEL)
    y1, hs = _lru_core_fwd(xb, gate1, conv_w, conv_b, wa_b, ba, wx_b, bx, a_param, ts)
    x1, h1, f, cparts = _fox_pre_fwd(x, y1, w_out_b, g1, wf_b, bf_pad, ts)
    qkv = _fox_proj_fwd(h1, cparts, fox_w_in, 0, 3, sel, bias, BF16, ts, "fox_proj_qkv")
    gate2 = _fox_proj_fwd(h1, None, fox_w_in, 3, 1, None, None, F32, ts, "fox_proj_gate")[0]
    o, qb = _attn_fwd(qkv, blk, hps=4)
    dx2, dx2b, y2, loss_acc, g_final = _fox_out_loss(o, gate2, fo_b, x1, target, gf, ts)

    do, dgate2 = _fox_out_bwd(dx2b, fo_b, o, gate2, ts)
    dq, dk, dv, dcum = _attn_bwd(qb, qkv, do, blk)
    dx1, dx1b, df, du_q, du_k, du_v, du_g, g_norm1, g_bf = _fox_in_bwd(
        dq, dk, dv, dgate2, fox_w_in, wf_b, dcum, f, x1, dx2, g1, ts)
    tw = 512
    g_q = _weight_grad(h1, du_q, tw, "grad_fox_wq", scale=QK_SCALE)
    g_k = _weight_grad(h1, du_k, tw, "grad_fox_wk")
    g_v = _weight_grad(h1, du_v, tw, "grad_fox_wv")
    g_g = _weight_grad(h1, du_g, tw, "grad_fox_wg")
    g_f = _weight_grad(h1, df, tw, "grad_fox_wf")
    g_fox_w_in = jnp.concatenate([g_q, g_k, g_v, g_g, g_f[:, :HEADS]], axis=1)
    g_fox_w_in = jnp.transpose(g_fox_w_in.reshape(D_MODEL, N_DEV, FOX_IN_SHARD), (1, 0, 2))
    g_fox_w_out = _weight_grad(y2, dx2b, tw, "grad_fox_w_out")
    fox_sums = _chip_sums(("fox_w_in", "fox_w_out"),
                          [g_fox_w_in, g_fox_w_out.reshape(N_DEV, -1, D_MODEL)])

    du, g_wa, g_wx, g_vec, (r_fox_in, r_fox_out) = _lru_core_bwd(
        dx1b, w_out_b, xb, gate1, hs, conv_w, conv_b, wa_b, ba, wx_b, bx, a_param,
        jnp.transpose(wa_b, (0, 2, 1)), jnp.transpose(wx_b, (0, 2, 1)), fox_sums, ts)
    g_lru_w_in = _weight_grad(h0, du, tw, "grad_lru_w_in", col_shards=N_DEV)
    g_lru_w_out = _weight_grad(y1, dx1b, tw, "grad_lru_w_out")
    conv_send = jnp.transpose(g_vec[0:CONV_WIDTH].reshape(CONV_WIDTH, N_DEV, -1), (1, 0, 2))
    lru_sums = _chip_sums(("lru_w_in", "lru_conv_w", "lru_w_out"),
                          [g_lru_w_in, conv_send, g_lru_w_out.reshape(N_DEV, -1, D_MODEL)])
    grad_x, g_norm0, (r_w_in, r_conv, r_w_out) = _lru_in_bwd(du, w_in8, x, dx1, g0, lru_sums, ts)

    small = dict(
        norm_g=jnp.concatenate([g_norm0, g_norm1], axis=0), final_g=g_final[0],
        lru_conv_b=g_vec[4:5], lru_wa=g_wa, lru_ba=g_vec[5:6], lru_wx=g_wx, lru_bx=g_vec[6:7],
        lru_a_param=g_vec[7:8], fox_b_f=g_bf[:, :HEADS])
    received = dict(lru_w_in=r_w_in, lru_conv_w=r_conv, lru_w_out=r_w_out, fox_w_in=r_fox_in,
                    fox_w_out=r_fox_out)
    return loss_acc[0, 0], grad_x, small, received


SMALL =("norm_g", "final_g", "lru_conv_b", "lru_wa", "lru_ba", "lru_wx", "lru_bx", "lru_a_param",
         "fox_b_f")
ALL_WEIGHTS = ("norm_g", "final_g", "lru_w_in", "lru_conv_w", "lru_conv_b", "lru_wa", "lru_ba",
               "lru_wx", "lru_bx", "lru_a_param", "lru_w_out", "fox_w_in", "fox_b_f", "fox_w_out")


def _pack_small(d):
    rows = []
    for n in SMALL:
        a = d[n].reshape(-1)
        if a.shape[0] % LANES:
            a = jnp.pad(a, (0, LANES - a.shape[0] % LANES))
        rows.append(a.reshape(-1, LANES))
    packed = jnp.concatenate(rows, axis=0)
    return jnp.pad(packed, ((0, N_DEV * SMALL_CHUNK_ROWS - packed.shape[0]), (0, 0)))


def _unpack_small(packed, like):
    out, off = {}, 0
    for n, nrows in zip(SMALL, SMALL_ROWS):
        size = like[n].size
        out[n] = packed[off:off + nrows].reshape(-1)[:size].reshape(like[n].shape)
        off += nrows
    return out


def kernel(x, norm_g, final_g, lru_w_in, lru_conv_w, lru_conv_b, lru_wa, lru_ba, lru_wx, lru_bx, lru_a_param, lru_w_out, fox_w_in, fox_b_f, fox_w_out, loss_target, m_norm_g, m_final_g, m_lru_w_in, m_lru_conv_w, m_lru_conv_b, m_lru_wa, m_lru_ba, m_lru_wx, m_lru_bx, m_lru_a_param, m_lru_w_out, m_fox_w_in, m_fox_b_f, m_fox_w_out, v_norm_g, v_final_g, v_lru_w_in, v_lru_conv_w, v_lru_conv_b, v_lru_wa, v_lru_ba, v_lru_wx, v_lru_bx, v_lru_a_param, v_lru_w_out, v_fox_w_in, v_fox_b_f, v_fox_w_out):
    w_loc = dict(norm_g=norm_g, final_g=final_g, lru_w_in=lru_w_in, lru_conv_w=lru_conv_w,
                 lru_conv_b=lru_conv_b, lru_wa=lru_wa, lru_ba=lru_ba, lru_wx=lru_wx, lru_bx=lru_bx,
                 lru_a_param=lru_a_param, lru_w_out=lru_w_out, fox_w_in=fox_w_in, fox_b_f=fox_b_f,
                 fox_w_out=fox_w_out)
    m_loc = dict(norm_g=m_norm_g, final_g=m_final_g, lru_w_in=m_lru_w_in, lru_conv_w=m_lru_conv_w,
                 lru_conv_b=m_lru_conv_b, lru_wa=m_lru_wa, lru_ba=m_lru_ba, lru_wx=m_lru_wx,
                 lru_bx=m_lru_bx, lru_a_param=m_lru_a_param, lru_w_out=m_lru_w_out,
                 fox_w_in=m_fox_w_in, fox_b_f=m_fox_b_f, fox_w_out=m_fox_w_out)
    v_loc = dict(norm_g=v_norm_g, final_g=v_final_g, lru_w_in=v_lru_w_in, lru_conv_w=v_lru_conv_w,
                 lru_conv_b=v_lru_conv_b, lru_wa=v_lru_wa, lru_ba=v_lru_ba, lru_wx=v_lru_wx,
                 lru_bx=v_lru_bx, lru_a_param=v_lru_a_param, lru_w_out=v_lru_w_out,
                 fox_w_in=v_fox_w_in, fox_b_f=v_fox_b_f, fox_w_out=v_fox_w_out)

    w_in8, conv8, w_out8 = _gather_two_level(
        [lru_w_in[0].astype(BF16), lru_conv_w[0], lru_w_out[0].astype(BF16)], "gather_weights")
    conv_full = jnp.transpose(conv8, (1, 0, 2)).reshape(CONV_WIDTH, LRU_WIDTH)

    loss, grad_x, small_grads, received = _local_step(
        x[0], loss_target[0], norm_g, final_g, w_in8, conv_full, lru_conv_b, lru_wa[0], lru_ba,
        lru_wx[0], lru_bx, lru_a_param, w_out8.reshape(LRU_WIDTH, D_MODEL),
        fox_w_in[0].astype(BF16), fox_b_f, fox_w_out[0].astype(BF16))

    out = {}
    for n, tr in (("lru_w_in", 256), ("lru_conv_w", CONV_WIDTH), ("lru_w_out", 96),
                  ("fox_w_in", 128), ("fox_w_out", 64)):
        res = _adamw(received[n], w_loc[n][0], m_loc[n][0], v_loc[n][0], tr, "adamw_" + n)
        out[n] = [a[None] for a in res]

    small_sums = _chip_sums(
        ("small",), [_pack_small(small_grads).reshape(N_DEV, SMALL_CHUNK_ROWS, LANES)])
    r_small, = _exchange_chips(small_sums, "scatter_small_grads")

    g_chunk = _reduce_parts(r_small, "reduce_small_grads")
    g_small, = _exchange([g_chunk], False, "gather_small_grads")
    g_small = g_small.reshape(1, N_DEV * SMALL_CHUNK_ROWS, LANES)
    res = _adamw(g_small, _pack_small(w_loc), _pack_small(m_loc), _pack_small(v_loc),
                 N_DEV * SMALL_CHUNK_ROWS, "adamw_replicated")
    small_out = [_unpack_small(a, w_loc) for a in res]
    for n in SMALL:
        out[n] = [d[n] for d in small_out]

    loss = lax.psum(loss, ("x", "y", "c"))
    return (loss, grad_x[None], *[out[n][0] for n in ALL_WEIGHTS], *[out[n][1] for n in ALL_WEIGHTS],
            *[out[n][2] for n in ALL_WEIGHTS], *[out[n][3] for n in ALL_WEIGHTS])
```

```python
import functools

import jax
import jax.numpy as jnp
from jax import lax
from jax.experimental import pallas as pl
from jax.experimental.pallas import tpu as pltpu

F32 = jnp.float32
BF16 = jnp.bfloat16

D_MODEL = 1024
LRU_WIDTH = 1536
LRU_BLOCKS = 12
LRU_BLOCK_W = 128
CONV_WIDTH = 4
LRU_C = 8.0
HEADS = 16
HEAD_DIM = 64
HEAD_PAD = 128
FOX_PAD = HEADS * HEAD_PAD
HEADS_PER_STEP = 2
QK_SCALE = 1.0 / HEAD_DIM ** 0.5
EPS = 1e-6
NEG_BIG = -1e30
N_DEV = 8

ADAM_LR = 0.001
ADAM_B1 = 0.9
ADAM_B2 = 0.999
ADAM_EPS = 1e-08
ADAM_WD = 0.01
ADAM_STEP = 10

LANE_RB = 64
LANE_CK = 67
LANE_LSE = 70
LANE_ONE_V = 64

VMEM_LIMIT_BYTES = 56 * 1024 * 1024
LANES = 128
SUBLANES = 8

LRU_IN_SHARD = 2 * LRU_WIDTH // N_DEV
FOX_IN_COLS = 4 * HEADS * HEAD_DIM + HEADS
FOX_IN_SHARD = FOX_IN_COLS // N_DEV

SMALL_ROWS = (16, 8, 12, 1536, 12, 1536, 12, 12, 1)
SMALL_CHUNK_ROWS = 400
assert sum(SMALL_ROWS) <= N_DEV * SMALL_CHUNK_ROWS


def _params(n_grid_axes=1):
    return pltpu.CompilerParams(
        dimension_semantics=("arbitrary",) * n_grid_axes,
        vmem_limit_bytes=VMEM_LIMIT_BYTES)


def _const_spec(shape):
    nd = len(shape)
    return pl.BlockSpec(shape, lambda *_: (0,) * nd, pipeline_mode=pl.Buffered(1))


def _shift_down(x, k, fill):
    rows = lax.broadcasted_iota(jnp.int32, x.shape, 0)
    return jnp.where(rows >= k, pltpu.roll(x, k, 0), fill)


def _shift_up(x, k, fill):
    n = x.shape[0]
    rows = lax.broadcasted_iota(jnp.int32, x.shape, 0)
    return jnp.where(rows < n - k, pltpu.roll(x, n - k, 0), fill)


def _scan_rows(a, b, reverse=False):
    n = a.shape[0]
    shift = _shift_up if reverse else _shift_down
    k = 1
    while k < n:
        b = a * shift(b, k, 0.0) + b
        a = a * shift(a, k, 1.0)
        k *= 2
    return a, b


def _cumsum_rows(x, reverse=False):
    n = x.shape[0]
    shift = _shift_up if reverse else _shift_down
    k = 1
    while k < n:
        x = x + shift(x, k, 0.0)
        k *= 2
    return x


def _rstd(x):
    return lax.rsqrt(jnp.mean(x * x, axis=-1, keepdims=True) + EPS)


def _norm_bwd(x, g, dh):
    rstd = _rstd(x)
    xhat = x * rstd
    dg = jnp.sum(dh * xhat, axis=0, keepdims=True)
    dxh = dh * g
    dx = rstd * (dxh - xhat * jnp.mean(dxh * xhat, axis=-1, keepdims=True))
    return dx, dg


def _split3(x):
    hi = x.astype(BF16)
    r1 = x - hi.astype(F32)
    mid = r1.astype(BF16)
    lo = (r1 - mid.astype(F32)).astype(BF16)
    return hi, mid, lo


def _sigmoid(x):
    return jax.nn.sigmoid(x)


def _dot(a, b):
    return jnp.dot(a, b, preferred_element_type=F32)


def _dot_nt(a, b):
    return lax.dot_general(a, b, (((1,), (1,)), ((), ())), preferred_element_type=F32)


def _dot_tn(a, b):
    return lax.dot_general(a, b, (((0,), (0,)), ((), ())), preferred_element_type=F32)


def _heads_to_padded(u):
    n = u.shape[0]
    low = lax.broadcasted_iota(jnp.int32, (n, LANES), 1) < HEAD_DIM
    zero = jnp.zeros((n, LANES), u.dtype)
    cols = []
    for p in range(HEADS // 2):
        pair = u[:, p * LANES:(p + 1) * LANES]
        cols.append(jnp.where(low, pair, zero))
        cols.append(jnp.where(low, pltpu.roll(pair, HEAD_DIM, 1), zero))
    return jnp.concatenate(cols, axis=1)


def _heads_from_padded(x):
    n = x.shape[0]
    low = lax.broadcasted_iota(jnp.int32, (n, LANES), 1) < HEAD_DIM
    cols = []
    for p in range(HEADS // 2):
        even = x[:, (2 * p) * HEAD_PAD:(2 * p + 1) * HEAD_PAD]
        odd = x[:, (2 * p + 1) * HEAD_PAD:(2 * p + 2) * HEAD_PAD]
        cols.append(jnp.where(low, even, pltpu.roll(odd, HEAD_DIM, 1)))
    return jnp.concatenate(cols, axis=1)


def _conv_taps(xb, prev8):
    rows8 = lax.broadcasted_iota(jnp.int32, prev8.shape, 0)
    taps = [xb]
    for j in range(1, CONV_WIDTH):
        r = pltpu.roll(xb, j, 0)
        p = pltpu.roll(prev8, j, 0)
        head = jnp.where(rows8 < j, p, r[0:SUBLANES])
        taps.append(jnp.concatenate([head, r[SUBLANES:]], axis=0))
    return taps


def _lru_pre(taps, cw, cb, wa_ref, ba, wx_ref, bx, a_param):
    xc = cb + cw[3:4] * taps[0] + cw[2:3] * taps[1] + cw[1:2] * taps[2] + cw[0:1] * taps[3]
    xcb = xc.astype(BF16)
    ra, ia = [], []
    for n in range(LRU_BLOCKS):
        blk = xcb[:, n * LRU_BLOCK_W:(n + 1) * LRU_BLOCK_W]
        ra.append(_dot(blk, wa_ref[n]))
        ia.append(_dot(blk, wx_ref[n]))
    r = _sigmoid(jnp.concatenate(ra, axis=1) + ba)
    i = _sigmoid(jnp.concatenate(ia, axis=1) + bx)
    z = -a_param
    sp = jnp.maximum(z, 0.0) + jnp.log1p(jnp.exp(-jnp.abs(z)))
    log_a = (-LRU_C) * r * sp
    a = jnp.exp(log_a)
    one_minus_a2 = -jnp.tanh(log_a) * (a * a + 1.0)
    mult = jnp.sqrt(one_minus_a2)
    return xc, xcb, r, i, sp, a, mult


def _lru_in_fwd(x, g0, w_in, later_shards, ts):
    s = x.shape[0]
    nt = s // ts
    n = len(later_shards)

    def body(*refs):
        x_ref, g_ref, w_ref = refs[:3]
        shard_refs = refs[3:3 + n]
        xb_ref, gate_ref, h_ref = refs[3 + n:6 + n]
        wfull_ref = refs[6 + 2 * n]
        start, forward, finish = _gather_phases(shard_refs, refs[6 + n:6 + 2 * n],
                                                *refs[7 + 2 * n:])
        step = pl.program_id(0)
        pl.when(step == 0)(start)

        @pl.when(step == 0)
        def _():
            for j in range(N_DEV):
                wfull_ref[:, j * LRU_IN_SHARD:(j + 1) * LRU_IN_SHARD] = w_ref[j]

        xv = x_ref[...]
        h = (xv * _rstd(xv) * g_ref[...]).astype(BF16)
        u = _dot(h, wfull_ref[...])
        xb_ref[...] = u[:, :LRU_WIDTH]
        gate_ref[...] = u[:, LRU_WIDTH:]
        h_ref[...] = h
        pl.when(step == (2 * nt) // 3)(forward)
        pl.when(step == nt - 1)(finish)

    hbm = pl.BlockSpec(memory_space=pl.ANY)
    res = pl.pallas_call(
        body, name="lru_in_fwd", grid=(nt,),
        in_specs=[pl.BlockSpec((ts, D_MODEL), lambda i: (i, 0)),
                  _const_spec((1, D_MODEL)),
                  _const_spec((N_DEV, D_MODEL, LRU_IN_SHARD))] + [hbm] * n,
        out_specs=[pl.BlockSpec((ts, LRU_WIDTH), lambda i: (i, 0)),
                   pl.BlockSpec((ts, LRU_WIDTH), lambda i: (i, 0)),
                   pl.BlockSpec((ts, D_MODEL), lambda i: (i, 0))] + [hbm] * n,
        out_shape=[jax.ShapeDtypeStruct((s, LRU_WIDTH), F32),
                   jax.ShapeDtypeStruct((s, LRU_WIDTH), F32),
                   jax.ShapeDtypeStruct((s, D_MODEL), BF16)]
        + [jax.ShapeDtypeStruct((N_DEV,) + a.shape, a.dtype) for a in later_shards],
        scratch_shapes=[pltpu.VMEM((D_MODEL, 2 * LRU_WIDTH), BF16)] + _gather_sems(n),
        compiler_params=_params(),
    )(x, g0, w_in, *later_shards)
    return res[0], res[1], res[2], res[3:]


def _lru_core_fwd(xb, gate, cw, cb, wa, ba, wx, bx, a_param, ts):
    s = xb.shape[0]

    def body(xb_ref, gate_ref, cw_ref, cb_ref, wa_ref, ba_ref, wx_ref, bx_ref, ap_ref,
             y_ref, hs_ref, prev_ref, hcar_ref):
        @pl.when(pl.program_id(0) == 0)
        def _():
            prev_ref[...] = jnp.zeros_like(prev_ref)
            hcar_ref[...] = jnp.zeros_like(hcar_ref)

        xbv = xb_ref[...]
        taps = _conv_taps(xbv, prev_ref[...])
        xc, _, _, i, _, a, mult = _lru_pre(taps, cw_ref[...], cb_ref[...], wa_ref, ba_ref[...],
                                           wx_ref, bx_ref[...], ap_ref[...])
        bterm = mult * (i * xc)
        cum_a, hloc = _scan_rows(a, bterm)
        hs = cum_a * hcar_ref[SUBLANES - 1:SUBLANES, :] + hloc
        gv = gate_ref[...]
        y_ref[...] = (hs * (gv * _sigmoid(gv))).astype(BF16)
        hs_ref[...] = hs
        prev_ref[...] = xbv[ts - SUBLANES:, :]
        hcar_ref[...] = hs[ts - SUBLANES:, :]

    vec = _const_spec((1, LRU_WIDTH))
    blk = _const_spec((LRU_BLOCKS, LRU_BLOCK_W, LRU_BLOCK_W))
    tile = pl.BlockSpec((ts, LRU_WIDTH), lambda i: (i, 0))
    return pl.pallas_call(
        body, name="lru_core_fwd", grid=(s // ts,),
        in_specs=[tile, tile, _const_spec((CONV_WIDTH, LRU_WIDTH)), vec, blk, vec, blk, vec, vec],
        out_specs=[tile, tile],
        out_shape=[jax.ShapeDtypeStruct((s, LRU_WIDTH), BF16),
                   jax.ShapeDtypeStruct((s, LRU_WIDTH), F32)],
        scratch_shapes=[pltpu.VMEM((SUBLANES, LRU_WIDTH), F32),
                        pltpu.VMEM((SUBLANES, LRU_WIDTH), F32)],
        compiler_params=_params(),
    )(xb, gate, cw, cb, wa, ba, wx, bx, a_param)


def _fox_pre_fwd(x, y, w_out, g1, wf, bf, ts):
    s = x.shape[0]

    def body(x_ref, y_ref, w_ref, g_ref, wf_ref, bf_ref, x1_ref, h1_ref, f_ref, cp_ref, ccar_ref):
        @pl.when(pl.program_id(0) == 0)
        def _():
            ccar_ref[...] = jnp.zeros_like(ccar_ref)

        x1 = x_ref[...] + _dot(y_ref[...], w_ref[...])
        h1 = (x1 * _rstd(x1) * g_ref[...]).astype(BF16)
        f = _dot(h1, wf_ref[...]) + bf_ref[...]
        logsig = jnp.minimum(f, 0.0) - jnp.log1p(jnp.exp(-jnp.abs(f)))
        cum = _cumsum_rows(logsig) + ccar_ref[SUBLANES - 1:SUBLANES, :]
        hi, mid, lo = _split3(cum)
        lane = lax.broadcasted_iota(jnp.int32, cum.shape, 1)
        packed = jnp.where(lane < HEADS, hi.astype(F32), jnp.where(
            lane < 2 * HEADS, pltpu.roll(mid.astype(F32), HEADS, 1), jnp.where(
                lane < 3 * HEADS, pltpu.roll(lo.astype(F32), 2 * HEADS, 1), 0.0)))
        x1_ref[...] = x1
        h1_ref[...] = h1
        f_ref[...] = f
        cp_ref[...] = packed.astype(BF16)
        ccar_ref[...] = cum[ts - SUBLANES:, :]

    return pl.pallas_call(
        body, name="fox_pre_fwd", grid=(s // ts,),
        in_specs=[pl.BlockSpec((ts, D_MODEL), lambda i: (i, 0)),
                  pl.BlockSpec((ts, LRU_WIDTH), lambda i: (i, 0)),
                  _const_spec((LRU_WIDTH, D_MODEL)),
                  _const_spec((1, D_MODEL)),
                  _const_spec((D_MODEL, LANES)),
                  _const_spec((1, LANES))],
        out_specs=[pl.BlockSpec((ts, D_MODEL), lambda i: (i, 0)),
                   pl.BlockSpec((ts, D_MODEL), lambda i: (i, 0)),
                   pl.BlockSpec((ts, LANES), lambda i: (i, 0)),
                   pl.BlockSpec((ts, LANES), lambda i: (i, 0))],
        out_shape=[jax.ShapeDtypeStruct((s, D_MODEL), F32),
                   jax.ShapeDtypeStruct((s, D_MODEL), BF16),
                   jax.ShapeDtypeStruct((s, LANES), F32),
                   jax.ShapeDtypeStruct((s, LANES), BF16)],
        scratch_shapes=[pltpu.VMEM((SUBLANES, LANES), F32)],
        compiler_params=_params(),
    )(x, y, w_out, g1, wf, bf)


def _fox_proj_fwd(h1, cparts, w, first, ng, sel, bias, out_dtype, ts, name):
    s = h1.shape[0]
    width = HEADS * HEAD_DIM
    use_sel = sel is not None

    def body(*refs):
        if use_sel:
            h_ref, cp_ref, w_ref, sel_ref, b_ref, o_ref = refs
            proj = _dot(h_ref[...], w_ref[...])
            if first == 0:
                proj = proj * jnp.where(pl.program_id(0) == 0, QK_SCALE, 1.0)
            acc = _heads_to_padded(proj) + _dot(cp_ref[...], sel_ref[...]) + b_ref[...]
        else:
            h_ref, w_ref, o_ref = refs
            acc = _heads_to_padded(_dot(h_ref[...], w_ref[...]))
        o_ref[...] = acc.astype(out_dtype)

    in_specs = [pl.BlockSpec((ts, D_MODEL), lambda j, i: (i, 0))]
    args = [h1]
    if use_sel:
        in_specs.append(pl.BlockSpec((ts, LANES), lambda j, i: (i, 0)))
        args.append(cparts)
    in_specs.append(pl.BlockSpec((D_MODEL, width), lambda j, i: (0, first + j)))
    args.append(w)
    if use_sel:
        in_specs.append(pl.BlockSpec((None, LANES, FOX_PAD), lambda j, i: (j, 0, 0)))
        in_specs.append(pl.BlockSpec((None, 1, FOX_PAD), lambda j, i: (j, 0, 0)))
        args += [sel, bias]
    return pl.pallas_call(
        body, name=name, grid=(ng, s // ts),
        in_specs=in_specs,
        out_specs=pl.BlockSpec((None, ts, FOX_PAD), lambda j, i: (j, i, 0)),
        out_shape=jax.ShapeDtypeStruct((ng, s, FOX_PAD), out_dtype),
        compiler_params=_params(2),
    )(*args)


def _attn_fwd(qkv, blk, hps=HEADS_PER_STEP):
    s = qkv.shape[1]
    nblk = s // blk
    wide = 2 * blk
    heads = [slice(i * HEAD_PAD, (i + 1) * HEAD_PAD) for i in range(hps)]

    def body(q_ref, k_ref, v_ref, o_ref, qb_ref, acc_ref, m_ref):
        qi = pl.program_id(1)
        row = lax.broadcasted_iota(jnp.int32, (blk, blk), 0)
        col = lax.broadcasted_iota(jnp.int32, (blk, blk), 1)
        lane = lax.broadcasted_iota(jnp.int32, (blk, HEAD_PAD), 1)
        qs = [q_ref[:, hd] for hd in heads]
        for i in range(hps):
            acc_ref[i] = jnp.zeros((blk, HEAD_PAD), F32)
            m_ref[i] = jnp.full((blk, HEAD_PAD), NEG_BIG, F32)

        def step(k0, size, masked):
            scores = [_dot_nt(q, k_ref[pl.ds(k0, size), hd]) for q, hd in zip(qs, heads)]
            for i, (sc, hd) in enumerate(zip(scores, heads)):
                v = v_ref[pl.ds(k0, size), hd]
                if masked:
                    sc = jnp.where(col <= row, sc, NEG_BIG)
                m = m_ref[i]
                m_new = jnp.maximum(m, jnp.max(sc, axis=-1, keepdims=True))
                p = jnp.exp((sc - jnp.tile(m_new, (1, size // HEAD_PAD))).astype(BF16))
                acc_ref[i] = jnp.exp(m - m_new) * acc_ref[i] + _dot(p, v)
                m_ref[i] = m_new

        def wide_step(kk, _):
            step(pl.multiple_of(kk * wide, wide), wide, False)
            return 0

        lax.fori_loop(0, qi // 2, wide_step, 0)

        @pl.when(qi % 2 == 1)
        def _():
            step(pl.multiple_of((qi - 1) * blk, blk), blk, False)

        step(pl.multiple_of(qi * blk, blk), blk, True)
        for i, (q, hd) in enumerate(zip(qs, heads)):
            acc = acc_ref[i]
            l = jnp.broadcast_to(acc[:, LANE_ONE_V:LANE_ONE_V + 1], (blk, HEAD_PAD))
            o_ref[:, hd] = (acc / l).astype(BF16)
            hi, mid, lo = _split3(-(m_ref[i] + jnp.log(l)))
            qb_ref[:, hd] = jnp.where(lane == LANE_LSE, hi, jnp.where(
                lane == LANE_LSE + 1, mid, jnp.where(lane == LANE_LSE + 2, lo, q)))

    width = hps * HEAD_PAD

    def whole(j):
        return pl.BlockSpec((None, s, width), lambda h, i: (j, 0, h))

    out_spec = pl.BlockSpec((blk, width), lambda h, i: (i, h))
    return pl.pallas_call(
        body, name="attn_fwd", grid=(HEADS // hps, nblk),
        in_specs=[pl.BlockSpec((None, blk, width), lambda h, i: (0, i, h)), whole(1), whole(2)],
        out_specs=[out_spec, out_spec],
        out_shape=[jax.ShapeDtypeStruct((s, FOX_PAD), BF16),
                   jax.ShapeDtypeStruct((s, FOX_PAD), BF16)],
        scratch_shapes=[pltpu.VMEM((hps, blk, HEAD_PAD), F32),
                        pltpu.VMEM((hps, blk, HEAD_PAD), F32)],
        compiler_params=_params(2),
    )(qkv, qkv, qkv)


def _fox_out_loss(o, gate, w_out, x1, target, gf, ts):
    s = x1.shape[0]

    def body(o_ref, gt_ref, w_ref, x1_ref, t_ref, g_ref, dx2_ref, dx2b_ref, y2_ref, loss_ref,
             gfin_ref):
        @pl.when(pl.program_id(0) == 0)
        def _():
            loss_ref[...] = jnp.zeros_like(loss_ref)
            gfin_ref[...] = jnp.zeros_like(gfin_ref)

        gv = gt_ref[...]
        y2 = _heads_from_padded(o_ref[...] * (gv * _sigmoid(gv))).astype(BF16)
        x2 = x1_ref[...] + _dot(y2, w_ref[...])
        rstd = _rstd(x2)
        xhat = x2 * rstd
        g = g_ref[...]
        diff = xhat * g - t_ref[...]
        loss_ref[...] += 0.5 * jnp.sum(jnp.mean(diff * diff, axis=-1, keepdims=True))
        dy = diff * (1.0 / D_MODEL)
        gfin_ref[...] += jnp.sum(dy * xhat, axis=0, keepdims=True)
        dxh = dy * g
        dx2 = rstd * (dxh - xhat * jnp.mean(dxh * xhat, axis=-1, keepdims=True))
        dx2_ref[...] = dx2
        dx2b_ref[...] = dx2.astype(BF16)
        y2_ref[...] = y2

    return pl.pallas_call(
        body, name="fox_out_loss", grid=(s // ts,),
        in_specs=[pl.BlockSpec((ts, FOX_PAD), lambda i: (i, 0)),
                  pl.BlockSpec((ts, FOX_PAD), lambda i: (i, 0)),
                  _const_spec((HEADS * HEAD_DIM, D_MODEL)),
                  pl.BlockSpec((ts, D_MODEL), lambda i: (i, 0)),
                  pl.BlockSpec((ts, D_MODEL), lambda i: (i, 0)),
                  _const_spec((1, D_MODEL))],
        out_specs=[pl.BlockSpec((ts, D_MODEL), lambda i: (i, 0)),
                   pl.BlockSpec((ts, D_MODEL), lambda i: (i, 0)),
                   pl.BlockSpec((ts, HEADS * HEAD_DIM), lambda i: (i, 0)),
                   pl.BlockSpec((SUBLANES, LANES), lambda i: (0, 0)),
                   pl.BlockSpec((1, D_MODEL), lambda i: (0, 0))],
        out_shape=[jax.ShapeDtypeStruct((s, D_MODEL), F32),
                   jax.ShapeDtypeStruct((s, D_MODEL), BF16),
                   jax.ShapeDtypeStruct((s, HEADS * HEAD_DIM), BF16),
                   jax.ShapeDtypeStruct((SUBLANES, LANES), F32),
                   jax.ShapeDtypeStruct((1, D_MODEL), F32)],
        compiler_params=_params(),
    )(o, gate, w_out, x1, target, gf)


def _fox_out_bwd(dx2, w_out, o, gate, ts):
    s = dx2.shape[0]

    def body(dx_ref, w_ref, o_ref, gt_ref, do_ref, dg_ref):
        lane = lax.broadcasted_iota(jnp.int32, (ts, HEAD_PAD), 1)
        dy2 = _heads_to_padded(_dot_nt(dx_ref[...], w_ref[...]))
        gv = gt_ref[...]
        sg = _sigmoid(gv)
        ov = o_ref[...]
        dov = dy2 * (gv * sg)
        dg_ref[...] = (dy2 * ov * (sg * (1.0 + gv * (1.0 - sg)))).astype(BF16)
        prod = dov * ov
        for h in range(HEADS):
            sl = slice(h * HEAD_PAD, (h + 1) * HEAD_PAD)
            delta = jnp.sum(prod[:, sl], axis=-1, keepdims=True)
            hi = delta.astype(BF16)
            lo = (delta - hi.astype(F32)).astype(BF16)
            do_h = dov[:, sl].astype(BF16)
            do_ref[:, sl] = jnp.where(lane == LANE_ONE_V, -hi,
                                      jnp.where(lane == LANE_ONE_V + 1, -lo, do_h))

    tile = pl.BlockSpec((ts, FOX_PAD), lambda i: (i, 0))
    return pl.pallas_call(
        body, name="fox_out_bwd", grid=(s // ts,),
        in_specs=[pl.BlockSpec((ts, D_MODEL), lambda i: (i, 0)),
                  _const_spec((HEADS * HEAD_DIM, D_MODEL)), tile, tile],
        out_specs=[tile, tile],
        out_shape=[jax.ShapeDtypeStruct((s, FOX_PAD), BF16),
                   jax.ShapeDtypeStruct((s, FOX_PAD), BF16)],
        compiler_params=_params(),
    )(dx2, w_out, o, gate)


def _attn_bwd(qb, qkv, do, blk):
    s = qb.shape[0]
    nblk = s // blk
    half = blk // 2
    heads = [slice(i * HEAD_PAD, (i + 1) * HEAD_PAD) for i in range(HEADS_PER_STEP)]

    def body(q_ref, k_ref, v_ref, do_ref, dq_ref, dk_ref, dv_ref, dcum_ref, dq_acc, dkt_acc,
             dvt_acc, qt_ref, dot_ref):
        group = pl.program_id(0)
        kj = pl.program_id(1)
        row = lax.broadcasted_iota(jnp.int32, (blk, blk), 0)
        col = lax.broadcasted_iota(jnp.int32, (blk, blk), 1)
        lane = lax.broadcasted_iota(jnp.int32, (blk, LANES), 1)
        mine = [lane == group * HEADS_PER_STEP + i for i in range(HEADS_PER_STEP)]

        @pl.when(kj == 0)
        def _():
            dq_acc[...] = jnp.zeros_like(dq_acc)

            def transpose_block(bi, _):
                r0 = pl.multiple_of(bi * blk, blk)
                for i, hd in enumerate(heads):
                    qt_ref[i, bi] = q_ref[pl.ds(r0, blk), hd].T
                    dot_ref[i, bi] = do_ref[pl.ds(r0, blk), hd].T
                return 0

            lax.fori_loop(0, nblk, transpose_block, 0)

        @pl.when((group == 0) & (kj == 0))
        def _():
            dcum_ref[...] = jnp.zeros_like(dcum_ref)

        k0 = pl.multiple_of(kj * blk, blk)
        ks = [k_ref[:, hd] for hd in heads]
        vs = [v_ref[:, hd] for hd in heads]

        def step(qi, q_lo, nq, k_lo, nk, masked):
            parts = ([(0, q_lo, 0, nq)] if nq <= blk
                     else [(b, 0, b * blk, blk) for b in range(nq // blk)])
            q0 = pl.multiple_of(qi * blk + q_lo, half)
            qs = [q_ref[pl.ds(q0, nq), hd] for hd in heads]
            dos = [do_ref[pl.ds(q0, nq), hd] for hd in heads]
            kk = [k[k_lo:k_lo + nk] for k in ks]
            vv = [v[k_lo:k_lo + nk] for v in vs]
            scores = [_dot_nt(q, k) for q, k in zip(qs, kk)]
            dps = [_dot_nt(dov, v) for dov, v in zip(dos, vv)]
            for i, (hd, k, sc, dp) in enumerate(zip(heads, kk, scores, dps)):
                p = jnp.exp(sc.astype(BF16))
                if masked:
                    p = jnp.where(col[:nq, :nk] + k_lo <= row[:nq, :nk] + q_lo, p,
                                  jnp.zeros_like(p))
                ds = (p.astype(F32) * dp).astype(BF16)
                dvt = sum(_dot(dot_ref[i, qi + b, :, c:c + n], p[r:r + n]) for b, c, r, n in parts)
                dkt = sum(_dot(qt_ref[i, qi + b, :, c:c + n], ds[r:r + n]) for b, c, r, n in parts)
                if masked:
                    dvt_acc[i, :, k_lo:k_lo + nk] = dvt
                    dkt_acc[i, :, k_lo:k_lo + nk] = dkt
                else:
                    dvt_acc[i, :, k_lo:k_lo + nk] += dvt
                    dkt_acc[i, :, k_lo:k_lo + nk] += dkt
                dq_acc[pl.ds(q0, nq), hd] += _dot(ds, k)

        step(kj, 0, blk, 0, half, True)
        step(kj, half, half, half, half, True)

        n_after = nblk - 1 - kj

        def q_step(t, _):
            step(kj + 1 + 2 * t, 0, 2 * blk, 0, blk, False)
            return 0

        lax.fori_loop(0, n_after // 2, q_step, 0)

        @pl.when(n_after % 2 == 1)
        def _():
            step(nblk - 1, 0, blk, 0, blk, False)
        dcum = dcum_ref[pl.ds(k0, blk), :]
        for i, (hd, mask) in enumerate(zip(heads, mine)):
            dk = dkt_acc[i].T
            dk_ref[:, hd] = dk.astype(BF16)
            dv_ref[:, hd] = dvt_acc[i].astype(BF16).T
            dcum = jnp.where(mask, -dk[:, LANE_CK:LANE_CK + 1], dcum)
        dcum_ref[pl.ds(k0, blk), :] = dcum

        @pl.when(kj == nblk - 1)
        def _():
            def finish(bi, _):
                r0 = pl.multiple_of(bi * blk, blk)
                dcum = dcum_ref[pl.ds(r0, blk), :]
                for hd, mask in zip(heads, mine):
                    dq = dq_acc[pl.ds(r0, blk), hd]
                    dq_ref[pl.ds(r0, blk), hd] = dq.astype(BF16)
                    dcum = dcum + jnp.where(mask, dq[:, LANE_RB:LANE_RB + 1], 0.0)
                dcum_ref[pl.ds(r0, blk), :] = dcum
                return 0

            lax.fori_loop(0, nblk, finish, 0)

    width = HEADS_PER_STEP * HEAD_PAD
    whole = pl.BlockSpec((s, width), lambda h, j: (0, h))
    part = pl.BlockSpec((blk, width), lambda h, j: (j, h))
    out = jax.ShapeDtypeStruct((s, FOX_PAD), BF16)
    return pl.pallas_call(
        body, name="attn_bwd", grid=(HEADS // HEADS_PER_STEP, nblk),
        in_specs=[whole,
                  pl.BlockSpec((None, blk, width), lambda h, j: (1, j, h)),
                  pl.BlockSpec((None, blk, width), lambda h, j: (2, j, h)),
                  whole],
        out_specs=[whole, part, part, pl.BlockSpec((s, LANES), lambda h, j: (0, 0))],
        out_shape=[out, out, out, jax.ShapeDtypeStruct((s, LANES), F32)],
        scratch_shapes=[pltpu.VMEM((s, width), F32),
                        pltpu.VMEM((HEADS_PER_STEP, HEAD_PAD, blk), F32),
                        pltpu.VMEM((HEADS_PER_STEP, HEAD_PAD, blk), F32),
                        pltpu.VMEM((HEADS_PER_STEP, nblk, HEAD_PAD, blk), BF16),
                        pltpu.VMEM((HEADS_PER_STEP, nblk, HEAD_PAD, blk), BF16)],
        compiler_params=_params(2),
    )(qb, qkv, qkv, do)


def _fox_in_bwd(dq, dk, dv, dg, wt, wft, dcum, f, x1, dx2, g1, ts):
    s = x1.shape[0]
    nt = s // ts
    width = HEADS * HEAD_DIM

    def body(dq_ref, dk_ref, dv_ref, dg_ref, wt_ref, wft_ref, dcum_ref, f_ref, x1_ref, dx2_ref,
             g_ref, dx1_ref, dx1b_ref, df_ref, duq_ref, duk_ref, duv_ref, dug_ref, gn_ref, gbf_ref,
             rcar_ref):
        du_refs = (duq_ref, duk_ref, duv_ref, dug_ref)


        @pl.when(pl.program_id(0) == 0)
        def _():
            rcar_ref[...] = jnp.zeros_like(rcar_ref)
            gn_ref[...] = jnp.zeros_like(gn_ref)
            gbf_ref[...] = jnp.zeros_like(gbf_ref)

        rsum = _cumsum_rows(dcum_ref[...], reverse=True) + rcar_ref[0:1, :]
        df = rsum * _sigmoid(-f_ref[...])
        dfb = df.astype(BF16)
        dh = _dot_nt(dfb, wft_ref[...])
        for j, ref in enumerate((dq_ref, dk_ref, dv_ref, dg_ref)):
            du = _heads_from_padded(ref[...])
            du_refs[j][...] = du
            if j == 0:
                du = du * QK_SCALE
            dh = dh + _dot_nt(du, wt_ref[:, j * width:(j + 1) * width])
        dxn, dgn = _norm_bwd(x1_ref[...], g_ref[...], dh)
        dx1 = dx2_ref[...] + dxn
        dx1_ref[...] = dx1
        dx1b_ref[...] = dx1.astype(BF16)
        df_ref[...] = dfb
        gn_ref[...] += dgn
        gbf_ref[...] += jnp.sum(df, axis=0, keepdims=True)
        rcar_ref[...] = rsum[0:SUBLANES, :]

    rev = lambda i: (nt - 1 - i, 0)
    wide = pl.BlockSpec((ts, FOX_PAD), rev)
    return pl.pallas_call(
        body, name="fox_in_bwd", grid=(nt,),
        in_specs=[wide, wide, wide, wide,
                  _const_spec((D_MODEL, FOX_IN_COLS)),
                  _const_spec((D_MODEL, LANES)),
                  pl.BlockSpec((ts, LANES), rev),
                  pl.BlockSpec((ts, LANES), rev),
                  pl.BlockSpec((ts, D_MODEL), rev),
                  pl.BlockSpec((ts, D_MODEL), rev),
                  _const_spec((1, D_MODEL))],
        out_specs=[pl.BlockSpec((ts, D_MODEL), rev),
                   pl.BlockSpec((ts, D_MODEL), rev),
                   pl.BlockSpec((ts, LANES), rev)]
        + [pl.BlockSpec((ts, width), rev)] * 4
        + [pl.BlockSpec((1, D_MODEL), lambda i: (0, 0)),
           pl.BlockSpec((1, LANES), lambda i: (0, 0))],
        out_shape=[jax.ShapeDtypeStruct((s, D_MODEL), F32),
                   jax.ShapeDtypeStruct((s, D_MODEL), BF16),
                   jax.ShapeDtypeStruct((s, LANES), BF16)]
        + [jax.ShapeDtypeStruct((s, width), BF16)] * 4
        + [jax.ShapeDtypeStruct((1, D_MODEL), F32),
                   jax.ShapeDtypeStruct((1, LANES), F32)],
        scratch_shapes=[pltpu.VMEM((SUBLANES, LANES), F32)],
        compiler_params=_params(),
    )(dq, dk, dv, dg, wt, wft, dcum, f, x1, dx2, g1)


def _lru_core_bwd(dx1b, w_out, xb, gate, hs, cw, cb, wa, ba, wx, bx, a_param, wa_t, wx_t,
                  chip_sums, ts):
    s = xb.shape[0]
    nt = s // ts
    tpb = ts // SUBLANES
    n_ex = len(chip_sums)

    def body(*refs):
        (dx_ref, wo_ref, xb_ref, xbh_ref, gate_ref, hs_ref, hsh_ref, cw_ref, cb_ref, wa_ref,
         ba_ref, wx_ref, bx_ref, ap_ref, wat_ref, wxt_ref) = refs[:16]
        sum_refs = refs[16:16 + n_ex]
        du_ref, gwa_ref, gwx_ref, gvec_ref = refs[16 + n_ex:20 + n_ex]
        got_refs = refs[20 + n_ex:20 + 2 * n_ex]
        acar_ref, dhcar_ref, dxccar_ref = refs[20 + 2 * n_ex:23 + 2 * n_ex]
        start, finish = _chip_exchange_phases(sum_refs, got_refs, *refs[23 + 2 * n_ex:])
        step = pl.program_id(0)
        pl.when(step == 0)(start)

        @pl.when(step == 0)
        def _():
            acar_ref[...] = jnp.zeros_like(acar_ref)
            dhcar_ref[...] = jnp.zeros_like(dhcar_ref)
            dxccar_ref[...] = jnp.zeros_like(dxccar_ref)
            gwa_ref[...] = jnp.zeros_like(gwa_ref)
            gwx_ref[...] = jnp.zeros_like(gwx_ref)
            gvec_ref[...] = jnp.zeros_like(gvec_ref)

        first_tile = step == nt - 1
        halo_on = jnp.where(first_tile, 0.0, 1.0)
        prev8 = xbh_ref[...] * halo_on
        hprev_row = hsh_ref[SUBLANES - 1:SUBLANES, :] * halo_on

        xbv = xb_ref[...]
        taps = _conv_taps(xbv, prev8)
        cw_v = cw_ref[...]
        xc, xcb, r, i, sp, a, mult = _lru_pre(taps, cw_v, cb_ref[...], wa_ref, ba_ref[...],
                                              wx_ref, bx_ref[...], ap_ref[...])
        hs = hs_ref[...]
        gv = gate_ref[...]
        sg = _sigmoid(gv)
        dy = _dot_nt(dx_ref[...], wo_ref[...])
        dhs = dy * (gv * sg)
        dgate = dy * hs * (sg * (1.0 + gv * (1.0 - sg)))

        rows = lax.broadcasted_iota(jnp.int32, a.shape, 0)
        a_next = jnp.where(rows < ts - 1, pltpu.roll(a, ts - 1, 0), acar_ref[0:1, :])
        cum_a, dh_loc = _scan_rows(a_next, dhs, reverse=True)
        dh = cum_a * dhcar_ref[0:1, :] + dh_loc
        h_prev = jnp.where(rows >= 1, pltpu.roll(hs, 1, 0), hprev_row)

        da = dh * h_prev
        ixc = i * xc
        dmult = dh * ixc
        di = dh * mult * xc
        dxc = dh * mult * i
        dlog_a = da * a - dmult * (a * a) / mult
        dr = dlog_a * ((-LRU_C) * sp)
        dsp = jnp.sum(dlog_a * ((-LRU_C) * r), axis=0, keepdims=True)
        dra = dr * r * (1.0 - r)
        dia = di * i * (1.0 - i)
        drab = dra.astype(BF16)
        diab = dia.astype(BF16)
        back = []
        for n in range(LRU_BLOCKS):
            sl = slice(n * LRU_BLOCK_W, (n + 1) * LRU_BLOCK_W)
            gwa_ref[n] += _dot_tn(xcb[:, sl], drab[:, sl])
            gwx_ref[n] += _dot_tn(xcb[:, sl], diab[:, sl])
            back.append(_dot(drab[:, sl], wat_ref[n]) + _dot(diab[:, sl], wxt_ref[n]))
        dxc = dxc + jnp.concatenate(back, axis=1)

        nxt8 = dxccar_ref[...]
        rows8 = lax.broadcasted_iota(jnp.int32, nxt8.shape, 0)
        dxb = cw_v[3:4] * dxc
        for j in range(1, CONV_WIDTH):
            rj = pltpu.roll(dxc, ts - j, 0)
            pj = pltpu.roll(nxt8, SUBLANES - j, 0)
            tail = jnp.where(rows8 >= SUBLANES - j, pj, rj[ts - SUBLANES:])
            dxb = dxb + cw_v[3 - j:4 - j] * jnp.concatenate([rj[:ts - SUBLANES], tail], axis=0)

        du_ref[:, :LRU_WIDTH] = dxb.astype(BF16)
        du_ref[:, LRU_WIDTH:] = dgate.astype(BF16)

        z = -ap_ref[...]
        gvec = [jnp.sum(dxc * taps[3 - k], axis=0, keepdims=True) for k in range(CONV_WIDTH)]
        gvec.append(jnp.sum(dxc, axis=0, keepdims=True))
        gvec.append(jnp.sum(dra, axis=0, keepdims=True))
        gvec.append(jnp.sum(dia, axis=0, keepdims=True))
        gvec.append(-dsp * _sigmoid(z))
        gvec_ref[...] += jnp.concatenate(gvec, axis=0)

        acar_ref[...] = a[0:SUBLANES, :]
        dhcar_ref[...] = dh[0:SUBLANES, :]
        dxccar_ref[...] = dxc[0:SUBLANES, :]
        pl.when(step == nt - 1)(finish)

    rev = lambda i: (nt - 1 - i, 0)
    halo = lambda i: (jnp.maximum((nt - 1 - i) * tpb - 1, 0), 0)
    tile = pl.BlockSpec((ts, LRU_WIDTH), rev)
    halo_spec = pl.BlockSpec((SUBLANES, LRU_WIDTH), halo)
    vec = _const_spec((1, LRU_WIDTH))
    blk = _const_spec((LRU_BLOCKS, LRU_BLOCK_W, LRU_BLOCK_W))
    acc_blk = pl.BlockSpec((LRU_BLOCKS, LRU_BLOCK_W, LRU_BLOCK_W), lambda i: (0, 0, 0))
    hbm = pl.BlockSpec(memory_space=pl.ANY)
    res = pl.pallas_call(
        body, name="lru_core_bwd", grid=(nt,),
        in_specs=[pl.BlockSpec((ts, D_MODEL), rev),
                  _const_spec((LRU_WIDTH, D_MODEL)),
                  tile, halo_spec, tile, tile, halo_spec,
                  _const_spec((CONV_WIDTH, LRU_WIDTH)), vec, blk, vec, blk, vec, vec, blk, blk]
        + [hbm] * n_ex,
        out_specs=[pl.BlockSpec((ts, 2 * LRU_WIDTH), rev), acc_blk, acc_blk,
                   pl.BlockSpec((SUBLANES, LRU_WIDTH), lambda i: (0, 0))] + [hbm] * n_ex,
        out_shape=[jax.ShapeDtypeStruct((s, 2 * LRU_WIDTH), BF16),
                   jax.ShapeDtypeStruct((LRU_BLOCKS, LRU_BLOCK_W, LRU_BLOCK_W), F32),
                   jax.ShapeDtypeStruct((LRU_BLOCKS, LRU_BLOCK_W, LRU_BLOCK_W), F32),
                   jax.ShapeDtypeStruct((SUBLANES, LRU_WIDTH), F32)]
        + [jax.ShapeDtypeStruct(a.shape, a.dtype) for a in chip_sums],
        scratch_shapes=[pltpu.VMEM((SUBLANES, LRU_WIDTH), F32),
                        pltpu.VMEM((SUBLANES, LRU_WIDTH), F32),
                        pltpu.VMEM((SUBLANES, LRU_WIDTH), F32)] + _chip_exchange_sems(n_ex),
        compiler_params=_params(),
    )(dx1b, w_out, xb, xb, gate, hs, hs, cw, cb, wa, ba, wx, bx, a_param, wa_t, wx_t, *chip_sums)
    return res[0], res[1], res[2], res[3], res[4:]


def _lru_in_bwd(du, w_in, x, dx1, g0, chip_sums, ts):
    s = x.shape[0]
    nt = s // ts
    n = len(chip_sums)

    def body(*refs):
        du_ref, w_ref, x_ref, dx1_ref, g_ref = refs[:5]
        sum_refs = refs[5:5 + n]
        gx_ref, gn_ref = refs[5 + n:7 + n]
        got_refs = refs[7 + n:7 + 2 * n]
        wfull_ref = refs[7 + 2 * n]
        start, finish = _chip_exchange_phases(sum_refs, got_refs, *refs[8 + 2 * n:])
        step = pl.program_id(0)
        pl.when(step == 0)(start)

        @pl.when(step == 0)
        def _():
            gn_ref[...] = jnp.zeros_like(gn_ref)
            for j in range(N_DEV):
                wfull_ref[:, j * LRU_IN_SHARD:(j + 1) * LRU_IN_SHARD] = w_ref[j]

        dh = _dot_nt(du_ref[...], wfull_ref[...])
        dxn, dgn = _norm_bwd(x_ref[...], g_ref[...], dh)
        gx_ref[...] = dx1_ref[...] + dxn
        gn_ref[...] += dgn
        pl.when(step == nt - 1)(finish)

    tile = pl.BlockSpec((ts, D_MODEL), lambda i: (i, 0))
    hbm = pl.BlockSpec(memory_space=pl.ANY)
    res = pl.pallas_call(
        body, name="lru_in_bwd", grid=(nt,),
        in_specs=[pl.BlockSpec((ts, 2 * LRU_WIDTH), lambda i: (i, 0)),
                  _const_spec((N_DEV, D_MODEL, LRU_IN_SHARD)), tile, tile,
                  _const_spec((1, D_MODEL))] + [hbm] * n,
        out_specs=[tile, pl.BlockSpec((1, D_MODEL), lambda i: (0, 0))] + [hbm] * n,
        out_shape=[jax.ShapeDtypeStruct((s, D_MODEL), F32),
                   jax.ShapeDtypeStruct((1, D_MODEL), F32)]
        + [jax.ShapeDtypeStruct(a.shape, a.dtype) for a in chip_sums],
        scratch_shapes=[pltpu.VMEM((D_MODEL, 2 * LRU_WIDTH), BF16)] + _chip_exchange_sems(n),
        compiler_params=_params(),
    )(du, w_in, x, dx1, g0, *chip_sums)
    return res[0], res[1], res[2:]


def _weight_grad(a, b, ts, name, scale=1.0, col_shards=1):
    s, ka = a.shape
    nb = b.shape[1]
    nt = s // ts
    per = nb // col_shards

    def body(a_ref, b_ref, o_ref):
        @pl.when(pl.program_id(0) == 0)
        def _():
            o_ref[...] = jnp.zeros_like(o_ref)

        if col_shards == 1:
            o_ref[...] += _dot_tn(a_ref[...], b_ref[...])
        else:
            acc = _dot_tn(a_ref[...], b_ref[...])
            for j in range(col_shards):
                o_ref[j] += acc[:, j * per:(j + 1) * per]
        if scale != 1.0:
            @pl.when(pl.program_id(0) == nt - 1)
            def _():
                o_ref[...] = o_ref[...] * scale

    out_dims = (ka, nb) if col_shards == 1 else (col_shards, ka, per)
    return pl.pallas_call(
        body, name=name, grid=(nt,),
        in_specs=[pl.BlockSpec((ts, ka), lambda i: (i, 0)),
                  pl.BlockSpec((ts, nb), lambda i: (i, 0))],
        out_specs=pl.BlockSpec(out_dims, lambda i: (0,) * len(out_dims)),
        out_shape=jax.ShapeDtypeStruct(out_dims, F32),
        compiler_params=_params(),
    )(a, b)


def _sum_parts(gp_ref):
    g = gp_ref[0].astype(F32)
    for k in range(1, gp_ref.shape[0]):
        g = g + gp_ref[k].astype(F32)
    return g


def _adamw(g_parts, w, m, v, tr, name):
    nparts, rows, cols = g_parts.shape

    def body(gp_ref, w_ref, m_ref, v_ref, g_ref, d_ref, mo_ref, vo_ref):
        g = _sum_parts(gp_ref)
        m2 = ADAM_B1 * m_ref[...] + (1.0 - ADAM_B1) * g
        v2 = ADAM_B2 * v_ref[...] + (1.0 - ADAM_B2) * (g * g)
        m_hat = m2 / (1.0 - ADAM_B1 ** ADAM_STEP)
        v_hat = v2 / (1.0 - ADAM_B2 ** ADAM_STEP)
        g_ref[...] = g
        d_ref[...] = (-ADAM_LR) * (m_hat / (jnp.sqrt(v_hat) + ADAM_EPS) + ADAM_WD * w_ref[...])
        mo_ref[...] = m2
        vo_ref[...] = v2

    tile = pl.BlockSpec((tr, cols), lambda i: (i, 0))
    out = jax.ShapeDtypeStruct((rows, cols), F32)
    return pl.pallas_call(
        body, name=name, grid=(rows // tr,),
        in_specs=[pl.BlockSpec((nparts, tr, cols), lambda i: (0, i, 0)), tile, tile, tile],
        out_specs=[tile, tile, tile, tile],
        out_shape=[out, out, out, out],
        compiler_params=_params(),
    )(g_parts, w, m, v)


def _reduce_parts(g_parts, name):
    _, rows, cols = g_parts.shape

    def body(gp_ref, g_ref):
        g_ref[...] = _sum_parts(gp_ref)

    return pl.pallas_call(
        body, name=name,
        out_shape=jax.ShapeDtypeStruct((rows, cols), F32),
        compiler_params=pltpu.CompilerParams(vmem_limit_bytes=VMEM_LIMIT_BYTES),
    )(g_parts)


def _mesh_pos():
    ix, iy, ic = lax.axis_index("x"), lax.axis_index("y"), lax.axis_index("c")
    return ix, iy, ic


def _peer(ix, iy, ic, mask):
    px = 1 - ix if mask & 4 else ix
    py = 1 - iy if mask & 2 else iy
    pc = 1 - ic if mask & 1 else ic
    return (px, py, pc), 4 * px + 2 * py + pc


def _exchange(arrays, scatter, name):
    n = len(arrays)

    def body(*refs):
        x_refs, o_refs = refs[:n], refs[n:2 * n]
        send_sems, recv_sems, local_sems = refs[2 * n:]
        ix, iy, ic = _mesh_pos()
        me = 4 * ix + 2 * iy + ic

        def src(a, dest):
            return x_refs[a].at[dest] if scatter else x_refs[a]

        local = [pltpu.make_async_copy(src(a, me), o_refs[a].at[me], local_sems.at[a])
                 for a in range(n)]
        for cp in local:
            cp.start()
        sends = []
        for mask in range(1, N_DEV):
            peer, pidx = _peer(ix, iy, ic, mask)
            for a in range(n):
                cp = pltpu.make_async_remote_copy(
                    src_ref=src(a, pidx), dst_ref=o_refs[a].at[me],
                    send_sem=send_sems.at[a, mask - 1], recv_sem=recv_sems.at[a, mask - 1],
                    device_id=peer, device_id_type=pl.DeviceIdType.MESH)
                cp.start()
                sends.append(cp)
        for mask in range(1, N_DEV):
            peer, pidx = _peer(ix, iy, ic, mask)
            for a in range(n):
                pltpu.make_async_remote_copy(
                    src_ref=src(a, me), dst_ref=o_refs[a].at[pidx],
                    send_sem=send_sems.at[a, mask - 1], recv_sem=recv_sems.at[a, mask - 1],
                    device_id=peer, device_id_type=pl.DeviceIdType.MESH).wait_recv()
        for cp in sends:
            cp.wait_send()
        for cp in local:
            cp.wait()

    out_shape = [jax.ShapeDtypeStruct(x.shape if scatter else (N_DEV,) + x.shape, x.dtype)
                 for x in arrays]
    return pl.pallas_call(
        body, name=name,
        in_specs=[pl.BlockSpec(memory_space=pl.ANY)] * n,
        out_specs=[pl.BlockSpec(memory_space=pl.ANY)] * n,
        out_shape=out_shape,
        scratch_shapes=[pltpu.SemaphoreType.DMA((n, N_DEV - 1)),
                        pltpu.SemaphoreType.DMA((n, N_DEV - 1)),
                        pltpu.SemaphoreType.DMA((n,))],
    )(*arrays)


def _gather_two_level(arrays, name):
    n = len(arrays)

    def body(*refs):
        start, forward, finish = _gather_phases(refs[:n], refs[n:2 * n], *refs[2 * n:])
        start()
        forward()
        finish()

    return pl.pallas_call(
        body, name=name,
        in_specs=[pl.BlockSpec(memory_space=pl.ANY)] * n,
        out_specs=[pl.BlockSpec(memory_space=pl.ANY)] * n,
        out_shape=[jax.ShapeDtypeStruct((N_DEV,) + x.shape, x.dtype) for x in arrays],
        scratch_shapes=_gather_sems(n),
    )(*arrays)


def _gather_sems(n):
    return [pltpu.SemaphoreType.DMA((n, N_DEV - 1)), pltpu.SemaphoreType.DMA((n, N_DEV - 1)),
            pltpu.SemaphoreType.DMA((n,))]


def _gather_phases(x_refs, o_refs, send_sems, recv_sems, local_sems):
    n = len(x_refs)
    ix, iy, ic = _mesh_pos()
    me, sibling = (ix, iy, ic), (ix, iy, 1 - ic)
    chips = [(1 - ix, iy), (ix, 1 - iy), (1 - ix, 1 - iy)]

    def idx(px, py, pc):
        return 4 * px + 2 * py + pc

    def copy(a, k, block, to, src=None):
        dst = o_refs[a].at[idx(*block)]
        return pltpu.make_async_remote_copy(
            src_ref=dst if src is None else src, dst_ref=dst,
            send_sem=send_sems.at[a, k], recv_sem=recv_sems.at[a, k],
            device_id=to, device_id_type=pl.DeviceIdType.MESH)

    def local():
        return [pltpu.make_async_copy(x_refs[a], o_refs[a].at[idx(*me)], local_sems.at[a])
                for a in range(n)]

    def first():
        out = []
        for a in range(n):
            out.append(copy(a, 0, me, sibling, src=x_refs[a]))
            out += [copy(a, 1 + j, me, (*chip, ic), src=x_refs[a])
                    for j, chip in enumerate(chips)]
        return out

    def passed():
        return [copy(a, 4 + j, (*chip, ic), sibling)
                for j, chip in enumerate(chips) for a in range(n)]

    def start():
        for cp in local() + first():
            cp.start()

    def forward():
        for j, chip in enumerate(chips):
            for a in range(n):
                copy(a, 1 + j, (*chip, ic), me).wait_recv()
                copy(a, 4 + j, (*chip, ic), sibling).start()

    def finish():
        for a in range(n):
            copy(a, 0, sibling, me).wait_recv()
            for j, chip in enumerate(chips):
                copy(a, 4 + j, (*chip, 1 - ic), me).wait_recv()
        for cp in first() + passed():
            cp.wait_send()
        for cp in local():
            cp.wait()

    return start, forward, finish


def _swap_sibling(arrays, name):
    n = len(arrays)
    n_chips = N_DEV // 2

    def body(*refs):
        x_refs, got_refs = refs[:n], refs[n:2 * n]
        send_sems, recv_sems = refs[2 * n:]
        ix, iy, ic = _mesh_pos()
        sibling = (ix, iy, 1 - ic)
        sends = []
        for a in range(n):
            for q in range(n_chips):
                cp = pltpu.make_async_remote_copy(
                    src_ref=x_refs[a].at[q, 1 - ic], dst_ref=got_refs[a].at[q],
                    send_sem=send_sems.at[a, q], recv_sem=recv_sems.at[a, q],
                    device_id=sibling, device_id_type=pl.DeviceIdType.MESH)
                cp.start()
                sends.append(cp)
        for cp in sends:
            cp.wait()

    return pl.pallas_call(
        body, name=name,
        in_specs=[pl.BlockSpec(memory_space=pl.ANY)] * n,
        out_specs=[pl.BlockSpec(memory_space=pl.ANY)] * n,
        out_shape=[jax.ShapeDtypeStruct((n_chips,) + x.shape[2:], x.dtype) for x in arrays],
        scratch_shapes=[pltpu.SemaphoreType.DMA((n, n_chips)),
                        pltpu.SemaphoreType.DMA((n, n_chips))],
    )(*arrays)


def _exchange_chips(arrays, name):
    n = len(arrays)

    def body(*refs):
        start, finish = _chip_exchange_phases(refs[:n], refs[n:2 * n], *refs[2 * n:])
        start()
        finish()

    return pl.pallas_call(
        body, name=name,
        in_specs=[pl.BlockSpec(memory_space=pl.ANY)] * n,
        out_specs=[pl.BlockSpec(memory_space=pl.ANY)] * n,
        out_shape=[jax.ShapeDtypeStruct(x.shape, x.dtype) for x in arrays],
        scratch_shapes=_chip_exchange_sems(n),
    )(*arrays)


def _chip_exchange_sems(n):
    n_chips = N_DEV // 2
    return [pltpu.SemaphoreType.DMA((n, n_chips - 1)), pltpu.SemaphoreType.DMA((n, n_chips - 1)),
            pltpu.SemaphoreType.DMA((n,))]


def _chip_exchange_phases(x_refs, o_refs, send_sems, recv_sems, local_sems):
    n = len(x_refs)
    n_chips = N_DEV // 2
    ix, iy, ic = _mesh_pos()
    my_chip = 2 * ix + iy

    def peers():
        for mask in range(1, n_chips):
            px = 1 - ix if mask & 2 else ix
            py = 1 - iy if mask & 1 else iy
            yield mask, (px, py, ic), 2 * px + py

    def local():
        return [pltpu.make_async_copy(x_refs[a].at[my_chip], o_refs[a].at[my_chip],
                                      local_sems.at[a]) for a in range(n)]

    def sends():
        return [pltpu.make_async_remote_copy(
            src_ref=x_refs[a].at[chip], dst_ref=o_refs[a].at[my_chip],
            send_sem=send_sems.at[a, mask - 1], recv_sem=recv_sems.at[a, mask - 1],
            device_id=peer, device_id_type=pl.DeviceIdType.MESH)
            for mask, peer, chip in peers() for a in range(n)]

    def start():
        for cp in local() + sends():
            cp.start()

    def finish():
        for mask, peer, chip in peers():
            for a in range(n):
                pltpu.make_async_remote_copy(
                    src_ref=x_refs[a].at[my_chip], dst_ref=o_refs[a].at[chip],
                    send_sem=send_sems.at[a, mask - 1], recv_sem=recv_sems.at[a, mask - 1],
                    device_id=peer, device_id_type=pl.DeviceIdType.MESH).wait_recv()
        for cp in sends():
            cp.wait_send()
        for cp in local():
            cp.wait()

    return start, finish


def _pair_sum(core, x, got, name):
    nq, rows, cols = got.shape

    def body(c_ref, x_ref, g_ref, o_ref):
        o_ref[...] = (x_ref[...] + g_ref[...]).astype(BF16)

    blk = pl.BlockSpec((None, rows, cols), lambda q, c: (q, 0, 0))
    return pl.pallas_call(
        body, name=name,
        grid_spec=pltpu.PrefetchScalarGridSpec(
            num_scalar_prefetch=1, grid=(nq,),
            in_specs=[pl.BlockSpec((None, None, rows, cols), lambda q, c: (q, c[0], 0, 0)), blk],
            out_specs=blk),
        out_shape=jax.ShapeDtypeStruct(got.shape, BF16),
        compiler_params=_params(),
    )(core, x, got)


def _selectors():
    r = lax.broadcasted_iota(jnp.int32, (LANES, FOX_PAD), 0)
    c = lax.broadcasted_iota(jnp.int32, (LANES, FOX_PAD), 1)
    part, head_r = r // HEADS, r % HEADS
    head_c, lane_c = c // HEAD_PAD, c % HEAD_PAD
    same = (head_r == head_c) & (part < 3)
    sel_q = jnp.where(same & (lane_c == LANE_RB + part), 1.0, 0.0)
    sel_k = jnp.where(same & (lane_c == LANE_CK + part), -1.0, 0.0)
    sel = jnp.stack([sel_q, sel_k, jnp.zeros_like(sel_q)]).astype(BF16)
    lane = lax.broadcasted_iota(jnp.int32, (1, FOX_PAD), 1) % HEAD_PAD
    ones_q = jnp.where((lane >= LANE_CK) & (lane < LANE_CK + 3), 1.0, 0.0)
    ones_k = jnp.where(((lane >= LANE_RB) & (lane < LANE_RB + 3))
                       | ((lane >= LANE_LSE) & (lane < LANE_LSE + 3)), 1.0, 0.0)
    ones_v = jnp.where((lane >= LANE_ONE_V) & (lane < LANE_ONE_V + 2), 1.0, 0.0)
    bias = jnp.stack([ones_q, ones_k, ones_v]).astype(F32)
    return sel, bias


def _chip_sums(names, send):
    send = [a.reshape((N_DEV // 2, 2) + a.shape[1:]) for a in send]
    got = _swap_sibling(send, "swap_" + names[0])
    core = lax.axis_index("c").astype(jnp.int32).reshape(1)
    return [_pair_sum(core, a, b, "pair_sum_" + n) for n, a, b in zip(names, send, got)]


def _local_step(x, target, norm_g, final_g, w_in8, conv_w, conv_b, wa, ba, wx, bx, a_param,
                w_out_b, fox_in_shard, b_f, fox_out_shard, blk=512, ts=256):
    g0, g1 = norm_g[0:1], norm_g[1:2]
    gf = final_g.reshape(1, D_MODEL)
    wa_b, wx_b = wa.astype(BF16), wx.astype(BF16)
    sel, bias = _selectors()

    xb, gate1, h0, (fox_in8, fox_out8) = _lru_in_fwd(x, g0, w_in8, [fox_in_shard, fox_out_shard],
                                                     ts)
    fox_w_in = jnp.transpose(fox_in8, (1, 0, 2)).reshape(D_MODEL, FOX_IN_COLS)
    width = HEADS * HEAD_DIM
    wf_b = jnp.pad(fox_w_in[:, 4 * width:], ((0, 0), (0, LANES - HEADS)))
    bf_pad = jnp.pad(b_f, ((0, 0), (0, LANES - HEADS)))
    fo_b = fox_out8.reshape(width, D_MODEL)
    y1, hs = _lru_core_fwd(xb, gate1, conv_w, conv_b, wa_b, ba, wx_b, bx, a_param, ts)
    x1, h1, f, cparts = _fox_pre_fwd(x, y1, w_out_b, g1, wf_b, bf_pad, ts)
    qkv = _fox_proj_fwd(h1, cparts, fox_w_in, 0, 3, sel, bias, BF16, ts, "fox_proj_qkv")
    gate2 = _fox_proj_fwd(h1, None, fox_w_in, 3, 1, None, None, F32, ts, "fox_proj_gate")[0]
    o, qb = _attn_fwd(qkv, blk, hps=4)
    dx2, dx2b, y2, loss_acc, g_final = _fox_out_loss(o, gate2, fo_b, x1, target, gf, ts)

    do, dgate2 = _fox_out_bwd(dx2b, fo_b, o, gate2, ts)
    dq, dk, dv, dcum = _attn_bwd(qb, qkv, do, blk)
    dx1, dx1b, df, du_q, du_k, du_v, du_g, g_norm1, g_bf = _fox_in_bwd(
        dq, dk, dv, dgate2, fox_w_in, wf_b, dcum, f, x1, dx2, g1, ts)
    tw = 512
    g_q = _weight_grad(h1, du_q, tw, "grad_fox_wq", scale=QK_SCALE)
    g_k = _weight_grad(h1, du_k, tw, "grad_fox_wk")
    g_v = _weight_grad(h1, du_v, tw, "grad_fox_wv")
    g_g = _weight_grad(h1, du_g, tw, "grad_fox_wg")
    g_f = _weight_grad(h1, df, tw, "grad_fox_wf")
    g_fox_w_in = jnp.concatenate([g_q, g_k, g_v, g_g, g_f[:, :HEADS]], axis=1)
    g_fox_w_in = jnp.transpose(g_fox_w_in.reshape(D_MODEL, N_DEV, FOX_IN_SHARD), (1, 0, 2))
    g_fox_w_out = _weight_grad(y2, dx2b, tw, "grad_fox_w_out")
    fox_sums = _chip_sums(("fox_w_in", "fox_w_out"),
                          [g_fox_w_in, g_fox_w_out.reshape(N_DEV, -1, D_MODEL)])

    du, g_wa, g_wx, g_vec, (r_fox_in, r_fox_out) = _lru_core_bwd(
        dx1b, w_out_b, xb, gate1, hs, conv_w, conv_b, wa_b, ba, wx_b, bx, a_param,
        jnp.transpose(wa_b, (0, 2, 1)), jnp.transpose(wx_b, (0, 2, 1)), fox_sums, ts)
    g_lru_w_in = _weight_grad(h0, du, tw, "grad_lru_w_in", col_shards=N_DEV)
    g_lru_w_out = _weight_grad(y1, dx1b, tw, "grad_lru_w_out")
    conv_send = jnp.transpose(g_vec[0:CONV_WIDTH].reshape(CONV_WIDTH, N_DEV, -1), (1, 0, 2))
    lru_sums = _chip_sums(("lru_w_in", "lru_conv_w", "lru_w_out"),
                          [g_lru_w_in, conv_send, g_lru_w_out.reshape(N_DEV, -1, D_MODEL)])
    grad_x, g_norm0, (r_w_in, r_conv, r_w_out) = _lru_in_bwd(du, w_in8, x, dx1, g0, lru_sums, ts)

    small = dict(
        norm_g=jnp.concatenate([g_norm0, g_norm1], axis=0), final_g=g_final[0],
        lru_conv_b=g_vec[4:5], lru_wa=g_wa, lru_ba=g_vec[5:6], lru_wx=g_wx, lru_bx=g_vec[6:7],
        lru_a_param=g_vec[7:8], fox_b_f=g_bf[:, :HEADS])
    received = dict(lru_w_in=r_w_in, lru_conv_w=r_conv, lru_w_out=r_w_out, fox_w_in=r_fox_in,
                    fox_w_out=r_fox_out)
    return loss_acc[0, 0], grad_x, small, received


SMALL =("norm_g", "final_g", "lru_conv_b", "lru_wa", "lru_ba", "lru_wx", "lru_bx", "lru_a_param",
         "fox_b_f")
ALL_WEIGHTS = ("norm_g", "final_g", "lru_w_in", "lru_conv_w", "lru_conv_b", "lru_wa", "lru_ba",
               "lru_wx", "lru_bx", "lru_a_param", "lru_w_out", "fox_w_in", "fox_b_f", "fox_w_out")


def _pack_small(d):
    rows = []
    for n in SMALL:
        a = d[n].reshape(-1)
        if a.shape[0] % LANES:
            a = jnp.pad(a, (0, LANES - a.shape[0] % LANES))
        rows.append(a.reshape(-1, LANES))
    packed = jnp.concatenate(rows, axis=0)
    return jnp.pad(packed, ((0, N_DEV * SMALL_CHUNK_ROWS - packed.shape[0]), (0, 0)))


def _unpack_small(packed, like):
    out, off = {}, 0
    for n, nrows in zip(SMALL, SMALL_ROWS):
        size = like[n].size
        out[n] = packed[off:off + nrows].reshape(-1)[:size].reshape(like[n].shape)
        off += nrows
    return out


def kernel(x, norm_g, final_g, lru_w_in, lru_conv_w, lru_conv_b, lru_wa, lru_ba, lru_wx, lru_bx, lru_a_param, lru_w_out, fox_w_in, fox_b_f, fox_w_out, loss_target, m_norm_g, m_final_g, m_lru_w_in, m_lru_conv_w, m_lru_conv_b, m_lru_wa, m_lru_ba, m_lru_wx, m_lru_bx, m_lru_a_param, m_lru_w_out, m_fox_w_in, m_fox_b_f, m_fox_w_out, v_norm_g, v_final_g, v_lru_w_in, v_lru_conv_w, v_lru_conv_b, v_lru_wa, v_lru_ba, v_lru_wx, v_lru_bx, v_lru_a_param, v_lru_w_out, v_fox_w_in, v_fox_b_f, v_fox_w_out):
    w_loc = dict(norm_g=norm_g, final_g=final_g, lru_w_in=lru_w_in, lru_conv_w=lru_conv_w,
                 lru_conv_b=lru_conv_b, lru_wa=lru_wa, lru_ba=lru_ba, lru_wx=lru_wx, lru_bx=lru_bx,
                 lru_a_param=lru_a_param, lru_w_out=lru_w_out, fox_w_in=fox_w_in, fox_b_f=fox_b_f,
                 fox_w_out=fox_w_out)
    m_loc = dict(norm_g=m_norm_g, final_g=m_final_g, lru_w_in=m_lru_w_in, lru_conv_w=m_lru_conv_w,
                 lru_conv_b=m_lru_conv_b, lru_wa=m_lru_wa, lru_ba=m_lru_ba, lru_wx=m_lru_wx,
                 lru_bx=m_lru_bx, lru_a_param=m_lru_a_param, lru_w_out=m_lru_w_out,
                 fox_w_in=m_fox_w_in, fox_b_f=m_fox_b_f, fox_w_out=m_fox_w_out)
    v_loc = dict(norm_g=v_norm_g, final_g=v_final_g, lru_w_in=v_lru_w_in, lru_conv_w=v_lru_conv_w,
                 lru_conv_b=v_lru_conv_b, lru_wa=v_lru_wa, lru_ba=v_lru_ba, lru_wx=v_lru_wx,
                 lru_bx=v_lru_bx, lru_a_param=v_lru_a_param, lru_w_out=v_lru_w_out,
                 fox_w_in=v_fox_w_in, fox_b_f=v_fox_b_f, fox_w_out=v_fox_w_out)

    w_in8, conv8, w_out8 = _gather_two_level(
        [lru_w_in[0].astype(BF16), lru_conv_w[0], lru_w_out[0].astype(BF16)], "gather_weights")
    conv_full = jnp.transpose(conv8, (1, 0, 2)).reshape(CONV_WIDTH, LRU_WIDTH)

    loss, grad_x, small_grads, received = _local_step(
        x[0], loss_target[0], norm_g, final_g, w_in8, conv_full, lru_conv_b, lru_wa[0], lru_ba,
        lru_wx[0], lru_bx, lru_a_param, w_out8.reshape(LRU_WIDTH, D_MODEL),
        fox_w_in[0].astype(BF16), fox_b_f, fox_w_out[0].astype(BF16))

    out = {}
    for n, tr in (("lru_w_in", 256), ("lru_conv_w", CONV_WIDTH), ("lru_w_out", 96),
                  ("fox_w_in", 128), ("fox_w_out", 64)):
        res = _adamw(received[n], w_loc[n][0], m_loc[n][0], v_loc[n][0], tr, "adamw_" + n)
        out[n] = [a[None] for a in res]

    small_sums = _chip_sums(
        ("small",), [_pack_small(small_grads).reshape(N_DEV, SMALL_CHUNK_ROWS, LANES)])
    r_small, = _exchange_chips(small_sums, "scatter_small_grads")

    g_chunk = _reduce_parts(r_small, "reduce_small_grads")
    g_small, = _exchange([g_chunk], False, "gather_small_grads")
    g_small = g_small.reshape(1, N_DEV * SMALL_CHUNK_ROWS, LANES)
    res = _adamw(g_small, _pack_small(w_loc), _pack_small(m_loc), _pack_small(v_loc),
                 N_DEV * SMALL_CHUNK_ROWS, "adamw_replicated")
    small_out = [_unpack_small(a, w_loc) for a in res]
    for n in SMALL:
        out[n] = [d[n] for d in small_out]

    loss = lax.psum(loss, ("x", "y", "c"))
    return (loss, grad_x[None], *[out[n][0] for n in ALL_WEIGHTS], *[out[n][1] for n in ALL_WEIGHTS],
            *[out[n][2] for n in ALL_WEIGHTS], *[out[n][3] for n in ALL_WEIGHTS])
```

```python
import functools

import jax
import jax.numpy as jnp
from jax import lax
from jax.experimental import pallas as pl
from jax.experimental.pallas import tpu as pltpu

F32 = jnp.float32
BF16 = jnp.bfloat16

D_MODEL = 1024
LRU_WIDTH = 1536
LRU_BLOCKS = 12
LRU_BLOCK_W = 128
CONV_WIDTH = 4
LRU_C = 8.0
HEADS = 16
HEAD_DIM = 64
HEAD_PAD = 128
FOX_PAD = HEADS * HEAD_PAD
HEADS_PER_STEP = 2
QK_SCALE = 1.0 / HEAD_DIM ** 0.5
EPS = 1e-6
NEG_BIG = -1e30
N_DEV = 8

ADAM_LR = 0.001
ADAM_B1 = 0.9
ADAM_B2 = 0.999
ADAM_EPS = 1e-08
ADAM_WD = 0.01
ADAM_STEP = 10

LANE_RB = 64
LANE_CK = 67
LANE_LSE = 70
LANE_ONE_V = 64

VMEM_LIMIT_BYTES = 56 * 1024 * 1024
LANES = 128
SUBLANES = 8

LRU_IN_SHARD = 2 * LRU_WIDTH // N_DEV
FOX_IN_COLS = 4 * HEADS * HEAD_DIM + HEADS
FOX_IN_SHARD = FOX_IN_COLS // N_DEV

SMALL_ROWS = (16, 8, 12, 1536, 12, 1536, 12, 12, 1)
SMALL_CHUNK_ROWS = 400
assert sum(SMALL_ROWS) <= N_DEV * SMALL_CHUNK_ROWS


def _params(n_grid_axes=1):
    return pltpu.CompilerParams(
        dimension_semantics=("arbitrary",) * n_grid_axes,
        vmem_limit_bytes=VMEM_LIMIT_BYTES)


def _const_spec(shape):
    nd = len(shape)
    return pl.BlockSpec(shape, lambda *_: (0,) * nd, pipeline_mode=pl.Buffered(1))


def _shift_down(x, k, fill):
    rows = lax.broadcasted_iota(jnp.int32, x.shape, 0)
    return jnp.where(rows >= k, pltpu.roll(x, k, 0), fill)


def _shift_up(x, k, fill):
    n = x.shape[0]
    rows = lax.broadcasted_iota(jnp.int32, x.shape, 0)
    return jnp.where(rows < n - k, pltpu.roll(x, n - k, 0), fill)


def _scan_rows(a, b, reverse=False):
    n = a.shape[0]
    shift = _shift_up if reverse else _shift_down
    k = 1
    while k < n:
        b = a * shift(b, k, 0.0) + b
        a = a * shift(a, k, 1.0)
        k *= 2
    return a, b


def _cumsum_rows(x, reverse=False):
    n = x.shape[0]
    shift = _shift_up if reverse else _shift_down
    k = 1
    while k < n:
        x = x + shift(x, k, 0.0)
        k *= 2
    return x


def _rstd(x):
    return lax.rsqrt(jnp.mean(x * x, axis=-1, keepdims=True) + EPS)


def _norm_bwd(x, g, dh):
    rstd = _rstd(x)
    xhat = x * rstd
    dg = jnp.sum(dh * xhat, axis=0, keepdims=True)
    dxh = dh * g
    dx = rstd * (dxh - xhat * jnp.mean(dxh * xhat, axis=-1, keepdims=True))
    return dx, dg


def _split3(x):
    hi = x.astype(BF16)
    r1 = x - hi.astype(F32)
    mid = r1.astype(BF16)
    lo = (r1 - mid.astype(F32)).astype(BF16)
    return hi, mid, lo


def _sigmoid(x):
    return jax.nn.sigmoid(x)


def _dot(a, b):
    return jnp.dot(a, b, preferred_element_type=F32)


def _dot_nt(a, b):
    return lax.dot_general(a, b, (((1,), (1,)), ((), ())), preferred_element_type=F32)


def _dot_tn(a, b):
    return lax.dot_general(a, b, (((0,), (0,)), ((), ())), preferred_element_type=F32)


def _heads_to_padded(u):
    n = u.shape[0]
    low = lax.broadcasted_iota(jnp.int32, (n, LANES), 1) < HEAD_DIM
    zero = jnp.zeros((n, LANES), u.dtype)
    cols = []
    for p in range(HEADS // 2):
        pair = u[:, p * LANES:(p + 1) * LANES]
        cols.append(jnp.where(low, pair, zero))
        cols.append(jnp.where(low, pltpu.roll(pair, HEAD_DIM, 1), zero))
    return jnp.concatenate(cols, axis=1)


def _heads_from_padded(x):
    n = x.shape[0]
    low = lax.broadcasted_iota(jnp.int32, (n, LANES), 1) < HEAD_DIM
    cols = []
    for p in range(HEADS // 2):
        even = x[:, (2 * p) * HEAD_PAD:(2 * p + 1) * HEAD_PAD]
        odd = x[:, (2 * p + 1) * HEAD_PAD:(2 * p + 2) * HEAD_PAD]
        cols.append(jnp.where(low, even, pltpu.roll(odd, HEAD_DIM, 1)))
    return jnp.concatenate(cols, axis=1)


def _conv_taps(xb, prev8):
    rows8 = lax.broadcasted_iota(jnp.int32, prev8.shape, 0)
    taps = [xb]
    for j in range(1, CONV_WIDTH):
        r = pltpu.roll(xb, j, 0)
        p = pltpu.roll(prev8, j, 0)
        head = jnp.where(rows8 < j, p, r[0:SUBLANES])
        taps.append(jnp.concatenate([head, r[SUBLANES:]], axis=0))
    return taps


def _lru_pre(taps, cw, cb, wa_ref, ba, wx_ref, bx, a_param):
    xc = cb + cw[3:4] * taps[0] + cw[2:3] * taps[1] + cw[1:2] * taps[2] + cw[0:1] * taps[3]
    xcb = xc.astype(BF16)
    ra, ia = [], []
    for n in range(LRU_BLOCKS):
        blk = xcb[:, n * LRU_BLOCK_W:(n + 1) * LRU_BLOCK_W]
        ra.append(_dot(blk, wa_ref[n]))
        ia.append(_dot(blk, wx_ref[n]))
    r = _sigmoid(jnp.concatenate(ra, axis=1) + ba)
    i = _sigmoid(jnp.concatenate(ia, axis=1) + bx)
    z = -a_param
    sp = jnp.maximum(z, 0.0) + jnp.log1p(jnp.exp(-jnp.abs(z)))
    log_a = (-LRU_C) * r * sp
    a = jnp.exp(log_a)
    one_minus_a2 = -jnp.tanh(log_a) * (a * a + 1.0)
    mult = jnp.sqrt(one_minus_a2)
    return xc, xcb, r, i, sp, a, mult


def _lru_in_fwd(x, g0, w_in, later_shards, ts):
    s = x.shape[0]
    nt = s // ts
    n = len(later_shards)

    def body(*refs):
        x_ref, g_ref, w_ref = refs[:3]
        shard_refs = refs[3:3 + n]
        xb_ref, gate_ref, h_ref = refs[3 + n:6 + n]
        wfull_ref = refs[6 + 2 * n]
        start, forward, finish = _gather_phases(shard_refs, refs[6 + n:6 + 2 * n],
                                                *refs[7 + 2 * n:])
        step = pl.program_id(0)
        pl.when(step == 0)(start)

        @pl.when(step == 0)
        def _():
            for j in range(N_DEV):
                wfull_ref[:, j * LRU_IN_SHARD:(j + 1) * LRU_IN_SHARD] = w_ref[j]

        xv = x_ref[...]
        h = (xv * _rstd(xv) * g_ref[...]).astype(BF16)
        u = _dot(h, wfull_ref[...])
        xb_ref[...] = u[:, :LRU_WIDTH]
        gate_ref[...] = u[:, LRU_WIDTH:]
        h_ref[...] = h
        pl.when(step == (2 * nt) // 3)(forward)
        pl.when(step == nt - 1)(finish)

    hbm = pl.BlockSpec(memory_space=pl.ANY)
    res = pl.pallas_call(
        body, name="lru_in_fwd", grid=(nt,),
        in_specs=[pl.BlockSpec((ts, D_MODEL), lambda i: (i, 0)),
                  _const_spec((1, D_MODEL)),
                  _const_spec((N_DEV, D_MODEL, LRU_IN_SHARD))] + [hbm] * n,
        out_specs=[pl.BlockSpec((ts, LRU_WIDTH), lambda i: (i, 0)),
                   pl.BlockSpec((ts, LRU_WIDTH), lambda i: (i, 0)),
                   pl.BlockSpec((ts, D_MODEL), lambda i: (i, 0))] + [hbm] * n,
        out_shape=[jax.ShapeDtypeStruct((s, LRU_WIDTH), F32),
                   jax.ShapeDtypeStruct((s, LRU_WIDTH), F32),
                   jax.ShapeDtypeStruct((s, D_MODEL), BF16)]
        + [jax.ShapeDtypeStruct((N_DEV,) + a.shape, a.dtype) for a in later_shards],
        scratch_shapes=[pltpu.VMEM((D_MODEL, 2 * LRU_WIDTH), BF16)] + _gather_sems(n),
        compiler_params=_params(),
    )(x, g0, w_in, *later_shards)
    return res[0], res[1], res[2], res[3:]


def _lru_core_fwd(xb, gate, cw, cb, wa, ba, wx, bx, a_param, ts):
    s = xb.shape[0]

    def body(xb_ref, gate_ref, cw_ref, cb_ref, wa_ref, ba_ref, wx_ref, bx_ref, ap_ref,
             y_ref, hs_ref, prev_ref, hcar_ref):
        @pl.when(pl.program_id(0) == 0)
        def _():
            prev_ref[...] = jnp.zeros_like(prev_ref)
            hcar_ref[...] = jnp.zeros_like(hcar_ref)

        xbv = xb_ref[...]
        taps = _conv_taps(xbv, prev_ref[...])
        xc, _, _, i, _, a, mult = _lru_pre(taps, cw_ref[...], cb_ref[...], wa_ref, ba_ref[...],
                                           wx_ref, bx_ref[...], ap_ref[...])
        bterm = mult * (i * xc)
        cum_a, hloc = _scan_rows(a, bterm)
        hs = cum_a * hcar_ref[SUBLANES - 1:SUBLANES, :] + hloc
        gv = gate_ref[...]
        y_ref[...] = (hs * (gv * _sigmoid(gv))).astype(BF16)
        hs_ref[...] = hs
        prev_ref[...] = xbv[ts - SUBLANES:, :]
        hcar_ref[...] = hs[ts - SUBLANES:, :]

    vec = _const_spec((1, LRU_WIDTH))
    blk = _const_spec((LRU_BLOCKS, LRU_BLOCK_W, LRU_BLOCK_W))
    tile = pl.BlockSpec((ts, LRU_WIDTH), lambda i: (i, 0))
    return pl.pallas_call(
        body, name="lru_core_fwd", grid=(s // ts,),
        in_specs=[tile, tile, _const_spec((CONV_WIDTH, LRU_WIDTH)), vec, blk, vec, blk, vec, vec],
        out_specs=[tile, tile],
        out_shape=[jax.ShapeDtypeStruct((s, LRU_WIDTH), BF16),
                   jax.ShapeDtypeStruct((s, LRU_WIDTH), F32)],
        scratch_shapes=[pltpu.VMEM((SUBLANES, LRU_WIDTH), F32),
                        pltpu.VMEM((SUBLANES, LRU_WIDTH), F32)],
        compiler_params=_params(),
    )(xb, gate, cw, cb, wa, ba, wx, bx, a_param)


def _fox_pre_fwd(x, y, w_out, g1, wf, bf, ts):
    s = x.shape[0]

    def body(x_ref, y_ref, w_ref, g_ref, wf_ref, bf_ref, x1_ref, h1_ref, f_ref, cp_ref, ccar_ref):
        @pl.when(pl.program_id(0) == 0)
        def _():
            ccar_ref[...] = jnp.zeros_like(ccar_ref)

        x1 = x_ref[...] + _dot(y_ref[...], w_ref[...])
        h1 = (x1 * _rstd(x1) * g_ref[...]).astype(BF16)
        f = _dot(h1, wf_ref[...]) + bf_ref[...]
        logsig = jnp.minimum(f, 0.0) - jnp.log1p(jnp.exp(-jnp.abs(f)))
        cum = _cumsum_rows(logsig) + ccar_ref[SUBLANES - 1:SUBLANES, :]
        hi, mid, lo = _split3(cum)
        lane = lax.broadcasted_iota(jnp.int32, cum.shape, 1)
        packed = jnp.where(lane < HEADS, hi.astype(F32), jnp.where(
            lane < 2 * HEADS, pltpu.roll(mid.astype(F32), HEADS, 1), jnp.where(
                lane < 3 * HEADS, pltpu.roll(lo.astype(F32), 2 * HEADS, 1), 0.0)))
        x1_ref[...] = x1
        h1_ref[...] = h1
        f_ref[...] = f
        cp_ref[...] = packed.astype(BF16)
        ccar_ref[...] = cum[ts - SUBLANES:, :]

    return pl.pallas_call(
        body, name="fox_pre_fwd", grid=(s // ts,),
        in_specs=[pl.BlockSpec((ts, D_MODEL), lambda i: (i, 0)),
                  pl.BlockSpec((ts, LRU_WIDTH), lambda i: (i, 0)),
                  _const_spec((LRU_WIDTH, D_MODEL)),
                  _const_spec((1, D_MODEL)),
                  _const_spec((D_MODEL, LANES)),
                  _const_spec((1, LANES))],
        out_specs=[pl.BlockSpec((ts, D_MODEL), lambda i: (i, 0)),
                   pl.BlockSpec((ts, D_MODEL), lambda i: (i, 0)),
                   pl.BlockSpec((ts, LANES), lambda i: (i, 0)),
                   pl.BlockSpec((ts, LANES), lambda i: (i, 0))],
        out_shape=[jax.ShapeDtypeStruct((s, D_MODEL), F32),
                   jax.ShapeDtypeStruct((s, D_MODEL), BF16),
                   jax.ShapeDtypeStruct((s, LANES), F32),
                   jax.ShapeDtypeStruct((s, LANES), BF16)],
        scratch_shapes=[pltpu.VMEM((SUBLANES, LANES), F32)],
        compiler_params=_params(),
    )(x, y, w_out, g1, wf, bf)


def _fox_proj_fwd(h1, cparts, w, first, ng, sel, bias, out_dtype, ts, name):
    s = h1.shape[0]
    width = HEADS * HEAD_DIM
    use_sel = sel is not None

    def body(*refs):
        if use_sel:
            h_ref, cp_ref, w_ref, sel_ref, b_ref, o_ref = refs
            proj = _dot(h_ref[...], w_ref[...])
            if first == 0:
                proj = proj * jnp.where(pl.program_id(0) == 0, QK_SCALE, 1.0)
            acc = _heads_to_padded(proj) + _dot(cp_ref[...], sel_ref[...]) + b_ref[...]
        else:
            h_ref, w_ref, o_ref = refs
            acc = _heads_to_padded(_dot(h_ref[...], w_ref[...]))
        o_ref[...] = acc.astype(out_dtype)

    in_specs = [pl.BlockSpec((ts, D_MODEL), lambda j, i: (i, 0))]
    args = [h1]
    if use_sel:
        in_specs.append(pl.BlockSpec((ts, LANES), lambda j, i: (i, 0)))
        args.append(cparts)
    in_specs.append(pl.BlockSpec((D_MODEL, width), lambda j, i: (0, first + j)))
    args.append(w)
    if use_sel:
        in_specs.append(pl.BlockSpec((None, LANES, FOX_PAD), lambda j, i: (j, 0, 0)))
        in_specs.append(pl.BlockSpec((None, 1, FOX_PAD), lambda j, i: (j, 0, 0)))
        args += [sel, bias]
    return pl.pallas_call(
        body, name=name, grid=(ng, s // ts),
        in_specs=in_specs,
        out_specs=pl.BlockSpec((None, ts, FOX_PAD), lambda j, i: (j, i, 0)),
        out_shape=jax.ShapeDtypeStruct((ng, s, FOX_PAD), out_dtype),
        compiler_params=_params(2),
    )(*args)


def _attn_fwd(qkv, blk, hps=HEADS_PER_STEP):
    s = qkv.shape[1]
    nblk = s // blk
    wide = 2 * blk
    heads = [slice(i * HEAD_PAD, (i + 1) * HEAD_PAD) for i in range(hps)]

    def body(q_ref, k_ref, v_ref, o_ref, qb_ref, acc_ref, m_ref):
        qi = pl.program_id(1)
        row = lax.broadcasted_iota(jnp.int32, (blk, blk), 0)
        col = lax.broadcasted_iota(jnp.int32, (blk, blk), 1)
        lane = lax.broadcasted_iota(jnp.int32, (blk, HEAD_PAD), 1)
        qs = [q_ref[:, hd] for hd in heads]
        for i in range(hps):
            acc_ref[i] = jnp.zeros((blk, HEAD_PAD), F32)
            m_ref[i] = jnp.full((blk, HEAD_PAD), NEG_BIG, F32)

        def step(k0, size, masked):
            scores = [_dot_nt(q, k_ref[pl.ds(k0, size), hd]) for q, hd in zip(qs, heads)]
            for i, (sc, hd) in enumerate(zip(scores, heads)):
                v = v_ref[pl.ds(k0, size), hd]
                if masked:
                    sc = jnp.where(col <= row, sc, NEG_BIG)
                m = m_ref[i]
                m_new = jnp.maximum(m, jnp.max(sc, axis=-1, keepdims=True))
                p = jnp.exp((sc - jnp.tile(m_new, (1, size // HEAD_PAD))).astype(BF16))
                acc_ref[i] = jnp.exp(m - m_new) * acc_ref[i] + _dot(p, v)
                m_ref[i] = m_new

        def wide_step(kk, _):
            step(pl.multiple_of(kk * wide, wide), wide, False)
            return 0

        lax.fori_loop(0, qi // 2, wide_step, 0)

        @pl.when(qi % 2 == 1)
        def _():
            step(pl.multiple_of((qi - 1) * blk, blk), blk, False)

        step(pl.multiple_of(qi * blk, blk), blk, True)
        for i, (q, hd) in enumerate(zip(qs, heads)):
            acc = acc_ref[i]
            l = jnp.broadcast_to(acc[:, LANE_ONE_V:LANE_ONE_V + 1], (blk, HEAD_PAD))
            o_ref[:, hd] = (acc / l).astype(BF16)
            hi, mid, lo = _split3(-(m_ref[i] + jnp.log(l)))
            qb_ref[:, hd] = jnp.where(lane == LANE_LSE, hi, jnp.where(
                lane == LANE_LSE + 1, mid, jnp.where(lane == LANE_LSE + 2, lo, q)))

    width = hps * HEAD_PAD

    def whole(j):
        return pl.BlockSpec((None, s, width), lambda h, i: (j, 0, h))

    out_spec = pl.BlockSpec((blk, width), lambda h, i: (i, h))
    return pl.pallas_call(
        body, name="attn_fwd", grid=(HEADS // hps, nblk),
        in_specs=[pl.BlockSpec((None, blk, width), lambda h, i: (0, i, h)), whole(1), whole(2)],
        out_specs=[out_spec, out_spec],
        out_shape=[jax.ShapeDtypeStruct((s, FOX_PAD), BF16),
                   jax.ShapeDtypeStruct((s, FOX_PAD), BF16)],
        scratch_shapes=[pltpu.VMEM((hps, blk, HEAD_PAD), F32),
                        pltpu.VMEM((hps, blk, HEAD_PAD), F32)],
        compiler_params=_params(2),
    )(qkv, qkv, qkv)


def _fox_out_loss(o, gate, w_out, x1, target, gf, ts):
    s = x1.shape[0]

    def body(o_ref, gt_ref, w_ref, x1_ref, t_ref, g_ref, dx2_ref, dx2b_ref, y2_ref, loss_ref,
             gfin_ref):
        @pl.when(pl.program_id(0) == 0)
        def _():
            loss_ref[...] = jnp.zeros_like(loss_ref)
            gfin_ref[...] = jnp.zeros_like(gfin_ref)

        gv = gt_ref[...]
        y2 = _heads_from_padded(o_ref[...] * (gv * _sigmoid(gv))).astype(BF16)
        x2 = x1_ref[...] + _dot(y2, w_ref[...])
        rstd = _rstd(x2)
        xhat = x2 * rstd
        g = g_ref[...]
        diff = xhat * g - t_ref[...]
        loss_ref[...] += 0.5 * jnp.sum(jnp.mean(diff * diff, axis=-1, keepdims=True))
        dy = diff * (1.0 / D_MODEL)
        gfin_ref[...] += jnp.sum(dy * xhat, axis=0, keepdims=True)
        dxh = dy * g
        dx2 = rstd * (dxh - xhat * jnp.mean(dxh * xhat, axis=-1, keepdims=True))
        dx2_ref[...] = dx2
        dx2b_ref[...] = dx2.astype(BF16)
        y2_ref[...] = y2

    return pl.pallas_call(
        body, name="fox_out_loss", grid=(s // ts,),
        in_specs=[pl.BlockSpec((ts, FOX_PAD), lambda i: (i, 0)),
                  pl.BlockSpec((ts, FOX_PAD), lambda i: (i, 0)),
                  _const_spec((HEADS * HEAD_DIM, D_MODEL)),
                  pl.BlockSpec((ts, D_MODEL), lambda i: (i, 0)),
                  pl.BlockSpec((ts, D_MODEL), lambda i: (i, 0)),
                  _const_spec((1, D_MODEL))],
        out_specs=[pl.BlockSpec((ts, D_MODEL), lambda i: (i, 0)),
                   pl.BlockSpec((ts, D_MODEL), lambda i: (i, 0)),
                   pl.BlockSpec((ts, HEADS * HEAD_DIM), lambda i: (i, 0)),
                   pl.BlockSpec((SUBLANES, LANES), lambda i: (0, 0)),
                   pl.BlockSpec((1, D_MODEL), lambda i: (0, 0))],
        out_shape=[jax.ShapeDtypeStruct((s, D_MODEL), F32),
                   jax.ShapeDtypeStruct((s, D_MODEL), BF16),
                   jax.ShapeDtypeStruct((s, HEADS * HEAD_DIM), BF16),
                   jax.ShapeDtypeStruct((SUBLANES, LANES), F32),
                   jax.ShapeDtypeStruct((1, D_MODEL), F32)],
        compiler_params=_params(),
    )(o, gate, w_out, x1, target, gf)


def _fox_out_bwd(dx2, w_out, o, gate, ts):
    s = dx2.shape[0]

    def body(dx_ref, w_ref, o_ref, gt_ref, do_ref, dg_ref):
        lane = lax.broadcasted_iota(jnp.int32, (ts, HEAD_PAD), 1)
        dy2 = _heads_to_padded(_dot_nt(dx_ref[...], w_ref[...]))
        gv = gt_ref[...]
        sg = _sigmoid(gv)
        ov = o_ref[...]
        dov = dy2 * (gv * sg)
        dg_ref[...] = (dy2 * ov * (sg * (1.0 + gv * (1.0 - sg)))).astype(BF16)
        prod = dov * ov
        for h in range(HEADS):
            sl = slice(h * HEAD_PAD, (h + 1) * HEAD_PAD)
            delta = jnp.sum(prod[:, sl], axis=-1, keepdims=True)
            hi = delta.astype(BF16)
            lo = (delta - hi.astype(F32)).astype(BF16)
            do_h = dov[:, sl].astype(BF16)
            do_ref[:, sl] = jnp.where(lane == LANE_ONE_V, -hi,
                                      jnp.where(lane == LANE_ONE_V + 1, -lo, do_h))

    tile = pl.BlockSpec((ts, FOX_PAD), lambda i: (i, 0))
    return pl.pallas_call(
        body, name="fox_out_bwd", grid=(s // ts,),
        in_specs=[pl.BlockSpec((ts, D_MODEL), lambda i: (i, 0)),
                  _const_spec((HEADS * HEAD_DIM, D_MODEL)), tile, tile],
        out_specs=[tile, tile],
        out_shape=[jax.ShapeDtypeStruct((s, FOX_PAD), BF16),
                   jax.ShapeDtypeStruct((s, FOX_PAD), BF16)],
        compiler_params=_params(),
    )(dx2, w_out, o, gate)


def _attn_bwd(qb, qkv, do, blk):
    s = qb.shape[0]
    nblk = s // blk
    half = blk // 2
    heads = [slice(i * HEAD_PAD, (i + 1) * HEAD_PAD) for i in range(HEADS_PER_STEP)]

    def body(q_ref, k_ref, v_ref, do_ref, dq_ref, dk_ref, dv_ref, dcum_ref, dq_acc, dkt_acc,
             dvt_acc, qt_ref, dot_ref):
        group = pl.program_id(0)
        kj = pl.program_id(1)
        row = lax.broadcasted_iota(jnp.int32, (blk, blk), 0)
        col = lax.broadcasted_iota(jnp.int32, (blk, blk), 1)
        lane = lax.broadcasted_iota(jnp.int32, (blk, LANES), 1)
        mine = [lane == group * HEADS_PER_STEP + i for i in range(HEADS_PER_STEP)]

        @pl.when(kj == 0)
        def _():
            dq_acc[...] = jnp.zeros_like(dq_acc)

            def transpose_block(bi, _):
                r0 = pl.multiple_of(bi * blk, blk)
                for i, hd in enumerate(heads):
                    qt_ref[i, bi] = q_ref[pl.ds(r0, blk), hd].T
                    dot_ref[i, bi] = do_ref[pl.ds(r0, blk), hd].T
                return 0

            lax.fori_loop(0, nblk, transpose_block, 0)

        @pl.when((group == 0) & (kj == 0))
        def _():
            dcum_ref[...] = jnp.zeros_like(dcum_ref)

        k0 = pl.multiple_of(kj * blk, blk)
        ks = [k_ref[:, hd] for hd in heads]
        vs = [v_ref[:, hd] for hd in heads]

        def step(qi, q_lo, nq, k_lo, nk, masked):
            parts = ([(0, q_lo, 0, nq)] if nq <= blk
                     else [(b, 0, b * blk, blk) for b in range(nq // blk)])
            q0 = pl.multiple_of(qi * blk + q_lo, half)
            qs = [q_ref[pl.ds(q0, nq), hd] for hd in heads]
            dos = [do_ref[pl.ds(q0, nq), hd] for hd in heads]
            kk = [k[k_lo:k_lo + nk] for k in ks]
            vv = [v[k_lo:k_lo + nk] for v in vs]
            scores = [_dot_nt(q, k) for q, k in zip(qs, kk)]
            dps = [_dot_nt(dov, v) for dov, v in zip(dos, vv)]
            for i, (hd, k, sc, dp) in enumerate(zip(heads, kk, scores, dps)):
                p = jnp.exp(sc.astype(BF16))
                if masked:
                    p = jnp.where(col[:nq, :nk] + k_lo <= row[:nq, :nk] + q_lo, p,
                                  jnp.zeros_like(p))
                ds = (p.astype(F32) * dp).astype(BF16)
                dvt = sum(_dot(dot_ref[i, qi + b, :, c:c + n], p[r:r + n]) for b, c, r, n in parts)
                dkt = sum(_dot(qt_ref[i, qi + b, :, c:c + n], ds[r:r + n]) for b, c, r, n in parts)
                if masked:
                    dvt_acc[i, :, k_lo:k_lo + nk] = dvt
                    dkt_acc[i, :, k_lo:k_lo + nk] = dkt
                else:
                    dvt_acc[i, :, k_lo:k_lo + nk] += dvt
                    dkt_acc[i, :, k_lo:k_lo + nk] += dkt
                dq_acc[pl.ds(q0, nq), hd] += _dot(ds, k)

        step(kj, 0, blk, 0, half, True)
        step(kj, half, half, half, half, True)

        n_after = nblk - 1 - kj

        def q_step(t, _):
            step(kj + 1 + 2 * t, 0, 2 * blk, 0, blk, False)
            return 0

        lax.fori_loop(0, n_after // 2, q_step, 0)

        @pl.when(n_after % 2 == 1)
        def _():
            step(nblk - 1, 0, blk, 0, blk, False)
        dcum = dcum_ref[pl.ds(k0, blk), :]
        for i, (hd, mask) in enumerate(zip(heads, mine)):
            dk = dkt_acc[i].T
            dk_ref[:, hd] = dk.astype(BF16)
            dv_ref[:, hd] = dvt_acc[i].astype(BF16).T
            dcum = jnp.where(mask, -dk[:, LANE_CK:LANE_CK + 1], dcum)
        dcum_ref[pl.ds(k0, blk), :] = dcum

        @pl.when(kj == nblk - 1)
        def _():
            def finish(bi, _):
                r0 = pl.multiple_of(bi * blk, blk)
                dcum = dcum_ref[pl.ds(r0, blk), :]
                for hd, mask in zip(heads, mine):
                    dq = dq_acc[pl.ds(r0, blk), hd]
                    dq_ref[pl.ds(r0, blk), hd] = dq.astype(BF16)
                    dcum = dcum + jnp.where(mask, dq[:, LANE_RB:LANE_RB + 1], 0.0)
                dcum_ref[pl.ds(r0, blk), :] = dcum
                return 0

            lax.fori_loop(0, nblk, finish, 0)

    width = HEADS_PER_STEP * HEAD_PAD
    whole = pl.BlockSpec((s, width), lambda h, j: (0, h))
    part = pl.BlockSpec((blk, width), lambda h, j: (j, h))
    out = jax.ShapeDtypeStruct((s, FOX_PAD), BF16)
    return pl.pallas_call(
        body, name="attn_bwd", grid=(HEADS // HEADS_PER_STEP, nblk),
        in_specs=[whole,
                  pl.BlockSpec((None, blk, width), lambda h, j: (1, j, h)),
                  pl.BlockSpec((None, blk, width), lambda h, j: (2, j, h)),
                  whole],
        out_specs=[whole, part, part, pl.BlockSpec((s, LANES), lambda h, j: (0, 0))],
        out_shape=[out, out, out, jax.ShapeDtypeStruct((s, LANES), F32)],
        scratch_shapes=[pltpu.VMEM((s, width), F32),
                        pltpu.VMEM((HEADS_PER_STEP, HEAD_PAD, blk), F32),
                        pltpu.VMEM((HEADS_PER_STEP, HEAD_PAD, blk), F32),
                        pltpu.VMEM((HEADS_PER_STEP, nblk, HEAD_PAD, blk), BF16),
                        pltpu.VMEM((HEADS_PER_STEP, nblk, HEAD_PAD, blk), BF16)],
        compiler_params=_params(2),
    )(qb, qkv, qkv, do)


def _fox_in_bwd(dq, dk, dv, dg, wt, wft, dcum, f, x1, dx2, g1, ts):
    s = x1.shape[0]
    nt = s // ts
    width = HEADS * HEAD_DIM

    def body(dq_ref, dk_ref, dv_ref, dg_ref, wt_ref, wft_ref, dcum_ref, f_ref, x1_ref, dx2_ref,
             g_ref, dx1_ref, dx1b_ref, df_ref, duq_ref, duk_ref, duv_ref, dug_ref, gn_ref, gbf_ref,
             rcar_ref):
        du_refs = (duq_ref, duk_ref, duv_ref, dug_ref)


        @pl.when(pl.program_id(0) == 0)
        def _():
            rcar_ref[...] = jnp.zeros_like(rcar_ref)
            gn_ref[...] = jnp.zeros_like(gn_ref)
            gbf_ref[...] = jnp.zeros_like(gbf_ref)

        rsum = _cumsum_rows(dcum_ref[...], reverse=True) + rcar_ref[0:1, :]
        df = rsum * _sigmoid(-f_ref[...])
        dfb = df.astype(BF16)
        dh = _dot_nt(dfb, wft_ref[...])
        for j, ref in enumerate((dq_ref, dk_ref, dv_ref, dg_ref)):
            du = _heads_from_padded(ref[...])
            du_refs[j][...] = du
            if j == 0:
                du = du * QK_SCALE
            dh = dh + _dot_nt(du, wt_ref[:, j * width:(j + 1) * width])
        dxn, dgn = _norm_bwd(x1_ref[...], g_ref[...], dh)
        dx1 = dx2_ref[...] + dxn
        dx1_ref[...] = dx1
        dx1b_ref[...] = dx1.astype(BF16)
        df_ref[...] = dfb
        gn_ref[...] += dgn
        gbf_ref[...] += jnp.sum(df, axis=0, keepdims=True)
        rcar_ref[...] = rsum[0:SUBLANES, :]

    rev = lambda i: (nt - 1 - i, 0)
    wide = pl.BlockSpec((ts, FOX_PAD), rev)
    return pl.pallas_call(
        body, name="fox_in_bwd", grid=(nt,),
        in_specs=[wide, wide, wide, wide,
                  _const_spec((D_MODEL, FOX_IN_COLS)),
                  _const_spec((D_MODEL, LANES)),
                  pl.BlockSpec((ts, LANES), rev),
                  pl.BlockSpec((ts, LANES), rev),
                  pl.BlockSpec((ts, D_MODEL), rev),
                  pl.BlockSpec((ts, D_MODEL), rev),
                  _const_spec((1, D_MODEL))],
        out_specs=[pl.BlockSpec((ts, D_MODEL), rev),
                   pl.BlockSpec((ts, D_MODEL), rev),
                   pl.BlockSpec((ts, LANES), rev)]
        + [pl.BlockSpec((ts, width), rev)] * 4
        + [pl.BlockSpec((1, D_MODEL), lambda i: (0, 0)),
           pl.BlockSpec((1, LANES), lambda i: (0, 0))],
        out_shape=[jax.ShapeDtypeStruct((s, D_MODEL), F32),
                   jax.ShapeDtypeStruct((s, D_MODEL), BF16),
                   jax.ShapeDtypeStruct((s, LANES), BF16)]
        + [jax.ShapeDtypeStruct((s, width), BF16)] * 4
        + [jax.ShapeDtypeStruct((1, D_MODEL), F32),
                   jax.ShapeDtypeStruct((1, LANES), F32)],
        scratch_shapes=[pltpu.VMEM((SUBLANES, LANES), F32)],
        compiler_params=_params(),
    )(dq, dk, dv, dg, wt, wft, dcum, f, x1, dx2, g1)


def _lru_core_bwd(dx1b, w_out, xb, gate, hs, cw, cb, wa, ba, wx, bx, a_param, wa_t, wx_t,
                  chip_sums, ts):
    s = xb.shape[0]
    nt = s // ts
    tpb = ts // SUBLANES
    n_ex = len(chip_sums)

    def body(*refs):
        (dx_ref, wo_ref, xb_ref, xbh_ref, gate_ref, hs_ref, hsh_ref, cw_ref, cb_ref, wa_ref,
         ba_ref, wx_ref, bx_ref, ap_ref, wat_ref, wxt_ref) = refs[:16]
        sum_refs = refs[16:16 + n_ex]
        du_ref, gwa_ref, gwx_ref, gvec_ref = refs[16 + n_ex:20 + n_ex]
        got_refs = refs[20 + n_ex:20 + 2 * n_ex]
        acar_ref, dhcar_ref, dxccar_ref = refs[20 + 2 * n_ex:23 + 2 * n_ex]
        start, finish = _chip_exchange_phases(sum_refs, got_refs, *refs[23 + 2 * n_ex:])
        step = pl.program_id(0)
        pl.when(step == 0)(start)

        @pl.when(step == 0)
        def _():
            acar_ref[...] = jnp.zeros_like(acar_ref)
            dhcar_ref[...] = jnp.zeros_like(dhcar_ref)
            dxccar_ref[...] = jnp.zeros_like(dxccar_ref)
            gwa_ref[...] = jnp.zeros_like(gwa_ref)
            gwx_ref[...] = jnp.zeros_like(gwx_ref)
            gvec_ref[...] = jnp.zeros_like(gvec_ref)

        first_tile = step == nt - 1
        halo_on = jnp.where(first_tile, 0.0, 1.0)
        prev8 = xbh_ref[...] * halo_on
        hprev_row = hsh_ref[SUBLANES - 1:SUBLANES, :] * halo_on

        xbv = xb_ref[...]
        taps = _conv_taps(xbv, prev8)
        cw_v = cw_ref[...]
        xc, xcb, r, i, sp, a, mult = _lru_pre(taps, cw_v, cb_ref[...], wa_ref, ba_ref[...],
                                              wx_ref, bx_ref[...], ap_ref[...])
        hs = hs_ref[...]
        gv = gate_ref[...]
        sg = _sigmoid(gv)
        dy = _dot_nt(dx_ref[...], wo_ref[...])
        dhs = dy * (gv * sg)
        dgate = dy * hs * (sg * (1.0 + gv * (1.0 - sg)))

        rows = lax.broadcasted_iota(jnp.int32, a.shape, 0)
        a_next = jnp.where(rows < ts - 1, pltpu.roll(a, ts - 1, 0), acar_ref[0:1, :])
        cum_a, dh_loc = _scan_rows(a_next, dhs, reverse=True)
        dh = cum_a * dhcar_ref[0:1, :] + dh_loc
        h_prev = jnp.where(rows >= 1, pltpu.roll(hs, 1, 0), hprev_row)

        da = dh * h_prev
        ixc = i * xc
        dmult = dh * ixc
        di = dh * mult * xc
        dxc = dh * mult * i
        dlog_a = da * a - dmult * (a * a) / mult
        dr = dlog_a * ((-LRU_C) * sp)
        dsp = jnp.sum(dlog_a * ((-LRU_C) * r), axis=0, keepdims=True)
        dra = dr * r * (1.0 - r)
        dia = di * i * (1.0 - i)
        drab = dra.astype(BF16)
        diab = dia.astype(BF16)
        back = []
        for n in range(LRU_BLOCKS):
            sl = slice(n * LRU_BLOCK_W, (n + 1) * LRU_BLOCK_W)
            gwa_ref[n] += _dot_tn(xcb[:, sl], drab[:, sl])
            gwx_ref[n] += _dot_tn(xcb[:, sl], diab[:, sl])
            back.append(_dot(drab[:, sl], wat_ref[n]) + _dot(diab[:, sl], wxt_ref[n]))
        dxc = dxc + jnp.concatenate(back, axis=1)

        nxt8 = dxccar_ref[...]
        rows8 = lax.broadcasted_iota(jnp.int32, nxt8.shape, 0)
        dxb = cw_v[3:4] * dxc
        for j in range(1, CONV_WIDTH):
            rj = pltpu.roll(dxc, ts - j, 0)
            pj = pltpu.roll(nxt8, SUBLANES - j, 0)
            tail = jnp.where(rows8 >= SUBLANES - j, pj, rj[ts - SUBLANES:])
            dxb = dxb + cw_v[3 - j:4 - j] * jnp.concatenate([rj[:ts - SUBLANES], tail], axis=0)

        du_ref[:, :LRU_WIDTH] = dxb.astype(BF16)
        du_ref[:, LRU_WIDTH:] = dgate.astype(BF16)

        z = -ap_ref[...]
        gvec = [jnp.sum(dxc * taps[3 - k], axis=0, keepdims=True) for k in range(CONV_WIDTH)]
        gvec.append(jnp.sum(dxc, axis=0, keepdims=True))
        gvec.append(jnp.sum(dra, axis=0, keepdims=True))
        gvec.append(jnp.sum(dia, axis=0, keepdims=True))
        gvec.append(-dsp * _sigmoid(z))
        gvec_ref[...] += jnp.concatenate(gvec, axis=0)

        acar_ref[...] = a[0:SUBLANES, :]
        dhcar_ref[...] = dh[0:SUBLANES, :]
        dxccar_ref[...] = dxc[0:SUBLANES, :]
        pl.when(step == nt - 1)(finish)

    rev = lambda i: (nt - 1 - i, 0)
    halo = lambda i: (jnp.maximum((nt - 1 - i) * tpb - 1, 0), 0)
    tile = pl.BlockSpec((ts, LRU_WIDTH), rev)
    halo_spec = pl.BlockSpec((SUBLANES, LRU_WIDTH), halo)
    vec = _const_spec((1, LRU_WIDTH))
    blk = _const_spec((LRU_BLOCKS, LRU_BLOCK_W, LRU_BLOCK_W))
    acc_blk = pl.BlockSpec((LRU_BLOCKS, LRU_BLOCK_W, LRU_BLOCK_W), lambda i: (0, 0, 0))
    hbm = pl.BlockSpec(memory_space=pl.ANY)
    res = pl.pallas_call(
        body, name="lru_core_bwd", grid=(nt,),
        in_specs=[pl.BlockSpec((ts, D_MODEL), rev),
                  _const_spec((LRU_WIDTH, D_MODEL)),
                  tile, halo_spec, tile, tile, halo_spec,
                  _const_spec((CONV_WIDTH, LRU_WIDTH)), vec, blk, vec, blk, vec, vec, blk, blk]
        + [hbm] * n_ex,
        out_specs=[pl.BlockSpec((ts, 2 * LRU_WIDTH), rev), acc_blk, acc_blk,
                   pl.BlockSpec((SUBLANES, LRU_WIDTH), lambda i: (0, 0))] + [hbm] * n_ex,
        out_shape=[jax.ShapeDtypeStruct((s, 2 * LRU_WIDTH), BF16),
                   jax.ShapeDtypeStruct((LRU_BLOCKS, LRU_BLOCK_W, LRU_BLOCK_W), F32),
                   jax.ShapeDtypeStruct((LRU_BLOCKS, LRU_BLOCK_W, LRU_BLOCK_W), F32),
                   jax.ShapeDtypeStruct((SUBLANES, LRU_WIDTH), F32)]
        + [jax.ShapeDtypeStruct(a.shape, a.dtype) for a in chip_sums],
        scratch_shapes=[pltpu.VMEM((SUBLANES, LRU_WIDTH), F32),
                        pltpu.VMEM((SUBLANES, LRU_WIDTH), F32),
                        pltpu.VMEM((SUBLANES, LRU_WIDTH), F32)] + _chip_exchange_sems(n_ex),
        compiler_params=_params(),
    )(dx1b, w_out, xb, xb, gate, hs, hs, cw, cb, wa, ba, wx, bx, a_param, wa_t, wx_t, *chip_sums)
    return res[0], res[1], res[2], res[3], res[4:]


def _lru_in_bwd(du, w_in, x, dx1, g0, chip_sums, ts):
    s = x.shape[0]
    nt = s // ts
    n = len(chip_sums)

    def body(*refs):
        du_ref, w_ref, x_ref, dx1_ref, g_ref = refs[:5]
        sum_refs = refs[5:5 + n]
        gx_ref, gn_ref = refs[5 + n:7 + n]
        got_refs = refs[7 + n:7 + 2 * n]
        wfull_ref = refs[7 + 2 * n]
        start, finish = _chip_exchange_phases(sum_refs, got_refs, *refs[8 + 2 * n:])
        step = pl.program_id(0)
        pl.when(step == 0)(start)

        @pl.when(step == 0)
        def _():
            gn_ref[...] = jnp.zeros_like(gn_ref)
            for j in range(N_DEV):
                wfull_ref[:, j * LRU_IN_SHARD:(j + 1) * LRU_IN_SHARD] = w_ref[j]

        dh = _dot_nt(du_ref[...], wfull_ref[...])
        dxn, dgn = _norm_bwd(x_ref[...], g_ref[...], dh)
        gx_ref[...] = dx1_ref[...] + dxn
        gn_ref[...] += dgn
        pl.when(step == nt - 1)(finish)

    tile = pl.BlockSpec((ts, D_MODEL), lambda i: (i, 0))
    hbm = pl.BlockSpec(memory_space=pl.ANY)
    res = pl.pallas_call(
        body, name="lru_in_bwd", grid=(nt,),
        in_specs=[pl.BlockSpec((ts, 2 * LRU_WIDTH), lambda i: (i, 0)),
                  _const_spec((N_DEV, D_MODEL, LRU_IN_SHARD)), tile, tile,
                  _const_spec((1, D_MODEL))] + [hbm] * n,
        out_specs=[tile, pl.BlockSpec((1, D_MODEL), lambda i: (0, 0))] + [hbm] * n,
        out_shape=[jax.ShapeDtypeStruct((s, D_MODEL), F32),
                   jax.ShapeDtypeStruct((1, D_MODEL), F32)]
        + [jax.ShapeDtypeStruct(a.shape, a.dtype) for a in chip_sums],
        scratch_shapes=[pltpu.VMEM((D_MODEL, 2 * LRU_WIDTH), BF16)] + _chip_exchange_sems(n),
        compiler_params=_params(),
    )(du, w_in, x, dx1, g0, *chip_sums)
    return res[0], res[1], res[2:]


def _weight_grad(a, b, ts, name, scale=1.0, col_shards=1):
    s, ka = a.shape
    nb = b.shape[1]
    nt = s // ts
    per = nb // col_shards

    def body(a_ref, b_ref, o_ref):
        @pl.when(pl.program_id(0) == 0)
        def _():
            o_ref[...] = jnp.zeros_like(o_ref)

        if col_shards == 1:
            o_ref[...] += _dot_tn(a_ref[...], b_ref[...])
        else:
            acc = _dot_tn(a_ref[...], b_ref[...])
            for j in range(col_shards):
                o_ref[j] += acc[:, j * per:(j + 1) * per]
        if scale != 1.0:
            @pl.when(pl.program_id(0) == nt - 1)
            def _():
                o_ref[...] = o_ref[...] * scale

    out_dims = (ka, nb) if col_shards == 1 else (col_shards, ka, per)
    return pl.pallas_call(
        body, name=name, grid=(nt,),
        in_specs=[pl.BlockSpec((ts, ka), lambda i: (i, 0)),
                  pl.BlockSpec((ts, nb), lambda i: (i, 0))],
        out_specs=pl.BlockSpec(out_dims, lambda i: (0,) * len(out_dims)),
        out_shape=jax.ShapeDtypeStruct(out_dims, F32),
        compiler_params=_params(),
    )(a, b)


def _sum_parts(gp_ref):
    g = gp_ref[0].astype(F32)
    for k in range(1, gp_ref.shape[0]):
        g = g + gp_ref[k].astype(F32)
    return g


def _adamw(g_parts, w, m, v, tr, name):
    nparts, rows, cols = g_parts.shape

    def body(gp_ref, w_ref, m_ref, v_ref, g_ref, d_ref, mo_ref, vo_ref):
        g = _sum_parts(gp_ref)
        m2 = ADAM_B1 * m_ref[...] + (1.0 - ADAM_B1) * g
        v2 = ADAM_B2 * v_ref[...] + (1.0 - ADAM_B2) * (g * g)
        m_hat = m2 / (1.0 - ADAM_B1 ** ADAM_STEP)
        v_hat = v2 / (1.0 - ADAM_B2 ** ADAM_STEP)
        g_ref[...] = g
        d_ref[...] = (-ADAM_LR) * (m_hat / (jnp.sqrt(v_hat) + ADAM_EPS) + ADAM_WD * w_ref[...])
        mo_ref[...] = m2
        vo_ref[...] = v2

    tile = pl.BlockSpec((tr, cols), lambda i: (i, 0))
    out = jax.ShapeDtypeStruct((rows, cols), F32)
    return pl.pallas_call(
        body, name=name, grid=(rows // tr,),
        in_specs=[pl.BlockSpec((nparts, tr, cols), lambda i: (0, i, 0)), tile, tile, tile],
        out_specs=[tile, tile, tile, tile],
        out_shape=[out, out, out, out],
        compiler_params=_params(),
    )(g_parts, w, m, v)


def _reduce_parts(g_parts, name):
    _, rows, cols = g_parts.shape

    def body(gp_ref, g_ref):
        g_ref[...] = _sum_parts(gp_ref)

    return pl.pallas_call(
        body, name=name,
        out_shape=jax.ShapeDtypeStruct((rows, cols), F32),
        compiler_params=pltpu.CompilerParams(vmem_limit_bytes=VMEM_LIMIT_BYTES),
    )(g_parts)


def _mesh_pos():
    ix, iy, ic = lax.axis_index("x"), lax.axis_index("y"), lax.axis_index("c")
    return ix, iy, ic


def _peer(ix, iy, ic, mask):
    px = 1 - ix if mask & 4 else ix
    py = 1 - iy if mask & 2 else iy
    pc = 1 - ic if mask & 1 else ic
    return (px, py, pc), 4 * px + 2 * py + pc


def _exchange(arrays, scatter, name):
    n = len(arrays)

    def body(*refs):
        x_refs, o_refs = refs[:n], refs[n:2 * n]
        send_sems, recv_sems, local_sems = refs[2 * n:]
        ix, iy, ic = _mesh_pos()
        me = 4 * ix + 2 * iy + ic

        def src(a, dest):
            return x_refs[a].at[dest] if scatter else x_refs[a]

        local = [pltpu.make_async_copy(src(a, me), o_refs[a].at[me], local_sems.at[a])
                 for a in range(n)]
        for cp in local:
            cp.start()
        sends = []
        for mask in range(1, N_DEV):
            peer, pidx = _peer(ix, iy, ic, mask)
            for a in range(n):
                cp = pltpu.make_async_remote_copy(
                    src_ref=src(a, pidx), dst_ref=o_refs[a].at[me],
                    send_sem=send_sems.at[a, mask - 1], recv_sem=recv_sems.at[a, mask - 1],
                    device_id=peer, device_id_type=pl.DeviceIdType.MESH)
                cp.start()
                sends.append(cp)
        for mask in range(1, N_DEV):
            peer, pidx = _peer(ix, iy, ic, mask)
            for a in range(n):
                pltpu.make_async_remote_copy(
                    src_ref=src(a, me), dst_ref=o_refs[a].at[pidx],
                    send_sem=send_sems.at[a, mask - 1], recv_sem=recv_sems.at[a, mask - 1],
                    device_id=peer, device_id_type=pl.DeviceIdType.MESH).wait_recv()
        for cp in sends:
            cp.wait_send()
        for cp in local:
            cp.wait()

    out_shape = [jax.ShapeDtypeStruct(x.shape if scatter else (N_DEV,) + x.shape, x.dtype)
                 for x in arrays]
    return pl.pallas_call(
        body, name=name,
        in_specs=[pl.BlockSpec(memory_space=pl.ANY)] * n,
        out_specs=[pl.BlockSpec(memory_space=pl.ANY)] * n,
        out_shape=out_shape,
        scratch_shapes=[pltpu.SemaphoreType.DMA((n, N_DEV - 1)),
                        pltpu.SemaphoreType.DMA((n, N_DEV - 1)),
                        pltpu.SemaphoreType.DMA((n,))],
    )(*arrays)


def _gather_two_level(arrays, name):
    n = len(arrays)

    def body(*refs):
        start, forward, finish = _gather_phases(refs[:n], refs[n:2 * n], *refs[2 * n:])
        start()
        forward()
        finish()

    return pl.pallas_call(
        body, name=name,
        in_specs=[pl.BlockSpec(memory_space=pl.ANY)] * n,
        out_specs=[pl.BlockSpec(memory_space=pl.ANY)] * n,
        out_shape=[jax.ShapeDtypeStruct((N_DEV,) + x.shape, x.dtype) for x in arrays],
        scratch_shapes=_gather_sems(n),
    )(*arrays)


def _gather_sems(n):
    return [pltpu.SemaphoreType.DMA((n, N_DEV - 1)), pltpu.SemaphoreType.DMA((n, N_DEV - 1)),
            pltpu.SemaphoreType.DMA((n,))]


def _gather_phases(x_refs, o_refs, send_sems, recv_sems, local_sems):
    n = len(x_refs)
    ix, iy, ic = _mesh_pos()
    me, sibling = (ix, iy, ic), (ix, iy, 1 - ic)
    chips = [(1 - ix, iy), (ix, 1 - iy), (1 - ix, 1 - iy)]

    def idx(px, py, pc):
        return 4 * px + 2 * py + pc

    def copy(a, k, block, to, src=None):
        dst = o_refs[a].at[idx(*block)]
        return pltpu.make_async_remote_copy(
            src_ref=dst if src is None else src, dst_ref=dst,
            send_sem=send_sems.at[a, k], recv_sem=recv_sems.at[a, k],
            device_id=to, device_id_type=pl.DeviceIdType.MESH)

    def local():
        return [pltpu.make_async_copy(x_refs[a], o_refs[a].at[idx(*me)], local_sems.at[a])
                for a in range(n)]

    def first():
        out = []
        for a in range(n):
            out.append(copy(a, 0, me, sibling, src=x_refs[a]))
            out += [copy(a, 1 + j, me, (*chip, ic), src=x_refs[a])
                    for j, chip in enumerate(chips)]
        return out

    def passed():
        return [copy(a, 4 + j, (*chip, ic), sibling)
                for j, chip in enumerate(chips) for a in range(n)]

    def start():
        for cp in local() + first():
            cp.start()

    def forward():
        for j, chip in enumerate(chips):
            for a in range(n):
                copy(a, 1 + j, (*chip, ic), me).wait_recv()
                copy(a, 4 + j, (*chip, ic), sibling).start()

    def finish():
        for a in range(n):
            copy(a, 0, sibling, me).wait_recv()
            for j, chip in enumerate(chips):
                copy(a, 4 + j, (*chip, 1 - ic), me).wait_recv()
        for cp in first() + passed():
            cp.wait_send()
        for cp in local():
            cp.wait()

    return start, forward, finish


def _swap_sibling(arrays, name):
    n = len(arrays)
    n_chips = N_DEV // 2

    def body(*refs):
        x_refs, got_refs = refs[:n], refs[n:2 * n]
        send_sems, recv_sems = refs[2 * n:]
        ix, iy, ic = _mesh_pos()
        sibling = (ix, iy, 1 - ic)
        sends = []
        for a in range(n):
            for q in range(n_chips):
                cp = pltpu.make_async_remote_copy(
                    src_ref=x_refs[a].at[q, 1 - ic], dst_ref=got_refs[a].at[q],
                    send_sem=send_sems.at[a, q], recv_sem=recv_sems.at[a, q],
                    device_id=sibling, device_id_type=pl.DeviceIdType.MESH)
                cp.start()
                sends.append(cp)
        for cp in sends:
            cp.wait()

    return pl.pallas_call(
        body, name=name,
        in_specs=[pl.BlockSpec(memory_space=pl.ANY)] * n,
        out_specs=[pl.BlockSpec(memory_space=pl.ANY)] * n,
        out_shape=[jax.ShapeDtypeStruct((n_chips,) + x.shape[2:], x.dtype) for x in arrays],
        scratch_shapes=[pltpu.SemaphoreType.DMA((n, n_chips)),
                        pltpu.SemaphoreType.DMA((n, n_chips))],
    )(*arrays)


def _exchange_chips(arrays, name):
    n = len(arrays)

    def body(*refs):
        start, finish = _chip_exchange_phases(refs[:n], refs[n:2 * n], *refs[2 * n:])
        start()
        finish()

    return pl.pallas_call(
        body, name=name,
        in_specs=[pl.BlockSpec(memory_space=pl.ANY)] * n,
        out_specs=[pl.BlockSpec(memory_space=pl.ANY)] * n,
        out_shape=[jax.ShapeDtypeStruct(x.shape, x.dtype) for x in arrays],
        scratch_shapes=_chip_exchange_sems(n),
    )(*arrays)


def _chip_exchange_sems(n):
    n_chips = N_DEV // 2
    return [pltpu.SemaphoreType.DMA((n, n_chips - 1)), pltpu.SemaphoreType.DMA((n, n_chips - 1)),
            pltpu.SemaphoreType.DMA((n,))]


def _chip_exchange_phases(x_refs, o_refs, send_sems, recv_sems, local_sems):
    n = len(x_refs)
    n_chips = N_DEV // 2
    ix, iy, ic = _mesh_pos()
    my_chip = 2 * ix + iy

    def peers():
        for mask in range(1, n_chips):
            px = 1 - ix if mask & 2 else ix
            py = 1 - iy if mask & 1 else iy
            yield mask, (px, py, ic), 2 * px + py

    def local():
        return [pltpu.make_async_copy(x_refs[a].at[my_chip], o_refs[a].at[my_chip],
                                      local_sems.at[a]) for a in range(n)]

    def sends():
        return [pltpu.make_async_remote_copy(
            src_ref=x_refs[a].at[chip], dst_ref=o_refs[a].at[my_chip],
            send_sem=send_sems.at[a, mask - 1], recv_sem=recv_sems.at[a, mask - 1],
            device_id=peer, device_id_type=pl.DeviceIdType.MESH)
            for mask, peer, chip in peers() for a in range(n)]

    def start():
        for cp in local() + sends():
            cp.start()

    def finish():
        for mask, peer, chip in peers():
            for a in range(n):
                pltpu.make_async_remote_copy(
                    src_ref=x_refs[a].at[my_chip], dst_ref=o_refs[a].at[chip],
                    send_sem=send_sems.at[a, mask - 1], recv_sem=recv_sems.at[a, mask - 1],
                    device_id=peer, device_id_type=pl.DeviceIdType.MESH).wait_recv()
        for cp in sends():
            cp.wait_send()
        for cp in local():
            cp.wait()

    return start, finish


def _pair_sum(core, x, got, name):
    nq, rows, cols = got.shape

    def body(c_ref, x_ref, g_ref, o_ref):
        o_ref[...] = (x_ref[...] + g_ref[...]).astype(BF16)

    blk = pl.BlockSpec((None, rows, cols), lambda q, c: (q, 0, 0))
    return pl.pallas_call(
        body, name=name,
        grid_spec=pltpu.PrefetchScalarGridSpec(
            num_scalar_prefetch=1, grid=(nq,),
            in_specs=[pl.BlockSpec((None, None, rows, cols), lambda q, c: (q, c[0], 0, 0)), blk],
            out_specs=blk),
        out_shape=jax.ShapeDtypeStruct(got.shape, BF16),
        compiler_params=_params(),
    )(core, x, got)


def _selectors():
    r = lax.broadcasted_iota(jnp.int32, (LANES, FOX_PAD), 0)
    c = lax.broadcasted_iota(jnp.int32, (LANES, FOX_PAD), 1)
    part, head_r = r // HEADS, r % HEADS
    head_c, lane_c = c // HEAD_PAD, c % HEAD_PAD
    same = (head_r == head_c) & (part < 3)
    sel_q = jnp.where(same & (lane_c == LANE_RB + part), 1.0, 0.0)
    sel_k = jnp.where(same & (lane_c == LANE_CK + part), -1.0, 0.0)
    sel = jnp.stack([sel_q, sel_k, jnp.zeros_like(sel_q)]).astype(BF16)
    lane = lax.broadcasted_iota(jnp.int32, (1, FOX_PAD), 1) % HEAD_PAD
    ones_q = jnp.where((lane >= LANE_CK) & (lane < LANE_CK + 3), 1.0, 0.0)
    ones_k = jnp.where(((lane >= LANE_RB) & (lane < LANE_RB + 3))
                       | ((lane >= LANE_LSE) & (lane < LANE_LSE + 3)), 1.0, 0.0)
    ones_v = jnp.where((lane >= LANE_ONE_V) & (lane < LANE_ONE_V + 2), 1.0, 0.0)
    bias = jnp.stack([ones_q, ones_k, ones_v]).astype(F32)
    return sel, bias


def _chip_sums(names, send):
    send = [a.reshape((N_DEV // 2, 2) + a.shape[1:]) for a in send]
    got = _swap_sibling(send, "swap_" + names[0])
    core = lax.axis_index("c").astype(jnp.int32).reshape(1)
    return [_pair_sum(core, a, b, "pair_sum_" + n) for n, a, b in zip(names, send, got)]


def _local_step(x, target, norm_g, final_g, w_in8, conv_w, conv_b, wa, ba, wx, bx, a_param,
                w_out_b, fox_in_shard, b_f, fox_out_shard, blk=512, ts=256):
    g0, g1 = norm_g[0:1], norm_g[1:2]
    gf = final_g.reshape(1, D_MODEL)
    wa_b, wx_b = wa.astype(BF16), wx.astype(BF16)
    sel, bias = _selectors()

    tm = min(2 * ts, x.shape[0])
    xb, gate1, h0, (fox_in8, fox_out8) = _lru_in_fwd(x, g0, w_in8, [fox_in_shard, fox_out_shard],
                                                     tm)
    fox_w_in = jnp.transpose(fox_in8, (1, 0, 2)).reshape(D_MODEL, FOX_IN_COLS)
    width = HEADS * HEAD_DIM
    wf_b = jnp.pad(fox_w_in[:, 4 * width:], ((0, 0), (0, LANES - HEADS)))
    bf_pad = jnp.pad(b_f, ((0, 0), (0, LANES - HEADS)))
    fo_b = fox_out8.reshape(width, D_MODEL)
    y1, hs = _lru_core_fwd(xb, gate1, conv_w, conv_b, wa_b, ba, wx_b, bx, a_param, ts)
    x1, h1, f, cparts = _fox_pre_fwd(x, y1, w_out_b, g1, wf_b, bf_pad, ts)
    qkv = _fox_proj_fwd(h1, cparts, fox_w_in, 0, 3, sel, bias, BF16, tm, "fox_proj_qkv")
    gate2 = _fox_proj_fwd(h1, None, fox_w_in, 3, 1, None, None, F32, tm, "fox_proj_gate")[0]
    o, qb = _attn_fwd(qkv, blk, hps=4)
    dx2, dx2b, y2, loss_acc, g_final = _fox_out_loss(o, gate2, fo_b, x1, target, gf, tm)

    do, dgate2 = _fox_out_bwd(dx2b, fo_b, o, gate2, ts)
    dq, dk, dv, dcum = _attn_bwd(qb, qkv, do, blk)
    dx1, dx1b, df, du_q, du_k, du_v, du_g, g_norm1, g_bf = _fox_in_bwd(
        dq, dk, dv, dgate2, fox_w_in, wf_b, dcum, f, x1, dx2, g1, ts)
    tw = 512
    g_q = _weight_grad(h1, du_q, tw, "grad_fox_wq", scale=QK_SCALE)
    g_k = _weight_grad(h1, du_k, tw, "grad_fox_wk")
    g_v = _weight_grad(h1, du_v, tw, "grad_fox_wv")
    g_g = _weight_grad(h1, du_g, tw, "grad_fox_wg")
    g_f = _weight_grad(h1, df, tw, "grad_fox_wf")
    g_fox_w_in = jnp.concatenate([g_q, g_k, g_v, g_g, g_f[:, :HEADS]], axis=1)
    g_fox_w_in = jnp.transpose(g_fox_w_in.reshape(D_MODEL, N_DEV, FOX_IN_SHARD), (1, 0, 2))
    g_fox_w_out = _weight_grad(y2, dx2b, tw, "grad_fox_w_out")
    fox_sums = _chip_sums(("fox_w_in", "fox_w_out"),
                          [g_fox_w_in, g_fox_w_out.reshape(N_DEV, -1, D_MODEL)])

    du, g_wa, g_wx, g_vec, (r_fox_in, r_fox_out) = _lru_core_bwd(
        dx1b, w_out_b, xb, gate1, hs, conv_w, conv_b, wa_b, ba, wx_b, bx, a_param,
        jnp.transpose(wa_b, (0, 2, 1)), jnp.transpose(wx_b, (0, 2, 1)), fox_sums, ts)
    g_lru_w_in = _weight_grad(h0, du, tw, "grad_lru_w_in", col_shards=N_DEV)
    g_lru_w_out = _weight_grad(y1, dx1b, tw, "grad_lru_w_out")
    conv_send = jnp.transpose(g_vec[0:CONV_WIDTH].reshape(CONV_WIDTH, N_DEV, -1), (1, 0, 2))
    lru_sums = _chip_sums(("lru_w_in", "lru_conv_w", "lru_w_out"),
                          [g_lru_w_in, conv_send, g_lru_w_out.reshape(N_DEV, -1, D_MODEL)])
    grad_x, g_norm0, (r_w_in, r_conv, r_w_out) = _lru_in_bwd(du, w_in8, x, dx1, g0, lru_sums, tm)

    small = dict(
        norm_g=jnp.concatenate([g_norm0, g_norm1], axis=0), final_g=g_final[0],
        lru_conv_b=g_vec[4:5], lru_wa=g_wa, lru_ba=g_vec[5:6], lru_wx=g_wx, lru_bx=g_vec[6:7],
        lru_a_param=g_vec[7:8], fox_b_f=g_bf[:, :HEADS])
    received = dict(lru_w_in=r_w_in, lru_conv_w=r_conv, lru_w_out=r_w_out, fox_w_in=r_fox_in,
                    fox_w_out=r_fox_out)
    return loss_acc[0, 0], grad_x, small, received


SMALL =("norm_g", "final_g", "lru_conv_b", "lru_wa", "lru_ba", "lru_wx", "lru_bx", "lru_a_param",
         "fox_b_f")
ALL_WEIGHTS = ("norm_g", "final_g", "lru_w_in", "lru_conv_w", "lru_conv_b", "lru_wa", "lru_ba",
               "lru_wx", "lru_bx", "lru_a_param", "lru_w_out", "fox_w_in", "fox_b_f", "fox_w_out")


def _pack_small(d):
    rows = []
    for n in SMALL:
        a = d[n].reshape(-1)
        if a.shape[0] % LANES:
            a = jnp.pad(a, (0, LANES - a.shape[0] % LANES))
        rows.append(a.reshape(-1, LANES))
    packed = jnp.concatenate(rows, axis=0)
    return jnp.pad(packed, ((0, N_DEV * SMALL_CHUNK_ROWS - packed.shape[0]), (0, 0)))


def _unpack_small(packed, like):
    out, off = {}, 0
    for n, nrows in zip(SMALL, SMALL_ROWS):
        size = like[n].size
        out[n] = packed[off:off + nrows].reshape(-1)[:size].reshape(like[n].shape)
        off += nrows
    return out


def kernel(x, norm_g, final_g, lru_w_in, lru_conv_w, lru_conv_b, lru_wa, lru_ba, lru_wx, lru_bx, lru_a_param, lru_w_out, fox_w_in, fox_b_f, fox_w_out, loss_target, m_norm_g, m_final_g, m_lru_w_in, m_lru_conv_w, m_lru_conv_b, m_lru_wa, m_lru_ba, m_lru_wx, m_lru_bx, m_lru_a_param, m_lru_w_out, m_fox_w_in, m_fox_b_f, m_fox_w_out, v_norm_g, v_final_g, v_lru_w_in, v_lru_conv_w, v_lru_conv_b, v_lru_wa, v_lru_ba, v_lru_wx, v_lru_bx, v_lru_a_param, v_lru_w_out, v_fox_w_in, v_fox_b_f, v_fox_w_out):
    w_loc = dict(norm_g=norm_g, final_g=final_g, lru_w_in=lru_w_in, lru_conv_w=lru_conv_w,
                 lru_conv_b=lru_conv_b, lru_wa=lru_wa, lru_ba=lru_ba, lru_wx=lru_wx, lru_bx=lru_bx,
                 lru_a_param=lru_a_param, lru_w_out=lru_w_out, fox_w_in=fox_w_in, fox_b_f=fox_b_f,
                 fox_w_out=fox_w_out)
    m_loc = dict(norm_g=m_norm_g, final_g=m_final_g, lru_w_in=m_lru_w_in, lru_conv_w=m_lru_conv_w,
                 lru_conv_b=m_lru_conv_b, lru_wa=m_lru_wa, lru_ba=m_lru_ba, lru_wx=m_lru_wx,
                 lru_bx=m_lru_bx, lru_a_param=m_lru_a_param, lru_w_out=m_lru_w_out,
                 fox_w_in=m_fox_w_in, fox_b_f=m_fox_b_f, fox_w_out=m_fox_w_out)
    v_loc = dict(norm_g=v_norm_g, final_g=v_final_g, lru_w_in=v_lru_w_in, lru_conv_w=v_lru_conv_w,
                 lru_conv_b=v_lru_conv_b, lru_wa=v_lru_wa, lru_ba=v_lru_ba, lru_wx=v_lru_wx,
                 lru_bx=v_lru_bx, lru_a_param=v_lru_a_param, lru_w_out=v_lru_w_out,
                 fox_w_in=v_fox_w_in, fox_b_f=v_fox_b_f, fox_w_out=v_fox_w_out)

    w_in8, conv8, w_out8 = _gather_two_level(
        [lru_w_in[0].astype(BF16), lru_conv_w[0], lru_w_out[0].astype(BF16)], "gather_weights")
    conv_full = jnp.transpose(conv8, (1, 0, 2)).reshape(CONV_WIDTH, LRU_WIDTH)

    loss, grad_x, small_grads, received = _local_step(
        x[0], loss_target[0], norm_g, final_g, w_in8, conv_full, lru_conv_b, lru_wa[0], lru_ba,
        lru_wx[0], lru_bx, lru_a_param, w_out8.reshape(LRU_WIDTH, D_MODEL),
        fox_w_in[0].astype(BF16), fox_b_f, fox_w_out[0].astype(BF16))

    out = {}
    for n, tr in (("lru_w_in", 256), ("lru_conv_w", CONV_WIDTH), ("lru_w_out", 96),
                  ("fox_w_in", 128), ("fox_w_out", 64)):
        res = _adamw(received[n], w_loc[n][0], m_loc[n][0], v_loc[n][0], tr, "adamw_" + n)
        out[n] = [a[None] for a in res]

    small_sums = _chip_sums(
        ("small",), [_pack_small(small_grads).reshape(N_DEV, SMALL_CHUNK_ROWS, LANES)])
    r_small, = _exchange_chips(small_sums, "scatter_small_grads")

    g_chunk = _reduce_parts(r_small, "reduce_small_grads")
    g_small, = _exchange([g_chunk], False, "gather_small_grads")
    g_small = g_small.reshape(1, N_DEV * SMALL_CHUNK_ROWS, LANES)
    res = _adamw(g_small, _pack_small(w_loc), _pack_small(m_loc), _pack_small(v_loc),
                 N_DEV * SMALL_CHUNK_ROWS, "adamw_replicated")
    small_out = [_unpack_small(a, w_loc) for a in res]
    for n in SMALL:
        out[n] = [d[n] for d in small_out]

    loss = lax.psum(loss, ("x", "y", "c"))
    return (loss, grad_x[None], *[out[n][0] for n in ALL_WEIGHTS], *[out[n][1] for n in ALL_WEIGHTS],
            *[out[n][2] for n in ALL_WEIGHTS], *[out[n][3] for n in ALL_WEIGHTS])
```

```python
import functools

import jax
import jax.numpy as jnp
from jax import lax
from jax.experimental import pallas as pl
from jax.experimental.pallas import tpu as pltpu

F32 = jnp.float32
BF16 = jnp.bfloat16

D_MODEL = 1024
LRU_WIDTH = 1536
LRU_BLOCKS = 12
LRU_BLOCK_W = 128
CONV_WIDTH = 4
LRU_C = 8.0
HEADS = 16
HEAD_DIM = 64
HEAD_PAD = 128
FOX_PAD = HEADS * HEAD_PAD
HEADS_PER_STEP = 2
QK_SCALE = 1.0 / HEAD_DIM ** 0.5
EPS = 1e-6
NEG_BIG = -1e30
N_DEV = 8

ADAM_LR = 0.001
ADAM_B1 = 0.9
ADAM_B2 = 0.999
ADAM_EPS = 1e-08
ADAM_WD = 0.01
ADAM_STEP = 10

LANE_RB = 64
LANE_CK = 67
LANE_LSE = 70
LANE_ONE_V = 64

VMEM_LIMIT_BYTES = 56 * 1024 * 1024
LANES = 128
SUBLANES = 8

LRU_IN_SHARD = 2 * LRU_WIDTH // N_DEV
FOX_IN_COLS = 4 * HEADS * HEAD_DIM + HEADS
FOX_IN_SHARD = FOX_IN_COLS // N_DEV

SMALL_ROWS = (16, 8, 12, 1536, 12, 1536, 12, 12, 1)
SMALL_CHUNK_ROWS = 400
assert sum(SMALL_ROWS) <= N_DEV * SMALL_CHUNK_ROWS


def _params(n_grid_axes=1):
    return pltpu.CompilerParams(
        dimension_semantics=("arbitrary",) * n_grid_axes,
        vmem_limit_bytes=VMEM_LIMIT_BYTES)


def _const_spec(shape):
    nd = len(shape)
    return pl.BlockSpec(shape, lambda *_: (0,) * nd, pipeline_mode=pl.Buffered(1))


def _shift_down(x, k, fill):
    rows = lax.broadcasted_iota(jnp.int32, x.shape, 0)
    return jnp.where(rows >= k, pltpu.roll(x, k, 0), fill)


def _shift_up(x, k, fill):
    n = x.shape[0]
    rows = lax.broadcasted_iota(jnp.int32, x.shape, 0)
    return jnp.where(rows < n - k, pltpu.roll(x, n - k, 0), fill)


def _scan_rows(a, b, reverse=False):
    n = a.shape[0]
    shift = _shift_up if reverse else _shift_down
    k = 1
    while k < n:
        b = a * shift(b, k, 0.0) + b
        a = a * shift(a, k, 1.0)
        k *= 2
    return a, b


def _cumsum_rows(x, reverse=False):
    n = x.shape[0]
    shift = _shift_up if reverse else _shift_down
    k = 1
    while k < n:
        x = x + shift(x, k, 0.0)
        k *= 2
    return x


def _rstd(x):
    return lax.rsqrt(jnp.mean(x * x, axis=-1, keepdims=True) + EPS)


def _norm_bwd(x, g, dh):
    rstd = _rstd(x)
    xhat = x * rstd
    dg = jnp.sum(dh * xhat, axis=0, keepdims=True)
    dxh = dh * g
    dx = rstd * (dxh - xhat * jnp.mean(dxh * xhat, axis=-1, keepdims=True))
    return dx, dg


def _split3(x):
    hi = x.astype(BF16)
    r1 = x - hi.astype(F32)
    mid = r1.astype(BF16)
    lo = (r1 - mid.astype(F32)).astype(BF16)
    return hi, mid, lo


def _sigmoid(x):
    return jax.nn.sigmoid(x)


def _dot(a, b):
    return jnp.dot(a, b, preferred_element_type=F32)


def _dot_nt(a, b):
    return lax.dot_general(a, b, (((1,), (1,)), ((), ())), preferred_element_type=F32)


def _dot_tn(a, b):
    return lax.dot_general(a, b, (((0,), (0,)), ((), ())), preferred_element_type=F32)


def _heads_to_padded(u):
    n = u.shape[0]
    low = lax.broadcasted_iota(jnp.int32, (n, LANES), 1) < HEAD_DIM
    zero = jnp.zeros((n, LANES), u.dtype)
    cols = []
    for p in range(HEADS // 2):
        pair = u[:, p * LANES:(p + 1) * LANES]
        cols.append(jnp.where(low, pair, zero))
        cols.append(jnp.where(low, pltpu.roll(pair, HEAD_DIM, 1), zero))
    return jnp.concatenate(cols, axis=1)


def _heads_from_padded(x):
    n = x.shape[0]
    low = lax.broadcasted_iota(jnp.int32, (n, LANES), 1) < HEAD_DIM
    cols = []
    for p in range(HEADS // 2):
        even = x[:, (2 * p) * HEAD_PAD:(2 * p + 1) * HEAD_PAD]
        odd = x[:, (2 * p + 1) * HEAD_PAD:(2 * p + 2) * HEAD_PAD]
        cols.append(jnp.where(low, even, pltpu.roll(odd, HEAD_DIM, 1)))
    return jnp.concatenate(cols, axis=1)


def _conv_taps(xb, prev8):
    rows8 = lax.broadcasted_iota(jnp.int32, prev8.shape, 0)
    taps = [xb]
    for j in range(1, CONV_WIDTH):
        r = pltpu.roll(xb, j, 0)
        p = pltpu.roll(prev8, j, 0)
        head = jnp.where(rows8 < j, p, r[0:SUBLANES])
        taps.append(jnp.concatenate([head, r[SUBLANES:]], axis=0))
    return taps


def _lru_pre(taps, cw, cb, wa_ref, ba, wx_ref, bx, a_param):
    xc = cb + cw[3:4] * taps[0] + cw[2:3] * taps[1] + cw[1:2] * taps[2] + cw[0:1] * taps[3]
    xcb = xc.astype(BF16)
    ra, ia = [], []
    for n in range(LRU_BLOCKS):
        blk = xcb[:, n * LRU_BLOCK_W:(n + 1) * LRU_BLOCK_W]
        ra.append(_dot(blk, wa_ref[n]))
        ia.append(_dot(blk, wx_ref[n]))
    r = _sigmoid(jnp.concatenate(ra, axis=1) + ba)
    i = _sigmoid(jnp.concatenate(ia, axis=1) + bx)
    z = -a_param
    sp = jnp.maximum(z, 0.0) + jnp.log1p(jnp.exp(-jnp.abs(z)))
    log_a = (-LRU_C) * r * sp
    a = jnp.exp(log_a)
    one_minus_a2 = -jnp.tanh(log_a) * (a * a + 1.0)
    mult = jnp.sqrt(one_minus_a2)
    return xc, xcb, r, i, sp, a, mult


def _lru_in_fwd(x, g0, w_in, later_shards, ts):
    s = x.shape[0]
    nt = s // ts
    n = len(later_shards)

    def body(*refs):
        x_ref, g_ref, w_ref = refs[:3]
        shard_refs = refs[3:3 + n]
        xb_ref, gate_ref, h_ref = refs[3 + n:6 + n]
        wfull_ref = refs[6 + 2 * n]
        start, forward, finish = _gather_phases(shard_refs, refs[6 + n:6 + 2 * n],
                                                *refs[7 + 2 * n:])
        step = pl.program_id(0)
        pl.when(step == 0)(start)

        @pl.when(step == 0)
        def _():
            for j in range(N_DEV):
                wfull_ref[:, j * LRU_IN_SHARD:(j + 1) * LRU_IN_SHARD] = w_ref[j]

        xv = x_ref[...]
        h = (xv * _rstd(xv) * g_ref[...]).astype(BF16)
        u = _dot(h, wfull_ref[...])
        xb_ref[...] = u[:, :LRU_WIDTH]
        gate_ref[...] = u[:, LRU_WIDTH:]
        h_ref[...] = h
        pl.when(step == (2 * nt) // 3)(forward)
        pl.when(step == nt - 1)(finish)

    hbm = pl.BlockSpec(memory_space=pl.ANY)
    res = pl.pallas_call(
        body, name="lru_in_fwd", grid=(nt,),
        in_specs=[pl.BlockSpec((ts, D_MODEL), lambda i: (i, 0)),
                  _const_spec((1, D_MODEL)),
                  _const_spec((N_DEV, D_MODEL, LRU_IN_SHARD))] + [hbm] * n,
        out_specs=[pl.BlockSpec((ts, LRU_WIDTH), lambda i: (i, 0)),
                   pl.BlockSpec((ts, LRU_WIDTH), lambda i: (i, 0)),
                   pl.BlockSpec((ts, D_MODEL), lambda i: (i, 0))] + [hbm] * n,
        out_shape=[jax.ShapeDtypeStruct((s, LRU_WIDTH), F32),
                   jax.ShapeDtypeStruct((s, LRU_WIDTH), F32),
                   jax.ShapeDtypeStruct((s, D_MODEL), BF16)]
        + [jax.ShapeDtypeStruct((N_DEV,) + a.shape, a.dtype) for a in later_shards],
        scratch_shapes=[pltpu.VMEM((D_MODEL, 2 * LRU_WIDTH), BF16)] + _gather_sems(n),
        compiler_params=_params(),
    )(x, g0, w_in, *later_shards)
    return res[0], res[1], res[2], res[3:]


def _lru_core_fwd(xb, gate, cw, cb, wa, ba, wx, bx, a_param, ts):
    s = xb.shape[0]

    def body(xb_ref, gate_ref, cw_ref, cb_ref, wa_ref, ba_ref, wx_ref, bx_ref, ap_ref,
             y_ref, hs_ref, prev_ref, hcar_ref):
        @pl.when(pl.program_id(0) == 0)
        def _():
            prev_ref[...] = jnp.zeros_like(prev_ref)
            hcar_ref[...] = jnp.zeros_like(hcar_ref)

        xbv = xb_ref[...]
        taps = _conv_taps(xbv, prev_ref[...])
        xc, _, _, i, _, a, mult = _lru_pre(taps, cw_ref[...], cb_ref[...], wa_ref, ba_ref[...],
                                           wx_ref, bx_ref[...], ap_ref[...])
        bterm = mult * (i * xc)
        cum_a, hloc = _scan_rows(a, bterm)
        hs = cum_a * hcar_ref[SUBLANES - 1:SUBLANES, :] + hloc
        gv = gate_ref[...]
        y_ref[...] = (hs * (gv * _sigmoid(gv))).astype(BF16)
        hs_ref[...] = hs
        prev_ref[...] = xbv[ts - SUBLANES:, :]
        hcar_ref[...] = hs[ts - SUBLANES:, :]

    vec = _const_spec((1, LRU_WIDTH))
    blk = _const_spec((LRU_BLOCKS, LRU_BLOCK_W, LRU_BLOCK_W))
    tile = pl.BlockSpec((ts, LRU_WIDTH), lambda i: (i, 0))
    return pl.pallas_call(
        body, name="lru_core_fwd", grid=(s // ts,),
        in_specs=[tile, tile, _const_spec((CONV_WIDTH, LRU_WIDTH)), vec, blk, vec, blk, vec, vec],
        out_specs=[tile, tile],
        out_shape=[jax.ShapeDtypeStruct((s, LRU_WIDTH), BF16),
                   jax.ShapeDtypeStruct((s, LRU_WIDTH), F32)],
        scratch_shapes=[pltpu.VMEM((SUBLANES, LRU_WIDTH), F32),
                        pltpu.VMEM((SUBLANES, LRU_WIDTH), F32)],
        compiler_params=_params(),
    )(xb, gate, cw, cb, wa, ba, wx, bx, a_param)


def _fox_pre_fwd(x, y, w_out, g1, wf, bf, ts):
    s = x.shape[0]

    def body(x_ref, y_ref, w_ref, g_ref, wf_ref, bf_ref, x1_ref, h1_ref, f_ref, cp_ref, ccar_ref):
        @pl.when(pl.program_id(0) == 0)
        def _():
            ccar_ref[...] = jnp.zeros_like(ccar_ref)

        x1 = x_ref[...] + _dot(y_ref[...], w_ref[...])
        h1 = (x1 * _rstd(x1) * g_ref[...]).astype(BF16)
        f = _dot(h1, wf_ref[...]) + bf_ref[...]
        logsig = jnp.minimum(f, 0.0) - jnp.log1p(jnp.exp(-jnp.abs(f)))
        cum = _cumsum_rows(logsig) + ccar_ref[SUBLANES - 1:SUBLANES, :]
        hi, mid, lo = _split3(cum)
        lane = lax.broadcasted_iota(jnp.int32, cum.shape, 1)
        packed = jnp.where(lane < HEADS, hi.astype(F32), jnp.where(
            lane < 2 * HEADS, pltpu.roll(mid.astype(F32), HEADS, 1), jnp.where(
                lane < 3 * HEADS, pltpu.roll(lo.astype(F32), 2 * HEADS, 1), 0.0)))
        x1_ref[...] = x1
        h1_ref[...] = h1
        f_ref[...] = f
        cp_ref[...] = packed.astype(BF16)
        ccar_ref[...] = cum[ts - SUBLANES:, :]

    return pl.pallas_call(
        body, name="fox_pre_fwd", grid=(s // ts,),
        in_specs=[pl.BlockSpec((ts, D_MODEL), lambda i: (i, 0)),
                  pl.BlockSpec((ts, LRU_WIDTH), lambda i: (i, 0)),
                  _const_spec((LRU_WIDTH, D_MODEL)),
                  _const_spec((1, D_MODEL)),
                  _const_spec((D_MODEL, LANES)),
                  _const_spec((1, LANES))],
        out_specs=[pl.BlockSpec((ts, D_MODEL), lambda i: (i, 0)),
                   pl.BlockSpec((ts, D_MODEL), lambda i: (i, 0)),
                   pl.BlockSpec((ts, LANES), lambda i: (i, 0)),
                   pl.BlockSpec((ts, LANES), lambda i: (i, 0))],
        out_shape=[jax.ShapeDtypeStruct((s, D_MODEL), F32),
                   jax.ShapeDtypeStruct((s, D_MODEL), BF16),
                   jax.ShapeDtypeStruct((s, LANES), F32),
                   jax.ShapeDtypeStruct((s, LANES), BF16)],
        scratch_shapes=[pltpu.VMEM((SUBLANES, LANES), F32)],
        compiler_params=_params(),
    )(x, y, w_out, g1, wf, bf)


def _fox_proj_fwd(h1, cparts, w, first, ng, sel, bias, out_dtype, ts, name):
    s = h1.shape[0]
    width = HEADS * HEAD_DIM
    use_sel = sel is not None

    def body(*refs):
        if use_sel:
            h_ref, cp_ref, w_ref, sel_ref, b_ref, o_ref = refs
            proj = _dot(h_ref[...], w_ref[...])
            if first == 0:
                proj = proj * jnp.where(pl.program_id(0) == 0, QK_SCALE, 1.0)
            acc = _heads_to_padded(proj) + _dot(cp_ref[...], sel_ref[...]) + b_ref[...]
        else:
            h_ref, w_ref, o_ref = refs
            acc = _heads_to_padded(_dot(h_ref[...], w_ref[...]))
        o_ref[...] = acc.astype(out_dtype)

    in_specs = [pl.BlockSpec((ts, D_MODEL), lambda j, i: (i, 0))]
    args = [h1]
    if use_sel:
        in_specs.append(pl.BlockSpec((ts, LANES), lambda j, i: (i, 0)))
        args.append(cparts)
    in_specs.append(pl.BlockSpec((D_MODEL, width), lambda j, i: (0, first + j)))
    args.append(w)
    if use_sel:
        in_specs.append(pl.BlockSpec((None, LANES, FOX_PAD), lambda j, i: (j, 0, 0)))
        in_specs.append(pl.BlockSpec((None, 1, FOX_PAD), lambda j, i: (j, 0, 0)))
        args += [sel, bias]
    return pl.pallas_call(
        body, name=name, grid=(ng, s // ts),
        in_specs=in_specs,
        out_specs=pl.BlockSpec((None, ts, FOX_PAD), lambda j, i: (j, i, 0)),
        out_shape=jax.ShapeDtypeStruct((ng, s, FOX_PAD), out_dtype),
        compiler_params=_params(2),
    )(*args)


def _attn_fwd(qkv, blk, hps=HEADS_PER_STEP):
    s = qkv.shape[1]
    nblk = s // blk
    wide = 2 * blk
    heads = [slice(i * HEAD_PAD, (i + 1) * HEAD_PAD) for i in range(hps)]

    def body(q_ref, k_ref, v_ref, o_ref, qb_ref, acc_ref, m_ref):
        qi = pl.program_id(1)
        row = lax.broadcasted_iota(jnp.int32, (blk, blk), 0)
        col = lax.broadcasted_iota(jnp.int32, (blk, blk), 1)
        lane = lax.broadcasted_iota(jnp.int32, (blk, HEAD_PAD), 1)
        qs = [q_ref[:, hd] for hd in heads]
        for i in range(hps):
            acc_ref[i] = jnp.zeros((blk, HEAD_PAD), F32)
            m_ref[i] = jnp.full((blk, HEAD_PAD), NEG_BIG, F32)

        def step(k0, size, masked):
            scores = [_dot_nt(q, k_ref[pl.ds(k0, size), hd]) for q, hd in zip(qs, heads)]
            for i, (sc, hd) in enumerate(zip(scores, heads)):
                v = v_ref[pl.ds(k0, size), hd]
                if masked:
                    sc = jnp.where(col <= row, sc, NEG_BIG)
                m = m_ref[i]
                m_new = jnp.maximum(m, jnp.max(sc, axis=-1, keepdims=True))
                p = jnp.exp((sc - jnp.tile(m_new, (1, size // HEAD_PAD))).astype(BF16))
                acc_ref[i] = jnp.exp(m - m_new) * acc_ref[i] + _dot(p, v)
                m_ref[i] = m_new

        def wide_step(kk, _):
            step(pl.multiple_of(kk * wide, wide), wide, False)
            return 0

        lax.fori_loop(0, qi // 2, wide_step, 0)

        @pl.when(qi % 2 == 1)
        def _():
            step(pl.multiple_of((qi - 1) * blk, blk), blk, False)

        step(pl.multiple_of(qi * blk, blk), blk, True)
        for i, (q, hd) in enumerate(zip(qs, heads)):
            acc = acc_ref[i]
            l = jnp.broadcast_to(acc[:, LANE_ONE_V:LANE_ONE_V + 1], (blk, HEAD_PAD))
            o_ref[:, hd] = (acc / l).astype(BF16)
            hi, mid, lo = _split3(-(m_ref[i] + jnp.log(l)))
            qb_ref[:, hd] = jnp.where(lane == LANE_LSE, hi, jnp.where(
                lane == LANE_LSE + 1, mid, jnp.where(lane == LANE_LSE + 2, lo, q)))

    width = hps * HEAD_PAD

    def whole(j):
        return pl.BlockSpec((None, s, width), lambda h, i: (j, 0, h))

    out_spec = pl.BlockSpec((blk, width), lambda h, i: (i, h))
    return pl.pallas_call(
        body, name="attn_fwd", grid=(HEADS // hps, nblk),
        in_specs=[pl.BlockSpec((None, blk, width), lambda h, i: (0, i, h)), whole(1), whole(2)],
        out_specs=[out_spec, out_spec],
        out_shape=[jax.ShapeDtypeStruct((s, FOX_PAD), BF16),
                   jax.ShapeDtypeStruct((s, FOX_PAD), BF16)],
        scratch_shapes=[pltpu.VMEM((hps, blk, HEAD_PAD), F32),
                        pltpu.VMEM((hps, blk, HEAD_PAD), F32)],
        compiler_params=_params(2),
    )(qkv, qkv, qkv)


def _fox_out_loss(o, gate, w_out, x1, target, gf, ts):
    s = x1.shape[0]

    def body(o_ref, gt_ref, w_ref, x1_ref, t_ref, g_ref, dx2_ref, dx2b_ref, y2_ref, loss_ref,
             gfin_ref):
        @pl.when(pl.program_id(0) == 0)
        def _():
            loss_ref[...] = jnp.zeros_like(loss_ref)
            gfin_ref[...] = jnp.zeros_like(gfin_ref)

        gv = gt_ref[...]
        y2 = _heads_from_padded(o_ref[...] * (gv * _sigmoid(gv))).astype(BF16)
        x2 = x1_ref[...] + _dot(y2, w_ref[...])
        rstd = _rstd(x2)
        xhat = x2 * rstd
        g = g_ref[...]
        diff = xhat * g - t_ref[...]
        loss_ref[...] += 0.5 * jnp.sum(jnp.mean(diff * diff, axis=-1, keepdims=True))
        dy = diff * (1.0 / D_MODEL)
        gfin_ref[...] += jnp.sum(dy * xhat, axis=0, keepdims=True)
        dxh = dy * g
        dx2 = rstd * (dxh - xhat * jnp.mean(dxh * xhat, axis=-1, keepdims=True))
        dx2_ref[...] = dx2
        dx2b_ref[...] = dx2.astype(BF16)
        y2_ref[...] = y2

    return pl.pallas_call(
        body, name="fox_out_loss", grid=(s // ts,),
        in_specs=[pl.BlockSpec((ts, FOX_PAD), lambda i: (i, 0)),
                  pl.BlockSpec((ts, FOX_PAD), lambda i: (i, 0)),
                  _const_spec((HEADS * HEAD_DIM, D_MODEL)),
                  pl.BlockSpec((ts, D_MODEL), lambda i: (i, 0)),
                  pl.BlockSpec((ts, D_MODEL), lambda i: (i, 0)),
                  _const_spec((1, D_MODEL))],
        out_specs=[pl.BlockSpec((ts, D_MODEL), lambda i: (i, 0)),
                   pl.BlockSpec((ts, D_MODEL), lambda i: (i, 0)),
                   pl.BlockSpec((ts, HEADS * HEAD_DIM), lambda i: (i, 0)),
                   pl.BlockSpec((SUBLANES, LANES), lambda i: (0, 0)),
                   pl.BlockSpec((1, D_MODEL), lambda i: (0, 0))],
        out_shape=[jax.ShapeDtypeStruct((s, D_MODEL), F32),
                   jax.ShapeDtypeStruct((s, D_MODEL), BF16),
                   jax.ShapeDtypeStruct((s, HEADS * HEAD_DIM), BF16),
                   jax.ShapeDtypeStruct((SUBLANES, LANES), F32),
                   jax.ShapeDtypeStruct((1, D_MODEL), F32)],
        compiler_params=_params(),
    )(o, gate, w_out, x1, target, gf)


def _fox_out_bwd(dx2, w_out, o, gate, ts):
    s = dx2.shape[0]

    def body(dx_ref, w_ref, o_ref, gt_ref, do_ref, dg_ref):
        lane = lax.broadcasted_iota(jnp.int32, (ts, HEAD_PAD), 1)
        dy2 = _heads_to_padded(_dot_nt(dx_ref[...], w_ref[...]))
        gv = gt_ref[...]
        sg = _sigmoid(gv)
        ov = o_ref[...]
        dov = dy2 * (gv * sg)
        dg_ref[...] = (dy2 * ov * (sg * (1.0 + gv * (1.0 - sg)))).astype(BF16)
        prod = dov * ov
        for h in range(HEADS):
            sl = slice(h * HEAD_PAD, (h + 1) * HEAD_PAD)
            delta = jnp.sum(prod[:, sl], axis=-1, keepdims=True)
            hi = delta.astype(BF16)
            lo = (delta - hi.astype(F32)).astype(BF16)
            do_h = dov[:, sl].astype(BF16)
            do_ref[:, sl] = jnp.where(lane == LANE_ONE_V, -hi,
                                      jnp.where(lane == LANE_ONE_V + 1, -lo, do_h))

    tile = pl.BlockSpec((ts, FOX_PAD), lambda i: (i, 0))
    return pl.pallas_call(
        body, name="fox_out_bwd", grid=(s // ts,),
        in_specs=[pl.BlockSpec((ts, D_MODEL), lambda i: (i, 0)),
                  _const_spec((HEADS * HEAD_DIM, D_MODEL)), tile, tile],
        out_specs=[tile, tile],
        out_shape=[jax.ShapeDtypeStruct((s, FOX_PAD), BF16),
                   jax.ShapeDtypeStruct((s, FOX_PAD), BF16)],
        compiler_params=_params(),
    )(dx2, w_out, o, gate)


def _attn_bwd(qb, qkv, do, blk):
    s = qb.shape[0]
    nblk = s // blk
    half = blk // 2
    heads = [slice(i * HEAD_PAD, (i + 1) * HEAD_PAD) for i in range(HEADS_PER_STEP)]

    def body(q_ref, k_ref, v_ref, do_ref, dq_ref, dk_ref, dv_ref, dcum_ref, dq_acc, dkt_acc,
             dvt_acc, qt_ref, dot_ref):
        group = pl.program_id(0)
        kj = pl.program_id(1)
        row = lax.broadcasted_iota(jnp.int32, (blk, blk), 0)
        col = lax.broadcasted_iota(jnp.int32, (blk, blk), 1)
        lane = lax.broadcasted_iota(jnp.int32, (blk, LANES), 1)
        mine = [lane == group * HEADS_PER_STEP + i for i in range(HEADS_PER_STEP)]

        @pl.when(kj == 0)
        def _():
            dq_acc[...] = jnp.zeros_like(dq_acc)

            def transpose_block(bi, _):
                r0 = pl.multiple_of(bi * blk, blk)
                for i, hd in enumerate(heads):
                    qt_ref[i, bi] = q_ref[pl.ds(r0, blk), hd].T
                    dot_ref[i, bi] = do_ref[pl.ds(r0, blk), hd].T
                return 0

            lax.fori_loop(0, nblk, transpose_block, 0)

        @pl.when((group == 0) & (kj == 0))
        def _():
            dcum_ref[...] = jnp.zeros_like(dcum_ref)

        k0 = pl.multiple_of(kj * blk, blk)
        ks = [k_ref[:, hd] for hd in heads]
        vs = [v_ref[:, hd] for hd in heads]

        def step(qi, q_lo, nq, k_lo, nk, masked):
            parts = ([(0, q_lo, 0, nq)] if nq <= blk
                     else [(b, 0, b * blk, blk) for b in range(nq // blk)])
            q0 = pl.multiple_of(qi * blk + q_lo, half)
            qs = [q_ref[pl.ds(q0, nq), hd] for hd in heads]
            dos = [do_ref[pl.ds(q0, nq), hd] for hd in heads]
            kk = [k[k_lo:k_lo + nk] for k in ks]
            vv = [v[k_lo:k_lo + nk] for v in vs]
            scores = [_dot_nt(q, k) for q, k in zip(qs, kk)]
            dps = [_dot_nt(dov, v) for dov, v in zip(dos, vv)]
            for i, (hd, k, sc, dp) in enumerate(zip(heads, kk, scores, dps)):
                p = jnp.exp(sc.astype(BF16))
                if masked:
                    p = jnp.where(col[:nq, :nk] + k_lo <= row[:nq, :nk] + q_lo, p,
                                  jnp.zeros_like(p))
                ds = (p.astype(F32) * dp).astype(BF16)
                dvt = sum(_dot(dot_ref[i, qi + b, :, c:c + n], p[r:r + n]) for b, c, r, n in parts)
                dkt = sum(_dot(qt_ref[i, qi + b, :, c:c + n], ds[r:r + n]) for b, c, r, n in parts)
                if masked:
                    dvt_acc[i, :, k_lo:k_lo + nk] = dvt
                    dkt_acc[i, :, k_lo:k_lo + nk] = dkt
                else:
                    dvt_acc[i, :, k_lo:k_lo + nk] += dvt
                    dkt_acc[i, :, k_lo:k_lo + nk] += dkt
                dq_acc[pl.ds(q0, nq), hd] += _dot(ds, k)

        step(kj, 0, blk, 0, half, True)
        step(kj, half, half, half, half, True)

        n_after = nblk - 1 - kj

        def q_step(t, _):
            step(kj + 1 + 2 * t, 0, 2 * blk, 0, blk, False)
            return 0

        lax.fori_loop(0, n_after // 2, q_step, 0)

        @pl.when(n_after % 2 == 1)
        def _():
            step(nblk - 1, 0, blk, 0, blk, False)
        dcum = dcum_ref[pl.ds(k0, blk), :]
        for i, (hd, mask) in enumerate(zip(heads, mine)):
            dk = dkt_acc[i].T
            dk_ref[:, hd] = dk.astype(BF16)
            dv_ref[:, hd] = dvt_acc[i].astype(BF16).T
            dcum = jnp.where(mask, -dk[:, LANE_CK:LANE_CK + 1], dcum)
        dcum_ref[pl.ds(k0, blk), :] = dcum

        @pl.when(kj == nblk - 1)
        def _():
            def finish(bi, _):
                r0 = pl.multiple_of(bi * blk, blk)
                dcum = dcum_ref[pl.ds(r0, blk), :]
                for hd, mask in zip(heads, mine):
                    dq = dq_acc[pl.ds(r0, blk), hd]
                    dq_ref[pl.ds(r0, blk), hd] = dq.astype(BF16)
                    dcum = dcum + jnp.where(mask, dq[:, LANE_RB:LANE_RB + 1], 0.0)
                dcum_ref[pl.ds(r0, blk), :] = dcum
                return 0

            lax.fori_loop(0, nblk, finish, 0)

    width = HEADS_PER_STEP * HEAD_PAD
    whole = pl.BlockSpec((s, width), lambda h, j: (0, h))
    part = pl.BlockSpec((blk, width), lambda h, j: (j, h))
    out = jax.ShapeDtypeStruct((s, FOX_PAD), BF16)
    return pl.pallas_call(
        body, name="attn_bwd", grid=(HEADS // HEADS_PER_STEP, nblk),
        in_specs=[whole,
                  pl.BlockSpec((None, blk, width), lambda h, j: (1, j, h)),
                  pl.BlockSpec((None, blk, width), lambda h, j: (2, j, h)),
                  whole],
        out_specs=[whole, part, part, pl.BlockSpec((s, LANES), lambda h, j: (0, 0))],
        out_shape=[out, out, out, jax.ShapeDtypeStruct((s, LANES), F32)],
        scratch_shapes=[pltpu.VMEM((s, width), F32),
                        pltpu.VMEM((HEADS_PER_STEP, HEAD_PAD, blk), F32),
                        pltpu.VMEM((HEADS_PER_STEP, HEAD_PAD, blk), F32),
                        pltpu.VMEM((HEADS_PER_STEP, nblk, HEAD_PAD, blk), BF16),
                        pltpu.VMEM((HEADS_PER_STEP, nblk, HEAD_PAD, blk), BF16)],
        compiler_params=_params(2),
    )(qb, qkv, qkv, do)


def _fox_in_bwd(dq, dk, dv, dg, wt, wft, dcum, f, x1, dx2, g1, ts):
    s = x1.shape[0]
    nt = s // ts
    width = HEADS * HEAD_DIM

    def body(dq_ref, dk_ref, dv_ref, dg_ref, wt_ref, wft_ref, dcum_ref, f_ref, x1_ref, dx2_ref,
             g_ref, dx1_ref, dx1b_ref, df_ref, duq_ref, duk_ref, duv_ref, dug_ref, gn_ref, gbf_ref,
             rcar_ref):
        du_refs = (duq_ref, duk_ref, duv_ref, dug_ref)


        @pl.when(pl.program_id(0) == 0)
        def _():
            rcar_ref[...] = jnp.zeros_like(rcar_ref)
            gn_ref[...] = jnp.zeros_like(gn_ref)
            gbf_ref[...] = jnp.zeros_like(gbf_ref)

        rsum = _cumsum_rows(dcum_ref[...], reverse=True) + rcar_ref[0:1, :]
        df = rsum * _sigmoid(-f_ref[...])
        dfb = df.astype(BF16)
        dh = _dot_nt(dfb, wft_ref[...])
        for j, ref in enumerate((dq_ref, dk_ref, dv_ref, dg_ref)):
            du = _heads_from_padded(ref[...])
            du_refs[j][...] = du
            if j == 0:
                du = du * QK_SCALE
            dh = dh + _dot_nt(du, wt_ref[:, j * width:(j + 1) * width])
        dxn, dgn = _norm_bwd(x1_ref[...], g_ref[...], dh)
        dx1 = dx2_ref[...] + dxn
        dx1_ref[...] = dx1
        dx1b_ref[...] = dx1.astype(BF16)
        df_ref[...] = dfb
        gn_ref[...] += dgn
        gbf_ref[...] += jnp.sum(df, axis=0, keepdims=True)
        rcar_ref[...] = rsum[0:SUBLANES, :]

    rev = lambda i: (nt - 1 - i, 0)
    wide = pl.BlockSpec((ts, FOX_PAD), rev)
    return pl.pallas_call(
        body, name="fox_in_bwd", grid=(nt,),
        in_specs=[wide, wide, wide, wide,
                  _const_spec((D_MODEL, FOX_IN_COLS)),
                  _const_spec((D_MODEL, LANES)),
                  pl.BlockSpec((ts, LANES), rev),
                  pl.BlockSpec((ts, LANES), rev),
                  pl.BlockSpec((ts, D_MODEL), rev),
                  pl.BlockSpec((ts, D_MODEL), rev),
                  _const_spec((1, D_MODEL))],
        out_specs=[pl.BlockSpec((ts, D_MODEL), rev),
                   pl.BlockSpec((ts, D_MODEL), rev),
                   pl.BlockSpec((ts, LANES), rev)]
        + [pl.BlockSpec((ts, width), rev)] * 4
        + [pl.BlockSpec((1, D_MODEL), lambda i: (0, 0)),
           pl.BlockSpec((1, LANES), lambda i: (0, 0))],
        out_shape=[jax.ShapeDtypeStruct((s, D_MODEL), F32),
                   jax.ShapeDtypeStruct((s, D_MODEL), BF16),
                   jax.ShapeDtypeStruct((s, LANES), BF16)]
        + [jax.ShapeDtypeStruct((s, width), BF16)] * 4
        + [jax.ShapeDtypeStruct((1, D_MODEL), F32),
                   jax.ShapeDtypeStruct((1, LANES), F32)],
        scratch_shapes=[pltpu.VMEM((SUBLANES, LANES), F32)],
        compiler_params=_params(),
    )(dq, dk, dv, dg, wt, wft, dcum, f, x1, dx2, g1)


def _lru_core_bwd(dx1b, w_out, xb, gate, hs, cw, cb, wa, ba, wx, bx, a_param, wa_t, wx_t,
                  chip_sums, ts):
    s = xb.shape[0]
    nt = s // ts
    tpb = ts // SUBLANES
    n_ex = len(chip_sums)

    def body(*refs):
        (dx_ref, wo_ref, xb_ref, xbh_ref, gate_ref, hs_ref, hsh_ref, cw_ref, cb_ref, wa_ref,
         ba_ref, wx_ref, bx_ref, ap_ref, wat_ref, wxt_ref) = refs[:16]
        sum_refs = refs[16:16 + n_ex]
        du_ref, gwa_ref, gwx_ref, gvec_ref = refs[16 + n_ex:20 + n_ex]
        got_refs = refs[20 + n_ex:20 + 2 * n_ex]
        acar_ref, dhcar_ref, dxccar_ref = refs[20 + 2 * n_ex:23 + 2 * n_ex]
        start, finish = _chip_exchange_phases(sum_refs, got_refs, *refs[23 + 2 * n_ex:])
        step = pl.program_id(0)
        pl.when(step == 0)(start)

        @pl.when(step == 0)
        def _():
            acar_ref[...] = jnp.zeros_like(acar_ref)
            dhcar_ref[...] = jnp.zeros_like(dhcar_ref)
            dxccar_ref[...] = jnp.zeros_like(dxccar_ref)
            gwa_ref[...] = jnp.zeros_like(gwa_ref)
            gwx_ref[...] = jnp.zeros_like(gwx_ref)
            gvec_ref[...] = jnp.zeros_like(gvec_ref)

        first_tile = step == nt - 1
        halo_on = jnp.where(first_tile, 0.0, 1.0)
        prev8 = xbh_ref[...] * halo_on
        hprev_row = hsh_ref[SUBLANES - 1:SUBLANES, :] * halo_on

        xbv = xb_ref[...]
        taps = _conv_taps(xbv, prev8)
        cw_v = cw_ref[...]
        xc, xcb, r, i, sp, a, mult = _lru_pre(taps, cw_v, cb_ref[...], wa_ref, ba_ref[...],
                                              wx_ref, bx_ref[...], ap_ref[...])
        hs = hs_ref[...]
        gv = gate_ref[...]
        sg = _sigmoid(gv)
        dy = _dot_nt(dx_ref[...], wo_ref[...])
        dhs = dy * (gv * sg)
        dgate = dy * hs * (sg * (1.0 + gv * (1.0 - sg)))

        rows = lax.broadcasted_iota(jnp.int32, a.shape, 0)
        a_next = jnp.where(rows < ts - 1, pltpu.roll(a, ts - 1, 0), acar_ref[0:1, :])
        cum_a, dh_loc = _scan_rows(a_next, dhs, reverse=True)
        dh = cum_a * dhcar_ref[0:1, :] + dh_loc
        h_prev = jnp.where(rows >= 1, pltpu.roll(hs, 1, 0), hprev_row)

        da = dh * h_prev
        ixc = i * xc
        dmult = dh * ixc
        di = dh * mult * xc
        dxc = dh * mult * i
        dlog_a = da * a - dmult * (a * a) / mult
        dr = dlog_a * ((-LRU_C) * sp)
        dsp = jnp.sum(dlog_a * ((-LRU_C) * r), axis=0, keepdims=True)
        dra = dr * r * (1.0 - r)
        dia = di * i * (1.0 - i)
        drab = dra.astype(BF16)
        diab = dia.astype(BF16)
        back = []
        for n in range(LRU_BLOCKS):
            sl = slice(n * LRU_BLOCK_W, (n + 1) * LRU_BLOCK_W)
            gwa_ref[n] += _dot_tn(xcb[:, sl], drab[:, sl])
            gwx_ref[n] += _dot_tn(xcb[:, sl], diab[:, sl])
            back.append(_dot(drab[:, sl], wat_ref[n]) + _dot(diab[:, sl], wxt_ref[n]))
        dxc = dxc + jnp.concatenate(back, axis=1)

        nxt8 = dxccar_ref[...]
        rows8 = lax.broadcasted_iota(jnp.int32, nxt8.shape, 0)
        dxb = cw_v[3:4] * dxc
        for j in range(1, CONV_WIDTH):
            rj = pltpu.roll(dxc, ts - j, 0)
            pj = pltpu.roll(nxt8, SUBLANES - j, 0)
            tail = jnp.where(rows8 >= SUBLANES - j, pj, rj[ts - SUBLANES:])
            dxb = dxb + cw_v[3 - j:4 - j] * jnp.concatenate([rj[:ts - SUBLANES], tail], axis=0)

        du_ref[:, :LRU_WIDTH] = dxb.astype(BF16)
        du_ref[:, LRU_WIDTH:] = dgate.astype(BF16)

        z = -ap_ref[...]
        gvec = [jnp.sum(dxc * taps[3 - k], axis=0, keepdims=True) for k in range(CONV_WIDTH)]
        gvec.append(jnp.sum(dxc, axis=0, keepdims=True))
        gvec.append(jnp.sum(dra, axis=0, keepdims=True))
        gvec.append(jnp.sum(dia, axis=0, keepdims=True))
        gvec.append(-dsp * _sigmoid(z))
        gvec_ref[...] += jnp.concatenate(gvec, axis=0)

        acar_ref[...] = a[0:SUBLANES, :]
        dhcar_ref[...] = dh[0:SUBLANES, :]
        dxccar_ref[...] = dxc[0:SUBLANES, :]
        pl.when(step == nt - 1)(finish)

    rev = lambda i: (nt - 1 - i, 0)
    halo = lambda i: (jnp.maximum((nt - 1 - i) * tpb - 1, 0), 0)
    tile = pl.BlockSpec((ts, LRU_WIDTH), rev)
    halo_spec = pl.BlockSpec((SUBLANES, LRU_WIDTH), halo)
    vec = _const_spec((1, LRU_WIDTH))
    blk = _const_spec((LRU_BLOCKS, LRU_BLOCK_W, LRU_BLOCK_W))
    acc_blk = pl.BlockSpec((LRU_BLOCKS, LRU_BLOCK_W, LRU_BLOCK_W), lambda i: (0, 0, 0))
    hbm = pl.BlockSpec(memory_space=pl.ANY)
    res = pl.pallas_call(
        body, name="lru_core_bwd", grid=(nt,),
        in_specs=[pl.BlockSpec((ts, D_MODEL), rev),
                  _const_spec((LRU_WIDTH, D_MODEL)),
                  tile, halo_spec, tile, tile, halo_spec,
                  _const_spec((CONV_WIDTH, LRU_WIDTH)), vec, blk, vec, blk, vec, vec, blk, blk]
        + [hbm] * n_ex,
        out_specs=[pl.BlockSpec((ts, 2 * LRU_WIDTH), rev), acc_blk, acc_blk,
                   pl.BlockSpec((SUBLANES, LRU_WIDTH), lambda i: (0, 0))] + [hbm] * n_ex,
        out_shape=[jax.ShapeDtypeStruct((s, 2 * LRU_WIDTH), BF16),
                   jax.ShapeDtypeStruct((LRU_BLOCKS, LRU_BLOCK_W, LRU_BLOCK_W), F32),
                   jax.ShapeDtypeStruct((LRU_BLOCKS, LRU_BLOCK_W, LRU_BLOCK_W), F32),
                   jax.ShapeDtypeStruct((SUBLANES, LRU_WIDTH), F32)]
        + [jax.ShapeDtypeStruct(a.shape, a.dtype) for a in chip_sums],
        scratch_shapes=[pltpu.VMEM((SUBLANES, LRU_WIDTH), F32),
                        pltpu.VMEM((SUBLANES, LRU_WIDTH), F32),
                        pltpu.VMEM((SUBLANES, LRU_WIDTH), F32)] + _chip_exchange_sems(n_ex),
        compiler_params=_params(),
    )(dx1b, w_out, xb, xb, gate, hs, hs, cw, cb, wa, ba, wx, bx, a_param, wa_t, wx_t, *chip_sums)
    return res[0], res[1], res[2], res[3], res[4:]


def _lru_in_bwd(du, w_in, x, dx1, g0, chip_sums, ts):
    s = x.shape[0]
    nt = s // ts
    n = len(chip_sums)

    def body(*refs):
        du_ref, w_ref, x_ref, dx1_ref, g_ref = refs[:5]
        sum_refs = refs[5:5 + n]
        gx_ref, gn_ref = refs[5 + n:7 + n]
        got_refs = refs[7 + n:7 + 2 * n]
        wfull_ref = refs[7 + 2 * n]
        start, finish = _chip_exchange_phases(sum_refs, got_refs, *refs[8 + 2 * n:])
        step = pl.program_id(0)
        pl.when(step == 0)(start)

        @pl.when(step == 0)
        def _():
            gn_ref[...] = jnp.zeros_like(gn_ref)
            for j in range(N_DEV):
                wfull_ref[:, j * LRU_IN_SHARD:(j + 1) * LRU_IN_SHARD] = w_ref[j]

        dh = _dot_nt(du_ref[...], wfull_ref[...])
        dxn, dgn = _norm_bwd(x_ref[...], g_ref[...], dh)
        gx_ref[...] = dx1_ref[...] + dxn
        gn_ref[...] += dgn
        pl.when(step == nt - 1)(finish)

    tile = pl.BlockSpec((ts, D_MODEL), lambda i: (i, 0))
    hbm = pl.BlockSpec(memory_space=pl.ANY)
    res = pl.pallas_call(
        body, name="lru_in_bwd", grid=(nt,),
        in_specs=[pl.BlockSpec((ts, 2 * LRU_WIDTH), lambda i: (i, 0)),
                  _const_spec((N_DEV, D_MODEL, LRU_IN_SHARD)), tile, tile,
                  _const_spec((1, D_MODEL))] + [hbm] * n,
        out_specs=[tile, pl.BlockSpec((1, D_MODEL), lambda i: (0, 0))] + [hbm] * n,
        out_shape=[jax.ShapeDtypeStruct((s, D_MODEL), F32),
                   jax.ShapeDtypeStruct((1, D_MODEL), F32)]
        + [jax.ShapeDtypeStruct(a.shape, a.dtype) for a in chip_sums],
        scratch_shapes=[pltpu.VMEM((D_MODEL, 2 * LRU_WIDTH), BF16)] + _chip_exchange_sems(n),
        compiler_params=_params(),
    )(du, w_in, x, dx1, g0, *chip_sums)
    return res[0], res[1], res[2:]


def _weight_grad(a, b, ts, name, scale=1.0, col_shards=1):
    s, ka = a.shape
    nb = b.shape[1]
    nt = s // ts
    per = nb // col_shards

    def body(a_ref, b_ref, o_ref):
        @pl.when(pl.program_id(0) == 0)
        def _():
            o_ref[...] = jnp.zeros_like(o_ref)

        if col_shards == 1:
            o_ref[...] += _dot_tn(a_ref[...], b_ref[...])
        else:
            acc = _dot_tn(a_ref[...], b_ref[...])
            for j in range(col_shards):
                o_ref[j] += acc[:, j * per:(j + 1) * per]
        if scale != 1.0:
            @pl.when(pl.program_id(0) == nt - 1)
            def _():
                o_ref[...] = o_ref[...] * scale

    out_dims = (ka, nb) if col_shards == 1 else (col_shards, ka, per)
    return pl.pallas_call(
        body, name=name, grid=(nt,),
        in_specs=[pl.BlockSpec((ts, ka), lambda i: (i, 0)),
                  pl.BlockSpec((ts, nb), lambda i: (i, 0))],
        out_specs=pl.BlockSpec(out_dims, lambda i: (0,) * len(out_dims)),
        out_shape=jax.ShapeDtypeStruct(out_dims, F32),
        compiler_params=_params(),
    )(a, b)


def _sum_parts(gp_ref):
    g = gp_ref[0].astype(F32)
    for k in range(1, gp_ref.shape[0]):
        g = g + gp_ref[k].astype(F32)
    return g


def _adamw(g_parts, w, m, v, tr, name):
    nparts, rows, cols = g_parts.shape

    def body(gp_ref, w_ref, m_ref, v_ref, g_ref, d_ref, mo_ref, vo_ref):
        g = _sum_parts(gp_ref)
        m2 = ADAM_B1 * m_ref[...] + (1.0 - ADAM_B1) * g
        v2 = ADAM_B2 * v_ref[...] + (1.0 - ADAM_B2) * (g * g)
        m_hat = m2 / (1.0 - ADAM_B1 ** ADAM_STEP)
        v_hat = v2 / (1.0 - ADAM_B2 ** ADAM_STEP)
        g_ref[...] = g
        d_ref[...] = (-ADAM_LR) * (m_hat / (jnp.sqrt(v_hat) + ADAM_EPS) + ADAM_WD * w_ref[...])
        mo_ref[...] = m2
        vo_ref[...] = v2

    tile = pl.BlockSpec((tr, cols), lambda i: (i, 0))
    out = jax.ShapeDtypeStruct((rows, cols), F32)
    return pl.pallas_call(
        body, name=name, grid=(rows // tr,),
        in_specs=[pl.BlockSpec((nparts, tr, cols), lambda i: (0, i, 0)), tile, tile, tile],
        out_specs=[tile, tile, tile, tile],
        out_shape=[out, out, out, out],
        compiler_params=_params(),
    )(g_parts, w, m, v)


def _reduce_parts(g_parts, name):
    _, rows, cols = g_parts.shape

    def body(gp_ref, g_ref):
        g_ref[...] = _sum_parts(gp_ref)

    return pl.pallas_call(
        body, name=name,
        out_shape=jax.ShapeDtypeStruct((rows, cols), F32),
        compiler_params=pltpu.CompilerParams(vmem_limit_bytes=VMEM_LIMIT_BYTES),
    )(g_parts)


def _mesh_pos():
    ix, iy, ic = lax.axis_index("x"), lax.axis_index("y"), lax.axis_index("c")
    return ix, iy, ic


def _peer(ix, iy, ic, mask):
    px = 1 - ix if mask & 4 else ix
    py = 1 - iy if mask & 2 else iy
    pc = 1 - ic if mask & 1 else ic
    return (px, py, pc), 4 * px + 2 * py + pc


def _exchange(arrays, scatter, name):
    n = len(arrays)

    def body(*refs):
        x_refs, o_refs = refs[:n], refs[n:2 * n]
        send_sems, recv_sems, local_sems = refs[2 * n:]
        ix, iy, ic = _mesh_pos()
        me = 4 * ix + 2 * iy + ic

        def src(a, dest):
            return x_refs[a].at[dest] if scatter else x_refs[a]

        local = [pltpu.make_async_copy(src(a, me), o_refs[a].at[me], local_sems.at[a])
                 for a in range(n)]
        for cp in local:
            cp.start()
        sends = []
        for mask in range(1, N_DEV):
            peer, pidx = _peer(ix, iy, ic, mask)
            for a in range(n):
                cp = pltpu.make_async_remote_copy(
                    src_ref=src(a, pidx), dst_ref=o_refs[a].at[me],
                    send_sem=send_sems.at[a, mask - 1], recv_sem=recv_sems.at[a, mask - 1],
                    device_id=peer, device_id_type=pl.DeviceIdType.MESH)
                cp.start()
                sends.append(cp)
        for mask in range(1, N_DEV):
            peer, pidx = _peer(ix, iy, ic, mask)
            for a in range(n):
                pltpu.make_async_remote_copy(
                    src_ref=src(a, me), dst_ref=o_refs[a].at[pidx],
                    send_sem=send_sems.at[a, mask - 1], recv_sem=recv_sems.at[a, mask - 1],
                    device_id=peer, device_id_type=pl.DeviceIdType.MESH).wait_recv()
        for cp in sends:
            cp.wait_send()
        for cp in local:
            cp.wait()

    out_shape = [jax.ShapeDtypeStruct(x.shape if scatter else (N_DEV,) + x.shape, x.dtype)
                 for x in arrays]
    return pl.pallas_call(
        body, name=name,
        in_specs=[pl.BlockSpec(memory_space=pl.ANY)] * n,
        out_specs=[pl.BlockSpec(memory_space=pl.ANY)] * n,
        out_shape=out_shape,
        scratch_shapes=[pltpu.SemaphoreType.DMA((n, N_DEV - 1)),
                        pltpu.SemaphoreType.DMA((n, N_DEV - 1)),
                        pltpu.SemaphoreType.DMA((n,))],
    )(*arrays)


def _gather_two_level(arrays, name):
    n = len(arrays)

    def body(*refs):
        start, forward, finish = _gather_phases(refs[:n], refs[n:2 * n], *refs[2 * n:])
        start()
        forward()
        finish()

    return pl.pallas_call(
        body, name=name,
        in_specs=[pl.BlockSpec(memory_space=pl.ANY)] * n,
        out_specs=[pl.BlockSpec(memory_space=pl.ANY)] * n,
        out_shape=[jax.ShapeDtypeStruct((N_DEV,) + x.shape, x.dtype) for x in arrays],
        scratch_shapes=_gather_sems(n),
    )(*arrays)


def _gather_sems(n):
    return [pltpu.SemaphoreType.DMA((n, N_DEV - 1)), pltpu.SemaphoreType.DMA((n, N_DEV - 1)),
            pltpu.SemaphoreType.DMA((n,))]


def _gather_phases(x_refs, o_refs, send_sems, recv_sems, local_sems):
    n = len(x_refs)
    ix, iy, ic = _mesh_pos()
    me, sibling = (ix, iy, ic), (ix, iy, 1 - ic)
    chips = [(1 - ix, iy), (ix, 1 - iy), (1 - ix, 1 - iy)]

    def idx(px, py, pc):
        return 4 * px + 2 * py + pc

    def copy(a, k, block, to, src=None):
        dst = o_refs[a].at[idx(*block)]
        return pltpu.make_async_remote_copy(
            src_ref=dst if src is None else src, dst_ref=dst,
            send_sem=send_sems.at[a, k], recv_sem=recv_sems.at[a, k],
            device_id=to, device_id_type=pl.DeviceIdType.MESH)

    def local():
        return [pltpu.make_async_copy(x_refs[a], o_refs[a].at[idx(*me)], local_sems.at[a])
                for a in range(n)]

    def first():
        out = []
        for a in range(n):
            out.append(copy(a, 0, me, sibling, src=x_refs[a]))
            out += [copy(a, 1 + j, me, (*chip, ic), src=x_refs[a])
                    for j, chip in enumerate(chips)]
        return out

    def passed():
        return [copy(a, 4 + j, (*chip, ic), sibling)
                for j, chip in enumerate(chips) for a in range(n)]

    def start():
        for cp in local() + first():
            cp.start()

    def forward():
        for j, chip in enumerate(chips):
            for a in range(n):
                copy(a, 1 + j, (*chip, ic), me).wait_recv()
                copy(a, 4 + j, (*chip, ic), sibling).start()

    def finish():
        for a in range(n):
            copy(a, 0, sibling, me).wait_recv()
            for j, chip in enumerate(chips):
                copy(a, 4 + j, (*chip, 1 - ic), me).wait_recv()
        for cp in first() + passed():
            cp.wait_send()
        for cp in local():
            cp.wait()

    return start, forward, finish


def _swap_sibling(arrays, name):
    n = len(arrays)
    n_chips = N_DEV // 2

    def body(*refs):
        x_refs, got_refs = refs[:n], refs[n:2 * n]
        send_sems, recv_sems = refs[2 * n:]
        ix, iy, ic = _mesh_pos()
        sibling = (ix, iy, 1 - ic)
        sends = []
        for a in range(n):
            for q in range(n_chips):
                cp = pltpu.make_async_remote_copy(
                    src_ref=x_refs[a].at[q, 1 - ic], dst_ref=got_refs[a].at[q],
                    send_sem=send_sems.at[a, q], recv_sem=recv_sems.at[a, q],
                    device_id=sibling, device_id_type=pl.DeviceIdType.MESH)
                cp.start()
                sends.append(cp)
        for cp in sends:
            cp.wait()

    return pl.pallas_call(
        body, name=name,
        in_specs=[pl.BlockSpec(memory_space=pl.ANY)] * n,
        out_specs=[pl.BlockSpec(memory_space=pl.ANY)] * n,
        out_shape=[jax.ShapeDtypeStruct((n_chips,) + x.shape[2:], x.dtype) for x in arrays],
        scratch_shapes=[pltpu.SemaphoreType.DMA((n, n_chips)),
                        pltpu.SemaphoreType.DMA((n, n_chips))],
    )(*arrays)


def _exchange_chips(arrays, name):
    n = len(arrays)

    def body(*refs):
        start, finish = _chip_exchange_phases(refs[:n], refs[n:2 * n], *refs[2 * n:])
        start()
        finish()

    return pl.pallas_call(
        body, name=name,
        in_specs=[pl.BlockSpec(memory_space=pl.ANY)] * n,
        out_specs=[pl.BlockSpec(memory_space=pl.ANY)] * n,
        out_shape=[jax.ShapeDtypeStruct(x.shape, x.dtype) for x in arrays],
        scratch_shapes=_chip_exchange_sems(n),
    )(*arrays)


def _chip_exchange_sems(n):
    n_chips = N_DEV // 2
    return [pltpu.SemaphoreType.DMA((n, n_chips - 1)), pltpu.SemaphoreType.DMA((n, n_chips - 1)),
            pltpu.SemaphoreType.DMA((n,))]


def _chip_exchange_phases(x_refs, o_refs, send_sems, recv_sems, local_sems):
    n = len(x_refs)
    n_chips = N_DEV // 2
    ix, iy, ic = _mesh_pos()
    my_chip = 2 * ix + iy

    def peers():
        for mask in range(1, n_chips):
            px = 1 - ix if mask & 2 else ix
            py = 1 - iy if mask & 1 else iy
            yield mask, (px, py, ic), 2 * px + py

    def local():
        return [pltpu.make_async_copy(x_refs[a].at[my_chip], o_refs[a].at[my_chip],
                                      local_sems.at[a]) for a in range(n)]

    def sends():
        return [pltpu.make_async_remote_copy(
            src_ref=x_refs[a].at[chip], dst_ref=o_refs[a].at[my_chip],
            send_sem=send_sems.at[a, mask - 1], recv_sem=recv_sems.at[a, mask - 1],
            device_id=peer, device_id_type=pl.DeviceIdType.MESH)
            for mask, peer, chip in peers() for a in range(n)]

    def start():
        for cp in local() + sends():
            cp.start()

    def finish():
        for mask, peer, chip in peers():
            for a in range(n):
                pltpu.make_async_remote_copy(
                    src_ref=x_refs[a].at[my_chip], dst_ref=o_refs[a].at[chip],
                    send_sem=send_sems.at[a, mask - 1], recv_sem=recv_sems.at[a, mask - 1],
                    device_id=peer, device_id_type=pl.DeviceIdType.MESH).wait_recv()
        for cp in sends():
            cp.wait_send()
        for cp in local():
            cp.wait()

    return start, finish


def _pair_sum(core, x, got, name):
    nq, rows, cols = got.shape

    def body(c_ref, x_ref, g_ref, o_ref):
        o_ref[...] = (x_ref[...] + g_ref[...]).astype(BF16)

    blk = pl.BlockSpec((None, rows, cols), lambda q, c: (q, 0, 0))
    return pl.pallas_call(
        body, name=name,
        grid_spec=pltpu.PrefetchScalarGridSpec(
            num_scalar_prefetch=1, grid=(nq,),
            in_specs=[pl.BlockSpec((None, None, rows, cols), lambda q, c: (q, c[0], 0, 0)), blk],
            out_specs=blk),
        out_shape=jax.ShapeDtypeStruct(got.shape, BF16),
        compiler_params=_params(),
    )(core, x, got)


def _selectors():
    r = lax.broadcasted_iota(jnp.int32, (LANES, FOX_PAD), 0)
    c = lax.broadcasted_iota(jnp.int32, (LANES, FOX_PAD), 1)
    part, head_r = r // HEADS, r % HEADS
    head_c, lane_c = c // HEAD_PAD, c % HEAD_PAD
    same = (head_r == head_c) & (part < 3)
    sel_q = jnp.where(same & (lane_c == LANE_RB + part), 1.0, 0.0)
    sel_k = jnp.where(same & (lane_c == LANE_CK + part), -1.0, 0.0)
    sel = jnp.stack([sel_q, sel_k, jnp.zeros_like(sel_q)]).astype(BF16)
    lane = lax.broadcasted_iota(jnp.int32, (1, FOX_PAD), 1) % HEAD_PAD
    ones_q = jnp.where((lane >= LANE_CK) & (lane < LANE_CK + 3), 1.0, 0.0)
    ones_k = jnp.where(((lane >= LANE_RB) & (lane < LANE_RB + 3))
                       | ((lane >= LANE_LSE) & (lane < LANE_LSE + 3)), 1.0, 0.0)
    ones_v = jnp.where((lane >= LANE_ONE_V) & (lane < LANE_ONE_V + 2), 1.0, 0.0)
    bias = jnp.stack([ones_q, ones_k, ones_v]).astype(F32)
    return sel, bias


def _chip_sums(names, send):
    send = [a.reshape((N_DEV // 2, 2) + a.shape[1:]) for a in send]
    got = _swap_sibling(send, "swap_" + names[0])
    core = lax.axis_index("c").astype(jnp.int32).reshape(1)
    return [_pair_sum(core, a, b, "pair_sum_" + n) for n, a, b in zip(names, send, got)]


def _local_step(x, target, norm_g, final_g, w_in8, conv_w, conv_b, wa, ba, wx, bx, a_param,
                w_out_b, fox_in_shard, b_f, fox_out_shard, blk=512, ts=256):
    g0, g1 = norm_g[0:1], norm_g[1:2]
    gf = final_g.reshape(1, D_MODEL)
    wa_b, wx_b = wa.astype(BF16), wx.astype(BF16)
    sel, bias = _selectors()

    tm = min(2 * ts, x.shape[0])
    xb, gate1, h0, (fox_in8, fox_out8) = _lru_in_fwd(x, g0, w_in8, [fox_in_shard, fox_out_shard],
                                                     tm)
    fox_w_in = jnp.transpose(fox_in8, (1, 0, 2)).reshape(D_MODEL, FOX_IN_COLS)
    width = HEADS * HEAD_DIM
    wf_b = jnp.pad(fox_w_in[:, 4 * width:], ((0, 0), (0, LANES - HEADS)))
    bf_pad = jnp.pad(b_f, ((0, 0), (0, LANES - HEADS)))
    fo_b = fox_out8.reshape(width, D_MODEL)
    y1, hs = _lru_core_fwd(xb, gate1, conv_w, conv_b, wa_b, ba, wx_b, bx, a_param, ts)
    x1, h1, f, cparts = _fox_pre_fwd(x, y1, w_out_b, g1, wf_b, bf_pad, ts)
    qkv = _fox_proj_fwd(h1, cparts, fox_w_in, 0, 3, sel, bias, BF16, tm, "fox_proj_qkv")
    gate2 = _fox_proj_fwd(h1, None, fox_w_in, 3, 1, None, None, F32, tm, "fox_proj_gate")[0]
    o, qb = _attn_fwd(qkv, blk, hps=4)
    dx2, dx2b, y2, loss_acc, g_final = _fox_out_loss(o, gate2, fo_b, x1, target, gf, tm)

    do, dgate2 = _fox_out_bwd(dx2b, fo_b, o, gate2, ts)
    dq, dk, dv, dcum = _attn_bwd(qb, qkv, do, blk)
    dx1, dx1b, df, du_q, du_k, du_v, du_g, g_norm1, g_bf = _fox_in_bwd(
        dq, dk, dv, dgate2, fox_w_in, wf_b, dcum, f, x1, dx2, g1, ts)
    tw = min(1024, x.shape[0])
    g_q = _weight_grad(h1, du_q, tw, "grad_fox_wq", scale=QK_SCALE)
    g_k = _weight_grad(h1, du_k, tw, "grad_fox_wk")
    g_v = _weight_grad(h1, du_v, tw, "grad_fox_wv")
    g_g = _weight_grad(h1, du_g, tw, "grad_fox_wg")
    g_f = _weight_grad(h1, df, tw, "grad_fox_wf")
    g_fox_w_in = jnp.concatenate([g_q, g_k, g_v, g_g, g_f[:, :HEADS]], axis=1)
    g_fox_w_in = jnp.transpose(g_fox_w_in.reshape(D_MODEL, N_DEV, FOX_IN_SHARD), (1, 0, 2))
    g_fox_w_out = _weight_grad(y2, dx2b, tw, "grad_fox_w_out")
    fox_sums = _chip_sums(("fox_w_in", "fox_w_out"),
                          [g_fox_w_in, g_fox_w_out.reshape(N_DEV, -1, D_MODEL)])

    du, g_wa, g_wx, g_vec, (r_fox_in, r_fox_out) = _lru_core_bwd(
        dx1b, w_out_b, xb, gate1, hs, conv_w, conv_b, wa_b, ba, wx_b, bx, a_param,
        jnp.transpose(wa_b, (0, 2, 1)), jnp.transpose(wx_b, (0, 2, 1)), fox_sums, ts)
    g_lru_w_in = _weight_grad(h0, du, tw, "grad_lru_w_in", col_shards=N_DEV)
    g_lru_w_out = _weight_grad(y1, dx1b, tw, "grad_lru_w_out")
    conv_send = jnp.transpose(g_vec[0:CONV_WIDTH].reshape(CONV_WIDTH, N_DEV, -1), (1, 0, 2))
    lru_sums = _chip_sums(("lru_w_in", "lru_conv_w", "lru_w_out"),
                          [g_lru_w_in, conv_send, g_lru_w_out.reshape(N_DEV, -1, D_MODEL)])
    grad_x, g_norm0, (r_w_in, r_conv, r_w_out) = _lru_in_bwd(du, w_in8, x, dx1, g0, lru_sums, tm)

    small = dict(
        norm_g=jnp.concatenate([g_norm0, g_norm1], axis=0), final_g=g_final[0],
        lru_conv_b=g_vec[4:5], lru_wa=g_wa, lru_ba=g_vec[5:6], lru_wx=g_wx, lru_bx=g_vec[6:7],
        lru_a_param=g_vec[7:8], fox_b_f=g_bf[:, :HEADS])
    received = dict(lru_w_in=r_w_in, lru_conv_w=r_conv, lru_w_out=r_w_out, fox_w_in=r_fox_in,
                    fox_w_out=r_fox_out)
    return loss_acc[0, 0], grad_x, small, received


SMALL =("norm_g", "final_g", "lru_conv_b", "lru_wa", "lru_ba", "lru_wx", "lru_bx", "lru_a_param",
         "fox_b_f")
ALL_WEIGHTS = ("norm_g", "final_g", "lru_w_in", "lru_conv_w", "lru_conv_b", "lru_wa", "lru_ba",
               "lru_wx", "lru_bx", "lru_a_param", "lru_w_out", "fox_w_in", "fox_b_f", "fox_w_out")


def _pack_small(d):
    rows = []
    for n in SMALL:
        a = d[n].reshape(-1)
        if a.shape[0] % LANES:
            a = jnp.pad(a, (0, LANES - a.shape[0] % LANES))
        rows.append(a.reshape(-1, LANES))
    packed = jnp.concatenate(rows, axis=0)
    return jnp.pad(packed, ((0, N_DEV * SMALL_CHUNK_ROWS - packed.shape[0]), (0, 0)))


def _unpack_small(packed, like):
    out, off = {}, 0
    for n, nrows in zip(SMALL, SMALL_ROWS):
        size = like[n].size
        out[n] = packed[off:off + nrows].reshape(-1)[:size].reshape(like[n].shape)
        off += nrows
    return out


def kernel(x, norm_g, final_g, lru_w_in, lru_conv_w, lru_conv_b, lru_wa, lru_ba, lru_wx, lru_bx, lru_a_param, lru_w_out, fox_w_in, fox_b_f, fox_w_out, loss_target, m_norm_g, m_final_g, m_lru_w_in, m_lru_conv_w, m_lru_conv_b, m_lru_wa, m_lru_ba, m_lru_wx, m_lru_bx, m_lru_a_param, m_lru_w_out, m_fox_w_in, m_fox_b_f, m_fox_w_out, v_norm_g, v_final_g, v_lru_w_in, v_lru_conv_w, v_lru_conv_b, v_lru_wa, v_lru_ba, v_lru_wx, v_lru_bx, v_lru_a_param, v_lru_w_out, v_fox_w_in, v_fox_b_f, v_fox_w_out):
    w_loc = dict(norm_g=norm_g, final_g=final_g, lru_w_in=lru_w_in, lru_conv_w=lru_conv_w,
                 lru_conv_b=lru_conv_b, lru_wa=lru_wa, lru_ba=lru_ba, lru_wx=lru_wx, lru_bx=lru_bx,
                 lru_a_param=lru_a_param, lru_w_out=lru_w_out, fox_w_in=fox_w_in, fox_b_f=fox_b_f,
                 fox_w_out=fox_w_out)
    m_loc = dict(norm_g=m_norm_g, final_g=m_final_g, lru_w_in=m_lru_w_in, lru_conv_w=m_lru_conv_w,
                 lru_conv_b=m_lru_conv_b, lru_wa=m_lru_wa, lru_ba=m_lru_ba, lru_wx=m_lru_wx,
                 lru_bx=m_lru_bx, lru_a_param=m_lru_a_param, lru_w_out=m_lru_w_out,
                 fox_w_in=m_fox_w_in, fox_b_f=m_fox_b_f, fox_w_out=m_fox_w_out)
    v_loc = dict(norm_g=v_norm_g, final_g=v_final_g, lru_w_in=v_lru_w_in, lru_conv_w=v_lru_conv_w,
                 lru_conv_b=v_lru_conv_b, lru_wa=v_lru_wa, lru_ba=v_lru_ba, lru_wx=v_lru_wx,
                 lru_bx=v_lru_bx, lru_a_param=v_lru_a_param, lru_w_out=v_lru_w_out,
                 fox_w_in=v_fox_w_in, fox_b_f=v_fox_b_f, fox_w_out=v_fox_w_out)

    w_in8, conv8, w_out8 = _gather_two_level(
        [lru_w_in[0].astype(BF16), lru_conv_w[0], lru_w_out[0].astype(BF16)], "gather_weights")
    conv_full = jnp.transpose(conv8, (1, 0, 2)).reshape(CONV_WIDTH, LRU_WIDTH)

    loss, grad_x, small_grads, received = _local_step(
        x[0], loss_target[0], norm_g, final_g, w_in8, conv_full, lru_conv_b, lru_wa[0], lru_ba,
        lru_wx[0], lru_bx, lru_a_param, w_out8.reshape(LRU_WIDTH, D_MODEL),
        fox_w_in[0].astype(BF16), fox_b_f, fox_w_out[0].astype(BF16))

    out = {}
    for n, tr in (("lru_w_in", 256), ("lru_conv_w", CONV_WIDTH), ("lru_w_out", 96),
                  ("fox_w_in", 128), ("fox_w_out", 64)):
        res = _adamw(received[n], w_loc[n][0], m_loc[n][0], v_loc[n][0], tr, "adamw_" + n)
        out[n] = [a[None] for a in res]

    small_sums = _chip_sums(
        ("small",), [_pack_small(small_grads).reshape(N_DEV, SMALL_CHUNK_ROWS, LANES)])
    r_small, = _exchange_chips(small_sums, "scatter_small_grads")

    g_chunk = _reduce_parts(r_small, "reduce_small_grads")
    g_small, = _exchange([g_chunk], False, "gather_small_grads")
    g_small = g_small.reshape(1, N_DEV * SMALL_CHUNK_ROWS, LANES)
    res = _adamw(g_small, _pack_small(w_loc), _pack_small(m_loc), _pack_small(v_loc),
                 N_DEV * SMALL_CHUNK_ROWS, "adamw_replicated")
    small_out = [_unpack_small(a, w_loc) for a in res]
    for n in SMALL:
        out[n] = [d[n] for d in small_out]

    loss = lax.psum(loss, ("x", "y", "c"))
    return (loss, grad_x[None], *[out[n][0] for n in ALL_WEIGHTS], *[out[n][1] for n in ALL_WEIGHTS],
            *[out[n][2] for n in ALL_WEIGHTS], *[out[n][3] for n in ALL_WEIGHTS])
```

```python
import functools

import jax
import jax.numpy as jnp
from jax import lax
from jax.experimental import pallas as pl
from jax.experimental.pallas import tpu as pltpu

F32 = jnp.float32
BF16 = jnp.bfloat16

D_MODEL = 1024
LRU_WIDTH = 1536
LRU_BLOCKS = 12
LRU_BLOCK_W = 128
CONV_WIDTH = 4
LRU_C = 8.0
HEADS = 16
HEAD_DIM = 64
HEAD_PAD = 128
FOX_PAD = HEADS * HEAD_PAD
HEADS_PER_STEP = 2
QK_SCALE = 1.0 / HEAD_DIM ** 0.5
EPS = 1e-6
NEG_BIG = -1e30
N_DEV = 8

ADAM_LR = 0.001
ADAM_B1 = 0.9
ADAM_B2 = 0.999
ADAM_EPS = 1e-08
ADAM_WD = 0.01
ADAM_STEP = 10

LANE_RB = 64
LANE_CK = 67
LANE_LSE = 70
LANE_ONE_V = 64

VMEM_LIMIT_BYTES = 56 * 1024 * 1024
LANES = 128
SUBLANES = 8

LRU_IN_SHARD = 2 * LRU_WIDTH // N_DEV
FOX_IN_COLS = 4 * HEADS * HEAD_DIM + HEADS
FOX_IN_SHARD = FOX_IN_COLS // N_DEV

SMALL_ROWS = (16, 8, 12, 1536, 12, 1536, 12, 12, 1)
SMALL_CHUNK_ROWS = 400
assert sum(SMALL_ROWS) <= N_DEV * SMALL_CHUNK_ROWS


def _params(n_grid_axes=1):
    return pltpu.CompilerParams(
        dimension_semantics=("arbitrary",) * n_grid_axes,
        vmem_limit_bytes=VMEM_LIMIT_BYTES)


def _const_spec(shape):
    nd = len(shape)
    return pl.BlockSpec(shape, lambda *_: (0,) * nd, pipeline_mode=pl.Buffered(1))


def _shift_down(x, k, fill):
    rows = lax.broadcasted_iota(jnp.int32, x.shape, 0)
    return jnp.where(rows >= k, pltpu.roll(x, k, 0), fill)


def _shift_up(x, k, fill):
    n = x.shape[0]
    rows = lax.broadcasted_iota(jnp.int32, x.shape, 0)
    return jnp.where(rows < n - k, pltpu.roll(x, n - k, 0), fill)


def _scan_rows(a, b, reverse=False):
    n = a.shape[0]
    shift = _shift_up if reverse else _shift_down
    k = 1
    while k < n:
        b = a * shift(b, k, 0.0) + b
        a = a * shift(a, k, 1.0)
        k *= 2
    return a, b


def _cumsum_rows(x, reverse=False):
    n = x.shape[0]
    shift = _shift_up if reverse else _shift_down
    k = 1
    while k < n:
        x = x + shift(x, k, 0.0)
        k *= 2
    return x


def _rstd(x):
    return lax.rsqrt(jnp.mean(x * x, axis=-1, keepdims=True) + EPS)


def _norm_bwd(x, g, dh):
    rstd = _rstd(x)
    xhat = x * rstd
    dg = jnp.sum(dh * xhat, axis=0, keepdims=True)
    dxh = dh * g
    dx = rstd * (dxh - xhat * jnp.mean(dxh * xhat, axis=-1, keepdims=True))
    return dx, dg


def _split3(x):
    hi = x.astype(BF16)
    r1 = x - hi.astype(F32)
    mid = r1.astype(BF16)
    lo = (r1 - mid.astype(F32)).astype(BF16)
    return hi, mid, lo


def _sigmoid(x):
    return jax.nn.sigmoid(x)


def _dot(a, b):
    return jnp.dot(a, b, preferred_element_type=F32)


def _dot_nt(a, b):
    return lax.dot_general(a, b, (((1,), (1,)), ((), ())), preferred_element_type=F32)


def _dot_tn(a, b):
    return lax.dot_general(a, b, (((0,), (0,)), ((), ())), preferred_element_type=F32)


def _heads_to_padded(u):
    n = u.shape[0]
    low = lax.broadcasted_iota(jnp.int32, (n, LANES), 1) < HEAD_DIM
    zero = jnp.zeros((n, LANES), u.dtype)
    cols = []
    for p in range(HEADS // 2):
        pair = u[:, p * LANES:(p + 1) * LANES]
        cols.append(jnp.where(low, pair, zero))
        cols.append(jnp.where(low, pltpu.roll(pair, HEAD_DIM, 1), zero))
    return jnp.concatenate(cols, axis=1)


def _heads_from_padded(x):
    n = x.shape[0]
    low = lax.broadcasted_iota(jnp.int32, (n, LANES), 1) < HEAD_DIM
    cols = []
    for p in range(HEADS // 2):
        even = x[:, (2 * p) * HEAD_PAD:(2 * p + 1) * HEAD_PAD]
        odd = x[:, (2 * p + 1) * HEAD_PAD:(2 * p + 2) * HEAD_PAD]
        cols.append(jnp.where(low, even, pltpu.roll(odd, HEAD_DIM, 1)))
    return jnp.concatenate(cols, axis=1)


def _conv_taps(xb, prev8):
    rows8 = lax.broadcasted_iota(jnp.int32, prev8.shape, 0)
    taps = [xb]
    for j in range(1, CONV_WIDTH):
        r = pltpu.roll(xb, j, 0)
        p = pltpu.roll(prev8, j, 0)
        head = jnp.where(rows8 < j, p, r[0:SUBLANES])
        taps.append(jnp.concatenate([head, r[SUBLANES:]], axis=0))
    return taps


def _lru_pre(taps, cw, cb, wa_ref, ba, wx_ref, bx, a_param):
    xc = cb + cw[3:4] * taps[0] + cw[2:3] * taps[1] + cw[1:2] * taps[2] + cw[0:1] * taps[3]
    xcb = xc.astype(BF16)
    ra, ia = [], []
    for n in range(LRU_BLOCKS):
        blk = xcb[:, n * LRU_BLOCK_W:(n + 1) * LRU_BLOCK_W]
        ra.append(_dot(blk, wa_ref[n]))
        ia.append(_dot(blk, wx_ref[n]))
    r = _sigmoid(jnp.concatenate(ra, axis=1) + ba)
    i = _sigmoid(jnp.concatenate(ia, axis=1) + bx)
    z = -a_param
    sp = jnp.maximum(z, 0.0) + jnp.log1p(jnp.exp(-jnp.abs(z)))
    log_a = (-LRU_C) * r * sp
    a = jnp.exp(log_a)
    one_minus_a2 = -jnp.tanh(log_a) * (a * a + 1.0)
    mult = jnp.sqrt(one_minus_a2)
    return xc, xcb, r, i, sp, a, mult


def _lru_in_fwd(x, g0, w_in, later_shards, ts):
    s = x.shape[0]
    nt = s // ts
    n = len(later_shards)

    def body(*refs):
        x_ref, g_ref, w_ref = refs[:3]
        shard_refs = refs[3:3 + n]
        xb_ref, gate_ref, h_ref = refs[3 + n:6 + n]
        wfull_ref = refs[6 + 2 * n]
        start, forward, finish = _gather_phases(shard_refs, refs[6 + n:6 + 2 * n],
                                                *refs[7 + 2 * n:])
        step = pl.program_id(0)
        pl.when(step == 0)(start)

        @pl.when(step == 0)
        def _():
            for j in range(N_DEV):
                wfull_ref[:, j * LRU_IN_SHARD:(j + 1) * LRU_IN_SHARD] = w_ref[j]

        xv = x_ref[...]
        h = (xv * _rstd(xv) * g_ref[...]).astype(BF16)
        u = _dot(h, wfull_ref[...])
        xb_ref[...] = u[:, :LRU_WIDTH]
        gate_ref[...] = u[:, LRU_WIDTH:]
        h_ref[...] = h
        pl.when(step == (2 * nt) // 3)(forward)
        pl.when(step == nt - 1)(finish)

    hbm = pl.BlockSpec(memory_space=pl.ANY)
    res = pl.pallas_call(
        body, name="lru_in_fwd", grid=(nt,),
        in_specs=[pl.BlockSpec((ts, D_MODEL), lambda i: (i, 0)),
                  _const_spec((1, D_MODEL)),
                  _const_spec((N_DEV, D_MODEL, LRU_IN_SHARD))] + [hbm] * n,
        out_specs=[pl.BlockSpec((ts, LRU_WIDTH), lambda i: (i, 0)),
                   pl.BlockSpec((ts, LRU_WIDTH), lambda i: (i, 0)),
                   pl.BlockSpec((ts, D_MODEL), lambda i: (i, 0))] + [hbm] * n,
        out_shape=[jax.ShapeDtypeStruct((s, LRU_WIDTH), F32),
                   jax.ShapeDtypeStruct((s, LRU_WIDTH), F32),
                   jax.ShapeDtypeStruct((s, D_MODEL), BF16)]
        + [jax.ShapeDtypeStruct((N_DEV,) + a.shape, a.dtype) for a in later_shards],
        scratch_shapes=[pltpu.VMEM((D_MODEL, 2 * LRU_WIDTH), BF16)] + _gather_sems(n),
        compiler_params=_params(),
    )(x, g0, w_in, *later_shards)
    return res[0], res[1], res[2], res[3:]


def _lru_core_fwd(xb, gate, cw, cb, wa, ba, wx, bx, a_param, ts):
    s = xb.shape[0]

    def body(xb_ref, gate_ref, cw_ref, cb_ref, wa_ref, ba_ref, wx_ref, bx_ref, ap_ref,
             y_ref, hs_ref, prev_ref, hcar_ref):
        @pl.when(pl.program_id(0) == 0)
        def _():
            prev_ref[...] = jnp.zeros_like(prev_ref)
            hcar_ref[...] = jnp.zeros_like(hcar_ref)

        xbv = xb_ref[...]
        taps = _conv_taps(xbv, prev_ref[...])
        xc, _, _, i, _, a, mult = _lru_pre(taps, cw_ref[...], cb_ref[...], wa_ref, ba_ref[...],
                                           wx_ref, bx_ref[...], ap_ref[...])
        bterm = mult * (i * xc)
        cum_a, hloc = _scan_rows(a, bterm)
        hs = cum_a * hcar_ref[SUBLANES - 1:SUBLANES, :] + hloc
        gv = gate_ref[...]
        y_ref[...] = (hs * (gv * _sigmoid(gv))).astype(BF16)
        hs_ref[...] = hs
        prev_ref[...] = xbv[ts - SUBLANES:, :]
        hcar_ref[...] = hs[ts - SUBLANES:, :]

    vec = _const_spec((1, LRU_WIDTH))
    blk = _const_spec((LRU_BLOCKS, LRU_BLOCK_W, LRU_BLOCK_W))
    tile = pl.BlockSpec((ts, LRU_WIDTH), lambda i: (i, 0))
    return pl.pallas_call(
        body, name="lru_core_fwd", grid=(s // ts,),
        in_specs=[tile, tile, _const_spec((CONV_WIDTH, LRU_WIDTH)), vec, blk, vec, blk, vec, vec],
        out_specs=[tile, tile],
        out_shape=[jax.ShapeDtypeStruct((s, LRU_WIDTH), BF16),
                   jax.ShapeDtypeStruct((s, LRU_WIDTH), F32)],
        scratch_shapes=[pltpu.VMEM((SUBLANES, LRU_WIDTH), F32),
                        pltpu.VMEM((SUBLANES, LRU_WIDTH), F32)],
        compiler_params=_params(),
    )(xb, gate, cw, cb, wa, ba, wx, bx, a_param)


def _fox_pre_fwd(x, y, w_out, g1, wf, bf, ts):
    s = x.shape[0]

    def body(x_ref, y_ref, w_ref, g_ref, wf_ref, bf_ref, x1_ref, h1_ref, f_ref, cp_ref, ccar_ref):
        @pl.when(pl.program_id(0) == 0)
        def _():
            ccar_ref[...] = jnp.zeros_like(ccar_ref)

        x1 = x_ref[...] + _dot(y_ref[...], w_ref[...])
        h1 = (x1 * _rstd(x1) * g_ref[...]).astype(BF16)
        f = _dot(h1, wf_ref[...]) + bf_ref[...]
        logsig = jnp.minimum(f, 0.0) - jnp.log1p(jnp.exp(-jnp.abs(f)))
        cum = _cumsum_rows(logsig) + ccar_ref[SUBLANES - 1:SUBLANES, :]
        hi, mid, lo = _split3(cum)
        lane = lax.broadcasted_iota(jnp.int32, cum.shape, 1)
        packed = jnp.where(lane < HEADS, hi.astype(F32), jnp.where(
            lane < 2 * HEADS, pltpu.roll(mid.astype(F32), HEADS, 1), jnp.where(
                lane < 3 * HEADS, pltpu.roll(lo.astype(F32), 2 * HEADS, 1), 0.0)))
        x1_ref[...] = x1
        h1_ref[...] = h1
        f_ref[...] = f
        cp_ref[...] = packed.astype(BF16)
        ccar_ref[...] = cum[ts - SUBLANES:, :]

    return pl.pallas_call(
        body, name="fox_pre_fwd", grid=(s // ts,),
        in_specs=[pl.BlockSpec((ts, D_MODEL), lambda i: (i, 0)),
                  pl.BlockSpec((ts, LRU_WIDTH), lambda i: (i, 0)),
                  _const_spec((LRU_WIDTH, D_MODEL)),
                  _const_spec((1, D_MODEL)),
                  _const_spec((D_MODEL, LANES)),
                  _const_spec((1, LANES))],
        out_specs=[pl.BlockSpec((ts, D_MODEL), lambda i: (i, 0)),
                   pl.BlockSpec((ts, D_MODEL), lambda i: (i, 0)),
                   pl.BlockSpec((ts, LANES), lambda i: (i, 0)),
                   pl.BlockSpec((ts, LANES), lambda i: (i, 0))],
        out_shape=[jax.ShapeDtypeStruct((s, D_MODEL), F32),
                   jax.ShapeDtypeStruct((s, D_MODEL), BF16),
                   jax.ShapeDtypeStruct((s, LANES), F32),
                   jax.ShapeDtypeStruct((s, LANES), BF16)],
        scratch_shapes=[pltpu.VMEM((SUBLANES, LANES), F32)],
        compiler_params=_params(),
    )(x, y, w_out, g1, wf, bf)


def _fox_proj_fwd(h1, cparts, w, first, ng, sel, bias, out_dtype, ts, name):
    s = h1.shape[0]
    width = HEADS * HEAD_DIM
    use_sel = sel is not None

    def body(*refs):
        if use_sel:
            h_ref, cp_ref, w_ref, sel_ref, b_ref, o_ref = refs
            proj = _dot(h_ref[...], w_ref[...])
            if first == 0:
                proj = proj * jnp.where(pl.program_id(0) == 0, QK_SCALE, 1.0)
            acc = _heads_to_padded(proj) + _dot(cp_ref[...], sel_ref[...]) + b_ref[...]
        else:
            h_ref, w_ref, o_ref = refs
            acc = _heads_to_padded(_dot(h_ref[...], w_ref[...]))
        o_ref[...] = acc.astype(out_dtype)

    in_specs = [pl.BlockSpec((ts, D_MODEL), lambda j, i: (i, 0))]
    args = [h1]
    if use_sel:
        in_specs.append(pl.BlockSpec((ts, LANES), lambda j, i: (i, 0)))
        args.append(cparts)
    in_specs.append(pl.BlockSpec((D_MODEL, width), lambda j, i: (0, first + j)))
    args.append(w)
    if use_sel:
        in_specs.append(pl.BlockSpec((None, LANES, FOX_PAD), lambda j, i: (j, 0, 0)))
        in_specs.append(pl.BlockSpec((None, 1, FOX_PAD), lambda j, i: (j, 0, 0)))
        args += [sel, bias]
    return pl.pallas_call(
        body, name=name, grid=(ng, s // ts),
        in_specs=in_specs,
        out_specs=pl.BlockSpec((None, ts, FOX_PAD), lambda j, i: (j, i, 0)),
        out_shape=jax.ShapeDtypeStruct((ng, s, FOX_PAD), out_dtype),
        compiler_params=_params(2),
    )(*args)


def _attn_fwd(qkv, blk, hps=HEADS_PER_STEP):
    s = qkv.shape[1]
    nblk = s // blk
    wide = 2 * blk
    heads = [slice(i * HEAD_PAD, (i + 1) * HEAD_PAD) for i in range(hps)]

    def body(q_ref, k_ref, v_ref, o_ref, qb_ref, acc_ref, m_ref):
        qi = pl.program_id(1)
        row = lax.broadcasted_iota(jnp.int32, (blk, blk), 0)
        col = lax.broadcasted_iota(jnp.int32, (blk, blk), 1)
        lane = lax.broadcasted_iota(jnp.int32, (blk, HEAD_PAD), 1)
        qs = [q_ref[:, hd] for hd in heads]
        for i in range(hps):
            acc_ref[i] = jnp.zeros((blk, HEAD_PAD), F32)
            m_ref[i] = jnp.full((blk, HEAD_PAD), NEG_BIG, F32)

        def step(k0, size, masked):
            scores = [_dot_nt(q, k_ref[pl.ds(k0, size), hd]) for q, hd in zip(qs, heads)]
            for i, (sc, hd) in enumerate(zip(scores, heads)):
                v = v_ref[pl.ds(k0, size), hd]
                if masked:
                    sc = jnp.where(col <= row, sc, NEG_BIG)
                m = m_ref[i]
                m_new = jnp.maximum(m, jnp.max(sc, axis=-1, keepdims=True))
                p = jnp.exp((sc - jnp.tile(m_new, (1, size // HEAD_PAD))).astype(BF16))
                acc_ref[i] = jnp.exp(m - m_new) * acc_ref[i] + _dot(p, v)
                m_ref[i] = m_new

        def wide_step(kk, _):
            step(pl.multiple_of(kk * wide, wide), wide, False)
            return 0

        lax.fori_loop(0, qi // 2, wide_step, 0)

        @pl.when(qi % 2 == 1)
        def _():
            step(pl.multiple_of((qi - 1) * blk, blk), blk, False)

        step(pl.multiple_of(qi * blk, blk), blk, True)
        for i, (q, hd) in enumerate(zip(qs, heads)):
            acc = acc_ref[i]
            l = jnp.broadcast_to(acc[:, LANE_ONE_V:LANE_ONE_V + 1], (blk, HEAD_PAD))
            o_ref[:, hd] = (acc / l).astype(BF16)
            hi, mid, lo = _split3(-(m_ref[i] + jnp.log(l)))
            qb_ref[:, hd] = jnp.where(lane == LANE_LSE, hi, jnp.where(
                lane == LANE_LSE + 1, mid, jnp.where(lane == LANE_LSE + 2, lo, q)))

    width = hps * HEAD_PAD

    def whole(j):
        return pl.BlockSpec((None, s, width), lambda h, i: (j, 0, h))

    out_spec = pl.BlockSpec((blk, width), lambda h, i: (i, h))
    return pl.pallas_call(
        body, name="attn_fwd", grid=(HEADS // hps, nblk),
        in_specs=[pl.BlockSpec((None, blk, width), lambda h, i: (0, i, h)), whole(1), whole(2)],
        out_specs=[out_spec, out_spec],
        out_shape=[jax.ShapeDtypeStruct((s, FOX_PAD), BF16),
                   jax.ShapeDtypeStruct((s, FOX_PAD), BF16)],
        scratch_shapes=[pltpu.VMEM((hps, blk, HEAD_PAD), F32),
                        pltpu.VMEM((hps, blk, HEAD_PAD), F32)],
        compiler_params=_params(2),
    )(qkv, qkv, qkv)


def _fox_out_loss(o, gate, w_out, x1, target, gf, ts):
    s = x1.shape[0]

    def body(o_ref, gt_ref, w_ref, x1_ref, t_ref, g_ref, dx2_ref, dx2b_ref, y2_ref, loss_ref,
             gfin_ref):
        @pl.when(pl.program_id(0) == 0)
        def _():
            loss_ref[...] = jnp.zeros_like(loss_ref)
            gfin_ref[...] = jnp.zeros_like(gfin_ref)

        gv = gt_ref[...]
        y2 = _heads_from_padded(o_ref[...] * (gv * _sigmoid(gv))).astype(BF16)
        x2 = x1_ref[...] + _dot(y2, w_ref[...])
        rstd = _rstd(x2)
        xhat = x2 * rstd
        g = g_ref[...]
        diff = xhat * g - t_ref[...]
        loss_ref[...] += 0.5 * jnp.sum(jnp.mean(diff * diff, axis=-1, keepdims=True))
        dy = diff * (1.0 / D_MODEL)
        gfin_ref[...] += jnp.sum(dy * xhat, axis=0, keepdims=True)
        dxh = dy * g
        dx2 = rstd * (dxh - xhat * jnp.mean(dxh * xhat, axis=-1, keepdims=True))
        dx2_ref[...] = dx2
        dx2b_ref[...] = dx2.astype(BF16)
        y2_ref[...] = y2

    return pl.pallas_call(
        body, name="fox_out_loss", grid=(s // ts,),
        in_specs=[pl.BlockSpec((ts, FOX_PAD), lambda i: (i, 0)),
                  pl.BlockSpec((ts, FOX_PAD), lambda i: (i, 0)),
                  _const_spec((HEADS * HEAD_DIM, D_MODEL)),
                  pl.BlockSpec((ts, D_MODEL), lambda i: (i, 0)),
                  pl.BlockSpec((ts, D_MODEL), lambda i: (i, 0)),
                  _const_spec((1, D_MODEL))],
        out_specs=[pl.BlockSpec((ts, D_MODEL), lambda i: (i, 0)),
                   pl.BlockSpec((ts, D_MODEL), lambda i: (i, 0)),
                   pl.BlockSpec((ts, HEADS * HEAD_DIM), lambda i: (i, 0)),
                   pl.BlockSpec((SUBLANES, LANES), lambda i: (0, 0)),
                   pl.BlockSpec((1, D_MODEL), lambda i: (0, 0))],
        out_shape=[jax.ShapeDtypeStruct((s, D_MODEL), F32),
                   jax.ShapeDtypeStruct((s, D_MODEL), BF16),
                   jax.ShapeDtypeStruct((s, HEADS * HEAD_DIM), BF16),
                   jax.ShapeDtypeStruct((SUBLANES, LANES), F32),
                   jax.ShapeDtypeStruct((1, D_MODEL), F32)],
        compiler_params=_params(),
    )(o, gate, w_out, x1, target, gf)


def _fox_out_bwd(dx2, w_out, o, gate, ts):
    s = dx2.shape[0]

    def body(dx_ref, w_ref, o_ref, gt_ref, do_ref, dg_ref):
        lane = lax.broadcasted_iota(jnp.int32, (ts, HEAD_PAD), 1)
        dy2 = _heads_to_padded(_dot_nt(dx_ref[...], w_ref[...]))
        gv = gt_ref[...]
        sg = _sigmoid(gv)
        ov = o_ref[...]
        dov = dy2 * (gv * sg)
        dg_ref[...] = (dy2 * ov * (sg * (1.0 + gv * (1.0 - sg)))).astype(BF16)
        prod = dov * ov
        for h in range(HEADS):
            sl = slice(h * HEAD_PAD, (h + 1) * HEAD_PAD)
            delta = jnp.sum(prod[:, sl], axis=-1, keepdims=True)
            hi = delta.astype(BF16)
            lo = (delta - hi.astype(F32)).astype(BF16)
            do_h = dov[:, sl].astype(BF16)
            do_ref[:, sl] = jnp.where(lane == LANE_ONE_V, -hi,
                                      jnp.where(lane == LANE_ONE_V + 1, -lo, do_h))

    tile = pl.BlockSpec((ts, FOX_PAD), lambda i: (i, 0))
    return pl.pallas_call(
        body, name="fox_out_bwd", grid=(s // ts,),
        in_specs=[pl.BlockSpec((ts, D_MODEL), lambda i: (i, 0)),
                  _const_spec((HEADS * HEAD_DIM, D_MODEL)), tile, tile],
        out_specs=[tile, tile],
        out_shape=[jax.ShapeDtypeStruct((s, FOX_PAD), BF16),
                   jax.ShapeDtypeStruct((s, FOX_PAD), BF16)],
        compiler_params=_params(),
    )(dx2, w_out, o, gate)


def _attn_bwd(qb, qkv, do, blk):
    s = qb.shape[0]
    nblk = s // blk
    half = blk // 2
    heads = [slice(i * HEAD_PAD, (i + 1) * HEAD_PAD) for i in range(HEADS_PER_STEP)]

    def body(q_ref, k_ref, v_ref, do_ref, dq_ref, dk_ref, dv_ref, dcum_ref, dq_acc, dkt_acc,
             dvt_acc, qt_ref, dot_ref):
        group = pl.program_id(0)
        kj = pl.program_id(1)
        row = lax.broadcasted_iota(jnp.int32, (blk, blk), 0)
        col = lax.broadcasted_iota(jnp.int32, (blk, blk), 1)
        lane = lax.broadcasted_iota(jnp.int32, (blk, LANES), 1)
        mine = [lane == group * HEADS_PER_STEP + i for i in range(HEADS_PER_STEP)]

        @pl.when(kj == 0)
        def _():
            dq_acc[...] = jnp.zeros_like(dq_acc)

            def transpose_block(bi, _):
                r0 = pl.multiple_of(bi * blk, blk)
                for i, hd in enumerate(heads):
                    qt_ref[i, bi] = q_ref[pl.ds(r0, blk), hd].T
                    dot_ref[i, bi] = do_ref[pl.ds(r0, blk), hd].T
                return 0

            lax.fori_loop(0, nblk, transpose_block, 0)

        @pl.when((group == 0) & (kj == 0))
        def _():
            dcum_ref[...] = jnp.zeros_like(dcum_ref)

        k0 = pl.multiple_of(kj * blk, blk)
        ks = [k_ref[:, hd] for hd in heads]
        vs = [v_ref[:, hd] for hd in heads]

        def step(qi, q_lo, nq, k_lo, nk, masked):
            parts = ([(0, q_lo, 0, nq)] if nq <= blk
                     else [(b, 0, b * blk, blk) for b in range(nq // blk)])
            q0 = pl.multiple_of(qi * blk + q_lo, half)
            qs = [q_ref[pl.ds(q0, nq), hd] for hd in heads]
            dos = [do_ref[pl.ds(q0, nq), hd] for hd in heads]
            kk = [k[k_lo:k_lo + nk] for k in ks]
            vv = [v[k_lo:k_lo + nk] for v in vs]
            scores = [_dot_nt(q, k) for q, k in zip(qs, kk)]
            dps = [_dot_nt(dov, v) for dov, v in zip(dos, vv)]
            for i, (hd, k, sc, dp) in enumerate(zip(heads, kk, scores, dps)):
                p = jnp.exp(sc.astype(BF16))
                if masked:
                    p = jnp.where(col[:nq, :nk] + k_lo <= row[:nq, :nk] + q_lo, p,
                                  jnp.zeros_like(p))
                ds = (p.astype(F32) * dp).astype(BF16)
                dvt = sum(_dot(dot_ref[i, qi + b, :, c:c + n], p[r:r + n]) for b, c, r, n in parts)
                dkt = sum(_dot(qt_ref[i, qi + b, :, c:c + n], ds[r:r + n]) for b, c, r, n in parts)
                if masked:
                    dvt_acc[i, :, k_lo:k_lo + nk] = dvt
                    dkt_acc[i, :, k_lo:k_lo + nk] = dkt
                else:
                    dvt_acc[i, :, k_lo:k_lo + nk] += dvt
                    dkt_acc[i, :, k_lo:k_lo + nk] += dkt
                dq_acc[pl.ds(q0, nq), hd] += _dot(ds, k)

        step(kj, 0, blk, 0, half, True)
        step(kj, half, half, half, half, True)

        n_after = nblk - 1 - kj

        def q_step(t, _):
            step(kj + 1 + 2 * t, 0, 2 * blk, 0, blk, False)
            return 0

        lax.fori_loop(0, n_after // 2, q_step, 0)

        @pl.when(n_after % 2 == 1)
        def _():
            step(nblk - 1, 0, blk, 0, blk, False)
        dcum = dcum_ref[pl.ds(k0, blk), :]
        for i, (hd, mask) in enumerate(zip(heads, mine)):
            dk = dkt_acc[i].T
            dk_ref[:, hd] = dk.astype(BF16)
            dv_ref[:, hd] = dvt_acc[i].astype(BF16).T
            dcum = jnp.where(mask, -dk[:, LANE_CK:LANE_CK + 1], dcum)
        dcum_ref[pl.ds(k0, blk), :] = dcum

        @pl.when(kj == nblk - 1)
        def _():
            def finish(bi, _):
                r0 = pl.multiple_of(bi * blk, blk)
                dcum = dcum_ref[pl.ds(r0, blk), :]
                for hd, mask in zip(heads, mine):
                    dq = dq_acc[pl.ds(r0, blk), hd]
                    dq_ref[pl.ds(r0, blk), hd] = dq.astype(BF16)
                    dcum = dcum + jnp.where(mask, dq[:, LANE_RB:LANE_RB + 1], 0.0)
                dcum_ref[pl.ds(r0, blk), :] = dcum
                return 0

            lax.fori_loop(0, nblk, finish, 0)

    width = HEADS_PER_STEP * HEAD_PAD
    whole = pl.BlockSpec((s, width), lambda h, j: (0, h))
    part = pl.BlockSpec((blk, width), lambda h, j: (j, h))
    out = jax.ShapeDtypeStruct((s, FOX_PAD), BF16)
    return pl.pallas_call(
        body, name="attn_bwd", grid=(HEADS // HEADS_PER_STEP, nblk),
        in_specs=[whole,
                  pl.BlockSpec((None, blk, width), lambda h, j: (1, j, h)),
                  pl.BlockSpec((None, blk, width), lambda h, j: (2, j, h)),
                  whole],
        out_specs=[whole, part, part, pl.BlockSpec((s, LANES), lambda h, j: (0, 0))],
        out_shape=[out, out, out, jax.ShapeDtypeStruct((s, LANES), F32)],
        scratch_shapes=[pltpu.VMEM((s, width), F32),
                        pltpu.VMEM((HEADS_PER_STEP, HEAD_PAD, blk), F32),
                        pltpu.VMEM((HEADS_PER_STEP, HEAD_PAD, blk), F32),
                        pltpu.VMEM((HEADS_PER_STEP, nblk, HEAD_PAD, blk), BF16),
                        pltpu.VMEM((HEADS_PER_STEP, nblk, HEAD_PAD, blk), BF16)],
        compiler_params=_params(2),
    )(qb, qkv, qkv, do)


def _fox_in_bwd(dq, dk, dv, dg, wt, wft, dcum, f, x1, dx2, g1, ts):
    s = x1.shape[0]
    nt = s // ts
    width = HEADS * HEAD_DIM

    def body(dq_ref, dk_ref, dv_ref, dg_ref, wt_ref, wft_ref, dcum_ref, f_ref, x1_ref, dx2_ref,
             g_ref, dx1_ref, dx1b_ref, df_ref, duq_ref, duk_ref, duv_ref, dug_ref, gn_ref, gbf_ref,
             rcar_ref):
        du_refs = (duq_ref, duk_ref, duv_ref, dug_ref)


        @pl.when(pl.program_id(0) == 0)
        def _():
            rcar_ref[...] = jnp.zeros_like(rcar_ref)
            gn_ref[...] = jnp.zeros_like(gn_ref)
            gbf_ref[...] = jnp.zeros_like(gbf_ref)

        rsum = _cumsum_rows(dcum_ref[...], reverse=True) + rcar_ref[0:1, :]
        df = rsum * _sigmoid(-f_ref[...])
        dfb = df.astype(BF16)
        dh = _dot_nt(dfb, wft_ref[...])
        for j, ref in enumerate((dq_ref, dk_ref, dv_ref, dg_ref)):
            du = _heads_from_padded(ref[...])
            du_refs[j][...] = du
            if j == 0:
                du = du * QK_SCALE
            dh = dh + _dot_nt(du, wt_ref[:, j * width:(j + 1) * width])
        dxn, dgn = _norm_bwd(x1_ref[...], g_ref[...], dh)
        dx1 = dx2_ref[...] + dxn
        dx1_ref[...] = dx1
        dx1b_ref[...] = dx1.astype(BF16)
        df_ref[...] = dfb
        gn_ref[...] += dgn
        gbf_ref[...] += jnp.sum(df, axis=0, keepdims=True)
        rcar_ref[...] = rsum[0:SUBLANES, :]

    rev = lambda i: (nt - 1 - i, 0)
    wide = pl.BlockSpec((ts, FOX_PAD), rev)
    return pl.pallas_call(
        body, name="fox_in_bwd", grid=(nt,),
        in_specs=[wide, wide, wide, wide,
                  _const_spec((D_MODEL, FOX_IN_COLS)),
                  _const_spec((D_MODEL, LANES)),
                  pl.BlockSpec((ts, LANES), rev),
                  pl.BlockSpec((ts, LANES), rev),
                  pl.BlockSpec((ts, D_MODEL), rev),
                  pl.BlockSpec((ts, D_MODEL), rev),
                  _const_spec((1, D_MODEL))],
        out_specs=[pl.BlockSpec((ts, D_MODEL), rev),
                   pl.BlockSpec((ts, D_MODEL), rev),
                   pl.BlockSpec((ts, LANES), rev)]
        + [pl.BlockSpec((ts, width), rev)] * 4
        + [pl.BlockSpec((1, D_MODEL), lambda i: (0, 0)),
           pl.BlockSpec((1, LANES), lambda i: (0, 0))],
        out_shape=[jax.ShapeDtypeStruct((s, D_MODEL), F32),
                   jax.ShapeDtypeStruct((s, D_MODEL), BF16),
                   jax.ShapeDtypeStruct((s, LANES), BF16)]
        + [jax.ShapeDtypeStruct((s, width), BF16)] * 4
        + [jax.ShapeDtypeStruct((1, D_MODEL), F32),
                   jax.ShapeDtypeStruct((1, LANES), F32)],
        scratch_shapes=[pltpu.VMEM((SUBLANES, LANES), F32)],
        compiler_params=_params(),
    )(dq, dk, dv, dg, wt, wft, dcum, f, x1, dx2, g1)


def _lru_core_bwd(dx1b, w_out, xb, gate, hs, cw, cb, wa, ba, wx, bx, a_param, wa_t, wx_t,
                  chip_sums, ts):
    s = xb.shape[0]
    nt = s // ts
    tpb = ts // SUBLANES
    n_ex = len(chip_sums)

    def body(*refs):
        (dx_ref, wo_ref, xb_ref, xbh_ref, gate_ref, hs_ref, hsh_ref, cw_ref, cb_ref, wa_ref,
         ba_ref, wx_ref, bx_ref, ap_ref, wat_ref, wxt_ref) = refs[:16]
        sum_refs = refs[16:16 + n_ex]
        du_ref, gwa_ref, gwx_ref, gvec_ref = refs[16 + n_ex:20 + n_ex]
        got_refs = refs[20 + n_ex:20 + 2 * n_ex]
        acar_ref, dhcar_ref, dxccar_ref = refs[20 + 2 * n_ex:23 + 2 * n_ex]
        start, finish = _chip_exchange_phases(sum_refs, got_refs, *refs[23 + 2 * n_ex:])
        step = pl.program_id(0)
        pl.when(step == 0)(start)

        @pl.when(step == 0)
        def _():
            acar_ref[...] = jnp.zeros_like(acar_ref)
            dhcar_ref[...] = jnp.zeros_like(dhcar_ref)
            dxccar_ref[...] = jnp.zeros_like(dxccar_ref)
            gwa_ref[...] = jnp.zeros_like(gwa_ref)
            gwx_ref[...] = jnp.zeros_like(gwx_ref)
            gvec_ref[...] = jnp.zeros_like(gvec_ref)

        first_tile = step == nt - 1
        halo_on = jnp.where(first_tile, 0.0, 1.0)
        prev8 = xbh_ref[...] * halo_on
        hprev_row = hsh_ref[SUBLANES - 1:SUBLANES, :] * halo_on

        xbv = xb_ref[...]
        taps = _conv_taps(xbv, prev8)
        cw_v = cw_ref[...]
        xc, xcb, r, i, sp, a, mult = _lru_pre(taps, cw_v, cb_ref[...], wa_ref, ba_ref[...],
                                              wx_ref, bx_ref[...], ap_ref[...])
        hs = hs_ref[...]
        gv = gate_ref[...]
        sg = _sigmoid(gv)
        dy = _dot_nt(dx_ref[...], wo_ref[...])
        dhs = dy * (gv * sg)
        dgate = dy * hs * (sg * (1.0 + gv * (1.0 - sg)))

        rows = lax.broadcasted_iota(jnp.int32, a.shape, 0)
        a_next = jnp.where(rows < ts - 1, pltpu.roll(a, ts - 1, 0), acar_ref[0:1, :])
        cum_a, dh_loc = _scan_rows(a_next, dhs, reverse=True)
        dh = cum_a * dhcar_ref[0:1, :] + dh_loc
        h_prev = jnp.where(rows >= 1, pltpu.roll(hs, 1, 0), hprev_row)

        da = dh * h_prev
        ixc = i * xc
        dmult = dh * ixc
        di = dh * mult * xc
        dxc = dh * mult * i
        dlog_a = da * a - dmult * (a * a) / mult
        dr = dlog_a * ((-LRU_C) * sp)
        dsp = jnp.sum(dlog_a * ((-LRU_C) * r), axis=0, keepdims=True)
        dra = dr * r * (1.0 - r)
        dia = di * i * (1.0 - i)
        drab = dra.astype(BF16)
        diab = dia.astype(BF16)
        back = []
        for n in range(LRU_BLOCKS):
            sl = slice(n * LRU_BLOCK_W, (n + 1) * LRU_BLOCK_W)
            gwa_ref[n] += _dot_tn(xcb[:, sl], drab[:, sl])
            gwx_ref[n] += _dot_tn(xcb[:, sl], diab[:, sl])
            back.append(_dot(drab[:, sl], wat_ref[n]) + _dot(diab[:, sl], wxt_ref[n]))
        dxc = dxc + jnp.concatenate(back, axis=1)

        nxt8 = dxccar_ref[...]
        rows8 = lax.broadcasted_iota(jnp.int32, nxt8.shape, 0)
        dxb = cw_v[3:4] * dxc
        for j in range(1, CONV_WIDTH):
            rj = pltpu.roll(dxc, ts - j, 0)
            pj = pltpu.roll(nxt8, SUBLANES - j, 0)
            tail = jnp.where(rows8 >= SUBLANES - j, pj, rj[ts - SUBLANES:])
            dxb = dxb + cw_v[3 - j:4 - j] * jnp.concatenate([rj[:ts - SUBLANES], tail], axis=0)

        du_ref[:, :LRU_WIDTH] = dxb.astype(BF16)
        du_ref[:, LRU_WIDTH:] = dgate.astype(BF16)

        z = -ap_ref[...]
        gvec = [jnp.sum(dxc * taps[3 - k], axis=0, keepdims=True) for k in range(CONV_WIDTH)]
        gvec.append(jnp.sum(dxc, axis=0, keepdims=True))
        gvec.append(jnp.sum(dra, axis=0, keepdims=True))
        gvec.append(jnp.sum(dia, axis=0, keepdims=True))
        gvec.append(-dsp * _sigmoid(z))
        gvec_ref[...] += jnp.concatenate(gvec, axis=0)

        acar_ref[...] = a[0:SUBLANES, :]
        dhcar_ref[...] = dh[0:SUBLANES, :]
        dxccar_ref[...] = dxc[0:SUBLANES, :]
        pl.when(step == nt - 1)(finish)

    rev = lambda i: (nt - 1 - i, 0)
    halo = lambda i: (jnp.maximum((nt - 1 - i) * tpb - 1, 0), 0)
    tile = pl.BlockSpec((ts, LRU_WIDTH), rev)
    halo_spec = pl.BlockSpec((SUBLANES, LRU_WIDTH), halo)
    vec = _const_spec((1, LRU_WIDTH))
    blk = _const_spec((LRU_BLOCKS, LRU_BLOCK_W, LRU_BLOCK_W))
    acc_blk = pl.BlockSpec((LRU_BLOCKS, LRU_BLOCK_W, LRU_BLOCK_W), lambda i: (0, 0, 0))
    hbm = pl.BlockSpec(memory_space=pl.ANY)
    res = pl.pallas_call(
        body, name="lru_core_bwd", grid=(nt,),
        in_specs=[pl.BlockSpec((ts, D_MODEL), rev),
                  _const_spec((LRU_WIDTH, D_MODEL)),
                  tile, halo_spec, tile, tile, halo_spec,
                  _const_spec((CONV_WIDTH, LRU_WIDTH)), vec, blk, vec, blk, vec, vec, blk, blk]
        + [hbm] * n_ex,
        out_specs=[pl.BlockSpec((ts, 2 * LRU_WIDTH), rev), acc_blk, acc_blk,
                   pl.BlockSpec((SUBLANES, LRU_WIDTH), lambda i: (0, 0))] + [hbm] * n_ex,
        out_shape=[jax.ShapeDtypeStruct((s, 2 * LRU_WIDTH), BF16),
                   jax.ShapeDtypeStruct((LRU_BLOCKS, LRU_BLOCK_W, LRU_BLOCK_W), F32),
                   jax.ShapeDtypeStruct((LRU_BLOCKS, LRU_BLOCK_W, LRU_BLOCK_W), F32),
                   jax.ShapeDtypeStruct((SUBLANES, LRU_WIDTH), F32)]
        + [jax.ShapeDtypeStruct(a.shape, a.dtype) for a in chip_sums],
        scratch_shapes=[pltpu.VMEM((SUBLANES, LRU_WIDTH), F32),
                        pltpu.VMEM((SUBLANES, LRU_WIDTH), F32),
                        pltpu.VMEM((SUBLANES, LRU_WIDTH), F32)] + _chip_exchange_sems(n_ex),
        compiler_params=_params(),
    )(dx1b, w_out, xb, xb, gate, hs, hs, cw, cb, wa, ba, wx, bx, a_param, wa_t, wx_t, *chip_sums)
    return res[0], res[1], res[2], res[3], res[4:]


def _lru_in_bwd(du, w_in, x, dx1, g0, chip_sums, ts):
    s = x.shape[0]
    nt = s // ts
    n = len(chip_sums)

    def body(*refs):
        du_ref, w_ref, x_ref, dx1_ref, g_ref = refs[:5]
        sum_refs = refs[5:5 + n]
        gx_ref, gn_ref = refs[5 + n:7 + n]
        got_refs = refs[7 + n:7 + 2 * n]
        wfull_ref = refs[7 + 2 * n]
        start, finish = _chip_exchange_phases(sum_refs, got_refs, *refs[8 + 2 * n:])
        step = pl.program_id(0)
        pl.when(step == 0)(start)

        @pl.when(step == 0)
        def _():
            gn_ref[...] = jnp.zeros_like(gn_ref)
            for j in range(N_DEV):
                wfull_ref[:, j * LRU_IN_SHARD:(j + 1) * LRU_IN_SHARD] = w_ref[j]

        dh = _dot_nt(du_ref[...], wfull_ref[...])
        dxn, dgn = _norm_bwd(x_ref[...], g_ref[...], dh)
        gx_ref[...] = dx1_ref[...] + dxn
        gn_ref[...] += dgn
        pl.when(step == nt - 1)(finish)

    tile = pl.BlockSpec((ts, D_MODEL), lambda i: (i, 0))
    hbm = pl.BlockSpec(memory_space=pl.ANY)
    res = pl.pallas_call(
        body, name="lru_in_bwd", grid=(nt,),
        in_specs=[pl.BlockSpec((ts, 2 * LRU_WIDTH), lambda i: (i, 0)),
                  _const_spec((N_DEV, D_MODEL, LRU_IN_SHARD)), tile, tile,
                  _const_spec((1, D_MODEL))] + [hbm] * n,
        out_specs=[tile, pl.BlockSpec((1, D_MODEL), lambda i: (0, 0))] + [hbm] * n,
        out_shape=[jax.ShapeDtypeStruct((s, D_MODEL), F32),
                   jax.ShapeDtypeStruct((1, D_MODEL), F32)]
        + [jax.ShapeDtypeStruct(a.shape, a.dtype) for a in chip_sums],
        scratch_shapes=[pltpu.VMEM((D_MODEL, 2 * LRU_WIDTH), BF16)] + _chip_exchange_sems(n),
        compiler_params=_params(),
    )(du, w_in, x, dx1, g0, *chip_sums)
    return res[0], res[1], res[2:]


def _weight_grad(a, b, ts, name, scale=1.0, col_shards=1):
    s, ka = a.shape
    nb = b.shape[1]
    nt = s // ts
    per = nb // col_shards

    def body(a_ref, b_ref, o_ref):
        @pl.when(pl.program_id(0) == 0)
        def _():
            o_ref[...] = jnp.zeros_like(o_ref)

        if col_shards == 1:
            o_ref[...] += _dot_tn(a_ref[...], b_ref[...])
        else:
            acc = _dot_tn(a_ref[...], b_ref[...])
            for j in range(col_shards):
                o_ref[j] += acc[:, j * per:(j + 1) * per]
        if scale != 1.0:
            @pl.when(pl.program_id(0) == nt - 1)
            def _():
                o_ref[...] = o_ref[...] * scale

    out_dims = (ka, nb) if col_shards == 1 else (col_shards, ka, per)
    return pl.pallas_call(
        body, name=name, grid=(nt,),
        in_specs=[pl.BlockSpec((ts, ka), lambda i: (i, 0)),
                  pl.BlockSpec((ts, nb), lambda i: (i, 0))],
        out_specs=pl.BlockSpec(out_dims, lambda i: (0,) * len(out_dims)),
        out_shape=jax.ShapeDtypeStruct(out_dims, F32),
        compiler_params=_params(),
    )(a, b)


def _sum_parts(gp_ref):
    g = gp_ref[0].astype(F32)
    for k in range(1, gp_ref.shape[0]):
        g = g + gp_ref[k].astype(F32)
    return g


def _adamw(g_parts, w, m, v, tr, name):
    nparts, rows, cols = g_parts.shape

    def body(gp_ref, w_ref, m_ref, v_ref, g_ref, d_ref, mo_ref, vo_ref):
        g = _sum_parts(gp_ref)
        m2 = ADAM_B1 * m_ref[...] + (1.0 - ADAM_B1) * g
        v2 = ADAM_B2 * v_ref[...] + (1.0 - ADAM_B2) * (g * g)
        m_hat = m2 / (1.0 - ADAM_B1 ** ADAM_STEP)
        v_hat = v2 / (1.0 - ADAM_B2 ** ADAM_STEP)
        g_ref[...] = g
        d_ref[...] = (-ADAM_LR) * (m_hat / (jnp.sqrt(v_hat) + ADAM_EPS) + ADAM_WD * w_ref[...])
        mo_ref[...] = m2
        vo_ref[...] = v2

    tile = pl.BlockSpec((tr, cols), lambda i: (i, 0))
    out = jax.ShapeDtypeStruct((rows, cols), F32)
    return pl.pallas_call(
        body, name=name, grid=(rows // tr,),
        in_specs=[pl.BlockSpec((nparts, tr, cols), lambda i: (0, i, 0)), tile, tile, tile],
        out_specs=[tile, tile, tile, tile],
        out_shape=[out, out, out, out],
        compiler_params=_params(),
    )(g_parts, w, m, v)


def _reduce_parts(g_parts, name):
    _, rows, cols = g_parts.shape

    def body(gp_ref, g_ref):
        g_ref[...] = _sum_parts(gp_ref)

    return pl.pallas_call(
        body, name=name,
        out_shape=jax.ShapeDtypeStruct((rows, cols), F32),
        compiler_params=pltpu.CompilerParams(vmem_limit_bytes=VMEM_LIMIT_BYTES),
    )(g_parts)


def _mesh_pos():
    ix, iy, ic = lax.axis_index("x"), lax.axis_index("y"), lax.axis_index("c")
    return ix, iy, ic


def _peer(ix, iy, ic, mask):
    px = 1 - ix if mask & 4 else ix
    py = 1 - iy if mask & 2 else iy
    pc = 1 - ic if mask & 1 else ic
    return (px, py, pc), 4 * px + 2 * py + pc


def _exchange(arrays, scatter, name):
    n = len(arrays)

    def body(*refs):
        x_refs, o_refs = refs[:n], refs[n:2 * n]
        send_sems, recv_sems, local_sems = refs[2 * n:]
        ix, iy, ic = _mesh_pos()
        me = 4 * ix + 2 * iy + ic

        def src(a, dest):
            return x_refs[a].at[dest] if scatter else x_refs[a]

        local = [pltpu.make_async_copy(src(a, me), o_refs[a].at[me], local_sems.at[a])
                 for a in range(n)]
        for cp in local:
            cp.start()
        sends = []
        for mask in range(1, N_DEV):
            peer, pidx = _peer(ix, iy, ic, mask)
            for a in range(n):
                cp = pltpu.make_async_remote_copy(
                    src_ref=src(a, pidx), dst_ref=o_refs[a].at[me],
                    send_sem=send_sems.at[a, mask - 1], recv_sem=recv_sems.at[a, mask - 1],
                    device_id=peer, device_id_type=pl.DeviceIdType.MESH)
                cp.start()
                sends.append(cp)
        for mask in range(1, N_DEV):
            peer, pidx = _peer(ix, iy, ic, mask)
            for a in range(n):
                pltpu.make_async_remote_copy(
                    src_ref=src(a, me), dst_ref=o_refs[a].at[pidx],
                    send_sem=send_sems.at[a, mask - 1], recv_sem=recv_sems.at[a, mask - 1],
                    device_id=peer, device_id_type=pl.DeviceIdType.MESH).wait_recv()
        for cp in sends:
            cp.wait_send()
        for cp in local:
            cp.wait()

    out_shape = [jax.ShapeDtypeStruct(x.shape if scatter else (N_DEV,) + x.shape, x.dtype)
                 for x in arrays]
    return pl.pallas_call(
        body, name=name,
        in_specs=[pl.BlockSpec(memory_space=pl.ANY)] * n,
        out_specs=[pl.BlockSpec(memory_space=pl.ANY)] * n,
        out_shape=out_shape,
        scratch_shapes=[pltpu.SemaphoreType.DMA((n, N_DEV - 1)),
                        pltpu.SemaphoreType.DMA((n, N_DEV - 1)),
                        pltpu.SemaphoreType.DMA((n,))],
    )(*arrays)


def _gather_two_level(arrays, name):
    n = len(arrays)

    def body(*refs):
        start, forward, finish = _gather_phases(refs[:n], refs[n:2 * n], *refs[2 * n:])
        start()
        forward()
        finish()

    return pl.pallas_call(
        body, name=name,
        in_specs=[pl.BlockSpec(memory_space=pl.ANY)] * n,
        out_specs=[pl.BlockSpec(memory_space=pl.ANY)] * n,
        out_shape=[jax.ShapeDtypeStruct((N_DEV,) + x.shape, x.dtype) for x in arrays],
        scratch_shapes=_gather_sems(n),
    )(*arrays)


def _gather_sems(n):
    return [pltpu.SemaphoreType.DMA((n, N_DEV - 1)), pltpu.SemaphoreType.DMA((n, N_DEV - 1)),
            pltpu.SemaphoreType.DMA((n,))]


def _gather_phases(x_refs, o_refs, send_sems, recv_sems, local_sems):
    n = len(x_refs)
    ix, iy, ic = _mesh_pos()
    me, sibling = (ix, iy, ic), (ix, iy, 1 - ic)
    chips = [(1 - ix, iy), (ix, 1 - iy), (1 - ix, 1 - iy)]

    def idx(px, py, pc):
        return 4 * px + 2 * py + pc

    def copy(a, k, block, to, src=None):
        dst = o_refs[a].at[idx(*block)]
        return pltpu.make_async_remote_copy(
            src_ref=dst if src is None else src, dst_ref=dst,
            send_sem=send_sems.at[a, k], recv_sem=recv_sems.at[a, k],
            device_id=to, device_id_type=pl.DeviceIdType.MESH)

    def local():
        return [pltpu.make_async_copy(x_refs[a], o_refs[a].at[idx(*me)], local_sems.at[a])
                for a in range(n)]

    def first():
        out = []
        for a in range(n):
            out.append(copy(a, 0, me, sibling, src=x_refs[a]))
            out += [copy(a, 1 + j, me, (*chip, ic), src=x_refs[a])
                    for j, chip in enumerate(chips)]
        return out

    def passed():
        return [copy(a, 4 + j, (*chip, ic), sibling)
                for j, chip in enumerate(chips) for a in range(n)]

    def start():
        for cp in local() + first():
            cp.start()

    def forward():
        for j, chip in enumerate(chips):
            for a in range(n):
                copy(a, 1 + j, (*chip, ic), me).wait_recv()
                copy(a, 4 + j, (*chip, ic), sibling).start()

    def finish():
        for a in range(n):
            copy(a, 0, sibling, me).wait_recv()
            for j, chip in enumerate(chips):
                copy(a, 4 + j, (*chip, 1 - ic), me).wait_recv()
        for cp in first() + passed():
            cp.wait_send()
        for cp in local():
            cp.wait()

    return start, forward, finish


def _swap_sibling(arrays, name):
    n = len(arrays)
    n_chips = N_DEV // 2

    def body(*refs):
        x_refs, got_refs = refs[:n], refs[n:2 * n]
        send_sems, recv_sems = refs[2 * n:]
        ix, iy, ic = _mesh_pos()
        sibling = (ix, iy, 1 - ic)
        sends = []
        for a in range(n):
            for q in range(n_chips):
                cp = pltpu.make_async_remote_copy(
                    src_ref=x_refs[a].at[q, 1 - ic], dst_ref=got_refs[a].at[q],
                    send_sem=send_sems.at[a, q], recv_sem=recv_sems.at[a, q],
                    device_id=sibling, device_id_type=pl.DeviceIdType.MESH)
                cp.start()
                sends.append(cp)
        for cp in sends:
            cp.wait()

    return pl.pallas_call(
        body, name=name,
        in_specs=[pl.BlockSpec(memory_space=pl.ANY)] * n,
        out_specs=[pl.BlockSpec(memory_space=pl.ANY)] * n,
        out_shape=[jax.ShapeDtypeStruct((n_chips,) + x.shape[2:], x.dtype) for x in arrays],
        scratch_shapes=[pltpu.SemaphoreType.DMA((n, n_chips)),
                        pltpu.SemaphoreType.DMA((n, n_chips))],
    )(*arrays)


def _exchange_chips(arrays, name):
    n = len(arrays)

    def body(*refs):
        start, finish = _chip_exchange_phases(refs[:n], refs[n:2 * n], *refs[2 * n:])
        start()
        finish()

    return pl.pallas_call(
        body, name=name,
        in_specs=[pl.BlockSpec(memory_space=pl.ANY)] * n,
        out_specs=[pl.BlockSpec(memory_space=pl.ANY)] * n,
        out_shape=[jax.ShapeDtypeStruct(x.shape, x.dtype) for x in arrays],
        scratch_shapes=_chip_exchange_sems(n),
    )(*arrays)


def _chip_exchange_sems(n):
    n_chips = N_DEV // 2
    return [pltpu.SemaphoreType.DMA((n, n_chips - 1)), pltpu.SemaphoreType.DMA((n, n_chips - 1)),
            pltpu.SemaphoreType.DMA((n,))]


def _chip_exchange_phases(x_refs, o_refs, send_sems, recv_sems, local_sems):
    n = len(x_refs)
    n_chips = N_DEV // 2
    ix, iy, ic = _mesh_pos()
    my_chip = 2 * ix + iy

    def peers():
        for mask in range(1, n_chips):
            px = 1 - ix if mask & 2 else ix
            py = 1 - iy if mask & 1 else iy
            yield mask, (px, py, ic), 2 * px + py

    def local():
        return [pltpu.make_async_copy(x_refs[a].at[my_chip], o_refs[a].at[my_chip],
                                      local_sems.at[a]) for a in range(n)]

    def sends():
        return [pltpu.make_async_remote_copy(
            src_ref=x_refs[a].at[chip], dst_ref=o_refs[a].at[my_chip],
            send_sem=send_sems.at[a, mask - 1], recv_sem=recv_sems.at[a, mask - 1],
            device_id=peer, device_id_type=pl.DeviceIdType.MESH)
            for mask, peer, chip in peers() for a in range(n)]

    def start():
        for cp in local() + sends():
            cp.start()

    def finish():
        for mask, peer, chip in peers():
            for a in range(n):
                pltpu.make_async_remote_copy(
                    src_ref=x_refs[a].at[my_chip], dst_ref=o_refs[a].at[chip],
                    send_sem=send_sems.at[a, mask - 1], recv_sem=recv_sems.at[a, mask - 1],
                    device_id=peer, device_id_type=pl.DeviceIdType.MESH).wait_recv()
        for cp in sends():
            cp.wait_send()
        for cp in local():
            cp.wait()

    return start, finish


def _pair_sum(core, x, got, name):
    nq, rows, cols = got.shape

    def body(c_ref, x_ref, g_ref, o_ref):
        o_ref[...] = (x_ref[...] + g_ref[...]).astype(BF16)

    blk = pl.BlockSpec((None, rows, cols), lambda q, c: (q, 0, 0))
    return pl.pallas_call(
        body, name=name,
        grid_spec=pltpu.PrefetchScalarGridSpec(
            num_scalar_prefetch=1, grid=(nq,),
            in_specs=[pl.BlockSpec((None, None, rows, cols), lambda q, c: (q, c[0], 0, 0)), blk],
            out_specs=blk),
        out_shape=jax.ShapeDtypeStruct(got.shape, BF16),
        compiler_params=_params(),
    )(core, x, got)


def _selectors():
    r = lax.broadcasted_iota(jnp.int32, (LANES, FOX_PAD), 0)
    c = lax.broadcasted_iota(jnp.int32, (LANES, FOX_PAD), 1)
    part, head_r = r // HEADS, r % HEADS
    head_c, lane_c = c // HEAD_PAD, c % HEAD_PAD
    same = (head_r == head_c) & (part < 3)
    sel_q = jnp.where(same & (lane_c == LANE_RB + part), 1.0, 0.0)
    sel_k = jnp.where(same & (lane_c == LANE_CK + part), -1.0, 0.0)
    sel = jnp.stack([sel_q, sel_k, jnp.zeros_like(sel_q)]).astype(BF16)
    lane = lax.broadcasted_iota(jnp.int32, (1, FOX_PAD), 1) % HEAD_PAD
    ones_q = jnp.where((lane >= LANE_CK) & (lane < LANE_CK + 3), 1.0, 0.0)
    ones_k = jnp.where(((lane >= LANE_RB) & (lane < LANE_RB + 3))
                       | ((lane >= LANE_LSE) & (lane < LANE_LSE + 3)), 1.0, 0.0)
    ones_v = jnp.where((lane >= LANE_ONE_V) & (lane < LANE_ONE_V + 2), 1.0, 0.0)
    bias = jnp.stack([ones_q, ones_k, ones_v]).astype(F32)
    return sel, bias


def _chip_sums(names, send):
    send = [a.reshape((N_DEV // 2, 2) + a.shape[1:]) for a in send]
    got = _swap_sibling(send, "swap_" + names[0])
    core = lax.axis_index("c").astype(jnp.int32).reshape(1)
    return [_pair_sum(core, a, b, "pair_sum_" + n) for n, a, b in zip(names, send, got)]


def _local_step(x, target, norm_g, final_g, w_in8, conv_w, conv_b, wa, ba, wx, bx, a_param,
                w_out_b, fox_in_shard, b_f, fox_out_shard, blk=512, ts=256):
    g0, g1 = norm_g[0:1], norm_g[1:2]
    gf = final_g.reshape(1, D_MODEL)
    wa_b, wx_b = wa.astype(BF16), wx.astype(BF16)
    sel, bias = _selectors()

    tm = min(2 * ts, x.shape[0])
    xb, gate1, h0, (fox_in8, fox_out8) = _lru_in_fwd(x, g0, w_in8, [fox_in_shard, fox_out_shard],
                                                     tm)
    fox_w_in = jnp.transpose(fox_in8, (1, 0, 2)).reshape(D_MODEL, FOX_IN_COLS)
    width = HEADS * HEAD_DIM
    wf_b = jnp.pad(fox_w_in[:, 4 * width:], ((0, 0), (0, LANES - HEADS)))
    bf_pad = jnp.pad(b_f, ((0, 0), (0, LANES - HEADS)))
    fo_b = fox_out8.reshape(width, D_MODEL)
    y1, hs = _lru_core_fwd(xb, gate1, conv_w, conv_b, wa_b, ba, wx_b, bx, a_param, ts)
    x1, h1, f, cparts = _fox_pre_fwd(x, y1, w_out_b, g1, wf_b, bf_pad, ts)
    qkv = _fox_proj_fwd(h1, cparts, fox_w_in, 0, 3, sel, bias, BF16, tm, "fox_proj_qkv")
    gate2 = _fox_proj_fwd(h1, None, fox_w_in, 3, 1, None, None, F32, tm, "fox_proj_gate")[0]
    o, qb = _attn_fwd(qkv, blk, hps=4)
    dx2, dx2b, y2, loss_acc, g_final = _fox_out_loss(o, gate2, fo_b, x1, target, gf, tm)

    do, dgate2 = _fox_out_bwd(dx2b, fo_b, o, gate2, ts)
    dq, dk, dv, dcum = _attn_bwd(qb, qkv, do, blk)
    dx1, dx1b, df, du_q, du_k, du_v, du_g, g_norm1, g_bf = _fox_in_bwd(
        dq, dk, dv, dgate2, fox_w_in, wf_b, dcum, f, x1, dx2, g1, ts)
    tw = min(1024, x.shape[0])
    g_q = _weight_grad(h1, du_q, tw, "grad_fox_wq", scale=QK_SCALE)
    g_k = _weight_grad(h1, du_k, tw, "grad_fox_wk")
    g_v = _weight_grad(h1, du_v, tw, "grad_fox_wv")
    g_g = _weight_grad(h1, du_g, tw, "grad_fox_wg")
    g_f = _weight_grad(h1, df, tw, "grad_fox_wf")
    g_fox_w_in = jnp.concatenate([g_q, g_k, g_v, g_g, g_f[:, :HEADS]], axis=1)
    g_fox_w_in = jnp.transpose(g_fox_w_in.reshape(D_MODEL, N_DEV, FOX_IN_SHARD), (1, 0, 2))
    g_fox_w_out = _weight_grad(y2, dx2b, tw, "grad_fox_w_out")
    fox_sums = _chip_sums(("fox_w_in", "fox_w_out"),
                          [g_fox_w_in, g_fox_w_out.reshape(N_DEV, -1, D_MODEL)])

    du, g_wa, g_wx, g_vec, (r_fox_in, r_fox_out) = _lru_core_bwd(
        dx1b, w_out_b, xb, gate1, hs, conv_w, conv_b, wa_b, ba, wx_b, bx, a_param,
        jnp.transpose(wa_b, (0, 2, 1)), jnp.transpose(wx_b, (0, 2, 1)), fox_sums, ts)
    g_lru_w_in = _weight_grad(h0, du, tw, "grad_lru_w_in", col_shards=N_DEV)
    g_lru_w_out = _weight_grad(y1, dx1b, tw, "grad_lru_w_out")
    conv_send = jnp.transpose(g_vec[0:CONV_WIDTH].reshape(CONV_WIDTH, N_DEV, -1), (1, 0, 2))
    small = dict(
        norm_g=jnp.concatenate([jnp.zeros_like(g_norm1), g_norm1], axis=0), final_g=g_final[0],
        lru_conv_b=g_vec[4:5], lru_wa=g_wa, lru_ba=g_vec[5:6], lru_wx=g_wx, lru_bx=g_vec[6:7],
        lru_a_param=g_vec[7:8], fox_b_f=g_bf[:, :HEADS])
    lru_sums = _chip_sums(("lru_w_in", "lru_conv_w", "lru_w_out", "small"),
                          [g_lru_w_in, conv_send, g_lru_w_out.reshape(N_DEV, -1, D_MODEL),
                           _pack_small(small).reshape(N_DEV, SMALL_CHUNK_ROWS, LANES)])
    grad_x, g_norm0, (r_w_in, r_conv, r_w_out, r_small) = _lru_in_bwd(du, w_in8, x, dx1, g0,
                                                                      lru_sums, tm)
    received = dict(lru_w_in=r_w_in, lru_conv_w=r_conv, lru_w_out=r_w_out, fox_w_in=r_fox_in,
                    fox_w_out=r_fox_out, small=r_small)
    return loss_acc[0, 0], grad_x, g_norm0, received


SMALL =("norm_g", "final_g", "lru_conv_b", "lru_wa", "lru_ba", "lru_wx", "lru_bx", "lru_a_param",
         "fox_b_f")
ALL_WEIGHTS = ("norm_g", "final_g", "lru_w_in", "lru_conv_w", "lru_conv_b", "lru_wa", "lru_ba",
               "lru_wx", "lru_bx", "lru_a_param", "lru_w_out", "fox_w_in", "fox_b_f", "fox_w_out")


def _pack_small(d):
    rows = []
    for n in SMALL:
        a = d[n].reshape(-1)
        if a.shape[0] % LANES:
            a = jnp.pad(a, (0, LANES - a.shape[0] % LANES))
        rows.append(a.reshape(-1, LANES))
    packed = jnp.concatenate(rows, axis=0)
    return jnp.pad(packed, ((0, N_DEV * SMALL_CHUNK_ROWS - packed.shape[0]), (0, 0)))


def _unpack_small(packed, like):
    out, off = {}, 0
    for n, nrows in zip(SMALL, SMALL_ROWS):
        size = like[n].size
        out[n] = packed[off:off + nrows].reshape(-1)[:size].reshape(like[n].shape)
        off += nrows
    return out


def kernel(x, norm_g, final_g, lru_w_in, lru_conv_w, lru_conv_b, lru_wa, lru_ba, lru_wx, lru_bx, lru_a_param, lru_w_out, fox_w_in, fox_b_f, fox_w_out, loss_target, m_norm_g, m_final_g, m_lru_w_in, m_lru_conv_w, m_lru_conv_b, m_lru_wa, m_lru_ba, m_lru_wx, m_lru_bx, m_lru_a_param, m_lru_w_out, m_fox_w_in, m_fox_b_f, m_fox_w_out, v_norm_g, v_final_g, v_lru_w_in, v_lru_conv_w, v_lru_conv_b, v_lru_wa, v_lru_ba, v_lru_wx, v_lru_bx, v_lru_a_param, v_lru_w_out, v_fox_w_in, v_fox_b_f, v_fox_w_out):
    w_loc = dict(norm_g=norm_g, final_g=final_g, lru_w_in=lru_w_in, lru_conv_w=lru_conv_w,
                 lru_conv_b=lru_conv_b, lru_wa=lru_wa, lru_ba=lru_ba, lru_wx=lru_wx, lru_bx=lru_bx,
                 lru_a_param=lru_a_param, lru_w_out=lru_w_out, fox_w_in=fox_w_in, fox_b_f=fox_b_f,
                 fox_w_out=fox_w_out)
    m_loc = dict(norm_g=m_norm_g, final_g=m_final_g, lru_w_in=m_lru_w_in, lru_conv_w=m_lru_conv_w,
                 lru_conv_b=m_lru_conv_b, lru_wa=m_lru_wa, lru_ba=m_lru_ba, lru_wx=m_lru_wx,
                 lru_bx=m_lru_bx, lru_a_param=m_lru_a_param, lru_w_out=m_lru_w_out,
                 fox_w_in=m_fox_w_in, fox_b_f=m_fox_b_f, fox_w_out=m_fox_w_out)
    v_loc = dict(norm_g=v_norm_g, final_g=v_final_g, lru_w_in=v_lru_w_in, lru_conv_w=v_lru_conv_w,
                 lru_conv_b=v_lru_conv_b, lru_wa=v_lru_wa, lru_ba=v_lru_ba, lru_wx=v_lru_wx,
                 lru_bx=v_lru_bx, lru_a_param=v_lru_a_param, lru_w_out=v_lru_w_out,
                 fox_w_in=v_fox_w_in, fox_b_f=v_fox_b_f, fox_w_out=v_fox_w_out)

    w_in8, conv8, w_out8 = _gather_two_level(
        [lru_w_in[0].astype(BF16), lru_conv_w[0], lru_w_out[0].astype(BF16)], "gather_weights")
    conv_full = jnp.transpose(conv8, (1, 0, 2)).reshape(CONV_WIDTH, LRU_WIDTH)

    loss, grad_x, g_norm0, received = _local_step(
        x[0], loss_target[0], norm_g, final_g, w_in8, conv_full, lru_conv_b, lru_wa[0], lru_ba,
        lru_wx[0], lru_bx, lru_a_param, w_out8.reshape(LRU_WIDTH, D_MODEL),
        fox_w_in[0].astype(BF16), fox_b_f, fox_w_out[0].astype(BF16))

    out = {}
    for n, tr in (("lru_w_in", 256), ("lru_conv_w", CONV_WIDTH), ("lru_w_out", 96),
                  ("fox_w_in", 128), ("fox_w_out", 64)):
        res = _adamw(received[n], w_loc[n][0], m_loc[n][0], v_loc[n][0], tr, "adamw_" + n)
        out[n] = [a[None] for a in res]

    g_chunk = _reduce_parts(received["small"], "reduce_small_grads")
    g_small, g_norm0_all = _exchange([g_chunk, g_norm0.reshape(SUBLANES, LANES)], False,
                                     "gather_small_grads")
    g_norm0_sum = _reduce_parts(g_norm0_all, "reduce_norm0_grads")
    g_small = g_small.reshape(N_DEV * SMALL_CHUNK_ROWS, LANES).at[0:SUBLANES].set(g_norm0_sum)
    g_small = g_small[None]
    res = _adamw(g_small, _pack_small(w_loc), _pack_small(m_loc), _pack_small(v_loc),
                 N_DEV * SMALL_CHUNK_ROWS, "adamw_replicated")
    small_out = [_unpack_small(a, w_loc) for a in res]
    for n in SMALL:
        out[n] = [d[n] for d in small_out]

    loss = lax.psum(loss, ("x", "y", "c"))
    return (loss, grad_x[None], *[out[n][0] for n in ALL_WEIGHTS], *[out[n][1] for n in ALL_WEIGHTS],
            *[out[n][2] for n in ALL_WEIGHTS], *[out[n][3] for n in ALL_WEIGHTS])
```

```python
import functools

import jax
import jax.numpy as jnp
from jax import lax
from jax.experimental import pallas as pl
from jax.experimental.pallas import tpu as pltpu

F32 = jnp.float32
BF16 = jnp.bfloat16

D_MODEL = 1024
LRU_WIDTH = 1536
LRU_BLOCKS = 12
LRU_BLOCK_W = 128
CONV_WIDTH = 4
LRU_C = 8.0
HEADS = 16
HEAD_DIM = 64
HEAD_PAD = 128
FOX_PAD = HEADS * HEAD_PAD
HEADS_PER_STEP = 2
QK_SCALE = 1.0 / HEAD_DIM ** 0.5
EPS = 1e-6
NEG_BIG = -1e30
N_DEV = 8

ADAM_LR = 0.001
ADAM_B1 = 0.9
ADAM_B2 = 0.999
ADAM_EPS = 1e-08
ADAM_WD = 0.01
ADAM_STEP = 10

LANE_RB = 64
LANE_CK = 67
LANE_LSE = 70
LANE_ONE_V = 64

VMEM_LIMIT_BYTES = 56 * 1024 * 1024
LANES = 128
SUBLANES = 8

LRU_IN_SHARD = 2 * LRU_WIDTH // N_DEV
FOX_IN_COLS = 4 * HEADS * HEAD_DIM + HEADS
FOX_IN_SHARD = FOX_IN_COLS // N_DEV

SMALL_ROWS = (16, 8, 12, 1536, 12, 1536, 12, 12, 1)
SMALL_CHUNK_ROWS = 400
assert sum(SMALL_ROWS) <= N_DEV * SMALL_CHUNK_ROWS


def _params(n_grid_axes=1):
    return pltpu.CompilerParams(
        dimension_semantics=("arbitrary",) * n_grid_axes,
        vmem_limit_bytes=VMEM_LIMIT_BYTES)


def _const_spec(shape):
    nd = len(shape)
    return pl.BlockSpec(shape, lambda *_: (0,) * nd, pipeline_mode=pl.Buffered(1))


def _shift_down(x, k, fill):
    rows = lax.broadcasted_iota(jnp.int32, x.shape, 0)
    return jnp.where(rows >= k, pltpu.roll(x, k, 0), fill)


def _shift_up(x, k, fill):
    n = x.shape[0]
    rows = lax.broadcasted_iota(jnp.int32, x.shape, 0)
    return jnp.where(rows < n - k, pltpu.roll(x, n - k, 0), fill)


def _scan_rows(a, b, reverse=False):
    n = a.shape[0]
    shift = _shift_up if reverse else _shift_down
    k = 1
    while k < n:
        b = a * shift(b, k, 0.0) + b
        a = a * shift(a, k, 1.0)
        k *= 2
    return a, b


def _cumsum_rows(x, reverse=False):
    n = x.shape[0]
    shift = _shift_up if reverse else _shift_down
    k = 1
    while k < n:
        x = x + shift(x, k, 0.0)
        k *= 2
    return x


def _rstd(x):
    return lax.rsqrt(jnp.mean(x * x, axis=-1, keepdims=True) + EPS)


def _norm_bwd(x, g, dh):
    rstd = _rstd(x)
    xhat = x * rstd
    dg = jnp.sum(dh * xhat, axis=0, keepdims=True)
    dxh = dh * g
    dx = rstd * (dxh - xhat * jnp.mean(dxh * xhat, axis=-1, keepdims=True))
    return dx, dg


def _split3(x):
    hi = x.astype(BF16)
    r1 = x - hi.astype(F32)
    mid = r1.astype(BF16)
    lo = (r1 - mid.astype(F32)).astype(BF16)
    return hi, mid, lo


def _sigmoid(x):
    return jax.nn.sigmoid(x)


def _dot(a, b):
    return jnp.dot(a, b, preferred_element_type=F32)


def _dot_nt(a, b):
    return lax.dot_general(a, b, (((1,), (1,)), ((), ())), preferred_element_type=F32)


def _dot_tn(a, b):
    return lax.dot_general(a, b, (((0,), (0,)), ((), ())), preferred_element_type=F32)


def _heads_to_padded(u):
    n = u.shape[0]
    low = lax.broadcasted_iota(jnp.int32, (n, LANES), 1) < HEAD_DIM
    zero = jnp.zeros((n, LANES), u.dtype)
    cols = []
    for p in range(HEADS // 2):
        pair = u[:, p * LANES:(p + 1) * LANES]
        cols.append(jnp.where(low, pair, zero))
        cols.append(jnp.where(low, pltpu.roll(pair, HEAD_DIM, 1), zero))
    return jnp.concatenate(cols, axis=1)


def _heads_from_padded(x):
    n = x.shape[0]
    low = lax.broadcasted_iota(jnp.int32, (n, LANES), 1) < HEAD_DIM
    cols = []
    for p in range(HEADS // 2):
        even = x[:, (2 * p) * HEAD_PAD:(2 * p + 1) * HEAD_PAD]
        odd = x[:, (2 * p + 1) * HEAD_PAD:(2 * p + 2) * HEAD_PAD]
        cols.append(jnp.where(low, even, pltpu.roll(odd, HEAD_DIM, 1)))
    return jnp.concatenate(cols, axis=1)


def _conv_taps(xb, prev8):
    rows8 = lax.broadcasted_iota(jnp.int32, prev8.shape, 0)
    taps = [xb]
    for j in range(1, CONV_WIDTH):
        r = pltpu.roll(xb, j, 0)
        p = pltpu.roll(prev8, j, 0)
        head = jnp.where(rows8 < j, p, r[0:SUBLANES])
        taps.append(jnp.concatenate([head, r[SUBLANES:]], axis=0))
    return taps


def _lru_pre(taps, cw, cb, wa_ref, ba, wx_ref, bx, a_param):
    xc = cb + cw[3:4] * taps[0] + cw[2:3] * taps[1] + cw[1:2] * taps[2] + cw[0:1] * taps[3]
    xcb = xc.astype(BF16)
    ra, ia = [], []
    for n in range(LRU_BLOCKS):
        blk = xcb[:, n * LRU_BLOCK_W:(n + 1) * LRU_BLOCK_W]
        ra.append(_dot(blk, wa_ref[n]))
        ia.append(_dot(blk, wx_ref[n]))
    r = _sigmoid(jnp.concatenate(ra, axis=1) + ba)
    i = _sigmoid(jnp.concatenate(ia, axis=1) + bx)
    z = -a_param
    sp = jnp.maximum(z, 0.0) + jnp.log1p(jnp.exp(-jnp.abs(z)))
    log_a = (-LRU_C) * r * sp
    a = jnp.exp(log_a)
    one_minus_a2 = -jnp.tanh(log_a) * (a * a + 1.0)
    mult = jnp.sqrt(one_minus_a2)
    return xc, xcb, r, i, sp, a, mult


def _lru_in_fwd(x, g0, w_in, later_shards, ts):
    s = x.shape[0]
    nt = s // ts
    n = len(later_shards)

    def body(*refs):
        x_ref, g_ref, w_ref = refs[:3]
        shard_refs = refs[3:3 + n]
        xb_ref, gate_ref, h_ref = refs[3 + n:6 + n]
        wfull_ref = refs[6 + 2 * n]
        start, forward, finish = _gather_phases(shard_refs, refs[6 + n:6 + 2 * n],
                                                *refs[7 + 2 * n:])
        step = pl.program_id(0)
        pl.when(step == 0)(start)

        @pl.when(step == 0)
        def _():
            for j in range(N_DEV):
                wfull_ref[:, j * LRU_IN_SHARD:(j + 1) * LRU_IN_SHARD] = w_ref[j]

        xv = x_ref[...]
        h = (xv * _rstd(xv) * g_ref[...]).astype(BF16)
        u = _dot(h, wfull_ref[...])
        xb_ref[...] = u[:, :LRU_WIDTH]
        gate_ref[...] = u[:, LRU_WIDTH:]
        h_ref[...] = h
        pl.when(step == (2 * nt) // 3)(forward)
        pl.when(step == nt - 1)(finish)

    hbm = pl.BlockSpec(memory_space=pl.ANY)
    res = pl.pallas_call(
        body, name="lru_in_fwd", grid=(nt,),
        in_specs=[pl.BlockSpec((ts, D_MODEL), lambda i: (i, 0)),
                  _const_spec((1, D_MODEL)),
                  _const_spec((N_DEV, D_MODEL, LRU_IN_SHARD))] + [hbm] * n,
        out_specs=[pl.BlockSpec((ts, LRU_WIDTH), lambda i: (i, 0)),
                   pl.BlockSpec((ts, LRU_WIDTH), lambda i: (i, 0)),
                   pl.BlockSpec((ts, D_MODEL), lambda i: (i, 0))] + [hbm] * n,
        out_shape=[jax.ShapeDtypeStruct((s, LRU_WIDTH), F32),
                   jax.ShapeDtypeStruct((s, LRU_WIDTH), F32),
                   jax.ShapeDtypeStruct((s, D_MODEL), BF16)]
        + [jax.ShapeDtypeStruct((N_DEV,) + a.shape, a.dtype) for a in later_shards],
        scratch_shapes=[pltpu.VMEM((D_MODEL, 2 * LRU_WIDTH), BF16)] + _gather_sems(n),
        compiler_params=_params(),
    )(x, g0, w_in, *later_shards)
    return res[0], res[1], res[2], res[3:]


def _lru_core_fwd(xb, gate, cw, cb, wa, ba, wx, bx, a_param, ts):
    s = xb.shape[0]

    def body(xb_ref, gate_ref, cw_ref, cb_ref, wa_ref, ba_ref, wx_ref, bx_ref, ap_ref,
             y_ref, hs_ref, prev_ref, hcar_ref):
        @pl.when(pl.program_id(0) == 0)
        def _():
            prev_ref[...] = jnp.zeros_like(prev_ref)
            hcar_ref[...] = jnp.zeros_like(hcar_ref)

        xbv = xb_ref[...]
        taps = _conv_taps(xbv, prev_ref[...])
        xc, _, _, i, _, a, mult = _lru_pre(taps, cw_ref[...], cb_ref[...], wa_ref, ba_ref[...],
                                           wx_ref, bx_ref[...], ap_ref[...])
        bterm = mult * (i * xc)
        cum_a, hloc = _scan_rows(a, bterm)
        hs = cum_a * hcar_ref[SUBLANES - 1:SUBLANES, :] + hloc
        gv = gate_ref[...]
        y_ref[...] = (hs * (gv * _sigmoid(gv))).astype(BF16)
        hs_ref[...] = hs
        prev_ref[...] = xbv[ts - SUBLANES:, :]
        hcar_ref[...] = hs[ts - SUBLANES:, :]

    vec = _const_spec((1, LRU_WIDTH))
    blk = _const_spec((LRU_BLOCKS, LRU_BLOCK_W, LRU_BLOCK_W))
    tile = pl.BlockSpec((ts, LRU_WIDTH), lambda i: (i, 0))
    return pl.pallas_call(
        body, name="lru_core_fwd", grid=(s // ts,),
        in_specs=[tile, tile, _const_spec((CONV_WIDTH, LRU_WIDTH)), vec, blk, vec, blk, vec, vec],
        out_specs=[tile, tile],
        out_shape=[jax.ShapeDtypeStruct((s, LRU_WIDTH), BF16),
                   jax.ShapeDtypeStruct((s, LRU_WIDTH), F32)],
        scratch_shapes=[pltpu.VMEM((SUBLANES, LRU_WIDTH), F32),
                        pltpu.VMEM((SUBLANES, LRU_WIDTH), F32)],
        compiler_params=_params(),
    )(xb, gate, cw, cb, wa, ba, wx, bx, a_param)


def _fox_pre_fwd(x, y, w_out, g1, wf, bf, ts):
    s = x.shape[0]

    def body(x_ref, y_ref, w_ref, g_ref, wf_ref, bf_ref, x1_ref, h1_ref, f_ref, cp_ref, ccar_ref):
        @pl.when(pl.program_id(0) == 0)
        def _():
            ccar_ref[...] = jnp.zeros_like(ccar_ref)

        x1 = x_ref[...] + _dot(y_ref[...], w_ref[...])
        h1 = (x1 * _rstd(x1) * g_ref[...]).astype(BF16)
        f = _dot(h1, wf_ref[...]) + bf_ref[...]
        logsig = jnp.minimum(f, 0.0) - jnp.log1p(jnp.exp(-jnp.abs(f)))
        cum = _cumsum_rows(logsig) + ccar_ref[SUBLANES - 1:SUBLANES, :]
        hi, mid, lo = _split3(cum)
        lane = lax.broadcasted_iota(jnp.int32, cum.shape, 1)
        packed = jnp.where(lane < HEADS, hi.astype(F32), jnp.where(
            lane < 2 * HEADS, pltpu.roll(mid.astype(F32), HEADS, 1), jnp.where(
                lane < 3 * HEADS, pltpu.roll(lo.astype(F32), 2 * HEADS, 1), 0.0)))
        x1_ref[...] = x1
        h1_ref[...] = h1
        f_ref[...] = f
        cp_ref[...] = packed.astype(BF16)
        ccar_ref[...] = cum[ts - SUBLANES:, :]

    return pl.pallas_call(
        body, name="fox_pre_fwd", grid=(s // ts,),
        in_specs=[pl.BlockSpec((ts, D_MODEL), lambda i: (i, 0)),
                  pl.BlockSpec((ts, LRU_WIDTH), lambda i: (i, 0)),
                  _const_spec((LRU_WIDTH, D_MODEL)),
                  _const_spec((1, D_MODEL)),
                  _const_spec((D_MODEL, LANES)),
                  _const_spec((1, LANES))],
        out_specs=[pl.BlockSpec((ts, D_MODEL), lambda i: (i, 0)),
                   pl.BlockSpec((ts, D_MODEL), lambda i: (i, 0)),
                   pl.BlockSpec((ts, LANES), lambda i: (i, 0)),
                   pl.BlockSpec((ts, LANES), lambda i: (i, 0))],
        out_shape=[jax.ShapeDtypeStruct((s, D_MODEL), F32),
                   jax.ShapeDtypeStruct((s, D_MODEL), BF16),
                   jax.ShapeDtypeStruct((s, LANES), F32),
                   jax.ShapeDtypeStruct((s, LANES), BF16)],
        scratch_shapes=[pltpu.VMEM((SUBLANES, LANES), F32)],
        compiler_params=_params(),
    )(x, y, w_out, g1, wf, bf)


def _fox_proj_fwd(h1, cparts, w, first, ng, sel, bias, out_dtype, ts, name):
    s = h1.shape[0]
    width = HEADS * HEAD_DIM
    use_sel = sel is not None

    def body(*refs):
        if use_sel:
            h_ref, cp_ref, w_ref, sel_ref, b_ref, o_ref = refs
            proj = _dot(h_ref[...], w_ref[...])
            if first == 0:
                proj = proj * jnp.where(pl.program_id(0) == 0, QK_SCALE, 1.0)
            acc = _heads_to_padded(proj) + _dot(cp_ref[...], sel_ref[...]) + b_ref[...]
        else:
            h_ref, w_ref, o_ref = refs
            acc = _heads_to_padded(_dot(h_ref[...], w_ref[...]))
        o_ref[...] = acc.astype(out_dtype)

    in_specs = [pl.BlockSpec((ts, D_MODEL), lambda j, i: (i, 0))]
    args = [h1]
    if use_sel:
        in_specs.append(pl.BlockSpec((ts, LANES), lambda j, i: (i, 0)))
        args.append(cparts)
    in_specs.append(pl.BlockSpec((D_MODEL, width), lambda j, i: (0, first + j)))
    args.append(w)
    if use_sel:
        in_specs.append(pl.BlockSpec((None, LANES, FOX_PAD), lambda j, i: (j, 0, 0)))
        in_specs.append(pl.BlockSpec((None, 1, FOX_PAD), lambda j, i: (j, 0, 0)))
        args += [sel, bias]
    return pl.pallas_call(
        body, name=name, grid=(ng, s // ts),
        in_specs=in_specs,
        out_specs=pl.BlockSpec((None, ts, FOX_PAD), lambda j, i: (j, i, 0)),
        out_shape=jax.ShapeDtypeStruct((ng, s, FOX_PAD), out_dtype),
        compiler_params=_params(2),
    )(*args)


def _attn_fwd(qkv, blk, hps=HEADS_PER_STEP):
    s = qkv.shape[1]
    nblk = s // blk
    wide = 2 * blk
    heads = [slice(i * HEAD_PAD, (i + 1) * HEAD_PAD) for i in range(hps)]

    def body(q_ref, k_ref, v_ref, o_ref, qb_ref, acc_ref, m_ref):
        qi = pl.program_id(1)
        row = lax.broadcasted_iota(jnp.int32, (blk, blk), 0)
        col = lax.broadcasted_iota(jnp.int32, (blk, blk), 1)
        lane = lax.broadcasted_iota(jnp.int32, (blk, HEAD_PAD), 1)
        qs = [q_ref[:, hd] for hd in heads]
        for i in range(hps):
            acc_ref[i] = jnp.zeros((blk, HEAD_PAD), F32)
            m_ref[i] = jnp.full((blk, HEAD_PAD), NEG_BIG, F32)

        def step(k0, size, masked):
            scores = [_dot_nt(q, k_ref[pl.ds(k0, size), hd]) for q, hd in zip(qs, heads)]
            for i, (sc, hd) in enumerate(zip(scores, heads)):
                v = v_ref[pl.ds(k0, size), hd]
                if masked:
                    sc = jnp.where(col <= row, sc, NEG_BIG)
                m = m_ref[i]
                m_new = jnp.maximum(m, jnp.max(sc, axis=-1, keepdims=True))
                p = jnp.exp((sc - jnp.tile(m_new, (1, size // HEAD_PAD))).astype(BF16))
                acc_ref[i] = jnp.exp(m - m_new) * acc_ref[i] + _dot(p, v)
                m_ref[i] = m_new

        def wide_step(kk, _):
            step(pl.multiple_of(kk * wide, wide), wide, False)
            return 0

        lax.fori_loop(0, qi // 2, wide_step, 0)

        @pl.when(qi % 2 == 1)
        def _():
            step(pl.multiple_of((qi - 1) * blk, blk), blk, False)

        step(pl.multiple_of(qi * blk, blk), blk, True)
        for i, (q, hd) in enumerate(zip(qs, heads)):
            acc = acc_ref[i]
            l = jnp.broadcast_to(acc[:, LANE_ONE_V:LANE_ONE_V + 1], (blk, HEAD_PAD))
            o_ref[:, hd] = (acc / l).astype(BF16)
            hi, mid, lo = _split3(-(m_ref[i] + jnp.log(l)))
            qb_ref[:, hd] = jnp.where(lane == LANE_LSE, hi, jnp.where(
                lane == LANE_LSE + 1, mid, jnp.where(lane == LANE_LSE + 2, lo, q)))

    width = hps * HEAD_PAD

    def whole(j):
        return pl.BlockSpec((None, s, width), lambda h, i: (j, 0, h))

    out_spec = pl.BlockSpec((blk, width), lambda h, i: (i, h))
    return pl.pallas_call(
        body, name="attn_fwd", grid=(HEADS // hps, nblk),
        in_specs=[pl.BlockSpec((None, blk, width), lambda h, i: (0, i, h)), whole(1), whole(2)],
        out_specs=[out_spec, out_spec],
        out_shape=[jax.ShapeDtypeStruct((s, FOX_PAD), BF16),
                   jax.ShapeDtypeStruct((s, FOX_PAD), BF16)],
        scratch_shapes=[pltpu.VMEM((hps, blk, HEAD_PAD), F32),
                        pltpu.VMEM((hps, blk, HEAD_PAD), F32)],
        compiler_params=_params(2),
    )(qkv, qkv, qkv)


def _fox_out_loss(o, gate, w_out, x1, target, gf, ts):
    s = x1.shape[0]

    def body(o_ref, gt_ref, w_ref, x1_ref, t_ref, g_ref, dx2_ref, dx2b_ref, y2_ref, loss_ref,
             gfin_ref):
        @pl.when(pl.program_id(0) == 0)
        def _():
            loss_ref[...] = jnp.zeros_like(loss_ref)
            gfin_ref[...] = jnp.zeros_like(gfin_ref)

        gv = gt_ref[...]
        y2 = _heads_from_padded(o_ref[...] * (gv * _sigmoid(gv))).astype(BF16)
        x2 = x1_ref[...] + _dot(y2, w_ref[...])
        rstd = _rstd(x2)
        xhat = x2 * rstd
        g = g_ref[...]
        diff = xhat * g - t_ref[...]
        loss_ref[...] += 0.5 * jnp.sum(jnp.mean(diff * diff, axis=-1, keepdims=True))
        dy = diff * (1.0 / D_MODEL)
        gfin_ref[...] += jnp.sum(dy * xhat, axis=0, keepdims=True)
        dxh = dy * g
        dx2 = rstd * (dxh - xhat * jnp.mean(dxh * xhat, axis=-1, keepdims=True))
        dx2_ref[...] = dx2
        dx2b_ref[...] = dx2.astype(BF16)
        y2_ref[...] = y2

    return pl.pallas_call(
        body, name="fox_out_loss", grid=(s // ts,),
        in_specs=[pl.BlockSpec((ts, FOX_PAD), lambda i: (i, 0)),
                  pl.BlockSpec((ts, FOX_PAD), lambda i: (i, 0)),
                  _const_spec((HEADS * HEAD_DIM, D_MODEL)),
                  pl.BlockSpec((ts, D_MODEL), lambda i: (i, 0)),
                  pl.BlockSpec((ts, D_MODEL), lambda i: (i, 0)),
                  _const_spec((1, D_MODEL))],
        out_specs=[pl.BlockSpec((ts, D_MODEL), lambda i: (i, 0)),
                   pl.BlockSpec((ts, D_MODEL), lambda i: (i, 0)),
                   pl.BlockSpec((ts, HEADS * HEAD_DIM), lambda i: (i, 0)),
                   pl.BlockSpec((SUBLANES, LANES), lambda i: (0, 0)),
                   pl.BlockSpec((1, D_MODEL), lambda i: (0, 0))],
        out_shape=[jax.ShapeDtypeStruct((s, D_MODEL), F32),
                   jax.ShapeDtypeStruct((s, D_MODEL), BF16),
                   jax.ShapeDtypeStruct((s, HEADS * HEAD_DIM), BF16),
                   jax.ShapeDtypeStruct((SUBLANES, LANES), F32),
                   jax.ShapeDtypeStruct((1, D_MODEL), F32)],
        compiler_params=_params(),
    )(o, gate, w_out, x1, target, gf)


def _fox_out_bwd(dx2, w_out, o, gate, ts):
    s = dx2.shape[0]

    def body(dx_ref, w_ref, o_ref, gt_ref, do_ref, dg_ref):
        lane = lax.broadcasted_iota(jnp.int32, (ts, HEAD_PAD), 1)
        dy2 = _heads_to_padded(_dot_nt(dx_ref[...], w_ref[...]))
        gv = gt_ref[...]
        sg = _sigmoid(gv)
        ov = o_ref[...]
        dov = dy2 * (gv * sg)
        dg_ref[...] = (dy2 * ov * (sg * (1.0 + gv * (1.0 - sg)))).astype(BF16)
        prod = dov * ov
        for h in range(HEADS):
            sl = slice(h * HEAD_PAD, (h + 1) * HEAD_PAD)
            delta = jnp.sum(prod[:, sl], axis=-1, keepdims=True)
            hi = delta.astype(BF16)
            lo = (delta - hi.astype(F32)).astype(BF16)
            do_h = dov[:, sl].astype(BF16)
            do_ref[:, sl] = jnp.where(lane == LANE_ONE_V, -hi,
                                      jnp.where(lane == LANE_ONE_V + 1, -lo, do_h))

    tile = pl.BlockSpec((ts, FOX_PAD), lambda i: (i, 0))
    return pl.pallas_call(
        body, name="fox_out_bwd", grid=(s // ts,),
        in_specs=[pl.BlockSpec((ts, D_MODEL), lambda i: (i, 0)),
                  _const_spec((HEADS * HEAD_DIM, D_MODEL)), tile, tile],
        out_specs=[tile, tile],
        out_shape=[jax.ShapeDtypeStruct((s, FOX_PAD), BF16),
                   jax.ShapeDtypeStruct((s, FOX_PAD), BF16)],
        compiler_params=_params(),
    )(dx2, w_out, o, gate)


def _attn_bwd(qb, qkv, do, blk):
    s = qb.shape[0]
    nblk = s // blk
    half = blk // 2
    heads = [slice(i * HEAD_PAD, (i + 1) * HEAD_PAD) for i in range(HEADS_PER_STEP)]

    def body(q_ref, k_ref, v_ref, do_ref, dq_ref, dk_ref, dv_ref, dcum_ref, dq_acc, dkt_acc,
             dvt_acc, qt_ref, dot_ref):
        group = pl.program_id(0)
        kj = pl.program_id(1)
        row = lax.broadcasted_iota(jnp.int32, (blk, blk), 0)
        col = lax.broadcasted_iota(jnp.int32, (blk, blk), 1)
        lane = lax.broadcasted_iota(jnp.int32, (blk, LANES), 1)
        mine = [lane == group * HEADS_PER_STEP + i for i in range(HEADS_PER_STEP)]

        @pl.when(kj == 0)
        def _():
            dq_acc[...] = jnp.zeros_like(dq_acc)

            def transpose_block(bi, _):
                r0 = pl.multiple_of(bi * blk, blk)
                for i, hd in enumerate(heads):
                    qt_ref[i, bi] = q_ref[pl.ds(r0, blk), hd].T
                    dot_ref[i, bi] = do_ref[pl.ds(r0, blk), hd].T
                return 0

            lax.fori_loop(0, nblk, transpose_block, 0)

        @pl.when((group == 0) & (kj == 0))
        def _():
            dcum_ref[...] = jnp.zeros_like(dcum_ref)

        k0 = pl.multiple_of(kj * blk, blk)
        ks = [k_ref[:, hd] for hd in heads]
        vs = [v_ref[:, hd] for hd in heads]

        def step(qi, q_lo, nq, k_lo, nk, masked):
            parts = ([(0, q_lo, 0, nq)] if nq <= blk
                     else [(b, 0, b * blk, blk) for b in range(nq // blk)])
            q0 = pl.multiple_of(qi * blk + q_lo, half)
            qs = [q_ref[pl.ds(q0, nq), hd] for hd in heads]
            dos = [do_ref[pl.ds(q0, nq), hd] for hd in heads]
            kk = [k[k_lo:k_lo + nk] for k in ks]
            vv = [v[k_lo:k_lo + nk] for v in vs]
            scores = [_dot_nt(q, k) for q, k in zip(qs, kk)]
            dps = [_dot_nt(dov, v) for dov, v in zip(dos, vv)]
            for i, (hd, k, sc, dp) in enumerate(zip(heads, kk, scores, dps)):
                p = jnp.exp(sc.astype(BF16))
                if masked:
                    p = jnp.where(col[:nq, :nk] + k_lo <= row[:nq, :nk] + q_lo, p,
                                  jnp.zeros_like(p))
                ds = (p.astype(F32) * dp).astype(BF16)
                dvt = sum(_dot(dot_ref[i, qi + b, :, c:c + n], p[r:r + n]) for b, c, r, n in parts)
                dkt = sum(_dot(qt_ref[i, qi + b, :, c:c + n], ds[r:r + n]) for b, c, r, n in parts)
                if masked:
                    dvt_acc[i, :, k_lo:k_lo + nk] = dvt
                    dkt_acc[i, :, k_lo:k_lo + nk] = dkt
                else:
                    dvt_acc[i, :, k_lo:k_lo + nk] += dvt
                    dkt_acc[i, :, k_lo:k_lo + nk] += dkt
                dq_acc[pl.ds(q0, nq), hd] += _dot(ds, k)

        step(kj, 0, blk, 0, half, True)
        step(kj, half, half, half, half, True)

        n_after = nblk - 1 - kj

        def q_step(t, _):
            step(kj + 1 + 2 * t, 0, 2 * blk, 0, blk, False)
            return 0

        lax.fori_loop(0, n_after // 2, q_step, 0)

        @pl.when(n_after % 2 == 1)
        def _():
            step(nblk - 1, 0, blk, 0, blk, False)
        dcum = dcum_ref[pl.ds(k0, blk), :]
        for i, (hd, mask) in enumerate(zip(heads, mine)):
            dk = dkt_acc[i].T
            dk_ref[:, hd] = dk.astype(BF16)
            dv_ref[:, hd] = dvt_acc[i].astype(BF16).T
            dcum = jnp.where(mask, -dk[:, LANE_CK:LANE_CK + 1], dcum)
        dcum_ref[pl.ds(k0, blk), :] = dcum

        @pl.when(kj == nblk - 1)
        def _():
            def finish(bi, _):
                r0 = pl.multiple_of(bi * blk, blk)
                dcum = dcum_ref[pl.ds(r0, blk), :]
                for hd, mask in zip(heads, mine):
                    dq = dq_acc[pl.ds(r0, blk), hd]
                    dq_ref[pl.ds(r0, blk), hd] = dq.astype(BF16)
                    dcum = dcum + jnp.where(mask, dq[:, LANE_RB:LANE_RB + 1], 0.0)
                dcum_ref[pl.ds(r0, blk), :] = dcum
                return 0

            lax.fori_loop(0, nblk, finish, 0)

    width = HEADS_PER_STEP * HEAD_PAD
    whole = pl.BlockSpec((s, width), lambda h, j: (0, h))
    part = pl.BlockSpec((blk, width), lambda h, j: (j, h))
    out = jax.ShapeDtypeStruct((s, FOX_PAD), BF16)
    return pl.pallas_call(
        body, name="attn_bwd", grid=(HEADS // HEADS_PER_STEP, nblk),
        in_specs=[whole,
                  pl.BlockSpec((None, blk, width), lambda h, j: (1, j, h)),
                  pl.BlockSpec((None, blk, width), lambda h, j: (2, j, h)),
                  whole],
        out_specs=[whole, part, part, pl.BlockSpec((s, LANES), lambda h, j: (0, 0))],
        out_shape=[out, out, out, jax.ShapeDtypeStruct((s, LANES), F32)],
        scratch_shapes=[pltpu.VMEM((s, width), F32),
                        pltpu.VMEM((HEADS_PER_STEP, HEAD_PAD, blk), F32),
                        pltpu.VMEM((HEADS_PER_STEP, HEAD_PAD, blk), F32),
                        pltpu.VMEM((HEADS_PER_STEP, nblk, HEAD_PAD, blk), BF16),
                        pltpu.VMEM((HEADS_PER_STEP, nblk, HEAD_PAD, blk), BF16)],
        compiler_params=_params(2),
    )(qb, qkv, qkv, do)


def _fox_in_bwd(dq, dk, dv, dg, wt, wft, dcum, f, x1, dx2, g1, ts):
    s = x1.shape[0]
    nt = s // ts
    width = HEADS * HEAD_DIM

    def body(dq_ref, dk_ref, dv_ref, dg_ref, wt_ref, wft_ref, dcum_ref, f_ref, x1_ref, dx2_ref,
             g_ref, dx1_ref, dx1b_ref, df_ref, duq_ref, duk_ref, duv_ref, dug_ref, gn_ref, gbf_ref,
             rcar_ref):
        du_refs = (duq_ref, duk_ref, duv_ref, dug_ref)


        @pl.when(pl.program_id(0) == 0)
        def _():
            rcar_ref[...] = jnp.zeros_like(rcar_ref)
            gn_ref[...] = jnp.zeros_like(gn_ref)
            gbf_ref[...] = jnp.zeros_like(gbf_ref)

        rsum = _cumsum_rows(dcum_ref[...], reverse=True) + rcar_ref[0:1, :]
        df = rsum * _sigmoid(-f_ref[...])
        dfb = df.astype(BF16)
        dh = _dot_nt(dfb, wft_ref[...])
        for j, ref in enumerate((dq_ref, dk_ref, dv_ref, dg_ref)):
            du = _heads_from_padded(ref[...])
            du_refs[j][...] = du
            if j == 0:
                du = du * QK_SCALE
            dh = dh + _dot_nt(du, wt_ref[:, j * width:(j + 1) * width])
        dxn, dgn = _norm_bwd(x1_ref[...], g_ref[...], dh)
        dx1 = dx2_ref[...] + dxn
        dx1_ref[...] = dx1
        dx1b_ref[...] = dx1.astype(BF16)
        df_ref[...] = dfb
        gn_ref[...] += dgn
        gbf_ref[...] += jnp.sum(df, axis=0, keepdims=True)
        rcar_ref[...] = rsum[0:SUBLANES, :]

    rev = lambda i: (nt - 1 - i, 0)
    wide = pl.BlockSpec((ts, FOX_PAD), rev)
    return pl.pallas_call(
        body, name="fox_in_bwd", grid=(nt,),
        in_specs=[wide, wide, wide, wide,
                  _const_spec((D_MODEL, FOX_IN_COLS)),
                  _const_spec((D_MODEL, LANES)),
                  pl.BlockSpec((ts, LANES), rev),
                  pl.BlockSpec((ts, LANES), rev),
                  pl.BlockSpec((ts, D_MODEL), rev),
                  pl.BlockSpec((ts, D_MODEL), rev),
                  _const_spec((1, D_MODEL))],
        out_specs=[pl.BlockSpec((ts, D_MODEL), rev),
                   pl.BlockSpec((ts, D_MODEL), rev),
                   pl.BlockSpec((ts, LANES), rev)]
        + [pl.BlockSpec((ts, width), rev)] * 4
        + [pl.BlockSpec((1, D_MODEL), lambda i: (0, 0)),
           pl.BlockSpec((1, LANES), lambda i: (0, 0))],
        out_shape=[jax.ShapeDtypeStruct((s, D_MODEL), F32),
                   jax.ShapeDtypeStruct((s, D_MODEL), BF16),
                   jax.ShapeDtypeStruct((s, LANES), BF16)]
        + [jax.ShapeDtypeStruct((s, width), BF16)] * 4
        + [jax.ShapeDtypeStruct((1, D_MODEL), F32),
                   jax.ShapeDtypeStruct((1, LANES), F32)],
        scratch_shapes=[pltpu.VMEM((SUBLANES, LANES), F32)],
        compiler_params=_params(),
    )(dq, dk, dv, dg, wt, wft, dcum, f, x1, dx2, g1)


def _lru_core_bwd(dx1b, w_out, xb, gate, hs, cw, cb, wa, ba, wx, bx, a_param, wa_t, wx_t,
                  chip_sums, ts):
    s = xb.shape[0]
    nt = s // ts
    tpb = ts // SUBLANES
    n_ex = len(chip_sums)

    def body(*refs):
        (dx_ref, wo_ref, xb_ref, xbh_ref, gate_ref, hs_ref, hsh_ref, cw_ref, cb_ref, wa_ref,
         ba_ref, wx_ref, bx_ref, ap_ref, wat_ref, wxt_ref) = refs[:16]
        sum_refs = refs[16:16 + n_ex]
        du_ref, gwa_ref, gwx_ref, gvec_ref = refs[16 + n_ex:20 + n_ex]
        got_refs = refs[20 + n_ex:20 + 2 * n_ex]
        acar_ref, dhcar_ref, dxccar_ref = refs[20 + 2 * n_ex:23 + 2 * n_ex]
        start, finish = _chip_exchange_phases(sum_refs, got_refs, *refs[23 + 2 * n_ex:])
        step = pl.program_id(0)
        pl.when(step == 0)(start)

        @pl.when(step == 0)
        def _():
            acar_ref[...] = jnp.zeros_like(acar_ref)
            dhcar_ref[...] = jnp.zeros_like(dhcar_ref)
            dxccar_ref[...] = jnp.zeros_like(dxccar_ref)
            gwa_ref[...] = jnp.zeros_like(gwa_ref)
            gwx_ref[...] = jnp.zeros_like(gwx_ref)
            gvec_ref[...] = jnp.zeros_like(gvec_ref)

        first_tile = step == nt - 1
        halo_on = jnp.where(first_tile, 0.0, 1.0)
        prev8 = xbh_ref[...] * halo_on
        hprev_row = hsh_ref[SUBLANES - 1:SUBLANES, :] * halo_on

        xbv = xb_ref[...]
        taps = _conv_taps(xbv, prev8)
        cw_v = cw_ref[...]
        xc, xcb, r, i, sp, a, mult = _lru_pre(taps, cw_v, cb_ref[...], wa_ref, ba_ref[...],
                                              wx_ref, bx_ref[...], ap_ref[...])
        hs = hs_ref[...]
        gv = gate_ref[...]
        sg = _sigmoid(gv)
        dy = _dot_nt(dx_ref[...], wo_ref[...])
        dhs = dy * (gv * sg)
        dgate = dy * hs * (sg * (1.0 + gv * (1.0 - sg)))

        rows = lax.broadcasted_iota(jnp.int32, a.shape, 0)
        a_next = jnp.where(rows < ts - 1, pltpu.roll(a, ts - 1, 0), acar_ref[0:1, :])
        cum_a, dh_loc = _scan_rows(a_next, dhs, reverse=True)
        dh = cum_a * dhcar_ref[0:1, :] + dh_loc
        h_prev = jnp.where(rows >= 1, pltpu.roll(hs, 1, 0), hprev_row)

        da = dh * h_prev
        ixc = i * xc
        dmult = dh * ixc
        di = dh * mult * xc
        dxc = dh * mult * i
        dlog_a = da * a - dmult * (a * a) / mult
        dr = dlog_a * ((-LRU_C) * sp)
        dsp = jnp.sum(dlog_a * ((-LRU_C) * r), axis=0, keepdims=True)
        dra = dr * r * (1.0 - r)
        dia = di * i * (1.0 - i)
        drab = dra.astype(BF16)
        diab = dia.astype(BF16)
        back = []
        for n in range(LRU_BLOCKS):
            sl = slice(n * LRU_BLOCK_W, (n + 1) * LRU_BLOCK_W)
            gwa_ref[n] += _dot_tn(xcb[:, sl], drab[:, sl])
            gwx_ref[n] += _dot_tn(xcb[:, sl], diab[:, sl])
            back.append(_dot(drab[:, sl], wat_ref[n]) + _dot(diab[:, sl], wxt_ref[n]))
        dxc = dxc + jnp.concatenate(back, axis=1)

        nxt8 = dxccar_ref[...]
        rows8 = lax.broadcasted_iota(jnp.int32, nxt8.shape, 0)
        dxb = cw_v[3:4] * dxc
        for j in range(1, CONV_WIDTH):
            rj = pltpu.roll(dxc, ts - j, 0)
            pj = pltpu.roll(nxt8, SUBLANES - j, 0)
            tail = jnp.where(rows8 >= SUBLANES - j, pj, rj[ts - SUBLANES:])
            dxb = dxb + cw_v[3 - j:4 - j] * jnp.concatenate([rj[:ts - SUBLANES], tail], axis=0)

        du_ref[:, :LRU_WIDTH] = dxb.astype(BF16)
        du_ref[:, LRU_WIDTH:] = dgate.astype(BF16)

        z = -ap_ref[...]
        gvec = [jnp.sum(dxc * taps[3 - k], axis=0, keepdims=True) for k in range(CONV_WIDTH)]
        gvec.append(jnp.sum(dxc, axis=0, keepdims=True))
        gvec.append(jnp.sum(dra, axis=0, keepdims=True))
        gvec.append(jnp.sum(dia, axis=0, keepdims=True))
        gvec.append(-dsp * _sigmoid(z))
        gvec_ref[...] += jnp.concatenate(gvec, axis=0)

        acar_ref[...] = a[0:SUBLANES, :]
        dhcar_ref[...] = dh[0:SUBLANES, :]
        dxccar_ref[...] = dxc[0:SUBLANES, :]
        pl.when(step == nt - 1)(finish)

    rev = lambda i: (nt - 1 - i, 0)
    halo = lambda i: (jnp.maximum((nt - 1 - i) * tpb - 1, 0), 0)
    tile = pl.BlockSpec((ts, LRU_WIDTH), rev)
    halo_spec = pl.BlockSpec((SUBLANES, LRU_WIDTH), halo)
    vec = _const_spec((1, LRU_WIDTH))
    blk = _const_spec((LRU_BLOCKS, LRU_BLOCK_W, LRU_BLOCK_W))
    acc_blk = pl.BlockSpec((LRU_BLOCKS, LRU_BLOCK_W, LRU_BLOCK_W), lambda i: (0, 0, 0))
    hbm = pl.BlockSpec(memory_space=pl.ANY)
    res = pl.pallas_call(
        body, name="lru_core_bwd", grid=(nt,),
        in_specs=[pl.BlockSpec((ts, D_MODEL), rev),
                  _const_spec((LRU_WIDTH, D_MODEL)),
                  tile, halo_spec, tile, tile, halo_spec,
                  _const_spec((CONV_WIDTH, LRU_WIDTH)), vec, blk, vec, blk, vec, vec, blk, blk]
        + [hbm] * n_ex,
        out_specs=[pl.BlockSpec((ts, 2 * LRU_WIDTH), rev), acc_blk, acc_blk,
                   pl.BlockSpec((SUBLANES, LRU_WIDTH), lambda i: (0, 0))] + [hbm] * n_ex,
        out_shape=[jax.ShapeDtypeStruct((s, 2 * LRU_WIDTH), BF16),
                   jax.ShapeDtypeStruct((LRU_BLOCKS, LRU_BLOCK_W, LRU_BLOCK_W), F32),
                   jax.ShapeDtypeStruct((LRU_BLOCKS, LRU_BLOCK_W, LRU_BLOCK_W), F32),
                   jax.ShapeDtypeStruct((SUBLANES, LRU_WIDTH), F32)]
        + [jax.ShapeDtypeStruct(a.shape, a.dtype) for a in chip_sums],
        scratch_shapes=[pltpu.VMEM((SUBLANES, LRU_WIDTH), F32),
                        pltpu.VMEM((SUBLANES, LRU_WIDTH), F32),
                        pltpu.VMEM((SUBLANES, LRU_WIDTH), F32)] + _chip_exchange_sems(n_ex),
        compiler_params=_params(),
    )(dx1b, w_out, xb, xb, gate, hs, hs, cw, cb, wa, ba, wx, bx, a_param, wa_t, wx_t, *chip_sums)
    return res[0], res[1], res[2], res[3], res[4:]


def _lru_in_bwd(du, w_in, x, dx1, g0, chip_sums, ts):
    s = x.shape[0]
    nt = s // ts
    n = len(chip_sums)

    def body(*refs):
        du_ref, w_ref, x_ref, dx1_ref, g_ref = refs[:5]
        sum_refs = refs[5:5 + n]
        gx_ref, gn_ref = refs[5 + n:7 + n]
        got_refs = refs[7 + n:7 + 2 * n]
        wfull_ref = refs[7 + 2 * n]
        start, finish = _chip_exchange_phases(sum_refs, got_refs, *refs[8 + 2 * n:])
        step = pl.program_id(0)
        pl.when(step == 0)(start)

        @pl.when(step == 0)
        def _():
            gn_ref[...] = jnp.zeros_like(gn_ref)
            for j in range(N_DEV):
                wfull_ref[:, j * LRU_IN_SHARD:(j + 1) * LRU_IN_SHARD] = w_ref[j]

        dh = _dot_nt(du_ref[...], wfull_ref[...])
        dxn, dgn = _norm_bwd(x_ref[...], g_ref[...], dh)
        gx_ref[...] = dx1_ref[...] + dxn
        gn_ref[...] += dgn
        pl.when(step == nt - 1)(finish)

    tile = pl.BlockSpec((ts, D_MODEL), lambda i: (i, 0))
    hbm = pl.BlockSpec(memory_space=pl.ANY)
    res = pl.pallas_call(
        body, name="lru_in_bwd", grid=(nt,),
        in_specs=[pl.BlockSpec((ts, 2 * LRU_WIDTH), lambda i: (i, 0)),
                  _const_spec((N_DEV, D_MODEL, LRU_IN_SHARD)), tile, tile,
                  _const_spec((1, D_MODEL))] + [hbm] * n,
        out_specs=[tile, pl.BlockSpec((1, D_MODEL), lambda i: (0, 0))] + [hbm] * n,
        out_shape=[jax.ShapeDtypeStruct((s, D_MODEL), F32),
                   jax.ShapeDtypeStruct((1, D_MODEL), F32)]
        + [jax.ShapeDtypeStruct(a.shape, a.dtype) for a in chip_sums],
        scratch_shapes=[pltpu.VMEM((D_MODEL, 2 * LRU_WIDTH), BF16)] + _chip_exchange_sems(n),
        compiler_params=_params(),
    )(du, w_in, x, dx1, g0, *chip_sums)
    return res[0], res[1], res[2:]


def _weight_grad(a, b, ts, name, scale=1.0, col_shards=1):
    s, ka = a.shape
    nb = b.shape[1]
    nt = s // ts
    per = nb // col_shards

    def body(a_ref, b_ref, o_ref):
        @pl.when(pl.program_id(0) == 0)
        def _():
            o_ref[...] = jnp.zeros_like(o_ref)

        if col_shards == 1:
            o_ref[...] += _dot_tn(a_ref[...], b_ref[...])
        else:
            acc = _dot_tn(a_ref[...], b_ref[...])
            for j in range(col_shards):
                o_ref[j] += acc[:, j * per:(j + 1) * per]
        if scale != 1.0:
            @pl.when(pl.program_id(0) == nt - 1)
            def _():
                o_ref[...] = o_ref[...] * scale

    out_dims = (ka, nb) if col_shards == 1 else (col_shards, ka, per)
    return pl.pallas_call(
        body, name=name, grid=(nt,),
        in_specs=[pl.BlockSpec((ts, ka), lambda i: (i, 0)),
                  pl.BlockSpec((ts, nb), lambda i: (i, 0))],
        out_specs=pl.BlockSpec(out_dims, lambda i: (0,) * len(out_dims)),
        out_shape=jax.ShapeDtypeStruct(out_dims, F32),
        compiler_params=_params(),
    )(a, b)


def _sum_parts(gp_ref):
    g = gp_ref[0].astype(F32)
    for k in range(1, gp_ref.shape[0]):
        g = g + gp_ref[k].astype(F32)
    return g


def _adamw(g_parts, w, m, v, tr, name):
    nparts, rows, cols = g_parts.shape

    def body(gp_ref, w_ref, m_ref, v_ref, g_ref, d_ref, mo_ref, vo_ref):
        g = _sum_parts(gp_ref)
        m2 = ADAM_B1 * m_ref[...] + (1.0 - ADAM_B1) * g
        v2 = ADAM_B2 * v_ref[...] + (1.0 - ADAM_B2) * (g * g)
        m_hat = m2 / (1.0 - ADAM_B1 ** ADAM_STEP)
        v_hat = v2 / (1.0 - ADAM_B2 ** ADAM_STEP)
        g_ref[...] = g
        d_ref[...] = (-ADAM_LR) * (m_hat / (jnp.sqrt(v_hat) + ADAM_EPS) + ADAM_WD * w_ref[...])
        mo_ref[...] = m2
        vo_ref[...] = v2

    tile = pl.BlockSpec((tr, cols), lambda i: (i, 0))
    out = jax.ShapeDtypeStruct((rows, cols), F32)
    return pl.pallas_call(
        body, name=name, grid=(rows // tr,),
        in_specs=[pl.BlockSpec((nparts, tr, cols), lambda i: (0, i, 0)), tile, tile, tile],
        out_specs=[tile, tile, tile, tile],
        out_shape=[out, out, out, out],
        compiler_params=_params(),
    )(g_parts, w, m, v)


def _reduce_parts(g_parts, name):
    _, rows, cols = g_parts.shape

    def body(gp_ref, g_ref):
        g_ref[...] = _sum_parts(gp_ref)

    return pl.pallas_call(
        body, name=name,
        out_shape=jax.ShapeDtypeStruct((rows, cols), F32),
        compiler_params=pltpu.CompilerParams(vmem_limit_bytes=VMEM_LIMIT_BYTES),
    )(g_parts)


def _mesh_pos():
    ix, iy, ic = lax.axis_index("x"), lax.axis_index("y"), lax.axis_index("c")
    return ix, iy, ic


def _peer(ix, iy, ic, mask):
    px = 1 - ix if mask & 4 else ix
    py = 1 - iy if mask & 2 else iy
    pc = 1 - ic if mask & 1 else ic
    return (px, py, pc), 4 * px + 2 * py + pc


def _exchange(arrays, scatter, name):
    n = len(arrays)

    def body(*refs):
        x_refs, o_refs = refs[:n], refs[n:2 * n]
        send_sems, recv_sems, local_sems = refs[2 * n:]
        ix, iy, ic = _mesh_pos()
        me = 4 * ix + 2 * iy + ic

        def src(a, dest):
            return x_refs[a].at[dest] if scatter else x_refs[a]

        local = [pltpu.make_async_copy(src(a, me), o_refs[a].at[me], local_sems.at[a])
                 for a in range(n)]
        for cp in local:
            cp.start()
        sends = []
        for mask in range(1, N_DEV):
            peer, pidx = _peer(ix, iy, ic, mask)
            for a in range(n):
                cp = pltpu.make_async_remote_copy(
                    src_ref=src(a, pidx), dst_ref=o_refs[a].at[me],
                    send_sem=send_sems.at[a, mask - 1], recv_sem=recv_sems.at[a, mask - 1],
                    device_id=peer, device_id_type=pl.DeviceIdType.MESH)
                cp.start()
                sends.append(cp)
        for mask in range(1, N_DEV):
            peer, pidx = _peer(ix, iy, ic, mask)
            for a in range(n):
                pltpu.make_async_remote_copy(
                    src_ref=src(a, me), dst_ref=o_refs[a].at[pidx],
                    send_sem=send_sems.at[a, mask - 1], recv_sem=recv_sems.at[a, mask - 1],
                    device_id=peer, device_id_type=pl.DeviceIdType.MESH).wait_recv()
        for cp in sends:
            cp.wait_send()
        for cp in local:
            cp.wait()

    out_shape = [jax.ShapeDtypeStruct(x.shape if scatter else (N_DEV,) + x.shape, x.dtype)
                 for x in arrays]
    return pl.pallas_call(
        body, name=name,
        in_specs=[pl.BlockSpec(memory_space=pl.ANY)] * n,
        out_specs=[pl.BlockSpec(memory_space=pl.ANY)] * n,
        out_shape=out_shape,
        scratch_shapes=[pltpu.SemaphoreType.DMA((n, N_DEV - 1)),
                        pltpu.SemaphoreType.DMA((n, N_DEV - 1)),
                        pltpu.SemaphoreType.DMA((n,))],
    )(*arrays)


def _gather_two_level(arrays, name):
    n = len(arrays)

    def body(*refs):
        start, forward, finish = _gather_phases(refs[:n], refs[n:2 * n], *refs[2 * n:])
        start()
        forward()
        finish()

    return pl.pallas_call(
        body, name=name,
        in_specs=[pl.BlockSpec(memory_space=pl.ANY)] * n,
        out_specs=[pl.BlockSpec(memory_space=pl.ANY)] * n,
        out_shape=[jax.ShapeDtypeStruct((N_DEV,) + x.shape, x.dtype) for x in arrays],
        scratch_shapes=_gather_sems(n),
    )(*arrays)


def _gather_sems(n):
    return [pltpu.SemaphoreType.DMA((n, N_DEV - 1)), pltpu.SemaphoreType.DMA((n, N_DEV - 1)),
            pltpu.SemaphoreType.DMA((n,))]


def _gather_phases(x_refs, o_refs, send_sems, recv_sems, local_sems):
    n = len(x_refs)
    ix, iy, ic = _mesh_pos()
    me, sibling = (ix, iy, ic), (ix, iy, 1 - ic)
    chips = [(1 - ix, iy), (ix, 1 - iy), (1 - ix, 1 - iy)]

    def idx(px, py, pc):
        return 4 * px + 2 * py + pc

    def copy(a, k, block, to, src=None):
        dst = o_refs[a].at[idx(*block)]
        return pltpu.make_async_remote_copy(
            src_ref=dst if src is None else src, dst_ref=dst,
            send_sem=send_sems.at[a, k], recv_sem=recv_sems.at[a, k],
            device_id=to, device_id_type=pl.DeviceIdType.MESH)

    def local():
        return [pltpu.make_async_copy(x_refs[a], o_refs[a].at[idx(*me)], local_sems.at[a])
                for a in range(n)]

    def first():
        out = []
        for a in range(n):
            out.append(copy(a, 0, me, sibling, src=x_refs[a]))
            out += [copy(a, 1 + j, me, (*chip, ic), src=x_refs[a])
                    for j, chip in enumerate(chips)]
        return out

    def passed():
        return [copy(a, 4 + j, (*chip, ic), sibling)
                for j, chip in enumerate(chips) for a in range(n)]

    def start():
        for cp in local() + first():
            cp.start()

    def forward():
        for j, chip in enumerate(chips):
            for a in range(n):
                copy(a, 1 + j, (*chip, ic), me).wait_recv()
                copy(a, 4 + j, (*chip, ic), sibling).start()

    def finish():
        for a in range(n):
            copy(a, 0, sibling, me).wait_recv()
            for j, chip in enumerate(chips):
                copy(a, 4 + j, (*chip, 1 - ic), me).wait_recv()
        for cp in first() + passed():
            cp.wait_send()
        for cp in local():
            cp.wait()

    return start, forward, finish


def _swap_sibling(arrays, name):
    n = len(arrays)
    n_chips = N_DEV // 2

    def body(*refs):
        x_refs, got_refs = refs[:n], refs[n:2 * n]
        send_sems, recv_sems = refs[2 * n:]
        ix, iy, ic = _mesh_pos()
        sibling = (ix, iy, 1 - ic)
        sends = []
        for a in range(n):
            for q in range(n_chips):
                cp = pltpu.make_async_remote_copy(
                    src_ref=x_refs[a].at[q, 1 - ic], dst_ref=got_refs[a].at[q],
                    send_sem=send_sems.at[a, q], recv_sem=recv_sems.at[a, q],
                    device_id=sibling, device_id_type=pl.DeviceIdType.MESH)
                cp.start()
                sends.append(cp)
        for cp in sends:
            cp.wait()

    return pl.pallas_call(
        body, name=name,
        in_specs=[pl.BlockSpec(memory_space=pl.ANY)] * n,
        out_specs=[pl.BlockSpec(memory_space=pl.ANY)] * n,
        out_shape=[jax.ShapeDtypeStruct((n_chips,) + x.shape[2:], x.dtype) for x in arrays],
        scratch_shapes=[pltpu.SemaphoreType.DMA((n, n_chips)),
                        pltpu.SemaphoreType.DMA((n, n_chips))],
    )(*arrays)


def _exchange_chips(arrays, name):
    n = len(arrays)

    def body(*refs):
        start, finish = _chip_exchange_phases(refs[:n], refs[n:2 * n], *refs[2 * n:])
        start()
        finish()

    return pl.pallas_call(
        body, name=name,
        in_specs=[pl.BlockSpec(memory_space=pl.ANY)] * n,
        out_specs=[pl.BlockSpec(memory_space=pl.ANY)] * n,
        out_shape=[jax.ShapeDtypeStruct(x.shape, x.dtype) for x in arrays],
        scratch_shapes=_chip_exchange_sems(n),
    )(*arrays)


def _chip_exchange_sems(n):
    n_chips = N_DEV // 2
    return [pltpu.SemaphoreType.DMA((n, n_chips - 1)), pltpu.SemaphoreType.DMA((n, n_chips - 1)),
            pltpu.SemaphoreType.DMA((n,))]


def _chip_exchange_phases(x_refs, o_refs, send_sems, recv_sems, local_sems):
    n = len(x_refs)
    n_chips = N_DEV // 2
    ix, iy, ic = _mesh_pos()
    my_chip = 2 * ix + iy

    def peers():
        for mask in range(1, n_chips):
            px = 1 - ix if mask & 2 else ix
            py = 1 - iy if mask & 1 else iy
            yield mask, (px, py, ic), 2 * px + py

    def local():
        return [pltpu.make_async_copy(x_refs[a].at[my_chip], o_refs[a].at[my_chip],
                                      local_sems.at[a]) for a in range(n)]

    def sends():
        return [pltpu.make_async_remote_copy(
            src_ref=x_refs[a].at[chip], dst_ref=o_refs[a].at[my_chip],
            send_sem=send_sems.at[a, mask - 1], recv_sem=recv_sems.at[a, mask - 1],
            device_id=peer, device_id_type=pl.DeviceIdType.MESH)
            for mask, peer, chip in peers() for a in range(n)]

    def start():
        for cp in local() + sends():
            cp.start()

    def finish():
        for mask, peer, chip in peers():
            for a in range(n):
                pltpu.make_async_remote_copy(
                    src_ref=x_refs[a].at[my_chip], dst_ref=o_refs[a].at[chip],
                    send_sem=send_sems.at[a, mask - 1], recv_sem=recv_sems.at[a, mask - 1],
                    device_id=peer, device_id_type=pl.DeviceIdType.MESH).wait_recv()
        for cp in sends():
            cp.wait_send()
        for cp in local():
            cp.wait()

    return start, finish


def _pair_sum(core, x, got, name):
    nq, rows, cols = got.shape

    def body(c_ref, x_ref, g_ref, o_ref):
        o_ref[...] = (x_ref[...] + g_ref[...]).astype(BF16)

    blk = pl.BlockSpec((None, rows, cols), lambda q, c: (q, 0, 0))
    return pl.pallas_call(
        body, name=name,
        grid_spec=pltpu.PrefetchScalarGridSpec(
            num_scalar_prefetch=1, grid=(nq,),
            in_specs=[pl.BlockSpec((None, None, rows, cols), lambda q, c: (q, c[0], 0, 0)), blk],
            out_specs=blk),
        out_shape=jax.ShapeDtypeStruct(got.shape, BF16),
        compiler_params=_params(),
    )(core, x, got)


def _selectors():
    r = lax.broadcasted_iota(jnp.int32, (LANES, FOX_PAD), 0)
    c = lax.broadcasted_iota(jnp.int32, (LANES, FOX_PAD), 1)
    part, head_r = r // HEADS, r % HEADS
    head_c, lane_c = c // HEAD_PAD, c % HEAD_PAD
    same = (head_r == head_c) & (part < 3)
    sel_q = jnp.where(same & (lane_c == LANE_RB + part), 1.0, 0.0)
    sel_k = jnp.where(same & (lane_c == LANE_CK + part), -1.0, 0.0)
    sel = jnp.stack([sel_q, sel_k, jnp.zeros_like(sel_q)]).astype(BF16)
    lane = lax.broadcasted_iota(jnp.int32, (1, FOX_PAD), 1) % HEAD_PAD
    ones_q = jnp.where((lane >= LANE_CK) & (lane < LANE_CK + 3), 1.0, 0.0)
    ones_k = jnp.where(((lane >= LANE_RB) & (lane < LANE_RB + 3))
                       | ((lane >= LANE_LSE) & (lane < LANE_LSE + 3)), 1.0, 0.0)
    ones_v = jnp.where((lane >= LANE_ONE_V) & (lane < LANE_ONE_V + 2), 1.0, 0.0)
    bias = jnp.stack([ones_q, ones_k, ones_v]).astype(F32)
    return sel, bias


def _chip_sums(names, send):
    send = [a.reshape((N_DEV // 2, 2) + a.shape[1:]) for a in send]
    got = _swap_sibling(send, "swap_" + names[0])
    core = lax.axis_index("c").astype(jnp.int32).reshape(1)
    return [_pair_sum(core, a, b, "pair_sum_" + n) for n, a, b in zip(names, send, got)]


def _local_step(x, target, norm_g, final_g, w_in8, conv_w, conv_b, wa, ba, wx, bx, a_param,
                w_out_b, fox_in_shard, b_f, fox_out_shard, blk=512, ts=256):
    g0, g1 = norm_g[0:1], norm_g[1:2]
    gf = final_g.reshape(1, D_MODEL)
    wa_b, wx_b = wa.astype(BF16), wx.astype(BF16)
    sel, bias = _selectors()

    tm = min(2 * ts, x.shape[0])
    xb, gate1, h0, (fox_in8, fox_out8) = _lru_in_fwd(x, g0, w_in8, [fox_in_shard, fox_out_shard],
                                                     tm)
    fox_w_in = jnp.transpose(fox_in8, (1, 0, 2)).reshape(D_MODEL, FOX_IN_COLS)
    width = HEADS * HEAD_DIM
    wf_b = jnp.pad(fox_w_in[:, 4 * width:], ((0, 0), (0, LANES - HEADS)))
    bf_pad = jnp.pad(b_f, ((0, 0), (0, LANES - HEADS)))
    fo_b = fox_out8.reshape(width, D_MODEL)
    y1, hs = _lru_core_fwd(xb, gate1, conv_w, conv_b, wa_b, ba, wx_b, bx, a_param, ts)
    x1, h1, f, cparts = _fox_pre_fwd(x, y1, w_out_b, g1, wf_b, bf_pad, ts)
    qkv = _fox_proj_fwd(h1, cparts, fox_w_in, 0, 3, sel, bias, BF16, tm, "fox_proj_qkv")
    gate2 = _fox_proj_fwd(h1, None, fox_w_in, 3, 1, None, None, F32, tm, "fox_proj_gate")[0]
    o, qb = _attn_fwd(qkv, blk, hps=4)
    dx2, dx2b, y2, loss_acc, g_final = _fox_out_loss(o, gate2, fo_b, x1, target, gf, tm)

    do, dgate2 = _fox_out_bwd(dx2b, fo_b, o, gate2, ts)
    dq, dk, dv, dcum = _attn_bwd(qb, qkv, do, blk)
    dx1, dx1b, df, du_q, du_k, du_v, du_g, g_norm1, g_bf = _fox_in_bwd(
        dq, dk, dv, dgate2, fox_w_in, wf_b, dcum, f, x1, dx2, g1, ts)
    tw = min(2048, x.shape[0])
    g_q = _weight_grad(h1, du_q, tw, "grad_fox_wq", scale=QK_SCALE)
    g_k = _weight_grad(h1, du_k, tw, "grad_fox_wk")
    g_v = _weight_grad(h1, du_v, tw, "grad_fox_wv")
    g_g = _weight_grad(h1, du_g, tw, "grad_fox_wg")
    g_f = _weight_grad(h1, df, tw, "grad_fox_wf")
    g_fox_w_in = jnp.concatenate([g_q, g_k, g_v, g_g, g_f[:, :HEADS]], axis=1)
    g_fox_w_in = jnp.transpose(g_fox_w_in.reshape(D_MODEL, N_DEV, FOX_IN_SHARD), (1, 0, 2))
    g_fox_w_out = _weight_grad(y2, dx2b, tw, "grad_fox_w_out")
    fox_sums = _chip_sums(("fox_w_in", "fox_w_out"),
                          [g_fox_w_in, g_fox_w_out.reshape(N_DEV, -1, D_MODEL)])

    du, g_wa, g_wx, g_vec, (r_fox_in, r_fox_out) = _lru_core_bwd(
        dx1b, w_out_b, xb, gate1, hs, conv_w, conv_b, wa_b, ba, wx_b, bx, a_param,
        jnp.transpose(wa_b, (0, 2, 1)), jnp.transpose(wx_b, (0, 2, 1)), fox_sums, ts)
    g_lru_w_in = _weight_grad(h0, du, tw, "grad_lru_w_in", col_shards=N_DEV)
    g_lru_w_out = _weight_grad(y1, dx1b, tw, "grad_lru_w_out")
    conv_send = jnp.transpose(g_vec[0:CONV_WIDTH].reshape(CONV_WIDTH, N_DEV, -1), (1, 0, 2))
    small = dict(
        norm_g=jnp.concatenate([jnp.zeros_like(g_norm1), g_norm1], axis=0), final_g=g_final[0],
        lru_conv_b=g_vec[4:5], lru_wa=g_wa, lru_ba=g_vec[5:6], lru_wx=g_wx, lru_bx=g_vec[6:7],
        lru_a_param=g_vec[7:8], fox_b_f=g_bf[:, :HEADS])
    lru_sums = _chip_sums(("lru_w_in", "lru_conv_w", "lru_w_out", "small"),
                          [g_lru_w_in, conv_send, g_lru_w_out.reshape(N_DEV, -1, D_MODEL),
                           _pack_small(small).reshape(N_DEV, SMALL_CHUNK_ROWS, LANES)])
    grad_x, g_norm0, (r_w_in, r_conv, r_w_out, r_small) = _lru_in_bwd(du, w_in8, x, dx1, g0,
                                                                      lru_sums, tm)
    received = dict(lru_w_in=r_w_in, lru_conv_w=r_conv, lru_w_out=r_w_out, fox_w_in=r_fox_in,
                    fox_w_out=r_fox_out, small=r_small)
    return loss_acc[0, 0], grad_x, g_norm0, received


SMALL =("norm_g", "final_g", "lru_conv_b", "lru_wa", "lru_ba", "lru_wx", "lru_bx", "lru_a_param",
         "fox_b_f")
ALL_WEIGHTS = ("norm_g", "final_g", "lru_w_in", "lru_conv_w", "lru_conv_b", "lru_wa", "lru_ba",
               "lru_wx", "lru_bx", "lru_a_param", "lru_w_out", "fox_w_in", "fox_b_f", "fox_w_out")


def _pack_small(d):
    rows = []
    for n in SMALL:
        a = d[n].reshape(-1)
        if a.shape[0] % LANES:
            a = jnp.pad(a, (0, LANES - a.shape[0] % LANES))
        rows.append(a.reshape(-1, LANES))
    packed = jnp.concatenate(rows, axis=0)
    return jnp.pad(packed, ((0, N_DEV * SMALL_CHUNK_ROWS - packed.shape[0]), (0, 0)))


def _unpack_small(packed, like):
    out, off = {}, 0
    for n, nrows in zip(SMALL, SMALL_ROWS):
        size = like[n].size
        out[n] = packed[off:off + nrows].reshape(-1)[:size].reshape(like[n].shape)
        off += nrows
    return out


def kernel(x, norm_g, final_g, lru_w_in, lru_conv_w, lru_conv_b, lru_wa, lru_ba, lru_wx, lru_bx, lru_a_param, lru_w_out, fox_w_in, fox_b_f, fox_w_out, loss_target, m_norm_g, m_final_g, m_lru_w_in, m_lru_conv_w, m_lru_conv_b, m_lru_wa, m_lru_ba, m_lru_wx, m_lru_bx, m_lru_a_param, m_lru_w_out, m_fox_w_in, m_fox_b_f, m_fox_w_out, v_norm_g, v_final_g, v_lru_w_in, v_lru_conv_w, v_lru_conv_b, v_lru_wa, v_lru_ba, v_lru_wx, v_lru_bx, v_lru_a_param, v_lru_w_out, v_fox_w_in, v_fox_b_f, v_fox_w_out):
    w_loc = dict(norm_g=norm_g, final_g=final_g, lru_w_in=lru_w_in, lru_conv_w=lru_conv_w,
                 lru_conv_b=lru_conv_b, lru_wa=lru_wa, lru_ba=lru_ba, lru_wx=lru_wx, lru_bx=lru_bx,
                 lru_a_param=lru_a_param, lru_w_out=lru_w_out, fox_w_in=fox_w_in, fox_b_f=fox_b_f,
                 fox_w_out=fox_w_out)
    m_loc = dict(norm_g=m_norm_g, final_g=m_final_g, lru_w_in=m_lru_w_in, lru_conv_w=m_lru_conv_w,
                 lru_conv_b=m_lru_conv_b, lru_wa=m_lru_wa, lru_ba=m_lru_ba, lru_wx=m_lru_wx,
                 lru_bx=m_lru_bx, lru_a_param=m_lru_a_param, lru_w_out=m_lru_w_out,
                 fox_w_in=m_fox_w_in, fox_b_f=m_fox_b_f, fox_w_out=m_fox_w_out)
    v_loc = dict(norm_g=v_norm_g, final_g=v_final_g, lru_w_in=v_lru_w_in, lru_conv_w=v_lru_conv_w,
                 lru_conv_b=v_lru_conv_b, lru_wa=v_lru_wa, lru_ba=v_lru_ba, lru_wx=v_lru_wx,
                 lru_bx=v_lru_bx, lru_a_param=v_lru_a_param, lru_w_out=v_lru_w_out,
                 fox_w_in=v_fox_w_in, fox_b_f=v_fox_b_f, fox_w_out=v_fox_w_out)

    w_in8, conv8, w_out8 = _gather_two_level(
        [lru_w_in[0].astype(BF16), lru_conv_w[0], lru_w_out[0].astype(BF16)], "gather_weights")
    conv_full = jnp.transpose(conv8, (1, 0, 2)).reshape(CONV_WIDTH, LRU_WIDTH)

    loss, grad_x, g_norm0, received = _local_step(
        x[0], loss_target[0], norm_g, final_g, w_in8, conv_full, lru_conv_b, lru_wa[0], lru_ba,
        lru_wx[0], lru_bx, lru_a_param, w_out8.reshape(LRU_WIDTH, D_MODEL),
        fox_w_in[0].astype(BF16), fox_b_f, fox_w_out[0].astype(BF16))

    out = {}
    for n, tr in (("lru_w_in", 256), ("lru_conv_w", CONV_WIDTH), ("lru_w_out", 96),
                  ("fox_w_in", 128), ("fox_w_out", 64)):
        res = _adamw(received[n], w_loc[n][0], m_loc[n][0], v_loc[n][0], tr, "adamw_" + n)
        out[n] = [a[None] for a in res]

    g_chunk = _reduce_parts(received["small"], "reduce_small_grads")
    g_small, g_norm0_all = _exchange([g_chunk, g_norm0.reshape(SUBLANES, LANES)], False,
                                     "gather_small_grads")
    g_norm0_sum = _reduce_parts(g_norm0_all, "reduce_norm0_grads")
    g_small = g_small.reshape(N_DEV * SMALL_CHUNK_ROWS, LANES).at[0:SUBLANES].set(g_norm0_sum)
    g_small = g_small[None]
    res = _adamw(g_small, _pack_small(w_loc), _pack_small(m_loc), _pack_small(v_loc),
                 N_DEV * SMALL_CHUNK_ROWS, "adamw_replicated")
    small_out = [_unpack_small(a, w_loc) for a in res]
    for n in SMALL:
        out[n] = [d[n] for d in small_out]

    loss = lax.psum(loss, ("x", "y", "c"))
    return (loss, grad_x[None], *[out[n][0] for n in ALL_WEIGHTS], *[out[n][1] for n in ALL_WEIGHTS],
            *[out[n][2] for n in ALL_WEIGHTS], *[out[n][3] for n in ALL_WEIGHTS])
```
